```python
import math
import jax, jax.numpy as jnp
from jax import lax
import numpy as np

D_MODEL = 1024
BATCH = 2
SEQ = 8192
DEPTH = 2

GRID_W = 64
CTX_LEN = 256
SSM_WIDTH = 3 * D_MODEL // 8
SSM_GROUP = 16
SSM_GROUPS = SSM_WIDTH // SSM_GROUP
SSM_STATE = 64
FNET_WIDTH = D_MODEL // 4
FNET_GROUPS = 4
FNET_GROUP = FNET_WIDTH // FNET_GROUPS
HEAD_DIM = 64
NA_WIDTH = 3 * D_MODEL // 8
NA_HEADS = NA_WIDTH // HEAD_DIM
IN_WIDTH = SSM_WIDTH + FNET_WIDTH + 3 * NA_WIDTH
NA_ROWS_MAX = 8
NA_COLS = 16
RPB_ROWS = 2 * NA_ROWS_MAX - 1
RPB_COLS = 2 * NA_COLS - 1
ROPE_BASE = 10000.0
D_FF = -(-8 * D_MODEL // (3 * 256)) * 256
EPS = 1e-6

kernel_name = "hybrid_s5_fnet_natten_dit_block"


def rmsnorm(x, g):
    x32 = x.astype(jnp.float32)
    y = x32 * lax.rsqrt(jnp.mean(x32 * x32, axis=-1, keepdims=True) + EPS)
    return (y * g.astype(jnp.float32)).astype(x.dtype)


def modulate(x, shift, scale):
    return x * (1 + scale) + shift


def axial_rope(x, row, col):
    half = HEAD_DIM // 2
    quarter = half // 2
    freqs = ROPE_BASE ** (-jnp.arange(quarter, dtype=jnp.float32) / quarter)

    def rot(xs, pos):
        ang = pos.astype(jnp.float32)[:, None] * freqs
        cos = jnp.cos(ang)[None, :, None, :]
        sin = jnp.sin(ang)[None, :, None, :]
        x1, x2 = xs[..., :quarter], xs[..., quarter:]
        return jnp.concatenate([x1 * cos - x2 * sin, x2 * cos + x1 * sin], axis=-1)

    x32 = x.astype(jnp.float32)
    return jnp.concatenate([rot(x32[..., :half], row), rot(x32[..., half:], col)], axis=-1).astype(x.dtype)


def ssm_discretise(a_re, a_im, log_dt, b_re, b_im):
    lam = lax.complex(a_re.astype(jnp.float32), a_im.astype(jnp.float32))
    dt = jnp.exp(log_dt.astype(jnp.float32))[:, None]
    lam_bar = jnp.exp(lam * dt)
    bmat = lax.complex(b_re.astype(jnp.float32), b_im.astype(jnp.float32))
    b_bar = ((lam_bar - 1) / lam)[..., None] * bmat
    return lam_bar, b_bar


def ssm_scan(lam_bar, bu, h0, reverse):
    if h0 is not None:
        edge = -1 if reverse else 0
        bu = bu.at[:, edge].add(lam_bar * h0)
    a = jnp.broadcast_to(lam_bar, bu.shape)

    def combine(e1, e2):
        a1, b1 = e1
        a2, b2 = e2
        return a1 * a2, a2 * b1 + b2

    _, h = lax.associative_scan(combine, (a, bu), axis=1, reverse=reverse)
    return h


def s5_mixer(u, uc, a_re, a_im, log_dt, b_re, b_im, c_re, c_im, d_skip, w_glu, need_ctx_out):
    B, L, _ = u.shape
    Lc = uc.shape[1]
    u32 = u.astype(jnp.float32)
    uc32 = uc.astype(jnp.float32)
    ug = u32.reshape(B, L, SSM_GROUPS, SSM_GROUP).astype(jnp.complex64)
    ucg = uc32.reshape(B, Lc, SSM_GROUPS, SSM_GROUP).astype(jnp.complex64)
    dsk = d_skip.astype(jnp.float32)
    y = u32 * dsk
    yc = uc32 * dsk if need_ctx_out else None
    for direction in range(2):
        rev = direction == 1
        lam_bar, b_bar = ssm_discretise(a_re[direction], a_im[direction], log_dt[direction],
                                        b_re[direction], b_im[direction])
        cmat = lax.complex(c_re[direction].astype(jnp.float32), c_im[direction].astype(jnp.float32))
        hc = ssm_scan(lam_bar, jnp.einsum('gph,blgh->blgp', b_bar, ucg), None, rev)
        h0 = hc[:, 0] if rev else hc[:, -1]
        h = ssm_scan(lam_bar, jnp.einsum('gph,blgh->blgp', b_bar, ug), h0, rev)
        y = y + jnp.einsum('ghp,blgp->blgh', cmat, h).real.reshape(B, L, SSM_WIDTH)
        if need_ctx_out:
            yc = yc + jnp.einsum('ghp,blgp->blgh', cmat, hc).real.reshape(B, Lc, SSM_WIDTH)

    def glu(z):
        z = jax.nn.gelu(z).astype(u.dtype)
        return z * jax.nn.sigmoid(z @ w_glu)

    return glu(y), (glu(yc) if need_ctx_out else None)


def fnet_mixer(f, w_fourier):
    B, L, _ = f.shape
    fg = f.astype(jnp.float32).reshape(B, L, FNET_GROUPS, FNET_GROUP)
    mixed = jnp.fft.fft2(fg, axes=(1, 3), norm="ortho").real.reshape(B, L, FNET_WIDTH)
    return mixed.astype(f.dtype) @ w_fourier


def na_latent(q, k, v, kc, vc, rpb):
    B, L, H, d = q.shape
    R = L // GRID_W
    wr = min(NA_ROWS_MAX, R)
    nwin = wr * NA_COLS
    qg = q.reshape(B, R, GRID_W, H, d)
    kg = k.reshape(B, R, GRID_W, H, d)
    vg = v.reshape(B, R, GRID_W, H, d)
    cols = jnp.arange(GRID_W)
    col_start = jnp.clip(cols - NA_COLS // 2, 0, GRID_W - NA_COLS)
    col_idx = col_start[:, None] + jnp.arange(NA_COLS)
    dc = col_idx - cols[:, None] + (NA_COLS - 1)
    scale = d ** -0.5

    def row_block(args):
        r, q_row = args
        rs = jnp.clip(r - wr // 2, 0, R - wr)
        k_rows = lax.dynamic_slice_in_dim(kg, rs, wr, axis=1)
        v_rows = lax.dynamic_slice_in_dim(vg, rs, wr, axis=1)
        k_win = k_rows[:, :, col_idx]
        v_win = v_rows[:, :, col_idx]
        dr = rs + jnp.arange(wr) - r + (NA_ROWS_MAX - 1)
        bias = rpb[:, dr[:, None, None], dc[None]]
        bias = jnp.transpose(bias, (0, 2, 1, 3)).astype(jnp.float32)
        s_win = jnp.einsum('bqhd,bwqkhd->bhqwk', q_row, k_win).astype(jnp.float32) * scale + bias[None]
        s_win = s_win.reshape(B, H, GRID_W, nwin)
        s_ctx = jnp.einsum('bqhd,bchd->bhqc', q_row, kc).astype(jnp.float32) * scale
        p = jax.nn.softmax(jnp.concatenate([s_win, s_ctx], axis=-1), axis=-1).astype(q.dtype)
        p_win = p[..., :nwin].reshape(B, H, GRID_W, wr, NA_COLS)
        p_ctx = p[..., nwin:]
        return (jnp.einsum('bhqwk,bwqkhd->bqhd', p_win, v_win)
                + jnp.einsum('bhqc,bchd->bqhd', p_ctx, vc))

    o = lax.map(row_block, (jnp.arange(R), jnp.moveaxis(qg, 1, 0)))
    return jnp.moveaxis(o, 0, 1).reshape(B, L, H * d)


def na_context(qc, kc, vc):
    B, Lc, H, d = qc.shape
    s = jnp.einsum('bqhd,bkhd->bhqk', qc, kc).astype(jnp.float32) * d ** -0.5
    p = jax.nn.softmax(s, axis=-1).astype(qc.dtype)
    return jnp.einsum('bhqk,bkhd->bqhd', p, vc).reshape(B, Lc, H * d)


def swiglu(h, w_gate, w_up, w_down):
    return (jax.nn.silu(h @ w_gate) * (h @ w_up)) @ w_down


def trunk_layer(x, xc, c, c_ctx, row, col, w_mod, b_mod, g_pre_mix, g_post_mix, w_in,
                ssm_a_re, ssm_a_im, ssm_log_dt, ssm_b_re, ssm_b_im, ssm_c_re, ssm_c_im, ssm_d, w_glu,
                w_fourier, na_rpb, w_out, g_pre_ffn, g_post_ffn, w_ffn_gate, w_ffn_up, w_ffn_down, last):
    B, L, _ = x.shape
    Lc = xc.shape[1]
    mod = (jax.nn.silu(c) @ w_mod + b_mod)[:, None, :]
    mod_c = jax.nn.silu(c_ctx) @ w_mod + b_mod
    sh1, sc1, g1, sh2, sc2, g2 = jnp.split(mod, 6, axis=-1)
    sh1c, sc1c, g1c, sh2c, sc2c, g2c = jnp.split(mod_c, 6, axis=-1)

    h = modulate(rmsnorm(x, g_pre_mix), sh1, sc1) @ w_in
    hc = modulate(rmsnorm(xc, g_pre_mix), sh1c, sc1c) @ w_in
    splits = list(np.cumsum([SSM_WIDTH, FNET_WIDTH, NA_WIDTH, NA_WIDTH]))
    u, f, q, k, v = jnp.split(h, splits, axis=-1)
    uc, fc, qc, kc, vc = jnp.split(hc, splits, axis=-1)

    y_ssm, yc_ssm = s5_mixer(u, uc, ssm_a_re, ssm_a_im, ssm_log_dt, ssm_b_re, ssm_b_im,
                             ssm_c_re, ssm_c_im, ssm_d, w_glu, not last)
    y_fft = fnet_mixer(f, w_fourier)
    q = axial_rope(q.reshape(B, L, NA_HEADS, HEAD_DIM), row, col)
    k = axial_rope(k.reshape(B, L, NA_HEADS, HEAD_DIM), row, col)
    v = v.reshape(B, L, NA_HEADS, HEAD_DIM)
    kc = kc.reshape(B, Lc, NA_HEADS, HEAD_DIM)
    vc = vc.reshape(B, Lc, NA_HEADS, HEAD_DIM)
    y_na = na_latent(q, k, v, kc, vc, na_rpb)
    o = jnp.concatenate([y_ssm, y_fft, y_na], axis=-1) @ w_out
    x = x + g1 * rmsnorm(o, g_post_mix)
    if not last:
        y_fft_c = fnet_mixer(fc, w_fourier)
        y_na_c = na_context(qc.reshape(B, Lc, NA_HEADS, HEAD_DIM), kc, vc)
        oc = jnp.concatenate([yc_ssm, y_fft_c, y_na_c], axis=-1) @ w_out
        xc = xc + g1c * rmsnorm(oc, g_post_mix)

    x = x + g2 * rmsnorm(swiglu(modulate(rmsnorm(x, g_pre_ffn), sh2, sc2), w_ffn_gate, w_ffn_up, w_ffn_down), g_post_ffn)
    if not last:
        xc = xc + g2c * rmsnorm(swiglu(modulate(rmsnorm(xc, g_pre_ffn), sh2c, sc2c), w_ffn_gate, w_ffn_up, w_ffn_down), g_post_ffn)
    else:
        xc = None
    return x, xc


def setup_inputs(seed: int = 0) -> dict:
    key = jax.random.key(seed)
    ks = jax.random.split(key, 26)
    f32 = jnp.float32

    def nrm(k, shape, scale):
        return jax.random.normal(k, shape, f32) * scale

    G, P, Hc = SSM_GROUPS, SSM_STATE, SSM_GROUP
    n_idx = jnp.arange(P, dtype=f32)
    return {
        "x": nrm(ks[0], (BATCH, SEQ, D_MODEL), 1.0),
        "c": nrm(ks[1], (BATCH, D_MODEL), 1.0),
        "ctx": nrm(ks[2], (BATCH, CTX_LEN, D_MODEL), 1.0),
        "c_ctx": nrm(ks[3], (D_MODEL,), 1.0),
        "w_mod": nrm(ks[4], (DEPTH, D_MODEL, 6 * D_MODEL), D_MODEL ** -0.5),
        "b_mod": nrm(ks[5], (DEPTH, 6 * D_MODEL), 0.01),
        "g_pre_mix": 1.0 + nrm(ks[6], (DEPTH, D_MODEL), 0.1),
        "g_post_mix": 1.0 + nrm(ks[7], (DEPTH, D_MODEL), 0.1),
        "w_in": nrm(ks[8], (DEPTH, D_MODEL, IN_WIDTH), D_MODEL ** -0.5),
        "ssm_a_re": -0.5 + nrm(ks[9], (DEPTH, 2, G, P), 0.01),
        "ssm_a_im": math.pi * n_idx + nrm(ks[10], (DEPTH, 2, G, P), 0.01),
        "ssm_log_dt": jax.random.uniform(ks[11], (DEPTH, 2, G), f32, math.log(1e-3), math.log(1e-1)),
        "ssm_b_re": nrm(ks[12], (DEPTH, 2, G, P, Hc), (2 * Hc) ** -0.5),
        "ssm_b_im": nrm(ks[13], (DEPTH, 2, G, P, Hc), (2 * Hc) ** -0.5),
        "ssm_c_re": nrm(ks[14], (DEPTH, 2, G, Hc, P), (2 * P) ** -0.5),
        "ssm_c_im": nrm(ks[15], (DEPTH, 2, G, Hc, P), (2 * P) ** -0.5),
        "ssm_d": nrm(ks[16], (DEPTH, SSM_WIDTH), 1.0),
        "w_glu": nrm(ks[17], (DEPTH, SSM_WIDTH, SSM_WIDTH), SSM_WIDTH ** -0.5),
        "w_fourier": nrm(ks[18], (DEPTH, FNET_WIDTH, FNET_WIDTH), FNET_WIDTH ** -0.5),
        "na_rpb": nrm(ks[19], (DEPTH, NA_HEADS, RPB_ROWS, RPB_COLS), 0.02),
        "w_out": nrm(ks[20], (DEPTH, D_MODEL, D_MODEL), D_MODEL ** -0.5),
        "g_pre_ffn": 1.0 + nrm(ks[21], (DEPTH, D_MODEL), 0.1),
        "g_post_ffn": 1.0 + nrm(ks[22], (DEPTH, D_MODEL), 0.1),
        "w_ffn_gate": nrm(ks[23], (DEPTH, D_MODEL, D_FF), D_MODEL ** -0.5),
        "w_ffn_up": nrm(ks[24], (DEPTH, D_MODEL, D_FF), D_MODEL ** -0.5),
        "w_ffn_down": nrm(ks[25], (DEPTH, D_FF, D_MODEL), D_FF ** -0.5),
    }


def reference(x, c, ctx, c_ctx, w_mod, b_mod, g_pre_mix, g_post_mix, w_in,
              ssm_a_re, ssm_a_im, ssm_log_dt, ssm_b_re, ssm_b_im, ssm_c_re, ssm_c_im, ssm_d, w_glu,
              w_fourier, na_rpb, w_out, g_pre_ffn, g_post_ffn, w_ffn_gate, w_ffn_up, w_ffn_down):
    L = x.shape[1]
    t = jnp.arange(L)
    row = t // GRID_W
    col = t % GRID_W
    xc = ctx
    for layer in range(DEPTH):
        x, xc = trunk_layer(
            x, xc, c, c_ctx, row, col, w_mod[layer], b_mod[layer], g_pre_mix[layer], g_post_mix[layer], w_in[layer],
            ssm_a_re[layer], ssm_a_im[layer], ssm_log_dt[layer], ssm_b_re[layer], ssm_b_im[layer],
            ssm_c_re[layer], ssm_c_im[layer], ssm_d[layer], w_glu[layer], w_fourier[layer], na_rpb[layer],
            w_out[layer], g_pre_ffn[layer], g_post_ffn[layer], w_ffn_gate[layer], w_ffn_up[layer], w_ffn_down[layer],
            last=(layer == DEPTH - 1))
    return x
```

```python
import functools
import math

import numpy as np
import jax
import jax.numpy as jnp
from jax import lax
from jax.experimental import pallas as pl
from jax.experimental.pallas import tpu as pltpu

BF = jnp.bfloat16
F32 = jnp.float32

EPS = 1e-6
GRID_W = 64
HEAD_DIM = 64
NA_ROWS = 8
NA_COLS = 16
ROPE_BASE = 10000.0
SSM_GROUP = 16
SSM_STATE = 64
FNET_GROUP = 64
LANES = 128
SSM_T = 8
SSM_SEG = 8
NEG = -1e30
VMEM_LIMIT = 56 * 1024 * 1024


def _cparams(*sem):
    return pltpu.CompilerParams(dimension_semantics=sem, vmem_limit_bytes=VMEM_LIMIT)


def _resident(shape, index_map):
    return pl.BlockSpec(shape, index_map, pipeline_mode=pl.Buffered(1))


def _mod_kernel(c_ref, w_ref, b_ref, o_ref):
    cs = c_ref[...]
    s = (cs * jax.nn.sigmoid(cs)).astype(BF)
    o_ref[...] = jnp.dot(s, w_ref[...].astype(BF), preferred_element_type=F32) + b_ref[...]


def _mod_rows(cstack, w_mod, b_mod):
    rows, d = cstack.shape
    n = w_mod.shape[1]
    return pl.pallas_call(
        _mod_kernel,
        grid=(n // d,),
        in_specs=[pl.BlockSpec((rows, d), lambda i: (0, 0)),
                  pl.BlockSpec((d, d), lambda i: (0, i)),
                  pl.BlockSpec((1, d), lambda i: (0, i))],
        out_specs=pl.BlockSpec((rows, d), lambda i: (0, i)),
        out_shape=jax.ShapeDtypeStruct((rows, n), F32),
        compiler_params=_cparams("arbitrary"),
        name="mod_rows",
    )(cstack, w_mod, b_mod.reshape(1, n))


def _rope_tile(t, cos, sin, first):
    partner = jnp.where(first, pltpu.roll(t, LANES - 16, 1), pltpu.roll(t, 16, 1))
    return t * cos + partner * sin


def _inproj_kernel(x_ref, sh_ref, sc_ref, g_ref, w_ref, cos_ref, sin_ref,
                   u_ref, f_ref, q_ref, k_ref, v_ref, *, widths, rope):
    x = x_ref[...]
    ms = jnp.mean(x * x, axis=-1, keepdims=True)
    xn = x * lax.rsqrt(ms + EPS) * g_ref[...]
    m = (xn * (1.0 + sc_ref[...]) + sh_ref[...]).astype(BF)
    w_ssm, w_fn, w_na = widths

    def proj(lo, n):
        return jnp.dot(m, w_ref[:, lo:lo + n], preferred_element_type=F32)

    u = proj(0, w_ssm)
    for j in range(w_ssm // LANES):
        u_ref[j] = u[:, j * LANES:(j + 1) * LANES].astype(BF)
    f_ref[...] = proj(w_ssm, w_fn).astype(BF)
    q = proj(w_ssm + w_fn, w_na)
    k = proj(w_ssm + w_fn + w_na, w_na)
    v_ref[...] = proj(w_ssm + w_fn + 2 * w_na, w_na).astype(BF)
    scale = HEAD_DIM ** -0.5
    if rope:
        cos = cos_ref[...]
        sin = sin_ref[...]
        lane = lax.broadcasted_iota(jnp.int32, cos.shape, 1)
        first = (lane % 32) < 16
        for j in range(w_na // LANES):
            sl = slice(j * LANES, (j + 1) * LANES)
            q_ref[:, sl] = (_rope_tile(q[:, sl], cos, sin, first) * scale).astype(BF)
            k_ref[:, sl] = _rope_tile(k[:, sl], cos, sin, first).astype(BF)
    else:
        q_ref[...] = (q * scale).astype(BF)
        k_ref[...] = k.astype(BF)


def _inproj(x2d, mod3, g_pre, w_in, cos, sin, *, tm, tiles_per_batch, mod_row, widths, rope):
    n, d = x2d.shape
    w_ssm, w_fn, w_na = widths
    nt = n // tm
    if mod_row is None:
        row = lambda i: i // tiles_per_batch
    else:
        row = lambda i: mod_row
    pos = lambda i: (i % tiles_per_batch, 0)
    kern = functools.partial(_inproj_kernel, widths=widths, rope=rope)
    return pl.pallas_call(
        kern,
        grid=(nt,),
        in_specs=[pl.BlockSpec((tm, d), lambda i: (i, 0)),
                  pl.BlockSpec((None, 1, d), lambda i: (row(i), 0, 0)),
                  pl.BlockSpec((None, 1, d), lambda i: (row(i), 0, 1)),
                  pl.BlockSpec((1, d), lambda i: (0, 0)),
                  _resident(w_in.shape, lambda i: (0, 0)),
                  pl.BlockSpec((tm, LANES), pos),
                  pl.BlockSpec((tm, LANES), pos)],
        out_specs=[pl.BlockSpec((w_ssm // LANES, tm, LANES), lambda i: (0, i, 0)),
                   pl.BlockSpec((tm, w_fn), lambda i: (i, 0)),
                   pl.BlockSpec((tm, w_na), lambda i: (i, 0)),
                   pl.BlockSpec((tm, w_na), lambda i: (i, 0)),
                   pl.BlockSpec((tm, w_na), lambda i: (i, 0))],
        out_shape=[jax.ShapeDtypeStruct((w_ssm // LANES, n, LANES), BF),
                   jax.ShapeDtypeStruct((n, w_fn), BF),
                   jax.ShapeDtypeStruct((n, w_na), BF),
                   jax.ShapeDtypeStruct((n, w_na), BF),
                   jax.ShapeDtypeStruct((n, w_na), BF)],
        compiler_params=_cparams("parallel"),
        name="inproj_rope" if rope else "inproj_ctx",
    )(x2d, mod3, mod3, g_pre.reshape(1, d), w_in, cos, sin)


def _rope_tables(seq):
    t = jnp.arange(seq)
    row = (t // GRID_W).astype(F32)
    col = (t % GRID_W).astype(F32)
    quarter = HEAD_DIM // 4
    freqs = ROPE_BASE ** (-jnp.arange(quarter, dtype=F32) / quarter)
    d = np.arange(LANES) % HEAD_DIM
    use_col = jnp.asarray((d // (HEAD_DIM // 2)) == 1)
    fidx = d % quarter
    sign = jnp.asarray(np.where((d % (HEAD_DIM // 2)) // quarter == 0, -1.0, 1.0), F32)
    pos = jnp.where(use_col[None, :], col[:, None], row[:, None])
    ang = pos * freqs[fidx][None, :]
    return jnp.cos(ang), jnp.sin(ang) * sign[None, :]


def _gelu_tanh(x):
    c = math.sqrt(2.0 / math.pi)
    return x * (0.5 * (1.0 + jnp.tanh(c * (x + 0.044715 * (x * x * x)))))


def _ssm_kernel(u_ref, bm_ref, km_ref, lam_ref, lamseg_ref, y_ref, yc_ref,
                s_ref, h_ref, e_ref, *, seg_len, ctx_lo, n_ctx):
    rows, tw = u_ref.shape
    nt = s_ref.shape[0]
    tpq = nt // 4
    nst = tpq * LANES
    u = u_ref[...]
    for q in range(4):
        r = jnp.dot(u, bm_ref[:, q * nst:(q + 1) * nst], preferred_element_type=F32)
        for t in range(tpq):
            s_ref[q * tpq + t] = r[:, t * LANES:(t + 1) * LANES]

    def lanes(c):
        return slice(c * LANES, (c + 1) * LANES)

    def scan_step(i, carry, record):
        new = list(carry)
        for d, ii in ((0, i), (1, seg_len - 1 - i)):
            rsl = pl.ds(ii, SSM_SEG, stride=seg_len)
            for t in range(tpq):
                cr, ci = 2 * d * tpq + t, (2 * d + 1) * tpq + t
                hr, hi = carry[cr], carry[ci]
                if record:
                    h_ref[cr, rsl, :] = hr
                    h_ref[ci, rsl, :] = hi
                lr, li = lam_ref[:, lanes(cr)], lam_ref[:, lanes(ci)]
                new[cr] = lr * hr - li * hi + s_ref[cr, rsl, :]
                new[ci] = lr * hi + li * hr + s_ref[ci, rsl, :]
        return tuple(new)

    zero = jnp.zeros((SSM_SEG, LANES), F32)
    fin = lax.fori_loop(0, seg_len, lambda i, c: scan_step(i, c, False), (zero,) * nt)
    for c in range(nt):
        e_ref[c, 0:SSM_SEG, :] = fin[c]

    for d in range(2):
        for t in range(tpq):
            cr, ci = 2 * d * tpq + t, (2 * d + 1) * tpq + t
            lr, li = lamseg_ref[:, lanes(cr)], lamseg_ref[:, lanes(ci)]
            er = jnp.zeros((1, LANES), F32)
            ei = jnp.zeros((1, LANES), F32)
            for n in range(SSM_SEG):
                s = n if d == 0 else SSM_SEG - 1 - n
                e_ref[cr, SSM_SEG + s:SSM_SEG + s + 1, :] = er
                e_ref[ci, SSM_SEG + s:SSM_SEG + s + 1, :] = ei
                fr, fi = e_ref[cr, s:s + 1, :], e_ref[ci, s:s + 1, :]
                er, ei = lr * er - li * ei + fr, lr * ei + li * er + fi

    ent = tuple(e_ref[c, SSM_SEG:2 * SSM_SEG, :] for c in range(nt))
    lax.fori_loop(0, seg_len, lambda i, c: scan_step(i, c, True), ent)

    hcat = jnp.concatenate([h_ref[c].astype(BF) for c in range(nt)], axis=1)
    y = (jnp.dot(u, km_ref[0:tw, :], preferred_element_type=F32)
         + jnp.dot(hcat, km_ref[tw:, :], preferred_element_type=F32))
    y_ref[...] = _gelu_tanh(y).astype(BF)

    uc = u_ref[ctx_lo:ctx_lo + n_ctx, :]
    hc = jnp.concatenate([h_ref[c, ctx_lo:ctx_lo + n_ctx, :].astype(BF) for c in range(2 * tpq)]
                         + [h_ref[c, rows - n_ctx:rows, :].astype(BF) for c in range(2 * tpq, nt)], axis=1)
    yc = (jnp.dot(uc, km_ref[0:tw, :], preferred_element_type=F32)
          + jnp.dot(hc, km_ref[tw:, :], preferred_element_type=F32))
    yc_ref[...] = _gelu_tanh(yc).astype(BF)


def _ssm_weights(a_re, a_im, log_dt, b_re, b_im, c_re, c_im, d_skip, seg_len):
    T = SSM_T
    g, p = a_re.shape[1], a_re.shape[2]
    hc = b_re.shape[-1]
    gpt = LANES // hc
    J = g // gpt
    lam = lax.complex(a_re.astype(F32), a_im.astype(F32))
    dt = jnp.exp(log_dt.astype(F32))[..., None]
    ldt = lam * dt
    lam_bar = jnp.exp(ldt)
    bbar = ((lam_bar - 1) / lam)[..., None] * lax.complex(b_re.astype(F32), b_im.astype(F32))
    cmat = lax.complex(c_re.astype(F32), c_im.astype(F32))
    kk = jnp.arange(T + 1, dtype=F32)
    pw = jnp.exp(ldt[:, None] * kk[None, :, None, None])
    eye = jnp.eye(gpt, dtype=F32)

    def tile_rows(w):
        return jnp.moveaxis(w.reshape((T, J, gpt) + w.shape[2:]), 1, 0)

    wf = pw[0, T - 1 - jnp.arange(T)][:, :, None, :] * jnp.swapaxes(bbar[0], 1, 2)[None]
    wb = pw[1, jnp.arange(T)][:, :, None, :] * jnp.swapaxes(bbar[1], 1, 2)[None]
    parts = []
    for w in (wf.real, wf.imag, wb.real, wb.imag):
        wt = tile_rows(w)
        parts.append(jnp.einsum('jtacp,ab->jtacbp', wt, eye).reshape(J, T * LANES, gpt * p))
    bmat = jnp.concatenate(parts, axis=-1)

    df = cmat[0][None] * pw[0, 1 + jnp.arange(T)][:, :, None, :]
    db = cmat[1][None] * pw[1, T - jnp.arange(T)][:, :, None, :]
    cparts = []
    for w in (df.real, -df.imag, db.real, -db.imag):
        wt = tile_rows(w)
        cparts.append(jnp.einsum('jtacp,ab->jbptac', wt, eye).reshape(J, gpt * p, T * LANES))
    cm = jnp.concatenate(cparts, axis=1)

    mf = jnp.einsum('gcp,kgp,gph->kgch', cmat[0], pw[0, :T], bbar[0]).real
    mb = jnp.einsum('gcp,kgp,gph->kgch', cmat[1], pw[1, :T], bbar[1]).real
    skip = d_skip.astype(F32).reshape(g, hc)[:, :, None] * jnp.eye(hc, dtype=F32)[None]
    m0 = mf[0] + mb[0] + skip
    lags = jnp.concatenate([mb[1:][::-1], m0[None], mf[1:]], axis=0)
    tin = np.arange(T)[:, None]
    tout = np.arange(T)[None, :]
    kt = lags[jnp.asarray(tout - tin + T - 1)]
    kt = jnp.moveaxis(kt.reshape(T, T, J, gpt, hc, hc), 2, 0)
    kin = jnp.einsum('jtsaoi,ab->jtaisbo', kt, eye).reshape(J, T * LANES, T * LANES)
    kmat = jnp.concatenate([kin, cm], axis=1)

    def lam_rows(power):
        lp = jnp.exp(ldt * power)
        lt = lp.reshape(2, J, gpt * p)
        return jnp.concatenate([lt[0].real, lt[0].imag, lt[1].real, lt[1].imag], axis=-1)[:, None, :]

    lam_t = jnp.broadcast_to(lam_rows(float(T)), (J, SSM_SEG, 4 * gpt * p))
    return bmat.astype(BF), kmat.astype(BF), lam_t, lam_rows(float(T * seg_len))


def _ssm(ucat, bmat, kmat, lam, lamseg, *, ctx_lo, n_ctx):
    J, B, rows, tw = ucat.shape
    ns = bmat.shape[-1]
    seg_len = rows // SSM_SEG
    kern = functools.partial(_ssm_kernel, seg_len=seg_len, ctx_lo=ctx_lo, n_ctx=n_ctx)
    return pl.pallas_call(
        kern,
        grid=(J, B),
        in_specs=[pl.BlockSpec((None, None, rows, tw), lambda j, b: (j, b, 0, 0)),
                  pl.BlockSpec((None, tw, ns), lambda j, b: (j, 0, 0), pipeline_mode=pl.Buffered(1)),
                  pl.BlockSpec((None, tw + ns, tw), lambda j, b: (j, 0, 0), pipeline_mode=pl.Buffered(1)),
                  pl.BlockSpec((None, SSM_SEG, ns), lambda j, b: (j, 0, 0)),
                  pl.BlockSpec((None, 1, ns), lambda j, b: (j, 0, 0))],
        out_specs=[pl.BlockSpec((None, None, rows, tw), lambda j, b: (j, b, 0, 0)),
                   pl.BlockSpec((None, None, n_ctx, tw), lambda j, b: (j, b, 0, 0))],
        out_shape=[jax.ShapeDtypeStruct((J, B, rows, tw), BF),
                   jax.ShapeDtypeStruct((J, B, n_ctx, tw), BF)],
        scratch_shapes=[pltpu.VMEM((ns // LANES, rows, LANES), F32),
                        pltpu.VMEM((ns // LANES, rows, LANES), F32),
                        pltpu.VMEM((ns // LANES, 2 * SSM_SEG, LANES), F32)],
        compiler_params=_cparams("arbitrary", "arbitrary"),
        name="ssm_chunk_scan",
    )(ucat, bmat, kmat, lam, lamseg)


def _dft_a_kernel(w_ref, x_ref, o_ref):
    o_ref[...] = jnp.dot(w_ref[...], x_ref[...], preferred_element_type=F32).astype(BF)


def _dft_rows(n):
    k = np.arange(n)
    m = (k[:, None] * k[None, :]) % n
    ang = jnp.asarray(m, F32) * (2.0 * math.pi / n)
    return jnp.concatenate([jnp.cos(ang), -jnp.sin(ang)], axis=0)


def _dft_a(w, x3):
    B, n, cols = x3.shape
    tn = min(cols, 4096)
    return pl.pallas_call(
        _dft_a_kernel,
        grid=(B, cols // tn),
        in_specs=[pl.BlockSpec((2 * n, n), lambda b, i: (0, 0)),
                  pl.BlockSpec((None, n, tn), lambda b, i: (b, 0, i))],
        out_specs=pl.BlockSpec((None, 2 * n, tn), lambda b, i: (b, 0, i)),
        out_shape=jax.ShapeDtypeStruct((B, 2 * n, cols), BF),
        compiler_params=_cparams("parallel", "parallel"),
        name="dft_stage_a",
    )(w, x3)


def _dft_b_kernel(a_ref, g_ref, o_ref, *, kb, l1, wf):
    for i in range(kb):
        g = jnp.concatenate([g_ref[0, i], g_ref[1, i]], axis=0)
        p = jnp.dot(a_ref[i], g, preferred_element_type=F32)
        o_ref[:, i * 2 * wf:i * 2 * wf + wf] = p[:l1].astype(BF)
        o_ref[:, i * 2 * wf + wf:(i + 1) * 2 * wf] = p[l1:].astype(BF)


def _dft_b_mats(seq, l1, l2):
    k2 = np.arange(l2)[:, None, None]
    k1 = np.arange(l1)[None, :, None]
    j1 = np.arange(l1)[None, None, :]
    m = ((k1 * l2 + k2) * j1) % seq
    ang = jnp.asarray(m, F32) * (2.0 * math.pi / seq)
    ar, ai = jnp.cos(ang), -jnp.sin(ang)
    top = jnp.concatenate([ar, -ai], axis=2)
    bot = jnp.concatenate([ai, ar], axis=2)
    return jnp.concatenate([top, bot], axis=1)


def _dft_b(amat, g5, *, kb):
    B, _, l2, l1, wf = g5.shape
    kern = functools.partial(_dft_b_kernel, kb=kb, l1=l1, wf=wf)
    return pl.pallas_call(
        kern,
        grid=(B, l2 // kb),
        in_specs=[pl.BlockSpec((kb, 2 * l1, 2 * l1), lambda b, i: (i, 0, 0)),
                  pl.BlockSpec((None, 2, kb, l1, wf), lambda b, i: (b, 0, i, 0, 0))],
        out_specs=pl.BlockSpec((None, l1, kb * 2 * wf), lambda b, i: (b, 0, i)),
        out_shape=jax.ShapeDtypeStruct((B, l1, l2 * 2 * wf), BF),
        compiler_params=_cparams("parallel", "parallel"),
        name="dft_stage_b",
    )(amat, g5)


def _dft_ctx_kernel(w_ref, x_ref, o_ref, *, n, wf):
    p = jnp.dot(w_ref[...], x_ref[...], preferred_element_type=F32)
    o_ref[:, 0:wf] = p[:n].astype(BF)
    o_ref[:, wf:2 * wf] = p[n:].astype(BF)


def _dft_ctx(w, x3):
    B, n, wf = x3.shape
    kern = functools.partial(_dft_ctx_kernel, n=n, wf=wf)
    return pl.pallas_call(
        kern,
        grid=(B,),
        in_specs=[pl.BlockSpec((2 * n, n), lambda b: (0, 0)),
                  pl.BlockSpec((None, n, wf), lambda b: (b, 0, 0))],
        out_specs=pl.BlockSpec((None, n, 2 * wf), lambda b: (b, 0, 0)),
        out_shape=jax.ShapeDtypeStruct((B, n, 2 * wf), BF),
        compiler_params=_cparams("parallel"),
        name="dft_ctx",
    )(w, x3)


def _channel_dft(wf, seq):
    c = np.arange(wf)
    same = (c[:, None] // FNET_GROUP) == (c[None, :] // FNET_GROUP)
    m = ((c[:, None] % FNET_GROUP) * (c[None, :] % FNET_GROUP)) % FNET_GROUP
    ang = jnp.asarray(m, F32) * (2.0 * math.pi / FNET_GROUP)
    norm = 1.0 / math.sqrt(seq * FNET_GROUP)
    mask = jnp.asarray(same, F32) * norm
    return jnp.concatenate([jnp.cos(ang) * mask, jnp.sin(ang) * mask], axis=0)


def _softmax_pv(parts, l_shape):
    m = parts[0][0].max(axis=-1, keepdims=True)
    for s, _ in parts[1:]:
        m = jnp.maximum(m, s.max(axis=-1, keepdims=True))
    l = jnp.zeros(l_shape, F32)
    o = None
    for s, v in parts:
        p = jnp.exp(s - m)
        l = l + p.sum(axis=-1, keepdims=True)
        pv = jnp.dot(p.astype(BF), v, preferred_element_type=F32)
        o = pv if o is None else o + pv
    return o / l


def _na_kernel(kb_ref, pid_ref, q_ref, k_ref, v_ref, kc_ref, vc_ref, bias_ref, o_ref, *, nkeys, heads):
    del pid_ref
    i = pl.program_id(1)
    start = pl.multiple_of(kb_ref[i] * GRID_W, GRID_W)
    nq = q_ref.shape[0]
    lane = lax.broadcasted_iota(jnp.int32, (nq, LANES), 1)
    low = lane < HEAD_DIM
    dn = (((1,), (1,)), ((), ()))
    for hp in range(heads // 2):
        sl = slice(hp * LANES, (hp + 1) * LANES)
        q2 = q_ref[:, sl]
        k2 = k_ref[pl.ds(start, nkeys), sl]
        v2 = v_ref[pl.ds(start, nkeys), sl]
        kc2 = kc_ref[:, sl]
        vc2 = vc_ref[:, sl]
        outs = []
        for hh in range(2):
            qm = jnp.where(low if hh == 0 else jnp.logical_not(low), q2, jnp.zeros_like(q2))
            s_w = lax.dot_general(qm, k2, dn, preferred_element_type=F32) + bias_ref[2 * hp + hh]
            s_c = lax.dot_general(qm, kc2, dn, preferred_element_type=F32)
            outs.append(_softmax_pv([(s_w, v2), (s_c, vc2)], (nq, 1)))
        o_ref[:, sl] = jnp.where(low, outs[0], outs[1]).astype(BF)


def _na_plan(rows, rq):
    wr = min(NA_ROWS, rows)
    kr = rq + wr - 1
    nblk = rows // rq
    kbs, pids, pats = [], [], []
    for blk in range(nblk):
        r0 = blk * rq
        rs = [int(np.clip(r0 + i - wr // 2, 0, rows - wr)) for i in range(rq)]
        kb = min(rs[0], rows - kr)
        pat = (tuple(r - kb for r in rs), r0 - kb)
        if pat not in pats:
            pats.append(pat)
        kbs.append(kb)
        pids.append(pats.index(pat))
    return np.asarray(kbs, np.int32), np.asarray(pids, np.int32), pats, kr, wr


def _na_bias(rpb, pats, rq, kr, wr):
    cols = np.arange(GRID_W)
    cstart = np.clip(cols - NA_COLS // 2, 0, GRID_W - NA_COLS)
    kc = np.arange(GRID_W)
    col_ok = (kc[None, :] >= cstart[:, None]) & (kc[None, :] < cstart[:, None] + NA_COLS)
    dc = np.clip(kc[None, :] - cols[:, None] + NA_COLS - 1, 0, 2 * NA_COLS - 2)
    tabs = []
    for rs_off, r_off in pats:
        i = np.arange(rq)[:, None]
        w = np.arange(kr)[None, :]
        rso = np.asarray(rs_off)[:, None]
        row_ok = (w >= rso) & (w < rso + wr)
        dr = np.clip(w - r_off - i + NA_ROWS - 1, 0, 2 * NA_ROWS - 2)
        ok = row_ok[:, None, :, None] & col_ok[None, :, None, :]
        b = rpb[:, jnp.asarray(dr)[:, None, :, None], jnp.asarray(dc)[None, :, None, :]]
        b = jnp.where(jnp.asarray(ok)[None], b.astype(F32), NEG)
        tabs.append(b.reshape(rpb.shape[0], rq * GRID_W, kr * GRID_W))
    return jnp.stack(tabs)


def _na_latent(q, k, v, kc, vc, rpb, *, batch, seq, n_ctx, rq):
    rows = seq // GRID_W
    heads = rpb.shape[0]
    kbs, pids, pats, kr, wr = _na_plan(rows, rq)
    bias = _na_bias(rpb, pats, rq, kr, wr)
    nblk = rows // rq
    nq = rq * GRID_W
    nkeys = kr * GRID_W
    w = q.shape[1]
    kern = functools.partial(_na_kernel, nkeys=nkeys, heads=heads)
    grid_spec = pltpu.PrefetchScalarGridSpec(
        num_scalar_prefetch=2,
        grid=(batch, nblk),
        in_specs=[pl.BlockSpec((nq, w), lambda b, i, kb, pid: (b * nblk + i, 0)),
                  pl.BlockSpec((seq, w), lambda b, i, kb, pid: (b, 0)),
                  pl.BlockSpec((seq, w), lambda b, i, kb, pid: (b, 0)),
                  pl.BlockSpec((n_ctx, w), lambda b, i, kb, pid: (b, 0)),
                  pl.BlockSpec((n_ctx, w), lambda b, i, kb, pid: (b, 0)),
                  pl.BlockSpec((None, heads, nq, nkeys), lambda b, i, kb, pid: (pid[i], 0, 0, 0))],
        out_specs=pl.BlockSpec((nq, w), lambda b, i, kb, pid: (b * nblk + i, 0)),
    )
    return pl.pallas_call(
        kern,
        grid_spec=grid_spec,
        out_shape=jax.ShapeDtypeStruct(q.shape, BF),
        compiler_params=_cparams("arbitrary", "arbitrary"),
        name="na_latent",
    )(jnp.asarray(kbs), jnp.asarray(pids), q, k, v, kc, vc, bias)


def _na_ctx_kernel(q_ref, k_ref, v_ref, o_ref, *, heads):
    nq = q_ref.shape[0]
    lane = lax.broadcasted_iota(jnp.int32, (nq, LANES), 1)
    low = lane < HEAD_DIM
    dn = (((1,), (1,)), ((), ()))
    for hp in range(heads // 2):
        sl = slice(hp * LANES, (hp + 1) * LANES)
        q2 = q_ref[:, sl]
        k2 = k_ref[:, sl]
        v2 = v_ref[:, sl]
        outs = []
        for hh in range(2):
            qm = jnp.where(low if hh == 0 else jnp.logical_not(low), q2, jnp.zeros_like(q2))
            s = lax.dot_general(qm, k2, dn, preferred_element_type=F32)
            outs.append(_softmax_pv([(s, v2)], (nq, 1)))
        o_ref[:, sl] = jnp.where(low, outs[0], outs[1]).astype(BF)


def _na_ctx(qc, kc, vc, *, batch, n_ctx, heads):
    w = qc.shape[1]
    spec = pl.BlockSpec((n_ctx, w), lambda b: (b, 0))
    return pl.pallas_call(
        functools.partial(_na_ctx_kernel, heads=heads),
        grid=(batch,),
        in_specs=[spec, spec, spec],
        out_specs=spec,
        out_shape=jax.ShapeDtypeStruct(qc.shape, BF),
        compiler_params=_cparams("parallel"),
        name="na_ctx",
    )(qc, kc, vc)


def _outproj_kernel(z_ref, pf_ref, na_ref, x_ref, gate_ref, gpost_ref,
                    wglu_ref, cs_ref, wfo_ref, wout_ref, o_ref, *, widths):
    w_ssm, w_fn, w_na = widths
    z = jnp.concatenate([z_ref[j] for j in range(w_ssm // LANES)], axis=1)
    gl = jnp.dot(z, wglu_ref[...], preferred_element_type=F32)
    y_ssm = (z.astype(F32) * jax.nn.sigmoid(gl)).astype(BF)
    mixed = jnp.dot(pf_ref[...], cs_ref[...], preferred_element_type=F32).astype(BF)
    y_fft = jnp.dot(mixed, wfo_ref[...], preferred_element_type=F32).astype(BF)
    o = (jnp.dot(y_ssm, wout_ref[0:w_ssm, :], preferred_element_type=F32)
         + jnp.dot(y_fft, wout_ref[w_ssm:w_ssm + w_fn, :], preferred_element_type=F32)
         + jnp.dot(na_ref[...], wout_ref[w_ssm + w_fn:, :], preferred_element_type=F32))
    ms = jnp.mean(o * o, axis=-1, keepdims=True)
    o_ref[...] = x_ref[...] + gate_ref[...] * (o * lax.rsqrt(ms + EPS) * gpost_ref[...])


def _outproj(z3, z_off, pf, yna, x2d, mod3, g_post, w_glu, cs, w_fo, w_out, *,
             tm, tiles_per_batch, mod_row, widths):
    n, d = x2d.shape
    w_ssm, w_fn, w_na = widths
    if mod_row is None:
        row = lambda i: i // tiles_per_batch
        zmap = lambda i: (0, i // tiles_per_batch, z_off + i % tiles_per_batch, 0)
    else:
        row = lambda i: mod_row
        zmap = lambda i: (0, i // tiles_per_batch, z_off + i % tiles_per_batch, 0)
    kern = functools.partial(_outproj_kernel, widths=widths)
    return pl.pallas_call(
        kern,
        grid=(n // tm,),
        in_specs=[pl.BlockSpec((w_ssm // LANES, None, tm, LANES), zmap),
                  pl.BlockSpec((tm, 2 * w_fn), lambda i: (i, 0)),
                  pl.BlockSpec((tm, w_na), lambda i: (i, 0)),
                  pl.BlockSpec((tm, d), lambda i: (i, 0)),
                  pl.BlockSpec((None, 1, d), lambda i: (row(i), 0, 2)),
                  pl.BlockSpec((1, d), lambda i: (0, 0)),
                  _resident(w_glu.shape, lambda i: (0, 0)),
                  _resident(cs.shape, lambda i: (0, 0)),
                  _resident(w_fo.shape, lambda i: (0, 0)),
                  _resident(w_out.shape, lambda i: (0, 0))],
        out_specs=pl.BlockSpec((tm, d), lambda i: (i, 0)),
        out_shape=jax.ShapeDtypeStruct((n, d), F32),
        compiler_params=_cparams("parallel"),
        name="outproj",
    )(z3, pf, yna, x2d, mod3, g_post.reshape(1, d), w_glu, cs, w_fo, w_out)


def _ffn_kernel(x_ref, sh_ref, sc_ref, gate_ref, gpre_ref, gpost_ref, wg_ref, wu_ref, wd_ref,
                o_ref, a_ref, *, nchunk):
    x = x_ref[...]
    ms = jnp.mean(x * x, axis=-1, keepdims=True)
    m = ((x * lax.rsqrt(ms + EPS) * gpre_ref[...]) * (1.0 + sc_ref[...]) + sh_ref[...]).astype(BF)
    fc = wg_ref.shape[1] // nchunk
    for c in range(nchunk):
        sl = slice(c * fc, (c + 1) * fc)
        g = jnp.dot(m, wg_ref[:, sl], preferred_element_type=F32)
        u = jnp.dot(m, wu_ref[:, sl], preferred_element_type=F32)
        a_ref[:, sl] = (g * jax.nn.sigmoid(g) * u).astype(BF)
    y = jnp.dot(a_ref[...], wd_ref[...], preferred_element_type=F32)
    ms2 = jnp.mean(y * y, axis=-1, keepdims=True)
    o_ref[...] = x + gate_ref[...] * (y * lax.rsqrt(ms2 + EPS) * gpost_ref[...])


def _ffn(x2d, mod3, g_pre, g_post, wg, wu, wd, *, tm, tiles_per_batch, mod_row):
    n, d = x2d.shape
    dff = wg.shape[1]
    if mod_row is None:
        row = lambda i: i // tiles_per_batch
    else:
        row = lambda i: mod_row
    nchunk = 2 if dff % (2 * LANES) == 0 else 1
    return pl.pallas_call(
        functools.partial(_ffn_kernel, nchunk=nchunk),
        grid=(n // tm,),
        in_specs=[pl.BlockSpec((tm, d), lambda i: (i, 0)),
                  pl.BlockSpec((None, 1, d), lambda i: (row(i), 0, 3)),
                  pl.BlockSpec((None, 1, d), lambda i: (row(i), 0, 4)),
                  pl.BlockSpec((None, 1, d), lambda i: (row(i), 0, 5)),
                  pl.BlockSpec((1, d), lambda i: (0, 0)),
                  pl.BlockSpec((1, d), lambda i: (0, 0)),
                  _resident(wg.shape, lambda i: (0, 0)),
                  _resident(wu.shape, lambda i: (0, 0)),
                  _resident(wd.shape, lambda i: (0, 0))],
        out_specs=pl.BlockSpec((tm, d), lambda i: (i, 0)),
        out_shape=jax.ShapeDtypeStruct((n, d), F32),
        scratch_shapes=[pltpu.VMEM((tm, dff), BF)],
        compiler_params=_cparams("parallel"),
        name="ffn",
    )(x2d, mod3, mod3, mod3, g_pre.reshape(1, d), g_post.reshape(1, d), wg, wu, wd)


def _layer(x2d, xc2d, cstack, p, *, batch, seq, n_ctx, last, rope_tabs, consts):
    d = x2d.shape[1]
    w_ssm = p["ssm_d"].shape[0]
    w_fn = p["w_fourier"].shape[0]
    w_na = (p["w_in"].shape[1] - w_ssm - w_fn) // 3
    widths = (w_ssm, w_fn, w_na)
    heads = w_na // HEAD_DIM
    T = SSM_T
    tm = min(512, seq)
    tmc = min(512, batch * n_ctx)

    mod = _mod_rows(cstack, p["w_mod"], p["b_mod"])
    mod3 = mod.reshape(mod.shape[0], 1, 6 * d)

    w_in = p["w_in"].astype(BF)
    cos, sin = rope_tabs
    u3, f, q, k, v = _inproj(x2d, mod3, p["g_pre_mix"], w_in, cos, sin, tm=tm,
                             tiles_per_batch=seq // tm, mod_row=None, widths=widths, rope=True)
    ones = jnp.ones((tmc, LANES), F32)
    u3c, fc, qc, kc, vc = _inproj(xc2d, mod3, p["g_pre_mix"], w_in, ones, ones, tm=tmc,
                                  tiles_per_batch=1, mod_row=batch, widths=widths, rope=False)

    J = w_ssm // LANES
    nlat, nctx = seq // T, n_ctx // T
    pad = (-nctx) % (tm // T)
    rows = pad + nctx + nlat + nctx
    rows_pad = (-rows) % SSM_SEG
    ul = u3.reshape(J, batch, nlat, T * LANES)
    uc = u3c.reshape(J, batch, nctx, T * LANES)
    ucat = jnp.concatenate([jnp.zeros((J, batch, pad, T * LANES), BF), uc, ul, uc,
                            jnp.zeros((J, batch, rows_pad, T * LANES), BF)], axis=2)
    rows += rows_pad
    bmat, kmat, lam, lamseg = _ssm_weights(p["ssm_a_re"], p["ssm_a_im"], p["ssm_log_dt"], p["ssm_b_re"],
                                           p["ssm_b_im"], p["ssm_c_re"], p["ssm_c_im"], p["ssm_d"],
                                           rows // SSM_SEG)
    assert rows_pad == 0
    z4, zc4 = _ssm(ucat, bmat, kmat, lam, lamseg, ctx_lo=pad, n_ctx=nctx)
    z3 = z4.reshape(J, batch, rows * T, LANES)
    z_off = (pad + nctx) * T // tm

    l2 = consts["l2"]
    l1 = seq // l2
    g = _dft_a(consts["dft_a"], f.reshape(batch, l2, l1 * w_fn))
    pf = _dft_b(consts["dft_b"], g.reshape(batch, 2, l2, l1, w_fn), kb=min(8, l2))
    pf = pf.reshape(batch * seq, 2 * w_fn)

    yna = _na_latent(q, k, v, kc, vc, p["na_rpb"], batch=batch, seq=seq, n_ctx=n_ctx, rq=consts["rq"])

    w_glu = p["w_glu"].astype(BF)
    w_fo = p["w_fourier"].astype(BF)
    w_out = p["w_out"].astype(BF)
    x2d = _outproj(z3, z_off, pf, yna, x2d, mod3, p["g_post_mix"], w_glu, consts["cs_lat"], w_fo, w_out,
                   tm=tm, tiles_per_batch=seq // tm, mod_row=None, widths=widths)
    wg = p["w_ffn_gate"].astype(BF)
    wu = p["w_ffn_up"].astype(BF)
    wd = p["w_ffn_down"].astype(BF)
    x2d = _ffn(x2d, mod3, p["g_pre_ffn"], p["g_post_ffn"], wg, wu, wd,
               tm=tm, tiles_per_batch=seq // tm, mod_row=None)

    if not last:
        pfc = _dft_ctx(consts["dft_ctx"], fc.reshape(batch, n_ctx, w_fn)).reshape(batch * n_ctx, 2 * w_fn)
        ynac = _na_ctx(qc, kc, vc, batch=batch, n_ctx=n_ctx, heads=heads)
        zc3 = zc4.reshape(J, batch, n_ctx, LANES)
        tc = min(tmc, n_ctx)
        xc2d = _outproj(zc3, 0, pfc, ynac, xc2d, mod3, p["g_post_mix"], w_glu, consts["cs_ctx"], w_fo, w_out,
                        tm=tc, tiles_per_batch=n_ctx // tc, mod_row=batch, widths=widths)
        xc2d = _ffn(xc2d, mod3, p["g_pre_ffn"], p["g_post_ffn"], wg, wu, wd,
                    tm=tc, tiles_per_batch=n_ctx // tc, mod_row=batch)
    else:
        xc2d = None
    return x2d, xc2d


def kernel(x, c, ctx, c_ctx, w_mod, b_mod, g_pre_mix, g_post_mix, w_in, ssm_a_re, ssm_a_im, ssm_log_dt,
           ssm_b_re, ssm_b_im, ssm_c_re, ssm_c_im, ssm_d, w_glu, w_fourier, na_rpb, w_out, g_pre_ffn,
           g_post_ffn, w_ffn_gate, w_ffn_up, w_ffn_down):
    batch, seq, d = x.shape
    n_ctx = ctx.shape[1]
    depth = w_mod.shape[0]
    w_fn = w_fourier.shape[1]
    params = dict(w_mod=w_mod, b_mod=b_mod, g_pre_mix=g_pre_mix, g_post_mix=g_post_mix, w_in=w_in,
                  ssm_a_re=ssm_a_re, ssm_a_im=ssm_a_im, ssm_log_dt=ssm_log_dt, ssm_b_re=ssm_b_re,
                  ssm_b_im=ssm_b_im, ssm_c_re=ssm_c_re, ssm_c_im=ssm_c_im, ssm_d=ssm_d, w_glu=w_glu,
                  w_fourier=w_fourier, na_rpb=na_rpb, w_out=w_out, g_pre_ffn=g_pre_ffn,
                  g_post_ffn=g_post_ffn, w_ffn_gate=w_ffn_gate, w_ffn_up=w_ffn_up, w_ffn_down=w_ffn_down)

    nrow = -(-(batch + 1) // 8) * 8
    cstack = jnp.concatenate([c, c_ctx[None, :], jnp.zeros((nrow - batch - 1, d), c.dtype)], axis=0)

    l2 = 64 if seq % (64 * 8) == 0 else 8
    l1 = seq // l2
    consts = dict(
        l2=l2,
        rq=4,
        dft_a=_dft_rows(l2).astype(BF),
        dft_b=_dft_b_mats(seq, l1, l2).astype(BF),
        dft_ctx=_dft_rows(n_ctx).astype(BF),
        cs_lat=_channel_dft(w_fn, seq).astype(BF),
        cs_ctx=_channel_dft(w_fn, n_ctx).astype(BF),
    )
    rope_tabs = _rope_tables(seq)

    x2d = x.reshape(batch * seq, d)
    xc2d = ctx.reshape(batch * n_ctx, d)
    for layer in range(depth):
        p = {name: val[layer] for name, val in params.items()}
        x2d, xc2d = _layer(x2d, xc2d, cstack, p, batch=batch, seq=seq, n_ctx=n_ctx,
                           last=(layer == depth - 1), rope_tabs=rope_tabs, consts=consts)
    return x2d.reshape(batch, seq, d)
```

```python
import functools
import math

import numpy as np
import jax
import jax.numpy as jnp
from jax import lax
from jax.experimental import pallas as pl
from jax.experimental.pallas import tpu as pltpu

BF = jnp.bfloat16
F32 = jnp.float32

EPS = 1e-6
GRID_W = 64
HEAD_DIM = 64
NA_ROWS = 8
NA_COLS = 16
ROPE_BASE = 10000.0
SSM_GROUP = 16
SSM_STATE = 64
FNET_GROUP = 64
LANES = 128
SSM_T = 8
SSM_SEG = 8
NEG = -1e30
VMEM_LIMIT = 56 * 1024 * 1024


def _cparams(*sem):
    return pltpu.CompilerParams(dimension_semantics=sem, vmem_limit_bytes=VMEM_LIMIT)


def _resident(shape, index_map):
    return pl.BlockSpec(shape, index_map, pipeline_mode=pl.Buffered(1))


def _mod_kernel(c_ref, w_ref, b_ref, o_ref):
    cs = c_ref[...]
    s = (cs * jax.nn.sigmoid(cs)).astype(BF)
    o_ref[...] = jnp.dot(s, w_ref[...].astype(BF), preferred_element_type=F32) + b_ref[...]


def _mod_rows(cstack, w_mod, b_mod, layer):
    rows, d = cstack.shape
    depth, _, n = w_mod.shape
    return pl.pallas_call(
        _mod_kernel,
        grid=(n // d,),
        in_specs=[pl.BlockSpec((rows, d), lambda i: (0, 0)),
                  pl.BlockSpec((None, d, d), lambda i: (layer, 0, i)),
                  pl.BlockSpec((None, 1, d), lambda i: (layer, 0, i))],
        out_specs=pl.BlockSpec((rows, d), lambda i: (0, i)),
        out_shape=jax.ShapeDtypeStruct((rows, n), F32),
        compiler_params=_cparams("arbitrary"),
        name="mod_rows",
    )(cstack, w_mod, b_mod.reshape(depth, 1, n))


def _rope_tile(t, cos, sin, first):
    partner = jnp.where(first, pltpu.roll(t, LANES - 16, 1), pltpu.roll(t, 16, 1))
    return t * cos + partner * sin


def _inproj_kernel(x_ref, sh_ref, sc_ref, g_ref, w_ref, cos_ref, sin_ref,
                   u_ref, f_ref, q_ref, k_ref, v_ref, *, widths, rope):
    x = x_ref[...]
    ms = jnp.mean(x * x, axis=-1, keepdims=True)
    xn = x * lax.rsqrt(ms + EPS) * g_ref[...]
    m = (xn * (1.0 + sc_ref[...]) + sh_ref[...]).astype(BF)
    w_ssm, w_fn, w_na = widths

    def proj(lo, n):
        return jnp.dot(m, w_ref[:, lo:lo + n], preferred_element_type=F32)

    u = proj(0, w_ssm)
    for j in range(w_ssm // LANES):
        u_ref[j] = u[:, j * LANES:(j + 1) * LANES]
    f_ref[...] = proj(w_ssm, w_fn).astype(BF)
    q = proj(w_ssm + w_fn, w_na)
    k = proj(w_ssm + w_fn + w_na, w_na)
    v_ref[...] = proj(w_ssm + w_fn + 2 * w_na, w_na).astype(BF)
    scale = HEAD_DIM ** -0.5
    if rope:
        cos = cos_ref[...]
        sin = sin_ref[...]
        lane = lax.broadcasted_iota(jnp.int32, cos.shape, 1)
        first = (lane % 32) < 16
        for j in range(w_na // LANES):
            sl = slice(j * LANES, (j + 1) * LANES)
            q_ref[:, sl] = (_rope_tile(q[:, sl], cos, sin, first) * scale).astype(BF)
            k_ref[:, sl] = _rope_tile(k[:, sl], cos, sin, first).astype(BF)
    else:
        q_ref[...] = (q * scale).astype(BF)
        k_ref[...] = k.astype(BF)


def _inproj(x2d, mod3, g_pre, w_in, cos, sin, *, tm, tiles_per_batch, mod_row, widths, rope):
    n, d = x2d.shape
    w_ssm, w_fn, w_na = widths
    nt = n // tm
    if mod_row is None:
        row = lambda i: i // tiles_per_batch
    else:
        row = lambda i: mod_row
    pos = lambda i: (i % tiles_per_batch, 0)
    kern = functools.partial(_inproj_kernel, widths=widths, rope=rope)
    return pl.pallas_call(
        kern,
        grid=(nt,),
        in_specs=[pl.BlockSpec((tm, d), lambda i: (i, 0)),
                  pl.BlockSpec((None, 1, d), lambda i: (row(i), 0, 0)),
                  pl.BlockSpec((None, 1, d), lambda i: (row(i), 0, 1)),
                  pl.BlockSpec((1, d), lambda i: (0, 0)),
                  _resident(w_in.shape, lambda i: (0, 0)),
                  pl.BlockSpec((tm, LANES), pos),
                  pl.BlockSpec((tm, LANES), pos)],
        out_specs=[pl.BlockSpec((w_ssm // LANES, tm, LANES), lambda i: (0, i, 0)),
                   pl.BlockSpec((tm, w_fn), lambda i: (i, 0)),
                   pl.BlockSpec((tm, w_na), lambda i: (i, 0)),
                   pl.BlockSpec((tm, w_na), lambda i: (i, 0)),
                   pl.BlockSpec((tm, w_na), lambda i: (i, 0))],
        out_shape=[jax.ShapeDtypeStruct((w_ssm // LANES, n, LANES), F32),
                   jax.ShapeDtypeStruct((n, w_fn), BF),
                   jax.ShapeDtypeStruct((n, w_na), BF),
                   jax.ShapeDtypeStruct((n, w_na), BF),
                   jax.ShapeDtypeStruct((n, w_na), BF)],
        compiler_params=_cparams("parallel"),
        name="inproj_rope" if rope else "inproj_ctx",
    )(x2d, mod3, mod3, g_pre.reshape(1, d), w_in, cos, sin)


def _rope_tables(seq):
    t = jnp.arange(seq)
    row = (t // GRID_W).astype(F32)
    col = (t % GRID_W).astype(F32)
    quarter = HEAD_DIM // 4
    freqs = ROPE_BASE ** (-jnp.arange(quarter, dtype=F32) / quarter)
    d = np.arange(LANES) % HEAD_DIM
    use_col = jnp.asarray((d // (HEAD_DIM // 2)) == 1)
    fidx = d % quarter
    sign = jnp.asarray(np.where((d % (HEAD_DIM // 2)) // quarter == 0, -1.0, 1.0), F32)
    pos = jnp.where(use_col[None, :], col[:, None], row[:, None])
    ang = pos * freqs[fidx][None, :]
    return jnp.cos(ang), jnp.sin(ang) * sign[None, :]


def _gelu_tanh(x):
    c = math.sqrt(2.0 / math.pi)
    return x * (0.5 * (1.0 + jnp.tanh(c * (x + 0.044715 * (x * x * x)))))


def _group_of(idx, width, groups):
    shift = width.bit_length() - 1
    assert width == 1 << shift and groups & (groups - 1) == 0
    return lax.bitwise_and(lax.shift_right_logical(idx, shift), groups - 1)


def _same_group(shape, row0, row_width, col0, col_width, groups):
    r = lax.broadcasted_iota(jnp.int32, shape, 0) + row0
    c = lax.broadcasted_iota(jnp.int32, shape, 1) + col0
    return _group_of(r, row_width, groups) == _group_of(c, col_width, groups)


def _ssm_kernel(ul_ref, uc_ref, wc_ref, kc_ref, dc_ref, rexp_ref, cexp_ref, lam_ref, lamseg_ref,
                z_ref, zc_ref, u_s, bm_s, kin_s, cm_s, s_ref, h_ref, e_ref, *, seg_len, n_lat, n_ctx):
    T = SSM_T
    rows, tw = u_s.shape
    nt = s_ref.shape[0]
    tpq = nt // 4
    nst = tpq * LANES
    gpt = LANES // SSM_GROUP
    cw = 512

    def lanes(c):
        return slice(c * LANES, (c + 1) * LANES)

    @pl.when(pl.program_id(1) == 0)
    def _expand_operators():
        rexp = rexp_ref[...]
        for c0 in range(0, 4 * nst, cw):
            blk = jnp.dot(rexp, wc_ref[:, c0:c0 + cw], preferred_element_type=F32)
            keep = _same_group(blk.shape, 0, SSM_GROUP, c0, SSM_STATE, gpt)
            bm_s[:, c0:c0 + cw] = jnp.where(keep, blk, 0.0).astype(BF)
        for c0 in range(0, tw, cw):
            blk = jnp.dot(rexp, kc_ref[:, c0:c0 + cw], preferred_element_type=F32)
            keep = _same_group(blk.shape, 0, SSM_GROUP, c0, SSM_GROUP, gpt)
            kin_s[:, c0:c0 + cw] = jnp.where(keep, blk, 0.0).astype(BF)
        cexp = cexp_ref[...]
        for r0 in range(0, 4 * nst, cw):
            blk = jnp.dot(dc_ref[r0:r0 + cw, :], cexp, preferred_element_type=F32)
            keep = _same_group(blk.shape, r0, SSM_STATE, 0, SSM_GROUP, gpt)
            cm_s[r0:r0 + cw, :] = jnp.where(keep, blk, 0.0).astype(BF)

    for t in range(T):
        ct = uc_ref[pl.ds(t, n_ctx, stride=T), :].astype(BF)
        u_s[0:n_ctx, lanes(t)] = ct
        u_s[n_ctx:n_ctx + n_lat, lanes(t)] = ul_ref[pl.ds(t, n_lat, stride=T), :].astype(BF)
        u_s[n_ctx + n_lat:rows, lanes(t)] = ct
    u = u_s[...]
    for q in range(4):
        r = jnp.dot(u, bm_s[:, q * nst:(q + 1) * nst], preferred_element_type=F32)
        for t in range(tpq):
            s_ref[q * tpq + t] = r[:, t * LANES:(t + 1) * LANES]

    def scan_step(i, carry, record):
        new = list(carry)
        for d, ii in ((0, i), (1, seg_len - 1 - i)):
            rsl = pl.ds(ii, SSM_SEG, stride=seg_len)
            for t in range(tpq):
                cr, ci = 2 * d * tpq + t, (2 * d + 1) * tpq + t
                hr, hi = carry[cr], carry[ci]
                if record:
                    h_ref[cr, rsl, :] = hr
                    h_ref[ci, rsl, :] = hi
                lr, li = lam_ref[:, lanes(cr)], lam_ref[:, lanes(ci)]
                new[cr] = lr * hr - li * hi + s_ref[cr, rsl, :]
                new[ci] = lr * hi + li * hr + s_ref[ci, rsl, :]
        return tuple(new)

    zero = jnp.zeros((SSM_SEG, LANES), F32)
    fin = lax.fori_loop(0, seg_len, lambda i, c: scan_step(i, c, False), (zero,) * nt)
    for c in range(nt):
        e_ref[c, 0:SSM_SEG, :] = fin[c]

    for d in range(2):
        for t in range(tpq):
            cr, ci = 2 * d * tpq + t, (2 * d + 1) * tpq + t
            lr, li = lamseg_ref[:, lanes(cr)], lamseg_ref[:, lanes(ci)]
            er = jnp.zeros((1, LANES), F32)
            ei = jnp.zeros((1, LANES), F32)
            for n in range(SSM_SEG):
                s = n if d == 0 else SSM_SEG - 1 - n
                e_ref[cr, SSM_SEG + s:SSM_SEG + s + 1, :] = er
                e_ref[ci, SSM_SEG + s:SSM_SEG + s + 1, :] = ei
                fr, fi = e_ref[cr, s:s + 1, :], e_ref[ci, s:s + 1, :]
                er, ei = lr * er - li * ei + fr, lr * ei + li * er + fi

    ent = tuple(e_ref[c, SSM_SEG:2 * SSM_SEG, :] for c in range(nt))
    lax.fori_loop(0, seg_len, lambda i, c: scan_step(i, c, True), ent)

    hcat = jnp.concatenate([h_ref[c].astype(BF) for c in range(nt)], axis=1)
    for t0 in range(0, T, 2):
        cs = slice(t0 * LANES, (t0 + 2) * LANES)
        y = (jnp.dot(u, kin_s[:, cs], preferred_element_type=F32)
             + jnp.dot(hcat, cm_s[:, cs], preferred_element_type=F32))
        g = _gelu_tanh(y)
        for t in (t0, t0 + 1):
            z_ref[pl.ds(t, n_lat, stride=T), :] = g[n_ctx:n_ctx + n_lat, lanes(t - t0)]

    uc = u_s[0:n_ctx, :]
    hc = jnp.concatenate([h_ref[c, 0:n_ctx, :].astype(BF) for c in range(2 * tpq)]
                         + [h_ref[c, rows - n_ctx:rows, :].astype(BF) for c in range(2 * tpq, nt)], axis=1)
    gc = _gelu_tanh(jnp.dot(uc, kin_s[...], preferred_element_type=F32)
                    + jnp.dot(hc, cm_s[...], preferred_element_type=F32))
    for t in range(T):
        zc_ref[pl.ds(t, n_ctx, stride=T), :] = gc[:, lanes(t)]


def _ssm_weights(a_re, a_im, log_dt, b_re, b_im, c_re, c_im, d_skip, seg_len):
    T = SSM_T
    g, p = a_re.shape[1], a_re.shape[2]
    hc = b_re.shape[-1]
    gpt = LANES // hc
    J = g // gpt
    lam = lax.complex(a_re.astype(F32), a_im.astype(F32))
    dt = jnp.exp(log_dt.astype(F32))[..., None]
    ldt = lam * dt
    lam_bar = jnp.exp(ldt)
    bbar = ((lam_bar - 1) / lam)[..., None] * lax.complex(b_re.astype(F32), b_im.astype(F32))
    cmat = lax.complex(c_re.astype(F32), c_im.astype(F32))
    kk = jnp.arange(T + 1, dtype=F32)
    pw = jnp.exp(ldt[:, None] * kk[None, :, None, None])

    wf = pw[0, T - 1 - jnp.arange(T)][:, :, None, :] * jnp.swapaxes(bbar[0], 1, 2)[None]
    wb = pw[1, jnp.arange(T)][:, :, None, :] * jnp.swapaxes(bbar[1], 1, 2)[None]
    wq = jnp.stack([wf.real, wf.imag, wb.real, wb.imag]).reshape(4, T, J, gpt, hc, p)
    wc = wq.transpose(2, 1, 4, 0, 3, 5).reshape(J, T * hc, 4 * gpt * p)

    df = cmat[0][None] * pw[0, 1 + jnp.arange(T)][:, :, None, :]
    db = cmat[1][None] * pw[1, T - jnp.arange(T)][:, :, None, :]
    dq = jnp.stack([df.real, -df.imag, db.real, -db.imag]).reshape(4, T, J, gpt, hc, p)
    dc = dq.transpose(2, 0, 3, 5, 1, 4).reshape(J, 4 * gpt * p, T * hc)

    mf = jnp.einsum('gcp,kgp,gph->kgch', cmat[0], pw[0, :T], bbar[0]).real
    mb = jnp.einsum('gcp,kgp,gph->kgch', cmat[1], pw[1, :T], bbar[1]).real
    skip = d_skip.astype(F32).reshape(g, hc)[:, :, None] * jnp.eye(hc, dtype=F32)[None]
    m0 = mf[0] + mb[0] + skip
    lags = jnp.concatenate([mb[1:][::-1], m0[None], mf[1:]], axis=0)
    tin = np.arange(T)[:, None]
    tout = np.arange(T)[None, :]
    kt = lags[jnp.asarray(tout - tin + T - 1)]
    kt = kt.reshape(T, T, J, gpt, hc, hc)
    kc = kt.transpose(2, 0, 5, 1, 3, 4).reshape(J, T * hc, T * LANES)

    def lam_rows(power):
        lp = jnp.exp(ldt * power)
        lt = lp.reshape(2, J, gpt * p)
        return jnp.concatenate([lt[0].real, lt[0].imag, lt[1].real, lt[1].imag], axis=-1)[:, None, :]

    lam_t = jnp.broadcast_to(lam_rows(float(T)), (J, SSM_SEG, 4 * gpt * p))
    return wc.astype(BF), kc.astype(BF), dc.astype(BF), lam_t, lam_rows(float(T * seg_len))


def _ssm_expanders():
    T = SSM_T
    r = np.arange(T * LANES)
    rexp = np.zeros((T * LANES, T * SSM_GROUP), np.float32)
    rexp[r, (r // LANES) * SSM_GROUP + r % SSM_GROUP] = 1.0
    return jnp.asarray(rexp, BF), jnp.asarray(rexp.T, BF)


def _ssm(u3, u3c, wc, kc, dc, lam, lamseg, *, batch, seq, n_ctx):
    T = SSM_T
    J = u3.shape[0]
    tw = T * LANES
    ns = wc.shape[-1]
    n_lat, n_c = seq // T, n_ctx // T
    rows = n_lat + 2 * n_c
    seg_len = rows // SSM_SEG
    assert rows % SSM_SEG == 0 and n_c % 16 == 0
    rexp, cexp = _ssm_expanders()
    kern = functools.partial(_ssm_kernel, seg_len=seg_len, n_lat=n_lat, n_ctx=n_c)
    const = lambda j, b: (0, 0)
    per_j = lambda j, b: (j, 0, 0)
    per_jb = lambda j, b: (j, b, 0)
    return pl.pallas_call(
        kern,
        grid=(J, batch),
        in_specs=[pl.BlockSpec((None, seq, LANES), per_jb, pipeline_mode=pl.Buffered(1)),
                  pl.BlockSpec((None, n_ctx, LANES), per_jb),
                  pl.BlockSpec((None,) + wc.shape[1:], per_j),
                  pl.BlockSpec((None,) + kc.shape[1:], per_j),
                  pl.BlockSpec((None,) + dc.shape[1:], per_j),
                  pl.BlockSpec(rexp.shape, const),
                  pl.BlockSpec(cexp.shape, const),
                  pl.BlockSpec((None, SSM_SEG, ns), per_j),
                  pl.BlockSpec((None, 1, ns), per_j)],
        out_specs=[pl.BlockSpec((None, seq, LANES), per_jb),
                   pl.BlockSpec((None, n_ctx, LANES), per_jb)],
        out_shape=[jax.ShapeDtypeStruct(u3.shape, F32),
                   jax.ShapeDtypeStruct(u3c.shape, F32)],
        scratch_shapes=[pltpu.VMEM((rows, tw), BF),
                        pltpu.VMEM((tw, ns), BF),
                        pltpu.VMEM((tw, tw), BF),
                        pltpu.VMEM((ns, tw), BF),
                        pltpu.VMEM((ns // LANES, rows, LANES), F32),
                        pltpu.VMEM((ns // LANES, rows, LANES), F32),
                        pltpu.VMEM((ns // LANES, 2 * SSM_SEG, LANES), F32)],
        compiler_params=_cparams("arbitrary", "arbitrary"),
        name="ssm_chunk_scan",
    )(u3, u3c, wc, kc, dc, rexp, cexp, lam, lamseg)


def _dft_a_kernel(w_ref, x_ref, o_ref):
    o_ref[...] = jnp.dot(w_ref[...], x_ref[...], preferred_element_type=F32).astype(BF)


def _dft_rows(n):
    k = np.arange(n)
    m = (k[:, None] * k[None, :]) % n
    ang = jnp.asarray(m, F32) * (2.0 * math.pi / n)
    return jnp.concatenate([jnp.cos(ang), -jnp.sin(ang)], axis=0)


def _dft_a(w, x3):
    B, n, cols = x3.shape
    tn = min(cols, 4096)
    return pl.pallas_call(
        _dft_a_kernel,
        grid=(B, cols // tn),
        in_specs=[pl.BlockSpec((2 * n, n), lambda b, i: (0, 0)),
                  pl.BlockSpec((None, n, tn), lambda b, i: (b, 0, i))],
        out_specs=pl.BlockSpec((None, 2 * n, tn), lambda b, i: (b, 0, i)),
        out_shape=jax.ShapeDtypeStruct((B, 2 * n, cols), BF),
        compiler_params=_cparams("parallel", "parallel"),
        name="dft_stage_a",
    )(w, x3)


def _dft_b_kernel(a_ref, g_ref, o_ref, *, kb, l1, wf):
    for i in range(kb):
        g = jnp.concatenate([g_ref[0, i], g_ref[1, i]], axis=0)
        p = jnp.dot(a_ref[i], g, preferred_element_type=F32)
        o_ref[:, i * 2 * wf:i * 2 * wf + wf] = p[:l1].astype(BF)
        o_ref[:, i * 2 * wf + wf:(i + 1) * 2 * wf] = p[l1:].astype(BF)


def _dft_b_mats(seq, l1, l2):
    k2 = np.arange(l2)[:, None, None]
    k1 = np.arange(l1)[None, :, None]
    j1 = np.arange(l1)[None, None, :]
    m = ((k1 * l2 + k2) * j1) % seq
    ang = jnp.asarray(m, F32) * (2.0 * math.pi / seq)
    ar, ai = jnp.cos(ang), -jnp.sin(ang)
    top = jnp.concatenate([ar, -ai], axis=2)
    bot = jnp.concatenate([ai, ar], axis=2)
    return jnp.concatenate([top, bot], axis=1)


def _dft_b(amat, g5, *, kb):
    B, _, l2, l1, wf = g5.shape
    kern = functools.partial(_dft_b_kernel, kb=kb, l1=l1, wf=wf)
    return pl.pallas_call(
        kern,
        grid=(B, l2 // kb),
        in_specs=[pl.BlockSpec((kb, 2 * l1, 2 * l1), lambda b, i: (i, 0, 0)),
                  pl.BlockSpec((None, 2, kb, l1, wf), lambda b, i: (b, 0, i, 0, 0))],
        out_specs=pl.BlockSpec((None, l1, kb * 2 * wf), lambda b, i: (b, 0, i)),
        out_shape=jax.ShapeDtypeStruct((B, l1, l2 * 2 * wf), BF),
        compiler_params=_cparams("parallel", "parallel"),
        name="dft_stage_b",
    )(amat, g5)


def _dft_ctx_kernel(w_ref, x_ref, o_ref, *, n, wf):
    p = jnp.dot(w_ref[...], x_ref[...], preferred_element_type=F32)
    o_ref[:, 0:wf] = p[:n].astype(BF)
    o_ref[:, wf:2 * wf] = p[n:].astype(BF)


def _dft_ctx(w, x3):
    B, n, wf = x3.shape
    kern = functools.partial(_dft_ctx_kernel, n=n, wf=wf)
    return pl.pallas_call(
        kern,
        grid=(B,),
        in_specs=[pl.BlockSpec((2 * n, n), lambda b: (0, 0)),
                  pl.BlockSpec((None, n, wf), lambda b: (b, 0, 0))],
        out_specs=pl.BlockSpec((None, n, 2 * wf), lambda b: (b, 0, 0)),
        out_shape=jax.ShapeDtypeStruct((B, n, 2 * wf), BF),
        compiler_params=_cparams("parallel"),
        name="dft_ctx",
    )(w, x3)


def _channel_dft(wf, seq):
    c = np.arange(wf)
    same = (c[:, None] // FNET_GROUP) == (c[None, :] // FNET_GROUP)
    m = ((c[:, None] % FNET_GROUP) * (c[None, :] % FNET_GROUP)) % FNET_GROUP
    ang = jnp.asarray(m, F32) * (2.0 * math.pi / FNET_GROUP)
    norm = 1.0 / math.sqrt(seq * FNET_GROUP)
    mask = jnp.asarray(same, F32) * norm
    return jnp.concatenate([jnp.cos(ang) * mask, jnp.sin(ang) * mask], axis=0)


def _softmax_pv(parts, l_shape):
    m = parts[0][0].max(axis=-1, keepdims=True)
    for s, _ in parts[1:]:
        m = jnp.maximum(m, s.max(axis=-1, keepdims=True))
    l = jnp.zeros(l_shape, F32)
    o = None
    for s, v in parts:
        p = jnp.exp(s - m)
        l = l + p.sum(axis=-1, keepdims=True)
        pv = jnp.dot(p.astype(BF), v, preferred_element_type=F32)
        o = pv if o is None else o + pv
    return o / l


def _na_kernel(kb_ref, pid_ref, q_ref, k_ref, v_ref, kc_ref, vc_ref, bias_ref, o_ref, *, nkeys, heads):
    del pid_ref
    i = pl.program_id(1)
    start = pl.multiple_of(kb_ref[i] * GRID_W, GRID_W)
    nq = q_ref.shape[0]
    lane = lax.broadcasted_iota(jnp.int32, (nq, LANES), 1)
    low = lane < HEAD_DIM
    dn = (((1,), (1,)), ((), ()))
    for hp in range(heads // 2):
        sl = slice(hp * LANES, (hp + 1) * LANES)
        q2 = q_ref[:, sl]
        k2 = k_ref[pl.ds(start, nkeys), sl]
        v2 = v_ref[pl.ds(start, nkeys), sl]
        kc2 = kc_ref[:, sl]
        vc2 = vc_ref[:, sl]
        outs = []
        for hh in range(2):
            qm = jnp.where(low if hh == 0 else jnp.logical_not(low), q2, jnp.zeros_like(q2))
            s_w = lax.dot_general(qm, k2, dn, preferred_element_type=F32) + bias_ref[2 * hp + hh]
            s_c = lax.dot_general(qm, kc2, dn, preferred_element_type=F32)
            outs.append(_softmax_pv([(s_w, v2), (s_c, vc2)], (nq, 1)))
        o_ref[:, sl] = jnp.where(low, outs[0], outs[1]).astype(BF)


def _na_plan(rows, rq):
    wr = min(NA_ROWS, rows)
    kr = rq + wr - 1
    nblk = rows // rq
    kbs, pids, pats = [], [], []
    for blk in range(nblk):
        r0 = blk * rq
        rs = [int(np.clip(r0 + i - wr // 2, 0, rows - wr)) for i in range(rq)]
        kb = min(rs[0], rows - kr)
        pat = (tuple(r - kb for r in rs), r0 - kb)
        if pat not in pats:
            pats.append(pat)
        kbs.append(kb)
        pids.append(pats.index(pat))
    return np.asarray(kbs, np.int32), np.asarray(pids, np.int32), pats, kr, wr


def _na_bias(rpb, pats, rq, kr, wr):
    cols = np.arange(GRID_W)
    cstart = np.clip(cols - NA_COLS // 2, 0, GRID_W - NA_COLS)
    kc = np.arange(GRID_W)
    col_ok = (kc[None, :] >= cstart[:, None]) & (kc[None, :] < cstart[:, None] + NA_COLS)
    dc = kc[None, :] - cols[:, None] + NA_COLS - 1
    heads, nr, ncol = rpb.shape
    sel = (dc[:, :, None] == np.arange(ncol)[None, None, :]) & col_ok[:, :, None]
    cb = jnp.einsum('hab,ckb->hack', rpb.astype(F32), jnp.asarray(sel, F32), precision=lax.Precision.HIGHEST)
    cb = jnp.where(jnp.asarray(col_ok)[None, None], cb, NEG)
    cbx = jnp.concatenate([cb, jnp.full((heads, 1, GRID_W, GRID_W), NEG, F32)], axis=1)
    tabs = []
    for rs_off, r_off in pats:
        i = np.arange(rq)[:, None]
        w = np.arange(kr)[None, :]
        rso = np.asarray(rs_off)[:, None]
        row_ok = (w >= rso) & (w < rso + wr)
        dr = np.where(row_ok, w - r_off - i + NA_ROWS - 1, nr)
        b = jnp.take(cbx, jnp.asarray(dr.reshape(-1), jnp.int32), axis=1)
        b = b.reshape(heads, rq, kr, GRID_W, GRID_W).transpose(0, 1, 3, 2, 4)
        tabs.append(b.reshape(heads, rq * GRID_W, kr * GRID_W))
    return jnp.stack(tabs)


def _na_latent(q, k, v, kc, vc, rpb, *, batch, seq, n_ctx, rq):
    rows = seq // GRID_W
    heads = rpb.shape[0]
    kbs, pids, pats, kr, wr = _na_plan(rows, rq)
    bias = _na_bias(rpb, pats, rq, kr, wr)
    nblk = rows // rq
    nq = rq * GRID_W
    nkeys = kr * GRID_W
    w = q.shape[1]
    kern = functools.partial(_na_kernel, nkeys=nkeys, heads=heads)
    grid_spec = pltpu.PrefetchScalarGridSpec(
        num_scalar_prefetch=2,
        grid=(batch, nblk),
        in_specs=[pl.BlockSpec((nq, w), lambda b, i, kb, pid: (b * nblk + i, 0)),
                  pl.BlockSpec((seq, w), lambda b, i, kb, pid: (b, 0)),
                  pl.BlockSpec((seq, w), lambda b, i, kb, pid: (b, 0)),
                  pl.BlockSpec((n_ctx, w), lambda b, i, kb, pid: (b, 0)),
                  pl.BlockSpec((n_ctx, w), lambda b, i, kb, pid: (b, 0)),
                  pl.BlockSpec((None, heads, nq, nkeys), lambda b, i, kb, pid: (pid[i], 0, 0, 0))],
        out_specs=pl.BlockSpec((nq, w), lambda b, i, kb, pid: (b * nblk + i, 0)),
    )
    return pl.pallas_call(
        kern,
        grid_spec=grid_spec,
        out_shape=jax.ShapeDtypeStruct(q.shape, BF),
        compiler_params=_cparams("arbitrary", "arbitrary"),
        name="na_latent",
    )(jnp.asarray(kbs), jnp.asarray(pids), q, k, v, kc, vc, bias)


def _na_ctx_kernel(q_ref, k_ref, v_ref, o_ref, *, heads):
    nq = q_ref.shape[0]
    lane = lax.broadcasted_iota(jnp.int32, (nq, LANES), 1)
    low = lane < HEAD_DIM
    dn = (((1,), (1,)), ((), ()))
    for hp in range(heads // 2):
        sl = slice(hp * LANES, (hp + 1) * LANES)
        q2 = q_ref[:, sl]
        k2 = k_ref[:, sl]
        v2 = v_ref[:, sl]
        outs = []
        for hh in range(2):
            qm = jnp.where(low if hh == 0 else jnp.logical_not(low), q2, jnp.zeros_like(q2))
            s = lax.dot_general(qm, k2, dn, preferred_element_type=F32)
            outs.append(_softmax_pv([(s, v2)], (nq, 1)))
        o_ref[:, sl] = jnp.where(low, outs[0], outs[1]).astype(BF)


def _na_ctx(qc, kc, vc, *, batch, n_ctx, heads):
    w = qc.shape[1]
    spec = pl.BlockSpec((n_ctx, w), lambda b: (b, 0))
    return pl.pallas_call(
        functools.partial(_na_ctx_kernel, heads=heads),
        grid=(batch,),
        in_specs=[spec, spec, spec],
        out_specs=spec,
        out_shape=jax.ShapeDtypeStruct(qc.shape, BF),
        compiler_params=_cparams("parallel"),
        name="na_ctx",
    )(qc, kc, vc)


def _outproj_kernel(z_ref, pf_ref, na_ref, x_ref, gate_ref, gpost_ref,
                    wglu_ref, cs_ref, wfo_ref, wout_ref, o_ref, *, widths):
    w_ssm, w_fn, w_na = widths
    z = jnp.concatenate([z_ref[j] for j in range(w_ssm // LANES)], axis=1)
    gl = jnp.dot(z.astype(BF), wglu_ref[...], preferred_element_type=F32)
    y_ssm = (z * jax.nn.sigmoid(gl)).astype(BF)
    mixed = jnp.dot(pf_ref[...], cs_ref[...], preferred_element_type=F32).astype(BF)
    y_fft = jnp.dot(mixed, wfo_ref[...], preferred_element_type=F32).astype(BF)
    o = (jnp.dot(y_ssm, wout_ref[0:w_ssm, :], preferred_element_type=F32)
         + jnp.dot(y_fft, wout_ref[w_ssm:w_ssm + w_fn, :], preferred_element_type=F32)
         + jnp.dot(na_ref[...], wout_ref[w_ssm + w_fn:, :], preferred_element_type=F32))
    ms = jnp.mean(o * o, axis=-1, keepdims=True)
    o_ref[...] = x_ref[...] + gate_ref[...] * (o * lax.rsqrt(ms + EPS) * gpost_ref[...])


def _outproj(z3, pf, yna, x2d, mod3, g_post, w_glu, cs, w_fo, w_out, *,
             tm, tiles_per_batch, mod_row, widths):
    n, d = x2d.shape
    w_ssm, w_fn, w_na = widths
    if mod_row is None:
        row = lambda i: i // tiles_per_batch
    else:
        row = lambda i: mod_row
    kern = functools.partial(_outproj_kernel, widths=widths)
    return pl.pallas_call(
        kern,
        grid=(n // tm,),
        in_specs=[pl.BlockSpec((w_ssm // LANES, tm, LANES), lambda i: (0, i, 0)),
                  pl.BlockSpec((tm, 2 * w_fn), lambda i: (i, 0)),
                  pl.BlockSpec((tm, w_na), lambda i: (i, 0)),
                  pl.BlockSpec((tm, d), lambda i: (i, 0)),
                  pl.BlockSpec((None, 1, d), lambda i: (row(i), 0, 2)),
                  pl.BlockSpec((1, d), lambda i: (0, 0)),
                  _resident(w_glu.shape, lambda i: (0, 0)),
                  _resident(cs.shape, lambda i: (0, 0)),
                  _resident(w_fo.shape, lambda i: (0, 0)),
                  _resident(w_out.shape, lambda i: (0, 0))],
        out_specs=pl.BlockSpec((tm, d), lambda i: (i, 0)),
        out_shape=jax.ShapeDtypeStruct((n, d), F32),
        compiler_params=_cparams("parallel"),
        name="outproj",
    )(z3, pf, yna, x2d, mod3, g_post.reshape(1, d), w_glu, cs, w_fo, w_out)


def _ffn_kernel(x_ref, sh_ref, sc_ref, gate_ref, gpre_ref, gpost_ref, wg_ref, wu_ref, wd_ref,
                o_ref, a_ref, *, nchunk):
    x = x_ref[...]
    ms = jnp.mean(x * x, axis=-1, keepdims=True)
    m = ((x * lax.rsqrt(ms + EPS) * gpre_ref[...]) * (1.0 + sc_ref[...]) + sh_ref[...]).astype(BF)
    fc = wg_ref.shape[1] // nchunk
    for c in range(nchunk):
        sl = slice(c * fc, (c + 1) * fc)
        g = jnp.dot(m, wg_ref[:, sl], preferred_element_type=F32)
        u = jnp.dot(m, wu_ref[:, sl], preferred_element_type=F32)
        a_ref[:, sl] = (g * jax.nn.sigmoid(g) * u).astype(BF)
    y = jnp.dot(a_ref[...], wd_ref[...], preferred_element_type=F32)
    ms2 = jnp.mean(y * y, axis=-1, keepdims=True)
    o_ref[...] = x + gate_ref[...] * (y * lax.rsqrt(ms2 + EPS) * gpost_ref[...])


def _ffn(x2d, mod3, g_pre, g_post, wg, wu, wd, *, tm, tiles_per_batch, mod_row):
    n, d = x2d.shape
    dff = wg.shape[1]
    if mod_row is None:
        row = lambda i: i // tiles_per_batch
    else:
        row = lambda i: mod_row
    nchunk = 2 if dff % (2 * LANES) == 0 else 1
    return pl.pallas_call(
        functools.partial(_ffn_kernel, nchunk=nchunk),
        grid=(n // tm,),
        in_specs=[pl.BlockSpec((tm, d), lambda i: (i, 0)),
                  pl.BlockSpec((None, 1, d), lambda i: (row(i), 0, 3)),
                  pl.BlockSpec((None, 1, d), lambda i: (row(i), 0, 4)),
                  pl.BlockSpec((None, 1, d), lambda i: (row(i), 0, 5)),
                  pl.BlockSpec((1, d), lambda i: (0, 0)),
                  pl.BlockSpec((1, d), lambda i: (0, 0)),
                  _resident(wg.shape, lambda i: (0, 0)),
                  _resident(wu.shape, lambda i: (0, 0)),
                  _resident(wd.shape, lambda i: (0, 0))],
        out_specs=pl.BlockSpec((tm, d), lambda i: (i, 0)),
        out_shape=jax.ShapeDtypeStruct((n, d), F32),
        scratch_shapes=[pltpu.VMEM((tm, dff), BF)],
        compiler_params=_cparams("parallel"),
        name="ffn",
    )(x2d, mod3, mod3, mod3, g_pre.reshape(1, d), g_post.reshape(1, d), wg, wu, wd)


def _layer(x2d, xc2d, cstack, p, *, batch, seq, n_ctx, last, rope_tabs, consts):
    d = x2d.shape[1]
    w_ssm = p["ssm_d"].shape[0]
    w_fn = p["w_fourier"].shape[0]
    w_na = (p["w_in"].shape[1] - w_ssm - w_fn) // 3
    widths = (w_ssm, w_fn, w_na)
    heads = w_na // HEAD_DIM
    T = SSM_T
    tm = min(512, seq)
    tmc = min(512, batch * n_ctx)

    mod = _mod_rows(cstack, p["w_mod_all"], p["b_mod_all"], p["layer"])
    mod3 = mod.reshape(mod.shape[0], 1, 6 * d)

    w_in = p["w_in"].astype(BF)
    cos, sin = rope_tabs
    u3, f, q, k, v = _inproj(x2d, mod3, p["g_pre_mix"], w_in, cos, sin, tm=tm,
                             tiles_per_batch=seq // tm, mod_row=None, widths=widths, rope=True)
    ones = jnp.ones((tmc, LANES), F32)
    u3c, fc, qc, kc, vc = _inproj(xc2d, mod3, p["g_pre_mix"], w_in, ones, ones, tm=tmc,
                                  tiles_per_batch=1, mod_row=batch, widths=widths, rope=False)

    seg_len = (seq + 2 * n_ctx) // T // SSM_SEG
    wc, kcm, dcm, lam, lamseg = _ssm_weights(p["ssm_a_re"], p["ssm_a_im"], p["ssm_log_dt"], p["ssm_b_re"],
                                             p["ssm_b_im"], p["ssm_c_re"], p["ssm_c_im"], p["ssm_d"], seg_len)
    z3, zc3 = _ssm(u3, u3c, wc, kcm, dcm, lam, lamseg, batch=batch, seq=seq, n_ctx=n_ctx)

    l2 = consts["l2"]
    l1 = seq // l2
    g = _dft_a(consts["dft_a"], f.reshape(batch, l2, l1 * w_fn))
    pf = _dft_b(consts["dft_b"], g.reshape(batch, 2, l2, l1, w_fn), kb=min(8, l2))
    pf = pf.reshape(batch * seq, 2 * w_fn)

    yna = _na_latent(q, k, v, kc, vc, p["na_rpb"], batch=batch, seq=seq, n_ctx=n_ctx, rq=consts["rq"])

    w_glu = p["w_glu"].astype(BF)
    w_fo = p["w_fourier"].astype(BF)
    w_out = p["w_out"].astype(BF)
    x2d = _outproj(z3, pf, yna, x2d, mod3, p["g_post_mix"], w_glu, consts["cs_lat"], w_fo, w_out,
                   tm=tm, tiles_per_batch=seq // tm, mod_row=None, widths=widths)
    wg = p["w_ffn_gate"].astype(BF)
    wu = p["w_ffn_up"].astype(BF)
    wd = p["w_ffn_down"].astype(BF)
    x2d = _ffn(x2d, mod3, p["g_pre_ffn"], p["g_post_ffn"], wg, wu, wd,
               tm=tm, tiles_per_batch=seq // tm, mod_row=None)

    if not last:
        pfc = _dft_ctx(consts["dft_ctx"], fc.reshape(batch, n_ctx, w_fn)).reshape(batch * n_ctx, 2 * w_fn)
        ynac = _na_ctx(qc, kc, vc, batch=batch, n_ctx=n_ctx, heads=heads)
        tc = min(tmc, n_ctx)
        xc2d = _outproj(zc3, pfc, ynac, xc2d, mod3, p["g_post_mix"], w_glu, consts["cs_ctx"], w_fo, w_out,
                        tm=tc, tiles_per_batch=n_ctx // tc, mod_row=batch, widths=widths)
        xc2d = _ffn(xc2d, mod3, p["g_pre_ffn"], p["g_post_ffn"], wg, wu, wd,
                    tm=tc, tiles_per_batch=n_ctx // tc, mod_row=batch)
    else:
        xc2d = None
    return x2d, xc2d


def kernel(x, c, ctx, c_ctx, w_mod, b_mod, g_pre_mix, g_post_mix, w_in, ssm_a_re, ssm_a_im, ssm_log_dt,
           ssm_b_re, ssm_b_im, ssm_c_re, ssm_c_im, ssm_d, w_glu, w_fourier, na_rpb, w_out, g_pre_ffn,
           g_post_ffn, w_ffn_gate, w_ffn_up, w_ffn_down):
    batch, seq, d = x.shape
    n_ctx = ctx.shape[1]
    depth = w_mod.shape[0]
    w_fn = w_fourier.shape[1]
    params = dict(g_pre_mix=g_pre_mix, g_post_mix=g_post_mix, w_in=w_in,
                  ssm_a_re=ssm_a_re, ssm_a_im=ssm_a_im, ssm_log_dt=ssm_log_dt, ssm_b_re=ssm_b_re,
                  ssm_b_im=ssm_b_im, ssm_c_re=ssm_c_re, ssm_c_im=ssm_c_im, ssm_d=ssm_d, w_glu=w_glu,
                  w_fourier=w_fourier, na_rpb=na_rpb, w_out=w_out, g_pre_ffn=g_pre_ffn,
                  g_post_ffn=g_post_ffn, w_ffn_gate=w_ffn_gate, w_ffn_up=w_ffn_up, w_ffn_down=w_ffn_down)

    nrow = -(-(batch + 1) // 8) * 8
    cstack = jnp.concatenate([c, c_ctx[None, :], jnp.zeros((nrow - batch - 1, d), c.dtype)], axis=0)

    l2 = 64 if seq % (64 * 8) == 0 else 8
    l1 = seq // l2
    consts = dict(
        l2=l2,
        rq=4,
        dft_a=_dft_rows(l2).astype(BF),
        dft_b=_dft_b_mats(seq, l1, l2).astype(BF),
        dft_ctx=_dft_rows(n_ctx).astype(BF),
        cs_lat=_channel_dft(w_fn, seq).astype(BF),
        cs_ctx=_channel_dft(w_fn, n_ctx).astype(BF),
    )
    rope_tabs = _rope_tables(seq)

    x2d = x.reshape(batch * seq, d)
    xc2d = ctx.reshape(batch * n_ctx, d)
    for layer in range(depth):
        p = {name: val[layer] for name, val in params.items()}
        p.update(w_mod_all=w_mod, b_mod_all=b_mod, layer=layer)
        x2d, xc2d = _layer(x2d, xc2d, cstack, p, batch=batch, seq=seq, n_ctx=n_ctx,
                           last=(layer == depth - 1), rope_tabs=rope_tabs, consts=consts)
    return x2d.reshape(batch, seq, d)
```

```python
import functools
import math

import numpy as np
import jax
import jax.numpy as jnp
from jax import lax
from jax.experimental import pallas as pl
from jax.experimental.pallas import tpu as pltpu

BF = jnp.bfloat16
F32 = jnp.float32

EPS = 1e-6
GRID_W = 64
HEAD_DIM = 64
NA_ROWS = 8
NA_COLS = 16
ROPE_BASE = 10000.0
SSM_GROUP = 16
SSM_STATE = 64
FNET_GROUP = 64
LANES = 128
SSM_T = 8
SSM_SEG = 8
NEG = -1e30
VMEM_LIMIT = 56 * 1024 * 1024


def _cparams(*sem):
    return pltpu.CompilerParams(dimension_semantics=sem, vmem_limit_bytes=VMEM_LIMIT)


def _resident(shape, index_map):
    return pl.BlockSpec(shape, index_map, pipeline_mode=pl.Buffered(1))


def _mod_kernel(c_ref, w_ref, b_ref, o_ref):
    cs = c_ref[...]
    s = (cs * jax.nn.sigmoid(cs)).astype(BF)
    o_ref[...] = jnp.dot(s, w_ref[...].astype(BF), preferred_element_type=F32) + b_ref[...]


def _mod_rows(cstack, w_mod, b_mod, layer):
    rows, d = cstack.shape
    depth, _, n = w_mod.shape
    return pl.pallas_call(
        _mod_kernel,
        grid=(n // d,),
        in_specs=[pl.BlockSpec((rows, d), lambda i: (0, 0)),
                  pl.BlockSpec((None, d, d), lambda i: (layer, 0, i)),
                  pl.BlockSpec((None, 1, d), lambda i: (layer, 0, i))],
        out_specs=pl.BlockSpec((rows, d), lambda i: (0, i)),
        out_shape=jax.ShapeDtypeStruct((rows, n), F32),
        compiler_params=_cparams("arbitrary"),
        name="mod_rows",
    )(cstack, w_mod, b_mod.reshape(depth, 1, n))


def _rope_tile(t, cos, sin, first):
    partner = jnp.where(first, pltpu.roll(t, LANES - 16, 1), pltpu.roll(t, 16, 1))
    return t * cos + partner * sin


def _inproj_kernel(x_ref, sh_ref, sc_ref, g_ref, w_ref, cos_ref, sin_ref,
                   u_ref, f_ref, q_ref, k_ref, v_ref, *, widths, rope):
    x = x_ref[...]
    ms = jnp.mean(x * x, axis=-1, keepdims=True)
    xn = x * lax.rsqrt(ms + EPS) * g_ref[...]
    m = (xn * (1.0 + sc_ref[...]) + sh_ref[...]).astype(BF)
    w_ssm, w_fn, w_na = widths

    def proj(lo, n):
        return jnp.dot(m, w_ref[:, lo:lo + n], preferred_element_type=F32)

    u = proj(0, w_ssm)
    for j in range(w_ssm // LANES):
        u_ref[j] = u[:, j * LANES:(j + 1) * LANES]
    f_ref[...] = proj(w_ssm, w_fn).astype(BF)
    q = proj(w_ssm + w_fn, w_na)
    k = proj(w_ssm + w_fn + w_na, w_na)
    v_ref[...] = proj(w_ssm + w_fn + 2 * w_na, w_na).astype(BF)
    scale = HEAD_DIM ** -0.5 * math.log2(math.e)
    if rope:
        cos = cos_ref[...]
        sin = sin_ref[...]
        lane = lax.broadcasted_iota(jnp.int32, cos.shape, 1)
        first = (lane % 32) < 16
        for j in range(w_na // LANES):
            sl = slice(j * LANES, (j + 1) * LANES)
            q_ref[:, sl] = (_rope_tile(q[:, sl], cos, sin, first) * scale).astype(BF)
            k_ref[:, sl] = _rope_tile(k[:, sl], cos, sin, first).astype(BF)
    else:
        q_ref[...] = (q * scale).astype(BF)
        k_ref[...] = k.astype(BF)


def _inproj(x2d, mod3, g_pre, w_in, cos, sin, *, tm, tiles_per_batch, mod_row, widths, rope):
    n, d = x2d.shape
    w_ssm, w_fn, w_na = widths
    nt = n // tm
    if mod_row is None:
        row = lambda i: i // tiles_per_batch
    else:
        row = lambda i: mod_row
    pos = lambda i: (i % tiles_per_batch, 0)
    kern = functools.partial(_inproj_kernel, widths=widths, rope=rope)
    return pl.pallas_call(
        kern,
        grid=(nt,),
        in_specs=[pl.BlockSpec((tm, d), lambda i: (i, 0)),
                  pl.BlockSpec((None, 1, d), lambda i: (row(i), 0, 0)),
                  pl.BlockSpec((None, 1, d), lambda i: (row(i), 0, 1)),
                  pl.BlockSpec((1, d), lambda i: (0, 0)),
                  _resident(w_in.shape, lambda i: (0, 0)),
                  pl.BlockSpec((tm, LANES), pos),
                  pl.BlockSpec((tm, LANES), pos)],
        out_specs=[pl.BlockSpec((w_ssm // LANES, tm, LANES), lambda i: (0, i, 0)),
                   pl.BlockSpec((tm, w_fn), lambda i: (i, 0)),
                   pl.BlockSpec((tm, w_na), lambda i: (i, 0)),
                   pl.BlockSpec((tm, w_na), lambda i: (i, 0)),
                   pl.BlockSpec((tm, w_na), lambda i: (i, 0))],
        out_shape=[jax.ShapeDtypeStruct((w_ssm // LANES, n, LANES), F32),
                   jax.ShapeDtypeStruct((n, w_fn), BF),
                   jax.ShapeDtypeStruct((n, w_na), BF),
                   jax.ShapeDtypeStruct((n, w_na), BF),
                   jax.ShapeDtypeStruct((n, w_na), BF)],
        compiler_params=_cparams("parallel"),
        name="inproj_rope" if rope else "inproj_ctx",
    )(x2d, mod3, mod3, g_pre.reshape(1, d), w_in, cos, sin)


def _rope_tables(seq):
    t = np.arange(seq)
    row = (t // GRID_W).astype(np.float32)
    col = (t % GRID_W).astype(np.float32)
    quarter = HEAD_DIM // 4
    freqs = (np.float32(ROPE_BASE) ** (-np.arange(quarter, dtype=np.float32) / quarter)).astype(np.float32)
    d = np.arange(LANES) % HEAD_DIM
    use_col = (d // (HEAD_DIM // 2)) == 1
    fidx = d % quarter
    sign = np.where((d % (HEAD_DIM // 2)) // quarter == 0, -1.0, 1.0)
    pos = np.where(use_col[None, :], col[:, None], row[:, None])
    ang = (pos * freqs[fidx][None, :]).astype(np.float32).astype(np.float64)
    return jnp.asarray(np.cos(ang), F32), jnp.asarray(np.sin(ang) * sign[None, :], F32)


def _gelu_tanh(x):
    c = math.sqrt(2.0 / math.pi)
    return x * (0.5 * (1.0 + jnp.tanh(c * (x + 0.044715 * (x * x * x)))))


def _group_of(idx, width, groups):
    shift = width.bit_length() - 1
    assert width == 1 << shift and groups & (groups - 1) == 0
    return lax.bitwise_and(lax.shift_right_logical(idx, shift), groups - 1)


def _same_group(shape, row0, row_width, col0, col_width, groups):
    r = lax.broadcasted_iota(jnp.int32, shape, 0) + row0
    c = lax.broadcasted_iota(jnp.int32, shape, 1) + col0
    return _group_of(r, row_width, groups) == _group_of(c, col_width, groups)


def _ssm_kernel(ul_ref, uc_ref, wc_ref, kc_ref, dc_ref, rexp_ref, cexp_ref, lam_ref, lamseg_ref,
                z_ref, zc_ref, u_s, bm_s, kin_s, cm_s, s_ref, h_ref, e_ref, *, seg_len, n_lat, n_ctx):
    T = SSM_T
    rows, tw = u_s.shape
    nt = s_ref.shape[0]
    tpq = nt // 4
    nst = tpq * LANES
    gpt = LANES // SSM_GROUP
    cw = 512

    def lanes(c):
        return slice(c * LANES, (c + 1) * LANES)

    @pl.when(pl.program_id(1) == 0)
    def _expand_operators():
        rexp = rexp_ref[...]
        for c0 in range(0, 4 * nst, cw):
            blk = jnp.dot(rexp, wc_ref[:, c0:c0 + cw], preferred_element_type=F32)
            keep = _same_group(blk.shape, 0, SSM_GROUP, c0, SSM_STATE, gpt)
            bm_s[:, c0:c0 + cw] = jnp.where(keep, blk, 0.0).astype(BF)
        for c0 in range(0, tw, cw):
            blk = jnp.dot(rexp, kc_ref[:, c0:c0 + cw], preferred_element_type=F32)
            keep = _same_group(blk.shape, 0, SSM_GROUP, c0, SSM_GROUP, gpt)
            kin_s[:, c0:c0 + cw] = jnp.where(keep, blk, 0.0).astype(BF)
        cexp = cexp_ref[...]
        for r0 in range(0, 4 * nst, cw):
            blk = jnp.dot(dc_ref[r0:r0 + cw, :], cexp, preferred_element_type=F32)
            keep = _same_group(blk.shape, r0, SSM_STATE, 0, SSM_GROUP, gpt)
            cm_s[r0:r0 + cw, :] = jnp.where(keep, blk, 0.0).astype(BF)

    for t in range(T):
        ct = uc_ref[pl.ds(t, n_ctx, stride=T), :].astype(BF)
        u_s[0:n_ctx, lanes(t)] = ct
        u_s[n_ctx:n_ctx + n_lat, lanes(t)] = ul_ref[pl.ds(t, n_lat, stride=T), :].astype(BF)
        u_s[n_ctx + n_lat:rows, lanes(t)] = ct
    u = u_s[...]
    for q in range(4):
        r = jnp.dot(u, bm_s[:, q * nst:(q + 1) * nst], preferred_element_type=F32)
        for t in range(tpq):
            s_ref[q * tpq + t] = r[:, t * LANES:(t + 1) * LANES]

    def scan_step(i, carry, record):
        new = list(carry)
        for d, ii in ((0, i), (1, seg_len - 1 - i)):
            rsl = pl.ds(ii, SSM_SEG, stride=seg_len)
            for t in range(tpq):
                cr, ci = 2 * d * tpq + t, (2 * d + 1) * tpq + t
                hr, hi = carry[cr], carry[ci]
                if record:
                    h_ref[cr, rsl, :] = hr
                    h_ref[ci, rsl, :] = hi
                lr, li = lam_ref[:, lanes(cr)], lam_ref[:, lanes(ci)]
                new[cr] = lr * hr - li * hi + s_ref[cr, rsl, :]
                new[ci] = lr * hi + li * hr + s_ref[ci, rsl, :]
        return tuple(new)

    zero = jnp.zeros((SSM_SEG, LANES), F32)
    fin = lax.fori_loop(0, seg_len, lambda i, c: scan_step(i, c, False), (zero,) * nt)
    for c in range(nt):
        e_ref[c, 0:SSM_SEG, :] = fin[c]

    for d in range(2):
        for t in range(tpq):
            cr, ci = 2 * d * tpq + t, (2 * d + 1) * tpq + t
            lr, li = lamseg_ref[:, lanes(cr)], lamseg_ref[:, lanes(ci)]
            er = jnp.zeros((1, LANES), F32)
            ei = jnp.zeros((1, LANES), F32)
            for n in range(SSM_SEG):
                s = n if d == 0 else SSM_SEG - 1 - n
                e_ref[cr, SSM_SEG + s:SSM_SEG + s + 1, :] = er
                e_ref[ci, SSM_SEG + s:SSM_SEG + s + 1, :] = ei
                fr, fi = e_ref[cr, s:s + 1, :], e_ref[ci, s:s + 1, :]
                er, ei = lr * er - li * ei + fr, lr * ei + li * er + fi

    ent = tuple(e_ref[c, SSM_SEG:2 * SSM_SEG, :] for c in range(nt))
    lax.fori_loop(0, seg_len, lambda i, c: scan_step(i, c, True), ent)

    hcat = jnp.concatenate([h_ref[c].astype(BF) for c in range(nt)], axis=1)
    for t0 in range(0, T, 2):
        cs = slice(t0 * LANES, (t0 + 2) * LANES)
        y = (jnp.dot(u, kin_s[:, cs], preferred_element_type=F32)
             + jnp.dot(hcat, cm_s[:, cs], preferred_element_type=F32))
        g = _gelu_tanh(y)
        for t in (t0, t0 + 1):
            z_ref[pl.ds(t, n_lat, stride=T), :] = g[n_ctx:n_ctx + n_lat, lanes(t - t0)]

    uc = u_s[0:n_ctx, :]
    hc = jnp.concatenate([h_ref[c, 0:n_ctx, :].astype(BF) for c in range(2 * tpq)]
                         + [h_ref[c, rows - n_ctx:rows, :].astype(BF) for c in range(2 * tpq, nt)], axis=1)
    gc = _gelu_tanh(jnp.dot(uc, kin_s[...], preferred_element_type=F32)
                    + jnp.dot(hc, cm_s[...], preferred_element_type=F32))
    for t in range(T):
        zc_ref[pl.ds(t, n_ctx, stride=T), :] = gc[:, lanes(t)]


def _ssm_weights(a_re, a_im, log_dt, b_re, b_im, c_re, c_im, d_skip, seg_len):
    T = SSM_T
    g, p = a_re.shape[1], a_re.shape[2]
    hc = b_re.shape[-1]
    gpt = LANES // hc
    J = g // gpt
    lam = lax.complex(a_re.astype(F32), a_im.astype(F32))
    dt = jnp.exp(log_dt.astype(F32))[..., None]
    ldt = lam * dt
    lam_bar = jnp.exp(ldt)
    bbar = ((lam_bar - 1) / lam)[..., None] * lax.complex(b_re.astype(F32), b_im.astype(F32))
    cmat = lax.complex(c_re.astype(F32), c_im.astype(F32))
    kk = jnp.arange(T + 1, dtype=F32)
    pw = jnp.exp(ldt[:, None] * kk[None, :, None, None])

    wf = pw[0, T - 1 - jnp.arange(T)][:, :, None, :] * jnp.swapaxes(bbar[0], 1, 2)[None]
    wb = pw[1, jnp.arange(T)][:, :, None, :] * jnp.swapaxes(bbar[1], 1, 2)[None]
    wq = jnp.stack([wf.real, wf.imag, wb.real, wb.imag]).reshape(4, T, J, gpt, hc, p)
    wc = wq.transpose(2, 1, 4, 0, 3, 5).reshape(J, T * hc, 4 * gpt * p)

    df = cmat[0][None] * pw[0, 1 + jnp.arange(T)][:, :, None, :]
    db = cmat[1][None] * pw[1, T - jnp.arange(T)][:, :, None, :]
    dq = jnp.stack([df.real, -df.imag, db.real, -db.imag]).reshape(4, T, J, gpt, hc, p)
    dc = dq.transpose(2, 0, 3, 5, 1, 4).reshape(J, 4 * gpt * p, T * hc)

    mf = jnp.einsum('gcp,kgp,gph->kgch', cmat[0], pw[0, :T], bbar[0]).real
    mb = jnp.einsum('gcp,kgp,gph->kgch', cmat[1], pw[1, :T], bbar[1]).real
    skip = d_skip.astype(F32).reshape(g, hc)[:, :, None] * jnp.eye(hc, dtype=F32)[None]
    m0 = mf[0] + mb[0] + skip
    lags = jnp.concatenate([mb[1:][::-1], m0[None], mf[1:]], axis=0)
    tin = np.arange(T)[:, None]
    tout = np.arange(T)[None, :]
    kt = lags[jnp.asarray(tout - tin + T - 1)]
    kt = kt.reshape(T, T, J, gpt, hc, hc)
    kc = kt.transpose(2, 0, 5, 1, 3, 4).reshape(J, T * hc, T * LANES)

    def lam_rows(power):
        lp = jnp.exp(ldt * power)
        lt = lp.reshape(2, J, gpt * p)
        return jnp.concatenate([lt[0].real, lt[0].imag, lt[1].real, lt[1].imag], axis=-1)[:, None, :]

    lam_t = jnp.broadcast_to(lam_rows(float(T)), (J, SSM_SEG, 4 * gpt * p))
    return wc.astype(BF), kc.astype(BF), dc.astype(BF), lam_t, lam_rows(float(T * seg_len))


def _ssm_expanders():
    T = SSM_T
    r = np.arange(T * LANES)
    rexp = np.zeros((T * LANES, T * SSM_GROUP), np.float32)
    rexp[r, (r // LANES) * SSM_GROUP + r % SSM_GROUP] = 1.0
    return jnp.asarray(rexp, BF), jnp.asarray(rexp.T, BF)


def _ssm(u3, u3c, wc, kc, dc, lam, lamseg, *, batch, seq, n_ctx):
    T = SSM_T
    J = u3.shape[0]
    tw = T * LANES
    ns = wc.shape[-1]
    n_lat, n_c = seq // T, n_ctx // T
    rows = n_lat + 2 * n_c
    seg_len = rows // SSM_SEG
    assert rows % SSM_SEG == 0 and n_c % 16 == 0
    rexp, cexp = _ssm_expanders()
    kern = functools.partial(_ssm_kernel, seg_len=seg_len, n_lat=n_lat, n_ctx=n_c)
    const = lambda j, b: (0, 0)
    per_j = lambda j, b: (j, 0, 0)
    per_jb = lambda j, b: (j, b, 0)
    return pl.pallas_call(
        kern,
        grid=(J, batch),
        in_specs=[pl.BlockSpec((None, seq, LANES), per_jb, pipeline_mode=pl.Buffered(1)),
                  pl.BlockSpec((None, n_ctx, LANES), per_jb),
                  pl.BlockSpec((None,) + wc.shape[1:], per_j),
                  pl.BlockSpec((None,) + kc.shape[1:], per_j),
                  pl.BlockSpec((None,) + dc.shape[1:], per_j),
                  pl.BlockSpec(rexp.shape, const),
                  pl.BlockSpec(cexp.shape, const),
                  pl.BlockSpec((None, SSM_SEG, ns), per_j),
                  pl.BlockSpec((None, 1, ns), per_j)],
        out_specs=[pl.BlockSpec((None, seq, LANES), per_jb),
                   pl.BlockSpec((None, n_ctx, LANES), per_jb)],
        out_shape=[jax.ShapeDtypeStruct(u3.shape, F32),
                   jax.ShapeDtypeStruct(u3c.shape, F32)],
        scratch_shapes=[pltpu.VMEM((rows, tw), BF),
                        pltpu.VMEM((tw, ns), BF),
                        pltpu.VMEM((tw, tw), BF),
                        pltpu.VMEM((ns, tw), BF),
                        pltpu.VMEM((ns // LANES, rows, LANES), F32),
                        pltpu.VMEM((ns // LANES, rows, LANES), F32),
                        pltpu.VMEM((ns // LANES, 2 * SSM_SEG, LANES), F32)],
        compiler_params=_cparams("arbitrary", "arbitrary"),
        name="ssm_chunk_scan",
    )(u3, u3c, wc, kc, dc, rexp, cexp, lam, lamseg)


def _dft_a_kernel(w_ref, x_ref, o_ref):
    o_ref[...] = jnp.dot(w_ref[...], x_ref[...], preferred_element_type=F32).astype(BF)


def _dft_rows(n):
    k = np.arange(n)
    m = (k[:, None] * k[None, :]) % n
    ang = m * (2.0 * math.pi / n)
    return jnp.asarray(np.concatenate([np.cos(ang), -np.sin(ang)], axis=0), BF)


def _dft_a(w, x3):
    B, n, cols = x3.shape
    tn = min(cols, 4096)
    return pl.pallas_call(
        _dft_a_kernel,
        grid=(B, cols // tn),
        in_specs=[pl.BlockSpec((2 * n, n), lambda b, i: (0, 0)),
                  pl.BlockSpec((None, n, tn), lambda b, i: (b, 0, i))],
        out_specs=pl.BlockSpec((None, 2 * n, tn), lambda b, i: (b, 0, i)),
        out_shape=jax.ShapeDtypeStruct((B, 2 * n, cols), BF),
        compiler_params=_cparams("parallel", "parallel"),
        name="dft_stage_a",
    )(w, x3)


def _dft_b_kernel(a_ref, g_ref, o_ref, *, kb, l1, wf):
    for i in range(kb):
        g = jnp.concatenate([g_ref[0, i], g_ref[1, i]], axis=0)
        p = jnp.dot(a_ref[i], g, preferred_element_type=F32)
        o_ref[:, i * 2 * wf:i * 2 * wf + wf] = p[:l1].astype(BF)
        o_ref[:, i * 2 * wf + wf:(i + 1) * 2 * wf] = p[l1:].astype(BF)


def _dft_b_mats(seq, l1, l2):
    k2 = np.arange(l2)[:, None, None]
    k1 = np.arange(l1)[None, :, None]
    j1 = np.arange(l1)[None, None, :]
    m = ((k1 * l2 + k2) * j1) % seq
    ang = m * (2.0 * math.pi / seq)
    ar, ai = np.cos(ang).astype(np.float32), -np.sin(ang).astype(np.float32)
    top = np.concatenate([ar, -ai], axis=2)
    bot = np.concatenate([ai, ar], axis=2)
    return jnp.asarray(np.concatenate([top, bot], axis=1), BF)


def _dft_b(amat, g5, *, kb):
    B, _, l2, l1, wf = g5.shape
    kern = functools.partial(_dft_b_kernel, kb=kb, l1=l1, wf=wf)
    return pl.pallas_call(
        kern,
        grid=(B, l2 // kb),
        in_specs=[pl.BlockSpec((kb, 2 * l1, 2 * l1), lambda b, i: (i, 0, 0)),
                  pl.BlockSpec((None, 2, kb, l1, wf), lambda b, i: (b, 0, i, 0, 0))],
        out_specs=pl.BlockSpec((None, l1, kb * 2 * wf), lambda b, i: (b, 0, i)),
        out_shape=jax.ShapeDtypeStruct((B, l1, l2 * 2 * wf), BF),
        compiler_params=_cparams("parallel", "parallel"),
        name="dft_stage_b",
    )(amat, g5)


def _dft_ctx_kernel(w_ref, x_ref, o_ref, *, n, wf):
    p = jnp.dot(w_ref[...], x_ref[...], preferred_element_type=F32)
    o_ref[:, 0:wf] = p[:n].astype(BF)
    o_ref[:, wf:2 * wf] = p[n:].astype(BF)


def _dft_ctx(w, x3):
    B, n, wf = x3.shape
    kern = functools.partial(_dft_ctx_kernel, n=n, wf=wf)
    return pl.pallas_call(
        kern,
        grid=(B,),
        in_specs=[pl.BlockSpec((2 * n, n), lambda b: (0, 0)),
                  pl.BlockSpec((None, n, wf), lambda b: (b, 0, 0))],
        out_specs=pl.BlockSpec((None, n, 2 * wf), lambda b: (b, 0, 0)),
        out_shape=jax.ShapeDtypeStruct((B, n, 2 * wf), BF),
        compiler_params=_cparams("parallel"),
        name="dft_ctx",
    )(w, x3)


def _channel_dft(wf, seq):
    c = np.arange(wf)
    same = (c[:, None] // FNET_GROUP) == (c[None, :] // FNET_GROUP)
    m = ((c[:, None] % FNET_GROUP) * (c[None, :] % FNET_GROUP)) % FNET_GROUP
    ang = m * (2.0 * math.pi / FNET_GROUP)
    mask = same / math.sqrt(seq * FNET_GROUP)
    return jnp.asarray(np.concatenate([np.cos(ang) * mask, np.sin(ang) * mask], axis=0), BF)


def _softmax_pv(parts, l_shape):
    m = parts[0][0].max(axis=-1, keepdims=True)
    for s, _ in parts[1:]:
        m = jnp.maximum(m, s.max(axis=-1, keepdims=True))
    l = jnp.zeros(l_shape, F32)
    o = None
    for s, v in parts:
        p = jnp.exp2(s - m)
        l = l + p.sum(axis=-1, keepdims=True)
        pv = jnp.dot(p.astype(BF), v, preferred_element_type=F32)
        o = pv if o is None else o + pv
    return o / l


def _na_kernel(kb_ref, pid_ref, dr_ref, q_ref, k_ref, v_ref, kc_ref, vc_ref, cbx_ref, o_ref, bias_ref,
               *, nkeys, heads):
    i = pl.program_id(1)
    start = pl.multiple_of(kb_ref[i] * GRID_W, GRID_W)
    nq = q_ref.shape[0]
    rq, kr = nq // GRID_W, nkeys // GRID_W

    pid = pid_ref[i]
    @pl.when((i == 0) | (pid != pid_ref[jnp.maximum(i - 1, 0)]))
    def _build_bias():
        for qi in range(rq):
            for w in range(kr):
                a = dr_ref[(pid * rq + qi) * kr + w]
                for h in range(heads):
                    bias_ref[h, qi * GRID_W:(qi + 1) * GRID_W, w * GRID_W:(w + 1) * GRID_W] = cbx_ref[h, a]

    lane = lax.broadcasted_iota(jnp.int32, (nq, LANES), 1)
    low = lane < HEAD_DIM
    dn = (((1,), (1,)), ((), ()))
    for hp in range(heads // 2):
        sl = slice(hp * LANES, (hp + 1) * LANES)
        q2 = q_ref[:, sl]
        k2 = k_ref[pl.ds(start, nkeys), sl]
        v2 = v_ref[pl.ds(start, nkeys), sl]
        kc2 = kc_ref[:, sl]
        vc2 = vc_ref[:, sl]
        outs = []
        for hh in range(2):
            qm = jnp.where(low if hh == 0 else jnp.logical_not(low), q2, jnp.zeros_like(q2))
            s_w = lax.dot_general(qm, k2, dn, preferred_element_type=F32) + bias_ref[2 * hp + hh]
            s_c = lax.dot_general(qm, kc2, dn, preferred_element_type=F32)
            outs.append(_softmax_pv([(s_w, v2), (s_c, vc2)], (nq, 1)))
        o_ref[:, sl] = jnp.where(low, outs[0], outs[1]).astype(BF)


def _na_plan(rows, rq):
    wr = min(NA_ROWS, rows)
    kr = rq + wr - 1
    nblk = rows // rq
    kbs, pids, pats = [], [], []
    for blk in range(nblk):
        r0 = blk * rq
        rs = [int(np.clip(r0 + i - wr // 2, 0, rows - wr)) for i in range(rq)]
        kb = min(rs[0], rows - kr)
        pat = (tuple(r - kb for r in rs), r0 - kb)
        if pat not in pats:
            pats.append(pat)
        kbs.append(kb)
        pids.append(pats.index(pat))
    return np.asarray(kbs, np.int32), np.asarray(pids, np.int32), pats, kr, wr


def _na_bias(rpb, pats, rq, kr, wr):
    cols = np.arange(GRID_W)
    cstart = np.clip(cols - NA_COLS // 2, 0, GRID_W - NA_COLS)
    kc = np.arange(GRID_W)
    col_ok = (kc[None, :] >= cstart[:, None]) & (kc[None, :] < cstart[:, None] + NA_COLS)
    dc = kc[None, :] - cols[:, None] + NA_COLS - 1
    heads, nr, ncol = rpb.shape
    sel = (dc[:, :, None] == np.arange(ncol)[None, None, :]) & col_ok[:, :, None]
    cb = jnp.einsum('hab,ckb->hack', rpb.astype(F32), jnp.asarray(sel, F32), precision=lax.Precision.HIGHEST)
    cb = jnp.where(jnp.asarray(col_ok)[None, None], cb * math.log2(math.e), NEG)
    cbx = jnp.concatenate([cb, jnp.full((heads, 1, GRID_W, GRID_W), NEG, F32)], axis=1)
    slots = []
    for rs_off, r_off in pats:
        i = np.arange(rq)[:, None]
        w = np.arange(kr)[None, :]
        rso = np.asarray(rs_off)[:, None]
        row_ok = (w >= rso) & (w < rso + wr)
        slots.append(np.where(row_ok, w - r_off - i + NA_ROWS - 1, nr))
    return cbx, np.stack(slots).reshape(-1).astype(np.int32)


def _na_latent(q, k, v, kc, vc, rpb, *, batch, seq, n_ctx, rq):
    rows = seq // GRID_W
    heads = rpb.shape[0]
    kbs, pids, pats, kr, wr = _na_plan(rows, rq)
    cbx, slots = _na_bias(rpb, pats, rq, kr, wr)
    nblk = rows // rq
    nq = rq * GRID_W
    nkeys = kr * GRID_W
    w = q.shape[1]
    kern = functools.partial(_na_kernel, nkeys=nkeys, heads=heads)
    grid_spec = pltpu.PrefetchScalarGridSpec(
        num_scalar_prefetch=3,
        grid=(batch, nblk),
        in_specs=[pl.BlockSpec((nq, w), lambda b, i, *_: (b * nblk + i, 0)),
                  pl.BlockSpec((seq, w), lambda b, i, *_: (b, 0)),
                  pl.BlockSpec((seq, w), lambda b, i, *_: (b, 0)),
                  pl.BlockSpec((n_ctx, w), lambda b, i, *_: (b, 0)),
                  pl.BlockSpec((n_ctx, w), lambda b, i, *_: (b, 0)),
                  pl.BlockSpec(cbx.shape, lambda b, i, *_: (0, 0, 0, 0))],
        out_specs=pl.BlockSpec((nq, w), lambda b, i, *_: (b * nblk + i, 0)),
        scratch_shapes=[pltpu.VMEM((heads, nq, nkeys), F32)],
    )
    return pl.pallas_call(
        kern,
        grid_spec=grid_spec,
        out_shape=jax.ShapeDtypeStruct(q.shape, BF),
        compiler_params=_cparams("arbitrary", "arbitrary"),
        name="na_latent",
    )(jnp.asarray(kbs), jnp.asarray(pids), jnp.asarray(slots), q, k, v, kc, vc, cbx)


def _na_ctx_kernel(q_ref, k_ref, v_ref, o_ref, *, heads):
    nq = q_ref.shape[0]
    lane = lax.broadcasted_iota(jnp.int32, (nq, LANES), 1)
    low = lane < HEAD_DIM
    dn = (((1,), (1,)), ((), ()))
    for hp in range(heads // 2):
        sl = slice(hp * LANES, (hp + 1) * LANES)
        q2 = q_ref[:, sl]
        k2 = k_ref[:, sl]
        v2 = v_ref[:, sl]
        outs = []
        for hh in range(2):
            qm = jnp.where(low if hh == 0 else jnp.logical_not(low), q2, jnp.zeros_like(q2))
            s = lax.dot_general(qm, k2, dn, preferred_element_type=F32)
            outs.append(_softmax_pv([(s, v2)], (nq, 1)))
        o_ref[:, sl] = jnp.where(low, outs[0], outs[1]).astype(BF)


def _na_ctx(qc, kc, vc, *, batch, n_ctx, heads):
    w = qc.shape[1]
    spec = pl.BlockSpec((n_ctx, w), lambda b: (b, 0))
    return pl.pallas_call(
        functools.partial(_na_ctx_kernel, heads=heads),
        grid=(batch,),
        in_specs=[spec, spec, spec],
        out_specs=spec,
        out_shape=jax.ShapeDtypeStruct(qc.shape, BF),
        compiler_params=_cparams("parallel"),
        name="na_ctx",
    )(qc, kc, vc)


def _outproj_kernel(z_ref, pf_ref, na_ref, x_ref, gate_ref, gpost_ref,
                    wglu_ref, cs_ref, wfo_ref, wout_ref, o_ref, *, widths):
    w_ssm, w_fn, w_na = widths
    z = jnp.concatenate([z_ref[j] for j in range(w_ssm // LANES)], axis=1)
    gl = jnp.dot(z.astype(BF), wglu_ref[...], preferred_element_type=F32)
    y_ssm = (z * jax.nn.sigmoid(gl)).astype(BF)
    mixed = jnp.dot(pf_ref[...], cs_ref[...], preferred_element_type=F32).astype(BF)
    y_fft = jnp.dot(mixed, wfo_ref[...], preferred_element_type=F32).astype(BF)
    o = (jnp.dot(y_ssm, wout_ref[0:w_ssm, :], preferred_element_type=F32)
         + jnp.dot(y_fft, wout_ref[w_ssm:w_ssm + w_fn, :], preferred_element_type=F32)
         + jnp.dot(na_ref[...], wout_ref[w_ssm + w_fn:, :], preferred_element_type=F32))
    ms = jnp.mean(o * o, axis=-1, keepdims=True)
    o_ref[...] = x_ref[...] + gate_ref[...] * (o * lax.rsqrt(ms + EPS) * gpost_ref[...])


def _outproj(z3, pf, yna, x2d, mod3, g_post, w_glu, cs, w_fo, w_out, *,
             tm, tiles_per_batch, mod_row, widths):
    n, d = x2d.shape
    w_ssm, w_fn, w_na = widths
    if mod_row is None:
        row = lambda i: i // tiles_per_batch
    else:
        row = lambda i: mod_row
    kern = functools.partial(_outproj_kernel, widths=widths)
    return pl.pallas_call(
        kern,
        grid=(n // tm,),
        in_specs=[pl.BlockSpec((w_ssm // LANES, tm, LANES), lambda i: (0, i, 0)),
                  pl.BlockSpec((tm, 2 * w_fn), lambda i: (i, 0)),
                  pl.BlockSpec((tm, w_na), lambda i: (i, 0)),
                  pl.BlockSpec((tm, d), lambda i: (i, 0)),
                  pl.BlockSpec((None, 1, d), lambda i: (row(i), 0, 2)),
                  pl.BlockSpec((1, d), lambda i: (0, 0)),
                  _resident(w_glu.shape, lambda i: (0, 0)),
                  _resident(cs.shape, lambda i: (0, 0)),
                  _resident(w_fo.shape, lambda i: (0, 0)),
                  _resident(w_out.shape, lambda i: (0, 0))],
        out_specs=pl.BlockSpec((tm, d), lambda i: (i, 0)),
        out_shape=jax.ShapeDtypeStruct((n, d), F32),
        compiler_params=_cparams("parallel"),
        name="outproj",
    )(z3, pf, yna, x2d, mod3, g_post.reshape(1, d), w_glu, cs, w_fo, w_out)


def _ffn_kernel(x_ref, sh_ref, sc_ref, gate_ref, gpre_ref, gpost_ref, wg_ref, wu_ref, wd_ref,
                o_ref, a_ref, *, nchunk):
    x = x_ref[...]
    ms = jnp.mean(x * x, axis=-1, keepdims=True)
    m = ((x * lax.rsqrt(ms + EPS) * gpre_ref[...]) * (1.0 + sc_ref[...]) + sh_ref[...]).astype(BF)
    fc = wg_ref.shape[1] // nchunk
    for c in range(nchunk):
        sl = slice(c * fc, (c + 1) * fc)
        g = jnp.dot(m, wg_ref[:, sl], preferred_element_type=F32)
        u = jnp.dot(m, wu_ref[:, sl], preferred_element_type=F32)
        a_ref[:, sl] = (g * jax.nn.sigmoid(g) * u).astype(BF)
    y = jnp.dot(a_ref[...], wd_ref[...], preferred_element_type=F32)
    ms2 = jnp.mean(y * y, axis=-1, keepdims=True)
    o_ref[...] = x + gate_ref[...] * (y * lax.rsqrt(ms2 + EPS) * gpost_ref[...])


def _ffn(x2d, mod3, g_pre, g_post, wg, wu, wd, *, tm, tiles_per_batch, mod_row):
    n, d = x2d.shape
    dff = wg.shape[1]
    if mod_row is None:
        row = lambda i: i // tiles_per_batch
    else:
        row = lambda i: mod_row
    nchunk = 2 if dff % (2 * LANES) == 0 else 1
    return pl.pallas_call(
        functools.partial(_ffn_kernel, nchunk=nchunk),
        grid=(n // tm,),
        in_specs=[pl.BlockSpec((tm, d), lambda i: (i, 0)),
                  pl.BlockSpec((None, 1, d), lambda i: (row(i), 0, 3)),
                  pl.BlockSpec((None, 1, d), lambda i: (row(i), 0, 4)),
                  pl.BlockSpec((None, 1, d), lambda i: (row(i), 0, 5)),
                  pl.BlockSpec((1, d), lambda i: (0, 0)),
                  pl.BlockSpec((1, d), lambda i: (0, 0)),
                  _resident(wg.shape, lambda i: (0, 0)),
                  _resident(wu.shape, lambda i: (0, 0)),
                  _resident(wd.shape, lambda i: (0, 0))],
        out_specs=pl.BlockSpec((tm, d), lambda i: (i, 0)),
        out_shape=jax.ShapeDtypeStruct((n, d), F32),
        scratch_shapes=[pltpu.VMEM((tm, dff), BF)],
        compiler_params=_cparams("parallel"),
        name="ffn",
    )(x2d, mod3, mod3, mod3, g_pre.reshape(1, d), g_post.reshape(1, d), wg, wu, wd)


def _layer(x2d, xc2d, cstack, p, *, batch, seq, n_ctx, last, rope_tabs, consts):
    d = x2d.shape[1]
    w_ssm = p["ssm_d"].shape[0]
    w_fn = p["w_fourier"].shape[0]
    w_na = (p["w_in"].shape[1] - w_ssm - w_fn) // 3
    widths = (w_ssm, w_fn, w_na)
    heads = w_na // HEAD_DIM
    T = SSM_T
    tm = min(512, seq)
    tmc = min(512, batch * n_ctx)

    mod = _mod_rows(cstack, p["w_mod_all"], p["b_mod_all"], p["layer"])
    mod3 = mod.reshape(mod.shape[0], 1, 6 * d)

    w_in = p["w_in"].astype(BF)
    cos, sin = rope_tabs
    u3, f, q, k, v = _inproj(x2d, mod3, p["g_pre_mix"], w_in, cos, sin, tm=tm,
                             tiles_per_batch=seq // tm, mod_row=None, widths=widths, rope=True)
    ones = jnp.ones((tmc, LANES), F32)
    u3c, fc, qc, kc, vc = _inproj(xc2d, mod3, p["g_pre_mix"], w_in, ones, ones, tm=tmc,
                                  tiles_per_batch=1, mod_row=batch, widths=widths, rope=False)

    seg_len = (seq + 2 * n_ctx) // T // SSM_SEG
    wc, kcm, dcm, lam, lamseg = _ssm_weights(p["ssm_a_re"], p["ssm_a_im"], p["ssm_log_dt"], p["ssm_b_re"],
                                             p["ssm_b_im"], p["ssm_c_re"], p["ssm_c_im"], p["ssm_d"], seg_len)
    z3, zc3 = _ssm(u3, u3c, wc, kcm, dcm, lam, lamseg, batch=batch, seq=seq, n_ctx=n_ctx)

    l2 = consts["l2"]
    l1 = seq // l2
    g = _dft_a(consts["dft_a"], f.reshape(batch, l2, l1 * w_fn))
    pf = _dft_b(consts["dft_b"], g.reshape(batch, 2, l2, l1, w_fn), kb=min(8, l2))
    pf = pf.reshape(batch * seq, 2 * w_fn)

    yna = _na_latent(q, k, v, kc, vc, p["na_rpb"], batch=batch, seq=seq, n_ctx=n_ctx, rq=consts["rq"])

    w_glu = p["w_glu"].astype(BF)
    w_fo = p["w_fourier"].astype(BF)
    w_out = p["w_out"].astype(BF)
    x2d = _outproj(z3, pf, yna, x2d, mod3, p["g_post_mix"], w_glu, consts["cs_lat"], w_fo, w_out,
                   tm=tm, tiles_per_batch=seq // tm, mod_row=None, widths=widths)
    wg = p["w_ffn_gate"].astype(BF)
    wu = p["w_ffn_up"].astype(BF)
    wd = p["w_ffn_down"].astype(BF)
    x2d = _ffn(x2d, mod3, p["g_pre_ffn"], p["g_post_ffn"], wg, wu, wd,
               tm=tm, tiles_per_batch=seq // tm, mod_row=None)

    if not last:
        pfc = _dft_ctx(consts["dft_ctx"], fc.reshape(batch, n_ctx, w_fn)).reshape(batch * n_ctx, 2 * w_fn)
        ynac = _na_ctx(qc, kc, vc, batch=batch, n_ctx=n_ctx, heads=heads)
        tc = min(tmc, n_ctx)
        xc2d = _outproj(zc3, pfc, ynac, xc2d, mod3, p["g_post_mix"], w_glu, consts["cs_ctx"], w_fo, w_out,
                        tm=tc, tiles_per_batch=n_ctx // tc, mod_row=batch, widths=widths)
        xc2d = _ffn(xc2d, mod3, p["g_pre_ffn"], p["g_post_ffn"], wg, wu, wd,
                    tm=tc, tiles_per_batch=n_ctx // tc, mod_row=batch)
    else:
        xc2d = None
    return x2d, xc2d


def kernel(x, c, ctx, c_ctx, w_mod, b_mod, g_pre_mix, g_post_mix, w_in, ssm_a_re, ssm_a_im, ssm_log_dt,
           ssm_b_re, ssm_b_im, ssm_c_re, ssm_c_im, ssm_d, w_glu, w_fourier, na_rpb, w_out, g_pre_ffn,
           g_post_ffn, w_ffn_gate, w_ffn_up, w_ffn_down):
    batch, seq, d = x.shape
    n_ctx = ctx.shape[1]
    depth = w_mod.shape[0]
    w_fn = w_fourier.shape[1]
    params = dict(g_pre_mix=g_pre_mix, g_post_mix=g_post_mix, w_in=w_in,
                  ssm_a_re=ssm_a_re, ssm_a_im=ssm_a_im, ssm_log_dt=ssm_log_dt, ssm_b_re=ssm_b_re,
                  ssm_b_im=ssm_b_im, ssm_c_re=ssm_c_re, ssm_c_im=ssm_c_im, ssm_d=ssm_d, w_glu=w_glu,
                  w_fourier=w_fourier, na_rpb=na_rpb, w_out=w_out, g_pre_ffn=g_pre_ffn,
                  g_post_ffn=g_post_ffn, w_ffn_gate=w_ffn_gate, w_ffn_up=w_ffn_up, w_ffn_down=w_ffn_down)

    nrow = -(-(batch + 1) // 8) * 8
    cstack = jnp.concatenate([c, c_ctx[None, :], jnp.zeros((nrow - batch - 1, d), c.dtype)], axis=0)

    l2 = 64 if seq % (64 * 8) == 0 else 8
    l1 = seq // l2
    consts = dict(
        l2=l2,
        rq=4,
        dft_a=_dft_rows(l2).astype(BF),
        dft_b=_dft_b_mats(seq, l1, l2).astype(BF),
        dft_ctx=_dft_rows(n_ctx).astype(BF),
        cs_lat=_channel_dft(w_fn, seq).astype(BF),
        cs_ctx=_channel_dft(w_fn, n_ctx).astype(BF),
    )
    rope_tabs = _rope_tables(seq)

    x2d = x.reshape(batch * seq, d)
    xc2d = ctx.reshape(batch * n_ctx, d)
    for layer in range(depth):
        p = {name: val[layer] for name, val in params.items()}
        p.update(w_mod_all=w_mod, b_mod_all=b_mod, layer=layer)
        x2d, xc2d = _layer(x2d, xc2d, cstack, p, batch=batch, seq=seq, n_ctx=n_ctx,
                           last=(layer == depth - 1), rope_tabs=rope_tabs, consts=consts)
    return x2d.reshape(batch, seq, d)
```

```python
import functools
import math

import numpy as np
import jax
import jax.numpy as jnp
from jax import lax
from jax.experimental import pallas as pl
from jax.experimental.pallas import tpu as pltpu

BF = jnp.bfloat16
F32 = jnp.float32

EPS = 1e-6
GRID_W = 64
HEAD_DIM = 64
NA_ROWS = 8
NA_COLS = 16
ROPE_BASE = 10000.0
SSM_GROUP = 16
SSM_STATE = 64
FNET_GROUP = 64
LANES = 128
SSM_T = 8
SSM_SEG = 8
NEG = -1e30
VMEM_LIMIT = 56 * 1024 * 1024


def _cparams(*sem):
    return pltpu.CompilerParams(dimension_semantics=sem, vmem_limit_bytes=VMEM_LIMIT)


def _resident(shape, index_map):
    return pl.BlockSpec(shape, index_map, pipeline_mode=pl.Buffered(1))


def _layer_resident(w_all, layer):
    return pl.BlockSpec((None,) + w_all.shape[1:], lambda *_: (layer, 0, 0), pipeline_mode=pl.Buffered(1))


def _cast_kernel(w_ref, o_ref):
    o_ref[...] = w_ref[...].astype(BF)


def _cast_bf16(w_all):
    depth, rows, cols = w_all.shape
    br = rows
    while br * cols * 4 > 4 * 1024 * 1024 and br % 32 == 0:
        br //= 2
    spec = pl.BlockSpec((None, br, cols), lambda l, i: (l, i, 0))
    return pl.pallas_call(
        _cast_kernel,
        grid=(depth, rows // br),
        in_specs=[spec],
        out_specs=spec,
        out_shape=jax.ShapeDtypeStruct(w_all.shape, BF),
        compiler_params=_cparams("parallel", "parallel"),
        name="cast_bf16",
    )(w_all)


def _mod_kernel(c_ref, w_ref, b_ref, o_ref):
    cs = c_ref[...]
    s = (cs * jax.nn.sigmoid(cs)).astype(BF)
    o_ref[...] = jnp.dot(s, w_ref[...].astype(BF), preferred_element_type=F32) + b_ref[...]


def _mod_rows(cstack, w_mod, b_mod, layer):
    rows, d = cstack.shape
    depth, _, n = w_mod.shape
    return pl.pallas_call(
        _mod_kernel,
        grid=(n // d,),
        in_specs=[pl.BlockSpec((rows, d), lambda i: (0, 0)),
                  pl.BlockSpec((None, d, d), lambda i: (layer, 0, i)),
                  pl.BlockSpec((None, 1, d), lambda i: (layer, 0, i))],
        out_specs=pl.BlockSpec((rows, d), lambda i: (0, i)),
        out_shape=jax.ShapeDtypeStruct((rows, n), F32),
        compiler_params=_cparams("arbitrary"),
        name="mod_rows",
    )(cstack, w_mod, b_mod.reshape(depth, 1, n))


def _rope_tile(t, cos, sin, first):
    partner = jnp.where(first, pltpu.roll(t, LANES - 16, 1), pltpu.roll(t, 16, 1))
    return t * cos + partner * sin


def _inproj_kernel(x_ref, sh_ref, sc_ref, g_ref, w_ref, cos_ref, sin_ref,
                   u_ref, f_ref, q_ref, k_ref, v_ref, *, widths, rope):
    x = x_ref[...]
    ms = jnp.mean(x * x, axis=-1, keepdims=True)
    xn = x * lax.rsqrt(ms + EPS) * g_ref[...]
    m = (xn * (1.0 + sc_ref[...]) + sh_ref[...]).astype(BF)
    w_ssm, w_fn, w_na = widths

    def proj(lo, n):
        return jnp.dot(m, w_ref[:, lo:lo + n], preferred_element_type=F32)

    u = proj(0, w_ssm)
    for j in range(w_ssm // LANES):
        u_ref[j] = u[:, j * LANES:(j + 1) * LANES]
    f_ref[...] = proj(w_ssm, w_fn).astype(BF)
    q = proj(w_ssm + w_fn, w_na)
    k = proj(w_ssm + w_fn + w_na, w_na)
    v_ref[...] = proj(w_ssm + w_fn + 2 * w_na, w_na).astype(BF)
    scale = HEAD_DIM ** -0.5 * math.log2(math.e)
    if rope:
        cos = cos_ref[...]
        sin = sin_ref[...]
        lane = lax.broadcasted_iota(jnp.int32, cos.shape, 1)
        first = (lane % 32) < 16
        for j in range(w_na // LANES):
            sl = slice(j * LANES, (j + 1) * LANES)
            q_ref[:, sl] = (_rope_tile(q[:, sl], cos, sin, first) * scale).astype(BF)
            k_ref[:, sl] = _rope_tile(k[:, sl], cos, sin, first).astype(BF)
    else:
        q_ref[...] = (q * scale).astype(BF)
        k_ref[...] = k.astype(BF)


def _inproj(x2d, mod3, g_pre, w_in, cos, sin, *, layer, tm, tiles_per_batch, mod_row, widths, rope):
    n, d = x2d.shape
    w_ssm, w_fn, w_na = widths
    nt = n // tm
    if mod_row is None:
        row = lambda i: i // tiles_per_batch
    else:
        row = lambda i: mod_row
    pos = lambda i: (i % tiles_per_batch, 0)
    kern = functools.partial(_inproj_kernel, widths=widths, rope=rope)
    return pl.pallas_call(
        kern,
        grid=(nt,),
        in_specs=[pl.BlockSpec((tm, d), lambda i: (i, 0)),
                  pl.BlockSpec((None, 1, d), lambda i: (row(i), 0, 0)),
                  pl.BlockSpec((None, 1, d), lambda i: (row(i), 0, 1)),
                  pl.BlockSpec((1, d), lambda i: (0, 0)),
                  _layer_resident(w_in, layer),
                  pl.BlockSpec((tm, LANES), pos),
                  pl.BlockSpec((tm, LANES), pos)],
        out_specs=[pl.BlockSpec((w_ssm // LANES, tm, LANES), lambda i: (0, i, 0)),
                   pl.BlockSpec((tm, w_fn), lambda i: (i, 0)),
                   pl.BlockSpec((tm, w_na), lambda i: (i, 0)),
                   pl.BlockSpec((tm, w_na), lambda i: (i, 0)),
                   pl.BlockSpec((tm, w_na), lambda i: (i, 0))],
        out_shape=[jax.ShapeDtypeStruct((w_ssm // LANES, n, LANES), F32),
                   jax.ShapeDtypeStruct((n, w_fn), BF),
                   jax.ShapeDtypeStruct((n, w_na), BF),
                   jax.ShapeDtypeStruct((n, w_na), BF),
                   jax.ShapeDtypeStruct((n, w_na), BF)],
        compiler_params=_cparams("parallel"),
        name="inproj_rope" if rope else "inproj_ctx",
    )(x2d, mod3, mod3, g_pre.reshape(1, d), w_in, cos, sin)


def _rope_tables(seq):
    t = np.arange(seq)
    row = (t // GRID_W).astype(np.float32)
    col = (t % GRID_W).astype(np.float32)
    quarter = HEAD_DIM // 4
    freqs = (np.float32(ROPE_BASE) ** (-np.arange(quarter, dtype=np.float32) / quarter)).astype(np.float32)
    d = np.arange(LANES) % HEAD_DIM
    use_col = (d // (HEAD_DIM // 2)) == 1
    fidx = d % quarter
    sign = np.where((d % (HEAD_DIM // 2)) // quarter == 0, -1.0, 1.0)
    pos = np.where(use_col[None, :], col[:, None], row[:, None])
    ang = (pos * freqs[fidx][None, :]).astype(np.float32).astype(np.float64)
    return jnp.asarray(np.cos(ang), F32), jnp.asarray(np.sin(ang) * sign[None, :], F32)


def _gelu_tanh(x):
    c = math.sqrt(2.0 / math.pi)
    return x * (0.5 * (1.0 + jnp.tanh(c * (x + 0.044715 * (x * x * x)))))


def _group_of(idx, width, groups):
    shift = width.bit_length() - 1
    assert width == 1 << shift and groups & (groups - 1) == 0
    return lax.bitwise_and(lax.shift_right_logical(idx, shift), groups - 1)


def _same_group(shape, row0, row_width, col0, col_width, groups):
    r = lax.broadcasted_iota(jnp.int32, shape, 0) + row0
    c = lax.broadcasted_iota(jnp.int32, shape, 1) + col0
    return _group_of(r, row_width, groups) == _group_of(c, col_width, groups)


def _ssm_kernel(ul_ref, uc_ref, wc_ref, kc_ref, dc_ref, rexp_ref, cexp_ref, lam_ref, lamseg_ref,
                z_ref, zc_ref, u_s, bm_s, kin_s, cm_s, s_ref, h_ref, e_ref, *, seg_len, n_lat, n_ctx):
    T = SSM_T
    rows, tw = u_s.shape
    nt = s_ref.shape[0]
    tpq = nt // 4
    nst = tpq * LANES
    gpt = LANES // SSM_GROUP
    cw = 512

    def lanes(c):
        return slice(c * LANES, (c + 1) * LANES)

    @pl.when(pl.program_id(1) == 0)
    def _expand_operators():
        rexp = rexp_ref[...]
        for c0 in range(0, 4 * nst, cw):
            blk = jnp.dot(rexp, wc_ref[:, c0:c0 + cw], preferred_element_type=F32)
            keep = _same_group(blk.shape, 0, SSM_GROUP, c0, SSM_STATE, gpt)
            bm_s[:, c0:c0 + cw] = jnp.where(keep, blk, 0.0).astype(BF)
        for c0 in range(0, tw, cw):
            blk = jnp.dot(rexp, kc_ref[:, c0:c0 + cw], preferred_element_type=F32)
            keep = _same_group(blk.shape, 0, SSM_GROUP, c0, SSM_GROUP, gpt)
            kin_s[:, c0:c0 + cw] = jnp.where(keep, blk, 0.0).astype(BF)
        cexp = cexp_ref[...]
        for r0 in range(0, 4 * nst, cw):
            blk = jnp.dot(dc_ref[r0:r0 + cw, :], cexp, preferred_element_type=F32)
            keep = _same_group(blk.shape, r0, SSM_STATE, 0, SSM_GROUP, gpt)
            cm_s[r0:r0 + cw, :] = jnp.where(keep, blk, 0.0).astype(BF)

    for t in range(T):
        ct = uc_ref[pl.ds(t, n_ctx, stride=T), :].astype(BF)
        u_s[0:n_ctx, lanes(t)] = ct
        u_s[n_ctx:n_ctx + n_lat, lanes(t)] = ul_ref[pl.ds(t, n_lat, stride=T), :].astype(BF)
        u_s[n_ctx + n_lat:rows, lanes(t)] = ct
    u = u_s[...]
    for q in range(4):
        r = jnp.dot(u, bm_s[:, q * nst:(q + 1) * nst], preferred_element_type=F32)
        for t in range(tpq):
            s_ref[q * tpq + t] = r[:, t * LANES:(t + 1) * LANES]

    def scan_step(i, carry, record):
        new = list(carry)
        for d, ii in ((0, i), (1, seg_len - 1 - i)):
            rsl = pl.ds(ii, SSM_SEG, stride=seg_len)
            for t in range(tpq):
                cr, ci = 2 * d * tpq + t, (2 * d + 1) * tpq + t
                hr, hi = carry[cr], carry[ci]
                if record:
                    h_ref[cr, rsl, :] = hr
                    h_ref[ci, rsl, :] = hi
                lr, li = lam_ref[:, lanes(cr)], lam_ref[:, lanes(ci)]
                new[cr] = lr * hr - li * hi + s_ref[cr, rsl, :]
                new[ci] = lr * hi + li * hr + s_ref[ci, rsl, :]
        return tuple(new)

    zero = jnp.zeros((SSM_SEG, LANES), F32)
    fin = lax.fori_loop(0, seg_len, lambda i, c: scan_step(i, c, False), (zero,) * nt, unroll=2)
    for c in range(nt):
        e_ref[c, 0:SSM_SEG, :] = fin[c]

    for d in range(2):
        for t in range(tpq):
            cr, ci = 2 * d * tpq + t, (2 * d + 1) * tpq + t
            lr, li = lamseg_ref[:, lanes(cr)], lamseg_ref[:, lanes(ci)]
            er = jnp.zeros((1, LANES), F32)
            ei = jnp.zeros((1, LANES), F32)
            for n in range(SSM_SEG):
                s = n if d == 0 else SSM_SEG - 1 - n
                e_ref[cr, SSM_SEG + s:SSM_SEG + s + 1, :] = er
                e_ref[ci, SSM_SEG + s:SSM_SEG + s + 1, :] = ei
                fr, fi = e_ref[cr, s:s + 1, :], e_ref[ci, s:s + 1, :]
                er, ei = lr * er - li * ei + fr, lr * ei + li * er + fi

    ent = tuple(e_ref[c, SSM_SEG:2 * SSM_SEG, :] for c in range(nt))
    lax.fori_loop(0, seg_len, lambda i, c: scan_step(i, c, True), ent, unroll=2)

    hcat = jnp.concatenate([h_ref[c].astype(BF) for c in range(nt)], axis=1)
    for t0 in range(0, T, 2):
        cs = slice(t0 * LANES, (t0 + 2) * LANES)
        y = (jnp.dot(u, kin_s[:, cs], preferred_element_type=F32)
             + jnp.dot(hcat, cm_s[:, cs], preferred_element_type=F32))
        g = _gelu_tanh(y)
        for t in (t0, t0 + 1):
            z_ref[pl.ds(t, n_lat, stride=T), :] = g[n_ctx:n_ctx + n_lat, lanes(t - t0)]

    uc = u_s[0:n_ctx, :]
    hc = jnp.concatenate([h_ref[c, 0:n_ctx, :].astype(BF) for c in range(2 * tpq)]
                         + [h_ref[c, rows - n_ctx:rows, :].astype(BF) for c in range(2 * tpq, nt)], axis=1)
    gc = _gelu_tanh(jnp.dot(uc, kin_s[...], preferred_element_type=F32)
                    + jnp.dot(hc, cm_s[...], preferred_element_type=F32))
    for t in range(T):
        zc_ref[pl.ds(t, n_ctx, stride=T), :] = gc[:, lanes(t)]


def _ssm_weights(a_re, a_im, log_dt, b_re, b_im, c_re, c_im, d_skip, seg_len):
    T = SSM_T
    g, p = a_re.shape[1], a_re.shape[2]
    hc = b_re.shape[-1]
    gpt = LANES // hc
    J = g // gpt
    lam = lax.complex(a_re.astype(F32), a_im.astype(F32))
    dt = jnp.exp(log_dt.astype(F32))[..., None]
    ldt = lam * dt
    lam_bar = jnp.exp(ldt)
    bbar = ((lam_bar - 1) / lam)[..., None] * lax.complex(b_re.astype(F32), b_im.astype(F32))
    cmat = lax.complex(c_re.astype(F32), c_im.astype(F32))
    kk = jnp.arange(T + 1, dtype=F32)
    pw = jnp.exp(ldt[:, None] * kk[None, :, None, None])

    wf = pw[0, T - 1 - jnp.arange(T)][:, :, None, :] * jnp.swapaxes(bbar[0], 1, 2)[None]
    wb = pw[1, jnp.arange(T)][:, :, None, :] * jnp.swapaxes(bbar[1], 1, 2)[None]
    wq = jnp.stack([wf.real, wf.imag, wb.real, wb.imag]).reshape(4, T, J, gpt, hc, p)
    wc = wq.transpose(2, 1, 4, 0, 3, 5).reshape(J, T * hc, 4 * gpt * p)

    df = cmat[0][None] * pw[0, 1 + jnp.arange(T)][:, :, None, :]
    db = cmat[1][None] * pw[1, T - jnp.arange(T)][:, :, None, :]
    dq = jnp.stack([df.real, -df.imag, db.real, -db.imag]).reshape(4, T, J, gpt, hc, p)
    dc = dq.transpose(2, 0, 3, 5, 1, 4).reshape(J, 4 * gpt * p, T * hc)

    mf = jnp.einsum('gcp,kgp,gph->kgch', cmat[0], pw[0, :T], bbar[0]).real
    mb = jnp.einsum('gcp,kgp,gph->kgch', cmat[1], pw[1, :T], bbar[1]).real
    skip = d_skip.astype(F32).reshape(g, hc)[:, :, None] * jnp.eye(hc, dtype=F32)[None]
    m0 = mf[0] + mb[0] + skip
    lags = jnp.concatenate([mb[1:][::-1], m0[None], mf[1:]], axis=0)
    tin = np.arange(T)[:, None]
    tout = np.arange(T)[None, :]
    kt = lags[jnp.asarray(tout - tin + T - 1)]
    kt = kt.reshape(T, T, J, gpt, hc, hc)
    kc = kt.transpose(2, 0, 5, 1, 3, 4).reshape(J, T * hc, T * LANES)

    def lam_rows(power):
        lp = jnp.exp(ldt * power)
        lt = lp.reshape(2, J, gpt * p)
        return jnp.concatenate([lt[0].real, lt[0].imag, lt[1].real, lt[1].imag], axis=-1)[:, None, :]

    lam_t = jnp.broadcast_to(lam_rows(float(T)), (J, SSM_SEG, 4 * gpt * p))
    return wc.astype(BF), kc.astype(BF), dc.astype(BF), lam_t, lam_rows(float(T * seg_len))


def _ssm_expanders():
    T = SSM_T
    r = np.arange(T * LANES)
    rexp = np.zeros((T * LANES, T * SSM_GROUP), np.float32)
    rexp[r, (r // LANES) * SSM_GROUP + r % SSM_GROUP] = 1.0
    return jnp.asarray(rexp, BF), jnp.asarray(rexp.T, BF)


def _ssm(u3, u3c, wc, kc, dc, lam, lamseg, *, layer, batch, seq, n_ctx):
    T = SSM_T
    J = u3.shape[0]
    tw = T * LANES
    ns = wc.shape[-1]
    n_lat, n_c = seq // T, n_ctx // T
    rows = n_lat + 2 * n_c
    seg_len = rows // SSM_SEG
    assert rows % SSM_SEG == 0 and n_c % 16 == 0
    rexp, cexp = _ssm_expanders()
    kern = functools.partial(_ssm_kernel, seg_len=seg_len, n_lat=n_lat, n_ctx=n_c)
    const = lambda j, b: (0, 0)
    per_j = lambda j, b: (layer, j, 0, 0)
    per_jb = lambda j, b: (j, b, 0)
    return pl.pallas_call(
        kern,
        grid=(J, batch),
        in_specs=[pl.BlockSpec((None, seq, LANES), per_jb, pipeline_mode=pl.Buffered(1)),
                  pl.BlockSpec((None, n_ctx, LANES), per_jb),
                  pl.BlockSpec((None, None) + wc.shape[2:], per_j),
                  pl.BlockSpec((None, None) + kc.shape[2:], per_j),
                  pl.BlockSpec((None, None) + dc.shape[2:], per_j),
                  pl.BlockSpec(rexp.shape, const),
                  pl.BlockSpec(cexp.shape, const),
                  pl.BlockSpec((None, None, SSM_SEG, ns), per_j),
                  pl.BlockSpec((None, None, 1, ns), per_j)],
        out_specs=[pl.BlockSpec((None, seq, LANES), per_jb),
                   pl.BlockSpec((None, n_ctx, LANES), per_jb)],
        out_shape=[jax.ShapeDtypeStruct(u3.shape, F32),
                   jax.ShapeDtypeStruct(u3c.shape, F32)],
        scratch_shapes=[pltpu.VMEM((rows, tw), BF),
                        pltpu.VMEM((tw, ns), BF),
                        pltpu.VMEM((tw, tw), BF),
                        pltpu.VMEM((ns, tw), BF),
                        pltpu.VMEM((ns // LANES, rows, LANES), F32),
                        pltpu.VMEM((ns // LANES, rows, LANES), F32),
                        pltpu.VMEM((ns // LANES, 2 * SSM_SEG, LANES), F32)],
        compiler_params=_cparams("arbitrary", "arbitrary"),
        name="ssm_chunk_scan",
    )(u3, u3c, wc, kc, dc, rexp, cexp, lam, lamseg)


def _dft_a_kernel(w_ref, x_ref, o_ref):
    o_ref[...] = jnp.dot(w_ref[...], x_ref[...], preferred_element_type=F32).astype(BF)


def _dft_rows(n):
    k = np.arange(n)
    m = (k[:, None] * k[None, :]) % n
    ang = m * (2.0 * math.pi / n)
    return jnp.asarray(np.concatenate([np.cos(ang), -np.sin(ang)], axis=0), BF)


def _dft_a(w, x3):
    B, n, cols = x3.shape
    tn = min(cols, 4096)
    return pl.pallas_call(
        _dft_a_kernel,
        grid=(B, cols // tn),
        in_specs=[pl.BlockSpec((2 * n, n), lambda b, i: (0, 0)),
                  pl.BlockSpec((None, n, tn), lambda b, i: (b, 0, i))],
        out_specs=pl.BlockSpec((None, 2 * n, tn), lambda b, i: (b, 0, i)),
        out_shape=jax.ShapeDtypeStruct((B, 2 * n, cols), BF),
        compiler_params=_cparams("parallel", "parallel"),
        name="dft_stage_a",
    )(w, x3)


def _dft_b_kernel(a_ref, g_ref, o_ref, *, kb, l1, wf):
    for i in range(kb):
        g = jnp.concatenate([g_ref[0, i], g_ref[1, i]], axis=0)
        p = jnp.dot(a_ref[i], g, preferred_element_type=F32)
        o_ref[:, i * 2 * wf:i * 2 * wf + wf] = p[:l1].astype(BF)
        o_ref[:, i * 2 * wf + wf:(i + 1) * 2 * wf] = p[l1:].astype(BF)


def _dft_b_mats(seq, l1, l2):
    k2 = np.arange(l2)[:, None, None]
    k1 = np.arange(l1)[None, :, None]
    j1 = np.arange(l1)[None, None, :]
    m = ((k1 * l2 + k2) * j1) % seq
    ang = m * (2.0 * math.pi / seq)
    ar, ai = np.cos(ang).astype(np.float32), -np.sin(ang).astype(np.float32)
    top = np.concatenate([ar, -ai], axis=2)
    bot = np.concatenate([ai, ar], axis=2)
    return jnp.asarray(np.concatenate([top, bot], axis=1), BF)


def _dft_b(amat, g5, *, kb):
    B, _, l2, l1, wf = g5.shape
    kern = functools.partial(_dft_b_kernel, kb=kb, l1=l1, wf=wf)
    return pl.pallas_call(
        kern,
        grid=(B, l2 // kb),
        in_specs=[pl.BlockSpec((kb, 2 * l1, 2 * l1), lambda b, i: (i, 0, 0)),
                  pl.BlockSpec((None, 2, kb, l1, wf), lambda b, i: (b, 0, i, 0, 0))],
        out_specs=pl.BlockSpec((None, l1, kb * 2 * wf), lambda b, i: (b, 0, i)),
        out_shape=jax.ShapeDtypeStruct((B, l1, l2 * 2 * wf), BF),
        compiler_params=_cparams("parallel", "parallel"),
        name="dft_stage_b",
    )(amat, g5)


def _dft_ctx_kernel(w_ref, x_ref, o_ref, *, n, wf):
    p = jnp.dot(w_ref[...], x_ref[...], preferred_element_type=F32)
    o_ref[:, 0:wf] = p[:n].astype(BF)
    o_ref[:, wf:2 * wf] = p[n:].astype(BF)


def _dft_ctx(w, x3):
    B, n, wf = x3.shape
    kern = functools.partial(_dft_ctx_kernel, n=n, wf=wf)
    return pl.pallas_call(
        kern,
        grid=(B,),
        in_specs=[pl.BlockSpec((2 * n, n), lambda b: (0, 0)),
                  pl.BlockSpec((None, n, wf), lambda b: (b, 0, 0))],
        out_specs=pl.BlockSpec((None, n, 2 * wf), lambda b: (b, 0, 0)),
        out_shape=jax.ShapeDtypeStruct((B, n, 2 * wf), BF),
        compiler_params=_cparams("parallel"),
        name="dft_ctx",
    )(w, x3)


def _channel_dft(wf, seq):
    c = np.arange(wf)
    same = (c[:, None] // FNET_GROUP) == (c[None, :] // FNET_GROUP)
    m = ((c[:, None] % FNET_GROUP) * (c[None, :] % FNET_GROUP)) % FNET_GROUP
    ang = m * (2.0 * math.pi / FNET_GROUP)
    mask = same / math.sqrt(seq * FNET_GROUP)
    return jnp.asarray(np.concatenate([np.cos(ang) * mask, np.sin(ang) * mask], axis=0), BF)


def _softmax_pv(parts, l_shape):
    m = parts[0][0].max(axis=-1, keepdims=True)
    for s, _ in parts[1:]:
        m = jnp.maximum(m, s.max(axis=-1, keepdims=True))
    l = jnp.zeros(l_shape, F32)
    o = None
    for s, v in parts:
        p = jnp.exp2(s - m)
        l = l + p.sum(axis=-1, keepdims=True)
        pv = jnp.dot(p.astype(BF), v, preferred_element_type=F32)
        o = pv if o is None else o + pv
    return o / l


def _na_kernel(kb_ref, pid_ref, dr_ref, q_ref, k_ref, v_ref, kc_ref, vc_ref, cbx_ref, o_ref, bias_ref,
               *, nkeys, heads):
    i = pl.program_id(1)
    start = pl.multiple_of(kb_ref[i] * GRID_W, GRID_W)
    nq = q_ref.shape[0]
    rq, kr = nq // GRID_W, nkeys // GRID_W

    pid = pid_ref[i]
    @pl.when((i == 0) | (pid != pid_ref[jnp.maximum(i - 1, 0)]))
    def _build_bias():
        for qi in range(rq):
            for w in range(kr):
                a = dr_ref[(pid * rq + qi) * kr + w]
                for h in range(heads):
                    bias_ref[h, qi * GRID_W:(qi + 1) * GRID_W, w * GRID_W:(w + 1) * GRID_W] = cbx_ref[h, a]

    lane = lax.broadcasted_iota(jnp.int32, (nq, LANES), 1)
    low = lane < HEAD_DIM
    dn = (((1,), (1,)), ((), ()))
    for hp in range(heads // 2):
        sl = slice(hp * LANES, (hp + 1) * LANES)
        q2 = q_ref[:, sl]
        k2 = k_ref[pl.ds(start, nkeys), sl]
        v2 = v_ref[pl.ds(start, nkeys), sl]
        kc2 = kc_ref[:, sl]
        vc2 = vc_ref[:, sl]
        outs = []
        for hh in range(2):
            qm = jnp.where(low if hh == 0 else jnp.logical_not(low), q2, jnp.zeros_like(q2))
            s_w = lax.dot_general(qm, k2, dn, preferred_element_type=F32) + bias_ref[2 * hp + hh]
            s_c = lax.dot_general(qm, kc2, dn, preferred_element_type=F32)
            outs.append(_softmax_pv([(s_w, v2), (s_c, vc2)], (nq, 1)))
        o_ref[:, sl] = jnp.where(low, outs[0], outs[1]).astype(BF)


def _na_plan(rows, rq):
    wr = min(NA_ROWS, rows)
    kr = rq + wr - 1
    nblk = rows // rq
    kbs, pids, pats = [], [], []
    for blk in range(nblk):
        r0 = blk * rq
        rs = [int(np.clip(r0 + i - wr // 2, 0, rows - wr)) for i in range(rq)]
        kb = min(rs[0], rows - kr)
        pat = (tuple(r - kb for r in rs), r0 - kb)
        if pat not in pats:
            pats.append(pat)
        kbs.append(kb)
        pids.append(pats.index(pat))
    return np.asarray(kbs, np.int32), np.asarray(pids, np.int32), pats, kr, wr


def _na_bias(rpb, pats, rq, kr, wr):
    cols = np.arange(GRID_W)
    cstart = np.clip(cols - NA_COLS // 2, 0, GRID_W - NA_COLS)
    kc = np.arange(GRID_W)
    col_ok = (kc[None, :] >= cstart[:, None]) & (kc[None, :] < cstart[:, None] + NA_COLS)
    dc = kc[None, :] - cols[:, None] + NA_COLS - 1
    heads, nr, ncol = rpb.shape
    sel = (dc[:, :, None] == np.arange(ncol)[None, None, :]) & col_ok[:, :, None]
    cb = jnp.einsum('hab,ckb->hack', rpb.astype(F32), jnp.asarray(sel, F32), precision=lax.Precision.HIGHEST)
    cb = jnp.where(jnp.asarray(col_ok)[None, None], cb * math.log2(math.e), NEG)
    cbx = jnp.concatenate([cb, jnp.full((heads, 1, GRID_W, GRID_W), NEG, F32)], axis=1)
    slots = []
    for rs_off, r_off in pats:
        i = np.arange(rq)[:, None]
        w = np.arange(kr)[None, :]
        rso = np.asarray(rs_off)[:, None]
        row_ok = (w >= rso) & (w < rso + wr)
        slots.append(np.where(row_ok, w - r_off - i + NA_ROWS - 1, nr))
    return cbx, np.stack(slots).reshape(-1).astype(np.int32)


def _na_latent(q, k, v, kc, vc, rpb, *, batch, seq, n_ctx, rq):
    rows = seq // GRID_W
    heads = rpb.shape[0]
    kbs, pids, pats, kr, wr = _na_plan(rows, rq)
    cbx, slots = _na_bias(rpb, pats, rq, kr, wr)
    nblk = rows // rq
    nq = rq * GRID_W
    nkeys = kr * GRID_W
    w = q.shape[1]
    kern = functools.partial(_na_kernel, nkeys=nkeys, heads=heads)
    grid_spec = pltpu.PrefetchScalarGridSpec(
        num_scalar_prefetch=3,
        grid=(batch, nblk),
        in_specs=[pl.BlockSpec((nq, w), lambda b, i, *_: (b * nblk + i, 0)),
                  pl.BlockSpec((seq, w), lambda b, i, *_: (b, 0)),
                  pl.BlockSpec((seq, w), lambda b, i, *_: (b, 0)),
                  pl.BlockSpec((n_ctx, w), lambda b, i, *_: (b, 0)),
                  pl.BlockSpec((n_ctx, w), lambda b, i, *_: (b, 0)),
                  pl.BlockSpec(cbx.shape, lambda b, i, *_: (0, 0, 0, 0))],
        out_specs=pl.BlockSpec((nq, w), lambda b, i, *_: (b * nblk + i, 0)),
        scratch_shapes=[pltpu.VMEM((heads, nq, nkeys), F32)],
    )
    return pl.pallas_call(
        kern,
        grid_spec=grid_spec,
        out_shape=jax.ShapeDtypeStruct(q.shape, BF),
        compiler_params=_cparams("arbitrary", "arbitrary"),
        name="na_latent",
    )(jnp.asarray(kbs), jnp.asarray(pids), jnp.asarray(slots), q, k, v, kc, vc, cbx)


def _na_ctx_kernel(q_ref, k_ref, v_ref, o_ref, *, heads):
    nq = q_ref.shape[0]
    lane = lax.broadcasted_iota(jnp.int32, (nq, LANES), 1)
    low = lane < HEAD_DIM
    dn = (((1,), (1,)), ((), ()))
    for hp in range(heads // 2):
        sl = slice(hp * LANES, (hp + 1) * LANES)
        q2 = q_ref[:, sl]
        k2 = k_ref[:, sl]
        v2 = v_ref[:, sl]
        outs = []
        for hh in range(2):
            qm = jnp.where(low if hh == 0 else jnp.logical_not(low), q2, jnp.zeros_like(q2))
            s = lax.dot_general(qm, k2, dn, preferred_element_type=F32)
            outs.append(_softmax_pv([(s, v2)], (nq, 1)))
        o_ref[:, sl] = jnp.where(low, outs[0], outs[1]).astype(BF)


def _na_ctx(qc, kc, vc, *, batch, n_ctx, heads):
    w = qc.shape[1]
    spec = pl.BlockSpec((n_ctx, w), lambda b: (b, 0))
    return pl.pallas_call(
        functools.partial(_na_ctx_kernel, heads=heads),
        grid=(batch,),
        in_specs=[spec, spec, spec],
        out_specs=spec,
        out_shape=jax.ShapeDtypeStruct(qc.shape, BF),
        compiler_params=_cparams("parallel"),
        name="na_ctx",
    )(qc, kc, vc)


def _outproj_kernel(z_ref, pf_ref, na_ref, x_ref, gate_ref, gpost_ref,
                    wglu_ref, cs_ref, wfo_ref, wout_ref, o_ref, *, widths):
    w_ssm, w_fn, w_na = widths
    z = jnp.concatenate([z_ref[j] for j in range(w_ssm // LANES)], axis=1)
    gl = jnp.dot(z.astype(BF), wglu_ref[...], preferred_element_type=F32)
    y_ssm = (z * jax.nn.sigmoid(gl)).astype(BF)
    mixed = jnp.dot(pf_ref[...], cs_ref[...], preferred_element_type=F32).astype(BF)
    y_fft = jnp.dot(mixed, wfo_ref[...], preferred_element_type=F32).astype(BF)
    o = (jnp.dot(y_ssm, wout_ref[0:w_ssm, :], preferred_element_type=F32)
         + jnp.dot(y_fft, wout_ref[w_ssm:w_ssm + w_fn, :], preferred_element_type=F32)
         + jnp.dot(na_ref[...], wout_ref[w_ssm + w_fn:, :], preferred_element_type=F32))
    ms = jnp.mean(o * o, axis=-1, keepdims=True)
    o_ref[...] = x_ref[...] + gate_ref[...] * (o * lax.rsqrt(ms + EPS) * gpost_ref[...])


def _outproj(z3, pf, yna, x2d, mod3, g_post, w_glu, cs, w_fo, w_out, *,
             layer, tm, tiles_per_batch, mod_row, widths):
    n, d = x2d.shape
    w_ssm, w_fn, w_na = widths
    if mod_row is None:
        row = lambda i: i // tiles_per_batch
    else:
        row = lambda i: mod_row
    kern = functools.partial(_outproj_kernel, widths=widths)
    return pl.pallas_call(
        kern,
        grid=(n // tm,),
        in_specs=[pl.BlockSpec((w_ssm // LANES, tm, LANES), lambda i: (0, i, 0)),
                  pl.BlockSpec((tm, 2 * w_fn), lambda i: (i, 0)),
                  pl.BlockSpec((tm, w_na), lambda i: (i, 0)),
                  pl.BlockSpec((tm, d), lambda i: (i, 0)),
                  pl.BlockSpec((None, 1, d), lambda i: (row(i), 0, 2)),
                  pl.BlockSpec((1, d), lambda i: (0, 0)),
                  _layer_resident(w_glu, layer),
                  _resident(cs.shape, lambda i: (0, 0)),
                  _layer_resident(w_fo, layer),
                  _layer_resident(w_out, layer)],
        out_specs=pl.BlockSpec((tm, d), lambda i: (i, 0)),
        out_shape=jax.ShapeDtypeStruct((n, d), F32),
        compiler_params=_cparams("parallel"),
        name="outproj",
    )(z3, pf, yna, x2d, mod3, g_post.reshape(1, d), w_glu, cs, w_fo, w_out)


def _ffn_kernel(x_ref, sh_ref, sc_ref, gate_ref, gpre_ref, gpost_ref, wg_ref, wu_ref, wd_ref,
                o_ref, a_ref, *, nchunk):
    x = x_ref[...]
    ms = jnp.mean(x * x, axis=-1, keepdims=True)
    m = ((x * lax.rsqrt(ms + EPS) * gpre_ref[...]) * (1.0 + sc_ref[...]) + sh_ref[...]).astype(BF)
    fc = wg_ref.shape[1] // nchunk
    for c in range(nchunk):
        sl = slice(c * fc, (c + 1) * fc)
        g = jnp.dot(m, wg_ref[:, sl], preferred_element_type=F32)
        u = jnp.dot(m, wu_ref[:, sl], preferred_element_type=F32)
        a_ref[:, sl] = (g * jax.nn.sigmoid(g) * u).astype(BF)
    y = jnp.dot(a_ref[...], wd_ref[...], preferred_element_type=F32)
    ms2 = jnp.mean(y * y, axis=-1, keepdims=True)
    o_ref[...] = x + gate_ref[...] * (y * lax.rsqrt(ms2 + EPS) * gpost_ref[...])


def _ffn(x2d, mod3, g_pre, g_post, wg, wu, wd, *, layer, tm, tiles_per_batch, mod_row):
    n, d = x2d.shape
    dff = wg.shape[2]
    if mod_row is None:
        row = lambda i: i // tiles_per_batch
    else:
        row = lambda i: mod_row
    nchunk = 2 if dff % (2 * LANES) == 0 else 1
    return pl.pallas_call(
        functools.partial(_ffn_kernel, nchunk=nchunk),
        grid=(n // tm,),
        in_specs=[pl.BlockSpec((tm, d), lambda i: (i, 0)),
                  pl.BlockSpec((None, 1, d), lambda i: (row(i), 0, 3)),
                  pl.BlockSpec((None, 1, d), lambda i: (row(i), 0, 4)),
                  pl.BlockSpec((None, 1, d), lambda i: (row(i), 0, 5)),
                  pl.BlockSpec((1, d), lambda i: (0, 0)),
                  pl.BlockSpec((1, d), lambda i: (0, 0)),
                  _layer_resident(wg, layer),
                  _layer_resident(wu, layer),
                  _layer_resident(wd, layer)],
        out_specs=pl.BlockSpec((tm, d), lambda i: (i, 0)),
        out_shape=jax.ShapeDtypeStruct((n, d), F32),
        scratch_shapes=[pltpu.VMEM((tm, dff), BF)],
        compiler_params=_cparams("parallel"),
        name="ffn",
    )(x2d, mod3, mod3, mod3, g_pre.reshape(1, d), g_post.reshape(1, d), wg, wu, wd)


def _layer(x2d, xc2d, cstack, p, w, layer, *, batch, seq, n_ctx, last, rope_tabs, consts):
    d = x2d.shape[1]
    w_ssm = p["ssm_d"].shape[0]
    w_fn = w["w_fourier"].shape[1]
    w_na = (w["w_in"].shape[2] - w_ssm - w_fn) // 3
    widths = (w_ssm, w_fn, w_na)
    heads = w_na // HEAD_DIM
    tm = min(512, seq)
    tmc = min(512, batch * n_ctx)

    mod = _mod_rows(cstack, w["w_mod"], w["b_mod"], layer)
    mod3 = mod.reshape(mod.shape[0], 1, 6 * d)

    cos, sin = rope_tabs
    u3, f, q, k, v = _inproj(x2d, mod3, p["g_pre_mix"], w["w_in"], cos, sin, layer=layer, tm=tm,
                             tiles_per_batch=seq // tm, mod_row=None, widths=widths, rope=True)
    ones = jnp.ones((tmc, LANES), F32)
    u3c, fc, qc, kc, vc = _inproj(xc2d, mod3, p["g_pre_mix"], w["w_in"], ones, ones, layer=layer, tm=tmc,
                                  tiles_per_batch=1, mod_row=batch, widths=widths, rope=False)

    z3, zc3 = _ssm(u3, u3c, *w["ssm"], layer=layer, batch=batch, seq=seq, n_ctx=n_ctx)

    l2 = consts["l2"]
    l1 = seq // l2
    g = _dft_a(consts["dft_a"], f.reshape(batch, l2, l1 * w_fn))
    pf = _dft_b(consts["dft_b"], g.reshape(batch, 2, l2, l1, w_fn), kb=min(8, l2))
    pf = pf.reshape(batch * seq, 2 * w_fn)

    yna = _na_latent(q, k, v, kc, vc, p["na_rpb"], batch=batch, seq=seq, n_ctx=n_ctx, rq=consts["rq"])

    w_glu, w_fo, w_out = w["w_glu"], w["w_fourier"], w["w_out"]
    x2d = _outproj(z3, pf, yna, x2d, mod3, p["g_post_mix"], w_glu, consts["cs_lat"], w_fo, w_out,
                   layer=layer, tm=tm, tiles_per_batch=seq // tm, mod_row=None, widths=widths)
    wg, wu, wd = w["w_ffn_gate"], w["w_ffn_up"], w["w_ffn_down"]
    x2d = _ffn(x2d, mod3, p["g_pre_ffn"], p["g_post_ffn"], wg, wu, wd,
               layer=layer, tm=tm, tiles_per_batch=seq // tm, mod_row=None)

    if not last:
        pfc = _dft_ctx(consts["dft_ctx"], fc.reshape(batch, n_ctx, w_fn)).reshape(batch * n_ctx, 2 * w_fn)
        ynac = _na_ctx(qc, kc, vc, batch=batch, n_ctx=n_ctx, heads=heads)
        tc = min(tmc, n_ctx)
        xc2d = _outproj(zc3, pfc, ynac, xc2d, mod3, p["g_post_mix"], w_glu, consts["cs_ctx"], w_fo, w_out,
                        layer=layer, tm=tc, tiles_per_batch=n_ctx // tc, mod_row=batch, widths=widths)
        xc2d = _ffn(xc2d, mod3, p["g_pre_ffn"], p["g_post_ffn"], wg, wu, wd,
                    layer=layer, tm=tc, tiles_per_batch=n_ctx // tc, mod_row=batch)
    else:
        xc2d = None
    return x2d, xc2d


def kernel(x, c, ctx, c_ctx, w_mod, b_mod, g_pre_mix, g_post_mix, w_in, ssm_a_re, ssm_a_im, ssm_log_dt,
           ssm_b_re, ssm_b_im, ssm_c_re, ssm_c_im, ssm_d, w_glu, w_fourier, na_rpb, w_out, g_pre_ffn,
           g_post_ffn, w_ffn_gate, w_ffn_up, w_ffn_down):
    batch, seq, d = x.shape
    n_ctx = ctx.shape[1]
    depth = w_mod.shape[0]
    w_fn = w_fourier.shape[1]
    params = dict(g_pre_mix=g_pre_mix, g_post_mix=g_post_mix, ssm_d=ssm_d, na_rpb=na_rpb,
                  g_pre_ffn=g_pre_ffn, g_post_ffn=g_post_ffn)
    seg_len = (seq + 2 * n_ctx) // SSM_T // SSM_SEG
    ssm_ops = jax.vmap(functools.partial(_ssm_weights, seg_len=seg_len))(
        ssm_a_re, ssm_a_im, ssm_log_dt, ssm_b_re, ssm_b_im, ssm_c_re, ssm_c_im, ssm_d)
    weights = dict(w_mod=w_mod, b_mod=b_mod, ssm=ssm_ops,
                   w_in=_cast_bf16(w_in), w_glu=_cast_bf16(w_glu), w_fourier=_cast_bf16(w_fourier),
                   w_out=_cast_bf16(w_out), w_ffn_gate=_cast_bf16(w_ffn_gate),
                   w_ffn_up=_cast_bf16(w_ffn_up), w_ffn_down=_cast_bf16(w_ffn_down))

    nrow = -(-(batch + 1) // 8) * 8
    cstack = jnp.concatenate([c, c_ctx[None, :], jnp.zeros((nrow - batch - 1, d), c.dtype)], axis=0)

    l2 = 64 if seq % (64 * 8) == 0 else 8
    l1 = seq // l2
    consts = dict(
        l2=l2,
        rq=4,
        dft_a=_dft_rows(l2).astype(BF),
        dft_b=_dft_b_mats(seq, l1, l2).astype(BF),
        dft_ctx=_dft_rows(n_ctx).astype(BF),
        cs_lat=_channel_dft(w_fn, seq).astype(BF),
        cs_ctx=_channel_dft(w_fn, n_ctx).astype(BF),
    )
    rope_tabs = _rope_tables(seq)

    x2d = x.reshape(batch * seq, d)
    xc2d = ctx.reshape(batch * n_ctx, d)
    for layer in range(depth):
        p = {name: val[layer] for name, val in params.items()}
        x2d, xc2d = _layer(x2d, xc2d, cstack, p, weights, layer, batch=batch, seq=seq, n_ctx=n_ctx,
                           last=(layer == depth - 1), rope_tabs=rope_tabs, consts=consts)
    return x2d.reshape(batch, seq, d)
```

```python
import functools
import math

import numpy as np
import jax
import jax.numpy as jnp
from jax import lax
from jax.experimental import pallas as pl
from jax.experimental.pallas import tpu as pltpu

BF = jnp.bfloat16
F32 = jnp.float32

EPS = 1e-6
GRID_W = 64
HEAD_DIM = 64
NA_ROWS = 8
NA_COLS = 16
ROPE_BASE = 10000.0
SSM_GROUP = 16
SSM_STATE = 64
FNET_GROUP = 64
LANES = 128
MXU_TILE = 256
SSM_T = 8
SSM_SEG = 8
NEG = -1e30
VMEM_LIMIT = 56 * 1024 * 1024


def _cparams(*sem):
    return pltpu.CompilerParams(dimension_semantics=sem, vmem_limit_bytes=VMEM_LIMIT)


def _resident(shape, index_map):
    return pl.BlockSpec(shape, index_map, pipeline_mode=pl.Buffered(1))


def _layer_resident(w_all, layer):
    return pl.BlockSpec((None,) + w_all.shape[1:], lambda *_: (layer, 0, 0), pipeline_mode=pl.Buffered(1))


def _cast_kernel(w_ref, o_ref):
    o_ref[...] = w_ref[...].astype(BF)


def _cast_bf16(w_all):
    depth, rows, cols = w_all.shape
    br = rows
    while br * cols * 4 > 4 * 1024 * 1024 and br % 32 == 0:
        br //= 2
    spec = pl.BlockSpec((None, br, cols), lambda l, i: (l, i, 0))
    return pl.pallas_call(
        _cast_kernel,
        grid=(depth, rows // br),
        in_specs=[spec],
        out_specs=spec,
        out_shape=jax.ShapeDtypeStruct(w_all.shape, BF),
        compiler_params=_cparams("parallel", "parallel"),
        name="cast_bf16",
    )(w_all)


def _mod_kernel(c_ref, w_ref, b_ref, o_ref):
    cs = c_ref[...]
    s = (cs * jax.nn.sigmoid(cs)).astype(BF)
    o_ref[...] = jnp.dot(s, w_ref[...].astype(BF), preferred_element_type=F32) + b_ref[...]


def _mod_rows(cstack, w_mod, b_mod, layer):
    rows, d = cstack.shape
    depth, _, n = w_mod.shape
    return pl.pallas_call(
        _mod_kernel,
        grid=(n // d,),
        in_specs=[pl.BlockSpec((rows, d), lambda i: (0, 0)),
                  pl.BlockSpec((None, d, d), lambda i: (layer, 0, i)),
                  pl.BlockSpec((None, 1, d), lambda i: (layer, 0, i))],
        out_specs=pl.BlockSpec((rows, d), lambda i: (0, i)),
        out_shape=jax.ShapeDtypeStruct((rows, n), F32),
        compiler_params=_cparams("arbitrary"),
        name="mod_rows",
    )(cstack, w_mod, b_mod.reshape(depth, 1, n))


def _rope_tile(t, cos, sin, first):
    partner = jnp.where(first, pltpu.roll(t, LANES - 16, 1), pltpu.roll(t, 16, 1))
    return t * cos + partner * sin


def _inproj_kernel(x_ref, sh_ref, sc_ref, g_ref, w_ref, cos_ref, sin_ref,
                   u_ref, f_ref, q_ref, k_ref, v_ref, *, widths, rope, sub):
    w_ssm, w_fn, w_na = widths
    scale = HEAD_DIM ** -0.5 * math.log2(math.e)
    for r0 in range(0, x_ref.shape[0], sub):
        rs = slice(r0, r0 + sub)
        x = x_ref[rs, :]
        ms = jnp.mean(x * x, axis=-1, keepdims=True)
        xn = x * lax.rsqrt(ms + EPS) * g_ref[...]
        m = (xn * (1.0 + sc_ref[...]) + sh_ref[...]).astype(BF)

        h = jnp.dot(m, w_ref[...], preferred_element_type=F32)

        def proj(lo, n, h=h):
            return h[:, lo:lo + n]

        u = proj(0, w_ssm)
        for j in range(w_ssm // LANES):
            u_ref[j, rs, :] = u[:, j * LANES:(j + 1) * LANES]
        f_ref[rs, :] = proj(w_ssm, w_fn).astype(BF)
        q = proj(w_ssm + w_fn, w_na)
        k = proj(w_ssm + w_fn + w_na, w_na)
        v_ref[rs, :] = proj(w_ssm + w_fn + 2 * w_na, w_na).astype(BF)
        if rope:
            cos = cos_ref[rs, :]
            sin = sin_ref[rs, :]
            lane = lax.broadcasted_iota(jnp.int32, cos.shape, 1)
            first = (lane % 32) < 16
            for j in range(w_na // LANES):
                sl = slice(j * LANES, (j + 1) * LANES)
                q_ref[rs, sl] = (_rope_tile(q[:, sl], cos, sin, first) * scale).astype(BF)
                k_ref[rs, sl] = _rope_tile(k[:, sl], cos, sin, first).astype(BF)
        else:
            q_ref[rs, :] = (q * scale).astype(BF)
            k_ref[rs, :] = k.astype(BF)


def _inproj(x2d, mod3, g_pre, w_in, cos, sin, *, layer, tm, tiles_per_batch, mod_row, widths, rope):
    n, d = x2d.shape
    w_ssm, w_fn, w_na = widths
    nt = n // tm
    if mod_row is None:
        row = lambda i: i // tiles_per_batch
    else:
        row = lambda i: mod_row
    pos = lambda i: (i % tiles_per_batch, 0)
    kern = functools.partial(_inproj_kernel, widths=widths, rope=rope, sub=min(tm, 512))
    return pl.pallas_call(
        kern,
        grid=(nt,),
        in_specs=[pl.BlockSpec((tm, d), lambda i: (i, 0)),
                  pl.BlockSpec((None, 1, d), lambda i: (row(i), 0, 0)),
                  pl.BlockSpec((None, 1, d), lambda i: (row(i), 0, 1)),
                  pl.BlockSpec((1, d), lambda i: (0, 0)),
                  _layer_resident(w_in, layer),
                  pl.BlockSpec((tm, LANES), pos),
                  pl.BlockSpec((tm, LANES), pos)],
        out_specs=[pl.BlockSpec((w_ssm // LANES, tm, LANES), lambda i: (0, i, 0)),
                   pl.BlockSpec((tm, w_fn), lambda i: (i, 0)),
                   pl.BlockSpec((tm, w_na), lambda i: (i, 0)),
                   pl.BlockSpec((tm, w_na), lambda i: (i, 0)),
                   pl.BlockSpec((tm, w_na), lambda i: (i, 0))],
        out_shape=[jax.ShapeDtypeStruct((w_ssm // LANES, n, LANES), F32),
                   jax.ShapeDtypeStruct((n, w_fn), BF),
                   jax.ShapeDtypeStruct((n, w_na), BF),
                   jax.ShapeDtypeStruct((n, w_na), BF),
                   jax.ShapeDtypeStruct((n, w_na), BF)],
        compiler_params=_cparams("parallel"),
        name="inproj_rope" if rope else "inproj_ctx",
    )(x2d, mod3, mod3, g_pre.reshape(1, d), w_in, cos, sin)


def _rope_tables(seq):
    t = np.arange(seq)
    row = (t // GRID_W).astype(np.float32)
    col = (t % GRID_W).astype(np.float32)
    quarter = HEAD_DIM // 4
    freqs = (np.float32(ROPE_BASE) ** (-np.arange(quarter, dtype=np.float32) / quarter)).astype(np.float32)
    d = np.arange(LANES) % HEAD_DIM
    use_col = (d // (HEAD_DIM // 2)) == 1
    fidx = d % quarter
    sign = np.where((d % (HEAD_DIM // 2)) // quarter == 0, -1.0, 1.0)
    pos = np.where(use_col[None, :], col[:, None], row[:, None])
    ang = (pos * freqs[fidx][None, :]).astype(np.float32).astype(np.float64)
    return jnp.asarray(np.cos(ang), F32), jnp.asarray(np.sin(ang) * sign[None, :], F32)


def _gelu_tanh(x):
    c = math.sqrt(2.0 / math.pi)
    return x * (0.5 * (1.0 + jnp.tanh(c * (x + 0.044715 * (x * x * x)))))


def _group_of(idx, width, groups):
    shift = width.bit_length() - 1
    assert width == 1 << shift and groups & (groups - 1) == 0
    return lax.bitwise_and(lax.shift_right_logical(idx, shift), groups - 1)


def _same_group(shape, row0, row_width, col0, col_width, groups):
    r = lax.broadcasted_iota(jnp.int32, shape, 0) + row0
    c = lax.broadcasted_iota(jnp.int32, shape, 1) + col0
    return _group_of(r, row_width, groups) == _group_of(c, col_width, groups)


def _ssm_kernel(ul_ref, uc_ref, wc_ref, kc_ref, dc_ref, rexp_ref, cexp_ref, lam_ref, lamseg_ref,
                z_ref, zc_ref, u_s, bm_s, kin_s, cm_s, s_ref, h_ref, e_ref, *, seg_len, n_lat, n_ctx):
    T = SSM_T
    rows, tw = u_s.shape
    nt = s_ref.shape[0]
    tpq = nt // 4
    nst = tpq * LANES
    gpt = LANES // SSM_GROUP
    cw = 512

    def lanes(c):
        return slice(c * LANES, (c + 1) * LANES)

    @pl.when(pl.program_id(1) == 0)
    def _expand_operators():
        rexp = rexp_ref[...]
        for c0 in range(0, 4 * nst, cw):
            blk = jnp.dot(rexp, wc_ref[:, c0:c0 + cw], preferred_element_type=F32)
            keep = _same_group(blk.shape, 0, SSM_GROUP, c0, SSM_STATE, gpt)
            bm_s[:, c0:c0 + cw] = jnp.where(keep, blk, 0.0).astype(BF)
        for c0 in range(0, tw, cw):
            blk = jnp.dot(rexp, kc_ref[:, c0:c0 + cw], preferred_element_type=F32)
            keep = _same_group(blk.shape, 0, SSM_GROUP, c0, SSM_GROUP, gpt)
            kin_s[:, c0:c0 + cw] = jnp.where(keep, blk, 0.0).astype(BF)
        cexp = cexp_ref[...]
        for r0 in range(0, 4 * nst, cw):
            blk = jnp.dot(dc_ref[r0:r0 + cw, :], cexp, preferred_element_type=F32)
            keep = _same_group(blk.shape, r0, SSM_STATE, 0, SSM_GROUP, gpt)
            cm_s[r0:r0 + cw, :] = jnp.where(keep, blk, 0.0).astype(BF)

    for t in range(T):
        ct = uc_ref[pl.ds(t, n_ctx, stride=T), :].astype(BF)
        u_s[0:n_ctx, lanes(t)] = ct
        u_s[n_ctx:n_ctx + n_lat, lanes(t)] = ul_ref[pl.ds(t, n_lat, stride=T), :].astype(BF)
        u_s[n_ctx + n_lat:rows, lanes(t)] = ct
    u = u_s[...]
    for q in range(4):
        r = jnp.dot(u, bm_s[:, q * nst:(q + 1) * nst], preferred_element_type=F32)
        for t in range(tpq):
            s_ref[q * tpq + t] = r[:, t * LANES:(t + 1) * LANES]

    def scan_step(i, carry, record):
        new = list(carry)
        for d, ii in ((0, i), (1, seg_len - 1 - i)):
            rsl = pl.ds(ii, SSM_SEG, stride=seg_len)
            for t in range(tpq):
                cr, ci = 2 * d * tpq + t, (2 * d + 1) * tpq + t
                hr, hi = carry[cr], carry[ci]
                if record:
                    h_ref[cr, rsl, :] = hr
                    h_ref[ci, rsl, :] = hi
                lr, li = lam_ref[:, lanes(cr)], lam_ref[:, lanes(ci)]
                new[cr] = lr * hr - li * hi + s_ref[cr, rsl, :]
                new[ci] = lr * hi + li * hr + s_ref[ci, rsl, :]
        return tuple(new)

    zero = jnp.zeros((SSM_SEG, LANES), F32)
    fin = lax.fori_loop(0, seg_len, lambda i, c: scan_step(i, c, False), (zero,) * nt, unroll=2)
    for c in range(nt):
        e_ref[c, 0:SSM_SEG, :] = fin[c]

    for d in range(2):
        for t in range(tpq):
            cr, ci = 2 * d * tpq + t, (2 * d + 1) * tpq + t
            lr, li = lamseg_ref[:, lanes(cr)], lamseg_ref[:, lanes(ci)]
            er = jnp.zeros((1, LANES), F32)
            ei = jnp.zeros((1, LANES), F32)
            for n in range(SSM_SEG):
                s = n if d == 0 else SSM_SEG - 1 - n
                e_ref[cr, SSM_SEG + s:SSM_SEG + s + 1, :] = er
                e_ref[ci, SSM_SEG + s:SSM_SEG + s + 1, :] = ei
                fr, fi = e_ref[cr, s:s + 1, :], e_ref[ci, s:s + 1, :]
                er, ei = lr * er - li * ei + fr, lr * ei + li * er + fi

    ent = tuple(e_ref[c, SSM_SEG:2 * SSM_SEG, :] for c in range(nt))
    lax.fori_loop(0, seg_len, lambda i, c: scan_step(i, c, True), ent, unroll=2)

    hcat = jnp.concatenate([h_ref[c].astype(BF) for c in range(nt)], axis=1)
    for t0 in range(0, T, 2):
        cs = slice(t0 * LANES, (t0 + 2) * LANES)
        y = (jnp.dot(u, kin_s[:, cs], preferred_element_type=F32)
             + jnp.dot(hcat, cm_s[:, cs], preferred_element_type=F32))
        g = _gelu_tanh(y)
        for t in (t0, t0 + 1):
            z_ref[pl.ds(t, n_lat, stride=T), :] = g[n_ctx:n_ctx + n_lat, lanes(t - t0)]

    uc = u_s[0:n_ctx, :]
    hc = jnp.concatenate([h_ref[c, 0:n_ctx, :].astype(BF) for c in range(2 * tpq)]
                         + [h_ref[c, rows - n_ctx:rows, :].astype(BF) for c in range(2 * tpq, nt)], axis=1)
    gc = _gelu_tanh(jnp.dot(uc, kin_s[...], preferred_element_type=F32)
                    + jnp.dot(hc, cm_s[...], preferred_element_type=F32))
    for t in range(T):
        zc_ref[pl.ds(t, n_ctx, stride=T), :] = gc[:, lanes(t)]


def _ssm_weights(a_re, a_im, log_dt, b_re, b_im, c_re, c_im, d_skip, seg_len):
    T = SSM_T
    g, p = a_re.shape[1], a_re.shape[2]
    hc = b_re.shape[-1]
    gpt = LANES // hc
    J = g // gpt
    lam = lax.complex(a_re.astype(F32), a_im.astype(F32))
    dt = jnp.exp(log_dt.astype(F32))[..., None]
    ldt = lam * dt
    lam_bar = jnp.exp(ldt)
    bbar = ((lam_bar - 1) / lam)[..., None] * lax.complex(b_re.astype(F32), b_im.astype(F32))
    cmat = lax.complex(c_re.astype(F32), c_im.astype(F32))
    kk = jnp.arange(T + 1, dtype=F32)
    pw = jnp.exp(ldt[:, None] * kk[None, :, None, None])

    wf = pw[0, T - 1 - jnp.arange(T)][:, :, None, :] * jnp.swapaxes(bbar[0], 1, 2)[None]
    wb = pw[1, jnp.arange(T)][:, :, None, :] * jnp.swapaxes(bbar[1], 1, 2)[None]
    wq = jnp.stack([wf.real, wf.imag, wb.real, wb.imag]).reshape(4, T, J, gpt, hc, p)
    wc = wq.transpose(2, 1, 4, 0, 3, 5).reshape(J, T * hc, 4 * gpt * p)

    df = cmat[0][None] * pw[0, 1 + jnp.arange(T)][:, :, None, :]
    db = cmat[1][None] * pw[1, T - jnp.arange(T)][:, :, None, :]
    dq = jnp.stack([df.real, -df.imag, db.real, -db.imag]).reshape(4, T, J, gpt, hc, p)
    dc = dq.transpose(2, 0, 3, 5, 1, 4).reshape(J, 4 * gpt * p, T * hc)

    mf = jnp.einsum('gcp,kgp,gph->kgch', cmat[0], pw[0, :T], bbar[0]).real
    mb = jnp.einsum('gcp,kgp,gph->kgch', cmat[1], pw[1, :T], bbar[1]).real
    skip = d_skip.astype(F32).reshape(g, hc)[:, :, None] * jnp.eye(hc, dtype=F32)[None]
    m0 = mf[0] + mb[0] + skip
    lags = jnp.concatenate([mb[1:][::-1], m0[None], mf[1:]], axis=0)
    tin = np.arange(T)[:, None]
    tout = np.arange(T)[None, :]
    kt = lags[jnp.asarray(tout - tin + T - 1)]
    kt = kt.reshape(T, T, J, gpt, hc, hc)
    kc = kt.transpose(2, 0, 5, 1, 3, 4).reshape(J, T * hc, T * LANES)

    def lam_rows(power):
        lp = jnp.exp(ldt * power)
        lt = lp.reshape(2, J, gpt * p)
        return jnp.concatenate([lt[0].real, lt[0].imag, lt[1].real, lt[1].imag], axis=-1)[:, None, :]

    lam_t = jnp.broadcast_to(lam_rows(float(T)), (J, SSM_SEG, 4 * gpt * p))
    return wc.astype(BF), kc.astype(BF), dc.astype(BF), lam_t, lam_rows(float(T * seg_len))


def _ssm_expanders():
    T = SSM_T
    r = np.arange(T * LANES)
    rexp = np.zeros((T * LANES, T * SSM_GROUP), np.float32)
    rexp[r, (r // LANES) * SSM_GROUP + r % SSM_GROUP] = 1.0
    return jnp.asarray(rexp, BF), jnp.asarray(rexp.T, BF)


def _ssm(u3, u3c, wc, kc, dc, lam, lamseg, *, layer, batch, seq, n_ctx):
    T = SSM_T
    J = u3.shape[0]
    tw = T * LANES
    ns = wc.shape[-1]
    n_lat, n_c = seq // T, n_ctx // T
    rows = n_lat + 2 * n_c
    seg_len = rows // SSM_SEG
    assert rows % SSM_SEG == 0 and n_c % 16 == 0
    rexp, cexp = _ssm_expanders()
    kern = functools.partial(_ssm_kernel, seg_len=seg_len, n_lat=n_lat, n_ctx=n_c)
    const = lambda j, b: (0, 0)
    per_j = lambda j, b: (layer, j, 0, 0)
    per_jb = lambda j, b: (j, b, 0)
    return pl.pallas_call(
        kern,
        grid=(J, batch),
        in_specs=[pl.BlockSpec((None, seq, LANES), per_jb, pipeline_mode=pl.Buffered(1)),
                  pl.BlockSpec((None, n_ctx, LANES), per_jb),
                  pl.BlockSpec((None, None) + wc.shape[2:], per_j),
                  pl.BlockSpec((None, None) + kc.shape[2:], per_j),
                  pl.BlockSpec((None, None) + dc.shape[2:], per_j),
                  pl.BlockSpec(rexp.shape, const),
                  pl.BlockSpec(cexp.shape, const),
                  pl.BlockSpec((None, None, SSM_SEG, ns), per_j),
                  pl.BlockSpec((None, None, 1, ns), per_j)],
        out_specs=[pl.BlockSpec((None, seq, LANES), per_jb),
                   pl.BlockSpec((None, n_ctx, LANES), per_jb)],
        out_shape=[jax.ShapeDtypeStruct(u3.shape, F32),
                   jax.ShapeDtypeStruct(u3c.shape, F32)],
        scratch_shapes=[pltpu.VMEM((rows, tw), BF),
                        pltpu.VMEM((tw, ns), BF),
                        pltpu.VMEM((tw, tw), BF),
                        pltpu.VMEM((ns, tw), BF),
                        pltpu.VMEM((ns // LANES, rows, LANES), F32),
                        pltpu.VMEM((ns // LANES, rows, LANES), F32),
                        pltpu.VMEM((ns // LANES, 2 * SSM_SEG, LANES), F32)],
        compiler_params=_cparams("arbitrary", "arbitrary"),
        name="ssm_chunk_scan",
    )(u3, u3c, wc, kc, dc, rexp, cexp, lam, lamseg)


def _dft_a_kernel(w_ref, x_ref, o_ref):
    o_ref[...] = jnp.dot(w_ref[...], x_ref[...], preferred_element_type=F32).astype(BF)


def _dft_rows(n):
    k = np.arange(n)
    m = (k[:, None] * k[None, :]) % n
    ang = m * (2.0 * math.pi / n)
    return jnp.asarray(np.concatenate([np.cos(ang), -np.sin(ang)], axis=0), BF)


def _dft_a(w, x3):
    B, n, cols = x3.shape
    tn = min(cols, 4096)
    return pl.pallas_call(
        _dft_a_kernel,
        grid=(B, cols // tn),
        in_specs=[pl.BlockSpec((2 * n, n), lambda b, i: (0, 0)),
                  pl.BlockSpec((None, n, tn), lambda b, i: (b, 0, i))],
        out_specs=pl.BlockSpec((None, 2 * n, tn), lambda b, i: (b, 0, i)),
        out_shape=jax.ShapeDtypeStruct((B, 2 * n, cols), BF),
        compiler_params=_cparams("parallel", "parallel"),
        name="dft_stage_a",
    )(w, x3)


def _dft_b_kernel(a_ref, g_ref, o_ref, *, kb, l1, wf):
    for i in range(kb):
        g = jnp.concatenate([g_ref[0, i], g_ref[1, i]], axis=0)
        p = jnp.dot(a_ref[i], g, preferred_element_type=F32)
        o_ref[:, i * 2 * wf:i * 2 * wf + wf] = p[:l1].astype(BF)
        o_ref[:, i * 2 * wf + wf:(i + 1) * 2 * wf] = p[l1:].astype(BF)


def _dft_b_mats(seq, l1, l2):
    k2 = np.arange(l2)[:, None, None]
    k1 = np.arange(l1)[None, :, None]
    j1 = np.arange(l1)[None, None, :]
    m = ((k1 * l2 + k2) * j1) % seq
    ang = m * (2.0 * math.pi / seq)
    ar, ai = np.cos(ang).astype(np.float32), -np.sin(ang).astype(np.float32)
    top = np.concatenate([ar, -ai], axis=2)
    bot = np.concatenate([ai, ar], axis=2)
    return jnp.asarray(np.concatenate([top, bot], axis=1), BF)


def _dft_b(amat, g5, *, kb):
    B, _, l2, l1, wf = g5.shape
    kern = functools.partial(_dft_b_kernel, kb=kb, l1=l1, wf=wf)
    return pl.pallas_call(
        kern,
        grid=(B, l2 // kb),
        in_specs=[pl.BlockSpec((kb, 2 * l1, 2 * l1), lambda b, i: (i, 0, 0)),
                  pl.BlockSpec((None, 2, kb, l1, wf), lambda b, i: (b, 0, i, 0, 0))],
        out_specs=pl.BlockSpec((None, l1, kb * 2 * wf), lambda b, i: (b, 0, i)),
        out_shape=jax.ShapeDtypeStruct((B, l1, l2 * 2 * wf), BF),
        compiler_params=_cparams("parallel", "parallel"),
        name="dft_stage_b",
    )(amat, g5)


def _dft_ctx_kernel(w_ref, x_ref, o_ref, *, n, wf):
    p = jnp.dot(w_ref[...], x_ref[...], preferred_element_type=F32)
    o_ref[:, 0:wf] = p[:n].astype(BF)
    o_ref[:, wf:2 * wf] = p[n:].astype(BF)


def _dft_ctx(w, x3):
    B, n, wf = x3.shape
    kern = functools.partial(_dft_ctx_kernel, n=n, wf=wf)
    return pl.pallas_call(
        kern,
        grid=(B,),
        in_specs=[pl.BlockSpec((2 * n, n), lambda b: (0, 0)),
                  pl.BlockSpec((None, n, wf), lambda b: (b, 0, 0))],
        out_specs=pl.BlockSpec((None, n, 2 * wf), lambda b: (b, 0, 0)),
        out_shape=jax.ShapeDtypeStruct((B, n, 2 * wf), BF),
        compiler_params=_cparams("parallel"),
        name="dft_ctx",
    )(w, x3)


def _channel_dft(wf, seq):
    c = np.arange(wf)
    same = (c[:, None] // FNET_GROUP) == (c[None, :] // FNET_GROUP)
    m = ((c[:, None] % FNET_GROUP) * (c[None, :] % FNET_GROUP)) % FNET_GROUP
    ang = m * (2.0 * math.pi / FNET_GROUP)
    mask = same / math.sqrt(seq * FNET_GROUP)
    return jnp.asarray(np.concatenate([np.cos(ang) * mask, np.sin(ang) * mask], axis=0), BF)


def _softmax_pv(parts, l_shape):
    m = parts[0][0].max(axis=-1, keepdims=True)
    for s, _ in parts[1:]:
        m = jnp.maximum(m, s.max(axis=-1, keepdims=True))
    l = jnp.zeros(l_shape, F32)
    o = None
    for s, v in parts:
        p = jnp.exp2(s - m)
        l = l + p.sum(axis=-1, keepdims=True)
        pv = jnp.dot(p.astype(BF), v, preferred_element_type=F32)
        o = pv if o is None else o + pv
    return o / l


def _na_kernel(kb_ref, pid_ref, dr_ref, q_ref, k_ref, v_ref, kc_ref, vc_ref, cbx_ref, o_ref, bias_ref,
               *, nkeys, heads):
    i = pl.program_id(1)
    start = pl.multiple_of(kb_ref[i] * GRID_W, GRID_W)
    nq = q_ref.shape[0]
    rq, kr = nq // GRID_W, nkeys // GRID_W

    pid = pid_ref[i]
    @pl.when((i == 0) | (pid != pid_ref[jnp.maximum(i - 1, 0)]))
    def _build_bias():
        for qi in range(rq):
            for w in range(kr):
                a = dr_ref[(pid * rq + qi) * kr + w]
                for h in range(heads):
                    bias_ref[h, qi * GRID_W:(qi + 1) * GRID_W, w * GRID_W:(w + 1) * GRID_W] = cbx_ref[h, a]

    lane = lax.broadcasted_iota(jnp.int32, (nq, LANES), 1)
    low = lane < HEAD_DIM
    dn = (((1,), (1,)), ((), ()))
    for hp in range(heads // 2):
        sl = slice(hp * LANES, (hp + 1) * LANES)
        q2 = q_ref[:, sl]
        k2 = k_ref[pl.ds(start, nkeys), sl]
        v2 = v_ref[pl.ds(start, nkeys), sl]
        kc2 = kc_ref[:, sl]
        vc2 = vc_ref[:, sl]
        outs = []
        for hh in range(2):
            qm = jnp.where(low if hh == 0 else jnp.logical_not(low), q2, jnp.zeros_like(q2))
            s_w = lax.dot_general(qm, k2, dn, preferred_element_type=F32) + bias_ref[2 * hp + hh]
            s_c = lax.dot_general(qm, kc2, dn, preferred_element_type=F32)
            outs.append(_softmax_pv([(s_w, v2), (s_c, vc2)], (nq, 1)))
        o_ref[:, sl] = jnp.where(low, outs[0], outs[1]).astype(BF)


def _na_plan(rows, rq):
    wr = min(NA_ROWS, rows)
    kr = rq + wr - 1
    nblk = rows // rq
    kbs, pids, pats = [], [], []
    for blk in range(nblk):
        r0 = blk * rq
        rs = [int(np.clip(r0 + i - wr // 2, 0, rows - wr)) for i in range(rq)]
        kb = min(rs[0], rows - kr)
        pat = (tuple(r - kb for r in rs), r0 - kb)
        if pat not in pats:
            pats.append(pat)
        kbs.append(kb)
        pids.append(pats.index(pat))
    return np.asarray(kbs, np.int32), np.asarray(pids, np.int32), pats, kr, wr


def _na_bias(rpb, pats, rq, kr, wr):
    cols = np.arange(GRID_W)
    cstart = np.clip(cols - NA_COLS // 2, 0, GRID_W - NA_COLS)
    kc = np.arange(GRID_W)
    col_ok = (kc[None, :] >= cstart[:, None]) & (kc[None, :] < cstart[:, None] + NA_COLS)
    dc = kc[None, :] - cols[:, None] + NA_COLS - 1
    heads, nr, ncol = rpb.shape
    sel = (dc[:, :, None] == np.arange(ncol)[None, None, :]) & col_ok[:, :, None]
    cb = jnp.einsum('hab,ckb->hack', rpb.astype(F32), jnp.asarray(sel, F32), precision=lax.Precision.HIGHEST)
    cb = jnp.where(jnp.asarray(col_ok)[None, None], cb * math.log2(math.e), NEG)
    cbx = jnp.concatenate([cb, jnp.full((heads, 1, GRID_W, GRID_W), NEG, F32)], axis=1)
    slots = []
    for rs_off, r_off in pats:
        i = np.arange(rq)[:, None]
        w = np.arange(kr)[None, :]
        rso = np.asarray(rs_off)[:, None]
        row_ok = (w >= rso) & (w < rso + wr)
        slots.append(np.where(row_ok, w - r_off - i + NA_ROWS - 1, nr))
    return cbx, np.stack(slots).reshape(-1).astype(np.int32)


def _na_latent(q, k, v, kc, vc, rpb, *, batch, seq, n_ctx, rq):
    rows = seq // GRID_W
    heads = rpb.shape[0]
    kbs, pids, pats, kr, wr = _na_plan(rows, rq)
    cbx, slots = _na_bias(rpb, pats, rq, kr, wr)
    nblk = rows // rq
    nq = rq * GRID_W
    nkeys = kr * GRID_W
    w = q.shape[1]
    kern = functools.partial(_na_kernel, nkeys=nkeys, heads=heads)
    grid_spec = pltpu.PrefetchScalarGridSpec(
        num_scalar_prefetch=3,
        grid=(batch, nblk),
        in_specs=[pl.BlockSpec((nq, w), lambda b, i, *_: (b * nblk + i, 0)),
                  pl.BlockSpec((seq, w), lambda b, i, *_: (b, 0)),
                  pl.BlockSpec((seq, w), lambda b, i, *_: (b, 0)),
                  pl.BlockSpec((n_ctx, w), lambda b, i, *_: (b, 0)),
                  pl.BlockSpec((n_ctx, w), lambda b, i, *_: (b, 0)),
                  pl.BlockSpec(cbx.shape, lambda b, i, *_: (0, 0, 0, 0))],
        out_specs=pl.BlockSpec((nq, w), lambda b, i, *_: (b * nblk + i, 0)),
        scratch_shapes=[pltpu.VMEM((heads, nq, nkeys), F32)],
    )
    return pl.pallas_call(
        kern,
        grid_spec=grid_spec,
        out_shape=jax.ShapeDtypeStruct(q.shape, BF),
        compiler_params=_cparams("arbitrary", "arbitrary"),
        name="na_latent",
    )(jnp.asarray(kbs), jnp.asarray(pids), jnp.asarray(slots), q, k, v, kc, vc, cbx)


def _na_ctx_kernel(q_ref, k_ref, v_ref, o_ref, *, heads):
    nq = q_ref.shape[0]
    lane = lax.broadcasted_iota(jnp.int32, (nq, LANES), 1)
    low = lane < HEAD_DIM
    dn = (((1,), (1,)), ((), ()))
    for hp in range(heads // 2):
        sl = slice(hp * LANES, (hp + 1) * LANES)
        q2 = q_ref[:, sl]
        k2 = k_ref[:, sl]
        v2 = v_ref[:, sl]
        outs = []
        for hh in range(2):
            qm = jnp.where(low if hh == 0 else jnp.logical_not(low), q2, jnp.zeros_like(q2))
            s = lax.dot_general(qm, k2, dn, preferred_element_type=F32)
            outs.append(_softmax_pv([(s, v2)], (nq, 1)))
        o_ref[:, sl] = jnp.where(low, outs[0], outs[1]).astype(BF)


def _na_ctx(qc, kc, vc, *, batch, n_ctx, heads):
    w = qc.shape[1]
    spec = pl.BlockSpec((n_ctx, w), lambda b: (b, 0))
    return pl.pallas_call(
        functools.partial(_na_ctx_kernel, heads=heads),
        grid=(batch,),
        in_specs=[spec, spec, spec],
        out_specs=spec,
        out_shape=jax.ShapeDtypeStruct(qc.shape, BF),
        compiler_params=_cparams("parallel"),
        name="na_ctx",
    )(qc, kc, vc)


def _outproj_kernel(z_ref, pf_ref, na_ref, x_ref, gate_ref, gpost_ref,
                    wglu_ref, cs_ref, wfo_ref, wout_ref, o_ref, *, widths, sub):
    w_ssm, w_fn, w_na = widths
    for r0 in range(0, x_ref.shape[0], sub):
        rs = slice(r0, r0 + sub)
        z = jnp.concatenate([z_ref[j, rs, :] for j in range(w_ssm // LANES)], axis=1)
        gl = jnp.dot(z.astype(BF), wglu_ref[...], preferred_element_type=F32)
        y_ssm = (z * jax.nn.sigmoid(gl)).astype(BF)
        mixed = jnp.dot(pf_ref[rs, :], cs_ref[...], preferred_element_type=F32).astype(BF)
        y_fft = jnp.dot(mixed, wfo_ref[...], preferred_element_type=F32).astype(BF)
        y = jnp.concatenate([y_ssm, y_fft, na_ref[rs, :]], axis=1)
        o = jnp.dot(y, wout_ref[...], preferred_element_type=F32)
        ms = jnp.mean(o * o, axis=-1, keepdims=True)
        o_ref[rs, :] = x_ref[rs, :] + gate_ref[...] * (o * lax.rsqrt(ms + EPS) * gpost_ref[...])


def _outproj(z3, pf, yna, x2d, mod3, g_post, w_glu, cs, w_fo, w_out, *,
             layer, tm, tiles_per_batch, mod_row, widths):
    n, d = x2d.shape
    w_ssm, w_fn, w_na = widths
    if mod_row is None:
        row = lambda i: i // tiles_per_batch
    else:
        row = lambda i: mod_row
    kern = functools.partial(_outproj_kernel, widths=widths, sub=min(tm, 512))
    return pl.pallas_call(
        kern,
        grid=(n // tm,),
        in_specs=[pl.BlockSpec((w_ssm // LANES, tm, LANES), lambda i: (0, i, 0)),
                  pl.BlockSpec((tm, 2 * w_fn), lambda i: (i, 0)),
                  pl.BlockSpec((tm, w_na), lambda i: (i, 0)),
                  pl.BlockSpec((tm, d), lambda i: (i, 0)),
                  pl.BlockSpec((None, 1, d), lambda i: (row(i), 0, 2)),
                  pl.BlockSpec((1, d), lambda i: (0, 0)),
                  _layer_resident(w_glu, layer),
                  _resident(cs.shape, lambda i: (0, 0)),
                  _layer_resident(w_fo, layer),
                  _layer_resident(w_out, layer)],
        out_specs=pl.BlockSpec((tm, d), lambda i: (i, 0)),
        out_shape=jax.ShapeDtypeStruct((n, d), F32),
        compiler_params=_cparams("parallel"),
        name="outproj",
    )(z3, pf, yna, x2d, mod3, g_post.reshape(1, d), w_glu, cs, w_fo, w_out)


def _ffn_kernel(x_ref, sh_ref, sc_ref, gate_ref, gpre_ref, gpost_ref, wg_ref, wu_ref, wd_ref,
                o_ref, a_ref, *, sub, chunk):
    dff = wg_ref.shape[1]
    for r0 in range(0, x_ref.shape[0], sub):
        rs = slice(r0, r0 + sub)
        x = x_ref[rs, :]
        ms = jnp.mean(x * x, axis=-1, keepdims=True)
        m = ((x * lax.rsqrt(ms + EPS) * gpre_ref[...]) * (1.0 + sc_ref[...]) + sh_ref[...]).astype(BF)
        for c0 in range(0, dff, chunk):
            sl = slice(c0, min(c0 + chunk, dff))
            g = jnp.dot(m, wg_ref[:, sl], preferred_element_type=F32)
            u = jnp.dot(m, wu_ref[:, sl], preferred_element_type=F32)
            a_ref[rs, sl] = (g * jax.nn.sigmoid(g) * u).astype(BF)
        y = jnp.dot(a_ref[rs, :], wd_ref[...], preferred_element_type=F32)
        ms2 = jnp.mean(y * y, axis=-1, keepdims=True)
        o_ref[rs, :] = x + gate_ref[...] * (y * lax.rsqrt(ms2 + EPS) * gpost_ref[...])


def _ffn(x2d, mod3, g_pre, g_post, wg, wu, wd, *, layer, tm, tiles_per_batch, mod_row):
    n, d = x2d.shape
    dff = wg.shape[2]
    if mod_row is None:
        row = lambda i: i // tiles_per_batch
    else:
        row = lambda i: mod_row
    return pl.pallas_call(
        functools.partial(_ffn_kernel, sub=min(tm, 512), chunk=3 * MXU_TILE),
        grid=(n // tm,),
        in_specs=[pl.BlockSpec((tm, d), lambda i: (i, 0)),
                  pl.BlockSpec((None, 1, d), lambda i: (row(i), 0, 3)),
                  pl.BlockSpec((None, 1, d), lambda i: (row(i), 0, 4)),
                  pl.BlockSpec((None, 1, d), lambda i: (row(i), 0, 5)),
                  pl.BlockSpec((1, d), lambda i: (0, 0)),
                  pl.BlockSpec((1, d), lambda i: (0, 0)),
                  _layer_resident(wg, layer),
                  _layer_resident(wu, layer),
                  _layer_resident(wd, layer)],
        out_specs=pl.BlockSpec((tm, d), lambda i: (i, 0)),
        out_shape=jax.ShapeDtypeStruct((n, d), F32),
        scratch_shapes=[pltpu.VMEM((tm, dff), BF)],
        compiler_params=_cparams("parallel"),
        name="ffn",
    )(x2d, mod3, mod3, mod3, g_pre.reshape(1, d), g_post.reshape(1, d), wg, wu, wd)


def _layer(x2d, xc2d, cstack, p, w, layer, *, batch, seq, n_ctx, last, rope_tabs, consts):
    d = x2d.shape[1]
    w_ssm = p["ssm_d"].shape[0]
    w_fn = w["w_fourier"].shape[1]
    w_na = (w["w_in"].shape[2] - w_ssm - w_fn) // 3
    widths = (w_ssm, w_fn, w_na)
    heads = w_na // HEAD_DIM
    tm = min(512, seq)
    tmc = min(512, batch * n_ctx)

    mod = _mod_rows(cstack, w["w_mod"], w["b_mod"], layer)
    mod3 = mod.reshape(mod.shape[0], 1, 6 * d)

    cos, sin = rope_tabs
    tm2 = min(2 * tm, seq)
    u3, f, q, k, v = _inproj(x2d, mod3, p["g_pre_mix"], w["w_in"], cos, sin, layer=layer, tm=tm2,
                             tiles_per_batch=seq // tm2, mod_row=None, widths=widths, rope=True)
    ones = jnp.ones((tmc, LANES), F32)
    u3c, fc, qc, kc, vc = _inproj(xc2d, mod3, p["g_pre_mix"], w["w_in"], ones, ones, layer=layer, tm=tmc,
                                  tiles_per_batch=1, mod_row=batch, widths=widths, rope=False)

    z3, zc3 = _ssm(u3, u3c, *w["ssm"], layer=layer, batch=batch, seq=seq, n_ctx=n_ctx)

    l2 = consts["l2"]
    l1 = seq // l2
    g = _dft_a(consts["dft_a"], f.reshape(batch, l2, l1 * w_fn))
    pf = _dft_b(consts["dft_b"], g.reshape(batch, 2, l2, l1, w_fn), kb=min(8, l2))
    pf = pf.reshape(batch * seq, 2 * w_fn)

    yna = _na_latent(q, k, v, kc, vc, p["na_rpb"], batch=batch, seq=seq, n_ctx=n_ctx, rq=consts["rq"])

    w_glu, w_fo, w_out = w["w_glu"], w["w_fourier"], w["w_out"]
    x2d = _outproj(z3, pf, yna, x2d, mod3, p["g_post_mix"], w_glu, consts["cs_lat"], w_fo, w_out,
                   layer=layer, tm=tm2, tiles_per_batch=seq // tm2, mod_row=None, widths=widths)
    wg, wu, wd = w["w_ffn_gate"], w["w_ffn_up"], w["w_ffn_down"]
    x2d = _ffn(x2d, mod3, p["g_pre_ffn"], p["g_post_ffn"], wg, wu, wd,
               layer=layer, tm=tm2, tiles_per_batch=seq // tm2, mod_row=None)

    if not last:
        pfc = _dft_ctx(consts["dft_ctx"], fc.reshape(batch, n_ctx, w_fn)).reshape(batch * n_ctx, 2 * w_fn)
        ynac = _na_ctx(qc, kc, vc, batch=batch, n_ctx=n_ctx, heads=heads)
        tc = min(tmc, n_ctx)
        xc2d = _outproj(zc3, pfc, ynac, xc2d, mod3, p["g_post_mix"], w_glu, consts["cs_ctx"], w_fo, w_out,
                        layer=layer, tm=tc, tiles_per_batch=n_ctx // tc, mod_row=batch, widths=widths)
        xc2d = _ffn(xc2d, mod3, p["g_pre_ffn"], p["g_post_ffn"], wg, wu, wd,
                    layer=layer, tm=tc, tiles_per_batch=n_ctx // tc, mod_row=batch)
    else:
        xc2d = None
    return x2d, xc2d


def kernel(x, c, ctx, c_ctx, w_mod, b_mod, g_pre_mix, g_post_mix, w_in, ssm_a_re, ssm_a_im, ssm_log_dt,
           ssm_b_re, ssm_b_im, ssm_c_re, ssm_c_im, ssm_d, w_glu, w_fourier, na_rpb, w_out, g_pre_ffn,
           g_post_ffn, w_ffn_gate, w_ffn_up, w_ffn_down):
    batch, seq, d = x.shape
    n_ctx = ctx.shape[1]
    depth = w_mod.shape[0]
    w_fn = w_fourier.shape[1]
    params = dict(g_pre_mix=g_pre_mix, g_post_mix=g_post_mix, ssm_d=ssm_d, na_rpb=na_rpb,
                  g_pre_ffn=g_pre_ffn, g_post_ffn=g_post_ffn)
    seg_len = (seq + 2 * n_ctx) // SSM_T // SSM_SEG
    ssm_ops = jax.vmap(functools.partial(_ssm_weights, seg_len=seg_len))(
        ssm_a_re, ssm_a_im, ssm_log_dt, ssm_b_re, ssm_b_im, ssm_c_re, ssm_c_im, ssm_d)
    weights = dict(w_mod=w_mod, b_mod=b_mod, ssm=ssm_ops,
                   w_in=_cast_bf16(w_in), w_glu=_cast_bf16(w_glu), w_fourier=_cast_bf16(w_fourier),
                   w_out=_cast_bf16(w_out), w_ffn_gate=_cast_bf16(w_ffn_gate),
                   w_ffn_up=_cast_bf16(w_ffn_up), w_ffn_down=_cast_bf16(w_ffn_down))

    nrow = -(-(batch + 1) // 8) * 8
    cstack = jnp.concatenate([c, c_ctx[None, :], jnp.zeros((nrow - batch - 1, d), c.dtype)], axis=0)

    l2 = 64 if seq % (64 * 8) == 0 else 8
    l1 = seq // l2
    consts = dict(
        l2=l2,
        rq=4,
        dft_a=_dft_rows(l2).astype(BF),
        dft_b=_dft_b_mats(seq, l1, l2).astype(BF),
        dft_ctx=_dft_rows(n_ctx).astype(BF),
        cs_lat=_channel_dft(w_fn, seq).astype(BF),
        cs_ctx=_channel_dft(w_fn, n_ctx).astype(BF),
    )
    rope_tabs = _rope_tables(seq)

    x2d = x.reshape(batch * seq, d)
    xc2d = ctx.reshape(batch * n_ctx, d)
    for layer in range(depth):
        p = {name: val[layer] for name, val in params.items()}
        x2d, xc2d = _layer(x2d, xc2d, cstack, p, weights, layer, batch=batch, seq=seq, n_ctx=n_ctx,
                           last=(layer == depth - 1), rope_tabs=rope_tabs, consts=consts)
    return x2d.reshape(batch, seq, d)
```

```python
import functools
import math

import numpy as np
import jax
import jax.numpy as jnp
from jax import lax
from jax.experimental import pallas as pl
from jax.experimental.pallas import tpu as pltpu

BF = jnp.bfloat16
F32 = jnp.float32

EPS = 1e-6
GRID_W = 64
HEAD_DIM = 64
NA_ROWS = 8
NA_COLS = 16
ROPE_BASE = 10000.0
SSM_GROUP = 16
SSM_STATE = 64
FNET_GROUP = 64
LANES = 128
MXU_TILE = 256
SSM_T = 8
SSM_SEG = 8
NEG = -1e30
VMEM_LIMIT = 56 * 1024 * 1024


def _cparams(*sem):
    return pltpu.CompilerParams(dimension_semantics=sem, vmem_limit_bytes=VMEM_LIMIT)


def _resident(shape, index_map):
    return pl.BlockSpec(shape, index_map, pipeline_mode=pl.Buffered(1))


def _layer_resident(w_all, layer):
    return pl.BlockSpec((None,) + w_all.shape[1:], lambda *_: (layer, 0, 0), pipeline_mode=pl.Buffered(1))


def _cast_kernel(w_ref, o_ref):
    o_ref[...] = w_ref[...].astype(BF)


def _cast_bf16(w_all):
    depth, rows, cols = w_all.shape
    br = rows
    while br * cols * 4 > 4 * 1024 * 1024 and br % 32 == 0:
        br //= 2
    spec = pl.BlockSpec((None, br, cols), lambda l, i: (l, i, 0))
    return pl.pallas_call(
        _cast_kernel,
        grid=(depth, rows // br),
        in_specs=[spec],
        out_specs=spec,
        out_shape=jax.ShapeDtypeStruct(w_all.shape, BF),
        compiler_params=_cparams("parallel", "parallel"),
        name="cast_bf16",
    )(w_all)


def _mod_kernel(c_ref, w_ref, b_ref, o_ref):
    cs = c_ref[...]
    s = (cs * jax.nn.sigmoid(cs)).astype(BF)
    o_ref[...] = jnp.dot(s, w_ref[...].astype(BF), preferred_element_type=F32) + b_ref[...]


def _mod_rows(cstack, w_mod, b_mod, layer):
    rows, d = cstack.shape
    depth, _, n = w_mod.shape
    return pl.pallas_call(
        _mod_kernel,
        grid=(n // d,),
        in_specs=[pl.BlockSpec((rows, d), lambda i: (0, 0)),
                  pl.BlockSpec((None, d, d), lambda i: (layer, 0, i)),
                  pl.BlockSpec((None, 1, d), lambda i: (layer, 0, i))],
        out_specs=pl.BlockSpec((rows, d), lambda i: (0, i)),
        out_shape=jax.ShapeDtypeStruct((rows, n), F32),
        compiler_params=_cparams("arbitrary"),
        name="mod_rows",
    )(cstack, w_mod, b_mod.reshape(depth, 1, n))


def _rope_tile(t, cos, sin, first):
    partner = jnp.where(first, pltpu.roll(t, LANES - 16, 1), pltpu.roll(t, 16, 1))
    return t * cos + partner * sin


def _inproj_kernel(x_ref, sh_ref, sc_ref, g_ref, w_ref, cos_ref, sin_ref,
                   u_ref, f_ref, q_ref, k_ref, v_ref, *, widths, rope, sub):
    w_ssm, w_fn, w_na = widths
    scale = HEAD_DIM ** -0.5 * math.log2(math.e)
    for r0 in range(0, x_ref.shape[0], sub):
        rs = slice(r0, r0 + sub)
        x = x_ref[rs, :]
        ms = jnp.mean(x * x, axis=-1, keepdims=True)
        xn = x * lax.rsqrt(ms + EPS) * g_ref[...]
        m = (xn * (1.0 + sc_ref[...]) + sh_ref[...]).astype(BF)

        h = jnp.dot(m, w_ref[...], preferred_element_type=F32)

        def proj(lo, n, h=h):
            return h[:, lo:lo + n]

        u = proj(0, w_ssm)
        for j in range(w_ssm // LANES):
            u_ref[j, rs, :] = u[:, j * LANES:(j + 1) * LANES]
        f_ref[rs, :] = proj(w_ssm, w_fn).astype(BF)
        q = proj(w_ssm + w_fn, w_na)
        k = proj(w_ssm + w_fn + w_na, w_na)
        v_ref[rs, :] = proj(w_ssm + w_fn + 2 * w_na, w_na).astype(BF)
        if rope:
            cos = cos_ref[rs, :]
            sin = sin_ref[rs, :]
            lane = lax.broadcasted_iota(jnp.int32, cos.shape, 1)
            first = (lane % 32) < 16
            for j in range(w_na // LANES):
                sl = slice(j * LANES, (j + 1) * LANES)
                q_ref[rs, sl] = (_rope_tile(q[:, sl], cos, sin, first) * scale).astype(BF)
                k_ref[rs, sl] = _rope_tile(k[:, sl], cos, sin, first).astype(BF)
        else:
            q_ref[rs, :] = (q * scale).astype(BF)
            k_ref[rs, :] = k.astype(BF)


def _inproj(x2d, mod3, g_pre, w_in, cos, sin, *, layer, tm, tiles_per_batch, mod_row, widths, rope):
    n, d = x2d.shape
    w_ssm, w_fn, w_na = widths
    nt = n // tm
    if mod_row is None:
        row = lambda i: i // tiles_per_batch
    else:
        row = lambda i: mod_row
    pos = lambda i: (i % tiles_per_batch, 0)
    kern = functools.partial(_inproj_kernel, widths=widths, rope=rope, sub=min(tm, 512))
    return pl.pallas_call(
        kern,
        grid=(nt,),
        in_specs=[pl.BlockSpec((tm, d), lambda i: (i, 0)),
                  pl.BlockSpec((None, 1, d), lambda i: (row(i), 0, 0)),
                  pl.BlockSpec((None, 1, d), lambda i: (row(i), 0, 1)),
                  pl.BlockSpec((1, d), lambda i: (0, 0)),
                  _layer_resident(w_in, layer),
                  pl.BlockSpec((tm, LANES), pos),
                  pl.BlockSpec((tm, LANES), pos)],
        out_specs=[pl.BlockSpec((w_ssm // LANES, tm, LANES), lambda i: (0, i, 0)),
                   pl.BlockSpec((tm, w_fn), lambda i: (i, 0)),
                   pl.BlockSpec((tm, w_na), lambda i: (i, 0)),
                   pl.BlockSpec((tm, w_na), lambda i: (i, 0)),
                   pl.BlockSpec((tm, w_na), lambda i: (i, 0))],
        out_shape=[jax.ShapeDtypeStruct((w_ssm // LANES, n, LANES), F32),
                   jax.ShapeDtypeStruct((n, w_fn), BF),
                   jax.ShapeDtypeStruct((n, w_na), BF),
                   jax.ShapeDtypeStruct((n, w_na), BF),
                   jax.ShapeDtypeStruct((n, w_na), BF)],
        compiler_params=_cparams("parallel"),
        name="inproj_rope" if rope else "inproj_ctx",
    )(x2d, mod3, mod3, g_pre.reshape(1, d), w_in, cos, sin)


def _rope_tables(seq):
    t = np.arange(seq)
    row = (t // GRID_W).astype(np.float32)
    col = (t % GRID_W).astype(np.float32)
    quarter = HEAD_DIM // 4
    freqs = (np.float32(ROPE_BASE) ** (-np.arange(quarter, dtype=np.float32) / quarter)).astype(np.float32)
    d = np.arange(LANES) % HEAD_DIM
    use_col = (d // (HEAD_DIM // 2)) == 1
    fidx = d % quarter
    sign = np.where((d % (HEAD_DIM // 2)) // quarter == 0, -1.0, 1.0)
    pos = np.where(use_col[None, :], col[:, None], row[:, None])
    ang = (pos * freqs[fidx][None, :]).astype(np.float32).astype(np.float64)
    return jnp.asarray(np.cos(ang), F32), jnp.asarray(np.sin(ang) * sign[None, :], F32)


def _gelu_tanh(x):
    c = math.sqrt(2.0 / math.pi)
    return x * (0.5 * (1.0 + jnp.tanh(c * (x + 0.044715 * (x * x * x)))))


def _group_of(idx, width, groups):
    shift = width.bit_length() - 1
    assert width == 1 << shift and groups & (groups - 1) == 0
    return lax.bitwise_and(lax.shift_right_logical(idx, shift), groups - 1)


def _same_group(shape, row0, row_width, col0, col_width, groups):
    r = lax.broadcasted_iota(jnp.int32, shape, 0) + row0
    c = lax.broadcasted_iota(jnp.int32, shape, 1) + col0
    return _group_of(r, row_width, groups) == _group_of(c, col_width, groups)


def _ssm_kernel(ul_ref, uc_ref, wc_ref, lag_ref, dc_ref, rexp_ref, cexp_ref, lam_ref, lamseg_ref,
                z_ref, zc_ref, u_s, bm_s, kin_s, cm_s, s_ref, h_ref, e_ref, *, seg_len, n_lat, n_ctx):
    T = SSM_T
    rows, tw = u_s.shape
    nt = s_ref.shape[0]
    tpq = nt // 4
    nst = tpq * LANES
    gpt = LANES // SSM_GROUP
    cw = 512

    def lanes(c):
        return slice(c * LANES, (c + 1) * LANES)

    @pl.when(pl.program_id(1) == 0)
    def _expand_operators():
        rexp = rexp_ref[...]
        for c0 in range(0, 4 * nst, cw):
            blk = jnp.dot(rexp, wc_ref[:, c0:c0 + cw], preferred_element_type=F32)
            keep = _same_group(blk.shape, 0, SSM_GROUP, c0, SSM_STATE, gpt)
            bm_s[:, c0:c0 + cw] = jnp.where(keep, blk, 0.0).astype(BF)
        r16 = lax.broadcasted_iota(jnp.int32, (LANES, SSM_GROUP), 0)
        c16 = lax.broadcasted_iota(jnp.int32, (LANES, SSM_GROUP), 1)
        chan = jnp.where(lax.bitwise_and(r16, SSM_GROUP - 1) == c16, 1.0, 0.0).astype(BF)
        for t in range(T):
            lo = (T - 1 - t) * LANES
            blk = jnp.dot(chan, lag_ref[:, lo:lo + tw], preferred_element_type=F32)
            keep = _same_group(blk.shape, 0, SSM_GROUP, 0, SSM_GROUP, gpt)
            kin_s[t * LANES:(t + 1) * LANES, :] = jnp.where(keep, blk, 0.0).astype(BF)
        cexp = cexp_ref[...]
        for r0 in range(0, 4 * nst, cw):
            blk = jnp.dot(dc_ref[r0:r0 + cw, :], cexp, preferred_element_type=F32)
            keep = _same_group(blk.shape, r0, SSM_STATE, 0, SSM_GROUP, gpt)
            cm_s[r0:r0 + cw, :] = jnp.where(keep, blk, 0.0).astype(BF)

    for t in range(T):
        ct = uc_ref[pl.ds(t, n_ctx, stride=T), :].astype(BF)
        u_s[0:n_ctx, lanes(t)] = ct
        u_s[n_ctx:n_ctx + n_lat, lanes(t)] = ul_ref[pl.ds(t, n_lat, stride=T), :].astype(BF)
        u_s[n_ctx + n_lat:rows, lanes(t)] = ct
    u = u_s[...]
    for q in range(4):
        r = jnp.dot(u, bm_s[:, q * nst:(q + 1) * nst], preferred_element_type=F32)
        for t in range(tpq):
            s_ref[q * tpq + t] = r[:, t * LANES:(t + 1) * LANES]

    def scan_step(i, carry, record):
        new = list(carry)
        for d, ii in ((0, i), (1, seg_len - 1 - i)):
            rsl = pl.ds(ii, SSM_SEG, stride=seg_len)
            for t in range(tpq):
                cr, ci = 2 * d * tpq + t, (2 * d + 1) * tpq + t
                hr, hi = carry[cr], carry[ci]
                if record:
                    h_ref[cr, rsl, :] = hr
                    h_ref[ci, rsl, :] = hi
                lr, li = lam_ref[:, lanes(cr)], lam_ref[:, lanes(ci)]
                new[cr] = lr * hr - li * hi + s_ref[cr, rsl, :]
                new[ci] = lr * hi + li * hr + s_ref[ci, rsl, :]
        return tuple(new)

    zero = jnp.zeros((SSM_SEG, LANES), F32)
    fin = lax.fori_loop(0, seg_len, lambda i, c: scan_step(i, c, False), (zero,) * nt, unroll=2)
    for c in range(nt):
        e_ref[c, 0:SSM_SEG, :] = fin[c]

    for d in range(2):
        for t in range(tpq):
            cr, ci = 2 * d * tpq + t, (2 * d + 1) * tpq + t
            lr, li = lamseg_ref[:, lanes(cr)], lamseg_ref[:, lanes(ci)]
            er = jnp.zeros((1, LANES), F32)
            ei = jnp.zeros((1, LANES), F32)
            for n in range(SSM_SEG):
                s = n if d == 0 else SSM_SEG - 1 - n
                e_ref[cr, SSM_SEG + s:SSM_SEG + s + 1, :] = er
                e_ref[ci, SSM_SEG + s:SSM_SEG + s + 1, :] = ei
                fr, fi = e_ref[cr, s:s + 1, :], e_ref[ci, s:s + 1, :]
                er, ei = lr * er - li * ei + fr, lr * ei + li * er + fi

    ent = tuple(e_ref[c, SSM_SEG:2 * SSM_SEG, :] for c in range(nt))
    lax.fori_loop(0, seg_len, lambda i, c: scan_step(i, c, True), ent, unroll=2)

    hcat = jnp.concatenate([h_ref[c].astype(BF) for c in range(nt)], axis=1)
    for t0 in range(0, T, 2):
        cs = slice(t0 * LANES, (t0 + 2) * LANES)
        y = (jnp.dot(u, kin_s[:, cs], preferred_element_type=F32)
             + jnp.dot(hcat, cm_s[:, cs], preferred_element_type=F32))
        g = _gelu_tanh(y)
        for t in (t0, t0 + 1):
            z_ref[pl.ds(t, n_lat, stride=T), :] = g[n_ctx:n_ctx + n_lat, lanes(t - t0)]

    uc = u_s[0:n_ctx, :]
    hc = jnp.concatenate([h_ref[c, 0:n_ctx, :].astype(BF) for c in range(2 * tpq)]
                         + [h_ref[c, rows - n_ctx:rows, :].astype(BF) for c in range(2 * tpq, nt)], axis=1)
    gc = _gelu_tanh(jnp.dot(uc, kin_s[...], preferred_element_type=F32)
                    + jnp.dot(hc, cm_s[...], preferred_element_type=F32))
    for t in range(T):
        zc_ref[pl.ds(t, n_ctx, stride=T), :] = gc[:, lanes(t)]


def _ssm_weights(a_re, a_im, log_dt, b_re, b_im, c_re, c_im, d_skip, seg_len):
    T = SSM_T
    g, p = a_re.shape[1], a_re.shape[2]
    hc = b_re.shape[-1]
    gpt = LANES // hc
    J = g // gpt
    lam = lax.complex(a_re.astype(F32), a_im.astype(F32))
    dt = jnp.exp(log_dt.astype(F32))[..., None]
    ldt = lam * dt
    lam_bar = jnp.exp(ldt)
    bbar = ((lam_bar - 1) / lam)[..., None] * lax.complex(b_re.astype(F32), b_im.astype(F32))
    cmat = lax.complex(c_re.astype(F32), c_im.astype(F32))
    kk = jnp.arange(T + 1, dtype=F32)
    pw = jnp.exp(ldt[:, None] * kk[None, :, None, None])

    wf = pw[0, T - 1 - jnp.arange(T)][:, :, None, :] * jnp.swapaxes(bbar[0], 1, 2)[None]
    wb = pw[1, jnp.arange(T)][:, :, None, :] * jnp.swapaxes(bbar[1], 1, 2)[None]
    wq = jnp.stack([wf.real, wf.imag, wb.real, wb.imag]).reshape(4, T, J, gpt, hc, p)
    wc = wq.transpose(2, 1, 4, 0, 3, 5).reshape(J, T * hc, 4 * gpt * p)

    df = cmat[0][None] * pw[0, 1 + jnp.arange(T)][:, :, None, :]
    db = cmat[1][None] * pw[1, T - jnp.arange(T)][:, :, None, :]
    dq = jnp.stack([df.real, -df.imag, db.real, -db.imag]).reshape(4, T, J, gpt, hc, p)
    dc = dq.transpose(2, 0, 3, 5, 1, 4).reshape(J, 4 * gpt * p, T * hc)

    mf = jnp.einsum('gcp,kgp,gph->kgch', cmat[0], pw[0, :T], bbar[0]).real
    mb = jnp.einsum('gcp,kgp,gph->kgch', cmat[1], pw[1, :T], bbar[1]).real
    skip = d_skip.astype(F32).reshape(g, hc)[:, :, None] * jnp.eye(hc, dtype=F32)[None]
    m0 = mf[0] + mb[0] + skip
    lags = jnp.concatenate([mb[1:][::-1], m0[None], mf[1:]], axis=0)
    kc = lags.reshape(2 * T - 1, J, gpt, hc, hc).transpose(1, 4, 0, 2, 3).reshape(J, hc, (2 * T - 1) * LANES)

    def lam_rows(power):
        lp = jnp.exp(ldt * power)
        lt = lp.reshape(2, J, gpt * p)
        return jnp.concatenate([lt[0].real, lt[0].imag, lt[1].real, lt[1].imag], axis=-1)[:, None, :]

    lam_t = jnp.broadcast_to(lam_rows(float(T)), (J, SSM_SEG, 4 * gpt * p))
    return wc.astype(BF), kc.astype(BF), dc.astype(BF), lam_t, lam_rows(float(T * seg_len))


def _ssm_expanders():
    T = SSM_T
    r = np.arange(T * LANES)
    rexp = np.zeros((T * LANES, T * SSM_GROUP), np.float32)
    rexp[r, (r // LANES) * SSM_GROUP + r % SSM_GROUP] = 1.0
    return jnp.asarray(rexp, BF), jnp.asarray(rexp.T, BF)


def _ssm(u3, u3c, wc, kc, dc, lam, lamseg, *, layer, batch, seq, n_ctx):
    T = SSM_T
    J = u3.shape[0]
    tw = T * LANES
    ns = wc.shape[-1]
    n_lat, n_c = seq // T, n_ctx // T
    rows = n_lat + 2 * n_c
    seg_len = rows // SSM_SEG
    assert rows % SSM_SEG == 0 and n_c % 16 == 0
    rexp, cexp = _ssm_expanders()
    kern = functools.partial(_ssm_kernel, seg_len=seg_len, n_lat=n_lat, n_ctx=n_c)
    const = lambda j, b: (0, 0)
    per_j = lambda j, b: (layer, j, 0, 0)
    per_jb = lambda j, b: (j, b, 0)
    return pl.pallas_call(
        kern,
        grid=(J, batch),
        in_specs=[pl.BlockSpec((None, seq, LANES), per_jb, pipeline_mode=pl.Buffered(1)),
                  pl.BlockSpec((None, n_ctx, LANES), per_jb),
                  pl.BlockSpec((None, None) + wc.shape[2:], per_j),
                  pl.BlockSpec((None, None) + kc.shape[2:], per_j),
                  pl.BlockSpec((None, None) + dc.shape[2:], per_j),
                  pl.BlockSpec(rexp.shape, const),
                  pl.BlockSpec(cexp.shape, const),
                  pl.BlockSpec((None, None, SSM_SEG, ns), per_j),
                  pl.BlockSpec((None, None, 1, ns), per_j)],
        out_specs=[pl.BlockSpec((None, seq, LANES), per_jb),
                   pl.BlockSpec((None, n_ctx, LANES), per_jb)],
        out_shape=[jax.ShapeDtypeStruct(u3.shape, F32),
                   jax.ShapeDtypeStruct(u3c.shape, F32)],
        scratch_shapes=[pltpu.VMEM((rows, tw), BF),
                        pltpu.VMEM((tw, ns), BF),
                        pltpu.VMEM((tw, tw), BF),
                        pltpu.VMEM((ns, tw), BF),
                        pltpu.VMEM((ns // LANES, rows, LANES), F32),
                        pltpu.VMEM((ns // LANES, rows, LANES), F32),
                        pltpu.VMEM((ns // LANES, 2 * SSM_SEG, LANES), F32)],
        compiler_params=_cparams("arbitrary", "arbitrary"),
        name="ssm_chunk_scan",
    )(u3, u3c, wc, kc, dc, rexp, cexp, lam, lamseg)


def _dft_a_kernel(w_ref, x_ref, o_ref):
    o_ref[...] = jnp.dot(w_ref[...], x_ref[...], preferred_element_type=F32).astype(BF)


def _dft_rows(n):
    k = np.arange(n)
    m = (k[:, None] * k[None, :]) % n
    ang = m * (2.0 * math.pi / n)
    return jnp.asarray(np.concatenate([np.cos(ang), -np.sin(ang)], axis=0), BF)


def _dft_a(w, x3):
    B, n, cols = x3.shape
    tn = min(cols, 4096)
    return pl.pallas_call(
        _dft_a_kernel,
        grid=(B, cols // tn),
        in_specs=[pl.BlockSpec((2 * n, n), lambda b, i: (0, 0)),
                  pl.BlockSpec((None, n, tn), lambda b, i: (b, 0, i))],
        out_specs=pl.BlockSpec((None, 2 * n, tn), lambda b, i: (b, 0, i)),
        out_shape=jax.ShapeDtypeStruct((B, 2 * n, cols), BF),
        compiler_params=_cparams("parallel", "parallel"),
        name="dft_stage_a",
    )(w, x3)


def _store_lane_tiles(o_ref, p, n, wf, idx):
    per = wf // LANES
    for c in range(2 * per):
        part = p[(c // per) * n:(c // per + 1) * n, (c % per) * LANES:(c % per + 1) * LANES]
        o_ref[(c,) + idx] = part


def _dft_b_kernel(a_ref, g_ref, o_ref, *, kb, l1, wf):
    for i in range(kb):
        g = jnp.concatenate([g_ref[0, i], g_ref[1, i]], axis=0)
        p = jnp.dot(a_ref[i], g, preferred_element_type=F32)
        _store_lane_tiles(o_ref, p, l1, wf, (slice(None), i, slice(None)))


def _dft_b_mats(seq, l1, l2):
    k2 = np.arange(l2)[:, None, None]
    k1 = np.arange(l1)[None, :, None]
    j1 = np.arange(l1)[None, None, :]
    m = ((k1 * l2 + k2) * j1) % seq
    ang = m * (2.0 * math.pi / seq)
    ar, ai = np.cos(ang).astype(np.float32), -np.sin(ang).astype(np.float32)
    top = np.concatenate([ar, -ai], axis=2)
    bot = np.concatenate([ai, ar], axis=2)
    return jnp.asarray(np.concatenate([top, bot], axis=1), BF)


def _dft_b(amat, g5, *, kb):
    B, _, l2, l1, wf = g5.shape
    nlt = 2 * wf // LANES
    kern = functools.partial(_dft_b_kernel, kb=kb, l1=l1, wf=wf)
    out = pl.pallas_call(
        kern,
        grid=(B, l2 // kb),
        in_specs=[pl.BlockSpec((kb, 2 * l1, 2 * l1), lambda b, i: (i, 0, 0)),
                  pl.BlockSpec((None, 2, kb, l1, wf), lambda b, i: (b, 0, i, 0, 0))],
        out_specs=pl.BlockSpec((None, nlt, l1, kb, LANES), lambda b, i: (b, 0, 0, i, 0)),
        out_shape=jax.ShapeDtypeStruct((B, nlt, l1, l2, LANES), F32),
        compiler_params=_cparams("parallel", "parallel"),
        name="dft_stage_b",
    )(amat, g5)
    return out.reshape(B, nlt, l1 * l2, LANES)


def _dft_ctx_kernel(w_ref, x_ref, o_ref, *, n, wf):
    p = jnp.dot(w_ref[...], x_ref[...], preferred_element_type=F32)
    _store_lane_tiles(o_ref, p, n, wf, ())


def _dft_ctx(w, x3):
    B, n, wf = x3.shape
    nlt = 2 * wf // LANES
    kern = functools.partial(_dft_ctx_kernel, n=n, wf=wf)
    return pl.pallas_call(
        kern,
        grid=(B,),
        in_specs=[pl.BlockSpec((2 * n, n), lambda b: (0, 0)),
                  pl.BlockSpec((None, n, wf), lambda b: (b, 0, 0))],
        out_specs=pl.BlockSpec((None, nlt, n, LANES), lambda b: (b, 0, 0, 0)),
        out_shape=jax.ShapeDtypeStruct((B, nlt, n, LANES), F32),
        compiler_params=_cparams("parallel"),
        name="dft_ctx",
    )(w, x3)


def _channel_dft(wf, seq):
    c = np.arange(wf)
    same = (c[:, None] // FNET_GROUP) == (c[None, :] // FNET_GROUP)
    m = ((c[:, None] % FNET_GROUP) * (c[None, :] % FNET_GROUP)) % FNET_GROUP
    ang = m * (2.0 * math.pi / FNET_GROUP)
    mask = same / math.sqrt(seq * FNET_GROUP)
    return jnp.asarray(np.concatenate([np.cos(ang) * mask, np.sin(ang) * mask], axis=0), BF)


def _softmax_pv(parts, l_shape):
    m = parts[0][0].max(axis=-1, keepdims=True)
    for s, _ in parts[1:]:
        m = jnp.maximum(m, s.max(axis=-1, keepdims=True))
    l = jnp.zeros(l_shape, F32)
    o = None
    for s, v in parts:
        p = jnp.exp2(s - m)
        l = l + p.sum(axis=-1, keepdims=True)
        pv = jnp.dot(p.astype(BF), v, preferred_element_type=F32)
        o = pv if o is None else o + pv
    return o / l


def _na_kernel(kb_ref, pid_ref, dr_ref, q_ref, k_ref, v_ref, kc_ref, vc_ref, cbx_ref, o_ref, bias_ref,
               *, nkeys, heads):
    i = pl.program_id(1)
    start = pl.multiple_of(kb_ref[i] * GRID_W, GRID_W)
    nq = q_ref.shape[0]
    rq, kr = nq // GRID_W, nkeys // GRID_W

    pid = pid_ref[i]
    @pl.when((i == 0) | (pid != pid_ref[jnp.maximum(i - 1, 0)]))
    def _build_bias():
        for qi in range(rq):
            for w in range(kr):
                a = dr_ref[(pid * rq + qi) * kr + w]
                for h in range(heads):
                    bias_ref[h, qi * GRID_W:(qi + 1) * GRID_W, w * GRID_W:(w + 1) * GRID_W] = cbx_ref[h, a]

    lane = lax.broadcasted_iota(jnp.int32, (nq, LANES), 1)
    low = lane < HEAD_DIM
    dn = (((1,), (1,)), ((), ()))
    for hp in range(heads // 2):
        sl = slice(hp * LANES, (hp + 1) * LANES)
        q2 = q_ref[:, sl]
        k2 = k_ref[pl.ds(start, nkeys), sl]
        v2 = v_ref[pl.ds(start, nkeys), sl]
        kc2 = kc_ref[:, sl]
        vc2 = vc_ref[:, sl]
        outs = []
        for hh in range(2):
            qm = jnp.where(low if hh == 0 else jnp.logical_not(low), q2, jnp.zeros_like(q2))
            s_w = lax.dot_general(qm, k2, dn, preferred_element_type=F32) + bias_ref[2 * hp + hh]
            s_c = lax.dot_general(qm, kc2, dn, preferred_element_type=F32)
            outs.append(_softmax_pv([(s_w, v2), (s_c, vc2)], (nq, 1)))
        o_ref[:, sl] = jnp.where(low, outs[0], outs[1]).astype(BF)


def _na_plan(rows, rq):
    wr = min(NA_ROWS, rows)
    kr = rq + wr - 1
    nblk = rows // rq
    kbs, pids, pats = [], [], []
    for blk in range(nblk):
        r0 = blk * rq
        rs = [int(np.clip(r0 + i - wr // 2, 0, rows - wr)) for i in range(rq)]
        kb = min(rs[0], rows - kr)
        pat = (tuple(r - kb for r in rs), r0 - kb)
        if pat not in pats:
            pats.append(pat)
        kbs.append(kb)
        pids.append(pats.index(pat))
    return np.asarray(kbs, np.int32), np.asarray(pids, np.int32), pats, kr, wr


def _na_bias(rpb, pats, rq, kr, wr):
    cols = np.arange(GRID_W)
    cstart = np.clip(cols - NA_COLS // 2, 0, GRID_W - NA_COLS)
    kc = np.arange(GRID_W)
    col_ok = (kc[None, :] >= cstart[:, None]) & (kc[None, :] < cstart[:, None] + NA_COLS)
    dc = kc[None, :] - cols[:, None] + NA_COLS - 1
    heads, nr, ncol = rpb.shape
    sel = (dc[:, :, None] == np.arange(ncol)[None, None, :]) & col_ok[:, :, None]
    cb = jnp.einsum('hab,ckb->hack', rpb.astype(F32), jnp.asarray(sel, F32), precision=lax.Precision.HIGHEST)
    cb = jnp.where(jnp.asarray(col_ok)[None, None], cb * math.log2(math.e), NEG)
    cbx = jnp.concatenate([cb, jnp.full((heads, 1, GRID_W, GRID_W), NEG, F32)], axis=1)
    slots = []
    for rs_off, r_off in pats:
        i = np.arange(rq)[:, None]
        w = np.arange(kr)[None, :]
        rso = np.asarray(rs_off)[:, None]
        row_ok = (w >= rso) & (w < rso + wr)
        slots.append(np.where(row_ok, w - r_off - i + NA_ROWS - 1, nr))
    return cbx, np.stack(slots).reshape(-1).astype(np.int32)


def _na_latent(q, k, v, kc, vc, rpb, *, batch, seq, n_ctx, rq):
    rows = seq // GRID_W
    heads = rpb.shape[0]
    kbs, pids, pats, kr, wr = _na_plan(rows, rq)
    cbx, slots = _na_bias(rpb, pats, rq, kr, wr)
    nblk = rows // rq
    nq = rq * GRID_W
    nkeys = kr * GRID_W
    w = q.shape[1]
    kern = functools.partial(_na_kernel, nkeys=nkeys, heads=heads)
    grid_spec = pltpu.PrefetchScalarGridSpec(
        num_scalar_prefetch=3,
        grid=(batch, nblk),
        in_specs=[pl.BlockSpec((nq, w), lambda b, i, *_: (b * nblk + i, 0)),
                  pl.BlockSpec((seq, w), lambda b, i, *_: (b, 0)),
                  pl.BlockSpec((seq, w), lambda b, i, *_: (b, 0)),
                  pl.BlockSpec((n_ctx, w), lambda b, i, *_: (b, 0)),
                  pl.BlockSpec((n_ctx, w), lambda b, i, *_: (b, 0)),
                  pl.BlockSpec(cbx.shape, lambda b, i, *_: (0, 0, 0, 0))],
        out_specs=pl.BlockSpec((nq, w), lambda b, i, *_: (b * nblk + i, 0)),
        scratch_shapes=[pltpu.VMEM((heads, nq, nkeys), F32)],
    )
    return pl.pallas_call(
        kern,
        grid_spec=grid_spec,
        out_shape=jax.ShapeDtypeStruct(q.shape, BF),
        compiler_params=_cparams("arbitrary", "arbitrary"),
        name="na_latent",
    )(jnp.asarray(kbs), jnp.asarray(pids), jnp.asarray(slots), q, k, v, kc, vc, cbx)


def _na_ctx_kernel(q_ref, k_ref, v_ref, o_ref, *, heads):
    nq = q_ref.shape[0]
    lane = lax.broadcasted_iota(jnp.int32, (nq, LANES), 1)
    low = lane < HEAD_DIM
    dn = (((1,), (1,)), ((), ()))
    for hp in range(heads // 2):
        sl = slice(hp * LANES, (hp + 1) * LANES)
        q2 = q_ref[:, sl]
        k2 = k_ref[:, sl]
        v2 = v_ref[:, sl]
        outs = []
        for hh in range(2):
            qm = jnp.where(low if hh == 0 else jnp.logical_not(low), q2, jnp.zeros_like(q2))
            s = lax.dot_general(qm, k2, dn, preferred_element_type=F32)
            outs.append(_softmax_pv([(s, v2)], (nq, 1)))
        o_ref[:, sl] = jnp.where(low, outs[0], outs[1]).astype(BF)


def _na_ctx(qc, kc, vc, *, batch, n_ctx, heads):
    w = qc.shape[1]
    spec = pl.BlockSpec((n_ctx, w), lambda b: (b, 0))
    return pl.pallas_call(
        functools.partial(_na_ctx_kernel, heads=heads),
        grid=(batch,),
        in_specs=[spec, spec, spec],
        out_specs=spec,
        out_shape=jax.ShapeDtypeStruct(qc.shape, BF),
        compiler_params=_cparams("parallel"),
        name="na_ctx",
    )(qc, kc, vc)


def _outproj_kernel(z_ref, pf_ref, na_ref, x_ref, gate_ref, gpost_ref,
                    wglu_ref, cs_ref, wfo_ref, wout_ref, o_ref, *, widths, sub):
    w_ssm, w_fn, w_na = widths
    for r0 in range(0, x_ref.shape[0], sub):
        rs = slice(r0, r0 + sub)
        z = jnp.concatenate([z_ref[j, rs, :] for j in range(w_ssm // LANES)], axis=1)
        gl = jnp.dot(z.astype(BF), wglu_ref[...], preferred_element_type=F32)
        y_ssm = (z * jax.nn.sigmoid(gl)).astype(BF)
        pf = jnp.concatenate([pf_ref[c, rs, :] for c in range(pf_ref.shape[0])], axis=1).astype(BF)
        mixed = jnp.dot(pf, cs_ref[...], preferred_element_type=F32).astype(BF)
        y_fft = jnp.dot(mixed, wfo_ref[...], preferred_element_type=F32).astype(BF)
        y = jnp.concatenate([y_ssm, y_fft, na_ref[rs, :]], axis=1)
        o = jnp.dot(y, wout_ref[...], preferred_element_type=F32)
        ms = jnp.mean(o * o, axis=-1, keepdims=True)
        o_ref[rs, :] = x_ref[rs, :] + gate_ref[...] * (o * lax.rsqrt(ms + EPS) * gpost_ref[...])


def _outproj(z3, pf, yna, x2d, mod3, g_post, w_glu, cs, w_fo, w_out, *,
             layer, tm, tiles_per_batch, mod_row, widths):
    n, d = x2d.shape
    w_ssm, w_fn, w_na = widths
    if mod_row is None:
        row = lambda i: i // tiles_per_batch
    else:
        row = lambda i: mod_row
    kern = functools.partial(_outproj_kernel, widths=widths, sub=min(tm, 512))
    return pl.pallas_call(
        kern,
        grid=(n // tm,),
        in_specs=[pl.BlockSpec((w_ssm // LANES, tm, LANES), lambda i: (0, i, 0)),
                  pl.BlockSpec((None, pf.shape[1], tm, LANES),
                               lambda i: (i // tiles_per_batch, 0, i % tiles_per_batch, 0)),
                  pl.BlockSpec((tm, w_na), lambda i: (i, 0)),
                  pl.BlockSpec((tm, d), lambda i: (i, 0)),
                  pl.BlockSpec((None, 1, d), lambda i: (row(i), 0, 2)),
                  pl.BlockSpec((1, d), lambda i: (0, 0)),
                  _layer_resident(w_glu, layer),
                  _resident(cs.shape, lambda i: (0, 0)),
                  _layer_resident(w_fo, layer),
                  _layer_resident(w_out, layer)],
        out_specs=pl.BlockSpec((tm, d), lambda i: (i, 0)),
        out_shape=jax.ShapeDtypeStruct((n, d), F32),
        compiler_params=_cparams("parallel"),
        name="outproj",
    )(z3, pf, yna, x2d, mod3, g_post.reshape(1, d), w_glu, cs, w_fo, w_out)


def _ffn_kernel(x_ref, sh_ref, sc_ref, gate_ref, gpre_ref, gpost_ref, wg_ref, wu_ref, wd_ref,
                o_ref, a_ref, *, sub, chunk):
    dff = wg_ref.shape[1]
    for r0 in range(0, x_ref.shape[0], sub):
        rs = slice(r0, r0 + sub)
        x = x_ref[rs, :]
        ms = jnp.mean(x * x, axis=-1, keepdims=True)
        m = ((x * lax.rsqrt(ms + EPS) * gpre_ref[...]) * (1.0 + sc_ref[...]) + sh_ref[...]).astype(BF)
        for c0 in range(0, dff, chunk):
            sl = slice(c0, min(c0 + chunk, dff))
            g = jnp.dot(m, wg_ref[:, sl], preferred_element_type=F32)
            u = jnp.dot(m, wu_ref[:, sl], preferred_element_type=F32)
            a_ref[rs, sl] = (g * jax.nn.sigmoid(g) * u).astype(BF)
        y = jnp.dot(a_ref[rs, :], wd_ref[...], preferred_element_type=F32)
        ms2 = jnp.mean(y * y, axis=-1, keepdims=True)
        o_ref[rs, :] = x + gate_ref[...] * (y * lax.rsqrt(ms2 + EPS) * gpost_ref[...])


def _ffn(x2d, mod3, g_pre, g_post, wg, wu, wd, *, layer, tm, tiles_per_batch, mod_row):
    n, d = x2d.shape
    dff = wg.shape[2]
    if mod_row is None:
        row = lambda i: i // tiles_per_batch
    else:
        row = lambda i: mod_row
    return pl.pallas_call(
        functools.partial(_ffn_kernel, sub=min(tm, 512), chunk=3 * MXU_TILE),
        grid=(n // tm,),
        in_specs=[pl.BlockSpec((tm, d), lambda i: (i, 0)),
                  pl.BlockSpec((None, 1, d), lambda i: (row(i), 0, 3)),
                  pl.BlockSpec((None, 1, d), lambda i: (row(i), 0, 4)),
                  pl.BlockSpec((None, 1, d), lambda i: (row(i), 0, 5)),
                  pl.BlockSpec((1, d), lambda i: (0, 0)),
                  pl.BlockSpec((1, d), lambda i: (0, 0)),
                  _layer_resident(wg, layer),
                  _layer_resident(wu, layer),
                  _layer_resident(wd, layer)],
        out_specs=pl.BlockSpec((tm, d), lambda i: (i, 0)),
        out_shape=jax.ShapeDtypeStruct((n, d), F32),
        scratch_shapes=[pltpu.VMEM((tm, dff), BF)],
        compiler_params=_cparams("parallel"),
        name="ffn",
    )(x2d, mod3, mod3, mod3, g_pre.reshape(1, d), g_post.reshape(1, d), wg, wu, wd)


def _layer(x2d, xc2d, cstack, p, w, layer, *, batch, seq, n_ctx, last, rope_tabs, consts):
    d = x2d.shape[1]
    w_ssm = p["ssm_d"].shape[0]
    w_fn = w["w_fourier"].shape[1]
    w_na = (w["w_in"].shape[2] - w_ssm - w_fn) // 3
    widths = (w_ssm, w_fn, w_na)
    heads = w_na // HEAD_DIM
    tm = min(512, seq)
    tmc = min(512, batch * n_ctx)

    mod = _mod_rows(cstack, w["w_mod"], w["b_mod"], layer)
    mod3 = mod.reshape(mod.shape[0], 1, 6 * d)

    cos, sin = rope_tabs
    tm2 = min(2 * tm, seq)
    u3, f, q, k, v = _inproj(x2d, mod3, p["g_pre_mix"], w["w_in"], cos, sin, layer=layer, tm=tm2,
                             tiles_per_batch=seq // tm2, mod_row=None, widths=widths, rope=True)
    ones = jnp.ones((tmc, LANES), F32)
    u3c, fc, qc, kc, vc = _inproj(xc2d, mod3, p["g_pre_mix"], w["w_in"], ones, ones, layer=layer, tm=tmc,
                                  tiles_per_batch=1, mod_row=batch, widths=widths, rope=False)

    z3, zc3 = _ssm(u3, u3c, *w["ssm"], layer=layer, batch=batch, seq=seq, n_ctx=n_ctx)

    l2 = consts["l2"]
    l1 = seq // l2
    g = _dft_a(consts["dft_a"], f.reshape(batch, l2, l1 * w_fn))
    pf = _dft_b(consts["dft_b"], g.reshape(batch, 2, l2, l1, w_fn), kb=min(8, l2))

    yna = _na_latent(q, k, v, kc, vc, p["na_rpb"], batch=batch, seq=seq, n_ctx=n_ctx, rq=consts["rq"])

    w_glu, w_fo, w_out = w["w_glu"], w["w_fourier"], w["w_out"]
    x2d = _outproj(z3, pf, yna, x2d, mod3, p["g_post_mix"], w_glu, consts["cs_lat"], w_fo, w_out,
                   layer=layer, tm=tm2, tiles_per_batch=seq // tm2, mod_row=None, widths=widths)
    wg, wu, wd = w["w_ffn_gate"], w["w_ffn_up"], w["w_ffn_down"]
    x2d = _ffn(x2d, mod3, p["g_pre_ffn"], p["g_post_ffn"], wg, wu, wd,
               layer=layer, tm=tm2, tiles_per_batch=seq // tm2, mod_row=None)

    if not last:
        pfc = _dft_ctx(consts["dft_ctx"], fc.reshape(batch, n_ctx, w_fn))
        ynac = _na_ctx(qc, kc, vc, batch=batch, n_ctx=n_ctx, heads=heads)
        tc = min(tmc, n_ctx)
        xc2d = _outproj(zc3, pfc, ynac, xc2d, mod3, p["g_post_mix"], w_glu, consts["cs_ctx"], w_fo, w_out,
                        layer=layer, tm=tc, tiles_per_batch=n_ctx // tc, mod_row=batch, widths=widths)
        xc2d = _ffn(xc2d, mod3, p["g_pre_ffn"], p["g_post_ffn"], wg, wu, wd,
                    layer=layer, tm=tc, tiles_per_batch=n_ctx // tc, mod_row=batch)
    else:
        xc2d = None
    return x2d, xc2d


def kernel(x, c, ctx, c_ctx, w_mod, b_mod, g_pre_mix, g_post_mix, w_in, ssm_a_re, ssm_a_im, ssm_log_dt,
           ssm_b_re, ssm_b_im, ssm_c_re, ssm_c_im, ssm_d, w_glu, w_fourier, na_rpb, w_out, g_pre_ffn,
           g_post_ffn, w_ffn_gate, w_ffn_up, w_ffn_down):
    batch, seq, d = x.shape
    n_ctx = ctx.shape[1]
    depth = w_mod.shape[0]
    w_fn = w_fourier.shape[1]
    params = dict(g_pre_mix=g_pre_mix, g_post_mix=g_post_mix, ssm_d=ssm_d, na_rpb=na_rpb,
                  g_pre_ffn=g_pre_ffn, g_post_ffn=g_post_ffn)
    seg_len = (seq + 2 * n_ctx) // SSM_T // SSM_SEG
    ssm_ops = jax.vmap(functools.partial(_ssm_weights, seg_len=seg_len))(
        ssm_a_re, ssm_a_im, ssm_log_dt, ssm_b_re, ssm_b_im, ssm_c_re, ssm_c_im, ssm_d)
    weights = dict(w_mod=w_mod, b_mod=b_mod, ssm=ssm_ops,
                   w_in=_cast_bf16(w_in), w_glu=_cast_bf16(w_glu), w_fourier=_cast_bf16(w_fourier),
                   w_out=_cast_bf16(w_out), w_ffn_gate=_cast_bf16(w_ffn_gate),
                   w_ffn_up=_cast_bf16(w_ffn_up), w_ffn_down=_cast_bf16(w_ffn_down))

    nrow = -(-(batch + 1) // 8) * 8
    cstack = jnp.concatenate([c, c_ctx[None, :], jnp.zeros((nrow - batch - 1, d), c.dtype)], axis=0)

    l2 = 64 if seq % (64 * 8) == 0 else 8
    l1 = seq // l2
    consts = dict(
        l2=l2,
        rq=4,
        dft_a=_dft_rows(l2).astype(BF),
        dft_b=_dft_b_mats(seq, l1, l2).astype(BF),
        dft_ctx=_dft_rows(n_ctx).astype(BF),
        cs_lat=_channel_dft(w_fn, seq).astype(BF),
        cs_ctx=_channel_dft(w_fn, n_ctx).astype(BF),
    )
    rope_tabs = _rope_tables(seq)

    x2d = x.reshape(batch * seq, d)
    xc2d = ctx.reshape(batch * n_ctx, d)
    for layer in range(depth):
        p = {name: val[layer] for name, val in params.items()}
        x2d, xc2d = _layer(x2d, xc2d, cstack, p, weights, layer, batch=batch, seq=seq, n_ctx=n_ctx,
                           last=(layer == depth - 1), rope_tabs=rope_tabs, consts=consts)
    return x2d.reshape(batch, seq, d)
```

```python
import functools
import math

import numpy as np
import jax
import jax.numpy as jnp
from jax import lax
from jax.experimental import pallas as pl
from jax.experimental.pallas import tpu as pltpu

BF = jnp.bfloat16
F32 = jnp.float32

EPS = 1e-6
GRID_W = 64
HEAD_DIM = 64
NA_ROWS = 8
NA_COLS = 16
ROPE_BASE = 10000.0
SSM_GROUP = 16
SSM_STATE = 64
FNET_GROUP = 64
LANES = 128
MXU_TILE = 256
SSM_T = 8
SSM_SEG = 8
SSM_STAGES = (4, 2)
NEG = -1e30
VMEM_LIMIT = 56 * 1024 * 1024


def _cparams(*sem):
    return pltpu.CompilerParams(dimension_semantics=sem, vmem_limit_bytes=VMEM_LIMIT)


def _resident(shape, index_map):
    return pl.BlockSpec(shape, index_map, pipeline_mode=pl.Buffered(1))


def _layer_resident(w_all, layer):
    return pl.BlockSpec((None,) + w_all.shape[1:], lambda *_: (layer, 0, 0), pipeline_mode=pl.Buffered(1))


def _cast_kernel(w_ref, o_ref):
    o_ref[...] = w_ref[...].astype(BF)


def _cast_bf16(w_all):
    depth, rows, cols = w_all.shape
    br = rows
    while br * cols * 4 > 4 * 1024 * 1024 and br % 32 == 0:
        br //= 2
    spec = pl.BlockSpec((None, br, cols), lambda l, i: (l, i, 0))
    return pl.pallas_call(
        _cast_kernel,
        grid=(depth, rows // br),
        in_specs=[spec],
        out_specs=spec,
        out_shape=jax.ShapeDtypeStruct(w_all.shape, BF),
        compiler_params=_cparams("parallel", "parallel"),
        name="cast_bf16",
    )(w_all)


def _mod_kernel(c_ref, w_ref, b_ref, o_ref):
    cs = c_ref[...]
    s = (cs * jax.nn.sigmoid(cs)).astype(BF)
    o_ref[...] = jnp.dot(s, w_ref[...].astype(BF), preferred_element_type=F32) + b_ref[...]


def _mod_rows(cstack, w_mod, b_mod, layer):
    rows, d = cstack.shape
    depth, _, n = w_mod.shape
    return pl.pallas_call(
        _mod_kernel,
        grid=(n // d,),
        in_specs=[pl.BlockSpec((rows, d), lambda i: (0, 0)),
                  pl.BlockSpec((None, d, d), lambda i: (layer, 0, i)),
                  pl.BlockSpec((None, 1, d), lambda i: (layer, 0, i))],
        out_specs=pl.BlockSpec((rows, d), lambda i: (0, i)),
        out_shape=jax.ShapeDtypeStruct((rows, n), F32),
        compiler_params=_cparams("arbitrary"),
        name="mod_rows",
    )(cstack, w_mod, b_mod.reshape(depth, 1, n))


def _rope_tile(t, cos, sin, first):
    partner = jnp.where(first, pltpu.roll(t, LANES - 16, 1), pltpu.roll(t, 16, 1))
    return t * cos + partner * sin


def _inproj_kernel(x_ref, sh_ref, sc_ref, g_ref, w_ref, cos_ref, sin_ref,
                   u_ref, f_ref, q_ref, k_ref, v_ref, *, widths, rope, sub):
    w_ssm, w_fn, w_na = widths
    scale = HEAD_DIM ** -0.5 * math.log2(math.e)
    for r0 in range(0, x_ref.shape[0], sub):
        rs = slice(r0, r0 + sub)
        x = x_ref[rs, :]
        ms = jnp.mean(x * x, axis=-1, keepdims=True)
        xn = x * lax.rsqrt(ms + EPS) * g_ref[...]
        m = (xn * (1.0 + sc_ref[...]) + sh_ref[...]).astype(BF)

        h = jnp.dot(m, w_ref[...], preferred_element_type=F32)

        def proj(lo, n, h=h):
            return h[:, lo:lo + n]

        u = proj(0, w_ssm)
        for j in range(w_ssm // LANES):
            u_ref[j, rs, :] = u[:, j * LANES:(j + 1) * LANES]
        f_ref[rs, :] = proj(w_ssm, w_fn).astype(BF)
        q = proj(w_ssm + w_fn, w_na)
        k = proj(w_ssm + w_fn + w_na, w_na)
        v_ref[rs, :] = proj(w_ssm + w_fn + 2 * w_na, w_na).astype(BF)
        if rope:
            cos = cos_ref[rs, :]
            sin = sin_ref[rs, :]
            lane = lax.broadcasted_iota(jnp.int32, cos.shape, 1)
            first = (lane % 32) < 16
            for j in range(w_na // LANES):
                sl = slice(j * LANES, (j + 1) * LANES)
                q_ref[rs, sl] = (_rope_tile(q[:, sl], cos, sin, first) * scale).astype(BF)
                k_ref[rs, sl] = _rope_tile(k[:, sl], cos, sin, first).astype(BF)
        else:
            q_ref[rs, :] = (q * scale).astype(BF)
            k_ref[rs, :] = k.astype(BF)


def _inproj(x2d, mod3, g_pre, w_in, cos, sin, *, layer, tm, tiles_per_batch, mod_row, widths, rope):
    n, d = x2d.shape
    w_ssm, w_fn, w_na = widths
    nt = n // tm
    if mod_row is None:
        row = lambda i: i // tiles_per_batch
    else:
        row = lambda i: mod_row
    pos = lambda i: (i % tiles_per_batch, 0)
    kern = functools.partial(_inproj_kernel, widths=widths, rope=rope, sub=min(tm, 512))
    return pl.pallas_call(
        kern,
        grid=(nt,),
        in_specs=[pl.BlockSpec((tm, d), lambda i: (i, 0)),
                  pl.BlockSpec((None, 1, d), lambda i: (row(i), 0, 0)),
                  pl.BlockSpec((None, 1, d), lambda i: (row(i), 0, 1)),
                  pl.BlockSpec((1, d), lambda i: (0, 0)),
                  _layer_resident(w_in, layer),
                  pl.BlockSpec((tm, LANES), pos),
                  pl.BlockSpec((tm, LANES), pos)],
        out_specs=[pl.BlockSpec((w_ssm // LANES, tm, LANES), lambda i: (0, i, 0)),
                   pl.BlockSpec((tm, w_fn), lambda i: (i, 0)),
                   pl.BlockSpec((tm, w_na), lambda i: (i, 0)),
                   pl.BlockSpec((tm, w_na), lambda i: (i, 0)),
                   pl.BlockSpec((tm, w_na), lambda i: (i, 0))],
        out_shape=[jax.ShapeDtypeStruct((w_ssm // LANES, n, LANES), F32),
                   jax.ShapeDtypeStruct((n, w_fn), BF),
                   jax.ShapeDtypeStruct((n, w_na), BF),
                   jax.ShapeDtypeStruct((n, w_na), BF),
                   jax.ShapeDtypeStruct((n, w_na), BF)],
        compiler_params=_cparams("parallel"),
        name="inproj_rope" if rope else "inproj_ctx",
    )(x2d, mod3, mod3, g_pre.reshape(1, d), w_in, cos, sin)


def _rope_tables(seq):
    t = np.arange(seq)
    row = (t // GRID_W).astype(np.float32)
    col = (t % GRID_W).astype(np.float32)
    quarter = HEAD_DIM // 4
    freqs = (np.float32(ROPE_BASE) ** (-np.arange(quarter, dtype=np.float32) / quarter)).astype(np.float32)
    d = np.arange(LANES) % HEAD_DIM
    use_col = (d // (HEAD_DIM // 2)) == 1
    fidx = d % quarter
    sign = np.where((d % (HEAD_DIM // 2)) // quarter == 0, -1.0, 1.0)
    pos = np.where(use_col[None, :], col[:, None], row[:, None])
    ang = (pos * freqs[fidx][None, :]).astype(np.float32).astype(np.float64)
    return jnp.asarray(np.cos(ang), F32), jnp.asarray(np.sin(ang) * sign[None, :], F32)


def _gelu_tanh(x):
    c = math.sqrt(2.0 / math.pi)
    return x * (0.5 * (1.0 + jnp.tanh(c * (x + 0.044715 * (x * x * x)))))


def _group_of(idx, width, groups):
    shift = width.bit_length() - 1
    assert width == 1 << shift and groups & (groups - 1) == 0
    return lax.bitwise_and(lax.shift_right_logical(idx, shift), groups - 1)


def _same_group(shape, row0, row_width, col0, col_width, groups):
    r = lax.broadcasted_iota(jnp.int32, shape, 0) + row0
    c = lax.broadcasted_iota(jnp.int32, shape, 1) + col0
    return _group_of(r, row_width, groups) == _group_of(c, col_width, groups)


def _swap_stage(tiles, d):
    lane = lax.broadcasted_iota(jnp.int32, tiles[0].shape, 1)
    upper = lax.bitwise_and(lax.shift_right_logical(lane, SSM_GROUP.bit_length() - 1), d) != 0
    out = list(tiles)
    for x in range(len(tiles)):
        if x & d:
            continue
        a, b = tiles[x], tiles[x + d]
        out[x] = jnp.where(upper, pltpu.roll(b, SSM_GROUP * d, 1), a)
        out[x + d] = jnp.where(upper, b, pltpu.roll(a, LANES - SSM_GROUP * d, 1))
    return out


def _ssm_kernel(ul_ref, uc_ref, wc_ref, lag_ref, dc_ref, rexp_ref, cexp_ref, lam_ref, lamseg_ref,
                z_ref, zc_ref, u_s, bm_s, kin_s, cm_s, s_ref, h_ref, e_ref, *, seg_len, n_lat, n_ctx, stages):
    T = SSM_T
    ns = 2 ** len(stages)
    gs = T // ns
    sw = gs * LANES
    rows = u_s.shape[1]
    tps = s_ref.shape[0] // ns
    tpq = tps // 4
    nst = tpq * LANES
    gpt = LANES // SSM_GROUP
    per_q = gpt * SSM_STATE

    def lanes(c):
        return slice(c * LANES, (c + 1) * LANES)

    def col0(s, q):
        return q * per_q + s * nst

    def swapped(tiles, order):
        for d in order:
            tiles = _swap_stage(tiles, d)
        return tiles

    @pl.when(pl.program_id(1) == 0)
    def _expand_operators():
        rexp = rexp_ref[...]
        cexp = cexp_ref[...]
        for s in range(ns):
            wsel = jnp.concatenate([wc_ref[:, col0(s, q):col0(s, q) + nst] for q in range(4)], axis=1)
            blk = jnp.dot(rexp, wsel, preferred_element_type=F32)
            keep = _same_group(blk.shape, 0, SSM_GROUP, 0, SSM_STATE, gs)
            bm_s[s] = jnp.where(keep, blk, 0.0).astype(BF)
            dsel = jnp.concatenate([dc_ref[col0(s, q):col0(s, q) + nst, :] for q in range(4)], axis=0)
            blk = jnp.dot(dsel, cexp, preferred_element_type=F32)
            keep = _same_group(blk.shape, 0, SSM_STATE, 0, SSM_GROUP, gs)
            cm_s[s] = jnp.where(keep, blk, 0.0).astype(BF)
        r16 = lax.broadcasted_iota(jnp.int32, (LANES, SSM_GROUP), 0)
        c16 = lax.broadcasted_iota(jnp.int32, (LANES, SSM_GROUP), 1)
        chan = jnp.where(lax.bitwise_and(r16, SSM_GROUP - 1) == c16, 1.0, 0.0).astype(BF)
        for t in range(T):
            lo = (T - 1 - t) * LANES
            blk = jnp.dot(chan, lag_ref[:, lo:lo + T * LANES], preferred_element_type=F32)
            keep = _same_group(blk.shape, 0, SSM_GROUP, 0, SSM_GROUP, gpt)
            blk = jnp.where(keep, blk, 0.0)
            tiles = swapped([blk[:, lanes(x)] for x in range(T)], stages)
            r0 = (t % gs) * LANES + (t // gs) * gs * SSM_GROUP
            for s in range(ns):
                part = jnp.concatenate(tiles[s * gs:(s + 1) * gs], axis=1)
                kin_s[s, r0:r0 + gs * SSM_GROUP, :] = part[s * gs * SSM_GROUP:(s + 1) * gs * SSM_GROUP, :].astype(BF)

    xs = []
    for t in range(T):
        ct = uc_ref[pl.ds(t, n_ctx, stride=T), :]
        xs.append(jnp.concatenate([ct, ul_ref[pl.ds(t, n_lat, stride=T), :], ct], axis=0))
    xs = swapped(xs, stages)
    for s in range(ns):
        for k in range(gs):
            u_s[s, :, lanes(k)] = xs[s * gs + k].astype(BF)
    for s in range(ns):
        r = jnp.dot(u_s[s], bm_s[s], preferred_element_type=F32)
        for c in range(tps):
            s_ref[s * tps + c] = r[:, lanes(c)]

    chains = []
    for s in range(ns):
        for d in range(2):
            for k in range(tpq):
                cr, ci = s * tps + 2 * d * tpq + k, s * tps + (2 * d + 1) * tpq + k
                lr = slice(col0(s, 2 * d) + k * LANES, col0(s, 2 * d) + (k + 1) * LANES)
                li = slice(col0(s, 2 * d + 1) + k * LANES, col0(s, 2 * d + 1) + (k + 1) * LANES)
                chains.append((cr, ci, d, lr, li))
    nt = ns * tps

    def scan_step(i, carry, record):
        new = list(carry)
        for cr, ci, d, lr_sl, li_sl in chains:
            rsl = pl.ds(i if d == 0 else seg_len - 1 - i, SSM_SEG, stride=seg_len)
            hr, hi = carry[cr], carry[ci]
            if record:
                h_ref[cr, rsl, :] = hr
                h_ref[ci, rsl, :] = hi
            lr, li = lam_ref[:, lr_sl], lam_ref[:, li_sl]
            new[cr] = lr * hr - li * hi + s_ref[cr, rsl, :]
            new[ci] = lr * hi + li * hr + s_ref[ci, rsl, :]
        return tuple(new)

    zero = jnp.zeros((SSM_SEG, LANES), F32)
    fin = lax.fori_loop(0, seg_len, lambda i, c: scan_step(i, c, False), (zero,) * nt, unroll=2)
    for c in range(nt):
        e_ref[c, 0:SSM_SEG, :] = fin[c]

    for cr, ci, d, lr_sl, li_sl in chains:
        lr, li = lamseg_ref[:, lr_sl], lamseg_ref[:, li_sl]
        er = jnp.zeros((1, LANES), F32)
        ei = jnp.zeros((1, LANES), F32)
        for n in range(SSM_SEG):
            g = n if d == 0 else SSM_SEG - 1 - n
            e_ref[cr, SSM_SEG + g:SSM_SEG + g + 1, :] = er
            e_ref[ci, SSM_SEG + g:SSM_SEG + g + 1, :] = ei
            fr, fi = e_ref[cr, g:g + 1, :], e_ref[ci, g:g + 1, :]
            er, ei = lr * er - li * ei + fr, lr * ei + li * er + fi

    ent = tuple(e_ref[c, SSM_SEG:2 * SSM_SEG, :] for c in range(nt))
    lax.fori_loop(0, seg_len, lambda i, c: scan_step(i, c, True), ent, unroll=2)

    ys, ycs = [], []
    for s in range(ns):
        hcat = jnp.concatenate([h_ref[s * tps + c].astype(BF) for c in range(tps)], axis=1)
        y = _gelu_tanh(jnp.dot(u_s[s], kin_s[s], preferred_element_type=F32)
                       + jnp.dot(hcat, cm_s[s], preferred_element_type=F32))
        ys += [y[:, lanes(k)] for k in range(gs)]
        hc = jnp.concatenate(
            [h_ref[s * tps + c, 0:n_ctx, :].astype(BF) for c in range(2 * tpq)]
            + [h_ref[s * tps + c, rows - n_ctx:rows, :].astype(BF) for c in range(2 * tpq, tps)], axis=1)
        yc = _gelu_tanh(jnp.dot(u_s[s, 0:n_ctx, :], kin_s[s], preferred_element_type=F32)
                        + jnp.dot(hc, cm_s[s], preferred_element_type=F32))
        ycs += [yc[:, lanes(k)] for k in range(gs)]
    ys = swapped(ys, stages[::-1])
    ycs = swapped(ycs, stages[::-1])
    for t in range(T):
        z_ref[pl.ds(t, n_lat, stride=T), :] = ys[t][n_ctx:n_ctx + n_lat, :]
        zc_ref[pl.ds(t, n_ctx, stride=T), :] = ycs[t]


def _ssm_weights(a_re, a_im, log_dt, b_re, b_im, c_re, c_im, d_skip, seg_len):
    T = SSM_T
    g, p = a_re.shape[1], a_re.shape[2]
    hc = b_re.shape[-1]
    gpt = LANES // hc
    J = g // gpt
    lam = lax.complex(a_re.astype(F32), a_im.astype(F32))
    dt = jnp.exp(log_dt.astype(F32))[..., None]
    ldt = lam * dt
    lam_bar = jnp.exp(ldt)
    bbar = ((lam_bar - 1) / lam)[..., None] * lax.complex(b_re.astype(F32), b_im.astype(F32))
    cmat = lax.complex(c_re.astype(F32), c_im.astype(F32))
    kk = jnp.arange(T + 1, dtype=F32)
    pw = jnp.exp(ldt[:, None] * kk[None, :, None, None])

    wf = pw[0, T - 1 - jnp.arange(T)][:, :, None, :] * jnp.swapaxes(bbar[0], 1, 2)[None]
    wb = pw[1, jnp.arange(T)][:, :, None, :] * jnp.swapaxes(bbar[1], 1, 2)[None]
    wq = jnp.stack([wf.real, wf.imag, wb.real, wb.imag]).reshape(4, T, J, gpt, hc, p)
    wc = wq.transpose(2, 1, 4, 0, 3, 5).reshape(J, T * hc, 4 * gpt * p)

    df = cmat[0][None] * pw[0, 1 + jnp.arange(T)][:, :, None, :]
    db = cmat[1][None] * pw[1, T - jnp.arange(T)][:, :, None, :]
    dq = jnp.stack([df.real, -df.imag, db.real, -db.imag]).reshape(4, T, J, gpt, hc, p)
    dc = dq.transpose(2, 0, 3, 5, 1, 4).reshape(J, 4 * gpt * p, T * hc)

    mf = jnp.einsum('gcp,kgp,gph->kgch', cmat[0], pw[0, :T], bbar[0]).real
    mb = jnp.einsum('gcp,kgp,gph->kgch', cmat[1], pw[1, :T], bbar[1]).real
    skip = d_skip.astype(F32).reshape(g, hc)[:, :, None] * jnp.eye(hc, dtype=F32)[None]
    m0 = mf[0] + mb[0] + skip
    lags = jnp.concatenate([mb[1:][::-1], m0[None], mf[1:]], axis=0)
    kc = lags.reshape(2 * T - 1, J, gpt, hc, hc).transpose(1, 4, 0, 2, 3).reshape(J, hc, (2 * T - 1) * LANES)

    def lam_rows(power):
        lp = jnp.exp(ldt * power)
        lt = lp.reshape(2, J, gpt * p)
        return jnp.concatenate([lt[0].real, lt[0].imag, lt[1].real, lt[1].imag], axis=-1)[:, None, :]

    lam_t = jnp.broadcast_to(lam_rows(float(T)), (J, SSM_SEG, 4 * gpt * p))
    return wc.astype(BF), kc.astype(BF), dc.astype(BF), lam_t, lam_rows(float(T * seg_len))


def _ssm_expanders(gs):
    r = np.arange(gs * LANES)
    k, slot, ch = r // LANES, (r % LANES) // SSM_GROUP, r % SSM_GROUP
    t = (slot // gs) * gs + k
    rexp = np.zeros((gs * LANES, SSM_T * SSM_GROUP), np.float32)
    rexp[r, t * SSM_GROUP + ch] = 1.0
    return jnp.asarray(rexp, BF), jnp.asarray(rexp.T, BF)


def _ssm(u3, u3c, wc, kc, dc, lam, lamseg, *, layer, batch, seq, n_ctx):
    T = SSM_T
    J = u3.shape[0]
    ns = wc.shape[-1]
    nsets = 2 ** len(SSM_STAGES)
    gs = T // nsets
    sw = gs * LANES
    n_lat, n_c = seq // T, n_ctx // T
    rows = n_lat + 2 * n_c
    seg_len = rows // SSM_SEG
    assert rows % SSM_SEG == 0 and n_c % 16 == 0
    rexp, cexp = _ssm_expanders(gs)
    kern = functools.partial(_ssm_kernel, seg_len=seg_len, n_lat=n_lat, n_ctx=n_c, stages=SSM_STAGES)
    const = lambda j, b: (0, 0)
    per_j = lambda j, b: (layer, j, 0, 0)
    per_jb = lambda j, b: (j, b, 0)
    return pl.pallas_call(
        kern,
        grid=(J, batch),
        in_specs=[pl.BlockSpec((None, seq, LANES), per_jb, pipeline_mode=pl.Buffered(1)),
                  pl.BlockSpec((None, n_ctx, LANES), per_jb),
                  pl.BlockSpec((None, None) + wc.shape[2:], per_j),
                  pl.BlockSpec((None, None) + kc.shape[2:], per_j),
                  pl.BlockSpec((None, None) + dc.shape[2:], per_j),
                  pl.BlockSpec(rexp.shape, const),
                  pl.BlockSpec(cexp.shape, const),
                  pl.BlockSpec((None, None, SSM_SEG, ns), per_j),
                  pl.BlockSpec((None, None, 1, ns), per_j)],
        out_specs=[pl.BlockSpec((None, seq, LANES), per_jb),
                   pl.BlockSpec((None, n_ctx, LANES), per_jb)],
        out_shape=[jax.ShapeDtypeStruct(u3.shape, F32),
                   jax.ShapeDtypeStruct(u3c.shape, F32)],
        scratch_shapes=[pltpu.VMEM((nsets, rows, sw), BF),
                        pltpu.VMEM((nsets, sw, ns // nsets), BF),
                        pltpu.VMEM((nsets, sw, sw), BF),
                        pltpu.VMEM((nsets, ns // nsets, sw), BF),
                        pltpu.VMEM((ns // LANES, rows, LANES), F32),
                        pltpu.VMEM((ns // LANES, rows, LANES), F32),
                        pltpu.VMEM((ns // LANES, 2 * SSM_SEG, LANES), F32)],
        compiler_params=_cparams("arbitrary", "arbitrary"),
        name="ssm_chunk_scan",
    )(u3, u3c, wc, kc, dc, rexp, cexp, lam, lamseg)


def _dft_a_kernel(w_ref, x_ref, o_ref):
    o_ref[...] = jnp.dot(w_ref[...], x_ref[...], preferred_element_type=F32).astype(BF)


def _dft_rows(n):
    k = np.arange(n)
    m = (k[:, None] * k[None, :]) % n
    ang = m * (2.0 * math.pi / n)
    return jnp.asarray(np.concatenate([np.cos(ang), -np.sin(ang)], axis=0), BF)


def _dft_a(w, x3):
    B, n, cols = x3.shape
    tn = min(cols, 4096)
    return pl.pallas_call(
        _dft_a_kernel,
        grid=(B, cols // tn),
        in_specs=[pl.BlockSpec((2 * n, n), lambda b, i: (0, 0)),
                  pl.BlockSpec((None, n, tn), lambda b, i: (b, 0, i))],
        out_specs=pl.BlockSpec((None, 2 * n, tn), lambda b, i: (b, 0, i)),
        out_shape=jax.ShapeDtypeStruct((B, 2 * n, cols), BF),
        compiler_params=_cparams("parallel", "parallel"),
        name="dft_stage_a",
    )(w, x3)


def _store_lane_tiles(o_ref, p, n, wf, idx):
    per = wf // LANES
    for c in range(2 * per):
        part = p[(c // per) * n:(c // per + 1) * n, (c % per) * LANES:(c % per + 1) * LANES]
        o_ref[(c,) + idx] = part


def _dft_b_kernel(a_ref, g_ref, o_ref, *, kb, l1, wf):
    for i in range(kb):
        g = jnp.concatenate([g_ref[0, i], g_ref[1, i]], axis=0)
        p = jnp.dot(a_ref[i], g, preferred_element_type=F32)
        _store_lane_tiles(o_ref, p, l1, wf, (slice(None), i, slice(None)))


def _dft_b_mats(seq, l1, l2):
    k2 = np.arange(l2)[:, None, None]
    k1 = np.arange(l1)[None, :, None]
    j1 = np.arange(l1)[None, None, :]
    m = ((k1 * l2 + k2) * j1) % seq
    ang = m * (2.0 * math.pi / seq)
    ar, ai = np.cos(ang).astype(np.float32), -np.sin(ang).astype(np.float32)
    top = np.concatenate([ar, -ai], axis=2)
    bot = np.concatenate([ai, ar], axis=2)
    return jnp.asarray(np.concatenate([top, bot], axis=1), BF)


def _dft_b(amat, g5, *, kb):
    B, _, l2, l1, wf = g5.shape
    nlt = 2 * wf // LANES
    kern = functools.partial(_dft_b_kernel, kb=kb, l1=l1, wf=wf)
    out = pl.pallas_call(
        kern,
        grid=(B, l2 // kb),
        in_specs=[pl.BlockSpec((kb, 2 * l1, 2 * l1), lambda b, i: (i, 0, 0)),
                  pl.BlockSpec((None, 2, kb, l1, wf), lambda b, i: (b, 0, i, 0, 0))],
        out_specs=pl.BlockSpec((None, nlt, l1, kb, LANES), lambda b, i: (b, 0, 0, i, 0)),
        out_shape=jax.ShapeDtypeStruct((B, nlt, l1, l2, LANES), F32),
        compiler_params=_cparams("parallel", "parallel"),
        name="dft_stage_b",
    )(amat, g5)
    return out.reshape(B, nlt, l1 * l2, LANES)


def _dft_ctx_kernel(w_ref, x_ref, o_ref, *, n, wf):
    p = jnp.dot(w_ref[...], x_ref[...], preferred_element_type=F32)
    _store_lane_tiles(o_ref, p, n, wf, ())


def _dft_ctx(w, x3):
    B, n, wf = x3.shape
    nlt = 2 * wf // LANES
    kern = functools.partial(_dft_ctx_kernel, n=n, wf=wf)
    return pl.pallas_call(
        kern,
        grid=(B,),
        in_specs=[pl.BlockSpec((2 * n, n), lambda b: (0, 0)),
                  pl.BlockSpec((None, n, wf), lambda b: (b, 0, 0))],
        out_specs=pl.BlockSpec((None, nlt, n, LANES), lambda b: (b, 0, 0, 0)),
        out_shape=jax.ShapeDtypeStruct((B, nlt, n, LANES), F32),
        compiler_params=_cparams("parallel"),
        name="dft_ctx",
    )(w, x3)


def _channel_dft(wf, seq):
    c = np.arange(wf)
    same = (c[:, None] // FNET_GROUP) == (c[None, :] // FNET_GROUP)
    m = ((c[:, None] % FNET_GROUP) * (c[None, :] % FNET_GROUP)) % FNET_GROUP
    ang = m * (2.0 * math.pi / FNET_GROUP)
    mask = same / math.sqrt(seq * FNET_GROUP)
    return jnp.asarray(np.concatenate([np.cos(ang) * mask, np.sin(ang) * mask], axis=0), BF)


def _softmax_pv(parts, l_shape):
    m = parts[0][0].max(axis=-1, keepdims=True)
    for s, _ in parts[1:]:
        m = jnp.maximum(m, s.max(axis=-1, keepdims=True))
    l = jnp.zeros(l_shape, F32)
    o = None
    for s, v in parts:
        p = jnp.exp2(s - m)
        l = l + p.sum(axis=-1, keepdims=True)
        pv = jnp.dot(p.astype(BF), v, preferred_element_type=F32)
        o = pv if o is None else o + pv
    return o / l


def _na_kernel(kb_ref, pid_ref, dr_ref, q_ref, k_ref, v_ref, kc_ref, vc_ref, cbx_ref, o_ref, bias_ref,
               *, nkeys, heads):
    i = pl.program_id(1)
    start = pl.multiple_of(kb_ref[i] * GRID_W, GRID_W)
    nq = q_ref.shape[0]
    rq, kr = nq // GRID_W, nkeys // GRID_W

    pid = pid_ref[i]
    @pl.when((i == 0) | (pid != pid_ref[jnp.maximum(i - 1, 0)]))
    def _build_bias():
        for qi in range(rq):
            for w in range(kr):
                a = dr_ref[(pid * rq + qi) * kr + w]
                for h in range(heads):
                    bias_ref[h, qi * GRID_W:(qi + 1) * GRID_W, w * GRID_W:(w + 1) * GRID_W] = cbx_ref[h, a]

    lane = lax.broadcasted_iota(jnp.int32, (nq, LANES), 1)
    low = lane < HEAD_DIM
    dn = (((1,), (1,)), ((), ()))
    for hp in range(heads // 2):
        sl = slice(hp * LANES, (hp + 1) * LANES)
        q2 = q_ref[:, sl]
        k2 = k_ref[pl.ds(start, nkeys), sl]
        v2 = v_ref[pl.ds(start, nkeys), sl]
        kc2 = kc_ref[:, sl]
        vc2 = vc_ref[:, sl]
        outs = []
        for hh in range(2):
            qm = jnp.where(low if hh == 0 else jnp.logical_not(low), q2, jnp.zeros_like(q2))
            s_w = lax.dot_general(qm, k2, dn, preferred_element_type=F32) + bias_ref[2 * hp + hh]
            s_c = lax.dot_general(qm, kc2, dn, preferred_element_type=F32)
            outs.append(_softmax_pv([(s_w, v2), (s_c, vc2)], (nq, 1)))
        o_ref[:, sl] = jnp.where(low, outs[0], outs[1]).astype(BF)


def _na_plan(rows, rq):
    wr = min(NA_ROWS, rows)
    kr = rq + wr - 1
    nblk = rows // rq
    kbs, pids, pats = [], [], []
    for blk in range(nblk):
        r0 = blk * rq
        rs = [int(np.clip(r0 + i - wr // 2, 0, rows - wr)) for i in range(rq)]
        kb = min(rs[0], rows - kr)
        pat = (tuple(r - kb for r in rs), r0 - kb)
        if pat not in pats:
            pats.append(pat)
        kbs.append(kb)
        pids.append(pats.index(pat))
    return np.asarray(kbs, np.int32), np.asarray(pids, np.int32), pats, kr, wr


def _na_bias(rpb, pats, rq, kr, wr):
    cols = np.arange(GRID_W)
    cstart = np.clip(cols - NA_COLS // 2, 0, GRID_W - NA_COLS)
    kc = np.arange(GRID_W)
    col_ok = (kc[None, :] >= cstart[:, None]) & (kc[None, :] < cstart[:, None] + NA_COLS)
    dc = kc[None, :] - cols[:, None] + NA_COLS - 1
    heads, nr, ncol = rpb.shape
    sel = (dc[:, :, None] == np.arange(ncol)[None, None, :]) & col_ok[:, :, None]
    cb = jnp.einsum('hab,ckb->hack', rpb.astype(F32), jnp.asarray(sel, F32), precision=lax.Precision.HIGHEST)
    cb = jnp.where(jnp.asarray(col_ok)[None, None], cb * math.log2(math.e), NEG)
    cbx = jnp.concatenate([cb, jnp.full((heads, 1, GRID_W, GRID_W), NEG, F32)], axis=1)
    slots = []
    for rs_off, r_off in pats:
        i = np.arange(rq)[:, None]
        w = np.arange(kr)[None, :]
        rso = np.asarray(rs_off)[:, None]
        row_ok = (w >= rso) & (w < rso + wr)
        slots.append(np.where(row_ok, w - r_off - i + NA_ROWS - 1, nr))
    return cbx, np.stack(slots).reshape(-1).astype(np.int32)


def _na_latent(q, k, v, kc, vc, rpb, *, batch, seq, n_ctx, rq):
    rows = seq // GRID_W
    heads = rpb.shape[0]
    kbs, pids, pats, kr, wr = _na_plan(rows, rq)
    cbx, slots = _na_bias(rpb, pats, rq, kr, wr)
    nblk = rows // rq
    nq = rq * GRID_W
    nkeys = kr * GRID_W
    w = q.shape[1]
    kern = functools.partial(_na_kernel, nkeys=nkeys, heads=heads)
    grid_spec = pltpu.PrefetchScalarGridSpec(
        num_scalar_prefetch=3,
        grid=(batch, nblk),
        in_specs=[pl.BlockSpec((nq, w), lambda b, i, *_: (b * nblk + i, 0)),
                  pl.BlockSpec((seq, w), lambda b, i, *_: (b, 0)),
                  pl.BlockSpec((seq, w), lambda b, i, *_: (b, 0)),
                  pl.BlockSpec((n_ctx, w), lambda b, i, *_: (b, 0)),
                  pl.BlockSpec((n_ctx, w), lambda b, i, *_: (b, 0)),
                  pl.BlockSpec(cbx.shape, lambda b, i, *_: (0, 0, 0, 0))],
        out_specs=pl.BlockSpec((nq, w), lambda b, i, *_: (b * nblk + i, 0)),
        scratch_shapes=[pltpu.VMEM((heads, nq, nkeys), F32)],
    )
    return pl.pallas_call(
        kern,
        grid_spec=grid_spec,
        out_shape=jax.ShapeDtypeStruct(q.shape, BF),
        compiler_params=_cparams("arbitrary", "arbitrary"),
        name="na_latent",
    )(jnp.asarray(kbs), jnp.asarray(pids), jnp.asarray(slots), q, k, v, kc, vc, cbx)


def _na_ctx_kernel(q_ref, k_ref, v_ref, o_ref, *, heads):
    nq = q_ref.shape[0]
    lane = lax.broadcasted_iota(jnp.int32, (nq, LANES), 1)
    low = lane < HEAD_DIM
    dn = (((1,), (1,)), ((), ()))
    for hp in range(heads // 2):
        sl = slice(hp * LANES, (hp + 1) * LANES)
        q2 = q_ref[:, sl]
        k2 = k_ref[:, sl]
        v2 = v_ref[:, sl]
        outs = []
        for hh in range(2):
            qm = jnp.where(low if hh == 0 else jnp.logical_not(low), q2, jnp.zeros_like(q2))
            s = lax.dot_general(qm, k2, dn, preferred_element_type=F32)
            outs.append(_softmax_pv([(s, v2)], (nq, 1)))
        o_ref[:, sl] = jnp.where(low, outs[0], outs[1]).astype(BF)


def _na_ctx(qc, kc, vc, *, batch, n_ctx, heads):
    w = qc.shape[1]
    spec = pl.BlockSpec((n_ctx, w), lambda b: (b, 0))
    return pl.pallas_call(
        functools.partial(_na_ctx_kernel, heads=heads),
        grid=(batch,),
        in_specs=[spec, spec, spec],
        out_specs=spec,
        out_shape=jax.ShapeDtypeStruct(qc.shape, BF),
        compiler_params=_cparams("parallel"),
        name="na_ctx",
    )(qc, kc, vc)


def _outproj_kernel(z_ref, pf_ref, na_ref, x_ref, gate_ref, gpost_ref,
                    wglu_ref, cs_ref, wfo_ref, wout_ref, o_ref, *, widths, sub):
    w_ssm, w_fn, w_na = widths
    for r0 in range(0, x_ref.shape[0], sub):
        rs = slice(r0, r0 + sub)
        z = jnp.concatenate([z_ref[j, rs, :] for j in range(w_ssm // LANES)], axis=1)
        gl = jnp.dot(z.astype(BF), wglu_ref[...], preferred_element_type=F32)
        y_ssm = (z * jax.nn.sigmoid(gl)).astype(BF)
        pf = jnp.concatenate([pf_ref[c, rs, :] for c in range(pf_ref.shape[0])], axis=1).astype(BF)
        mixed = jnp.dot(pf, cs_ref[...], preferred_element_type=F32).astype(BF)
        y_fft = jnp.dot(mixed, wfo_ref[...], preferred_element_type=F32).astype(BF)
        y = jnp.concatenate([y_ssm, y_fft, na_ref[rs, :]], axis=1)
        o = jnp.dot(y, wout_ref[...], preferred_element_type=F32)
        ms = jnp.mean(o * o, axis=-1, keepdims=True)
        o_ref[rs, :] = x_ref[rs, :] + gate_ref[...] * (o * lax.rsqrt(ms + EPS) * gpost_ref[...])


def _outproj(z3, pf, yna, x2d, mod3, g_post, w_glu, cs, w_fo, w_out, *,
             layer, tm, tiles_per_batch, mod_row, widths):
    n, d = x2d.shape
    w_ssm, w_fn, w_na = widths
    if mod_row is None:
        row = lambda i: i // tiles_per_batch
    else:
        row = lambda i: mod_row
    kern = functools.partial(_outproj_kernel, widths=widths, sub=min(tm, 512))
    return pl.pallas_call(
        kern,
        grid=(n // tm,),
        in_specs=[pl.BlockSpec((w_ssm // LANES, tm, LANES), lambda i: (0, i, 0)),
                  pl.BlockSpec((None, pf.shape[1], tm, LANES),
                               lambda i: (i // tiles_per_batch, 0, i % tiles_per_batch, 0)),
                  pl.BlockSpec((tm, w_na), lambda i: (i, 0)),
                  pl.BlockSpec((tm, d), lambda i: (i, 0)),
                  pl.BlockSpec((None, 1, d), lambda i: (row(i), 0, 2)),
                  pl.BlockSpec((1, d), lambda i: (0, 0)),
                  _layer_resident(w_glu, layer),
                  _resident(cs.shape, lambda i: (0, 0)),
                  _layer_resident(w_fo, layer),
                  _layer_resident(w_out, layer)],
        out_specs=pl.BlockSpec((tm, d), lambda i: (i, 0)),
        out_shape=jax.ShapeDtypeStruct((n, d), F32),
        compiler_params=_cparams("parallel"),
        name="outproj",
    )(z3, pf, yna, x2d, mod3, g_post.reshape(1, d), w_glu, cs, w_fo, w_out)


def _ffn_kernel(x_ref, sh_ref, sc_ref, gate_ref, gpre_ref, gpost_ref, wg_ref, wu_ref, wd_ref,
                o_ref, a_ref, *, sub, chunk):
    dff = wg_ref.shape[1]
    for r0 in range(0, x_ref.shape[0], sub):
        rs = slice(r0, r0 + sub)
        x = x_ref[rs, :]
        ms = jnp.mean(x * x, axis=-1, keepdims=True)
        m = ((x * lax.rsqrt(ms + EPS) * gpre_ref[...]) * (1.0 + sc_ref[...]) + sh_ref[...]).astype(BF)
        for c0 in range(0, dff, chunk):
            sl = slice(c0, min(c0 + chunk, dff))
            g = jnp.dot(m, wg_ref[:, sl], preferred_element_type=F32)
            u = jnp.dot(m, wu_ref[:, sl], preferred_element_type=F32)
            a_ref[rs, sl] = (g * jax.nn.sigmoid(g) * u).astype(BF)
        y = jnp.dot(a_ref[rs, :], wd_ref[...], preferred_element_type=F32)
        ms2 = jnp.mean(y * y, axis=-1, keepdims=True)
        o_ref[rs, :] = x + gate_ref[...] * (y * lax.rsqrt(ms2 + EPS) * gpost_ref[...])


def _ffn(x2d, mod3, g_pre, g_post, wg, wu, wd, *, layer, tm, tiles_per_batch, mod_row):
    n, d = x2d.shape
    dff = wg.shape[2]
    if mod_row is None:
        row = lambda i: i // tiles_per_batch
    else:
        row = lambda i: mod_row
    return pl.pallas_call(
        functools.partial(_ffn_kernel, sub=min(tm, 512), chunk=3 * MXU_TILE),
        grid=(n // tm,),
        in_specs=[pl.BlockSpec((tm, d), lambda i: (i, 0)),
                  pl.BlockSpec((None, 1, d), lambda i: (row(i), 0, 3)),
                  pl.BlockSpec((None, 1, d), lambda i: (row(i), 0, 4)),
                  pl.BlockSpec((None, 1, d), lambda i: (row(i), 0, 5)),
                  pl.BlockSpec((1, d), lambda i: (0, 0)),
                  pl.BlockSpec((1, d), lambda i: (0, 0)),
                  _layer_resident(wg, layer),
                  _layer_resident(wu, layer),
                  _layer_resident(wd, layer)],
        out_specs=pl.BlockSpec((tm, d), lambda i: (i, 0)),
        out_shape=jax.ShapeDtypeStruct((n, d), F32),
        scratch_shapes=[pltpu.VMEM((tm, dff), BF)],
        compiler_params=_cparams("parallel"),
        name="ffn",
    )(x2d, mod3, mod3, mod3, g_pre.reshape(1, d), g_post.reshape(1, d), wg, wu, wd)


def _layer(x2d, xc2d, cstack, p, w, layer, *, batch, seq, n_ctx, last, rope_tabs, consts):
    d = x2d.shape[1]
    w_ssm = p["ssm_d"].shape[0]
    w_fn = w["w_fourier"].shape[1]
    w_na = (w["w_in"].shape[2] - w_ssm - w_fn) // 3
    widths = (w_ssm, w_fn, w_na)
    heads = w_na // HEAD_DIM
    tm = min(512, seq)
    tmc = min(512, batch * n_ctx)

    mod = _mod_rows(cstack, w["w_mod"], w["b_mod"], layer)
    mod3 = mod.reshape(mod.shape[0], 1, 6 * d)

    cos, sin = rope_tabs
    tm2 = min(2 * tm, seq)
    u3, f, q, k, v = _inproj(x2d, mod3, p["g_pre_mix"], w["w_in"], cos, sin, layer=layer, tm=tm2,
                             tiles_per_batch=seq // tm2, mod_row=None, widths=widths, rope=True)
    ones = jnp.ones((tmc, LANES), F32)
    u3c, fc, qc, kc, vc = _inproj(xc2d, mod3, p["g_pre_mix"], w["w_in"], ones, ones, layer=layer, tm=tmc,
                                  tiles_per_batch=1, mod_row=batch, widths=widths, rope=False)

    z3, zc3 = _ssm(u3, u3c, *w["ssm"], layer=layer, batch=batch, seq=seq, n_ctx=n_ctx)

    l2 = consts["l2"]
    l1 = seq // l2
    g = _dft_a(consts["dft_a"], f.reshape(batch, l2, l1 * w_fn))
    pf = _dft_b(consts["dft_b"], g.reshape(batch, 2, l2, l1, w_fn), kb=min(8, l2))

    yna = _na_latent(q, k, v, kc, vc, p["na_rpb"], batch=batch, seq=seq, n_ctx=n_ctx, rq=consts["rq"])

    w_glu, w_fo, w_out = w["w_glu"], w["w_fourier"], w["w_out"]
    x2d = _outproj(z3, pf, yna, x2d, mod3, p["g_post_mix"], w_glu, consts["cs_lat"], w_fo, w_out,
                   layer=layer, tm=tm2, tiles_per_batch=seq // tm2, mod_row=None, widths=widths)
    wg, wu, wd = w["w_ffn_gate"], w["w_ffn_up"], w["w_ffn_down"]
    x2d = _ffn(x2d, mod3, p["g_pre_ffn"], p["g_post_ffn"], wg, wu, wd,
               layer=layer, tm=tm2, tiles_per_batch=seq // tm2, mod_row=None)

    if not last:
        pfc = _dft_ctx(consts["dft_ctx"], fc.reshape(batch, n_ctx, w_fn))
        ynac = _na_ctx(qc, kc, vc, batch=batch, n_ctx=n_ctx, heads=heads)
        tc = min(tmc, n_ctx)
        xc2d = _outproj(zc3, pfc, ynac, xc2d, mod3, p["g_post_mix"], w_glu, consts["cs_ctx"], w_fo, w_out,
                        layer=layer, tm=tc, tiles_per_batch=n_ctx // tc, mod_row=batch, widths=widths)
        xc2d = _ffn(xc2d, mod3, p["g_pre_ffn"], p["g_post_ffn"], wg, wu, wd,
                    layer=layer, tm=tc, tiles_per_batch=n_ctx // tc, mod_row=batch)
    else:
        xc2d = None
    return x2d, xc2d


def kernel(x, c, ctx, c_ctx, w_mod, b_mod, g_pre_mix, g_post_mix, w_in, ssm_a_re, ssm_a_im, ssm_log_dt,
           ssm_b_re, ssm_b_im, ssm_c_re, ssm_c_im, ssm_d, w_glu, w_fourier, na_rpb, w_out, g_pre_ffn,
           g_post_ffn, w_ffn_gate, w_ffn_up, w_ffn_down):
    batch, seq, d = x.shape
    n_ctx = ctx.shape[1]
    depth = w_mod.shape[0]
    w_fn = w_fourier.shape[1]
    params = dict(g_pre_mix=g_pre_mix, g_post_mix=g_post_mix, ssm_d=ssm_d, na_rpb=na_rpb,
                  g_pre_ffn=g_pre_ffn, g_post_ffn=g_post_ffn)
    seg_len = (seq + 2 * n_ctx) // SSM_T // SSM_SEG
    ssm_ops = jax.vmap(functools.partial(_ssm_weights, seg_len=seg_len))(
        ssm_a_re, ssm_a_im, ssm_log_dt, ssm_b_re, ssm_b_im, ssm_c_re, ssm_c_im, ssm_d)
    weights = dict(w_mod=w_mod, b_mod=b_mod, ssm=ssm_ops,
                   w_in=_cast_bf16(w_in), w_glu=_cast_bf16(w_glu), w_fourier=_cast_bf16(w_fourier),
                   w_out=_cast_bf16(w_out), w_ffn_gate=_cast_bf16(w_ffn_gate),
                   w_ffn_up=_cast_bf16(w_ffn_up), w_ffn_down=_cast_bf16(w_ffn_down))

    nrow = -(-(batch + 1) // 8) * 8
    cstack = jnp.concatenate([c, c_ctx[None, :], jnp.zeros((nrow - batch - 1, d), c.dtype)], axis=0)

    l2 = 64 if seq % (64 * 8) == 0 else 8
    l1 = seq // l2
    consts = dict(
        l2=l2,
        rq=4,
        dft_a=_dft_rows(l2).astype(BF),
        dft_b=_dft_b_mats(seq, l1, l2).astype(BF),
        dft_ctx=_dft_rows(n_ctx).astype(BF),
        cs_lat=_channel_dft(w_fn, seq).astype(BF),
        cs_ctx=_channel_dft(w_fn, n_ctx).astype(BF),
    )
    rope_tabs = _rope_tables(seq)

    x2d = x.reshape(batch * seq, d)
    xc2d = ctx.reshape(batch * n_ctx, d)
    for layer in range(depth):
        p = {name: val[layer] for name, val in params.items()}
        x2d, xc2d = _layer(x2d, xc2d, cstack, p, weights, layer, batch=batch, seq=seq, n_ctx=n_ctx,
                           last=(layer == depth - 1), rope_tabs=rope_tabs, consts=consts)
    return x2d.reshape(batch, seq, d)
```

```python
import functools
import math

import numpy as np
import jax
import jax.numpy as jnp
from jax import lax
from jax.experimental import pallas as pl
from jax.experimental.pallas import tpu as pltpu

BF = jnp.bfloat16
F32 = jnp.float32

EPS = 1e-6
GRID_W = 64
HEAD_DIM = 64
NA_ROWS = 8
NA_COLS = 16
ROPE_BASE = 10000.0
SSM_GROUP = 16
SSM_STATE = 64
FNET_GROUP = 64
LANES = 128
MXU_TILE = 256
SSM_T = 8
SSM_SEG = 8
SSM_STAGES = (4, 2)
NEG = -1e30
VMEM_LIMIT = 56 * 1024 * 1024


def _cparams(*sem):
    return pltpu.CompilerParams(dimension_semantics=sem, vmem_limit_bytes=VMEM_LIMIT)


def _resident(shape, index_map):
    return pl.BlockSpec(shape, index_map, pipeline_mode=pl.Buffered(1))


def _layer_resident(w_all, layer):
    return pl.BlockSpec((None,) + w_all.shape[1:], lambda *_: (layer, 0, 0), pipeline_mode=pl.Buffered(1))


def _cast_kernel(w_ref, o_ref):
    o_ref[...] = w_ref[...].astype(BF)


def _cast_bf16(w_all):
    depth, rows, cols = w_all.shape
    br = rows
    while br * cols * 4 > 4 * 1024 * 1024 and br % 32 == 0:
        br //= 2
    spec = pl.BlockSpec((None, br, cols), lambda l, i: (l, i, 0))
    return pl.pallas_call(
        _cast_kernel,
        grid=(depth, rows // br),
        in_specs=[spec],
        out_specs=spec,
        out_shape=jax.ShapeDtypeStruct(w_all.shape, BF),
        compiler_params=_cparams("parallel", "parallel"),
        name="cast_bf16",
    )(w_all)


def _mod_kernel(c_ref, w_ref, b_ref, o_ref):
    cs = c_ref[...]
    s = (cs * jax.nn.sigmoid(cs)).astype(BF)
    o_ref[...] = jnp.dot(s, w_ref[...].astype(BF), preferred_element_type=F32) + b_ref[...]


def _mod_rows(cstack, w_mod, b_mod):
    rows, d = cstack.shape
    depth, _, n = w_mod.shape
    bn = 2 * d if n % (2 * d) == 0 else d
    return pl.pallas_call(
        _mod_kernel,
        grid=(depth, n // bn),
        in_specs=[pl.BlockSpec((rows, d), lambda l, i: (0, 0)),
                  pl.BlockSpec((None, d, bn), lambda l, i: (l, 0, i)),
                  pl.BlockSpec((None, 1, bn), lambda l, i: (l, 0, i))],
        out_specs=pl.BlockSpec((None, rows, bn), lambda l, i: (l, 0, i)),
        out_shape=jax.ShapeDtypeStruct((depth, rows, n), F32),
        compiler_params=_cparams("parallel", "parallel"),
        name="mod_rows",
    )(cstack, w_mod, b_mod.reshape(depth, 1, n))


def _rope_tile(t, cos, sin, first):
    partner = jnp.where(first, pltpu.roll(t, LANES - 16, 1), pltpu.roll(t, 16, 1))
    return t * cos + partner * sin


def _inproj_kernel(x_ref, sh_ref, sc_ref, g_ref, w_ref, cos_ref, sin_ref,
                   u_ref, f_ref, q_ref, k_ref, v_ref, *, widths, rope, sub):
    w_ssm, w_fn, w_na = widths
    scale = HEAD_DIM ** -0.5 * math.log2(math.e)
    for r0 in range(0, x_ref.shape[0], sub):
        rs = slice(r0, r0 + sub)
        x = x_ref[rs, :]
        ms = jnp.mean(x * x, axis=-1, keepdims=True)
        xn = x * lax.rsqrt(ms + EPS) * g_ref[...]
        m = (xn * (1.0 + sc_ref[...]) + sh_ref[...]).astype(BF)

        h = jnp.dot(m, w_ref[...], preferred_element_type=F32)

        def proj(lo, n, h=h):
            return h[:, lo:lo + n]

        u = proj(0, w_ssm)
        for j in range(w_ssm // LANES):
            u_ref[j, rs, :] = u[:, j * LANES:(j + 1) * LANES]
        f_ref[rs, :] = proj(w_ssm, w_fn).astype(BF)
        q = proj(w_ssm + w_fn, w_na)
        k = proj(w_ssm + w_fn + w_na, w_na)
        v_ref[rs, :] = proj(w_ssm + w_fn + 2 * w_na, w_na).astype(BF)
        if rope:
            cos = cos_ref[rs, :]
            sin = sin_ref[rs, :]
            lane = lax.broadcasted_iota(jnp.int32, cos.shape, 1)
            first = (lane % 32) < 16
            for j in range(w_na // LANES):
                sl = slice(j * LANES, (j + 1) * LANES)
                q_ref[rs, sl] = (_rope_tile(q[:, sl], cos, sin, first) * scale).astype(BF)
                k_ref[rs, sl] = _rope_tile(k[:, sl], cos, sin, first).astype(BF)
        else:
            q_ref[rs, :] = (q * scale).astype(BF)
            k_ref[rs, :] = k.astype(BF)


def _inproj(x2d, mod3, g_pre, w_in, cos, sin, *, layer, tm, tiles_per_batch, mod_row, widths, rope):
    n, d = x2d.shape
    w_ssm, w_fn, w_na = widths
    nt = n // tm
    if mod_row is None:
        row = lambda i: i // tiles_per_batch
    else:
        row = lambda i: mod_row
    pos = lambda i: (i % tiles_per_batch, 0)
    kern = functools.partial(_inproj_kernel, widths=widths, rope=rope, sub=min(tm, 512))
    return pl.pallas_call(
        kern,
        grid=(nt,),
        in_specs=[pl.BlockSpec((tm, d), lambda i: (i, 0)),
                  pl.BlockSpec((None, 1, d), lambda i: (row(i), 0, 0)),
                  pl.BlockSpec((None, 1, d), lambda i: (row(i), 0, 1)),
                  pl.BlockSpec((1, d), lambda i: (0, 0)),
                  _layer_resident(w_in, layer),
                  pl.BlockSpec((tm, LANES), pos),
                  pl.BlockSpec((tm, LANES), pos)],
        out_specs=[pl.BlockSpec((w_ssm // LANES, tm, LANES), lambda i: (0, i, 0)),
                   pl.BlockSpec((tm, w_fn), lambda i: (i, 0)),
                   pl.BlockSpec((tm, w_na), lambda i: (i, 0)),
                   pl.BlockSpec((tm, w_na), lambda i: (i, 0)),
                   pl.BlockSpec((tm, w_na), lambda i: (i, 0))],
        out_shape=[jax.ShapeDtypeStruct((w_ssm // LANES, n, LANES), F32),
                   jax.ShapeDtypeStruct((n, w_fn), BF),
                   jax.ShapeDtypeStruct((n, w_na), BF),
                   jax.ShapeDtypeStruct((n, w_na), BF),
                   jax.ShapeDtypeStruct((n, w_na), BF)],
        compiler_params=_cparams("parallel"),
        name="inproj_rope" if rope else "inproj_ctx",
    )(x2d, mod3, mod3, g_pre.reshape(1, d), w_in, cos, sin)


def _rope_tables(seq):
    t = np.arange(seq)
    row = (t // GRID_W).astype(np.float32)
    col = (t % GRID_W).astype(np.float32)
    quarter = HEAD_DIM // 4
    freqs = (np.float32(ROPE_BASE) ** (-np.arange(quarter, dtype=np.float32) / quarter)).astype(np.float32)
    d = np.arange(LANES) % HEAD_DIM
    use_col = (d // (HEAD_DIM // 2)) == 1
    fidx = d % quarter
    sign = np.where((d % (HEAD_DIM // 2)) // quarter == 0, -1.0, 1.0)
    pos = np.where(use_col[None, :], col[:, None], row[:, None])
    ang = (pos * freqs[fidx][None, :]).astype(np.float32).astype(np.float64)
    return jnp.asarray(np.cos(ang), F32), jnp.asarray(np.sin(ang) * sign[None, :], F32)


def _gelu_tanh(x):
    c = math.sqrt(2.0 / math.pi)
    return x * (0.5 * (1.0 + jnp.tanh(c * (x + 0.044715 * (x * x * x)))))


def _group_of(idx, width, groups):
    shift = width.bit_length() - 1
    assert width == 1 << shift and groups & (groups - 1) == 0
    return lax.bitwise_and(lax.shift_right_logical(idx, shift), groups - 1)


def _same_group(shape, row0, row_width, col0, col_width, groups):
    r = lax.broadcasted_iota(jnp.int32, shape, 0) + row0
    c = lax.broadcasted_iota(jnp.int32, shape, 1) + col0
    return _group_of(r, row_width, groups) == _group_of(c, col_width, groups)


def _swap_stage(tiles, d):
    lane = lax.broadcasted_iota(jnp.int32, tiles[0].shape, 1)
    upper = lax.bitwise_and(lax.shift_right_logical(lane, SSM_GROUP.bit_length() - 1), d) != 0
    out = list(tiles)
    for x in range(len(tiles)):
        if x & d:
            continue
        a, b = tiles[x], tiles[x + d]
        out[x] = jnp.where(upper, pltpu.roll(b, SSM_GROUP * d, 1), a)
        out[x + d] = jnp.where(upper, b, pltpu.roll(a, LANES - SSM_GROUP * d, 1))
    return out


def _ssm_kernel(ul_ref, uc_ref, wc_ref, lag_ref, dc_ref, rexp_ref, cexp_ref, lam_ref, lamseg_ref,
                z_ref, zc_ref, u_s, bm_s, kin_s, cm_s, s_ref, h_ref, e_ref, *, seg_len, n_lat, n_ctx, stages):
    T = SSM_T
    ns = 2 ** len(stages)
    gs = T // ns
    sw = gs * LANES
    rows = u_s.shape[1]
    tps = s_ref.shape[0] // ns
    tpq = tps // 4
    nst = tpq * LANES
    gpt = LANES // SSM_GROUP
    per_q = gpt * SSM_STATE

    def lanes(c):
        return slice(c * LANES, (c + 1) * LANES)

    def col0(s, q):
        return q * per_q + s * nst

    def swapped(tiles, order):
        for d in order:
            tiles = _swap_stage(tiles, d)
        return tiles

    @pl.when(pl.program_id(1) == 0)
    def _expand_operators():
        rexp = rexp_ref[...]
        cexp = cexp_ref[...]
        for s in range(ns):
            wsel = jnp.concatenate([wc_ref[:, col0(s, q):col0(s, q) + nst] for q in range(4)], axis=1)
            blk = jnp.dot(rexp, wsel, preferred_element_type=F32)
            keep = _same_group(blk.shape, 0, SSM_GROUP, 0, SSM_STATE, gs)
            bm_s[s] = jnp.where(keep, blk, 0.0).astype(BF)
            dsel = jnp.concatenate([dc_ref[col0(s, q):col0(s, q) + nst, :] for q in range(4)], axis=0)
            blk = jnp.dot(dsel, cexp, preferred_element_type=F32)
            keep = _same_group(blk.shape, 0, SSM_STATE, 0, SSM_GROUP, gs)
            cm_s[s] = jnp.where(keep, blk, 0.0).astype(BF)
        r16 = lax.broadcasted_iota(jnp.int32, (LANES, SSM_GROUP), 0)
        c16 = lax.broadcasted_iota(jnp.int32, (LANES, SSM_GROUP), 1)
        chan = jnp.where(lax.bitwise_and(r16, SSM_GROUP - 1) == c16, 1.0, 0.0).astype(BF)
        for t in range(T):
            lo = (T - 1 - t) * LANES
            blk = jnp.dot(chan, lag_ref[:, lo:lo + T * LANES], preferred_element_type=F32)
            keep = _same_group(blk.shape, 0, SSM_GROUP, 0, SSM_GROUP, gpt)
            blk = jnp.where(keep, blk, 0.0)
            tiles = swapped([blk[:, lanes(x)] for x in range(T)], stages)
            r0 = (t % gs) * LANES + (t // gs) * gs * SSM_GROUP
            for s in range(ns):
                part = jnp.concatenate(tiles[s * gs:(s + 1) * gs], axis=1)
                kin_s[s, r0:r0 + gs * SSM_GROUP, :] = part[s * gs * SSM_GROUP:(s + 1) * gs * SSM_GROUP, :].astype(BF)

    xs = []
    for t in range(T):
        ct = uc_ref[pl.ds(t, n_ctx, stride=T), :]
        xs.append(jnp.concatenate([ct, ul_ref[pl.ds(t, n_lat, stride=T), :], ct], axis=0))
    xs = swapped(xs, stages)
    for s in range(ns):
        for k in range(gs):
            u_s[s, :, lanes(k)] = xs[s * gs + k].astype(BF)
    for s in range(ns):
        r = jnp.dot(u_s[s], bm_s[s], preferred_element_type=F32)
        for c in range(tps):
            s_ref[s * tps + c] = r[:, lanes(c)]

    chains = []
    for s in range(ns):
        for d in range(2):
            for k in range(tpq):
                cr, ci = s * tps + 2 * d * tpq + k, s * tps + (2 * d + 1) * tpq + k
                lr = slice(col0(s, 2 * d) + k * LANES, col0(s, 2 * d) + (k + 1) * LANES)
                li = slice(col0(s, 2 * d + 1) + k * LANES, col0(s, 2 * d + 1) + (k + 1) * LANES)
                chains.append((cr, ci, d, lr, li))
    nt = ns * tps

    def scan_step(i, carry, record):
        new = list(carry)
        for cr, ci, d, lr_sl, li_sl in chains:
            rsl = pl.ds(i if d == 0 else seg_len - 1 - i, SSM_SEG, stride=seg_len)
            hr, hi = carry[cr], carry[ci]
            if record:
                h_ref[cr, rsl, :] = hr
                h_ref[ci, rsl, :] = hi
            lr, li = lam_ref[:, lr_sl], lam_ref[:, li_sl]
            new[cr] = lr * hr - li * hi + s_ref[cr, rsl, :]
            new[ci] = lr * hi + li * hr + s_ref[ci, rsl, :]
        return tuple(new)

    zero = jnp.zeros((SSM_SEG, LANES), F32)
    fin = lax.fori_loop(0, seg_len, lambda i, c: scan_step(i, c, False), (zero,) * nt, unroll=2)
    for c in range(nt):
        e_ref[c, 0:SSM_SEG, :] = fin[c]

    for cr, ci, d, lr_sl, li_sl in chains:
        lr, li = lamseg_ref[:, lr_sl], lamseg_ref[:, li_sl]
        er = jnp.zeros((1, LANES), F32)
        ei = jnp.zeros((1, LANES), F32)
        for n in range(SSM_SEG):
            g = n if d == 0 else SSM_SEG - 1 - n
            e_ref[cr, SSM_SEG + g:SSM_SEG + g + 1, :] = er
            e_ref[ci, SSM_SEG + g:SSM_SEG + g + 1, :] = ei
            fr, fi = e_ref[cr, g:g + 1, :], e_ref[ci, g:g + 1, :]
            er, ei = lr * er - li * ei + fr, lr * ei + li * er + fi

    ent = tuple(e_ref[c, SSM_SEG:2 * SSM_SEG, :] for c in range(nt))
    lax.fori_loop(0, seg_len, lambda i, c: scan_step(i, c, True), ent, unroll=2)

    ys, ycs = [], []
    for s in range(ns):
        hcat = jnp.concatenate([h_ref[s * tps + c].astype(BF) for c in range(tps)], axis=1)
        y = _gelu_tanh(jnp.dot(u_s[s], kin_s[s], preferred_element_type=F32)
                       + jnp.dot(hcat, cm_s[s], preferred_element_type=F32))
        ys += [y[:, lanes(k)] for k in range(gs)]
        hc = jnp.concatenate(
            [h_ref[s * tps + c, 0:n_ctx, :].astype(BF) for c in range(2 * tpq)]
            + [h_ref[s * tps + c, rows - n_ctx:rows, :].astype(BF) for c in range(2 * tpq, tps)], axis=1)
        yc = _gelu_tanh(jnp.dot(u_s[s, 0:n_ctx, :], kin_s[s], preferred_element_type=F32)
                        + jnp.dot(hc, cm_s[s], preferred_element_type=F32))
        ycs += [yc[:, lanes(k)] for k in range(gs)]
    ys = swapped(ys, stages[::-1])
    ycs = swapped(ycs, stages[::-1])
    for t in range(T):
        z_ref[pl.ds(t, n_lat, stride=T), :] = ys[t][n_ctx:n_ctx + n_lat, :]
        zc_ref[pl.ds(t, n_ctx, stride=T), :] = ycs[t]


def _ssm_weights(a_re, a_im, log_dt, b_re, b_im, c_re, c_im, d_skip, seg_len):
    T = SSM_T
    g, p = a_re.shape[1], a_re.shape[2]
    hc = b_re.shape[-1]
    gpt = LANES // hc
    J = g // gpt
    lam = lax.complex(a_re.astype(F32), a_im.astype(F32))
    dt = jnp.exp(log_dt.astype(F32))[..., None]
    ldt = lam * dt
    lam_bar = jnp.exp(ldt)
    bbar = ((lam_bar - 1) / lam)[..., None] * lax.complex(b_re.astype(F32), b_im.astype(F32))
    cmat = lax.complex(c_re.astype(F32), c_im.astype(F32))
    kk = jnp.arange(T + 1, dtype=F32)
    pw = jnp.exp(ldt[:, None] * kk[None, :, None, None])

    wf = pw[0, T - 1 - jnp.arange(T)][:, :, None, :] * jnp.swapaxes(bbar[0], 1, 2)[None]
    wb = pw[1, jnp.arange(T)][:, :, None, :] * jnp.swapaxes(bbar[1], 1, 2)[None]
    wq = jnp.stack([wf.real, wf.imag, wb.real, wb.imag]).reshape(4, T, J, gpt, hc, p)
    wc = wq.transpose(2, 1, 4, 0, 3, 5).reshape(J, T * hc, 4 * gpt * p)

    df = cmat[0][None] * pw[0, 1 + jnp.arange(T)][:, :, None, :]
    db = cmat[1][None] * pw[1, T - jnp.arange(T)][:, :, None, :]
    dq = jnp.stack([df.real, -df.imag, db.real, -db.imag]).reshape(4, T, J, gpt, hc, p)
    dc = dq.transpose(2, 0, 3, 5, 1, 4).reshape(J, 4 * gpt * p, T * hc)

    mf = jnp.einsum('gcp,kgp,gph->kgch', cmat[0], pw[0, :T], bbar[0]).real
    mb = jnp.einsum('gcp,kgp,gph->kgch', cmat[1], pw[1, :T], bbar[1]).real
    skip = d_skip.astype(F32).reshape(g, hc)[:, :, None] * jnp.eye(hc, dtype=F32)[None]
    m0 = mf[0] + mb[0] + skip
    lags = jnp.concatenate([mb[1:][::-1], m0[None], mf[1:]], axis=0)
    kc = lags.reshape(2 * T - 1, J, gpt, hc, hc).transpose(1, 4, 0, 2, 3).reshape(J, hc, (2 * T - 1) * LANES)

    def lam_rows(power):
        lp = jnp.exp(ldt * power)
        lt = lp.reshape(2, J, gpt * p)
        return jnp.concatenate([lt[0].real, lt[0].imag, lt[1].real, lt[1].imag], axis=-1)[:, None, :]

    lam_t = jnp.broadcast_to(lam_rows(float(T)), (J, SSM_SEG, 4 * gpt * p))
    return wc.astype(BF), kc.astype(BF), dc.astype(BF), lam_t, lam_rows(float(T * seg_len))


def _ssm_expanders(gs):
    r = np.arange(gs * LANES)
    k, slot, ch = r // LANES, (r % LANES) // SSM_GROUP, r % SSM_GROUP
    t = (slot // gs) * gs + k
    rexp = np.zeros((gs * LANES, SSM_T * SSM_GROUP), np.float32)
    rexp[r, t * SSM_GROUP + ch] = 1.0
    return jnp.asarray(rexp, BF), jnp.asarray(rexp.T, BF)


def _ssm(u3, u3c, wc, kc, dc, lam, lamseg, *, layer, batch, seq, n_ctx):
    T = SSM_T
    J = u3.shape[0]
    ns = wc.shape[-1]
    nsets = 2 ** len(SSM_STAGES)
    gs = T // nsets
    sw = gs * LANES
    n_lat, n_c = seq // T, n_ctx // T
    rows = n_lat + 2 * n_c
    seg_len = rows // SSM_SEG
    assert rows % SSM_SEG == 0 and n_c % 16 == 0
    rexp, cexp = _ssm_expanders(gs)
    kern = functools.partial(_ssm_kernel, seg_len=seg_len, n_lat=n_lat, n_ctx=n_c, stages=SSM_STAGES)
    const = lambda j, b: (0, 0)
    per_j = lambda j, b: (layer, j, 0, 0)
    per_jb = lambda j, b: (j, b, 0)
    return pl.pallas_call(
        kern,
        grid=(J, batch),
        in_specs=[pl.BlockSpec((None, seq, LANES), per_jb, pipeline_mode=pl.Buffered(1)),
                  pl.BlockSpec((None, n_ctx, LANES), per_jb),
                  pl.BlockSpec((None, None) + wc.shape[2:], per_j),
                  pl.BlockSpec((None, None) + kc.shape[2:], per_j),
                  pl.BlockSpec((None, None) + dc.shape[2:], per_j),
                  pl.BlockSpec(rexp.shape, const),
                  pl.BlockSpec(cexp.shape, const),
                  pl.BlockSpec((None, None, SSM_SEG, ns), per_j),
                  pl.BlockSpec((None, None, 1, ns), per_j)],
        out_specs=[pl.BlockSpec((None, seq, LANES), per_jb),
                   pl.BlockSpec((None, n_ctx, LANES), per_jb)],
        out_shape=[jax.ShapeDtypeStruct(u3.shape, F32),
                   jax.ShapeDtypeStruct(u3c.shape, F32)],
        scratch_shapes=[pltpu.VMEM((nsets, rows, sw), BF),
                        pltpu.VMEM((nsets, sw, ns // nsets), BF),
                        pltpu.VMEM((nsets, sw, sw), BF),
                        pltpu.VMEM((nsets, ns // nsets, sw), BF),
                        pltpu.VMEM((ns // LANES, rows, LANES), F32),
                        pltpu.VMEM((ns // LANES, rows, LANES), F32),
                        pltpu.VMEM((ns // LANES, 2 * SSM_SEG, LANES), F32)],
        compiler_params=_cparams("arbitrary", "arbitrary"),
        name="ssm_chunk_scan",
    )(u3, u3c, wc, kc, dc, rexp, cexp, lam, lamseg)


def _dft_a_kernel(w_ref, x_ref, o_ref):
    o_ref[...] = jnp.dot(w_ref[...], x_ref[...], preferred_element_type=F32).astype(BF)


def _dft_rows(n):
    k = np.arange(n)
    m = (k[:, None] * k[None, :]) % n
    ang = m * (2.0 * math.pi / n)
    return jnp.asarray(np.concatenate([np.cos(ang), -np.sin(ang)], axis=0), BF)


def _dft_a(w, x3):
    B, n, cols = x3.shape
    tn = min(cols, 4096)
    return pl.pallas_call(
        _dft_a_kernel,
        grid=(B, cols // tn),
        in_specs=[pl.BlockSpec((2 * n, n), lambda b, i: (0, 0)),
                  pl.BlockSpec((None, n, tn), lambda b, i: (b, 0, i))],
        out_specs=pl.BlockSpec((None, 2 * n, tn), lambda b, i: (b, 0, i)),
        out_shape=jax.ShapeDtypeStruct((B, 2 * n, cols), BF),
        compiler_params=_cparams("parallel", "parallel"),
        name="dft_stage_a",
    )(w, x3)


def _store_lane_tiles(o_ref, p, n, wf, idx):
    per = wf // LANES
    for c in range(2 * per):
        part = p[(c // per) * n:(c // per + 1) * n, (c % per) * LANES:(c % per + 1) * LANES]
        o_ref[(c,) + idx] = part


def _dft_b_kernel(a_ref, g_ref, o_ref, *, kb, l1, wf):
    for i in range(kb):
        g = jnp.concatenate([g_ref[0, i], g_ref[1, i]], axis=0)
        p = jnp.dot(a_ref[i], g, preferred_element_type=F32)
        _store_lane_tiles(o_ref, p, l1, wf, (slice(None), i, slice(None)))


def _dft_b_mats(seq, l1, l2):
    k2 = np.arange(l2)[:, None, None]
    k1 = np.arange(l1)[None, :, None]
    j1 = np.arange(l1)[None, None, :]
    m = ((k1 * l2 + k2) * j1) % seq
    ang = m * (2.0 * math.pi / seq)
    ar, ai = np.cos(ang).astype(np.float32), -np.sin(ang).astype(np.float32)
    top = np.concatenate([ar, -ai], axis=2)
    bot = np.concatenate([ai, ar], axis=2)
    return jnp.asarray(np.concatenate([top, bot], axis=1), BF)


def _dft_b(amat, g5, *, kb):
    B, _, l2, l1, wf = g5.shape
    nlt = 2 * wf // LANES
    kern = functools.partial(_dft_b_kernel, kb=kb, l1=l1, wf=wf)
    out = pl.pallas_call(
        kern,
        grid=(B, l2 // kb),
        in_specs=[pl.BlockSpec((kb, 2 * l1, 2 * l1), lambda b, i: (i, 0, 0)),
                  pl.BlockSpec((None, 2, kb, l1, wf), lambda b, i: (b, 0, i, 0, 0))],
        out_specs=pl.BlockSpec((None, nlt, l1, kb, LANES), lambda b, i: (b, 0, 0, i, 0)),
        out_shape=jax.ShapeDtypeStruct((B, nlt, l1, l2, LANES), F32),
        compiler_params=_cparams("parallel", "parallel"),
        name="dft_stage_b",
    )(amat, g5)
    return out.reshape(B, nlt, l1 * l2, LANES)


def _dft_ctx_kernel(w_ref, x_ref, o_ref, *, n, wf):
    p = jnp.dot(w_ref[...], x_ref[...], preferred_element_type=F32)
    _store_lane_tiles(o_ref, p, n, wf, ())


def _dft_ctx(w, x3):
    B, n, wf = x3.shape
    nlt = 2 * wf // LANES
    kern = functools.partial(_dft_ctx_kernel, n=n, wf=wf)
    return pl.pallas_call(
        kern,
        grid=(B,),
        in_specs=[pl.BlockSpec((2 * n, n), lambda b: (0, 0)),
                  pl.BlockSpec((None, n, wf), lambda b: (b, 0, 0))],
        out_specs=pl.BlockSpec((None, nlt, n, LANES), lambda b: (b, 0, 0, 0)),
        out_shape=jax.ShapeDtypeStruct((B, nlt, n, LANES), F32),
        compiler_params=_cparams("parallel"),
        name="dft_ctx",
    )(w, x3)


def _channel_dft(wf, seq):
    c = np.arange(wf)
    same = (c[:, None] // FNET_GROUP) == (c[None, :] // FNET_GROUP)
    m = ((c[:, None] % FNET_GROUP) * (c[None, :] % FNET_GROUP)) % FNET_GROUP
    ang = m * (2.0 * math.pi / FNET_GROUP)
    mask = same / math.sqrt(seq * FNET_GROUP)
    return jnp.asarray(np.concatenate([np.cos(ang) * mask, np.sin(ang) * mask], axis=0), BF)


def _softmax_pv(parts, l_shape):
    m = parts[0][0].max(axis=-1, keepdims=True)
    for s, _ in parts[1:]:
        m = jnp.maximum(m, s.max(axis=-1, keepdims=True))
    l = jnp.zeros(l_shape, F32)
    o = None
    for s, v in parts:
        p = jnp.exp2(s - m)
        l = l + p.sum(axis=-1, keepdims=True)
        pv = jnp.dot(p.astype(BF), v, preferred_element_type=F32)
        o = pv if o is None else o + pv
    return o / l


def _na_kernel(kb_ref, pid_ref, dr_ref, q_ref, k_ref, v_ref, kc_ref, vc_ref, cbx_ref, o_ref, bias_ref,
               *, nkeys, heads, nsub):
    step = pl.program_id(1)
    nq = q_ref.shape[0] // nsub
    rq, kr = nq // GRID_W, nkeys // GRID_W
    lane = lax.broadcasted_iota(jnp.int32, (nq, LANES), 1)
    low = lane < HEAD_DIM
    dn = (((1,), (1,)), ((), ()))

    for sb in range(nsub):
        i = step * nsub + sb
        pid = pid_ref[i]

        @pl.when((step == 0) | (pid != pid_ref[jnp.maximum(i - nsub, 0)]))
        def _build_bias(sb=sb, pid=pid):
            for qi in range(rq):
                for w in range(kr):
                    a = dr_ref[(pid * rq + qi) * kr + w]
                    for h in range(heads):
                        bias_ref[sb, h, qi * GRID_W:(qi + 1) * GRID_W, w * GRID_W:(w + 1) * GRID_W] = cbx_ref[h, a]

    tasks = [(sb, h) for sb in range(nsub) for h in range(heads)]
    starts = [pl.multiple_of(kb_ref[step * nsub + sb] * GRID_W, GRID_W) for sb in range(nsub)]

    def scores(sb, h):
        sl = slice((h // 2) * LANES, (h // 2 + 1) * LANES)
        q2 = q_ref[sb * nq:(sb + 1) * nq, sl]
        qm = jnp.where(low if h % 2 == 0 else jnp.logical_not(low), q2, jnp.zeros_like(q2))
        s_w = lax.dot_general(qm, k_ref[pl.ds(starts[sb], nkeys), sl], dn, preferred_element_type=F32)
        s_c = lax.dot_general(qm, kc_ref[:, sl], dn, preferred_element_type=F32)
        return s_w + bias_ref[sb, h], s_c

    def weights(s_w, s_c):
        m = jnp.maximum(s_w.max(axis=-1, keepdims=True), s_c.max(axis=-1, keepdims=True))
        p_w = jnp.exp2(s_w - m)
        p_c = jnp.exp2(s_c - m)
        l = p_w.sum(axis=-1, keepdims=True) + p_c.sum(axis=-1, keepdims=True)
        return p_w.astype(BF), p_c.astype(BF), l

    outs = {}

    def finish(sb, h, p_w, p_c, l):
        sl = slice((h // 2) * LANES, (h // 2 + 1) * LANES)
        o = (jnp.dot(p_w, v_ref[pl.ds(starts[sb], nkeys), sl], preferred_element_type=F32)
             + jnp.dot(p_c, vc_ref[:, sl], preferred_element_type=F32))
        outs[h % 2] = o / l
        if h % 2 == 1:
            o_ref[sb * nq:(sb + 1) * nq, sl] = jnp.where(low, outs[0], outs[1]).astype(BF)

    nt = len(tasks)
    sc = {0: scores(*tasks[0])}
    if nt > 1:
        sc[1] = scores(*tasks[1])
    pw = {0: weights(*sc.pop(0))}
    for n in range(nt):
        if n + 2 < nt:
            sc[n + 2] = scores(*tasks[n + 2])
        if n + 1 < nt:
            pw[n + 1] = weights(*sc.pop(n + 1))
        finish(*tasks[n], *pw.pop(n))


def _na_plan(rows, rq):
    wr = min(NA_ROWS, rows)
    kr = rq + wr - 1
    nblk = rows // rq
    kbs, pids, pats = [], [], []
    for blk in range(nblk):
        r0 = blk * rq
        rs = [int(np.clip(r0 + i - wr // 2, 0, rows - wr)) for i in range(rq)]
        kb = min(rs[0], rows - kr)
        pat = (tuple(r - kb for r in rs), r0 - kb)
        if pat not in pats:
            pats.append(pat)
        kbs.append(kb)
        pids.append(pats.index(pat))
    return np.asarray(kbs, np.int32), np.asarray(pids, np.int32), pats, kr, wr


def _na_bias(rpb, pats, rq, kr, wr):
    cols = np.arange(GRID_W)
    cstart = np.clip(cols - NA_COLS // 2, 0, GRID_W - NA_COLS)
    kc = np.arange(GRID_W)
    col_ok = (kc[None, :] >= cstart[:, None]) & (kc[None, :] < cstart[:, None] + NA_COLS)
    dc = kc[None, :] - cols[:, None] + NA_COLS - 1
    heads, nr, ncol = rpb.shape
    sel = (dc[:, :, None] == np.arange(ncol)[None, None, :]) & col_ok[:, :, None]
    cb = jnp.einsum('hab,ckb->hack', rpb.astype(F32), jnp.asarray(sel, F32), precision=lax.Precision.HIGHEST)
    cb = jnp.where(jnp.asarray(col_ok)[None, None], cb * math.log2(math.e), NEG)
    cbx = jnp.concatenate([cb, jnp.full((heads, 1, GRID_W, GRID_W), NEG, F32)], axis=1)
    slots = []
    for rs_off, r_off in pats:
        i = np.arange(rq)[:, None]
        w = np.arange(kr)[None, :]
        rso = np.asarray(rs_off)[:, None]
        row_ok = (w >= rso) & (w < rso + wr)
        slots.append(np.where(row_ok, w - r_off - i + NA_ROWS - 1, nr))
    return cbx, np.stack(slots).reshape(-1).astype(np.int32)


def _na_latent(q, k, v, kc, vc, rpb, *, batch, seq, n_ctx, rq):
    rows = seq // GRID_W
    heads = rpb.shape[0]
    kbs, pids, pats, kr, wr = _na_plan(rows, rq)
    cbx, slots = _na_bias(rpb, pats, rq, kr, wr)
    nblk = rows // rq
    nq = rq * GRID_W
    nkeys = kr * GRID_W
    w = q.shape[1]
    nsub = 2 if nblk % 2 == 0 else 1
    nstep = nblk // nsub
    kern = functools.partial(_na_kernel, nkeys=nkeys, heads=heads, nsub=nsub)
    grid_spec = pltpu.PrefetchScalarGridSpec(
        num_scalar_prefetch=3,
        grid=(batch, nstep),
        in_specs=[pl.BlockSpec((nsub * nq, w), lambda b, i, *_: (b * nstep + i, 0)),
                  pl.BlockSpec((seq, w), lambda b, i, *_: (b, 0)),
                  pl.BlockSpec((seq, w), lambda b, i, *_: (b, 0)),
                  pl.BlockSpec((n_ctx, w), lambda b, i, *_: (b, 0)),
                  pl.BlockSpec((n_ctx, w), lambda b, i, *_: (b, 0)),
                  pl.BlockSpec(cbx.shape, lambda b, i, *_: (0, 0, 0, 0))],
        out_specs=pl.BlockSpec((nsub * nq, w), lambda b, i, *_: (b * nstep + i, 0)),
        scratch_shapes=[pltpu.VMEM((nsub, heads, nq, nkeys), F32)],
    )
    return pl.pallas_call(
        kern,
        grid_spec=grid_spec,
        out_shape=jax.ShapeDtypeStruct(q.shape, BF),
        compiler_params=_cparams("arbitrary", "arbitrary"),
        name="na_latent",
    )(jnp.asarray(kbs), jnp.asarray(pids), jnp.asarray(slots), q, k, v, kc, vc, cbx)


def _na_ctx_kernel(q_ref, k_ref, v_ref, o_ref, *, heads):
    nq = q_ref.shape[0]
    lane = lax.broadcasted_iota(jnp.int32, (nq, LANES), 1)
    low = lane < HEAD_DIM
    dn = (((1,), (1,)), ((), ()))
    for hp in range(heads // 2):
        sl = slice(hp * LANES, (hp + 1) * LANES)
        q2 = q_ref[:, sl]
        k2 = k_ref[:, sl]
        v2 = v_ref[:, sl]
        outs = []
        for hh in range(2):
            qm = jnp.where(low if hh == 0 else jnp.logical_not(low), q2, jnp.zeros_like(q2))
            s = lax.dot_general(qm, k2, dn, preferred_element_type=F32)
            outs.append(_softmax_pv([(s, v2)], (nq, 1)))
        o_ref[:, sl] = jnp.where(low, outs[0], outs[1]).astype(BF)


def _na_ctx(qc, kc, vc, *, batch, n_ctx, heads):
    w = qc.shape[1]
    spec = pl.BlockSpec((n_ctx, w), lambda b: (b, 0))
    return pl.pallas_call(
        functools.partial(_na_ctx_kernel, heads=heads),
        grid=(batch,),
        in_specs=[spec, spec, spec],
        out_specs=spec,
        out_shape=jax.ShapeDtypeStruct(qc.shape, BF),
        compiler_params=_cparams("parallel"),
        name="na_ctx",
    )(qc, kc, vc)


def _outproj_kernel(z_ref, pf_ref, na_ref, x_ref, gate_ref, gpost_ref,
                    wglu_ref, cs_ref, wfo_ref, wout_ref, o_ref, *, widths, sub):
    w_ssm, w_fn, w_na = widths
    for r0 in range(0, x_ref.shape[0], sub):
        rs = slice(r0, r0 + sub)
        z = jnp.concatenate([z_ref[j, rs, :] for j in range(w_ssm // LANES)], axis=1)
        gl = jnp.dot(z.astype(BF), wglu_ref[...], preferred_element_type=F32)
        y_ssm = (z * jax.nn.sigmoid(gl)).astype(BF)
        pf = jnp.concatenate([pf_ref[c, rs, :] for c in range(pf_ref.shape[0])], axis=1).astype(BF)
        mixed = jnp.dot(pf, cs_ref[...], preferred_element_type=F32).astype(BF)
        y_fft = jnp.dot(mixed, wfo_ref[...], preferred_element_type=F32).astype(BF)
        y = jnp.concatenate([y_ssm, y_fft, na_ref[rs, :]], axis=1)
        o = jnp.dot(y, wout_ref[...], preferred_element_type=F32)
        ms = jnp.mean(o * o, axis=-1, keepdims=True)
        o_ref[rs, :] = x_ref[rs, :] + gate_ref[...] * (o * lax.rsqrt(ms + EPS) * gpost_ref[...])


def _outproj(z3, pf, yna, x2d, mod3, g_post, w_glu, cs, w_fo, w_out, *,
             layer, tm, tiles_per_batch, mod_row, widths):
    n, d = x2d.shape
    w_ssm, w_fn, w_na = widths
    if mod_row is None:
        row = lambda i: i // tiles_per_batch
    else:
        row = lambda i: mod_row
    kern = functools.partial(_outproj_kernel, widths=widths, sub=min(tm, 512))
    return pl.pallas_call(
        kern,
        grid=(n // tm,),
        in_specs=[pl.BlockSpec((w_ssm // LANES, tm, LANES), lambda i: (0, i, 0)),
                  pl.BlockSpec((None, pf.shape[1], tm, LANES),
                               lambda i: (i // tiles_per_batch, 0, i % tiles_per_batch, 0)),
                  pl.BlockSpec((tm, w_na), lambda i: (i, 0)),
                  pl.BlockSpec((tm, d), lambda i: (i, 0)),
                  pl.BlockSpec((None, 1, d), lambda i: (row(i), 0, 2)),
                  pl.BlockSpec((1, d), lambda i: (0, 0)),
                  _layer_resident(w_glu, layer),
                  _resident(cs.shape, lambda i: (0, 0)),
                  _layer_resident(w_fo, layer),
                  _layer_resident(w_out, layer)],
        out_specs=pl.BlockSpec((tm, d), lambda i: (i, 0)),
        out_shape=jax.ShapeDtypeStruct((n, d), F32),
        compiler_params=_cparams("parallel"),
        name="outproj",
    )(z3, pf, yna, x2d, mod3, g_post.reshape(1, d), w_glu, cs, w_fo, w_out)


def _ffn_kernel(x_ref, sh_ref, sc_ref, gate_ref, gpre_ref, gpost_ref, wg_ref, wu_ref, wd_ref,
                o_ref, a_ref, *, sub, chunk):
    dff = wg_ref.shape[1]
    for r0 in range(0, x_ref.shape[0], sub):
        rs = slice(r0, r0 + sub)
        x = x_ref[rs, :]
        ms = jnp.mean(x * x, axis=-1, keepdims=True)
        m = ((x * lax.rsqrt(ms + EPS) * gpre_ref[...]) * (1.0 + sc_ref[...]) + sh_ref[...]).astype(BF)
        for c0 in range(0, dff, chunk):
            sl = slice(c0, min(c0 + chunk, dff))
            g = jnp.dot(m, wg_ref[:, sl], preferred_element_type=F32)
            u = jnp.dot(m, wu_ref[:, sl], preferred_element_type=F32)
            a_ref[rs, sl] = (g * jax.nn.sigmoid(g) * u).astype(BF)
        y = jnp.dot(a_ref[rs, :], wd_ref[...], preferred_element_type=F32)
        ms2 = jnp.mean(y * y, axis=-1, keepdims=True)
        o_ref[rs, :] = x + gate_ref[...] * (y * lax.rsqrt(ms2 + EPS) * gpost_ref[...])


def _ffn(x2d, mod3, g_pre, g_post, wg, wu, wd, *, layer, tm, tiles_per_batch, mod_row):
    n, d = x2d.shape
    dff = wg.shape[2]
    if mod_row is None:
        row = lambda i: i // tiles_per_batch
    else:
        row = lambda i: mod_row
    return pl.pallas_call(
        functools.partial(_ffn_kernel, sub=min(tm, 512), chunk=3 * MXU_TILE),
        grid=(n // tm,),
        in_specs=[pl.BlockSpec((tm, d), lambda i: (i, 0)),
                  pl.BlockSpec((None, 1, d), lambda i: (row(i), 0, 3)),
                  pl.BlockSpec((None, 1, d), lambda i: (row(i), 0, 4)),
                  pl.BlockSpec((None, 1, d), lambda i: (row(i), 0, 5)),
                  pl.BlockSpec((1, d), lambda i: (0, 0)),
                  pl.BlockSpec((1, d), lambda i: (0, 0)),
                  _layer_resident(wg, layer),
                  _layer_resident(wu, layer),
                  _layer_resident(wd, layer)],
        out_specs=pl.BlockSpec((tm, d), lambda i: (i, 0)),
        out_shape=jax.ShapeDtypeStruct((n, d), F32),
        scratch_shapes=[pltpu.VMEM((tm, dff), BF)],
        compiler_params=_cparams("parallel"),
        name="ffn",
    )(x2d, mod3, mod3, mod3, g_pre.reshape(1, d), g_post.reshape(1, d), wg, wu, wd)


def _layer(x2d, xc2d, cstack, p, w, layer, *, batch, seq, n_ctx, last, rope_tabs, consts):
    d = x2d.shape[1]
    w_ssm = p["ssm_d"].shape[0]
    w_fn = w["w_fourier"].shape[1]
    w_na = (w["w_in"].shape[2] - w_ssm - w_fn) // 3
    widths = (w_ssm, w_fn, w_na)
    heads = w_na // HEAD_DIM
    tm = min(512, seq)
    tmc = min(512, batch * n_ctx)

    mod = w["mod"][layer]
    mod3 = mod.reshape(mod.shape[0], 1, 6 * d)

    cos, sin = rope_tabs
    tm2 = min(2 * tm, seq)
    u3, f, q, k, v = _inproj(x2d, mod3, p["g_pre_mix"], w["w_in"], cos, sin, layer=layer, tm=tm2,
                             tiles_per_batch=seq // tm2, mod_row=None, widths=widths, rope=True)
    ones = jnp.ones((tmc, LANES), F32)
    u3c, fc, qc, kc, vc = _inproj(xc2d, mod3, p["g_pre_mix"], w["w_in"], ones, ones, layer=layer, tm=tmc,
                                  tiles_per_batch=1, mod_row=batch, widths=widths, rope=False)

    z3, zc3 = _ssm(u3, u3c, *w["ssm"], layer=layer, batch=batch, seq=seq, n_ctx=n_ctx)

    l2 = consts["l2"]
    l1 = seq // l2
    g = _dft_a(consts["dft_a"], f.reshape(batch, l2, l1 * w_fn))
    pf = _dft_b(consts["dft_b"], g.reshape(batch, 2, l2, l1, w_fn), kb=min(8, l2))

    yna = _na_latent(q, k, v, kc, vc, p["na_rpb"], batch=batch, seq=seq, n_ctx=n_ctx, rq=consts["rq"])

    w_glu, w_fo, w_out = w["w_glu"], w["w_fourier"], w["w_out"]
    x2d = _outproj(z3, pf, yna, x2d, mod3, p["g_post_mix"], w_glu, consts["cs_lat"], w_fo, w_out,
                   layer=layer, tm=tm2, tiles_per_batch=seq // tm2, mod_row=None, widths=widths)
    wg, wu, wd = w["w_ffn_gate"], w["w_ffn_up"], w["w_ffn_down"]
    x2d = _ffn(x2d, mod3, p["g_pre_ffn"], p["g_post_ffn"], wg, wu, wd,
               layer=layer, tm=tm2, tiles_per_batch=seq // tm2, mod_row=None)

    if not last:
        pfc = _dft_ctx(consts["dft_ctx"], fc.reshape(batch, n_ctx, w_fn))
        ynac = _na_ctx(qc, kc, vc, batch=batch, n_ctx=n_ctx, heads=heads)
        tc = min(tmc, n_ctx)
        xc2d = _outproj(zc3, pfc, ynac, xc2d, mod3, p["g_post_mix"], w_glu, consts["cs_ctx"], w_fo, w_out,
                        layer=layer, tm=tc, tiles_per_batch=n_ctx // tc, mod_row=batch, widths=widths)
        xc2d = _ffn(xc2d, mod3, p["g_pre_ffn"], p["g_post_ffn"], wg, wu, wd,
                    layer=layer, tm=tc, tiles_per_batch=n_ctx // tc, mod_row=batch)
    else:
        xc2d = None
    return x2d, xc2d


def kernel(x, c, ctx, c_ctx, w_mod, b_mod, g_pre_mix, g_post_mix, w_in, ssm_a_re, ssm_a_im, ssm_log_dt,
           ssm_b_re, ssm_b_im, ssm_c_re, ssm_c_im, ssm_d, w_glu, w_fourier, na_rpb, w_out, g_pre_ffn,
           g_post_ffn, w_ffn_gate, w_ffn_up, w_ffn_down):
    batch, seq, d = x.shape
    n_ctx = ctx.shape[1]
    depth = w_mod.shape[0]
    w_fn = w_fourier.shape[1]
    params = dict(g_pre_mix=g_pre_mix, g_post_mix=g_post_mix, ssm_d=ssm_d, na_rpb=na_rpb,
                  g_pre_ffn=g_pre_ffn, g_post_ffn=g_post_ffn)
    seg_len = (seq + 2 * n_ctx) // SSM_T // SSM_SEG
    ssm_ops = jax.vmap(functools.partial(_ssm_weights, seg_len=seg_len))(
        ssm_a_re, ssm_a_im, ssm_log_dt, ssm_b_re, ssm_b_im, ssm_c_re, ssm_c_im, ssm_d)
    nrow = -(-(batch + 1) // 8) * 8
    cstack = jnp.concatenate([c, c_ctx[None, :], jnp.zeros((nrow - batch - 1, d), c.dtype)], axis=0)
    weights = dict(mod=_mod_rows(cstack, w_mod, b_mod), ssm=ssm_ops,
                   w_in=_cast_bf16(w_in), w_glu=_cast_bf16(w_glu), w_fourier=_cast_bf16(w_fourier),
                   w_out=_cast_bf16(w_out), w_ffn_gate=_cast_bf16(w_ffn_gate),
                   w_ffn_up=_cast_bf16(w_ffn_up), w_ffn_down=_cast_bf16(w_ffn_down))

    l2 = 64 if seq % (64 * 8) == 0 else 8
    l1 = seq // l2
    consts = dict(
        l2=l2,
        rq=4,
        dft_a=_dft_rows(l2).astype(BF),
        dft_b=_dft_b_mats(seq, l1, l2).astype(BF),
        dft_ctx=_dft_rows(n_ctx).astype(BF),
        cs_lat=_channel_dft(w_fn, seq).astype(BF),
        cs_ctx=_channel_dft(w_fn, n_ctx).astype(BF),
    )
    rope_tabs = _rope_tables(seq)

    x2d = x.reshape(batch * seq, d)
    xc2d = ctx.reshape(batch * n_ctx, d)
    for layer in range(depth):
        p = {name: val[layer] for name, val in params.items()}
        x2d, xc2d = _layer(x2d, xc2d, cstack, p, weights, layer, batch=batch, seq=seq, n_ctx=n_ctx,
                           last=(layer == depth - 1), rope_tabs=rope_tabs, consts=consts)
    return x2d.reshape(batch, seq, d)
```

```python
import functools
import math

import numpy as np
import jax
import jax.numpy as jnp
from jax import lax
from jax.experimental import pallas as pl
from jax.experimental.pallas import tpu as pltpu

BF = jnp.bfloat16
F32 = jnp.float32

EPS = 1e-6
GRID_W = 64
HEAD_DIM = 64
NA_ROWS = 8
NA_COLS = 16
ROPE_BASE = 10000.0
SSM_GROUP = 16
SSM_STATE = 64
FNET_GROUP = 64
LANES = 128
MXU_TILE = 256
SSM_T = 8
SSM_SEG = 8
SSM_STAGES = (4, 2)
NEG = -1e30
VMEM_LIMIT = 56 * 1024 * 1024


def _cparams(*sem):
    return pltpu.CompilerParams(dimension_semantics=sem, vmem_limit_bytes=VMEM_LIMIT)


def _resident(shape, index_map):
    return pl.BlockSpec(shape, index_map, pipeline_mode=pl.Buffered(1))


def _layer_resident(w_all, layer):
    return pl.BlockSpec((None,) + w_all.shape[1:], lambda *_: (layer, 0, 0), pipeline_mode=pl.Buffered(1))


def _cast_kernel(w_ref, o_ref):
    o_ref[...] = w_ref[...].astype(BF)


def _cast_bf16(w_all):
    depth, rows, cols = w_all.shape
    br = rows
    while br * cols * 4 > 4 * 1024 * 1024 and br % 32 == 0:
        br //= 2
    spec = pl.BlockSpec((None, br, cols), lambda l, i: (l, i, 0))
    return pl.pallas_call(
        _cast_kernel,
        grid=(depth, rows // br),
        in_specs=[spec],
        out_specs=spec,
        out_shape=jax.ShapeDtypeStruct(w_all.shape, BF),
        compiler_params=_cparams("parallel", "parallel"),
        name="cast_bf16",
    )(w_all)


def _mod_kernel(c_ref, w_ref, b_ref, o_ref):
    cs = c_ref[...]
    s = (cs * jax.nn.sigmoid(cs)).astype(BF)
    o_ref[...] = jnp.dot(s, w_ref[...].astype(BF), preferred_element_type=F32) + b_ref[...]


def _mod_rows(cstack, w_mod, b_mod):
    rows, d = cstack.shape
    depth, _, n = w_mod.shape
    bn = 2 * d if n % (2 * d) == 0 else d
    return pl.pallas_call(
        _mod_kernel,
        grid=(depth, n // bn),
        in_specs=[pl.BlockSpec((rows, d), lambda l, i: (0, 0)),
                  pl.BlockSpec((None, d, bn), lambda l, i: (l, 0, i)),
                  pl.BlockSpec((None, 1, bn), lambda l, i: (l, 0, i))],
        out_specs=pl.BlockSpec((None, rows, bn), lambda l, i: (l, 0, i)),
        out_shape=jax.ShapeDtypeStruct((depth, rows, n), F32),
        compiler_params=_cparams("parallel", "parallel"),
        name="mod_rows",
    )(cstack, w_mod, b_mod.reshape(depth, 1, n))


def _rope_tile(t, cos, sin, first):
    partner = jnp.where(first, pltpu.roll(t, LANES - 16, 1), pltpu.roll(t, 16, 1))
    return t * cos + partner * sin


def _inproj_kernel(x_ref, sh_ref, sc_ref, g_ref, w_ref, cos_ref, sin_ref,
                   u_ref, f_ref, q_ref, k_ref, v_ref, *, widths, rope, sub):
    w_ssm, w_fn, w_na = widths
    scale = HEAD_DIM ** -0.5 * math.log2(math.e)
    for r0 in range(0, x_ref.shape[0], sub):
        rs = slice(r0, r0 + sub)
        x = x_ref[rs, :]
        ms = jnp.mean(x * x, axis=-1, keepdims=True)
        xn = x * lax.rsqrt(ms + EPS) * g_ref[...]
        m = (xn * (1.0 + sc_ref[...]) + sh_ref[...]).astype(BF)

        h = jnp.dot(m, w_ref[...], preferred_element_type=F32)

        def proj(lo, n, h=h):
            return h[:, lo:lo + n]

        u = proj(0, w_ssm)
        for j in range(w_ssm // LANES):
            u_ref[j, rs, :] = u[:, j * LANES:(j + 1) * LANES]
        f_ref[rs, :] = proj(w_ssm, w_fn).astype(BF)
        q = proj(w_ssm + w_fn, w_na)
        k = proj(w_ssm + w_fn + w_na, w_na)
        v_ref[rs, :] = proj(w_ssm + w_fn + 2 * w_na, w_na).astype(BF)
        if rope:
            cos = cos_ref[rs, :]
            sin = sin_ref[rs, :]
            lane = lax.broadcasted_iota(jnp.int32, cos.shape, 1)
            first = (lane % 32) < 16
            for j in range(w_na // LANES):
                sl = slice(j * LANES, (j + 1) * LANES)
                q_ref[rs, sl] = (_rope_tile(q[:, sl], cos, sin, first) * scale).astype(BF)
                k_ref[rs, sl] = _rope_tile(k[:, sl], cos, sin, first).astype(BF)
        else:
            q_ref[rs, :] = (q * scale).astype(BF)
            k_ref[rs, :] = k.astype(BF)


def _inproj(x2d, mod3, g_pre, w_in, cos, sin, *, layer, tm, tiles_per_batch, mod_row, widths, rope):
    n, d = x2d.shape
    w_ssm, w_fn, w_na = widths
    nt = n // tm
    if mod_row is None:
        row = lambda i: i // tiles_per_batch
    else:
        row = lambda i: mod_row
    pos = lambda i: (i % tiles_per_batch, 0)
    kern = functools.partial(_inproj_kernel, widths=widths, rope=rope, sub=min(tm, 512))
    return pl.pallas_call(
        kern,
        grid=(nt,),
        in_specs=[pl.BlockSpec((tm, d), lambda i: (i, 0)),
                  pl.BlockSpec((None, 1, d), lambda i: (row(i), 0, 0)),
                  pl.BlockSpec((None, 1, d), lambda i: (row(i), 0, 1)),
                  pl.BlockSpec((None, 1, d), lambda i: (layer, 0, 0)),
                  _layer_resident(w_in, layer),
                  pl.BlockSpec((tm, LANES), pos),
                  pl.BlockSpec((tm, LANES), pos)],
        out_specs=[pl.BlockSpec((w_ssm // LANES, tm, LANES), lambda i: (0, i, 0)),
                   pl.BlockSpec((tm, w_fn), lambda i: (i, 0)),
                   pl.BlockSpec((tm, w_na), lambda i: (i, 0)),
                   pl.BlockSpec((tm, w_na), lambda i: (i, 0)),
                   pl.BlockSpec((tm, w_na), lambda i: (i, 0))],
        out_shape=[jax.ShapeDtypeStruct((w_ssm // LANES, n, LANES), F32),
                   jax.ShapeDtypeStruct((n, w_fn), BF),
                   jax.ShapeDtypeStruct((n, w_na), BF),
                   jax.ShapeDtypeStruct((n, w_na), BF),
                   jax.ShapeDtypeStruct((n, w_na), BF)],
        compiler_params=_cparams("parallel"),
        name="inproj_rope" if rope else "inproj_ctx",
    )(x2d, mod3, mod3, g_pre.reshape(-1, 1, d), w_in, cos, sin)


def _rope_tables(seq):
    t = np.arange(seq)
    row = (t // GRID_W).astype(np.float32)
    col = (t % GRID_W).astype(np.float32)
    quarter = HEAD_DIM // 4
    freqs = (np.float32(ROPE_BASE) ** (-np.arange(quarter, dtype=np.float32) / quarter)).astype(np.float32)
    d = np.arange(LANES) % HEAD_DIM
    use_col = (d // (HEAD_DIM // 2)) == 1
    fidx = d % quarter
    sign = np.where((d % (HEAD_DIM // 2)) // quarter == 0, -1.0, 1.0)
    pos = np.where(use_col[None, :], col[:, None], row[:, None])
    ang = (pos * freqs[fidx][None, :]).astype(np.float32).astype(np.float64)
    return jnp.asarray(np.cos(ang), F32), jnp.asarray(np.sin(ang) * sign[None, :], F32)


def _gelu_tanh(x):
    c = math.sqrt(2.0 / math.pi)
    return x * (0.5 * (1.0 + jnp.tanh(c * (x + 0.044715 * (x * x * x)))))


def _group_of(idx, width, groups):
    shift = width.bit_length() - 1
    assert width == 1 << shift and groups & (groups - 1) == 0
    return lax.bitwise_and(lax.shift_right_logical(idx, shift), groups - 1)


def _same_group(shape, row0, row_width, col0, col_width, groups):
    r = lax.broadcasted_iota(jnp.int32, shape, 0) + row0
    c = lax.broadcasted_iota(jnp.int32, shape, 1) + col0
    return _group_of(r, row_width, groups) == _group_of(c, col_width, groups)


def _swap_stage(tiles, d):
    lane = lax.broadcasted_iota(jnp.int32, tiles[0].shape, 1)
    upper = lax.bitwise_and(lax.shift_right_logical(lane, SSM_GROUP.bit_length() - 1), d) != 0
    out = list(tiles)
    for x in range(len(tiles)):
        if x & d:
            continue
        a, b = tiles[x], tiles[x + d]
        out[x] = jnp.where(upper, pltpu.roll(b, SSM_GROUP * d, 1), a)
        out[x + d] = jnp.where(upper, b, pltpu.roll(a, LANES - SSM_GROUP * d, 1))
    return out


def _ssm_kernel(ul_ref, uc_ref, wc_ref, lag_ref, dc_ref, rexp_ref, cexp_ref, lam_ref, lamseg_ref,
                z_ref, zc_ref, u_s, bm_s, kin_s, cm_s, s_ref, h_ref, e_ref, *, seg_len, n_lat, n_ctx, stages):
    T = SSM_T
    ns = 2 ** len(stages)
    gs = T // ns
    sw = gs * LANES
    rows = u_s.shape[1]
    tps = s_ref.shape[0] // ns
    tpq = tps // 4
    nst = tpq * LANES
    gpt = LANES // SSM_GROUP
    per_q = gpt * SSM_STATE

    def lanes(c):
        return slice(c * LANES, (c + 1) * LANES)

    def col0(s, q):
        return q * per_q + s * nst

    def swapped(tiles, order):
        for d in order:
            tiles = _swap_stage(tiles, d)
        return tiles

    @pl.when(pl.program_id(1) == 0)
    def _expand_operators():
        rexp = rexp_ref[...]
        cexp = cexp_ref[...]
        for s in range(ns):
            wsel = jnp.concatenate([wc_ref[:, col0(s, q):col0(s, q) + nst] for q in range(4)], axis=1)
            blk = jnp.dot(rexp, wsel, preferred_element_type=F32)
            keep = _same_group(blk.shape, 0, SSM_GROUP, 0, SSM_STATE, gs)
            bm_s[s] = jnp.where(keep, blk, 0.0).astype(BF)
            dsel = jnp.concatenate([dc_ref[col0(s, q):col0(s, q) + nst, :] for q in range(4)], axis=0)
            blk = jnp.dot(dsel, cexp, preferred_element_type=F32)
            keep = _same_group(blk.shape, 0, SSM_STATE, 0, SSM_GROUP, gs)
            cm_s[s] = jnp.where(keep, blk, 0.0).astype(BF)
        r16 = lax.broadcasted_iota(jnp.int32, (LANES, SSM_GROUP), 0)
        c16 = lax.broadcasted_iota(jnp.int32, (LANES, SSM_GROUP), 1)
        chan = jnp.where(lax.bitwise_and(r16, SSM_GROUP - 1) == c16, 1.0, 0.0).astype(BF)
        for t in range(T):
            lo = (T - 1 - t) * LANES
            blk = jnp.dot(chan, lag_ref[:, lo:lo + T * LANES], preferred_element_type=F32)
            keep = _same_group(blk.shape, 0, SSM_GROUP, 0, SSM_GROUP, gpt)
            blk = jnp.where(keep, blk, 0.0)
            tiles = swapped([blk[:, lanes(x)] for x in range(T)], stages)
            r0 = (t % gs) * LANES + (t // gs) * gs * SSM_GROUP
            for s in range(ns):
                part = jnp.concatenate(tiles[s * gs:(s + 1) * gs], axis=1)
                kin_s[s, r0:r0 + gs * SSM_GROUP, :] = part[s * gs * SSM_GROUP:(s + 1) * gs * SSM_GROUP, :].astype(BF)

    xs = []
    for t in range(T):
        ct = uc_ref[pl.ds(t, n_ctx, stride=T), :]
        xs.append(jnp.concatenate([ct, ul_ref[pl.ds(t, n_lat, stride=T), :], ct], axis=0))
    xs = swapped(xs, stages)
    for s in range(ns):
        for k in range(gs):
            u_s[s, :, lanes(k)] = xs[s * gs + k].astype(BF)
    for s in range(ns):
        r = jnp.dot(u_s[s], bm_s[s], preferred_element_type=F32)
        for c in range(tps):
            s_ref[s * tps + c] = r[:, lanes(c)]

    chains = []
    for s in range(ns):
        for d in range(2):
            for k in range(tpq):
                cr, ci = s * tps + 2 * d * tpq + k, s * tps + (2 * d + 1) * tpq + k
                lr = slice(col0(s, 2 * d) + k * LANES, col0(s, 2 * d) + (k + 1) * LANES)
                li = slice(col0(s, 2 * d + 1) + k * LANES, col0(s, 2 * d + 1) + (k + 1) * LANES)
                chains.append((cr, ci, d, lr, li))
    nt = ns * tps

    def scan_step(i, carry, record):
        new = list(carry)
        for cr, ci, d, lr_sl, li_sl in chains:
            rsl = pl.ds(i if d == 0 else seg_len - 1 - i, SSM_SEG, stride=seg_len)
            hr, hi = carry[cr], carry[ci]
            if record:
                h_ref[cr, rsl, :] = hr
                h_ref[ci, rsl, :] = hi
            lr, li = lam_ref[:, lr_sl], lam_ref[:, li_sl]
            new[cr] = lr * hr - li * hi + s_ref[cr, rsl, :]
            new[ci] = lr * hi + li * hr + s_ref[ci, rsl, :]
        return tuple(new)

    zero = jnp.zeros((SSM_SEG, LANES), F32)
    fin = lax.fori_loop(0, seg_len, lambda i, c: scan_step(i, c, False), (zero,) * nt, unroll=2)
    for c in range(nt):
        e_ref[c, 0:SSM_SEG, :] = fin[c]

    for cr, ci, d, lr_sl, li_sl in chains:
        lr, li = lamseg_ref[:, lr_sl], lamseg_ref[:, li_sl]
        er = jnp.zeros((1, LANES), F32)
        ei = jnp.zeros((1, LANES), F32)
        for n in range(SSM_SEG):
            g = n if d == 0 else SSM_SEG - 1 - n
            e_ref[cr, SSM_SEG + g:SSM_SEG + g + 1, :] = er
            e_ref[ci, SSM_SEG + g:SSM_SEG + g + 1, :] = ei
            fr, fi = e_ref[cr, g:g + 1, :], e_ref[ci, g:g + 1, :]
            er, ei = lr * er - li * ei + fr, lr * ei + li * er + fi

    ent = tuple(e_ref[c, SSM_SEG:2 * SSM_SEG, :] for c in range(nt))
    lax.fori_loop(0, seg_len, lambda i, c: scan_step(i, c, True), ent, unroll=2)

    ys, ycs = [], []
    for s in range(ns):
        hcat = jnp.concatenate([h_ref[s * tps + c].astype(BF) for c in range(tps)], axis=1)
        y = _gelu_tanh(jnp.dot(u_s[s], kin_s[s], preferred_element_type=F32)
                       + jnp.dot(hcat, cm_s[s], preferred_element_type=F32))
        ys += [y[:, lanes(k)] for k in range(gs)]
        hc = jnp.concatenate(
            [h_ref[s * tps + c, 0:n_ctx, :].astype(BF) for c in range(2 * tpq)]
            + [h_ref[s * tps + c, rows - n_ctx:rows, :].astype(BF) for c in range(2 * tpq, tps)], axis=1)
        yc = _gelu_tanh(jnp.dot(u_s[s, 0:n_ctx, :], kin_s[s], preferred_element_type=F32)
                        + jnp.dot(hc, cm_s[s], preferred_element_type=F32))
        ycs += [yc[:, lanes(k)] for k in range(gs)]
    ys = swapped(ys, stages[::-1])
    ycs = swapped(ycs, stages[::-1])
    for t in range(T):
        z_ref[pl.ds(t, n_lat, stride=T), :] = ys[t][n_ctx:n_ctx + n_lat, :]
        zc_ref[pl.ds(t, n_ctx, stride=T), :] = ycs[t]


def _ssm_weights(a_re, a_im, log_dt, b_re, b_im, c_re, c_im, d_skip, seg_len):
    T = SSM_T
    g, p = a_re.shape[1], a_re.shape[2]
    hc = b_re.shape[-1]
    gpt = LANES // hc
    J = g // gpt
    lam = lax.complex(a_re.astype(F32), a_im.astype(F32))
    dt = jnp.exp(log_dt.astype(F32))[..., None]
    ldt = lam * dt
    lam_bar = jnp.exp(ldt)
    bbar = ((lam_bar - 1) / lam)[..., None] * lax.complex(b_re.astype(F32), b_im.astype(F32))
    cmat = lax.complex(c_re.astype(F32), c_im.astype(F32))
    kk = jnp.arange(T + 1, dtype=F32)
    pw = jnp.exp(ldt[:, None] * kk[None, :, None, None])

    wf = pw[0, :T][::-1][:, :, None, :] * jnp.swapaxes(bbar[0], 1, 2)[None]
    wb = pw[1, :T][:, :, None, :] * jnp.swapaxes(bbar[1], 1, 2)[None]
    wq = jnp.stack([wf.real, wf.imag, wb.real, wb.imag]).reshape(4, T, J, gpt, hc, p)
    wc = wq.transpose(2, 1, 4, 0, 3, 5).reshape(J, T * hc, 4 * gpt * p)

    df = cmat[0][None] * pw[0, 1:][:, :, None, :]
    db = cmat[1][None] * pw[1, 1:][::-1][:, :, None, :]
    dq = jnp.stack([df.real, -df.imag, db.real, -db.imag]).reshape(4, T, J, gpt, hc, p)
    dc = dq.transpose(2, 0, 3, 5, 1, 4).reshape(J, 4 * gpt * p, T * hc)

    mf = jnp.einsum('gcp,kgp,gph->khgc', cmat[0], pw[0, :T], bbar[0]).real
    mb = jnp.einsum('gcp,kgp,gph->khgc', cmat[1], pw[1, :T], bbar[1]).real
    skip = jnp.eye(hc, dtype=F32)[:, None, :] * d_skip.astype(F32).reshape(g, hc)[None]
    m0 = mf[0] + mb[0] + skip
    lags = jnp.concatenate([mb[1:][::-1], m0[None], mf[1:]], axis=0)
    kc = lags.reshape(2 * T - 1, hc, J, LANES).transpose(2, 1, 0, 3).reshape(J, hc, (2 * T - 1) * LANES)

    def lam_rows(power):
        lp = jnp.exp(ldt * power)
        lt = lp.reshape(2, J, gpt * p)
        return jnp.concatenate([lt[0].real, lt[0].imag, lt[1].real, lt[1].imag], axis=-1)[:, None, :]

    lam_t = jnp.broadcast_to(lam_rows(float(T)), (J, SSM_SEG, 4 * gpt * p))
    return wc.astype(BF), kc.astype(BF), dc.astype(BF), lam_t, lam_rows(float(T * seg_len))


def _ssm_expanders(gs):
    r = np.arange(gs * LANES)
    k, slot, ch = r // LANES, (r % LANES) // SSM_GROUP, r % SSM_GROUP
    t = (slot // gs) * gs + k
    rexp = np.zeros((gs * LANES, SSM_T * SSM_GROUP), np.float32)
    rexp[r, t * SSM_GROUP + ch] = 1.0
    return jnp.asarray(rexp, BF), jnp.asarray(rexp.T, BF)


def _ssm(u3, u3c, wc, kc, dc, lam, lamseg, *, layer, batch, seq, n_ctx):
    T = SSM_T
    J = u3.shape[0]
    ns = wc.shape[-1]
    nsets = 2 ** len(SSM_STAGES)
    gs = T // nsets
    sw = gs * LANES
    n_lat, n_c = seq // T, n_ctx // T
    rows = n_lat + 2 * n_c
    seg_len = rows // SSM_SEG
    assert rows % SSM_SEG == 0 and n_c % 16 == 0
    rexp, cexp = _ssm_expanders(gs)
    kern = functools.partial(_ssm_kernel, seg_len=seg_len, n_lat=n_lat, n_ctx=n_c, stages=SSM_STAGES)
    const = lambda j, b: (0, 0)
    per_j = lambda j, b: (layer, j, 0, 0)
    per_jb = lambda j, b: (j, b, 0)
    return pl.pallas_call(
        kern,
        grid=(J, batch),
        in_specs=[pl.BlockSpec((None, seq, LANES), per_jb, pipeline_mode=pl.Buffered(1)),
                  pl.BlockSpec((None, n_ctx, LANES), per_jb),
                  pl.BlockSpec((None, None) + wc.shape[2:], per_j),
                  pl.BlockSpec((None, None) + kc.shape[2:], per_j),
                  pl.BlockSpec((None, None) + dc.shape[2:], per_j),
                  pl.BlockSpec(rexp.shape, const),
                  pl.BlockSpec(cexp.shape, const),
                  pl.BlockSpec((None, None, SSM_SEG, ns), per_j),
                  pl.BlockSpec((None, None, 1, ns), per_j)],
        out_specs=[pl.BlockSpec((None, seq, LANES), per_jb),
                   pl.BlockSpec((None, n_ctx, LANES), per_jb)],
        out_shape=[jax.ShapeDtypeStruct(u3.shape, F32),
                   jax.ShapeDtypeStruct(u3c.shape, F32)],
        scratch_shapes=[pltpu.VMEM((nsets, rows, sw), BF),
                        pltpu.VMEM((nsets, sw, ns // nsets), BF),
                        pltpu.VMEM((nsets, sw, sw), BF),
                        pltpu.VMEM((nsets, ns // nsets, sw), BF),
                        pltpu.VMEM((ns // LANES, rows, LANES), F32),
                        pltpu.VMEM((ns // LANES, rows, LANES), F32),
                        pltpu.VMEM((ns // LANES, 2 * SSM_SEG, LANES), F32)],
        compiler_params=_cparams("arbitrary", "arbitrary"),
        name="ssm_chunk_scan",
    )(u3, u3c, wc, kc, dc, rexp, cexp, lam, lamseg)


def _dft_a_kernel(w_ref, x_ref, o_ref):
    o_ref[...] = jnp.dot(w_ref[...], x_ref[...], preferred_element_type=F32).astype(BF)


def _dft_rows(n):
    k = np.arange(n)
    m = (k[:, None] * k[None, :]) % n
    ang = m * (2.0 * math.pi / n)
    return jnp.asarray(np.concatenate([np.cos(ang), -np.sin(ang)], axis=0), BF)


def _dft_a(w, x3):
    B, n, cols = x3.shape
    tn = min(cols, 4096)
    return pl.pallas_call(
        _dft_a_kernel,
        grid=(B, cols // tn),
        in_specs=[pl.BlockSpec((2 * n, n), lambda b, i: (0, 0)),
                  pl.BlockSpec((None, n, tn), lambda b, i: (b, 0, i))],
        out_specs=pl.BlockSpec((None, 2 * n, tn), lambda b, i: (b, 0, i)),
        out_shape=jax.ShapeDtypeStruct((B, 2 * n, cols), BF),
        compiler_params=_cparams("parallel", "parallel"),
        name="dft_stage_a",
    )(w, x3)


def _store_lane_tiles(o_ref, p, n, wf, idx):
    per = wf // LANES
    for c in range(2 * per):
        part = p[(c // per) * n:(c // per + 1) * n, (c % per) * LANES:(c % per + 1) * LANES]
        o_ref[(c,) + idx] = part


def _dft_b_kernel(a_ref, g_ref, o_ref, *, kb, l1, wf):
    for i in range(kb):
        g = jnp.concatenate([g_ref[0, i], g_ref[1, i]], axis=0)
        p = jnp.dot(a_ref[i], g, preferred_element_type=F32)
        _store_lane_tiles(o_ref, p, l1, wf, (slice(None), i, slice(None)))


def _dft_b_mats(seq, l1, l2):
    k2 = np.arange(l2)[:, None, None]
    k1 = np.arange(l1)[None, :, None]
    j1 = np.arange(l1)[None, None, :]
    m = ((k1 * l2 + k2) * j1) % seq
    ang = m * (2.0 * math.pi / seq)
    ar, ai = np.cos(ang).astype(np.float32), -np.sin(ang).astype(np.float32)
    top = np.concatenate([ar, -ai], axis=2)
    bot = np.concatenate([ai, ar], axis=2)
    return jnp.asarray(np.concatenate([top, bot], axis=1), BF)


def _dft_b(amat, g5, *, kb):
    B, _, l2, l1, wf = g5.shape
    nlt = 2 * wf // LANES
    kern = functools.partial(_dft_b_kernel, kb=kb, l1=l1, wf=wf)
    out = pl.pallas_call(
        kern,
        grid=(B, l2 // kb),
        in_specs=[pl.BlockSpec((kb, 2 * l1, 2 * l1), lambda b, i: (i, 0, 0)),
                  pl.BlockSpec((None, 2, kb, l1, wf), lambda b, i: (b, 0, i, 0, 0))],
        out_specs=pl.BlockSpec((None, nlt, l1, kb, LANES), lambda b, i: (b, 0, 0, i, 0)),
        out_shape=jax.ShapeDtypeStruct((B, nlt, l1, l2, LANES), F32),
        compiler_params=_cparams("parallel", "parallel"),
        name="dft_stage_b",
    )(amat, g5)
    return out.reshape(B, nlt, l1 * l2, LANES)


def _dft_ctx_kernel(w_ref, x_ref, o_ref, *, n, wf):
    p = jnp.dot(w_ref[...], x_ref[...], preferred_element_type=F32)
    _store_lane_tiles(o_ref, p, n, wf, ())


def _dft_ctx(w, x3):
    B, n, wf = x3.shape
    nlt = 2 * wf // LANES
    kern = functools.partial(_dft_ctx_kernel, n=n, wf=wf)
    return pl.pallas_call(
        kern,
        grid=(B,),
        in_specs=[pl.BlockSpec((2 * n, n), lambda b: (0, 0)),
                  pl.BlockSpec((None, n, wf), lambda b: (b, 0, 0))],
        out_specs=pl.BlockSpec((None, nlt, n, LANES), lambda b: (b, 0, 0, 0)),
        out_shape=jax.ShapeDtypeStruct((B, nlt, n, LANES), F32),
        compiler_params=_cparams("parallel"),
        name="dft_ctx",
    )(w, x3)


def _channel_dft(wf, seq):
    c = np.arange(wf)
    same = (c[:, None] // FNET_GROUP) == (c[None, :] // FNET_GROUP)
    m = ((c[:, None] % FNET_GROUP) * (c[None, :] % FNET_GROUP)) % FNET_GROUP
    ang = m * (2.0 * math.pi / FNET_GROUP)
    mask = same / math.sqrt(seq * FNET_GROUP)
    return jnp.asarray(np.concatenate([np.cos(ang) * mask, np.sin(ang) * mask], axis=0), BF)


def _softmax_pv(parts, l_shape):
    m = parts[0][0].max(axis=-1, keepdims=True)
    for s, _ in parts[1:]:
        m = jnp.maximum(m, s.max(axis=-1, keepdims=True))
    l = jnp.zeros(l_shape, F32)
    o = None
    for s, v in parts:
        p = jnp.exp2(s - m)
        l = l + p.sum(axis=-1, keepdims=True)
        pv = jnp.dot(p.astype(BF), v, preferred_element_type=F32)
        o = pv if o is None else o + pv
    return o / l


def _na_kernel(kb_ref, pid_ref, dr_ref, q_ref, k_ref, v_ref, kc_ref, vc_ref, cbx_ref, o_ref, bias_ref,
               *, nkeys, heads, nsub):
    step = pl.program_id(1)
    nq = q_ref.shape[0] // nsub
    rq, kr = nq // GRID_W, nkeys // GRID_W
    lane = lax.broadcasted_iota(jnp.int32, (nq, LANES), 1)
    low = lane < HEAD_DIM
    dn = (((1,), (1,)), ((), ()))

    for sb in range(nsub):
        i = step * nsub + sb
        pid = pid_ref[i]

        @pl.when((step == 0) | (pid != pid_ref[jnp.maximum(i - nsub, 0)]))
        def _build_bias(sb=sb, pid=pid):
            for qi in range(rq):
                for w in range(kr):
                    a = dr_ref[(pid * rq + qi) * kr + w]
                    for h in range(heads):
                        bias_ref[sb, h, qi * GRID_W:(qi + 1) * GRID_W, w * GRID_W:(w + 1) * GRID_W] = cbx_ref[h, a]

    tasks = [(sb, h) for sb in range(nsub) for h in range(heads)]
    starts = [pl.multiple_of(kb_ref[step * nsub + sb] * GRID_W, GRID_W) for sb in range(nsub)]

    def scores(sb, h):
        sl = slice((h // 2) * LANES, (h // 2 + 1) * LANES)
        q2 = q_ref[sb * nq:(sb + 1) * nq, sl]
        qm = jnp.where(low if h % 2 == 0 else jnp.logical_not(low), q2, jnp.zeros_like(q2))
        s_w = lax.dot_general(qm, k_ref[pl.ds(starts[sb], nkeys), sl], dn, preferred_element_type=F32)
        s_c = lax.dot_general(qm, kc_ref[:, sl], dn, preferred_element_type=F32)
        return s_w + bias_ref[sb, h], s_c

    def weights(s_w, s_c):
        m = jnp.maximum(s_w.max(axis=-1, keepdims=True), s_c.max(axis=-1, keepdims=True))
        p_w = jnp.exp2(s_w - m)
        p_c = jnp.exp2(s_c - m)
        l = p_w.sum(axis=-1, keepdims=True) + p_c.sum(axis=-1, keepdims=True)
        return p_w.astype(BF), p_c.astype(BF), l

    outs = {}

    def finish(sb, h, p_w, p_c, l):
        sl = slice((h // 2) * LANES, (h // 2 + 1) * LANES)
        o = (jnp.dot(p_w, v_ref[pl.ds(starts[sb], nkeys), sl], preferred_element_type=F32)
             + jnp.dot(p_c, vc_ref[:, sl], preferred_element_type=F32))
        outs[h % 2] = o / l
        if h % 2 == 1:
            o_ref[sb * nq:(sb + 1) * nq, sl] = jnp.where(low, outs[0], outs[1]).astype(BF)

    nt = len(tasks)
    sc = {0: scores(*tasks[0])}
    if nt > 1:
        sc[1] = scores(*tasks[1])
    pw = {0: weights(*sc.pop(0))}
    for n in range(nt):
        if n + 2 < nt:
            sc[n + 2] = scores(*tasks[n + 2])
        if n + 1 < nt:
            pw[n + 1] = weights(*sc.pop(n + 1))
        finish(*tasks[n], *pw.pop(n))


def _na_plan(rows, rq):
    wr = min(NA_ROWS, rows)
    kr = rq + wr - 1
    nblk = rows // rq
    kbs, pids, pats = [], [], []
    for blk in range(nblk):
        r0 = blk * rq
        rs = [int(np.clip(r0 + i - wr // 2, 0, rows - wr)) for i in range(rq)]
        kb = min(rs[0], rows - kr)
        pat = (tuple(r - kb for r in rs), r0 - kb)
        if pat not in pats:
            pats.append(pat)
        kbs.append(kb)
        pids.append(pats.index(pat))
    return np.asarray(kbs, np.int32), np.asarray(pids, np.int32), pats, kr, wr


def _na_bias(rpb, pats, rq, kr, wr):
    cols = np.arange(GRID_W)
    cstart = np.clip(cols - NA_COLS // 2, 0, GRID_W - NA_COLS)
    kc = np.arange(GRID_W)
    col_ok = (kc[None, :] >= cstart[:, None]) & (kc[None, :] < cstart[:, None] + NA_COLS)
    dc = kc[None, :] - cols[:, None] + NA_COLS - 1
    depth, heads, nr, ncol = rpb.shape
    sel = (dc[:, :, None] == np.arange(ncol)[None, None, :]) & col_ok[:, :, None]
    cb = jnp.einsum('lhab,ckb->lhack', rpb.astype(F32), jnp.asarray(sel, F32), precision=lax.Precision.HIGHEST)
    cb = jnp.where(jnp.asarray(col_ok)[None, None, None], cb * math.log2(math.e), NEG)
    cbx = jnp.concatenate([cb, jnp.full((depth, heads, 1, GRID_W, GRID_W), NEG, F32)], axis=2)
    slots = []
    for rs_off, r_off in pats:
        i = np.arange(rq)[:, None]
        w = np.arange(kr)[None, :]
        rso = np.asarray(rs_off)[:, None]
        row_ok = (w >= rso) & (w < rso + wr)
        slots.append(np.where(row_ok, w - r_off - i + NA_ROWS - 1, nr))
    return cbx, np.stack(slots).reshape(-1).astype(np.int32)


def _na_tables(rpb_all, seq, rq):
    kbs, pids, pats, kr, wr = _na_plan(seq // GRID_W, rq)
    cbx, slots = _na_bias(rpb_all, pats, rq, kr, wr)
    return dict(kbs=jnp.asarray(kbs), pids=jnp.asarray(pids), slots=jnp.asarray(slots), cbx=cbx, kr=kr, rq=rq)


def _na_latent(q, k, v, kc, vc, tabs, layer, *, batch, seq, n_ctx):
    rows = seq // GRID_W
    cbx, rq, kr = tabs["cbx"], tabs["rq"], tabs["kr"]
    heads = cbx.shape[1]
    nblk = rows // rq
    nq = rq * GRID_W
    nkeys = kr * GRID_W
    w = q.shape[1]
    nsub = 2 if nblk % 2 == 0 else 1
    nstep = nblk // nsub
    kern = functools.partial(_na_kernel, nkeys=nkeys, heads=heads, nsub=nsub)
    grid_spec = pltpu.PrefetchScalarGridSpec(
        num_scalar_prefetch=3,
        grid=(batch, nstep),
        in_specs=[pl.BlockSpec((nsub * nq, w), lambda b, i, *_: (b * nstep + i, 0)),
                  pl.BlockSpec((seq, w), lambda b, i, *_: (b, 0)),
                  pl.BlockSpec((seq, w), lambda b, i, *_: (b, 0)),
                  pl.BlockSpec((n_ctx, w), lambda b, i, *_: (b, 0)),
                  pl.BlockSpec((n_ctx, w), lambda b, i, *_: (b, 0)),
                  pl.BlockSpec((None,) + cbx.shape[1:], lambda b, i, *_: (layer, 0, 0, 0, 0))],
        out_specs=pl.BlockSpec((nsub * nq, w), lambda b, i, *_: (b * nstep + i, 0)),
        scratch_shapes=[pltpu.VMEM((nsub, heads, nq, nkeys), F32)],
    )
    return pl.pallas_call(
        kern,
        grid_spec=grid_spec,
        out_shape=jax.ShapeDtypeStruct(q.shape, BF),
        compiler_params=_cparams("arbitrary", "arbitrary"),
        name="na_latent",
    )(tabs["kbs"], tabs["pids"], tabs["slots"], q, k, v, kc, vc, cbx)


def _na_ctx_kernel(q_ref, k_ref, v_ref, o_ref, *, heads):
    nq = q_ref.shape[0]
    lane = lax.broadcasted_iota(jnp.int32, (nq, LANES), 1)
    low = lane < HEAD_DIM
    dn = (((1,), (1,)), ((), ()))
    for hp in range(heads // 2):
        sl = slice(hp * LANES, (hp + 1) * LANES)
        q2 = q_ref[:, sl]
        k2 = k_ref[:, sl]
        v2 = v_ref[:, sl]
        outs = []
        for hh in range(2):
            qm = jnp.where(low if hh == 0 else jnp.logical_not(low), q2, jnp.zeros_like(q2))
            s = lax.dot_general(qm, k2, dn, preferred_element_type=F32)
            outs.append(_softmax_pv([(s, v2)], (nq, 1)))
        o_ref[:, sl] = jnp.where(low, outs[0], outs[1]).astype(BF)


def _na_ctx(qc, kc, vc, *, batch, n_ctx, heads):
    w = qc.shape[1]
    spec = pl.BlockSpec((n_ctx, w), lambda b: (b, 0))
    return pl.pallas_call(
        functools.partial(_na_ctx_kernel, heads=heads),
        grid=(batch,),
        in_specs=[spec, spec, spec],
        out_specs=spec,
        out_shape=jax.ShapeDtypeStruct(qc.shape, BF),
        compiler_params=_cparams("parallel"),
        name="na_ctx",
    )(qc, kc, vc)


def _outproj_kernel(z_ref, pf_ref, na_ref, x_ref, gate_ref, gpost_ref,
                    wglu_ref, cs_ref, wfo_ref, wout_ref, o_ref, *, widths, sub):
    w_ssm, w_fn, w_na = widths
    for r0 in range(0, x_ref.shape[0], sub):
        rs = slice(r0, r0 + sub)
        z = jnp.concatenate([z_ref[j, rs, :] for j in range(w_ssm // LANES)], axis=1)
        gl = jnp.dot(z.astype(BF), wglu_ref[...], preferred_element_type=F32)
        y_ssm = (z * jax.nn.sigmoid(gl)).astype(BF)
        pf = jnp.concatenate([pf_ref[c, rs, :] for c in range(pf_ref.shape[0])], axis=1).astype(BF)
        mixed = jnp.dot(pf, cs_ref[...], preferred_element_type=F32).astype(BF)
        y_fft = jnp.dot(mixed, wfo_ref[...], preferred_element_type=F32).astype(BF)
        y = jnp.concatenate([y_ssm, y_fft, na_ref[rs, :]], axis=1)
        o = jnp.dot(y, wout_ref[...], preferred_element_type=F32)
        ms = jnp.mean(o * o, axis=-1, keepdims=True)
        o_ref[rs, :] = x_ref[rs, :] + gate_ref[...] * (o * lax.rsqrt(ms + EPS) * gpost_ref[...])


def _outproj(z3, pf, yna, x2d, mod3, g_post, w_glu, cs, w_fo, w_out, *,
             layer, tm, tiles_per_batch, mod_row, widths):
    n, d = x2d.shape
    w_ssm, w_fn, w_na = widths
    if mod_row is None:
        row = lambda i: i // tiles_per_batch
    else:
        row = lambda i: mod_row
    kern = functools.partial(_outproj_kernel, widths=widths, sub=min(tm, 512))
    return pl.pallas_call(
        kern,
        grid=(n // tm,),
        in_specs=[pl.BlockSpec((w_ssm // LANES, tm, LANES), lambda i: (0, i, 0)),
                  pl.BlockSpec((None, pf.shape[1], tm, LANES),
                               lambda i: (i // tiles_per_batch, 0, i % tiles_per_batch, 0)),
                  pl.BlockSpec((tm, w_na), lambda i: (i, 0)),
                  pl.BlockSpec((tm, d), lambda i: (i, 0)),
                  pl.BlockSpec((None, 1, d), lambda i: (row(i), 0, 2)),
                  pl.BlockSpec((None, 1, d), lambda i: (layer, 0, 0)),
                  _layer_resident(w_glu, layer),
                  _resident(cs.shape, lambda i: (0, 0)),
                  _layer_resident(w_fo, layer),
                  _layer_resident(w_out, layer)],
        out_specs=pl.BlockSpec((tm, d), lambda i: (i, 0)),
        out_shape=jax.ShapeDtypeStruct((n, d), F32),
        compiler_params=_cparams("parallel"),
        name="outproj",
    )(z3, pf, yna, x2d, mod3, g_post.reshape(-1, 1, d), w_glu, cs, w_fo, w_out)


def _ffn_kernel(x_ref, sh_ref, sc_ref, gate_ref, gpre_ref, gpost_ref, wg_ref, wu_ref, wd_ref,
                o_ref, a_ref, *, sub, chunk):
    dff = wg_ref.shape[1]
    for r0 in range(0, x_ref.shape[0], sub):
        rs = slice(r0, r0 + sub)
        x = x_ref[rs, :]
        ms = jnp.mean(x * x, axis=-1, keepdims=True)
        m = ((x * lax.rsqrt(ms + EPS) * gpre_ref[...]) * (1.0 + sc_ref[...]) + sh_ref[...]).astype(BF)
        for c0 in range(0, dff, chunk):
            sl = slice(c0, min(c0 + chunk, dff))
            g = jnp.dot(m, wg_ref[:, sl], preferred_element_type=F32)
            u = jnp.dot(m, wu_ref[:, sl], preferred_element_type=F32)
            a_ref[rs, sl] = (g * jax.nn.sigmoid(g) * u).astype(BF)
        y = jnp.dot(a_ref[rs, :], wd_ref[...], preferred_element_type=F32)
        ms2 = jnp.mean(y * y, axis=-1, keepdims=True)
        o_ref[rs, :] = x + gate_ref[...] * (y * lax.rsqrt(ms2 + EPS) * gpost_ref[...])


def _ffn(x2d, mod3, g_pre, g_post, wg, wu, wd, *, layer, tm, tiles_per_batch, mod_row):
    n, d = x2d.shape
    dff = wg.shape[2]
    if mod_row is None:
        row = lambda i: i // tiles_per_batch
    else:
        row = lambda i: mod_row
    return pl.pallas_call(
        functools.partial(_ffn_kernel, sub=min(tm, 512), chunk=3 * MXU_TILE),
        grid=(n // tm,),
        in_specs=[pl.BlockSpec((tm, d), lambda i: (i, 0)),
                  pl.BlockSpec((None, 1, d), lambda i: (row(i), 0, 3)),
                  pl.BlockSpec((None, 1, d), lambda i: (row(i), 0, 4)),
                  pl.BlockSpec((None, 1, d), lambda i: (row(i), 0, 5)),
                  pl.BlockSpec((None, 1, d), lambda i: (layer, 0, 0)),
                  pl.BlockSpec((None, 1, d), lambda i: (layer, 0, 0)),
                  _layer_resident(wg, layer),
                  _layer_resident(wu, layer),
                  _layer_resident(wd, layer)],
        out_specs=pl.BlockSpec((tm, d), lambda i: (i, 0)),
        out_shape=jax.ShapeDtypeStruct((n, d), F32),
        scratch_shapes=[pltpu.VMEM((tm, dff), BF)],
        compiler_params=_cparams("parallel"),
        name="ffn",
    )(x2d, mod3, mod3, mod3, g_pre.reshape(-1, 1, d), g_post.reshape(-1, 1, d), wg, wu, wd)


def _layer(x2d, xc2d, p, w, layer, *, batch, seq, n_ctx, last, rope_tabs, consts):
    d = x2d.shape[1]
    w_ssm = consts["w_ssm"]
    w_fn = w["w_fourier"].shape[1]
    w_na = (w["w_in"].shape[2] - w_ssm - w_fn) // 3
    widths = (w_ssm, w_fn, w_na)
    heads = w_na // HEAD_DIM
    tm = min(512, seq)
    tmc = min(512, batch * n_ctx)

    mod = w["mod"][layer]
    mod3 = mod.reshape(mod.shape[0], 1, 6 * d)

    cos, sin = rope_tabs
    tm2 = min(2 * tm, seq)
    u3, f, q, k, v = _inproj(x2d, mod3, p["g_pre_mix"], w["w_in"], cos, sin, layer=layer, tm=tm2,
                             tiles_per_batch=seq // tm2, mod_row=None, widths=widths, rope=True)
    ones = jnp.ones((tmc, LANES), F32)
    u3c, fc, qc, kc, vc = _inproj(xc2d, mod3, p["g_pre_mix"], w["w_in"], ones, ones, layer=layer, tm=tmc,
                                  tiles_per_batch=1, mod_row=batch, widths=widths, rope=False)

    z3, zc3 = _ssm(u3, u3c, *w["ssm"], layer=layer, batch=batch, seq=seq, n_ctx=n_ctx)

    l2 = consts["l2"]
    l1 = seq // l2
    g = _dft_a(consts["dft_a"], f.reshape(batch, l2, l1 * w_fn))
    pf = _dft_b(consts["dft_b"], g.reshape(batch, 2, l2, l1, w_fn), kb=min(8, l2))

    yna = _na_latent(q, k, v, kc, vc, consts["na"], layer, batch=batch, seq=seq, n_ctx=n_ctx)

    w_glu, w_fo, w_out = w["w_glu"], w["w_fourier"], w["w_out"]
    x2d = _outproj(z3, pf, yna, x2d, mod3, p["g_post_mix"], w_glu, consts["cs_lat"], w_fo, w_out,
                   layer=layer, tm=tm2, tiles_per_batch=seq // tm2, mod_row=None, widths=widths)
    wg, wu, wd = w["w_ffn_gate"], w["w_ffn_up"], w["w_ffn_down"]
    x2d = _ffn(x2d, mod3, p["g_pre_ffn"], p["g_post_ffn"], wg, wu, wd,
               layer=layer, tm=tm2, tiles_per_batch=seq // tm2, mod_row=None)

    if not last:
        pfc = _dft_ctx(consts["dft_ctx"], fc.reshape(batch, n_ctx, w_fn))
        ynac = _na_ctx(qc, kc, vc, batch=batch, n_ctx=n_ctx, heads=heads)
        tc = min(tmc, n_ctx)
        xc2d = _outproj(zc3, pfc, ynac, xc2d, mod3, p["g_post_mix"], w_glu, consts["cs_ctx"], w_fo, w_out,
                        layer=layer, tm=tc, tiles_per_batch=n_ctx // tc, mod_row=batch, widths=widths)
        xc2d = _ffn(xc2d, mod3, p["g_pre_ffn"], p["g_post_ffn"], wg, wu, wd,
                    layer=layer, tm=tc, tiles_per_batch=n_ctx // tc, mod_row=batch)
    else:
        xc2d = None
    return x2d, xc2d


def kernel(x, c, ctx, c_ctx, w_mod, b_mod, g_pre_mix, g_post_mix, w_in, ssm_a_re, ssm_a_im, ssm_log_dt,
           ssm_b_re, ssm_b_im, ssm_c_re, ssm_c_im, ssm_d, w_glu, w_fourier, na_rpb, w_out, g_pre_ffn,
           g_post_ffn, w_ffn_gate, w_ffn_up, w_ffn_down):
    batch, seq, d = x.shape
    n_ctx = ctx.shape[1]
    depth = w_mod.shape[0]
    w_fn = w_fourier.shape[1]
    params = dict(g_pre_mix=g_pre_mix, g_post_mix=g_post_mix, g_pre_ffn=g_pre_ffn, g_post_ffn=g_post_ffn)
    seg_len = (seq + 2 * n_ctx) // SSM_T // SSM_SEG
    ssm_ops = jax.vmap(functools.partial(_ssm_weights, seg_len=seg_len))(
        ssm_a_re, ssm_a_im, ssm_log_dt, ssm_b_re, ssm_b_im, ssm_c_re, ssm_c_im, ssm_d)
    nrow = -(-(batch + 1) // 8) * 8
    cstack = jnp.concatenate([c, c_ctx[None, :], jnp.zeros((nrow - batch - 1, d), c.dtype)], axis=0)
    weights = dict(mod=_mod_rows(cstack, w_mod, b_mod), ssm=ssm_ops,
                   w_in=_cast_bf16(w_in), w_glu=_cast_bf16(w_glu), w_fourier=_cast_bf16(w_fourier),
                   w_out=_cast_bf16(w_out), w_ffn_gate=_cast_bf16(w_ffn_gate),
                   w_ffn_up=_cast_bf16(w_ffn_up), w_ffn_down=_cast_bf16(w_ffn_down))

    l2 = 64 if seq % (64 * 8) == 0 else 8
    l1 = seq // l2
    consts = dict(
        l2=l2,
        w_ssm=ssm_d.shape[1],
        na=_na_tables(na_rpb, seq, 4),
        dft_a=_dft_rows(l2).astype(BF),
        dft_b=_dft_b_mats(seq, l1, l2).astype(BF),
        dft_ctx=_dft_rows(n_ctx).astype(BF),
        cs_lat=_channel_dft(w_fn, seq).astype(BF),
        cs_ctx=_channel_dft(w_fn, n_ctx).astype(BF),
    )
    rope_tabs = _rope_tables(seq)

    x2d = x.reshape(batch * seq, d)
    xc2d = ctx.reshape(batch * n_ctx, d)
    for layer in range(depth):
        x2d, xc2d = _layer(x2d, xc2d, params, weights, layer, batch=batch, seq=seq, n_ctx=n_ctx,
                           last=(layer == depth - 1), rope_tabs=rope_tabs, consts=consts)
    return x2d.reshape(batch, seq, d)
```

```python
import functools
import math

import numpy as np
import jax
import jax.numpy as jnp
from jax import lax
from jax.experimental import pallas as pl
from jax.experimental.pallas import tpu as pltpu

BF = jnp.bfloat16
F32 = jnp.float32

EPS = 1e-6
GRID_W = 64
HEAD_DIM = 64
NA_ROWS = 8
NA_COLS = 16
ROPE_BASE = 10000.0
SSM_GROUP = 16
SSM_STATE = 64
FNET_GROUP = 64
LANES = 128
MXU_TILE = 256
SSM_T = 8
SSM_SEG = 8
SSM_STAGES = (4, 2)
NEG = -1e30
VMEM_LIMIT = 56 * 1024 * 1024


def _cparams(*sem):
    return pltpu.CompilerParams(dimension_semantics=sem, vmem_limit_bytes=VMEM_LIMIT)


def _resident(shape, index_map):
    return pl.BlockSpec(shape, index_map, pipeline_mode=pl.Buffered(1))


def _layer_resident(w_all, layer):
    return pl.BlockSpec((None,) + w_all.shape[1:], lambda *_: (layer, 0, 0), pipeline_mode=pl.Buffered(1))


def _cast_kernel(w_ref, o_ref):
    o_ref[...] = w_ref[...].astype(BF)


def _cast_bf16(w_all):
    depth, rows, cols = w_all.shape
    br = rows
    while br * cols * 4 > 4 * 1024 * 1024 and br % 32 == 0:
        br //= 2
    spec = pl.BlockSpec((None, br, cols), lambda l, i: (l, i, 0))
    return pl.pallas_call(
        _cast_kernel,
        grid=(depth, rows // br),
        in_specs=[spec],
        out_specs=spec,
        out_shape=jax.ShapeDtypeStruct(w_all.shape, BF),
        compiler_params=_cparams("parallel", "parallel"),
        name="cast_bf16",
    )(w_all)


def _mod_kernel(c_ref, w_ref, b_ref, o_ref):
    cs = c_ref[...]
    s = (cs * jax.nn.sigmoid(cs)).astype(BF)
    o_ref[...] = jnp.dot(s, w_ref[...].astype(BF), preferred_element_type=F32) + b_ref[...]


def _mod_rows(cstack, w_mod, b_mod):
    rows, d = cstack.shape
    depth, _, n = w_mod.shape
    bn = 2 * d if n % (2 * d) == 0 else d
    return pl.pallas_call(
        _mod_kernel,
        grid=(depth, n // bn),
        in_specs=[pl.BlockSpec((rows, d), lambda l, i: (0, 0)),
                  pl.BlockSpec((None, d, bn), lambda l, i: (l, 0, i)),
                  pl.BlockSpec((None, 1, bn), lambda l, i: (l, 0, i))],
        out_specs=pl.BlockSpec((None, rows, bn), lambda l, i: (l, 0, i)),
        out_shape=jax.ShapeDtypeStruct((depth, rows, n), F32),
        compiler_params=_cparams("parallel", "parallel"),
        name="mod_rows",
    )(cstack, w_mod, b_mod.reshape(depth, 1, n))


def _rope_tile(t, cos, sin, first):
    partner = jnp.where(first, pltpu.roll(t, LANES - 16, 1), pltpu.roll(t, 16, 1))
    return t * cos + partner * sin


def _inproj_kernel(x_ref, sh_ref, sc_ref, g_ref, w_ref, cos_ref, sin_ref,
                   u_ref, f_ref, q_ref, k_ref, v_ref, *, widths, rope, sub):
    w_ssm, w_fn, w_na = widths
    scale = HEAD_DIM ** -0.5 * math.log2(math.e)
    for r0 in range(0, x_ref.shape[0], sub):
        rs = slice(r0, r0 + sub)
        x = x_ref[rs, :]
        ms = jnp.mean(x * x, axis=-1, keepdims=True)
        xn = x * lax.rsqrt(ms + EPS) * g_ref[...]
        m = (xn * (1.0 + sc_ref[...]) + sh_ref[...]).astype(BF)

        h = jnp.dot(m, w_ref[...], preferred_element_type=F32)

        def proj(lo, n, h=h):
            return h[:, lo:lo + n]

        u = proj(0, w_ssm)
        for j in range(w_ssm // LANES):
            u_ref[j, rs, :] = u[:, j * LANES:(j + 1) * LANES]
        f_ref[rs, :] = proj(w_ssm, w_fn).astype(BF)
        q = proj(w_ssm + w_fn, w_na)
        k = proj(w_ssm + w_fn + w_na, w_na)
        v_ref[rs, :] = proj(w_ssm + w_fn + 2 * w_na, w_na).astype(BF)
        if rope:
            cos = cos_ref[rs, :]
            sin = sin_ref[rs, :]
            lane = lax.broadcasted_iota(jnp.int32, cos.shape, 1)
            first = (lane % 32) < 16
            for j in range(w_na // LANES):
                sl = slice(j * LANES, (j + 1) * LANES)
                q_ref[rs, sl] = (_rope_tile(q[:, sl], cos, sin, first) * scale).astype(BF)
                k_ref[rs, sl] = _rope_tile(k[:, sl], cos, sin, first).astype(BF)
        else:
            q_ref[rs, :] = (q * scale).astype(BF)
            k_ref[rs, :] = k.astype(BF)


def _inproj(x2d, mod3, g_pre, w_in, cos, sin, *, layer, tm, tiles_per_batch, mod_row, widths, rope):
    n, d = x2d.shape
    w_ssm, w_fn, w_na = widths
    nt = n // tm
    if mod_row is None:
        row = lambda i: i // tiles_per_batch
    else:
        row = lambda i: mod_row
    pos = lambda i: (i % tiles_per_batch, 0)
    kern = functools.partial(_inproj_kernel, widths=widths, rope=rope, sub=min(tm, 512))
    return pl.pallas_call(
        kern,
        grid=(nt,),
        in_specs=[pl.BlockSpec((tm, d), lambda i: (i, 0)),
                  pl.BlockSpec((None, 1, d), lambda i: (row(i), 0, 0)),
                  pl.BlockSpec((None, 1, d), lambda i: (row(i), 0, 1)),
                  pl.BlockSpec((None, 1, d), lambda i: (layer, 0, 0)),
                  _layer_resident(w_in, layer),
                  pl.BlockSpec((tm, LANES), pos),
                  pl.BlockSpec((tm, LANES), pos)],
        out_specs=[pl.BlockSpec((w_ssm // LANES, tm, LANES), lambda i: (0, i, 0)),
                   pl.BlockSpec((tm, w_fn), lambda i: (i, 0)),
                   pl.BlockSpec((tm, w_na), lambda i: (i, 0)),
                   pl.BlockSpec((tm, w_na), lambda i: (i, 0)),
                   pl.BlockSpec((tm, w_na), lambda i: (i, 0))],
        out_shape=[jax.ShapeDtypeStruct((w_ssm // LANES, n, LANES), F32),
                   jax.ShapeDtypeStruct((n, w_fn), BF),
                   jax.ShapeDtypeStruct((n, w_na), BF),
                   jax.ShapeDtypeStruct((n, w_na), BF),
                   jax.ShapeDtypeStruct((n, w_na), BF)],
        compiler_params=_cparams("parallel"),
        name="inproj_rope" if rope else "inproj_ctx",
    )(x2d, mod3, mod3, g_pre.reshape(-1, 1, d), w_in, cos, sin)


def _rope_tables(seq):
    t = np.arange(seq)
    row = (t // GRID_W).astype(np.float32)
    col = (t % GRID_W).astype(np.float32)
    quarter = HEAD_DIM // 4
    freqs = (np.float32(ROPE_BASE) ** (-np.arange(quarter, dtype=np.float32) / quarter)).astype(np.float32)
    d = np.arange(LANES) % HEAD_DIM
    use_col = (d // (HEAD_DIM // 2)) == 1
    fidx = d % quarter
    sign = np.where((d % (HEAD_DIM // 2)) // quarter == 0, -1.0, 1.0)
    pos = np.where(use_col[None, :], col[:, None], row[:, None])
    ang = (pos * freqs[fidx][None, :]).astype(np.float32).astype(np.float64)
    return jnp.asarray(np.cos(ang), F32), jnp.asarray(np.sin(ang) * sign[None, :], F32)


def _gelu_tanh(x):
    c = math.sqrt(2.0 / math.pi)
    return x * (0.5 * (1.0 + jnp.tanh(c * (x + 0.044715 * (x * x * x)))))


def _group_of(idx, width, groups):
    shift = width.bit_length() - 1
    assert width == 1 << shift and groups & (groups - 1) == 0
    return lax.bitwise_and(lax.shift_right_logical(idx, shift), groups - 1)


def _same_group(shape, row0, row_width, col0, col_width, groups):
    r = lax.broadcasted_iota(jnp.int32, shape, 0) + row0
    c = lax.broadcasted_iota(jnp.int32, shape, 1) + col0
    return _group_of(r, row_width, groups) == _group_of(c, col_width, groups)


def _swap_stage(tiles, d):
    lane = lax.broadcasted_iota(jnp.int32, tiles[0].shape, 1)
    upper = lax.bitwise_and(lax.shift_right_logical(lane, SSM_GROUP.bit_length() - 1), d) != 0
    out = list(tiles)
    for x in range(len(tiles)):
        if x & d:
            continue
        a, b = tiles[x], tiles[x + d]
        out[x] = jnp.where(upper, pltpu.roll(b, SSM_GROUP * d, 1), a)
        out[x + d] = jnp.where(upper, b, pltpu.roll(a, LANES - SSM_GROUP * d, 1))
    return out


def _ssm_kernel(ul_ref, uc_ref, wc_ref, lag_ref, dc_ref, rexp_ref, cexp_ref, lam_ref, lamseg_ref,
                z_ref, zc_ref, u_s, bm_s, kin_s, cm_s, s_ref, h_ref, e_ref, *, seg_len, n_lat, n_ctx, stages):
    T = SSM_T
    ns = 2 ** len(stages)
    gs = T // ns
    sw = gs * LANES
    rows = u_s.shape[1]
    tps = s_ref.shape[0] // ns
    tpq = tps // 4
    nst = tpq * LANES
    gpt = LANES // SSM_GROUP
    per_q = gpt * SSM_STATE

    def lanes(c):
        return slice(c * LANES, (c + 1) * LANES)

    def col0(s, q):
        return q * per_q + s * nst

    def swapped(tiles, order):
        for d in order:
            tiles = _swap_stage(tiles, d)
        return tiles

    @pl.when(pl.program_id(1) == 0)
    def _expand_operators():
        rexp = rexp_ref[...]
        cexp = cexp_ref[...]
        for s in range(ns):
            wsel = jnp.concatenate([wc_ref[:, col0(s, q):col0(s, q) + nst] for q in range(4)], axis=1)
            blk = jnp.dot(rexp, wsel, preferred_element_type=F32)
            keep = _same_group(blk.shape, 0, SSM_GROUP, 0, SSM_STATE, gs)
            bm_s[s] = jnp.where(keep, blk, 0.0).astype(BF)
            dsel = jnp.concatenate([dc_ref[col0(s, q):col0(s, q) + nst, :] for q in range(4)], axis=0)
            blk = jnp.dot(dsel, cexp, preferred_element_type=F32)
            keep = _same_group(blk.shape, 0, SSM_STATE, 0, SSM_GROUP, gs)
            cm_s[s] = jnp.where(keep, blk, 0.0).astype(BF)
        r16 = lax.broadcasted_iota(jnp.int32, (LANES, SSM_GROUP), 0)
        c16 = lax.broadcasted_iota(jnp.int32, (LANES, SSM_GROUP), 1)
        chan = jnp.where(lax.bitwise_and(r16, SSM_GROUP - 1) == c16, 1.0, 0.0).astype(BF)
        for t in range(T):
            lo = (T - 1 - t) * LANES
            blk = jnp.dot(chan, lag_ref[:, lo:lo + T * LANES], preferred_element_type=F32)
            keep = _same_group(blk.shape, 0, SSM_GROUP, 0, SSM_GROUP, gpt)
            blk = jnp.where(keep, blk, 0.0)
            tiles = swapped([blk[:, lanes(x)] for x in range(T)], stages)
            r0 = (t % gs) * LANES + (t // gs) * gs * SSM_GROUP
            for s in range(ns):
                part = jnp.concatenate(tiles[s * gs:(s + 1) * gs], axis=1)
                kin_s[s, r0:r0 + gs * SSM_GROUP, :] = part[s * gs * SSM_GROUP:(s + 1) * gs * SSM_GROUP, :].astype(BF)

    xs = []
    for t in range(T):
        ct = uc_ref[pl.ds(t, n_ctx, stride=T), :]
        xs.append(jnp.concatenate([ct, ul_ref[pl.ds(t, n_lat, stride=T), :], ct], axis=0))
    xs = swapped(xs, stages)
    for s in range(ns):
        for k in range(gs):
            u_s[s, :, lanes(k)] = xs[s * gs + k].astype(BF)
    for s in range(ns):
        r = jnp.dot(u_s[s], bm_s[s], preferred_element_type=F32)
        for c in range(tps):
            s_ref[s * tps + c] = r[:, lanes(c)]

    chains = []
    for s in range(ns):
        for d in range(2):
            for k in range(tpq):
                cr, ci = s * tps + 2 * d * tpq + k, s * tps + (2 * d + 1) * tpq + k
                lr = slice(col0(s, 2 * d) + k * LANES, col0(s, 2 * d) + (k + 1) * LANES)
                li = slice(col0(s, 2 * d + 1) + k * LANES, col0(s, 2 * d + 1) + (k + 1) * LANES)
                chains.append((cr, ci, d, lr, li))
    nt = ns * tps

    def scan_step(i, carry, record):
        new = list(carry)
        for cr, ci, d, lr_sl, li_sl in chains:
            rsl = pl.ds(i if d == 0 else seg_len - 1 - i, SSM_SEG, stride=seg_len)
            hr, hi = carry[cr], carry[ci]
            if record:
                h_ref[cr, rsl, :] = hr
                h_ref[ci, rsl, :] = hi
            lr, li = lam_ref[:, lr_sl], lam_ref[:, li_sl]
            new[cr] = lr * hr - li * hi + s_ref[cr, rsl, :]
            new[ci] = lr * hi + li * hr + s_ref[ci, rsl, :]
        return tuple(new)

    zero = jnp.zeros((SSM_SEG, LANES), F32)
    fin = lax.fori_loop(0, seg_len, lambda i, c: scan_step(i, c, False), (zero,) * nt, unroll=4)
    for c in range(nt):
        e_ref[c, 0:SSM_SEG, :] = fin[c]

    for cr, ci, d, lr_sl, li_sl in chains:
        lr, li = lamseg_ref[:, lr_sl], lamseg_ref[:, li_sl]
        er = jnp.zeros((1, LANES), F32)
        ei = jnp.zeros((1, LANES), F32)
        for n in range(SSM_SEG):
            g = n if d == 0 else SSM_SEG - 1 - n
            e_ref[cr, SSM_SEG + g:SSM_SEG + g + 1, :] = er
            e_ref[ci, SSM_SEG + g:SSM_SEG + g + 1, :] = ei
            fr, fi = e_ref[cr, g:g + 1, :], e_ref[ci, g:g + 1, :]
            er, ei = lr * er - li * ei + fr, lr * ei + li * er + fi

    ent = tuple(e_ref[c, SSM_SEG:2 * SSM_SEG, :] for c in range(nt))
    lax.fori_loop(0, seg_len, lambda i, c: scan_step(i, c, True), ent, unroll=2)

    ys, ycs = [], []
    for s in range(ns):
        hcat = jnp.concatenate([h_ref[s * tps + c].astype(BF) for c in range(tps)], axis=1)
        y = _gelu_tanh(jnp.dot(u_s[s], kin_s[s], preferred_element_type=F32)
                       + jnp.dot(hcat, cm_s[s], preferred_element_type=F32))
        ys += [y[:, lanes(k)] for k in range(gs)]
        hc = jnp.concatenate(
            [h_ref[s * tps + c, 0:n_ctx, :].astype(BF) for c in range(2 * tpq)]
            + [h_ref[s * tps + c, rows - n_ctx:rows, :].astype(BF) for c in range(2 * tpq, tps)], axis=1)
        yc = _gelu_tanh(jnp.dot(u_s[s, 0:n_ctx, :], kin_s[s], preferred_element_type=F32)
                        + jnp.dot(hc, cm_s[s], preferred_element_type=F32))
        ycs += [yc[:, lanes(k)] for k in range(gs)]
    ys = swapped(ys, stages[::-1])
    ycs = swapped(ycs, stages[::-1])
    for t in range(T):
        z_ref[pl.ds(t, n_lat, stride=T), :] = ys[t][n_ctx:n_ctx + n_lat, :]
        zc_ref[pl.ds(t, n_ctx, stride=T), :] = ycs[t]


def _ssm_weights(a_re, a_im, log_dt, b_re, b_im, c_re, c_im, d_skip, seg_len):
    T = SSM_T
    g, p = a_re.shape[1], a_re.shape[2]
    hc = b_re.shape[-1]
    gpt = LANES // hc
    J = g // gpt
    lam = lax.complex(a_re.astype(F32), a_im.astype(F32))
    dt = jnp.exp(log_dt.astype(F32))[..., None]
    ldt = lam * dt
    lam_bar = jnp.exp(ldt)
    bbar = ((lam_bar - 1) / lam)[..., None] * lax.complex(b_re.astype(F32), b_im.astype(F32))
    cmat = lax.complex(c_re.astype(F32), c_im.astype(F32))
    kk = jnp.arange(T + 1, dtype=F32)
    pw = jnp.exp(ldt[:, None] * kk[None, :, None, None])

    wf = pw[0, :T][::-1][:, :, None, :] * jnp.swapaxes(bbar[0], 1, 2)[None]
    wb = pw[1, :T][:, :, None, :] * jnp.swapaxes(bbar[1], 1, 2)[None]
    wq = jnp.stack([wf.real, wf.imag, wb.real, wb.imag]).reshape(4, T, J, gpt, hc, p)
    wc = wq.transpose(2, 1, 4, 0, 3, 5).reshape(J, T * hc, 4 * gpt * p)

    df = cmat[0][None] * pw[0, 1:][:, :, None, :]
    db = cmat[1][None] * pw[1, 1:][::-1][:, :, None, :]
    dq = jnp.stack([df.real, -df.imag, db.real, -db.imag]).reshape(4, T, J, gpt, hc, p)
    dc = dq.transpose(2, 0, 3, 5, 1, 4).reshape(J, 4 * gpt * p, T * hc)

    mf = jnp.einsum('gcp,kgp,gph->khgc', cmat[0], pw[0, :T], bbar[0]).real
    mb = jnp.einsum('gcp,kgp,gph->khgc', cmat[1], pw[1, :T], bbar[1]).real
    skip = jnp.eye(hc, dtype=F32)[:, None, :] * d_skip.astype(F32).reshape(g, hc)[None]
    m0 = mf[0] + mb[0] + skip
    lags = jnp.concatenate([mb[1:][::-1], m0[None], mf[1:]], axis=0)
    kc = lags.reshape(2 * T - 1, hc, J, LANES).transpose(2, 1, 0, 3).reshape(J, hc, (2 * T - 1) * LANES)

    def lam_rows(power):
        lp = jnp.exp(ldt * power)
        lt = lp.reshape(2, J, gpt * p)
        return jnp.concatenate([lt[0].real, lt[0].imag, lt[1].real, lt[1].imag], axis=-1)[:, None, :]

    lam_t = jnp.broadcast_to(lam_rows(float(T)), (J, SSM_SEG, 4 * gpt * p))
    return wc.astype(BF), kc.astype(BF), dc.astype(BF), lam_t, lam_rows(float(T * seg_len))


def _ssm_expanders(gs):
    r = np.arange(gs * LANES)
    k, slot, ch = r // LANES, (r % LANES) // SSM_GROUP, r % SSM_GROUP
    t = (slot // gs) * gs + k
    rexp = np.zeros((gs * LANES, SSM_T * SSM_GROUP), np.float32)
    rexp[r, t * SSM_GROUP + ch] = 1.0
    return jnp.asarray(rexp, BF), jnp.asarray(rexp.T, BF)


def _ssm(u3, u3c, wc, kc, dc, lam, lamseg, *, layer, batch, seq, n_ctx):
    T = SSM_T
    J = u3.shape[0]
    ns = wc.shape[-1]
    nsets = 2 ** len(SSM_STAGES)
    gs = T // nsets
    sw = gs * LANES
    n_lat, n_c = seq // T, n_ctx // T
    rows = n_lat + 2 * n_c
    seg_len = rows // SSM_SEG
    assert rows % SSM_SEG == 0 and n_c % 16 == 0
    rexp, cexp = _ssm_expanders(gs)
    kern = functools.partial(_ssm_kernel, seg_len=seg_len, n_lat=n_lat, n_ctx=n_c, stages=SSM_STAGES)
    const = lambda j, b: (0, 0)
    per_j = lambda j, b: (layer, j, 0, 0)
    per_jb = lambda j, b: (j, b, 0)
    return pl.pallas_call(
        kern,
        grid=(J, batch),
        in_specs=[pl.BlockSpec((None, seq, LANES), per_jb, pipeline_mode=pl.Buffered(1)),
                  pl.BlockSpec((None, n_ctx, LANES), per_jb),
                  pl.BlockSpec((None, None) + wc.shape[2:], per_j),
                  pl.BlockSpec((None, None) + kc.shape[2:], per_j),
                  pl.BlockSpec((None, None) + dc.shape[2:], per_j),
                  pl.BlockSpec(rexp.shape, const),
                  pl.BlockSpec(cexp.shape, const),
                  pl.BlockSpec((None, None, SSM_SEG, ns), per_j),
                  pl.BlockSpec((None, None, 1, ns), per_j)],
        out_specs=[pl.BlockSpec((None, seq, LANES), per_jb),
                   pl.BlockSpec((None, n_ctx, LANES), per_jb)],
        out_shape=[jax.ShapeDtypeStruct(u3.shape, F32),
                   jax.ShapeDtypeStruct(u3c.shape, F32)],
        scratch_shapes=[pltpu.VMEM((nsets, rows, sw), BF),
                        pltpu.VMEM((nsets, sw, ns // nsets), BF),
                        pltpu.VMEM((nsets, sw, sw), BF),
                        pltpu.VMEM((nsets, ns // nsets, sw), BF),
                        pltpu.VMEM((ns // LANES, rows, LANES), F32),
                        pltpu.VMEM((ns // LANES, rows, LANES), F32),
                        pltpu.VMEM((ns // LANES, 2 * SSM_SEG, LANES), F32)],
        compiler_params=_cparams("arbitrary", "arbitrary"),
        name="ssm_chunk_scan",
    )(u3, u3c, wc, kc, dc, rexp, cexp, lam, lamseg)


def _dft_a_kernel(w_ref, x_ref, o_ref):
    o_ref[...] = jnp.dot(w_ref[...], x_ref[...], preferred_element_type=F32).astype(BF)


def _dft_rows(n):
    k = np.arange(n)
    m = (k[:, None] * k[None, :]) % n
    ang = m * (2.0 * math.pi / n)
    return jnp.asarray(np.concatenate([np.cos(ang), -np.sin(ang)], axis=0), BF)


def _dft_a(w, x3):
    B, n, cols = x3.shape
    tn = min(cols, 4096)
    return pl.pallas_call(
        _dft_a_kernel,
        grid=(B, cols // tn),
        in_specs=[pl.BlockSpec((2 * n, n), lambda b, i: (0, 0)),
                  pl.BlockSpec((None, n, tn), lambda b, i: (b, 0, i))],
        out_specs=pl.BlockSpec((None, 2 * n, tn), lambda b, i: (b, 0, i)),
        out_shape=jax.ShapeDtypeStruct((B, 2 * n, cols), BF),
        compiler_params=_cparams("parallel", "parallel"),
        name="dft_stage_a",
    )(w, x3)


def _store_lane_tiles(o_ref, p, n, wf, idx):
    per = wf // LANES
    for c in range(2 * per):
        part = p[(c // per) * n:(c // per + 1) * n, (c % per) * LANES:(c % per + 1) * LANES]
        o_ref[(c,) + idx] = part


def _dft_b_kernel(a_ref, g_ref, o_ref, *, kb, l1, wf):
    for i in range(kb):
        g = jnp.concatenate([g_ref[0, i], g_ref[1, i]], axis=0)
        p = jnp.dot(a_ref[i], g, preferred_element_type=F32)
        _store_lane_tiles(o_ref, p, l1, wf, (i,))


def _dft_b_mats(seq, l1, l2):
    k2 = np.arange(l2)[:, None, None]
    k1 = np.arange(l1)[None, :, None]
    j1 = np.arange(l1)[None, None, :]
    m = ((k1 * l2 + k2) * j1) % seq
    ang = m * (2.0 * math.pi / seq)
    ar, ai = np.cos(ang).astype(np.float32), -np.sin(ang).astype(np.float32)
    top = np.concatenate([ar, -ai], axis=2)
    bot = np.concatenate([ai, ar], axis=2)
    return jnp.asarray(np.concatenate([top, bot], axis=1), BF)


def _dft_b(amat, g5, *, kb):
    B, _, l2, l1, wf = g5.shape
    nlt = 2 * wf // LANES
    kern = functools.partial(_dft_b_kernel, kb=kb, l1=l1, wf=wf)
    return pl.pallas_call(
        kern,
        grid=(B, l2 // kb),
        in_specs=[pl.BlockSpec((kb, 2 * l1, 2 * l1), lambda b, i: (i, 0, 0)),
                  pl.BlockSpec((None, 2, kb, l1, wf), lambda b, i: (b, 0, i, 0, 0))],
        out_specs=pl.BlockSpec((None, nlt, kb, l1, LANES), lambda b, i: (b, 0, i, 0, 0)),
        out_shape=jax.ShapeDtypeStruct((B, nlt, l2, l1, LANES), F32),
        compiler_params=_cparams("parallel", "parallel"),
        name="dft_stage_b",
    )(amat, g5)


def _dft_ctx_kernel(w_ref, x_ref, o_ref, *, n, wf):
    p = jnp.dot(w_ref[...], x_ref[...], preferred_element_type=F32)
    _store_lane_tiles(o_ref, p, n, wf, ())


def _dft_ctx(w, x3):
    B, n, wf = x3.shape
    nlt = 2 * wf // LANES
    kern = functools.partial(_dft_ctx_kernel, n=n, wf=wf)
    return pl.pallas_call(
        kern,
        grid=(B,),
        in_specs=[pl.BlockSpec((2 * n, n), lambda b: (0, 0)),
                  pl.BlockSpec((None, n, wf), lambda b: (b, 0, 0))],
        out_specs=pl.BlockSpec((None, nlt, n, LANES), lambda b: (b, 0, 0, 0)),
        out_shape=jax.ShapeDtypeStruct((B, nlt, n, LANES), F32),
        compiler_params=_cparams("parallel"),
        name="dft_ctx",
    )(w, x3)


def _channel_dft(wf, seq):
    c = np.arange(wf)
    same = (c[:, None] // FNET_GROUP) == (c[None, :] // FNET_GROUP)
    m = ((c[:, None] % FNET_GROUP) * (c[None, :] % FNET_GROUP)) % FNET_GROUP
    ang = m * (2.0 * math.pi / FNET_GROUP)
    mask = same / math.sqrt(seq * FNET_GROUP)
    return jnp.asarray(np.concatenate([np.cos(ang) * mask, np.sin(ang) * mask], axis=0), BF)


def _softmax_pv(parts, l_shape):
    m = parts[0][0].max(axis=-1, keepdims=True)
    for s, _ in parts[1:]:
        m = jnp.maximum(m, s.max(axis=-1, keepdims=True))
    l = jnp.zeros(l_shape, F32)
    o = None
    for s, v in parts:
        p = jnp.exp2(s - m)
        l = l + p.sum(axis=-1, keepdims=True)
        pv = jnp.dot(p.astype(BF), v, preferred_element_type=F32)
        o = pv if o is None else o + pv
    return o / l


def _na_kernel(kb_ref, pid_ref, dr_ref, q_ref, k_ref, v_ref, kc_ref, vc_ref, cbx_ref, o_ref, bias_ref,
               *, nkeys, heads, nsub):
    step = pl.program_id(1)
    nq = q_ref.shape[0] // nsub
    rq, kr = nq // GRID_W, nkeys // GRID_W
    lane = lax.broadcasted_iota(jnp.int32, (nq, LANES), 1)
    low = lane < HEAD_DIM
    dn = (((1,), (1,)), ((), ()))

    for sb in range(nsub):
        i = step * nsub + sb
        pid = pid_ref[i]

        @pl.when((step == 0) | (pid != pid_ref[jnp.maximum(i - nsub, 0)]))
        def _build_bias(sb=sb, pid=pid):
            for qi in range(rq):
                for w in range(kr):
                    a = dr_ref[(pid * rq + qi) * kr + w]
                    for h in range(heads):
                        bias_ref[sb, h, qi * GRID_W:(qi + 1) * GRID_W, w * GRID_W:(w + 1) * GRID_W] = cbx_ref[h, a]

    tasks = [(sb, h) for sb in range(nsub) for h in range(heads)]
    starts = [pl.multiple_of(kb_ref[step * nsub + sb] * GRID_W, GRID_W) for sb in range(nsub)]

    def scores(sb, h):
        sl = slice((h // 2) * LANES, (h // 2 + 1) * LANES)
        q2 = q_ref[sb * nq:(sb + 1) * nq, sl]
        qm = jnp.where(low if h % 2 == 0 else jnp.logical_not(low), q2, jnp.zeros_like(q2))
        s_w = lax.dot_general(qm, k_ref[pl.ds(starts[sb], nkeys), sl], dn, preferred_element_type=F32)
        s_c = lax.dot_general(qm, kc_ref[:, sl], dn, preferred_element_type=F32)
        return s_w + bias_ref[sb, h], s_c

    def weights(s_w, s_c):
        m = jnp.maximum(s_w.max(axis=-1, keepdims=True), s_c.max(axis=-1, keepdims=True))
        p_w = jnp.exp2(s_w - m)
        p_c = jnp.exp2(s_c - m)
        l = p_w.sum(axis=-1, keepdims=True) + p_c.sum(axis=-1, keepdims=True)
        return p_w.astype(BF), p_c.astype(BF), l

    outs = {}

    def finish(sb, h, p_w, p_c, l):
        sl = slice((h // 2) * LANES, (h // 2 + 1) * LANES)
        o = (jnp.dot(p_w, v_ref[pl.ds(starts[sb], nkeys), sl], preferred_element_type=F32)
             + jnp.dot(p_c, vc_ref[:, sl], preferred_element_type=F32))
        outs[h % 2] = o / l
        if h % 2 == 1:
            o_ref[sb * nq:(sb + 1) * nq, sl] = jnp.where(low, outs[0], outs[1]).astype(BF)

    nt = len(tasks)
    sc = {0: scores(*tasks[0])}
    if nt > 1:
        sc[1] = scores(*tasks[1])
    pw = {0: weights(*sc.pop(0))}
    for n in range(nt):
        if n + 2 < nt:
            sc[n + 2] = scores(*tasks[n + 2])
        if n + 1 < nt:
            pw[n + 1] = weights(*sc.pop(n + 1))
        finish(*tasks[n], *pw.pop(n))


def _na_plan(rows, rq):
    wr = min(NA_ROWS, rows)
    kr = rq + wr - 1
    nblk = rows // rq
    kbs, pids, pats = [], [], []
    for blk in range(nblk):
        r0 = blk * rq
        rs = [int(np.clip(r0 + i - wr // 2, 0, rows - wr)) for i in range(rq)]
        kb = min(rs[0], rows - kr)
        pat = (tuple(r - kb for r in rs), r0 - kb)
        if pat not in pats:
            pats.append(pat)
        kbs.append(kb)
        pids.append(pats.index(pat))
    return np.asarray(kbs, np.int32), np.asarray(pids, np.int32), pats, kr, wr


def _na_bias(rpb, pats, rq, kr, wr):
    cols = np.arange(GRID_W)
    cstart = np.clip(cols - NA_COLS // 2, 0, GRID_W - NA_COLS)
    kc = np.arange(GRID_W)
    col_ok = (kc[None, :] >= cstart[:, None]) & (kc[None, :] < cstart[:, None] + NA_COLS)
    dc = kc[None, :] - cols[:, None] + NA_COLS - 1
    depth, heads, nr, ncol = rpb.shape
    sel = (dc[:, :, None] == np.arange(ncol)[None, None, :]) & col_ok[:, :, None]
    cb = jnp.einsum('lhab,ckb->lhack', rpb.astype(F32), jnp.asarray(sel, F32), precision=lax.Precision.HIGHEST)
    cb = jnp.where(jnp.asarray(col_ok)[None, None, None], cb * math.log2(math.e), NEG)
    cbx = jnp.concatenate([cb, jnp.full((depth, heads, 1, GRID_W, GRID_W), NEG, F32)], axis=2)
    slots = []
    for rs_off, r_off in pats:
        i = np.arange(rq)[:, None]
        w = np.arange(kr)[None, :]
        rso = np.asarray(rs_off)[:, None]
        row_ok = (w >= rso) & (w < rso + wr)
        slots.append(np.where(row_ok, w - r_off - i + NA_ROWS - 1, nr))
    return cbx, np.stack(slots).reshape(-1).astype(np.int32)


def _na_tables(rpb_all, seq, rq):
    kbs, pids, pats, kr, wr = _na_plan(seq // GRID_W, rq)
    cbx, slots = _na_bias(rpb_all, pats, rq, kr, wr)
    return dict(kbs=jnp.asarray(kbs), pids=jnp.asarray(pids), slots=jnp.asarray(slots), cbx=cbx, kr=kr, rq=rq)


def _na_latent(q, k, v, kc, vc, tabs, layer, *, batch, seq, n_ctx):
    rows = seq // GRID_W
    cbx, rq, kr = tabs["cbx"], tabs["rq"], tabs["kr"]
    heads = cbx.shape[1]
    nblk = rows // rq
    nq = rq * GRID_W
    nkeys = kr * GRID_W
    w = q.shape[1]
    nsub = 2 if nblk % 2 == 0 else 1
    nstep = nblk // nsub
    kern = functools.partial(_na_kernel, nkeys=nkeys, heads=heads, nsub=nsub)
    grid_spec = pltpu.PrefetchScalarGridSpec(
        num_scalar_prefetch=3,
        grid=(batch, nstep),
        in_specs=[pl.BlockSpec((nsub * nq, w), lambda b, i, *_: (b * nstep + i, 0)),
                  pl.BlockSpec((seq, w), lambda b, i, *_: (b, 0)),
                  pl.BlockSpec((seq, w), lambda b, i, *_: (b, 0)),
                  pl.BlockSpec((n_ctx, w), lambda b, i, *_: (b, 0)),
                  pl.BlockSpec((n_ctx, w), lambda b, i, *_: (b, 0)),
                  pl.BlockSpec((None,) + cbx.shape[1:], lambda b, i, *_: (layer, 0, 0, 0, 0))],
        out_specs=pl.BlockSpec((nsub * nq, w), lambda b, i, *_: (b * nstep + i, 0)),
        scratch_shapes=[pltpu.VMEM((nsub, heads, nq, nkeys), F32)],
    )
    return pl.pallas_call(
        kern,
        grid_spec=grid_spec,
        out_shape=jax.ShapeDtypeStruct(q.shape, BF),
        compiler_params=_cparams("arbitrary", "arbitrary"),
        name="na_latent",
    )(tabs["kbs"], tabs["pids"], tabs["slots"], q, k, v, kc, vc, cbx)


def _na_ctx_kernel(q_ref, k_ref, v_ref, o_ref, *, heads):
    nq = q_ref.shape[0]
    lane = lax.broadcasted_iota(jnp.int32, (nq, LANES), 1)
    low = lane < HEAD_DIM
    dn = (((1,), (1,)), ((), ()))
    for hp in range(heads // 2):
        sl = slice(hp * LANES, (hp + 1) * LANES)
        q2 = q_ref[:, sl]
        k2 = k_ref[:, sl]
        v2 = v_ref[:, sl]
        outs = []
        for hh in range(2):
            qm = jnp.where(low if hh == 0 else jnp.logical_not(low), q2, jnp.zeros_like(q2))
            s = lax.dot_general(qm, k2, dn, preferred_element_type=F32)
            outs.append(_softmax_pv([(s, v2)], (nq, 1)))
        o_ref[:, sl] = jnp.where(low, outs[0], outs[1]).astype(BF)


def _na_ctx(qc, kc, vc, *, batch, n_ctx, heads):
    w = qc.shape[1]
    spec = pl.BlockSpec((n_ctx, w), lambda b: (b, 0))
    return pl.pallas_call(
        functools.partial(_na_ctx_kernel, heads=heads),
        grid=(batch,),
        in_specs=[spec, spec, spec],
        out_specs=spec,
        out_shape=jax.ShapeDtypeStruct(qc.shape, BF),
        compiler_params=_cparams("parallel"),
        name="na_ctx",
    )(qc, kc, vc)


def _outproj_kernel(z_ref, pf_ref, na_ref, x_ref, gate_ref, gpost_ref,
                    wglu_ref, cs_ref, wfo_ref, wout_ref, o_ref, *, widths, sub):
    w_ssm, w_fn, w_na = widths
    for r0 in range(0, x_ref.shape[0], sub):
        rs = slice(r0, r0 + sub)
        z = jnp.concatenate([z_ref[j, rs, :] for j in range(w_ssm // LANES)], axis=1)
        gl = jnp.dot(z.astype(BF), wglu_ref[...], preferred_element_type=F32)
        y_ssm = (z * jax.nn.sigmoid(gl)).astype(BF)
        if len(pf_ref.shape) == 3:
            tiles = [pf_ref[c, rs, :] for c in range(pf_ref.shape[0])]
        else:
            l2 = pf_ref.shape[1]
            tiles = [jnp.concatenate([pf_ref[c, :, k1, :] for k1 in range(r0 // l2, (r0 + sub) // l2)], axis=0)
                     for c in range(pf_ref.shape[0])]
        pf = jnp.concatenate(tiles, axis=1).astype(BF)
        mixed = jnp.dot(pf, cs_ref[...], preferred_element_type=F32).astype(BF)
        y_fft = jnp.dot(mixed, wfo_ref[...], preferred_element_type=F32).astype(BF)
        y = jnp.concatenate([y_ssm, y_fft, na_ref[rs, :]], axis=1)
        o = jnp.dot(y, wout_ref[...], preferred_element_type=F32)
        ms = jnp.mean(o * o, axis=-1, keepdims=True)
        o_ref[rs, :] = x_ref[rs, :] + gate_ref[...] * (o * lax.rsqrt(ms + EPS) * gpost_ref[...])


def _outproj(z3, pf, yna, x2d, mod3, g_post, w_glu, cs, w_fo, w_out, *,
             layer, tm, tiles_per_batch, mod_row, widths):
    n, d = x2d.shape
    w_ssm, w_fn, w_na = widths
    if mod_row is None:
        row = lambda i: i // tiles_per_batch
    else:
        row = lambda i: mod_row
    kern = functools.partial(_outproj_kernel, widths=widths, sub=min(tm, 512))
    return pl.pallas_call(
        kern,
        grid=(n // tm,),
        in_specs=[pl.BlockSpec((w_ssm // LANES, tm, LANES), lambda i: (0, i, 0)),
                  (pl.BlockSpec((None, pf.shape[1], tm, LANES),
                                lambda i: (i // tiles_per_batch, 0, i % tiles_per_batch, 0))
                   if pf.ndim == 4 else
                   pl.BlockSpec((None, pf.shape[1], pf.shape[2], tm // pf.shape[2], LANES),
                                lambda i: (i // tiles_per_batch, 0, 0, i % tiles_per_batch, 0))),
                  pl.BlockSpec((tm, w_na), lambda i: (i, 0)),
                  pl.BlockSpec((tm, d), lambda i: (i, 0)),
                  pl.BlockSpec((None, 1, d), lambda i: (row(i), 0, 2)),
                  pl.BlockSpec((None, 1, d), lambda i: (layer, 0, 0)),
                  _layer_resident(w_glu, layer),
                  _resident(cs.shape, lambda i: (0, 0)),
                  _layer_resident(w_fo, layer),
                  _layer_resident(w_out, layer)],
        out_specs=pl.BlockSpec((tm, d), lambda i: (i, 0)),
        out_shape=jax.ShapeDtypeStruct((n, d), F32),
        compiler_params=_cparams("parallel"),
        name="outproj",
    )(z3, pf, yna, x2d, mod3, g_post.reshape(-1, 1, d), w_glu, cs, w_fo, w_out)


def _ffn_kernel(x_ref, sh_ref, sc_ref, gate_ref, gpre_ref, gpost_ref, wg_ref, wu_ref, wd_ref,
                o_ref, a_ref, *, sub, chunk):
    dff = wg_ref.shape[1]
    for r0 in range(0, x_ref.shape[0], sub):
        rs = slice(r0, r0 + sub)
        x = x_ref[rs, :]
        ms = jnp.mean(x * x, axis=-1, keepdims=True)
        m = ((x * lax.rsqrt(ms + EPS) * gpre_ref[...]) * (1.0 + sc_ref[...]) + sh_ref[...]).astype(BF)
        for c0 in range(0, dff, chunk):
            sl = slice(c0, min(c0 + chunk, dff))
            g = jnp.dot(m, wg_ref[:, sl], preferred_element_type=F32)
            u = jnp.dot(m, wu_ref[:, sl], preferred_element_type=F32)
            a_ref[rs, sl] = (g * jax.nn.sigmoid(g) * u).astype(BF)
        y = jnp.dot(a_ref[rs, :], wd_ref[...], preferred_element_type=F32)
        ms2 = jnp.mean(y * y, axis=-1, keepdims=True)
        o_ref[rs, :] = x + gate_ref[...] * (y * lax.rsqrt(ms2 + EPS) * gpost_ref[...])


def _ffn(x2d, mod3, g_pre, g_post, wg, wu, wd, *, layer, tm, tiles_per_batch, mod_row):
    n, d = x2d.shape
    dff = wg.shape[2]
    if mod_row is None:
        row = lambda i: i // tiles_per_batch
    else:
        row = lambda i: mod_row
    return pl.pallas_call(
        functools.partial(_ffn_kernel, sub=min(tm, 512), chunk=3 * MXU_TILE),
        grid=(n // tm,),
        in_specs=[pl.BlockSpec((tm, d), lambda i: (i, 0)),
                  pl.BlockSpec((None, 1, d), lambda i: (row(i), 0, 3)),
                  pl.BlockSpec((None, 1, d), lambda i: (row(i), 0, 4)),
                  pl.BlockSpec((None, 1, d), lambda i: (row(i), 0, 5)),
                  pl.BlockSpec((None, 1, d), lambda i: (layer, 0, 0)),
                  pl.BlockSpec((None, 1, d), lambda i: (layer, 0, 0)),
                  _layer_resident(wg, layer),
                  _layer_resident(wu, layer),
                  _layer_resident(wd, layer)],
        out_specs=pl.BlockSpec((tm, d), lambda i: (i, 0)),
        out_shape=jax.ShapeDtypeStruct((n, d), F32),
        scratch_shapes=[pltpu.VMEM((tm, dff), BF)],
        compiler_params=_cparams("parallel"),
        name="ffn",
    )(x2d, mod3, mod3, mod3, g_pre.reshape(-1, 1, d), g_post.reshape(-1, 1, d), wg, wu, wd)


def _layer(x2d, xc2d, p, w, layer, *, batch, seq, n_ctx, last, rope_tabs, consts):
    d = x2d.shape[1]
    w_ssm = consts["w_ssm"]
    w_fn = w["w_fourier"].shape[1]
    w_na = (w["w_in"].shape[2] - w_ssm - w_fn) // 3
    widths = (w_ssm, w_fn, w_na)
    heads = w_na // HEAD_DIM
    tm = min(512, seq)
    tmc = min(512, batch * n_ctx)

    mod = w["mod"][layer]
    mod3 = mod.reshape(mod.shape[0], 1, 6 * d)

    cos, sin = rope_tabs
    tm2 = min(2 * tm, seq)
    u3, f, q, k, v = _inproj(x2d, mod3, p["g_pre_mix"], w["w_in"], cos, sin, layer=layer, tm=tm2,
                             tiles_per_batch=seq // tm2, mod_row=None, widths=widths, rope=True)
    ones = jnp.ones((tmc, LANES), F32)
    u3c, fc, qc, kc, vc = _inproj(xc2d, mod3, p["g_pre_mix"], w["w_in"], ones, ones, layer=layer, tm=tmc,
                                  tiles_per_batch=1, mod_row=batch, widths=widths, rope=False)

    z3, zc3 = _ssm(u3, u3c, *w["ssm"], layer=layer, batch=batch, seq=seq, n_ctx=n_ctx)

    l2 = consts["l2"]
    l1 = seq // l2
    g = _dft_a(consts["dft_a"], f.reshape(batch, l2, l1 * w_fn))
    pf = _dft_b(consts["dft_b"], g.reshape(batch, 2, l2, l1, w_fn), kb=min(8, l2))

    yna = _na_latent(q, k, v, kc, vc, consts["na"], layer, batch=batch, seq=seq, n_ctx=n_ctx)

    w_glu, w_fo, w_out = w["w_glu"], w["w_fourier"], w["w_out"]
    x2d = _outproj(z3, pf, yna, x2d, mod3, p["g_post_mix"], w_glu, consts["cs_lat"], w_fo, w_out,
                   layer=layer, tm=tm2, tiles_per_batch=seq // tm2, mod_row=None, widths=widths)
    wg, wu, wd = w["w_ffn_gate"], w["w_ffn_up"], w["w_ffn_down"]
    x2d = _ffn(x2d, mod3, p["g_pre_ffn"], p["g_post_ffn"], wg, wu, wd,
               layer=layer, tm=tm2, tiles_per_batch=seq // tm2, mod_row=None)

    if not last:
        pfc = _dft_ctx(consts["dft_ctx"], fc.reshape(batch, n_ctx, w_fn))
        ynac = _na_ctx(qc, kc, vc, batch=batch, n_ctx=n_ctx, heads=heads)
        tc = min(tmc, n_ctx)
        xc2d = _outproj(zc3, pfc, ynac, xc2d, mod3, p["g_post_mix"], w_glu, consts["cs_ctx"], w_fo, w_out,
                        layer=layer, tm=tc, tiles_per_batch=n_ctx // tc, mod_row=batch, widths=widths)
        xc2d = _ffn(xc2d, mod3, p["g_pre_ffn"], p["g_post_ffn"], wg, wu, wd,
                    layer=layer, tm=tc, tiles_per_batch=n_ctx // tc, mod_row=batch)
    else:
        xc2d = None
    return x2d, xc2d


def kernel(x, c, ctx, c_ctx, w_mod, b_mod, g_pre_mix, g_post_mix, w_in, ssm_a_re, ssm_a_im, ssm_log_dt,
           ssm_b_re, ssm_b_im, ssm_c_re, ssm_c_im, ssm_d, w_glu, w_fourier, na_rpb, w_out, g_pre_ffn,
           g_post_ffn, w_ffn_gate, w_ffn_up, w_ffn_down):
    batch, seq, d = x.shape
    n_ctx = ctx.shape[1]
    depth = w_mod.shape[0]
    w_fn = w_fourier.shape[1]
    params = dict(g_pre_mix=g_pre_mix, g_post_mix=g_post_mix, g_pre_ffn=g_pre_ffn, g_post_ffn=g_post_ffn)
    seg_len = (seq + 2 * n_ctx) // SSM_T // SSM_SEG
    ssm_ops = jax.vmap(functools.partial(_ssm_weights, seg_len=seg_len))(
        ssm_a_re, ssm_a_im, ssm_log_dt, ssm_b_re, ssm_b_im, ssm_c_re, ssm_c_im, ssm_d)
    nrow = -(-(batch + 1) // 8) * 8
    cstack = jnp.concatenate([c, c_ctx[None, :], jnp.zeros((nrow - batch - 1, d), c.dtype)], axis=0)
    weights = dict(mod=_mod_rows(cstack, w_mod, b_mod), ssm=ssm_ops,
                   w_in=_cast_bf16(w_in), w_glu=_cast_bf16(w_glu), w_fourier=_cast_bf16(w_fourier),
                   w_out=_cast_bf16(w_out), w_ffn_gate=_cast_bf16(w_ffn_gate),
                   w_ffn_up=_cast_bf16(w_ffn_up), w_ffn_down=_cast_bf16(w_ffn_down))

    l2 = 64 if seq % (64 * 8) == 0 else 8
    l1 = seq // l2
    consts = dict(
        l2=l2,
        w_ssm=ssm_d.shape[1],
        na=_na_tables(na_rpb, seq, 4),
        dft_a=_dft_rows(l2).astype(BF),
        dft_b=_dft_b_mats(seq, l1, l2).astype(BF),
        dft_ctx=_dft_rows(n_ctx).astype(BF),
        cs_lat=_channel_dft(w_fn, seq).astype(BF),
        cs_ctx=_channel_dft(w_fn, n_ctx).astype(BF),
    )
    rope_tabs = _rope_tables(seq)

    x2d = x.reshape(batch * seq, d)
    xc2d = ctx.reshape(batch * n_ctx, d)
    for layer in range(depth):
        x2d, xc2d = _layer(x2d, xc2d, params, weights, layer, batch=batch, seq=seq, n_ctx=n_ctx,
                           last=(layer == depth - 1), rope_tabs=rope_tabs, consts=consts)
    return x2d.reshape(batch, seq, d)
```

```python
import functools
import math

import numpy as np
import jax
import jax.numpy as jnp
from jax import lax
from jax.experimental import pallas as pl
from jax.experimental.pallas import tpu as pltpu

BF = jnp.bfloat16
F32 = jnp.float32

EPS = 1e-6
GRID_W = 64
HEAD_DIM = 64
NA_ROWS = 8
NA_COLS = 16
ROPE_BASE = 10000.0
SSM_GROUP = 16
SSM_STATE = 64
FNET_GROUP = 64
LANES = 128
MXU_TILE = 256
SSM_T = 8
SSM_SEG = 8
SSM_STAGES = (4, 2)
NEG = -1e30
VMEM_LIMIT = 56 * 1024 * 1024


def _cparams(*sem):
    return pltpu.CompilerParams(dimension_semantics=sem, vmem_limit_bytes=VMEM_LIMIT)


def _resident(shape, index_map):
    return pl.BlockSpec(shape, index_map, pipeline_mode=pl.Buffered(1))


def _layer_resident(w_all, layer):
    return pl.BlockSpec((None,) + w_all.shape[1:], lambda *_: (layer, 0, 0), pipeline_mode=pl.Buffered(1))


def _cast_kernel(w_ref, o_ref):
    o_ref[...] = w_ref[...].astype(BF)


def _cast_bf16(w_all):
    depth, rows, cols = w_all.shape
    br = rows
    while br * cols * 4 > 4 * 1024 * 1024 and br % 32 == 0:
        br //= 2
    spec = pl.BlockSpec((None, br, cols), lambda l, i: (l, i, 0))
    return pl.pallas_call(
        _cast_kernel,
        grid=(depth, rows // br),
        in_specs=[spec],
        out_specs=spec,
        out_shape=jax.ShapeDtypeStruct(w_all.shape, BF),
        compiler_params=_cparams("parallel", "parallel"),
        name="cast_bf16",
    )(w_all)


def _mod_kernel(c_ref, w_ref, b_ref, o_ref):
    cs = c_ref[...]
    s = (cs * jax.nn.sigmoid(cs)).astype(BF)
    o_ref[...] = jnp.dot(s, w_ref[...].astype(BF), preferred_element_type=F32) + b_ref[...]


def _mod_rows(cstack, w_mod, b_mod):
    rows, d = cstack.shape
    depth, _, n = w_mod.shape
    bn = 2 * d if n % (2 * d) == 0 else d
    return pl.pallas_call(
        _mod_kernel,
        grid=(depth, n // bn),
        in_specs=[pl.BlockSpec((rows, d), lambda l, i: (0, 0)),
                  pl.BlockSpec((None, d, bn), lambda l, i: (l, 0, i)),
                  pl.BlockSpec((None, 1, bn), lambda l, i: (l, 0, i))],
        out_specs=pl.BlockSpec((None, rows, bn), lambda l, i: (l, 0, i)),
        out_shape=jax.ShapeDtypeStruct((depth, rows, n), F32),
        compiler_params=_cparams("parallel", "parallel"),
        name="mod_rows",
    )(cstack, w_mod, b_mod.reshape(depth, 1, n))


def _rope_tile(t, cos, sin, first):
    partner = jnp.where(first, pltpu.roll(t, LANES - 16, 1), pltpu.roll(t, 16, 1))
    return t * cos + partner * sin


def _inproj_kernel(x_ref, sh_ref, sc_ref, g_ref, w_ref, cos_ref, sin_ref,
                   u_ref, f_ref, q_ref, k_ref, v_ref, *, widths, rope, sub):
    w_ssm, w_fn, w_na = widths
    scale = HEAD_DIM ** -0.5 * math.log2(math.e)
    for r0 in range(0, x_ref.shape[0], sub):
        rs = slice(r0, r0 + sub)
        x = x_ref[rs, :]
        ms = jnp.mean(x * x, axis=-1, keepdims=True)
        xn = x * lax.rsqrt(ms + EPS) * g_ref[...]
        m = (xn * (1.0 + sc_ref[...]) + sh_ref[...]).astype(BF)

        h = jnp.dot(m, w_ref[...], preferred_element_type=F32)

        def proj(lo, n, h=h):
            return h[:, lo:lo + n]

        u = proj(0, w_ssm)
        for j in range(w_ssm // LANES):
            u_ref[j, rs, :] = u[:, j * LANES:(j + 1) * LANES]
        f_ref[rs, :] = proj(w_ssm, w_fn).astype(BF)
        q = proj(w_ssm + w_fn, w_na)
        k = proj(w_ssm + w_fn + w_na, w_na)
        v_ref[rs, :] = proj(w_ssm + w_fn + 2 * w_na, w_na).astype(BF)
        if rope:
            cos = cos_ref[rs, :]
            sin = sin_ref[rs, :]
            lane = lax.broadcasted_iota(jnp.int32, cos.shape, 1)
            first = (lane % 32) < 16
            for j in range(w_na // LANES):
                sl = slice(j * LANES, (j + 1) * LANES)
                q_ref[rs, sl] = (_rope_tile(q[:, sl], cos, sin, first) * scale).astype(BF)
                k_ref[rs, sl] = _rope_tile(k[:, sl], cos, sin, first).astype(BF)
        else:
            q_ref[rs, :] = (q * scale).astype(BF)
            k_ref[rs, :] = k.astype(BF)


def _inproj(x2d, mod3, g_pre, w_in, cos, sin, *, layer, tm, tiles_per_batch, mod_row, widths, rope):
    n, d = x2d.shape
    w_ssm, w_fn, w_na = widths
    nt = n // tm
    if mod_row is None:
        row = lambda i: i // tiles_per_batch
    else:
        row = lambda i: mod_row
    pos = lambda i: (i % tiles_per_batch, 0)
    kern = functools.partial(_inproj_kernel, widths=widths, rope=rope, sub=min(tm, 512))
    return pl.pallas_call(
        kern,
        grid=(nt,),
        in_specs=[pl.BlockSpec((tm, d), lambda i: (i, 0)),
                  pl.BlockSpec((None, 1, d), lambda i: (row(i), 0, 0)),
                  pl.BlockSpec((None, 1, d), lambda i: (row(i), 0, 1)),
                  pl.BlockSpec((None, 1, d), lambda i: (layer, 0, 0)),
                  _layer_resident(w_in, layer),
                  pl.BlockSpec((tm, LANES), pos),
                  pl.BlockSpec((tm, LANES), pos)],
        out_specs=[pl.BlockSpec((w_ssm // LANES, tm, LANES), lambda i: (0, i, 0)),
                   pl.BlockSpec((tm, w_fn), lambda i: (i, 0)),
                   pl.BlockSpec((tm, w_na), lambda i: (i, 0)),
                   pl.BlockSpec((tm, w_na), lambda i: (i, 0)),
                   pl.BlockSpec((tm, w_na), lambda i: (i, 0))],
        out_shape=[jax.ShapeDtypeStruct((w_ssm // LANES, n, LANES), F32),
                   jax.ShapeDtypeStruct((n, w_fn), BF),
                   jax.ShapeDtypeStruct((n, w_na), BF),
                   jax.ShapeDtypeStruct((n, w_na), BF),
                   jax.ShapeDtypeStruct((n, w_na), BF)],
        compiler_params=_cparams("parallel"),
        name="inproj_rope" if rope else "inproj_ctx",
    )(x2d, mod3, mod3, g_pre.reshape(-1, 1, d), w_in, cos, sin)


def _rope_tables(seq):
    t = np.arange(seq)
    row = (t // GRID_W).astype(np.float32)
    col = (t % GRID_W).astype(np.float32)
    quarter = HEAD_DIM // 4
    freqs = (np.float32(ROPE_BASE) ** (-np.arange(quarter, dtype=np.float32) / quarter)).astype(np.float32)
    d = np.arange(LANES) % HEAD_DIM
    use_col = (d // (HEAD_DIM // 2)) == 1
    fidx = d % quarter
    sign = np.where((d % (HEAD_DIM // 2)) // quarter == 0, -1.0, 1.0)
    pos = np.where(use_col[None, :], col[:, None], row[:, None])
    ang = (pos * freqs[fidx][None, :]).astype(np.float32).astype(np.float64)
    return jnp.asarray(np.cos(ang), F32), jnp.asarray(np.sin(ang) * sign[None, :], F32)


def _gelu_tanh(x):
    c = math.sqrt(2.0 / math.pi)
    return x * (0.5 * (1.0 + jnp.tanh(c * (x + 0.044715 * (x * x * x)))))


def _group_of(idx, width, groups):
    shift = width.bit_length() - 1
    assert width == 1 << shift and groups & (groups - 1) == 0
    return lax.bitwise_and(lax.shift_right_logical(idx, shift), groups - 1)


def _same_group(shape, row0, row_width, col0, col_width, groups):
    r = lax.broadcasted_iota(jnp.int32, shape, 0) + row0
    c = lax.broadcasted_iota(jnp.int32, shape, 1) + col0
    return _group_of(r, row_width, groups) == _group_of(c, col_width, groups)


def _swap_stage(tiles, d):
    lane = lax.broadcasted_iota(jnp.int32, tiles[0].shape, 1)
    upper = lax.bitwise_and(lax.shift_right_logical(lane, SSM_GROUP.bit_length() - 1), d) != 0
    out = list(tiles)
    for x in range(len(tiles)):
        if x & d:
            continue
        a, b = tiles[x], tiles[x + d]
        out[x] = jnp.where(upper, pltpu.roll(b, SSM_GROUP * d, 1), a)
        out[x + d] = jnp.where(upper, b, pltpu.roll(a, LANES - SSM_GROUP * d, 1))
    return out


def _ssm_kernel(ul_ref, uc_ref, wc_ref, lag_ref, dc_ref, rexp_ref, cexp_ref, lam_ref, lamseg_ref,
                z_ref, zc_ref, u_s, bm_s, kin_s, cm_s, s_ref, h_ref, e_ref, *, seg_len, n_lat, n_ctx, stages):
    T = SSM_T
    ns = 2 ** len(stages)
    gs = T // ns
    sw = gs * LANES
    rows = u_s.shape[1]
    tps = s_ref.shape[0] // ns
    tpq = tps // 4
    nst = tpq * LANES
    gpt = LANES // SSM_GROUP
    per_q = gpt * SSM_STATE

    def lanes(c):
        return slice(c * LANES, (c + 1) * LANES)

    def col0(s, q):
        return q * per_q + s * nst

    def swapped(tiles, order):
        for d in order:
            tiles = _swap_stage(tiles, d)
        return tiles

    @pl.when(pl.program_id(1) == 0)
    def _expand_operators():
        rexp = rexp_ref[...]
        cexp = cexp_ref[...]
        for s in range(ns):
            wsel = jnp.concatenate([wc_ref[:, col0(s, q):col0(s, q) + nst] for q in range(4)], axis=1)
            blk = jnp.dot(rexp, wsel, preferred_element_type=F32)
            keep = _same_group(blk.shape, 0, SSM_GROUP, 0, SSM_STATE, gs)
            bm_s[s] = jnp.where(keep, blk, 0.0).astype(BF)
            dsel = jnp.concatenate([dc_ref[col0(s, q):col0(s, q) + nst, :] for q in range(4)], axis=0)
            blk = jnp.dot(dsel, cexp, preferred_element_type=F32)
            keep = _same_group(blk.shape, 0, SSM_STATE, 0, SSM_GROUP, gs)
            cm_s[s] = jnp.where(keep, blk, 0.0).astype(BF)
        r16 = lax.broadcasted_iota(jnp.int32, (LANES, SSM_GROUP), 0)
        c16 = lax.broadcasted_iota(jnp.int32, (LANES, SSM_GROUP), 1)
        chan = jnp.where(lax.bitwise_and(r16, SSM_GROUP - 1) == c16, 1.0, 0.0).astype(BF)
        for t in range(T):
            lo = (T - 1 - t) * LANES
            blk = jnp.dot(chan, lag_ref[:, lo:lo + T * LANES], preferred_element_type=F32)
            keep = _same_group(blk.shape, 0, SSM_GROUP, 0, SSM_GROUP, gpt)
            blk = jnp.where(keep, blk, 0.0)
            tiles = swapped([blk[:, lanes(x)] for x in range(T)], stages)
            r0 = (t % gs) * LANES + (t // gs) * gs * SSM_GROUP
            for s in range(ns):
                part = jnp.concatenate(tiles[s * gs:(s + 1) * gs], axis=1)
                kin_s[s, r0:r0 + gs * SSM_GROUP, :] = part[s * gs * SSM_GROUP:(s + 1) * gs * SSM_GROUP, :].astype(BF)

    xs = []
    for t in range(T):
        ct = uc_ref[pl.ds(t, n_ctx, stride=T), :]
        xs.append(jnp.concatenate([ct, ul_ref[pl.ds(t, n_lat, stride=T), :], ct], axis=0))
    xs = swapped(xs, stages)
    for s in range(ns):
        for k in range(gs):
            u_s[s, :, lanes(k)] = xs[s * gs + k].astype(BF)
    for s in range(ns):
        r = jnp.dot(u_s[s], bm_s[s], preferred_element_type=F32)
        for c in range(tps):
            s_ref[s * tps + c] = r[:, lanes(c)]

    chains = []
    for s in range(ns):
        for d in range(2):
            for k in range(tpq):
                cr, ci = s * tps + 2 * d * tpq + k, s * tps + (2 * d + 1) * tpq + k
                lr = slice(col0(s, 2 * d) + k * LANES, col0(s, 2 * d) + (k + 1) * LANES)
                li = slice(col0(s, 2 * d + 1) + k * LANES, col0(s, 2 * d + 1) + (k + 1) * LANES)
                chains.append((cr, ci, d, lr, li))
    nt = ns * tps

    def scan_step(i, carry, record):
        new = list(carry)
        for cr, ci, d, lr_sl, li_sl in chains:
            rsl = pl.ds(i if d == 0 else seg_len - 1 - i, SSM_SEG, stride=seg_len)
            hr, hi = carry[cr], carry[ci]
            if record:
                h_ref[cr, rsl, :] = hr
                h_ref[ci, rsl, :] = hi
            lr, li = lam_ref[:, lr_sl], lam_ref[:, li_sl]
            new[cr] = lr * hr - li * hi + s_ref[cr, rsl, :]
            new[ci] = lr * hi + li * hr + s_ref[ci, rsl, :]
        return tuple(new)

    zero = jnp.zeros((SSM_SEG, LANES), F32)
    fin = lax.fori_loop(0, seg_len, lambda i, c: scan_step(i, c, False), (zero,) * nt, unroll=4)
    for c in range(nt):
        e_ref[c, 0:SSM_SEG, :] = fin[c]

    for cr, ci, d, lr_sl, li_sl in chains:
        lr, li = lamseg_ref[:, lr_sl], lamseg_ref[:, li_sl]
        er = jnp.zeros((1, LANES), F32)
        ei = jnp.zeros((1, LANES), F32)
        for n in range(SSM_SEG):
            g = n if d == 0 else SSM_SEG - 1 - n
            e_ref[cr, SSM_SEG + g:SSM_SEG + g + 1, :] = er
            e_ref[ci, SSM_SEG + g:SSM_SEG + g + 1, :] = ei
            fr, fi = e_ref[cr, g:g + 1, :], e_ref[ci, g:g + 1, :]
            er, ei = lr * er - li * ei + fr, lr * ei + li * er + fi

    ent = tuple(e_ref[c, SSM_SEG:2 * SSM_SEG, :] for c in range(nt))
    lax.fori_loop(0, seg_len, lambda i, c: scan_step(i, c, True), ent, unroll=2)

    ys, ycs = [], []
    for s in range(ns):
        hcat = jnp.concatenate([h_ref[s * tps + c].astype(BF) for c in range(tps)], axis=1)
        y = _gelu_tanh(jnp.dot(u_s[s], kin_s[s], preferred_element_type=F32)
                       + jnp.dot(hcat, cm_s[s], preferred_element_type=F32))
        ys += [y[:, lanes(k)] for k in range(gs)]
        hc = jnp.concatenate(
            [h_ref[s * tps + c, 0:n_ctx, :].astype(BF) for c in range(2 * tpq)]
            + [h_ref[s * tps + c, rows - n_ctx:rows, :].astype(BF) for c in range(2 * tpq, tps)], axis=1)
        yc = _gelu_tanh(jnp.dot(u_s[s, 0:n_ctx, :], kin_s[s], preferred_element_type=F32)
                        + jnp.dot(hc, cm_s[s], preferred_element_type=F32))
        ycs += [yc[:, lanes(k)] for k in range(gs)]
    ys = swapped(ys, stages[::-1])
    ycs = swapped(ycs, stages[::-1])
    for t in range(T):
        z_ref[pl.ds(t, n_lat, stride=T), :] = ys[t][n_ctx:n_ctx + n_lat, :]
        zc_ref[pl.ds(t, n_ctx, stride=T), :] = ycs[t]


def _ssm_weights(a_re, a_im, log_dt, b_re, b_im, c_re, c_im, d_skip, seg_len):
    T = SSM_T
    g, p = a_re.shape[1], a_re.shape[2]
    hc = b_re.shape[-1]
    gpt = LANES // hc
    J = g // gpt
    lam = lax.complex(a_re.astype(F32), a_im.astype(F32))
    dt = jnp.exp(log_dt.astype(F32))[..., None]
    ldt = lam * dt
    lam_bar = jnp.exp(ldt)
    bbar = ((lam_bar - 1) / lam)[..., None] * lax.complex(b_re.astype(F32), b_im.astype(F32))
    cmat = lax.complex(c_re.astype(F32), c_im.astype(F32))
    kk = jnp.arange(T + 1, dtype=F32)
    pw = jnp.exp(ldt[:, None] * kk[None, :, None, None])

    wf = pw[0, :T][::-1][:, :, None, :] * jnp.swapaxes(bbar[0], 1, 2)[None]
    wb = pw[1, :T][:, :, None, :] * jnp.swapaxes(bbar[1], 1, 2)[None]
    wq = jnp.stack([wf.real, wf.imag, wb.real, wb.imag]).reshape(4, T, J, gpt, hc, p)
    wc = wq.transpose(2, 1, 4, 0, 3, 5).reshape(J, T * hc, 4 * gpt * p)

    df = cmat[0][None] * pw[0, 1:][:, :, None, :]
    db = cmat[1][None] * pw[1, 1:][::-1][:, :, None, :]
    dq = jnp.stack([df.real, -df.imag, db.real, -db.imag]).reshape(4, T, J, gpt, hc, p)
    dc = dq.transpose(2, 0, 3, 5, 1, 4).reshape(J, 4 * gpt * p, T * hc)

    mf = jnp.einsum('gcp,kgp,gph->khgc', cmat[0], pw[0, :T], bbar[0]).real
    mb = jnp.einsum('gcp,kgp,gph->khgc', cmat[1], pw[1, :T], bbar[1]).real
    skip = jnp.eye(hc, dtype=F32)[:, None, :] * d_skip.astype(F32).reshape(g, hc)[None]
    m0 = mf[0] + mb[0] + skip
    lags = jnp.concatenate([mb[1:][::-1], m0[None], mf[1:]], axis=0)
    kc = lags.reshape(2 * T - 1, hc, J, LANES).transpose(2, 1, 0, 3).reshape(J, hc, (2 * T - 1) * LANES)

    def lam_rows(power):
        lp = jnp.exp(ldt * power)
        lt = lp.reshape(2, J, gpt * p)
        return jnp.concatenate([lt[0].real, lt[0].imag, lt[1].real, lt[1].imag], axis=-1)[:, None, :]

    lam_t = jnp.broadcast_to(lam_rows(float(T)), (J, SSM_SEG, 4 * gpt * p))
    return wc.astype(BF), kc.astype(BF), dc.astype(BF), lam_t, lam_rows(float(T * seg_len))


def _ssm_expanders(gs):
    r = np.arange(gs * LANES)
    k, slot, ch = r // LANES, (r % LANES) // SSM_GROUP, r % SSM_GROUP
    t = (slot // gs) * gs + k
    rexp = np.zeros((gs * LANES, SSM_T * SSM_GROUP), np.float32)
    rexp[r, t * SSM_GROUP + ch] = 1.0
    return jnp.asarray(rexp, BF), jnp.asarray(rexp.T, BF)


def _ssm(u3, u3c, wc, kc, dc, lam, lamseg, *, layer, batch, seq, n_ctx):
    T = SSM_T
    J = u3.shape[0]
    ns = wc.shape[-1]
    nsets = 2 ** len(SSM_STAGES)
    gs = T // nsets
    sw = gs * LANES
    n_lat, n_c = seq // T, n_ctx // T
    rows = n_lat + 2 * n_c
    seg_len = rows // SSM_SEG
    assert rows % SSM_SEG == 0 and n_c % 16 == 0
    rexp, cexp = _ssm_expanders(gs)
    kern = functools.partial(_ssm_kernel, seg_len=seg_len, n_lat=n_lat, n_ctx=n_c, stages=SSM_STAGES)
    const = lambda j, b: (0, 0)
    per_j = lambda j, b: (layer, j, 0, 0)
    per_jb = lambda j, b: (j, b, 0)
    return pl.pallas_call(
        kern,
        grid=(J, batch),
        in_specs=[pl.BlockSpec((None, seq, LANES), per_jb),
                  pl.BlockSpec((None, n_ctx, LANES), per_jb),
                  pl.BlockSpec((None, None) + wc.shape[2:], per_j),
                  pl.BlockSpec((None, None) + kc.shape[2:], per_j),
                  pl.BlockSpec((None, None) + dc.shape[2:], per_j),
                  pl.BlockSpec(rexp.shape, const),
                  pl.BlockSpec(cexp.shape, const),
                  pl.BlockSpec((None, None, SSM_SEG, ns), per_j),
                  pl.BlockSpec((None, None, 1, ns), per_j)],
        out_specs=[pl.BlockSpec((None, seq, LANES), per_jb),
                   pl.BlockSpec((None, n_ctx, LANES), per_jb)],
        out_shape=[jax.ShapeDtypeStruct(u3.shape, F32),
                   jax.ShapeDtypeStruct(u3c.shape, F32)],
        scratch_shapes=[pltpu.VMEM((nsets, rows, sw), BF),
                        pltpu.VMEM((nsets, sw, ns // nsets), BF),
                        pltpu.VMEM((nsets, sw, sw), BF),
                        pltpu.VMEM((nsets, ns // nsets, sw), BF),
                        pltpu.VMEM((ns // LANES, rows, LANES), F32),
                        pltpu.VMEM((ns // LANES, rows, LANES), F32),
                        pltpu.VMEM((ns // LANES, 2 * SSM_SEG, LANES), F32)],
        compiler_params=_cparams("arbitrary", "arbitrary"),
        name="ssm_chunk_scan",
    )(u3, u3c, wc, kc, dc, rexp, cexp, lam, lamseg)


def _dft_a_kernel(w_ref, x_ref, o_ref):
    o_ref[...] = jnp.dot(w_ref[...], x_ref[...], preferred_element_type=F32).astype(BF)


def _dft_rows(n):
    k = np.arange(n)
    m = (k[:, None] * k[None, :]) % n
    ang = m * (2.0 * math.pi / n)
    return jnp.asarray(np.concatenate([np.cos(ang), -np.sin(ang)], axis=0), BF)


def _dft_a(w, x3):
    B, n, cols = x3.shape
    tn = min(cols, 4096)
    return pl.pallas_call(
        _dft_a_kernel,
        grid=(B, cols // tn),
        in_specs=[pl.BlockSpec((2 * n, n), lambda b, i: (0, 0)),
                  pl.BlockSpec((None, n, tn), lambda b, i: (b, 0, i))],
        out_specs=pl.BlockSpec((None, 2 * n, tn), lambda b, i: (b, 0, i)),
        out_shape=jax.ShapeDtypeStruct((B, 2 * n, cols), BF),
        compiler_params=_cparams("parallel", "parallel"),
        name="dft_stage_a",
    )(w, x3)


def _store_lane_tiles(o_ref, p, n, wf, idx):
    per = wf // LANES
    for c in range(2 * per):
        part = p[(c // per) * n:(c // per + 1) * n, (c % per) * LANES:(c % per + 1) * LANES]
        o_ref[(c,) + idx] = part


def _dft_b_kernel(a_ref, g_ref, o_ref, *, kb, l1, wf):
    for i in range(kb):
        g = jnp.concatenate([g_ref[0, i], g_ref[1, i]], axis=0)
        p = jnp.dot(a_ref[i], g, preferred_element_type=F32)
        _store_lane_tiles(o_ref, p, l1, wf, (i,))


def _dft_b_mats(seq, l1, l2):
    k2 = np.arange(l2)[:, None, None]
    k1 = np.arange(l1)[None, :, None]
    j1 = np.arange(l1)[None, None, :]
    m = ((k1 * l2 + k2) * j1) % seq
    ang = m * (2.0 * math.pi / seq)
    ar, ai = np.cos(ang).astype(np.float32), -np.sin(ang).astype(np.float32)
    top = np.concatenate([ar, -ai], axis=2)
    bot = np.concatenate([ai, ar], axis=2)
    return jnp.asarray(np.concatenate([top, bot], axis=1), BF)


def _dft_b(amat, g5, *, kb):
    B, _, l2, l1, wf = g5.shape
    nlt = 2 * wf // LANES
    kern = functools.partial(_dft_b_kernel, kb=kb, l1=l1, wf=wf)
    return pl.pallas_call(
        kern,
        grid=(B, l2 // kb),
        in_specs=[pl.BlockSpec((kb, 2 * l1, 2 * l1), lambda b, i: (i, 0, 0)),
                  pl.BlockSpec((None, 2, kb, l1, wf), lambda b, i: (b, 0, i, 0, 0))],
        out_specs=pl.BlockSpec((None, nlt, kb, l1, LANES), lambda b, i: (b, 0, i, 0, 0)),
        out_shape=jax.ShapeDtypeStruct((B, nlt, l2, l1, LANES), F32),
        compiler_params=_cparams("parallel", "parallel"),
        name="dft_stage_b",
    )(amat, g5)


def _dft_ctx_kernel(w_ref, x_ref, o_ref, *, n, wf):
    p = jnp.dot(w_ref[...], x_ref[...], preferred_element_type=F32)
    _store_lane_tiles(o_ref, p, n, wf, ())


def _dft_ctx(w, x3):
    B, n, wf = x3.shape
    nlt = 2 * wf // LANES
    kern = functools.partial(_dft_ctx_kernel, n=n, wf=wf)
    return pl.pallas_call(
        kern,
        grid=(B,),
        in_specs=[pl.BlockSpec((2 * n, n), lambda b: (0, 0)),
                  pl.BlockSpec((None, n, wf), lambda b: (b, 0, 0))],
        out_specs=pl.BlockSpec((None, nlt, n, LANES), lambda b: (b, 0, 0, 0)),
        out_shape=jax.ShapeDtypeStruct((B, nlt, n, LANES), F32),
        compiler_params=_cparams("parallel"),
        name="dft_ctx",
    )(w, x3)


def _channel_dft(wf, seq):
    c = np.arange(wf)
    same = (c[:, None] // FNET_GROUP) == (c[None, :] // FNET_GROUP)
    m = ((c[:, None] % FNET_GROUP) * (c[None, :] % FNET_GROUP)) % FNET_GROUP
    ang = m * (2.0 * math.pi / FNET_GROUP)
    mask = same / math.sqrt(seq * FNET_GROUP)
    return jnp.asarray(np.concatenate([np.cos(ang) * mask, np.sin(ang) * mask], axis=0), BF)


def _softmax_pv(parts, l_shape):
    m = parts[0][0].max(axis=-1, keepdims=True)
    for s, _ in parts[1:]:
        m = jnp.maximum(m, s.max(axis=-1, keepdims=True))
    l = jnp.zeros(l_shape, F32)
    o = None
    for s, v in parts:
        p = jnp.exp2(s - m)
        l = l + p.sum(axis=-1, keepdims=True)
        pv = jnp.dot(p.astype(BF), v, preferred_element_type=F32)
        o = pv if o is None else o + pv
    return o / l


def _na_kernel(kb_ref, pid_ref, dr_ref, q_ref, k_ref, v_ref, kc_ref, vc_ref, cbx_ref, o_ref, bias_ref,
               *, nkeys, heads, nsub):
    step = pl.program_id(1)
    nq = q_ref.shape[0] // nsub
    rq, kr = nq // GRID_W, nkeys // GRID_W
    lane = lax.broadcasted_iota(jnp.int32, (nq, LANES), 1)
    low = lane < HEAD_DIM
    dn = (((1,), (1,)), ((), ()))

    for sb in range(nsub):
        i = step * nsub + sb
        pid = pid_ref[i]

        @pl.when((step == 0) | (pid != pid_ref[jnp.maximum(i - nsub, 0)]))
        def _build_bias(sb=sb, pid=pid):
            for qi in range(rq):
                for w in range(kr):
                    a = dr_ref[(pid * rq + qi) * kr + w]
                    for h in range(heads):
                        bias_ref[sb, h, qi * GRID_W:(qi + 1) * GRID_W, w * GRID_W:(w + 1) * GRID_W] = cbx_ref[h, a]

    tasks = [(sb, h) for sb in range(nsub) for h in range(heads)]
    starts = [pl.multiple_of(kb_ref[step * nsub + sb] * GRID_W, GRID_W) for sb in range(nsub)]

    def scores(sb, h):
        sl = slice((h // 2) * LANES, (h // 2 + 1) * LANES)
        q2 = q_ref[sb * nq:(sb + 1) * nq, sl]
        qm = jnp.where(low if h % 2 == 0 else jnp.logical_not(low), q2, jnp.zeros_like(q2))
        s_w = lax.dot_general(qm, k_ref[pl.ds(starts[sb], nkeys), sl], dn, preferred_element_type=F32)
        s_c = lax.dot_general(qm, kc_ref[:, sl], dn, preferred_element_type=F32)
        return s_w + bias_ref[sb, h], s_c

    def weights(s_w, s_c):
        m = jnp.maximum(s_w.max(axis=-1, keepdims=True), s_c.max(axis=-1, keepdims=True))
        p_w = jnp.exp2(s_w - m)
        p_c = jnp.exp2(s_c - m)
        l = p_w.sum(axis=-1, keepdims=True) + p_c.sum(axis=-1, keepdims=True)
        return p_w.astype(BF), p_c.astype(BF), l

    outs = {}

    def finish(sb, h, p_w, p_c, l):
        sl = slice((h // 2) * LANES, (h // 2 + 1) * LANES)
        o = (jnp.dot(p_w, v_ref[pl.ds(starts[sb], nkeys), sl], preferred_element_type=F32)
             + jnp.dot(p_c, vc_ref[:, sl], preferred_element_type=F32))
        outs[h % 2] = o / l
        if h % 2 == 1:
            o_ref[sb * nq:(sb + 1) * nq, sl] = jnp.where(low, outs[0], outs[1]).astype(BF)

    nt = len(tasks)
    sc = {0: scores(*tasks[0])}
    if nt > 1:
        sc[1] = scores(*tasks[1])
    pw = {0: weights(*sc.pop(0))}
    for n in range(nt):
        if n + 2 < nt:
            sc[n + 2] = scores(*tasks[n + 2])
        if n + 1 < nt:
            pw[n + 1] = weights(*sc.pop(n + 1))
        finish(*tasks[n], *pw.pop(n))


def _na_plan(rows, rq):
    wr = min(NA_ROWS, rows)
    kr = rq + wr - 1
    nblk = rows // rq
    kbs, pids, pats = [], [], []
    for blk in range(nblk):
        r0 = blk * rq
        rs = [int(np.clip(r0 + i - wr // 2, 0, rows - wr)) for i in range(rq)]
        kb = min(rs[0], rows - kr)
        pat = (tuple(r - kb for r in rs), r0 - kb)
        if pat not in pats:
            pats.append(pat)
        kbs.append(kb)
        pids.append(pats.index(pat))
    return np.asarray(kbs, np.int32), np.asarray(pids, np.int32), pats, kr, wr


def _na_bias(rpb, pats, rq, kr, wr):
    cols = np.arange(GRID_W)
    cstart = np.clip(cols - NA_COLS // 2, 0, GRID_W - NA_COLS)
    kc = np.arange(GRID_W)
    col_ok = (kc[None, :] >= cstart[:, None]) & (kc[None, :] < cstart[:, None] + NA_COLS)
    dc = kc[None, :] - cols[:, None] + NA_COLS - 1
    depth, heads, nr, ncol = rpb.shape
    sel = (dc[:, :, None] == np.arange(ncol)[None, None, :]) & col_ok[:, :, None]
    cb = jnp.einsum('lhab,ckb->lhack', rpb.astype(F32), jnp.asarray(sel, F32), precision=lax.Precision.HIGHEST)
    cb = jnp.where(jnp.asarray(col_ok)[None, None, None], cb * math.log2(math.e), NEG)
    cbx = jnp.concatenate([cb, jnp.full((depth, heads, 1, GRID_W, GRID_W), NEG, F32)], axis=2)
    slots = []
    for rs_off, r_off in pats:
        i = np.arange(rq)[:, None]
        w = np.arange(kr)[None, :]
        rso = np.asarray(rs_off)[:, None]
        row_ok = (w >= rso) & (w < rso + wr)
        slots.append(np.where(row_ok, w - r_off - i + NA_ROWS - 1, nr))
    return cbx, np.stack(slots).reshape(-1).astype(np.int32)


def _na_tables(rpb_all, seq, rq):
    kbs, pids, pats, kr, wr = _na_plan(seq // GRID_W, rq)
    cbx, slots = _na_bias(rpb_all, pats, rq, kr, wr)
    return dict(kbs=jnp.asarray(kbs), pids=jnp.asarray(pids), slots=jnp.asarray(slots), cbx=cbx, kr=kr, rq=rq)


def _na_latent(q, k, v, kc, vc, tabs, layer, *, batch, seq, n_ctx):
    rows = seq // GRID_W
    cbx, rq, kr = tabs["cbx"], tabs["rq"], tabs["kr"]
    heads = cbx.shape[1]
    nblk = rows // rq
    nq = rq * GRID_W
    nkeys = kr * GRID_W
    w = q.shape[1]
    nsub = 2 if nblk % 2 == 0 else 1
    nstep = nblk // nsub
    kern = functools.partial(_na_kernel, nkeys=nkeys, heads=heads, nsub=nsub)
    grid_spec = pltpu.PrefetchScalarGridSpec(
        num_scalar_prefetch=3,
        grid=(batch, nstep),
        in_specs=[pl.BlockSpec((nsub * nq, w), lambda b, i, *_: (b * nstep + i, 0)),
                  pl.BlockSpec((seq, w), lambda b, i, *_: (b, 0)),
                  pl.BlockSpec((seq, w), lambda b, i, *_: (b, 0)),
                  pl.BlockSpec((n_ctx, w), lambda b, i, *_: (b, 0)),
                  pl.BlockSpec((n_ctx, w), lambda b, i, *_: (b, 0)),
                  pl.BlockSpec((None,) + cbx.shape[1:], lambda b, i, *_: (layer, 0, 0, 0, 0))],
        out_specs=pl.BlockSpec((nsub * nq, w), lambda b, i, *_: (b * nstep + i, 0)),
        scratch_shapes=[pltpu.VMEM((nsub, heads, nq, nkeys), F32)],
    )
    return pl.pallas_call(
        kern,
        grid_spec=grid_spec,
        out_shape=jax.ShapeDtypeStruct(q.shape, BF),
        compiler_params=_cparams("arbitrary", "arbitrary"),
        name="na_latent",
    )(tabs["kbs"], tabs["pids"], tabs["slots"], q, k, v, kc, vc, cbx)


def _na_ctx_kernel(q_ref, k_ref, v_ref, o_ref, *, heads):
    nq = q_ref.shape[0]
    lane = lax.broadcasted_iota(jnp.int32, (nq, LANES), 1)
    low = lane < HEAD_DIM
    dn = (((1,), (1,)), ((), ()))
    for hp in range(heads // 2):
        sl = slice(hp * LANES, (hp + 1) * LANES)
        q2 = q_ref[:, sl]
        k2 = k_ref[:, sl]
        v2 = v_ref[:, sl]
        outs = []
        for hh in range(2):
            qm = jnp.where(low if hh == 0 else jnp.logical_not(low), q2, jnp.zeros_like(q2))
            s = lax.dot_general(qm, k2, dn, preferred_element_type=F32)
            outs.append(_softmax_pv([(s, v2)], (nq, 1)))
        o_ref[:, sl] = jnp.where(low, outs[0], outs[1]).astype(BF)


def _na_ctx(qc, kc, vc, *, batch, n_ctx, heads):
    w = qc.shape[1]
    spec = pl.BlockSpec((n_ctx, w), lambda b: (b, 0))
    return pl.pallas_call(
        functools.partial(_na_ctx_kernel, heads=heads),
        grid=(batch,),
        in_specs=[spec, spec, spec],
        out_specs=spec,
        out_shape=jax.ShapeDtypeStruct(qc.shape, BF),
        compiler_params=_cparams("parallel"),
        name="na_ctx",
    )(qc, kc, vc)


def _outproj_kernel(z_ref, pf_ref, na_ref, x_ref, gate_ref, gpost_ref,
                    wglu_ref, cs_ref, wfo_ref, wout_ref, o_ref, *, widths, sub):
    w_ssm, w_fn, w_na = widths
    for r0 in range(0, x_ref.shape[0], sub):
        rs = slice(r0, r0 + sub)
        z = jnp.concatenate([z_ref[j, rs, :] for j in range(w_ssm // LANES)], axis=1)
        gl = jnp.dot(z.astype(BF), wglu_ref[...], preferred_element_type=F32)
        y_ssm = (z * jax.nn.sigmoid(gl)).astype(BF)
        if len(pf_ref.shape) == 3:
            tiles = [pf_ref[c, rs, :] for c in range(pf_ref.shape[0])]
        else:
            l2 = pf_ref.shape[1]
            tiles = [jnp.concatenate([pf_ref[c, :, k1, :] for k1 in range(r0 // l2, (r0 + sub) // l2)], axis=0)
                     for c in range(pf_ref.shape[0])]
        pf = jnp.concatenate(tiles, axis=1).astype(BF)
        mixed = jnp.dot(pf, cs_ref[...], preferred_element_type=F32).astype(BF)
        y_fft = jnp.dot(mixed, wfo_ref[...], preferred_element_type=F32).astype(BF)
        y = jnp.concatenate([y_ssm, y_fft, na_ref[rs, :]], axis=1)
        o = jnp.dot(y, wout_ref[...], preferred_element_type=F32)
        ms = jnp.mean(o * o, axis=-1, keepdims=True)
        o_ref[rs, :] = x_ref[rs, :] + gate_ref[...] * (o * lax.rsqrt(ms + EPS) * gpost_ref[...])


def _outproj(z3, pf, yna, x2d, mod3, g_post, w_glu, cs, w_fo, w_out, *,
             layer, tm, tiles_per_batch, mod_row, widths):
    n, d = x2d.shape
    w_ssm, w_fn, w_na = widths
    if mod_row is None:
        row = lambda i: i // tiles_per_batch
    else:
        row = lambda i: mod_row
    kern = functools.partial(_outproj_kernel, widths=widths, sub=min(tm, 512))
    return pl.pallas_call(
        kern,
        grid=(n // tm,),
        in_specs=[pl.BlockSpec((w_ssm // LANES, tm, LANES), lambda i: (0, i, 0)),
                  (pl.BlockSpec((None, pf.shape[1], tm, LANES),
                                lambda i: (i // tiles_per_batch, 0, i % tiles_per_batch, 0))
                   if pf.ndim == 4 else
                   pl.BlockSpec((None, pf.shape[1], pf.shape[2], tm // pf.shape[2], LANES),
                                lambda i: (i // tiles_per_batch, 0, 0, i % tiles_per_batch, 0))),
                  pl.BlockSpec((tm, w_na), lambda i: (i, 0)),
                  pl.BlockSpec((tm, d), lambda i: (i, 0)),
                  pl.BlockSpec((None, 1, d), lambda i: (row(i), 0, 2)),
                  pl.BlockSpec((None, 1, d), lambda i: (layer, 0, 0)),
                  _layer_resident(w_glu, layer),
                  _resident(cs.shape, lambda i: (0, 0)),
                  _layer_resident(w_fo, layer),
                  _layer_resident(w_out, layer)],
        out_specs=pl.BlockSpec((tm, d), lambda i: (i, 0)),
        out_shape=jax.ShapeDtypeStruct((n, d), F32),
        compiler_params=_cparams("parallel"),
        name="outproj",
    )(z3, pf, yna, x2d, mod3, g_post.reshape(-1, 1, d), w_glu, cs, w_fo, w_out)


def _ffn_kernel(x_ref, sh_ref, sc_ref, gate_ref, gpre_ref, gpost_ref, wg_ref, wu_ref, wd_ref,
                o_ref, a_ref, *, sub, chunk):
    dff = wg_ref.shape[1]
    for r0 in range(0, x_ref.shape[0], sub):
        rs = slice(r0, r0 + sub)
        x = x_ref[rs, :]
        ms = jnp.mean(x * x, axis=-1, keepdims=True)
        m = ((x * lax.rsqrt(ms + EPS) * gpre_ref[...]) * (1.0 + sc_ref[...]) + sh_ref[...]).astype(BF)
        for c0 in range(0, dff, chunk):
            sl = slice(c0, min(c0 + chunk, dff))
            g = jnp.dot(m, wg_ref[:, sl], preferred_element_type=F32)
            u = jnp.dot(m, wu_ref[:, sl], preferred_element_type=F32)
            a_ref[rs, sl] = (g * jax.nn.sigmoid(g) * u).astype(BF)
        y = jnp.dot(a_ref[rs, :], wd_ref[...], preferred_element_type=F32)
        ms2 = jnp.mean(y * y, axis=-1, keepdims=True)
        o_ref[rs, :] = x + gate_ref[...] * (y * lax.rsqrt(ms2 + EPS) * gpost_ref[...])


def _ffn_pair_kernel(x_ref, xc_ref, sh_ref, sc_ref, gate_ref, gpre_ref, gpost_ref, wg_ref, wu_ref, wd_ref,
                     o_ref, oc_ref, a_ref, *, sub, chunk, n_lat_tiles):
    i = pl.program_id(0)
    common = (sh_ref, sc_ref, gate_ref, gpre_ref, gpost_ref, wg_ref, wu_ref, wd_ref)

    @pl.when(i < n_lat_tiles)
    def _latent():
        _ffn_kernel(x_ref, *common, o_ref, a_ref, sub=sub, chunk=chunk)

    @pl.when(i >= n_lat_tiles)
    def _context():
        _ffn_kernel(xc_ref, *common, oc_ref, a_ref, sub=min(sub, xc_ref.shape[0]), chunk=chunk)


def _ffn(x2d, xc2d, mod3, g_pre, g_post, wg, wu, wd, *, layer, tm, tiles_per_batch, ctx_row):
    n, d = x2d.shape
    dff = wg.shape[2]
    nl = n // tm
    gains = (g_pre.reshape(-1, 1, d), g_post.reshape(-1, 1, d))
    weights = [_layer_resident(wg, layer), _layer_resident(wu, layer), _layer_resident(wd, layer)]
    gain_specs = [pl.BlockSpec((None, 1, d), lambda i: (layer, 0, 0))] * 2
    if xc2d is None:
        row = lambda i: i // tiles_per_batch
        return pl.pallas_call(
            functools.partial(_ffn_kernel, sub=min(tm, 512), chunk=3 * MXU_TILE),
            grid=(nl,),
            in_specs=[pl.BlockSpec((tm, d), lambda i: (i, 0))]
                     + [pl.BlockSpec((None, 1, d), functools.partial(lambda i, c: (row(i), 0, c), c=c))
                        for c in (3, 4, 5)] + gain_specs + weights,
            out_specs=pl.BlockSpec((tm, d), lambda i: (i, 0)),
            out_shape=jax.ShapeDtypeStruct((n, d), F32),
            scratch_shapes=[pltpu.VMEM((tm, dff), BF)],
            compiler_params=_cparams("parallel"),
            name="ffn",
        )(x2d, mod3, mod3, mod3, *gains, wg, wu, wd), None
    nc = xc2d.shape[0]
    row = lambda i: jnp.where(i < nl, i // tiles_per_batch, ctx_row)
    lat = lambda i: (jnp.minimum(i, nl - 1), 0)
    ctx = lambda i: (jnp.maximum(i - nl, 0), 0)
    return pl.pallas_call(
        functools.partial(_ffn_pair_kernel, sub=min(tm, 512), chunk=3 * MXU_TILE, n_lat_tiles=nl),
        grid=(nl + 1,),
        in_specs=[pl.BlockSpec((tm, d), lat), pl.BlockSpec((nc, d), ctx)]
                 + [pl.BlockSpec((None, 1, d), functools.partial(lambda i, c: (row(i), 0, c), c=c))
                    for c in (3, 4, 5)] + gain_specs + weights,
        out_specs=[pl.BlockSpec((tm, d), lat), pl.BlockSpec((nc, d), ctx)],
        out_shape=[jax.ShapeDtypeStruct((n, d), F32), jax.ShapeDtypeStruct((nc, d), F32)],
        scratch_shapes=[pltpu.VMEM((tm, dff), BF)],
        compiler_params=_cparams("arbitrary"),
        name="ffn_pair",
    )(x2d, xc2d, mod3, mod3, mod3, *gains, wg, wu, wd)


def _layer(x2d, xc2d, p, w, layer, *, batch, seq, n_ctx, last, rope_tabs, consts):
    d = x2d.shape[1]
    w_ssm = consts["w_ssm"]
    w_fn = w["w_fourier"].shape[1]
    w_na = (w["w_in"].shape[2] - w_ssm - w_fn) // 3
    widths = (w_ssm, w_fn, w_na)
    heads = w_na // HEAD_DIM
    tm = min(512, seq)
    tmc = min(512, batch * n_ctx)

    mod = w["mod"][layer]
    mod3 = mod.reshape(mod.shape[0], 1, 6 * d)

    cos, sin = rope_tabs
    tm2 = min(2 * tm, seq)
    u3, f, q, k, v = _inproj(x2d, mod3, p["g_pre_mix"], w["w_in"], cos, sin, layer=layer, tm=tm2,
                             tiles_per_batch=seq // tm2, mod_row=None, widths=widths, rope=True)
    ones = jnp.ones((tmc, LANES), F32)
    u3c, fc, qc, kc, vc = _inproj(xc2d, mod3, p["g_pre_mix"], w["w_in"], ones, ones, layer=layer, tm=tmc,
                                  tiles_per_batch=1, mod_row=batch, widths=widths, rope=False)

    z3, zc3 = _ssm(u3, u3c, *w["ssm"], layer=layer, batch=batch, seq=seq, n_ctx=n_ctx)

    l2 = consts["l2"]
    l1 = seq // l2
    g = _dft_a(consts["dft_a"], f.reshape(batch, l2, l1 * w_fn))
    pf = _dft_b(consts["dft_b"], g.reshape(batch, 2, l2, l1, w_fn), kb=min(8, l2))

    yna = _na_latent(q, k, v, kc, vc, consts["na"], layer, batch=batch, seq=seq, n_ctx=n_ctx)

    w_glu, w_fo, w_out = w["w_glu"], w["w_fourier"], w["w_out"]
    x2d = _outproj(z3, pf, yna, x2d, mod3, p["g_post_mix"], w_glu, consts["cs_lat"], w_fo, w_out,
                   layer=layer, tm=tm2, tiles_per_batch=seq // tm2, mod_row=None, widths=widths)
    if not last:
        pfc = _dft_ctx(consts["dft_ctx"], fc.reshape(batch, n_ctx, w_fn))
        ynac = _na_ctx(qc, kc, vc, batch=batch, n_ctx=n_ctx, heads=heads)
        tc = min(tmc, n_ctx)
        xc2d = _outproj(zc3, pfc, ynac, xc2d, mod3, p["g_post_mix"], w_glu, consts["cs_ctx"], w_fo, w_out,
                        layer=layer, tm=tc, tiles_per_batch=n_ctx // tc, mod_row=batch, widths=widths)
    else:
        xc2d = None
    wg, wu, wd = w["w_ffn_gate"], w["w_ffn_up"], w["w_ffn_down"]
    return _ffn(x2d, xc2d, mod3, p["g_pre_ffn"], p["g_post_ffn"], wg, wu, wd,
                layer=layer, tm=tm2, tiles_per_batch=seq // tm2, ctx_row=batch)


def kernel(x, c, ctx, c_ctx, w_mod, b_mod, g_pre_mix, g_post_mix, w_in, ssm_a_re, ssm_a_im, ssm_log_dt,
           ssm_b_re, ssm_b_im, ssm_c_re, ssm_c_im, ssm_d, w_glu, w_fourier, na_rpb, w_out, g_pre_ffn,
           g_post_ffn, w_ffn_gate, w_ffn_up, w_ffn_down):
    batch, seq, d = x.shape
    n_ctx = ctx.shape[1]
    depth = w_mod.shape[0]
    w_fn = w_fourier.shape[1]
    params = dict(g_pre_mix=g_pre_mix, g_post_mix=g_post_mix, g_pre_ffn=g_pre_ffn, g_post_ffn=g_post_ffn)
    seg_len = (seq + 2 * n_ctx) // SSM_T // SSM_SEG
    ssm_ops = jax.vmap(functools.partial(_ssm_weights, seg_len=seg_len))(
        ssm_a_re, ssm_a_im, ssm_log_dt, ssm_b_re, ssm_b_im, ssm_c_re, ssm_c_im, ssm_d)
    nrow = -(-(batch + 1) // 8) * 8
    cstack = jnp.concatenate([c, c_ctx[None, :], jnp.zeros((nrow - batch - 1, d), c.dtype)], axis=0)
    weights = dict(mod=_mod_rows(cstack, w_mod, b_mod), ssm=ssm_ops,
                   w_in=_cast_bf16(w_in), w_glu=_cast_bf16(w_glu), w_fourier=_cast_bf16(w_fourier),
                   w_out=_cast_bf16(w_out), w_ffn_gate=_cast_bf16(w_ffn_gate),
                   w_ffn_up=_cast_bf16(w_ffn_up), w_ffn_down=_cast_bf16(w_ffn_down))

    l2 = 64 if seq % (64 * 8) == 0 else 8
    l1 = seq // l2
    consts = dict(
        l2=l2,
        w_ssm=ssm_d.shape[1],
        na=_na_tables(na_rpb, seq, 4),
        dft_a=_dft_rows(l2).astype(BF),
        dft_b=_dft_b_mats(seq, l1, l2).astype(BF),
        dft_ctx=_dft_rows(n_ctx).astype(BF),
        cs_lat=_channel_dft(w_fn, seq).astype(BF),
        cs_ctx=_channel_dft(w_fn, n_ctx).astype(BF),
    )
    rope_tabs = _rope_tables(seq)

    x2d = x.reshape(batch * seq, d)
    xc2d = ctx.reshape(batch * n_ctx, d)
    for layer in range(depth):
        x2d, xc2d = _layer(x2d, xc2d, params, weights, layer, batch=batch, seq=seq, n_ctx=n_ctx,
                           last=(layer == depth - 1), rope_tabs=rope_tabs, consts=consts)
    return x2d.reshape(batch, seq, d)
```

```python
import functools
import math

import numpy as np
import jax
import jax.numpy as jnp
from jax import lax
from jax.experimental import pallas as pl
from jax.experimental.pallas import tpu as pltpu

BF = jnp.bfloat16
F32 = jnp.float32

EPS = 1e-6
GRID_W = 64
HEAD_DIM = 64
NA_ROWS = 8
NA_COLS = 16
ROPE_BASE = 10000.0
SSM_GROUP = 16
SSM_STATE = 64
FNET_GROUP = 64
LANES = 128
MXU_TILE = 256
SSM_T = 8
SSM_SEG = 8
SSM_STAGES = (4, 2)
NEG = -1e30
VMEM_LIMIT = 56 * 1024 * 1024


def _cparams(*sem):
    return pltpu.CompilerParams(dimension_semantics=sem, vmem_limit_bytes=VMEM_LIMIT)


def _resident(shape, index_map):
    return pl.BlockSpec(shape, index_map, pipeline_mode=pl.Buffered(1))


def _layer_resident(w_all, layer):
    return pl.BlockSpec((None,) + w_all.shape[1:], lambda *_: (layer, 0, 0), pipeline_mode=pl.Buffered(1))


def _cast_kernel(w_ref, o_ref):
    o_ref[...] = w_ref[...].astype(BF)


def _cast_bf16(w_all):
    depth, rows, cols = w_all.shape
    br = rows
    while br * cols * 4 > 4 * 1024 * 1024 and br % 32 == 0:
        br //= 2
    spec = pl.BlockSpec((None, br, cols), lambda l, i: (l, i, 0))
    return pl.pallas_call(
        _cast_kernel,
        grid=(depth, rows // br),
        in_specs=[spec],
        out_specs=spec,
        out_shape=jax.ShapeDtypeStruct(w_all.shape, BF),
        compiler_params=_cparams("parallel", "parallel"),
        name="cast_bf16",
    )(w_all)


def _mod_kernel(c_ref, w_ref, b_ref, o_ref):
    cs = c_ref[...]
    s = (cs * jax.nn.sigmoid(cs)).astype(BF)
    o_ref[...] = jnp.dot(s, w_ref[...].astype(BF), preferred_element_type=F32) + b_ref[...]


def _mod_rows(cstack, w_mod, b_mod):
    rows, d = cstack.shape
    depth, _, n = w_mod.shape
    bn = 2 * d if n % (2 * d) == 0 else d
    return pl.pallas_call(
        _mod_kernel,
        grid=(depth, n // bn),
        in_specs=[pl.BlockSpec((rows, d), lambda l, i: (0, 0)),
                  pl.BlockSpec((None, d, bn), lambda l, i: (l, 0, i)),
                  pl.BlockSpec((None, 1, bn), lambda l, i: (l, 0, i))],
        out_specs=pl.BlockSpec((None, rows, bn), lambda l, i: (l, 0, i)),
        out_shape=jax.ShapeDtypeStruct((depth, rows, n), F32),
        compiler_params=_cparams("parallel", "parallel"),
        name="mod_rows",
    )(cstack, w_mod, b_mod.reshape(depth, 1, n))


def _rope_tile(t, cos, sin, first):
    partner = jnp.where(first, pltpu.roll(t, LANES - 16, 1), pltpu.roll(t, 16, 1))
    return t * cos + partner * sin


def _inproj_kernel(x_ref, sh_ref, sc_ref, g_ref, w_ref, cos_ref, sin_ref,
                   u_ref, f_ref, q_ref, k_ref, v_ref, *, widths, rope, sub):
    w_ssm, w_fn, w_na = widths
    scale = HEAD_DIM ** -0.5 * math.log2(math.e)
    w = w_ref[...].astype(BF)
    for r0 in range(0, x_ref.shape[0], sub):
        rs = slice(r0, r0 + sub)
        x = x_ref[rs, :]
        ms = jnp.mean(x * x, axis=-1, keepdims=True)
        xn = x * lax.rsqrt(ms + EPS) * g_ref[...]
        m = (xn * (1.0 + sc_ref[...]) + sh_ref[...]).astype(BF)

        h = jnp.dot(m, w, preferred_element_type=F32)

        def proj(lo, n, h=h):
            return h[:, lo:lo + n]

        u = proj(0, w_ssm)
        for j in range(w_ssm // LANES):
            u_ref[j, rs, :] = u[:, j * LANES:(j + 1) * LANES]
        f_ref[rs, :] = proj(w_ssm, w_fn).astype(BF)
        q = proj(w_ssm + w_fn, w_na)
        k = proj(w_ssm + w_fn + w_na, w_na)
        v_ref[rs, :] = proj(w_ssm + w_fn + 2 * w_na, w_na).astype(BF)
        if rope:
            cos = cos_ref[rs, :]
            sin = sin_ref[rs, :]
            lane = lax.broadcasted_iota(jnp.int32, cos.shape, 1)
            first = (lane % 32) < 16
            for j in range(w_na // LANES):
                sl = slice(j * LANES, (j + 1) * LANES)
                q_ref[rs, sl] = (_rope_tile(q[:, sl], cos, sin, first) * scale).astype(BF)
                k_ref[rs, sl] = _rope_tile(k[:, sl], cos, sin, first).astype(BF)
        else:
            q_ref[rs, :] = (q * scale).astype(BF)
            k_ref[rs, :] = k.astype(BF)


def _inproj(x2d, mod3, g_pre, w_in, cos, sin, *, layer, tm, tiles_per_batch, mod_row, widths, rope):
    n, d = x2d.shape
    w_ssm, w_fn, w_na = widths
    nt = n // tm
    if mod_row is None:
        row = lambda i: i // tiles_per_batch
    else:
        row = lambda i: mod_row
    pos = lambda i: (i % tiles_per_batch, 0)
    kern = functools.partial(_inproj_kernel, widths=widths, rope=rope, sub=min(tm, 512))
    return pl.pallas_call(
        kern,
        grid=(nt,),
        in_specs=[pl.BlockSpec((tm, d), lambda i: (i, 0)),
                  pl.BlockSpec((None, 1, d), lambda i: (row(i), 0, 0)),
                  pl.BlockSpec((None, 1, d), lambda i: (row(i), 0, 1)),
                  pl.BlockSpec((None, 1, d), lambda i: (layer, 0, 0)),
                  _layer_resident(w_in, layer),
                  pl.BlockSpec((tm, LANES), pos),
                  pl.BlockSpec((tm, LANES), pos)],
        out_specs=[pl.BlockSpec((w_ssm // LANES, tm, LANES), lambda i: (0, i, 0)),
                   pl.BlockSpec((tm, w_fn), lambda i: (i, 0)),
                   pl.BlockSpec((tm, w_na), lambda i: (i, 0)),
                   pl.BlockSpec((tm, w_na), lambda i: (i, 0)),
                   pl.BlockSpec((tm, w_na), lambda i: (i, 0))],
        out_shape=[jax.ShapeDtypeStruct((w_ssm // LANES, n, LANES), F32),
                   jax.ShapeDtypeStruct((n, w_fn), BF),
                   jax.ShapeDtypeStruct((n, w_na), BF),
                   jax.ShapeDtypeStruct((n, w_na), BF),
                   jax.ShapeDtypeStruct((n, w_na), BF)],
        compiler_params=_cparams("parallel"),
        name="inproj_rope" if rope else "inproj_ctx",
    )(x2d, mod3, mod3, g_pre.reshape(-1, 1, d), w_in, cos, sin)


def _rope_tables(seq):
    t = np.arange(seq)
    row = (t // GRID_W).astype(np.float32)
    col = (t % GRID_W).astype(np.float32)
    quarter = HEAD_DIM // 4
    freqs = (np.float32(ROPE_BASE) ** (-np.arange(quarter, dtype=np.float32) / quarter)).astype(np.float32)
    d = np.arange(LANES) % HEAD_DIM
    use_col = (d // (HEAD_DIM // 2)) == 1
    fidx = d % quarter
    sign = np.where((d % (HEAD_DIM // 2)) // quarter == 0, -1.0, 1.0)
    pos = np.where(use_col[None, :], col[:, None], row[:, None])
    ang = (pos * freqs[fidx][None, :]).astype(np.float32).astype(np.float64)
    return jnp.asarray(np.cos(ang), F32), jnp.asarray(np.sin(ang) * sign[None, :], F32)


def _gelu_tanh(x):
    c = math.sqrt(2.0 / math.pi)
    return x * (0.5 * (1.0 + jnp.tanh(c * (x + 0.044715 * (x * x * x)))))


def _group_of(idx, width, groups):
    shift = width.bit_length() - 1
    assert width == 1 << shift and groups & (groups - 1) == 0
    return lax.bitwise_and(lax.shift_right_logical(idx, shift), groups - 1)


def _same_group(shape, row0, row_width, col0, col_width, groups):
    r = lax.broadcasted_iota(jnp.int32, shape, 0) + row0
    c = lax.broadcasted_iota(jnp.int32, shape, 1) + col0
    return _group_of(r, row_width, groups) == _group_of(c, col_width, groups)


def _swap_stage(tiles, d):
    lane = lax.broadcasted_iota(jnp.int32, tiles[0].shape, 1)
    upper = lax.bitwise_and(lax.shift_right_logical(lane, SSM_GROUP.bit_length() - 1), d) != 0
    out = list(tiles)
    for x in range(len(tiles)):
        if x & d:
            continue
        a, b = tiles[x], tiles[x + d]
        out[x] = jnp.where(upper, pltpu.roll(b, SSM_GROUP * d, 1), a)
        out[x + d] = jnp.where(upper, b, pltpu.roll(a, LANES - SSM_GROUP * d, 1))
    return out


def _ssm_kernel(ul_ref, uc_ref, wc_ref, lag_ref, dc_ref, rexp_ref, cexp_ref, lam_ref, lamseg_ref,
                z_ref, zc_ref, u_s, bm_s, kin_s, cm_s, s_ref, h_ref, e_ref, *, seg_len, n_lat, n_ctx, stages):
    T = SSM_T
    ns = 2 ** len(stages)
    gs = T // ns
    sw = gs * LANES
    rows = u_s.shape[1]
    tps = s_ref.shape[0] // ns
    tpq = tps // 4
    nst = tpq * LANES
    gpt = LANES // SSM_GROUP
    per_q = gpt * SSM_STATE

    def lanes(c):
        return slice(c * LANES, (c + 1) * LANES)

    def col0(s, q):
        return q * per_q + s * nst

    def swapped(tiles, order):
        for d in order:
            tiles = _swap_stage(tiles, d)
        return tiles

    @pl.when(pl.program_id(1) == 0)
    def _expand_operators():
        rexp = rexp_ref[...]
        cexp = cexp_ref[...]
        for s in range(ns):
            wsel = jnp.concatenate([wc_ref[:, col0(s, q):col0(s, q) + nst] for q in range(4)], axis=1)
            blk = jnp.dot(rexp, wsel, preferred_element_type=F32)
            keep = _same_group(blk.shape, 0, SSM_GROUP, 0, SSM_STATE, gs)
            bm_s[s] = jnp.where(keep, blk, 0.0).astype(BF)
            dsel = jnp.concatenate([dc_ref[col0(s, q):col0(s, q) + nst, :] for q in range(4)], axis=0)
            blk = jnp.dot(dsel, cexp, preferred_element_type=F32)
            keep = _same_group(blk.shape, 0, SSM_STATE, 0, SSM_GROUP, gs)
            cm_s[s] = jnp.where(keep, blk, 0.0).astype(BF)
        r16 = lax.broadcasted_iota(jnp.int32, (LANES, SSM_GROUP), 0)
        c16 = lax.broadcasted_iota(jnp.int32, (LANES, SSM_GROUP), 1)
        chan = jnp.where(lax.bitwise_and(r16, SSM_GROUP - 1) == c16, 1.0, 0.0).astype(BF)
        for t in range(T):
            lo = (T - 1 - t) * LANES
            blk = jnp.dot(chan, lag_ref[:, lo:lo + T * LANES], preferred_element_type=F32)
            keep = _same_group(blk.shape, 0, SSM_GROUP, 0, SSM_GROUP, gpt)
            blk = jnp.where(keep, blk, 0.0)
            tiles = swapped([blk[:, lanes(x)] for x in range(T)], stages)
            r0 = (t % gs) * LANES + (t // gs) * gs * SSM_GROUP
            for s in range(ns):
                part = jnp.concatenate(tiles[s * gs:(s + 1) * gs], axis=1)
                kin_s[s, r0:r0 + gs * SSM_GROUP, :] = part[s * gs * SSM_GROUP:(s + 1) * gs * SSM_GROUP, :].astype(BF)

    xs = []
    for t in range(T):
        ct = uc_ref[pl.ds(t, n_ctx, stride=T), :]
        xs.append(jnp.concatenate([ct, ul_ref[pl.ds(t, n_lat, stride=T), :], ct], axis=0))
    xs = swapped(xs, stages)
    for s in range(ns):
        for k in range(gs):
            u_s[s, :, lanes(k)] = xs[s * gs + k].astype(BF)
    for s in range(ns):
        r = jnp.dot(u_s[s], bm_s[s], preferred_element_type=F32)
        for c in range(tps):
            s_ref[s * tps + c] = r[:, lanes(c)]

    chains = []
    for s in range(ns):
        for d in range(2):
            for k in range(tpq):
                cr, ci = s * tps + 2 * d * tpq + k, s * tps + (2 * d + 1) * tpq + k
                lr = slice(col0(s, 2 * d) + k * LANES, col0(s, 2 * d) + (k + 1) * LANES)
                li = slice(col0(s, 2 * d + 1) + k * LANES, col0(s, 2 * d + 1) + (k + 1) * LANES)
                chains.append((cr, ci, d, lr, li))
    nt = ns * tps

    def scan_step(i, carry, record):
        new = list(carry)
        for cr, ci, d, lr_sl, li_sl in chains:
            rsl = pl.ds(i if d == 0 else seg_len - 1 - i, SSM_SEG, stride=seg_len)
            hr, hi = carry[cr], carry[ci]
            if record:
                h_ref[cr, rsl, :] = hr
                h_ref[ci, rsl, :] = hi
            lr, li = lam_ref[:, lr_sl], lam_ref[:, li_sl]
            new[cr] = lr * hr - li * hi + s_ref[cr, rsl, :]
            new[ci] = lr * hi + li * hr + s_ref[ci, rsl, :]
        return tuple(new)

    zero = jnp.zeros((SSM_SEG, LANES), F32)
    fin = lax.fori_loop(0, seg_len, lambda i, c: scan_step(i, c, False), (zero,) * nt, unroll=4)
    for c in range(nt):
        e_ref[c, 0:SSM_SEG, :] = fin[c]

    for cr, ci, d, lr_sl, li_sl in chains:
        lr, li = lamseg_ref[:, lr_sl], lamseg_ref[:, li_sl]
        er = jnp.zeros((1, LANES), F32)
        ei = jnp.zeros((1, LANES), F32)
        for n in range(SSM_SEG):
            g = n if d == 0 else SSM_SEG - 1 - n
            e_ref[cr, SSM_SEG + g:SSM_SEG + g + 1, :] = er
            e_ref[ci, SSM_SEG + g:SSM_SEG + g + 1, :] = ei
            fr, fi = e_ref[cr, g:g + 1, :], e_ref[ci, g:g + 1, :]
            er, ei = lr * er - li * ei + fr, lr * ei + li * er + fi

    ent = tuple(e_ref[c, SSM_SEG:2 * SSM_SEG, :] for c in range(nt))
    lax.fori_loop(0, seg_len, lambda i, c: scan_step(i, c, True), ent, unroll=2)

    ys, ycs = [], []
    for s in range(ns):
        hcat = jnp.concatenate([h_ref[s * tps + c].astype(BF) for c in range(tps)], axis=1)
        y = _gelu_tanh(jnp.dot(u_s[s], kin_s[s], preferred_element_type=F32)
                       + jnp.dot(hcat, cm_s[s], preferred_element_type=F32))
        ys += [y[:, lanes(k)] for k in range(gs)]
        hc = jnp.concatenate(
            [h_ref[s * tps + c, 0:n_ctx, :].astype(BF) for c in range(2 * tpq)]
            + [h_ref[s * tps + c, rows - n_ctx:rows, :].astype(BF) for c in range(2 * tpq, tps)], axis=1)
        yc = _gelu_tanh(jnp.dot(u_s[s, 0:n_ctx, :], kin_s[s], preferred_element_type=F32)
                        + jnp.dot(hc, cm_s[s], preferred_element_type=F32))
        ycs += [yc[:, lanes(k)] for k in range(gs)]
    ys = swapped(ys, stages[::-1])
    ycs = swapped(ycs, stages[::-1])
    for t in range(T):
        z_ref[pl.ds(t, n_lat, stride=T), :] = ys[t][n_ctx:n_ctx + n_lat, :]
        zc_ref[pl.ds(t, n_ctx, stride=T), :] = ycs[t]


def _ssm_weights(a_re, a_im, log_dt, b_re, b_im, c_re, c_im, d_skip, seg_len):
    T = SSM_T
    g, p = a_re.shape[1], a_re.shape[2]
    hc = b_re.shape[-1]
    gpt = LANES // hc
    J = g // gpt
    lam = lax.complex(a_re.astype(F32), a_im.astype(F32))
    dt = jnp.exp(log_dt.astype(F32))[..., None]
    ldt = lam * dt
    lam_bar = jnp.exp(ldt)
    bbar = ((lam_bar - 1) / lam)[..., None] * lax.complex(b_re.astype(F32), b_im.astype(F32))
    cmat = lax.complex(c_re.astype(F32), c_im.astype(F32))
    kk = jnp.arange(T + 1, dtype=F32)
    pw = jnp.exp(ldt[:, None] * kk[None, :, None, None])

    wf = pw[0, :T][::-1][:, :, None, :] * jnp.swapaxes(bbar[0], 1, 2)[None]
    wb = pw[1, :T][:, :, None, :] * jnp.swapaxes(bbar[1], 1, 2)[None]
    wq = jnp.stack([wf.real, wf.imag, wb.real, wb.imag]).reshape(4, T, J, gpt, hc, p)
    wc = wq.transpose(2, 1, 4, 0, 3, 5).reshape(J, T * hc, 4 * gpt * p)

    df = cmat[0][None] * pw[0, 1:][:, :, None, :]
    db = cmat[1][None] * pw[1, 1:][::-1][:, :, None, :]
    dq = jnp.stack([df.real, -df.imag, db.real, -db.imag]).reshape(4, T, J, gpt, hc, p)
    dc = dq.transpose(2, 0, 3, 5, 1, 4).reshape(J, 4 * gpt * p, T * hc)

    mf = jnp.einsum('gcp,kgp,gph->khgc', cmat[0], pw[0, :T], bbar[0]).real
    mb = jnp.einsum('gcp,kgp,gph->khgc', cmat[1], pw[1, :T], bbar[1]).real
    skip = jnp.eye(hc, dtype=F32)[:, None, :] * d_skip.astype(F32).reshape(g, hc)[None]
    m0 = mf[0] + mb[0] + skip
    lags = jnp.concatenate([mb[1:][::-1], m0[None], mf[1:]], axis=0)
    kc = lags.reshape(2 * T - 1, hc, J, LANES).transpose(2, 1, 0, 3).reshape(J, hc, (2 * T - 1) * LANES)

    def lam_rows(power):
        lp = jnp.exp(ldt * power)
        lt = lp.reshape(2, J, gpt * p)
        return jnp.concatenate([lt[0].real, lt[0].imag, lt[1].real, lt[1].imag], axis=-1)[:, None, :]

    lam_t = jnp.broadcast_to(lam_rows(float(T)), (J, SSM_SEG, 4 * gpt * p))
    return wc.astype(BF), kc.astype(BF), dc.astype(BF), lam_t, lam_rows(float(T * seg_len))


def _ssm_expanders(gs):
    r = np.arange(gs * LANES)
    k, slot, ch = r // LANES, (r % LANES) // SSM_GROUP, r % SSM_GROUP
    t = (slot // gs) * gs + k
    rexp = np.zeros((gs * LANES, SSM_T * SSM_GROUP), np.float32)
    rexp[r, t * SSM_GROUP + ch] = 1.0
    return jnp.asarray(rexp, BF), jnp.asarray(rexp.T, BF)


def _ssm(u3, u3c, wc, kc, dc, lam, lamseg, *, layer, batch, seq, n_ctx):
    T = SSM_T
    J = u3.shape[0]
    ns = wc.shape[-1]
    nsets = 2 ** len(SSM_STAGES)
    gs = T // nsets
    sw = gs * LANES
    n_lat, n_c = seq // T, n_ctx // T
    rows = n_lat + 2 * n_c
    seg_len = rows // SSM_SEG
    assert rows % SSM_SEG == 0 and n_c % 16 == 0
    rexp, cexp = _ssm_expanders(gs)
    kern = functools.partial(_ssm_kernel, seg_len=seg_len, n_lat=n_lat, n_ctx=n_c, stages=SSM_STAGES)
    const = lambda j, b: (0, 0)
    per_j = lambda j, b: (layer, j, 0, 0)
    per_jb = lambda j, b: (j, b, 0)
    return pl.pallas_call(
        kern,
        grid=(J, batch),
        in_specs=[pl.BlockSpec((None, seq, LANES), per_jb),
                  pl.BlockSpec((None, n_ctx, LANES), per_jb),
                  pl.BlockSpec((None, None) + wc.shape[2:], per_j),
                  pl.BlockSpec((None, None) + kc.shape[2:], per_j),
                  pl.BlockSpec((None, None) + dc.shape[2:], per_j),
                  pl.BlockSpec(rexp.shape, const),
                  pl.BlockSpec(cexp.shape, const),
                  pl.BlockSpec((None, None, SSM_SEG, ns), per_j),
                  pl.BlockSpec((None, None, 1, ns), per_j)],
        out_specs=[pl.BlockSpec((None, seq, LANES), per_jb),
                   pl.BlockSpec((None, n_ctx, LANES), per_jb)],
        out_shape=[jax.ShapeDtypeStruct(u3.shape, F32),
                   jax.ShapeDtypeStruct(u3c.shape, F32)],
        scratch_shapes=[pltpu.VMEM((nsets, rows, sw), BF),
                        pltpu.VMEM((nsets, sw, ns // nsets), BF),
                        pltpu.VMEM((nsets, sw, sw), BF),
                        pltpu.VMEM((nsets, ns // nsets, sw), BF),
                        pltpu.VMEM((ns // LANES, rows, LANES), F32),
                        pltpu.VMEM((ns // LANES, rows, LANES), F32),
                        pltpu.VMEM((ns // LANES, 2 * SSM_SEG, LANES), F32)],
        compiler_params=_cparams("arbitrary", "arbitrary"),
        name="ssm_chunk_scan",
    )(u3, u3c, wc, kc, dc, rexp, cexp, lam, lamseg)


def _dft_a_kernel(w_ref, x_ref, o_ref):
    o_ref[...] = jnp.dot(w_ref[...], x_ref[...], preferred_element_type=F32).astype(BF)


def _dft_rows(n):
    k = np.arange(n)
    m = (k[:, None] * k[None, :]) % n
    ang = m * (2.0 * math.pi / n)
    return jnp.asarray(np.concatenate([np.cos(ang), -np.sin(ang)], axis=0), BF)


def _dft_a(w, x3):
    B, n, cols = x3.shape
    tn = min(cols, 4096)
    return pl.pallas_call(
        _dft_a_kernel,
        grid=(B, cols // tn),
        in_specs=[pl.BlockSpec((2 * n, n), lambda b, i: (0, 0)),
                  pl.BlockSpec((None, n, tn), lambda b, i: (b, 0, i))],
        out_specs=pl.BlockSpec((None, 2 * n, tn), lambda b, i: (b, 0, i)),
        out_shape=jax.ShapeDtypeStruct((B, 2 * n, cols), BF),
        compiler_params=_cparams("parallel", "parallel"),
        name="dft_stage_a",
    )(w, x3)


def _store_lane_tiles(o_ref, p, n, wf, idx):
    per = wf // LANES
    for c in range(2 * per):
        part = p[(c // per) * n:(c // per + 1) * n, (c % per) * LANES:(c % per + 1) * LANES]
        o_ref[(c,) + idx] = part


def _dft_b_kernel(a_ref, g_ref, o_ref, *, kb, l1, wf):
    for i in range(kb):
        g = jnp.concatenate([g_ref[0, i], g_ref[1, i]], axis=0)
        p = jnp.dot(a_ref[i], g, preferred_element_type=F32)
        _store_lane_tiles(o_ref, p, l1, wf, (i,))


def _dft_b_mats(seq, l1, l2):
    k2 = np.arange(l2)[:, None, None]
    k1 = np.arange(l1)[None, :, None]
    j1 = np.arange(l1)[None, None, :]
    m = ((k1 * l2 + k2) * j1) % seq
    ang = m * (2.0 * math.pi / seq)
    ar, ai = np.cos(ang).astype(np.float32), -np.sin(ang).astype(np.float32)
    top = np.concatenate([ar, -ai], axis=2)
    bot = np.concatenate([ai, ar], axis=2)
    return jnp.asarray(np.concatenate([top, bot], axis=1), BF)


def _dft_b(amat, g5, *, kb):
    B, _, l2, l1, wf = g5.shape
    nlt = 2 * wf // LANES
    kern = functools.partial(_dft_b_kernel, kb=kb, l1=l1, wf=wf)
    return pl.pallas_call(
        kern,
        grid=(B, l2 // kb),
        in_specs=[pl.BlockSpec((kb, 2 * l1, 2 * l1), lambda b, i: (i, 0, 0)),
                  pl.BlockSpec((None, 2, kb, l1, wf), lambda b, i: (b, 0, i, 0, 0))],
        out_specs=pl.BlockSpec((None, nlt, kb, l1, LANES), lambda b, i: (b, 0, i, 0, 0)),
        out_shape=jax.ShapeDtypeStruct((B, nlt, l2, l1, LANES), F32),
        compiler_params=_cparams("parallel", "parallel"),
        name="dft_stage_b",
    )(amat, g5)


def _dft_ctx_kernel(w_ref, x_ref, o_ref, *, n, wf):
    p = jnp.dot(w_ref[...], x_ref[...], preferred_element_type=F32)
    _store_lane_tiles(o_ref, p, n, wf, ())


def _dft_ctx(w, x3):
    B, n, wf = x3.shape
    nlt = 2 * wf // LANES
    kern = functools.partial(_dft_ctx_kernel, n=n, wf=wf)
    return pl.pallas_call(
        kern,
        grid=(B,),
        in_specs=[pl.BlockSpec((2 * n, n), lambda b: (0, 0)),
                  pl.BlockSpec((None, n, wf), lambda b: (b, 0, 0))],
        out_specs=pl.BlockSpec((None, nlt, n, LANES), lambda b: (b, 0, 0, 0)),
        out_shape=jax.ShapeDtypeStruct((B, nlt, n, LANES), F32),
        compiler_params=_cparams("parallel"),
        name="dft_ctx",
    )(w, x3)


def _channel_dft(wf, seq):
    c = np.arange(wf)
    same = (c[:, None] // FNET_GROUP) == (c[None, :] // FNET_GROUP)
    m = ((c[:, None] % FNET_GROUP) * (c[None, :] % FNET_GROUP)) % FNET_GROUP
    ang = m * (2.0 * math.pi / FNET_GROUP)
    mask = same / math.sqrt(seq * FNET_GROUP)
    return jnp.asarray(np.concatenate([np.cos(ang) * mask, np.sin(ang) * mask], axis=0), BF)


def _softmax_pv(parts, l_shape):
    m = parts[0][0].max(axis=-1, keepdims=True)
    for s, _ in parts[1:]:
        m = jnp.maximum(m, s.max(axis=-1, keepdims=True))
    l = jnp.zeros(l_shape, F32)
    o = None
    for s, v in parts:
        p = jnp.exp2(s - m)
        l = l + p.sum(axis=-1, keepdims=True)
        pv = jnp.dot(p.astype(BF), v, preferred_element_type=F32)
        o = pv if o is None else o + pv
    return o / l


def _na_kernel(kb_ref, pid_ref, dr_ref, q_ref, k_ref, v_ref, kc_ref, vc_ref, cbx_ref, o_ref, bias_ref,
               *, nkeys, heads, nsub):
    step = pl.program_id(1)
    nq = q_ref.shape[0] // nsub
    rq, kr = nq // GRID_W, nkeys // GRID_W
    lane = lax.broadcasted_iota(jnp.int32, (nq, LANES), 1)
    low = lane < HEAD_DIM
    dn = (((1,), (1,)), ((), ()))

    for sb in range(nsub):
        i = step * nsub + sb
        pid = pid_ref[i]

        @pl.when((step == 0) | (pid != pid_ref[jnp.maximum(i - nsub, 0)]))
        def _build_bias(sb=sb, pid=pid):
            for qi in range(rq):
                for w in range(kr):
                    a = dr_ref[(pid * rq + qi) * kr + w]
                    for h in range(heads):
                        bias_ref[sb, h, qi * GRID_W:(qi + 1) * GRID_W, w * GRID_W:(w + 1) * GRID_W] = cbx_ref[h, a]

    tasks = [(sb, h) for sb in range(nsub) for h in range(heads)]
    starts = [pl.multiple_of(kb_ref[step * nsub + sb] * GRID_W, GRID_W) for sb in range(nsub)]

    def scores(sb, h):
        sl = slice((h // 2) * LANES, (h // 2 + 1) * LANES)
        q2 = q_ref[sb * nq:(sb + 1) * nq, sl]
        qm = jnp.where(low if h % 2 == 0 else jnp.logical_not(low), q2, jnp.zeros_like(q2))
        s_w = lax.dot_general(qm, k_ref[pl.ds(starts[sb], nkeys), sl], dn, preferred_element_type=F32)
        s_c = lax.dot_general(qm, kc_ref[:, sl], dn, preferred_element_type=F32)
        return s_w + bias_ref[sb, h], s_c

    def weights(s_w, s_c):
        m = jnp.maximum(s_w.max(axis=-1, keepdims=True), s_c.max(axis=-1, keepdims=True))
        p_w = jnp.exp2(s_w - m)
        p_c = jnp.exp2(s_c - m)
        l = p_w.sum(axis=-1, keepdims=True) + p_c.sum(axis=-1, keepdims=True)
        return p_w.astype(BF), p_c.astype(BF), l

    outs = {}

    def finish(sb, h, p_w, p_c, l):
        sl = slice((h // 2) * LANES, (h // 2 + 1) * LANES)
        o = (jnp.dot(p_w, v_ref[pl.ds(starts[sb], nkeys), sl], preferred_element_type=F32)
             + jnp.dot(p_c, vc_ref[:, sl], preferred_element_type=F32))
        outs[h % 2] = o / l
        if h % 2 == 1:
            o_ref[sb * nq:(sb + 1) * nq, sl] = jnp.where(low, outs[0], outs[1]).astype(BF)

    nt = len(tasks)
    sc = {0: scores(*tasks[0])}
    if nt > 1:
        sc[1] = scores(*tasks[1])
    pw = {0: weights(*sc.pop(0))}
    for n in range(nt):
        if n + 2 < nt:
            sc[n + 2] = scores(*tasks[n + 2])
        if n + 1 < nt:
            pw[n + 1] = weights(*sc.pop(n + 1))
        finish(*tasks[n], *pw.pop(n))


def _na_plan(rows, rq):
    wr = min(NA_ROWS, rows)
    kr = rq + wr - 1
    nblk = rows // rq
    kbs, pids, pats = [], [], []
    for blk in range(nblk):
        r0 = blk * rq
        rs = [int(np.clip(r0 + i - wr // 2, 0, rows - wr)) for i in range(rq)]
        kb = min(rs[0], rows - kr)
        pat = (tuple(r - kb for r in rs), r0 - kb)
        if pat not in pats:
            pats.append(pat)
        kbs.append(kb)
        pids.append(pats.index(pat))
    return np.asarray(kbs, np.int32), np.asarray(pids, np.int32), pats, kr, wr


def _na_bias(rpb, pats, rq, kr, wr):
    cols = np.arange(GRID_W)
    cstart = np.clip(cols - NA_COLS // 2, 0, GRID_W - NA_COLS)
    kc = np.arange(GRID_W)
    col_ok = (kc[None, :] >= cstart[:, None]) & (kc[None, :] < cstart[:, None] + NA_COLS)
    dc = kc[None, :] - cols[:, None] + NA_COLS - 1
    depth, heads, nr, ncol = rpb.shape
    sel = (dc[:, :, None] == np.arange(ncol)[None, None, :]) & col_ok[:, :, None]
    cb = jnp.einsum('lhab,ckb->lhack', rpb.astype(F32), jnp.asarray(sel, F32), precision=lax.Precision.HIGHEST)
    cb = jnp.where(jnp.asarray(col_ok)[None, None, None], cb * math.log2(math.e), NEG)
    cbx = jnp.concatenate([cb, jnp.full((depth, heads, 1, GRID_W, GRID_W), NEG, F32)], axis=2)
    slots = []
    for rs_off, r_off in pats:
        i = np.arange(rq)[:, None]
        w = np.arange(kr)[None, :]
        rso = np.asarray(rs_off)[:, None]
        row_ok = (w >= rso) & (w < rso + wr)
        slots.append(np.where(row_ok, w - r_off - i + NA_ROWS - 1, nr))
    return cbx, np.stack(slots).reshape(-1).astype(np.int32)


def _na_tables(rpb_all, seq, rq):
    kbs, pids, pats, kr, wr = _na_plan(seq // GRID_W, rq)
    cbx, slots = _na_bias(rpb_all, pats, rq, kr, wr)
    return dict(kbs=jnp.asarray(kbs), pids=jnp.asarray(pids), slots=jnp.asarray(slots), cbx=cbx, kr=kr, rq=rq)


def _na_latent(q, k, v, kc, vc, tabs, layer, *, batch, seq, n_ctx):
    rows = seq // GRID_W
    cbx, rq, kr = tabs["cbx"], tabs["rq"], tabs["kr"]
    heads = cbx.shape[1]
    nblk = rows // rq
    nq = rq * GRID_W
    nkeys = kr * GRID_W
    w = q.shape[1]
    nsub = 2 if nblk % 2 == 0 else 1
    nstep = nblk // nsub
    kern = functools.partial(_na_kernel, nkeys=nkeys, heads=heads, nsub=nsub)
    grid_spec = pltpu.PrefetchScalarGridSpec(
        num_scalar_prefetch=3,
        grid=(batch, nstep),
        in_specs=[pl.BlockSpec((nsub * nq, w), lambda b, i, *_: (b * nstep + i, 0)),
                  pl.BlockSpec((seq, w), lambda b, i, *_: (b, 0)),
                  pl.BlockSpec((seq, w), lambda b, i, *_: (b, 0)),
                  pl.BlockSpec((n_ctx, w), lambda b, i, *_: (b, 0)),
                  pl.BlockSpec((n_ctx, w), lambda b, i, *_: (b, 0)),
                  pl.BlockSpec((None,) + cbx.shape[1:], lambda b, i, *_: (layer, 0, 0, 0, 0))],
        out_specs=pl.BlockSpec((nsub * nq, w), lambda b, i, *_: (b * nstep + i, 0)),
        scratch_shapes=[pltpu.VMEM((nsub, heads, nq, nkeys), F32)],
    )
    return pl.pallas_call(
        kern,
        grid_spec=grid_spec,
        out_shape=jax.ShapeDtypeStruct(q.shape, BF),
        compiler_params=_cparams("arbitrary", "arbitrary"),
        name="na_latent",
    )(tabs["kbs"], tabs["pids"], tabs["slots"], q, k, v, kc, vc, cbx)


def _na_ctx_kernel(q_ref, k_ref, v_ref, o_ref, *, heads):
    nq = q_ref.shape[0]
    lane = lax.broadcasted_iota(jnp.int32, (nq, LANES), 1)
    low = lane < HEAD_DIM
    dn = (((1,), (1,)), ((), ()))
    for hp in range(heads // 2):
        sl = slice(hp * LANES, (hp + 1) * LANES)
        q2 = q_ref[:, sl]
        k2 = k_ref[:, sl]
        v2 = v_ref[:, sl]
        outs = []
        for hh in range(2):
            qm = jnp.where(low if hh == 0 else jnp.logical_not(low), q2, jnp.zeros_like(q2))
            s = lax.dot_general(qm, k2, dn, preferred_element_type=F32)
            outs.append(_softmax_pv([(s, v2)], (nq, 1)))
        o_ref[:, sl] = jnp.where(low, outs[0], outs[1]).astype(BF)


def _na_ctx(qc, kc, vc, *, batch, n_ctx, heads):
    w = qc.shape[1]
    spec = pl.BlockSpec((n_ctx, w), lambda b: (b, 0))
    return pl.pallas_call(
        functools.partial(_na_ctx_kernel, heads=heads),
        grid=(batch,),
        in_specs=[spec, spec, spec],
        out_specs=spec,
        out_shape=jax.ShapeDtypeStruct(qc.shape, BF),
        compiler_params=_cparams("parallel"),
        name="na_ctx",
    )(qc, kc, vc)


def _outproj_kernel(z_ref, pf_ref, na_ref, x_ref, gate_ref, gpost_ref,
                    wglu_ref, cs_ref, wfo_ref, wout_ref, o_ref, *, widths, sub):
    w_ssm, w_fn, w_na = widths
    wglu, wfo, wout = (r[...].astype(BF) for r in (wglu_ref, wfo_ref, wout_ref))
    for r0 in range(0, x_ref.shape[0], sub):
        rs = slice(r0, r0 + sub)
        z = jnp.concatenate([z_ref[j, rs, :] for j in range(w_ssm // LANES)], axis=1)
        gl = jnp.dot(z.astype(BF), wglu, preferred_element_type=F32)
        y_ssm = (z * jax.nn.sigmoid(gl)).astype(BF)
        if len(pf_ref.shape) == 3:
            tiles = [pf_ref[c, rs, :] for c in range(pf_ref.shape[0])]
        else:
            l2 = pf_ref.shape[1]
            tiles = [jnp.concatenate([pf_ref[c, :, k1, :] for k1 in range(r0 // l2, (r0 + sub) // l2)], axis=0)
                     for c in range(pf_ref.shape[0])]
        pf = jnp.concatenate(tiles, axis=1).astype(BF)
        mixed = jnp.dot(pf, cs_ref[...], preferred_element_type=F32).astype(BF)
        y_fft = jnp.dot(mixed, wfo, preferred_element_type=F32).astype(BF)
        y = jnp.concatenate([y_ssm, y_fft, na_ref[rs, :]], axis=1)
        o = jnp.dot(y, wout, preferred_element_type=F32)
        ms = jnp.mean(o * o, axis=-1, keepdims=True)
        o_ref[rs, :] = x_ref[rs, :] + gate_ref[...] * (o * lax.rsqrt(ms + EPS) * gpost_ref[...])


def _outproj(z3, pf, yna, x2d, mod3, g_post, w_glu, cs, w_fo, w_out, *,
             layer, tm, tiles_per_batch, mod_row, widths):
    n, d = x2d.shape
    w_ssm, w_fn, w_na = widths
    if mod_row is None:
        row = lambda i: i // tiles_per_batch
    else:
        row = lambda i: mod_row
    kern = functools.partial(_outproj_kernel, widths=widths, sub=min(tm, 512))
    return pl.pallas_call(
        kern,
        grid=(n // tm,),
        in_specs=[pl.BlockSpec((w_ssm // LANES, tm, LANES), lambda i: (0, i, 0)),
                  (pl.BlockSpec((None, pf.shape[1], tm, LANES),
                                lambda i: (i // tiles_per_batch, 0, i % tiles_per_batch, 0))
                   if pf.ndim == 4 else
                   pl.BlockSpec((None, pf.shape[1], pf.shape[2], tm // pf.shape[2], LANES),
                                lambda i: (i // tiles_per_batch, 0, 0, i % tiles_per_batch, 0))),
                  pl.BlockSpec((tm, w_na), lambda i: (i, 0)),
                  pl.BlockSpec((tm, d), lambda i: (i, 0)),
                  pl.BlockSpec((None, 1, d), lambda i: (row(i), 0, 2)),
                  pl.BlockSpec((None, 1, d), lambda i: (layer, 0, 0)),
                  _layer_resident(w_glu, layer),
                  _resident(cs.shape, lambda i: (0, 0)),
                  _layer_resident(w_fo, layer),
                  _layer_resident(w_out, layer)],
        out_specs=pl.BlockSpec((tm, d), lambda i: (i, 0)),
        out_shape=jax.ShapeDtypeStruct((n, d), F32),
        compiler_params=_cparams("parallel"),
        name="outproj",
    )(z3, pf, yna, x2d, mod3, g_post.reshape(-1, 1, d), w_glu, cs, w_fo, w_out)


def _ffn_kernel(x_ref, sh_ref, sc_ref, gate_ref, gpre_ref, gpost_ref, wg_ref, wu_ref, wd_ref,
                o_ref, a_ref, *, sub, chunk):
    dff = wg_ref.shape[1]
    for r0 in range(0, x_ref.shape[0], sub):
        rs = slice(r0, r0 + sub)
        x = x_ref[rs, :]
        ms = jnp.mean(x * x, axis=-1, keepdims=True)
        m = ((x * lax.rsqrt(ms + EPS) * gpre_ref[...]) * (1.0 + sc_ref[...]) + sh_ref[...]).astype(BF)
        for c0 in range(0, dff, chunk):
            sl = slice(c0, min(c0 + chunk, dff))
            g = jnp.dot(m, wg_ref[:, sl], preferred_element_type=F32)
            u = jnp.dot(m, wu_ref[:, sl], preferred_element_type=F32)
            a_ref[rs, sl] = (g * jax.nn.sigmoid(g) * u).astype(BF)
        y = jnp.dot(a_ref[rs, :], wd_ref[...], preferred_element_type=F32)
        ms2 = jnp.mean(y * y, axis=-1, keepdims=True)
        o_ref[rs, :] = x + gate_ref[...] * (y * lax.rsqrt(ms2 + EPS) * gpost_ref[...])


def _ffn_pair_kernel(x_ref, xc_ref, sh_ref, sc_ref, gate_ref, gpre_ref, gpost_ref, wg_ref, wu_ref, wd_ref,
                     o_ref, oc_ref, a_ref, *, sub, chunk, n_lat_tiles):
    i = pl.program_id(0)
    common = (sh_ref, sc_ref, gate_ref, gpre_ref, gpost_ref, wg_ref, wu_ref, wd_ref)

    @pl.when(i < n_lat_tiles)
    def _latent():
        _ffn_kernel(x_ref, *common, o_ref, a_ref, sub=sub, chunk=chunk)

    @pl.when(i >= n_lat_tiles)
    def _context():
        _ffn_kernel(xc_ref, *common, oc_ref, a_ref, sub=min(sub, xc_ref.shape[0]), chunk=chunk)


def _ffn(x2d, xc2d, mod3, g_pre, g_post, wg, wu, wd, *, layer, tm, tiles_per_batch, ctx_row):
    n, d = x2d.shape
    dff = wg.shape[2]
    nl = n // tm
    gains = (g_pre.reshape(-1, 1, d), g_post.reshape(-1, 1, d))
    weights = [_layer_resident(wg, layer), _layer_resident(wu, layer), _layer_resident(wd, layer)]
    gain_specs = [pl.BlockSpec((None, 1, d), lambda i: (layer, 0, 0))] * 2
    if xc2d is None:
        row = lambda i: i // tiles_per_batch
        return pl.pallas_call(
            functools.partial(_ffn_kernel, sub=min(tm, 512), chunk=3 * MXU_TILE),
            grid=(nl,),
            in_specs=[pl.BlockSpec((tm, d), lambda i: (i, 0))]
                     + [pl.BlockSpec((None, 1, d), functools.partial(lambda i, c: (row(i), 0, c), c=c))
                        for c in (3, 4, 5)] + gain_specs + weights,
            out_specs=pl.BlockSpec((tm, d), lambda i: (i, 0)),
            out_shape=jax.ShapeDtypeStruct((n, d), F32),
            scratch_shapes=[pltpu.VMEM((tm, dff), BF)],
            compiler_params=_cparams("parallel"),
            name="ffn",
        )(x2d, mod3, mod3, mod3, *gains, wg, wu, wd), None
    nc = xc2d.shape[0]
    row = lambda i: jnp.where(i < nl, i // tiles_per_batch, ctx_row)
    lat = lambda i: (jnp.minimum(i, nl - 1), 0)
    ctx = lambda i: (jnp.maximum(i - nl, 0), 0)
    return pl.pallas_call(
        functools.partial(_ffn_pair_kernel, sub=min(tm, 512), chunk=3 * MXU_TILE, n_lat_tiles=nl),
        grid=(nl + 1,),
        in_specs=[pl.BlockSpec((tm, d), lat), pl.BlockSpec((nc, d), ctx)]
                 + [pl.BlockSpec((None, 1, d), functools.partial(lambda i, c: (row(i), 0, c), c=c))
                    for c in (3, 4, 5)] + gain_specs + weights,
        out_specs=[pl.BlockSpec((tm, d), lat), pl.BlockSpec((nc, d), ctx)],
        out_shape=[jax.ShapeDtypeStruct((n, d), F32), jax.ShapeDtypeStruct((nc, d), F32)],
        scratch_shapes=[pltpu.VMEM((tm, dff), BF)],
        compiler_params=_cparams("arbitrary"),
        name="ffn_pair",
    )(x2d, xc2d, mod3, mod3, mod3, *gains, wg, wu, wd)


def _layer(x2d, xc2d, p, w, layer, *, batch, seq, n_ctx, last, rope_tabs, consts):
    d = x2d.shape[1]
    w_ssm = consts["w_ssm"]
    w_fn = w["w_fourier"].shape[1]
    w_na = (w["w_in"].shape[2] - w_ssm - w_fn) // 3
    widths = (w_ssm, w_fn, w_na)
    heads = w_na // HEAD_DIM
    tm = min(512, seq)
    tmc = min(512, batch * n_ctx)

    mod = w["mod"][layer]
    mod3 = mod.reshape(mod.shape[0], 1, 6 * d)

    cos, sin = rope_tabs
    tm2 = min(2 * tm, seq)
    u3, f, q, k, v = _inproj(x2d, mod3, p["g_pre_mix"], w["w_in"], cos, sin, layer=layer, tm=tm2,
                             tiles_per_batch=seq // tm2, mod_row=None, widths=widths, rope=True)
    ones = jnp.ones((tmc, LANES), F32)
    u3c, fc, qc, kc, vc = _inproj(xc2d, mod3, p["g_pre_mix"], w["w_in"], ones, ones, layer=layer, tm=tmc,
                                  tiles_per_batch=1, mod_row=batch, widths=widths, rope=False)

    z3, zc3 = _ssm(u3, u3c, *w["ssm"], layer=layer, batch=batch, seq=seq, n_ctx=n_ctx)

    l2 = consts["l2"]
    l1 = seq // l2
    g = _dft_a(consts["dft_a"], f.reshape(batch, l2, l1 * w_fn))
    pf = _dft_b(consts["dft_b"], g.reshape(batch, 2, l2, l1, w_fn), kb=min(8, l2))

    yna = _na_latent(q, k, v, kc, vc, consts["na"], layer, batch=batch, seq=seq, n_ctx=n_ctx)

    w_glu, w_fo, w_out = w["w_glu"], w["w_fourier"], w["w_out"]
    x2d = _outproj(z3, pf, yna, x2d, mod3, p["g_post_mix"], w_glu, consts["cs_lat"], w_fo, w_out,
                   layer=layer, tm=tm2, tiles_per_batch=seq // tm2, mod_row=None, widths=widths)
    if not last:
        pfc = _dft_ctx(consts["dft_ctx"], fc.reshape(batch, n_ctx, w_fn))
        ynac = _na_ctx(qc, kc, vc, batch=batch, n_ctx=n_ctx, heads=heads)
        tc = min(tmc, n_ctx)
        xc2d = _outproj(zc3, pfc, ynac, xc2d, mod3, p["g_post_mix"], w_glu, consts["cs_ctx"], w_fo, w_out,
                        layer=layer, tm=tc, tiles_per_batch=n_ctx // tc, mod_row=batch, widths=widths)
    else:
        xc2d = None
    wg, wu, wd = w["w_ffn_gate"], w["w_ffn_up"], w["w_ffn_down"]
    return _ffn(x2d, xc2d, mod3, p["g_pre_ffn"], p["g_post_ffn"], wg, wu, wd,
                layer=layer, tm=tm2, tiles_per_batch=seq // tm2, ctx_row=batch)


def kernel(x, c, ctx, c_ctx, w_mod, b_mod, g_pre_mix, g_post_mix, w_in, ssm_a_re, ssm_a_im, ssm_log_dt,
           ssm_b_re, ssm_b_im, ssm_c_re, ssm_c_im, ssm_d, w_glu, w_fourier, na_rpb, w_out, g_pre_ffn,
           g_post_ffn, w_ffn_gate, w_ffn_up, w_ffn_down):
    batch, seq, d = x.shape
    n_ctx = ctx.shape[1]
    depth = w_mod.shape[0]
    w_fn = w_fourier.shape[1]
    params = dict(g_pre_mix=g_pre_mix, g_post_mix=g_post_mix, g_pre_ffn=g_pre_ffn, g_post_ffn=g_post_ffn)
    seg_len = (seq + 2 * n_ctx) // SSM_T // SSM_SEG
    ssm_ops = jax.vmap(functools.partial(_ssm_weights, seg_len=seg_len))(
        ssm_a_re, ssm_a_im, ssm_log_dt, ssm_b_re, ssm_b_im, ssm_c_re, ssm_c_im, ssm_d)
    nrow = -(-(batch + 1) // 8) * 8
    cstack = jnp.concatenate([c, c_ctx[None, :], jnp.zeros((nrow - batch - 1, d), c.dtype)], axis=0)
    weights = dict(mod=_mod_rows(cstack, w_mod, b_mod), ssm=ssm_ops,
                   w_in=w_in, w_glu=w_glu, w_fourier=w_fourier, w_out=w_out,
                   w_ffn_gate=_cast_bf16(w_ffn_gate),
                   w_ffn_up=_cast_bf16(w_ffn_up), w_ffn_down=_cast_bf16(w_ffn_down))

    l2 = 64 if seq % (64 * 8) == 0 else 8
    l1 = seq // l2
    consts = dict(
        l2=l2,
        w_ssm=ssm_d.shape[1],
        na=_na_tables(na_rpb, seq, 4),
        dft_a=_dft_rows(l2).astype(BF),
        dft_b=_dft_b_mats(seq, l1, l2).astype(BF),
        dft_ctx=_dft_rows(n_ctx).astype(BF),
        cs_lat=_channel_dft(w_fn, seq).astype(BF),
        cs_ctx=_channel_dft(w_fn, n_ctx).astype(BF),
    )
    rope_tabs = _rope_tables(seq)

    x2d = x.reshape(batch * seq, d)
    xc2d = ctx.reshape(batch * n_ctx, d)
    for layer in range(depth):
        x2d, xc2d = _layer(x2d, xc2d, params, weights, layer, batch=batch, seq=seq, n_ctx=n_ctx,
                           last=(layer == depth - 1), rope_tabs=rope_tabs, consts=consts)
    return x2d.reshape(batch, seq, d)
```

```python
import functools
import math

import numpy as np
import jax
import jax.numpy as jnp
from jax import lax
from jax.experimental import pallas as pl
from jax.experimental.pallas import tpu as pltpu

BF = jnp.bfloat16
F32 = jnp.float32

EPS = 1e-6
GRID_W = 64
HEAD_DIM = 64
NA_ROWS = 8
NA_COLS = 16
ROPE_BASE = 10000.0
SSM_GROUP = 16
SSM_STATE = 64
FNET_GROUP = 64
LANES = 128
MXU_TILE = 256
SSM_T = 8
SSM_SEG = 8
SSM_STAGES = (4, 2)
NEG = -1e30
VMEM_LIMIT = 56 * 1024 * 1024


def _cparams(*sem):
    return pltpu.CompilerParams(dimension_semantics=sem, vmem_limit_bytes=VMEM_LIMIT)


def _resident(shape, index_map):
    return pl.BlockSpec(shape, index_map, pipeline_mode=pl.Buffered(1))


def _layer_resident(w_all, layer):
    return pl.BlockSpec((None,) + w_all.shape[1:], lambda *_: (layer, 0, 0), pipeline_mode=pl.Buffered(1))


def _mod_kernel(c_ref, w_ref, b_ref, o_ref):
    cs = c_ref[...]
    s = (cs * jax.nn.sigmoid(cs)).astype(BF)
    o_ref[...] = jnp.dot(s, w_ref[...].astype(BF), preferred_element_type=F32) + b_ref[...]


def _mod_rows(cstack, w_mod, b_mod):
    rows, d = cstack.shape
    depth, _, n = w_mod.shape
    bn = 2 * d if n % (2 * d) == 0 else d
    return pl.pallas_call(
        _mod_kernel,
        grid=(depth, n // bn),
        in_specs=[pl.BlockSpec((rows, d), lambda l, i: (0, 0)),
                  pl.BlockSpec((None, d, bn), lambda l, i: (l, 0, i)),
                  pl.BlockSpec((None, 1, bn), lambda l, i: (l, 0, i))],
        out_specs=pl.BlockSpec((None, rows, bn), lambda l, i: (l, 0, i)),
        out_shape=jax.ShapeDtypeStruct((depth, rows, n), F32),
        compiler_params=_cparams("parallel", "parallel"),
        name="mod_rows",
    )(cstack, w_mod, b_mod.reshape(depth, 1, n))


def _rope_tile(t, cos, sin, first):
    partner = jnp.where(first, pltpu.roll(t, LANES - 16, 1), pltpu.roll(t, 16, 1))
    return t * cos + partner * sin


def _inproj_kernel(x_ref, sh_ref, sc_ref, g_ref, w_ref, cos_ref, sin_ref,
                   u_ref, f_ref, q_ref, k_ref, v_ref, *, widths, rope, sub):
    w_ssm, w_fn, w_na = widths
    scale = HEAD_DIM ** -0.5 * math.log2(math.e)
    w = w_ref[...].astype(BF)
    for r0 in range(0, x_ref.shape[0], sub):
        rs = slice(r0, r0 + sub)
        x = x_ref[rs, :]
        ms = jnp.mean(x * x, axis=-1, keepdims=True)
        xn = x * lax.rsqrt(ms + EPS) * g_ref[...]
        m = (xn * (1.0 + sc_ref[...]) + sh_ref[...]).astype(BF)

        h = jnp.dot(m, w, preferred_element_type=F32)

        def proj(lo, n, h=h):
            return h[:, lo:lo + n]

        u = proj(0, w_ssm)
        for j in range(w_ssm // LANES):
            u_ref[j, rs, :] = u[:, j * LANES:(j + 1) * LANES]
        f_ref[rs, :] = proj(w_ssm, w_fn).astype(BF)
        q = proj(w_ssm + w_fn, w_na)
        k = proj(w_ssm + w_fn + w_na, w_na)
        v_ref[rs, :] = proj(w_ssm + w_fn + 2 * w_na, w_na).astype(BF)
        if rope:
            cos = cos_ref[rs, :]
            sin = sin_ref[rs, :]
            lane = lax.broadcasted_iota(jnp.int32, cos.shape, 1)
            first = (lane % 32) < 16
            for j in range(w_na // LANES):
                sl = slice(j * LANES, (j + 1) * LANES)
                q_ref[rs, sl] = (_rope_tile(q[:, sl], cos, sin, first) * scale).astype(BF)
                k_ref[rs, sl] = _rope_tile(k[:, sl], cos, sin, first).astype(BF)
        else:
            q_ref[rs, :] = (q * scale).astype(BF)
            k_ref[rs, :] = k.astype(BF)


def _inproj(x2d, mod3, g_pre, w_in, cos, sin, *, layer, tm, tiles_per_batch, mod_row, widths, rope):
    n, d = x2d.shape
    w_ssm, w_fn, w_na = widths
    nt = n // tm
    if mod_row is None:
        row = lambda i: i // tiles_per_batch
    else:
        row = lambda i: mod_row
    pos = lambda i: (i % tiles_per_batch, 0)
    kern = functools.partial(_inproj_kernel, widths=widths, rope=rope, sub=min(tm, 512))
    return pl.pallas_call(
        kern,
        grid=(nt,),
        in_specs=[pl.BlockSpec((tm, d), lambda i: (i, 0)),
                  pl.BlockSpec((None, 1, d), lambda i: (row(i), 0, 0)),
                  pl.BlockSpec((None, 1, d), lambda i: (row(i), 0, 1)),
                  pl.BlockSpec((None, 1, d), lambda i: (layer, 0, 0)),
                  _layer_resident(w_in, layer),
                  pl.BlockSpec((tm, LANES), pos),
                  pl.BlockSpec((tm, LANES), pos)],
        out_specs=[pl.BlockSpec((w_ssm // LANES, tm, LANES), lambda i: (0, i, 0)),
                   pl.BlockSpec((tm, w_fn), lambda i: (i, 0)),
                   pl.BlockSpec((tm, w_na), lambda i: (i, 0)),
                   pl.BlockSpec((tm, w_na), lambda i: (i, 0)),
                   pl.BlockSpec((tm, w_na), lambda i: (i, 0))],
        out_shape=[jax.ShapeDtypeStruct((w_ssm // LANES, n, LANES), F32),
                   jax.ShapeDtypeStruct((n, w_fn), BF),
                   jax.ShapeDtypeStruct((n, w_na), BF),
                   jax.ShapeDtypeStruct((n, w_na), BF),
                   jax.ShapeDtypeStruct((n, w_na), BF)],
        compiler_params=_cparams("parallel"),
        name="inproj_rope" if rope else "inproj_ctx",
    )(x2d, mod3, mod3, g_pre.reshape(-1, 1, d), w_in, cos, sin)


def _rope_tables(seq):
    t = np.arange(seq)
    row = (t // GRID_W).astype(np.float32)
    col = (t % GRID_W).astype(np.float32)
    quarter = HEAD_DIM // 4
    freqs = (np.float32(ROPE_BASE) ** (-np.arange(quarter, dtype=np.float32) / quarter)).astype(np.float32)
    d = np.arange(LANES) % HEAD_DIM
    use_col = (d // (HEAD_DIM // 2)) == 1
    fidx = d % quarter
    sign = np.where((d % (HEAD_DIM // 2)) // quarter == 0, -1.0, 1.0)
    pos = np.where(use_col[None, :], col[:, None], row[:, None])
    ang = (pos * freqs[fidx][None, :]).astype(np.float32).astype(np.float64)
    return jnp.asarray(np.cos(ang), F32), jnp.asarray(np.sin(ang) * sign[None, :], F32)


def _gelu_tanh(x):
    c = math.sqrt(2.0 / math.pi)
    return x * (0.5 * (1.0 + jnp.tanh(c * (x + 0.044715 * (x * x * x)))))


def _group_of(idx, width, groups):
    shift = width.bit_length() - 1
    assert width == 1 << shift and groups & (groups - 1) == 0
    return lax.bitwise_and(lax.shift_right_logical(idx, shift), groups - 1)


def _same_group(shape, row0, row_width, col0, col_width, groups):
    r = lax.broadcasted_iota(jnp.int32, shape, 0) + row0
    c = lax.broadcasted_iota(jnp.int32, shape, 1) + col0
    return _group_of(r, row_width, groups) == _group_of(c, col_width, groups)


def _swap_stage(tiles, d):
    lane = lax.broadcasted_iota(jnp.int32, tiles[0].shape, 1)
    upper = lax.bitwise_and(lax.shift_right_logical(lane, SSM_GROUP.bit_length() - 1), d) != 0
    out = list(tiles)
    for x in range(len(tiles)):
        if x & d:
            continue
        a, b = tiles[x], tiles[x + d]
        out[x] = jnp.where(upper, pltpu.roll(b, SSM_GROUP * d, 1), a)
        out[x + d] = jnp.where(upper, b, pltpu.roll(a, LANES - SSM_GROUP * d, 1))
    return out


def _ssm_kernel(ul_ref, uc_ref, wc_ref, lag_ref, dc_ref, rexp_ref, cexp_ref, lam_ref, lamseg_ref,
                z_ref, zc_ref, u_s, bm_s, kin_s, cm_s, s_ref, h_ref, e_ref, *, seg_len, n_lat, n_ctx, stages):
    T = SSM_T
    ns = 2 ** len(stages)
    gs = T // ns
    sw = gs * LANES
    rows = u_s.shape[1]
    tps = s_ref.shape[0] // ns
    tpq = tps // 4
    nst = tpq * LANES
    gpt = LANES // SSM_GROUP
    per_q = gpt * SSM_STATE

    def lanes(c):
        return slice(c * LANES, (c + 1) * LANES)

    def col0(s, q):
        return q * per_q + s * nst

    def swapped(tiles, order):
        for d in order:
            tiles = _swap_stage(tiles, d)
        return tiles

    @pl.when(pl.program_id(1) == 0)
    def _expand_operators():
        rexp = rexp_ref[...]
        cexp = cexp_ref[...]
        for s in range(ns):
            wsel = jnp.concatenate([wc_ref[:, col0(s, q):col0(s, q) + nst] for q in range(4)], axis=1)
            blk = jnp.dot(rexp, wsel, preferred_element_type=F32)
            keep = _same_group(blk.shape, 0, SSM_GROUP, 0, SSM_STATE, gs)
            bm_s[s] = jnp.where(keep, blk, 0.0).astype(BF)
            dsel = jnp.concatenate([dc_ref[col0(s, q):col0(s, q) + nst, :] for q in range(4)], axis=0)
            blk = jnp.dot(dsel, cexp, preferred_element_type=F32)
            keep = _same_group(blk.shape, 0, SSM_STATE, 0, SSM_GROUP, gs)
            cm_s[s] = jnp.where(keep, blk, 0.0).astype(BF)
        r16 = lax.broadcasted_iota(jnp.int32, (LANES, SSM_GROUP), 0)
        c16 = lax.broadcasted_iota(jnp.int32, (LANES, SSM_GROUP), 1)
        chan = jnp.where(lax.bitwise_and(r16, SSM_GROUP - 1) == c16, 1.0, 0.0).astype(BF)
        for t in range(T):
            lo = (T - 1 - t) * LANES
            blk = jnp.dot(chan, lag_ref[:, lo:lo + T * LANES], preferred_element_type=F32)
            keep = _same_group(blk.shape, 0, SSM_GROUP, 0, SSM_GROUP, gpt)
            blk = jnp.where(keep, blk, 0.0)
            tiles = swapped([blk[:, lanes(x)] for x in range(T)], stages)
            r0 = (t % gs) * LANES + (t // gs) * gs * SSM_GROUP
            for s in range(ns):
                part = jnp.concatenate(tiles[s * gs:(s + 1) * gs], axis=1)
                kin_s[s, r0:r0 + gs * SSM_GROUP, :] = part[s * gs * SSM_GROUP:(s + 1) * gs * SSM_GROUP, :].astype(BF)

    xs = []
    for t in range(T):
        ct = uc_ref[pl.ds(t, n_ctx, stride=T), :]
        xs.append(jnp.concatenate([ct, ul_ref[pl.ds(t, n_lat, stride=T), :], ct], axis=0))
    xs = swapped(xs, stages)
    for s in range(ns):
        for k in range(gs):
            u_s[s, :, lanes(k)] = xs[s * gs + k].astype(BF)
    for s in range(ns):
        r = jnp.dot(u_s[s], bm_s[s], preferred_element_type=F32)
        for c in range(tps):
            s_ref[s * tps + c] = r[:, lanes(c)]

    chains = []
    for s in range(ns):
        for d in range(2):
            for k in range(tpq):
                cr, ci = s * tps + 2 * d * tpq + k, s * tps + (2 * d + 1) * tpq + k
                lr = slice(col0(s, 2 * d) + k * LANES, col0(s, 2 * d) + (k + 1) * LANES)
                li = slice(col0(s, 2 * d + 1) + k * LANES, col0(s, 2 * d + 1) + (k + 1) * LANES)
                chains.append((cr, ci, d, lr, li))
    nt = ns * tps

    def scan_step(i, carry, record):
        new = list(carry)
        for cr, ci, d, lr_sl, li_sl in chains:
            rsl = pl.ds(i if d == 0 else seg_len - 1 - i, SSM_SEG, stride=seg_len)
            hr, hi = carry[cr], carry[ci]
            if record:
                h_ref[cr, rsl, :] = hr
                h_ref[ci, rsl, :] = hi
            lr, li = lam_ref[:, lr_sl], lam_ref[:, li_sl]
            new[cr] = lr * hr - li * hi + s_ref[cr, rsl, :]
            new[ci] = lr * hi + li * hr + s_ref[ci, rsl, :]
        return tuple(new)

    zero = jnp.zeros((SSM_SEG, LANES), F32)
    fin = lax.fori_loop(0, seg_len, lambda i, c: scan_step(i, c, False), (zero,) * nt, unroll=4)
    for c in range(nt):
        e_ref[c, 0:SSM_SEG, :] = fin[c]

    for cr, ci, d, lr_sl, li_sl in chains:
        lr, li = lamseg_ref[:, lr_sl], lamseg_ref[:, li_sl]
        er = jnp.zeros((1, LANES), F32)
        ei = jnp.zeros((1, LANES), F32)
        for n in range(SSM_SEG):
            g = n if d == 0 else SSM_SEG - 1 - n
            e_ref[cr, SSM_SEG + g:SSM_SEG + g + 1, :] = er
            e_ref[ci, SSM_SEG + g:SSM_SEG + g + 1, :] = ei
            fr, fi = e_ref[cr, g:g + 1, :], e_ref[ci, g:g + 1, :]
            er, ei = lr * er - li * ei + fr, lr * ei + li * er + fi

    ent = tuple(e_ref[c, SSM_SEG:2 * SSM_SEG, :] for c in range(nt))
    lax.fori_loop(0, seg_len, lambda i, c: scan_step(i, c, True), ent, unroll=2)

    ys, ycs = [], []
    for s in range(ns):
        hcat = jnp.concatenate([h_ref[s * tps + c].astype(BF) for c in range(tps)], axis=1)
        y = _gelu_tanh(jnp.dot(u_s[s], kin_s[s], preferred_element_type=F32)
                       + jnp.dot(hcat, cm_s[s], preferred_element_type=F32))
        ys += [y[:, lanes(k)] for k in range(gs)]
        hc = jnp.concatenate(
            [h_ref[s * tps + c, 0:n_ctx, :].astype(BF) for c in range(2 * tpq)]
            + [h_ref[s * tps + c, rows - n_ctx:rows, :].astype(BF) for c in range(2 * tpq, tps)], axis=1)
        yc = _gelu_tanh(jnp.dot(u_s[s, 0:n_ctx, :], kin_s[s], preferred_element_type=F32)
                        + jnp.dot(hc, cm_s[s], preferred_element_type=F32))
        ycs += [yc[:, lanes(k)] for k in range(gs)]
    ys = swapped(ys, stages[::-1])
    ycs = swapped(ycs, stages[::-1])
    for t in range(T):
        z_ref[pl.ds(t, n_lat, stride=T), :] = ys[t][n_ctx:n_ctx + n_lat, :]
        zc_ref[pl.ds(t, n_ctx, stride=T), :] = ycs[t]


def _ssm_weights(a_re, a_im, log_dt, b_re, b_im, c_re, c_im, d_skip, seg_len):
    T = SSM_T
    g, p = a_re.shape[1], a_re.shape[2]
    hc = b_re.shape[-1]
    gpt = LANES // hc
    J = g // gpt
    lam = lax.complex(a_re.astype(F32), a_im.astype(F32))
    dt = jnp.exp(log_dt.astype(F32))[..., None]
    ldt = lam * dt
    lam_bar = jnp.exp(ldt)
    bbar = ((lam_bar - 1) / lam)[..., None] * lax.complex(b_re.astype(F32), b_im.astype(F32))
    cmat = lax.complex(c_re.astype(F32), c_im.astype(F32))
    kk = jnp.arange(T + 1, dtype=F32)
    pw = jnp.exp(ldt[:, None] * kk[None, :, None, None])

    wf = pw[0, :T][::-1][:, :, None, :] * jnp.swapaxes(bbar[0], 1, 2)[None]
    wb = pw[1, :T][:, :, None, :] * jnp.swapaxes(bbar[1], 1, 2)[None]
    wq = jnp.stack([wf.real, wf.imag, wb.real, wb.imag]).reshape(4, T, J, gpt, hc, p)
    wc = wq.transpose(2, 1, 4, 0, 3, 5).reshape(J, T * hc, 4 * gpt * p)

    df = cmat[0][None] * pw[0, 1:][:, :, None, :]
    db = cmat[1][None] * pw[1, 1:][::-1][:, :, None, :]
    dq = jnp.stack([df.real, -df.imag, db.real, -db.imag]).reshape(4, T, J, gpt, hc, p)
    dc = dq.transpose(2, 0, 3, 5, 1, 4).reshape(J, 4 * gpt * p, T * hc)

    mf = jnp.einsum('gcp,kgp,gph->khgc', cmat[0], pw[0, :T], bbar[0]).real
    mb = jnp.einsum('gcp,kgp,gph->khgc', cmat[1], pw[1, :T], bbar[1]).real
    skip = jnp.eye(hc, dtype=F32)[:, None, :] * d_skip.astype(F32).reshape(g, hc)[None]
    m0 = mf[0] + mb[0] + skip
    lags = jnp.concatenate([mb[1:][::-1], m0[None], mf[1:]], axis=0)
    kc = lags.reshape(2 * T - 1, hc, J, LANES).transpose(2, 1, 0, 3).reshape(J, hc, (2 * T - 1) * LANES)

    def lam_rows(power):
        lp = jnp.exp(ldt * power)
        lt = lp.reshape(2, J, gpt * p)
        return jnp.concatenate([lt[0].real, lt[0].imag, lt[1].real, lt[1].imag], axis=-1)[:, None, :]

    lam_t = jnp.broadcast_to(lam_rows(float(T)), (J, SSM_SEG, 4 * gpt * p))
    return wc.astype(BF), kc.astype(BF), dc.astype(BF), lam_t, lam_rows(float(T * seg_len))


def _ssm_expanders(gs):
    r = np.arange(gs * LANES)
    k, slot, ch = r // LANES, (r % LANES) // SSM_GROUP, r % SSM_GROUP
    t = (slot // gs) * gs + k
    rexp = np.zeros((gs * LANES, SSM_T * SSM_GROUP), np.float32)
    rexp[r, t * SSM_GROUP + ch] = 1.0
    return jnp.asarray(rexp, BF), jnp.asarray(rexp.T, BF)


def _ssm(u3, u3c, wc, kc, dc, lam, lamseg, *, layer, batch, seq, n_ctx):
    T = SSM_T
    J = u3.shape[0]
    ns = wc.shape[-1]
    nsets = 2 ** len(SSM_STAGES)
    gs = T // nsets
    sw = gs * LANES
    n_lat, n_c = seq // T, n_ctx // T
    rows = n_lat + 2 * n_c
    seg_len = rows // SSM_SEG
    assert rows % SSM_SEG == 0 and n_c % 16 == 0
    rexp, cexp = _ssm_expanders(gs)
    kern = functools.partial(_ssm_kernel, seg_len=seg_len, n_lat=n_lat, n_ctx=n_c, stages=SSM_STAGES)
    const = lambda j, b: (0, 0)
    per_j = lambda j, b: (layer, j, 0, 0)
    per_jb = lambda j, b: (j, b, 0)
    return pl.pallas_call(
        kern,
        grid=(J, batch),
        in_specs=[pl.BlockSpec((None, seq, LANES), per_jb),
                  pl.BlockSpec((None, n_ctx, LANES), per_jb),
                  pl.BlockSpec((None, None) + wc.shape[2:], per_j),
                  pl.BlockSpec((None, None) + kc.shape[2:], per_j),
                  pl.BlockSpec((None, None) + dc.shape[2:], per_j),
                  pl.BlockSpec(rexp.shape, const),
                  pl.BlockSpec(cexp.shape, const),
                  pl.BlockSpec((None, None, SSM_SEG, ns), per_j),
                  pl.BlockSpec((None, None, 1, ns), per_j)],
        out_specs=[pl.BlockSpec((None, seq, LANES), per_jb),
                   pl.BlockSpec((None, n_ctx, LANES), per_jb)],
        out_shape=[jax.ShapeDtypeStruct(u3.shape, F32),
                   jax.ShapeDtypeStruct(u3c.shape, F32)],
        scratch_shapes=[pltpu.VMEM((nsets, rows, sw), BF),
                        pltpu.VMEM((nsets, sw, ns // nsets), BF),
                        pltpu.VMEM((nsets, sw, sw), BF),
                        pltpu.VMEM((nsets, ns // nsets, sw), BF),
                        pltpu.VMEM((ns // LANES, rows, LANES), F32),
                        pltpu.VMEM((ns // LANES, rows, LANES), F32),
                        pltpu.VMEM((ns // LANES, 2 * SSM_SEG, LANES), F32)],
        compiler_params=_cparams("arbitrary", "arbitrary"),
        name="ssm_chunk_scan",
    )(u3, u3c, wc, kc, dc, rexp, cexp, lam, lamseg)


def _dft_a_kernel(w_ref, x_ref, o_ref):
    o_ref[...] = jnp.dot(w_ref[...], x_ref[...], preferred_element_type=F32).astype(BF)


def _dft_rows(n):
    k = np.arange(n)
    m = (k[:, None] * k[None, :]) % n
    ang = m * (2.0 * math.pi / n)
    return jnp.asarray(np.concatenate([np.cos(ang), -np.sin(ang)], axis=0), BF)


def _dft_a(w, x3):
    B, n, cols = x3.shape
    tn = min(cols, 4096)
    return pl.pallas_call(
        _dft_a_kernel,
        grid=(B, cols // tn),
        in_specs=[pl.BlockSpec((2 * n, n), lambda b, i: (0, 0)),
                  pl.BlockSpec((None, n, tn), lambda b, i: (b, 0, i))],
        out_specs=pl.BlockSpec((None, 2 * n, tn), lambda b, i: (b, 0, i)),
        out_shape=jax.ShapeDtypeStruct((B, 2 * n, cols), BF),
        compiler_params=_cparams("parallel", "parallel"),
        name="dft_stage_a",
    )(w, x3)


def _store_lane_tiles(o_ref, p, n, wf, idx):
    per = wf // LANES
    for c in range(2 * per):
        part = p[(c // per) * n:(c // per + 1) * n, (c % per) * LANES:(c % per + 1) * LANES]
        o_ref[(c,) + idx] = part


def _dft_b_kernel(a_ref, g_ref, o_ref, *, kb, l1, wf):
    for i in range(kb):
        g = jnp.concatenate([g_ref[0, i], g_ref[1, i]], axis=0)
        p = jnp.dot(a_ref[i], g, preferred_element_type=F32)
        _store_lane_tiles(o_ref, p, l1, wf, (i,))


def _dft_b_mats(seq, l1, l2):
    k2 = np.arange(l2)[:, None, None]
    k1 = np.arange(l1)[None, :, None]
    j1 = np.arange(l1)[None, None, :]
    m = ((k1 * l2 + k2) * j1) % seq
    ang = m * (2.0 * math.pi / seq)
    ar, ai = np.cos(ang).astype(np.float32), -np.sin(ang).astype(np.float32)
    top = np.concatenate([ar, -ai], axis=2)
    bot = np.concatenate([ai, ar], axis=2)
    return jnp.asarray(np.concatenate([top, bot], axis=1), BF)


def _dft_b(amat, g5, *, kb):
    B, _, l2, l1, wf = g5.shape
    nlt = 2 * wf // LANES
    kern = functools.partial(_dft_b_kernel, kb=kb, l1=l1, wf=wf)
    return pl.pallas_call(
        kern,
        grid=(B, l2 // kb),
        in_specs=[pl.BlockSpec((kb, 2 * l1, 2 * l1), lambda b, i: (i, 0, 0)),
                  pl.BlockSpec((None, 2, kb, l1, wf), lambda b, i: (b, 0, i, 0, 0))],
        out_specs=pl.BlockSpec((None, nlt, kb, l1, LANES), lambda b, i: (b, 0, i, 0, 0)),
        out_shape=jax.ShapeDtypeStruct((B, nlt, l2, l1, LANES), F32),
        compiler_params=_cparams("parallel", "parallel"),
        name="dft_stage_b",
    )(amat, g5)


def _dft_ctx_kernel(w_ref, x_ref, o_ref, *, n, wf):
    p = jnp.dot(w_ref[...], x_ref[...], preferred_element_type=F32)
    _store_lane_tiles(o_ref, p, n, wf, ())


def _dft_ctx(w, x3):
    B, n, wf = x3.shape
    nlt = 2 * wf // LANES
    kern = functools.partial(_dft_ctx_kernel, n=n, wf=wf)
    return pl.pallas_call(
        kern,
        grid=(B,),
        in_specs=[pl.BlockSpec((2 * n, n), lambda b: (0, 0)),
                  pl.BlockSpec((None, n, wf), lambda b: (b, 0, 0))],
        out_specs=pl.BlockSpec((None, nlt, n, LANES), lambda b: (b, 0, 0, 0)),
        out_shape=jax.ShapeDtypeStruct((B, nlt, n, LANES), F32),
        compiler_params=_cparams("parallel"),
        name="dft_ctx",
    )(w, x3)


def _channel_dft(wf, seq):
    c = np.arange(wf)
    same = (c[:, None] // FNET_GROUP) == (c[None, :] // FNET_GROUP)
    m = ((c[:, None] % FNET_GROUP) * (c[None, :] % FNET_GROUP)) % FNET_GROUP
    ang = m * (2.0 * math.pi / FNET_GROUP)
    mask = same / math.sqrt(seq * FNET_GROUP)
    return jnp.asarray(np.concatenate([np.cos(ang) * mask, np.sin(ang) * mask], axis=0), BF)


def _softmax_pv(parts, l_shape):
    m = parts[0][0].max(axis=-1, keepdims=True)
    for s, _ in parts[1:]:
        m = jnp.maximum(m, s.max(axis=-1, keepdims=True))
    l = jnp.zeros(l_shape, F32)
    o = None
    for s, v in parts:
        p = jnp.exp2(s - m)
        l = l + p.sum(axis=-1, keepdims=True)
        pv = jnp.dot(p.astype(BF), v, preferred_element_type=F32)
        o = pv if o is None else o + pv
    return o / l


def _na_kernel(kb_ref, pid_ref, dr_ref, q_ref, k_ref, v_ref, kc_ref, vc_ref, cbx_ref, o_ref, bias_ref,
               *, nkeys, heads, nsub):
    step = pl.program_id(1)
    nq = q_ref.shape[0] // nsub
    rq, kr = nq // GRID_W, nkeys // GRID_W
    lane = lax.broadcasted_iota(jnp.int32, (nq, LANES), 1)
    low = lane < HEAD_DIM
    dn = (((1,), (1,)), ((), ()))

    @pl.when((pl.program_id(0) == 0) & (step == 0))
    def _build_bias():
        for pat in range(bias_ref.shape[0]):
            for qi in range(rq):
                for w in range(kr):
                    a = dr_ref[(pat * rq + qi) * kr + w]
                    for h in range(heads):
                        bias_ref[pat, h, qi * GRID_W:(qi + 1) * GRID_W, w * GRID_W:(w + 1) * GRID_W] = cbx_ref[h, a]

    tasks = [(sb, h) for sb in range(nsub) for h in range(heads)]
    starts = [pl.multiple_of(kb_ref[step * nsub + sb] * GRID_W, GRID_W) for sb in range(nsub)]
    pats = [pid_ref[step * nsub + sb] for sb in range(nsub)]

    def scores(sb, h):
        sl = slice((h // 2) * LANES, (h // 2 + 1) * LANES)
        q2 = q_ref[sb * nq:(sb + 1) * nq, sl]
        qm = jnp.where(low if h % 2 == 0 else jnp.logical_not(low), q2, jnp.zeros_like(q2))
        s_w = lax.dot_general(qm, k_ref[pl.ds(starts[sb], nkeys), sl], dn, preferred_element_type=F32)
        s_c = lax.dot_general(qm, kc_ref[:, sl], dn, preferred_element_type=F32)
        return s_w + bias_ref[pats[sb], h], s_c

    def weights(s_w, s_c):
        m = jnp.maximum(s_w.max(axis=-1, keepdims=True), s_c.max(axis=-1, keepdims=True))
        p_w = jnp.exp2(s_w - m)
        p_c = jnp.exp2(s_c - m)
        l = p_w.sum(axis=-1, keepdims=True) + p_c.sum(axis=-1, keepdims=True)
        return p_w.astype(BF), p_c.astype(BF), l

    outs = {}

    def finish(sb, h, p_w, p_c, l):
        sl = slice((h // 2) * LANES, (h // 2 + 1) * LANES)
        o = (jnp.dot(p_w, v_ref[pl.ds(starts[sb], nkeys), sl], preferred_element_type=F32)
             + jnp.dot(p_c, vc_ref[:, sl], preferred_element_type=F32))
        outs[h % 2] = o / l
        if h % 2 == 1:
            o_ref[sb * nq:(sb + 1) * nq, sl] = jnp.where(low, outs[0], outs[1]).astype(BF)

    nt = len(tasks)
    sc = {0: scores(*tasks[0])}
    if nt > 1:
        sc[1] = scores(*tasks[1])
    pw = {0: weights(*sc.pop(0))}
    for n in range(nt):
        if n + 2 < nt:
            sc[n + 2] = scores(*tasks[n + 2])
        if n + 1 < nt:
            pw[n + 1] = weights(*sc.pop(n + 1))
        finish(*tasks[n], *pw.pop(n))


def _na_plan(rows, rq):
    wr = min(NA_ROWS, rows)
    kr = rq + wr - 1
    nblk = rows // rq
    kbs, pids, pats = [], [], []
    for blk in range(nblk):
        r0 = blk * rq
        rs = [int(np.clip(r0 + i - wr // 2, 0, rows - wr)) for i in range(rq)]
        kb = min(rs[0], rows - kr)
        pat = (tuple(r - kb for r in rs), r0 - kb)
        if pat not in pats:
            pats.append(pat)
        kbs.append(kb)
        pids.append(pats.index(pat))
    return np.asarray(kbs, np.int32), np.asarray(pids, np.int32), pats, kr, wr


def _na_bias(rpb, pats, rq, kr, wr):
    cols = np.arange(GRID_W)
    cstart = np.clip(cols - NA_COLS // 2, 0, GRID_W - NA_COLS)
    kc = np.arange(GRID_W)
    col_ok = (kc[None, :] >= cstart[:, None]) & (kc[None, :] < cstart[:, None] + NA_COLS)
    dc = kc[None, :] - cols[:, None] + NA_COLS - 1
    depth, heads, nr, ncol = rpb.shape
    sel = (dc[:, :, None] == np.arange(ncol)[None, None, :]) & col_ok[:, :, None]
    cb = jnp.einsum('lhab,ckb->lhack', rpb.astype(F32), jnp.asarray(sel, F32), precision=lax.Precision.HIGHEST)
    cb = jnp.where(jnp.asarray(col_ok)[None, None, None], cb * math.log2(math.e), NEG)
    cbx = jnp.concatenate([cb, jnp.full((depth, heads, 1, GRID_W, GRID_W), NEG, F32)], axis=2)
    slots = []
    for rs_off, r_off in pats:
        i = np.arange(rq)[:, None]
        w = np.arange(kr)[None, :]
        rso = np.asarray(rs_off)[:, None]
        row_ok = (w >= rso) & (w < rso + wr)
        slots.append(np.where(row_ok, w - r_off - i + NA_ROWS - 1, nr))
    return cbx, np.stack(slots).reshape(-1).astype(np.int32)


def _na_tables(rpb_all, seq, rq):
    kbs, pids, pats, kr, wr = _na_plan(seq // GRID_W, rq)
    cbx, slots = _na_bias(rpb_all, pats, rq, kr, wr)
    return dict(kbs=jnp.asarray(kbs), pids=jnp.asarray(pids), slots=jnp.asarray(slots), cbx=cbx, kr=kr, rq=rq,
                npat=len(pats))


def _na_latent(q, k, v, kc, vc, tabs, layer, *, batch, seq, n_ctx):
    rows = seq // GRID_W
    cbx, rq, kr = tabs["cbx"], tabs["rq"], tabs["kr"]
    heads = cbx.shape[1]
    nblk = rows // rq
    nq = rq * GRID_W
    nkeys = kr * GRID_W
    w = q.shape[1]
    nsub = 2 if nblk % 2 == 0 else 1
    nstep = nblk // nsub
    kern = functools.partial(_na_kernel, nkeys=nkeys, heads=heads, nsub=nsub)
    grid_spec = pltpu.PrefetchScalarGridSpec(
        num_scalar_prefetch=3,
        grid=(batch, nstep),
        in_specs=[pl.BlockSpec((nsub * nq, w), lambda b, i, *_: (b * nstep + i, 0)),
                  pl.BlockSpec((seq, w), lambda b, i, *_: (b, 0)),
                  pl.BlockSpec((seq, w), lambda b, i, *_: (b, 0)),
                  pl.BlockSpec((n_ctx, w), lambda b, i, *_: (b, 0)),
                  pl.BlockSpec((n_ctx, w), lambda b, i, *_: (b, 0)),
                  pl.BlockSpec((None,) + cbx.shape[1:], lambda b, i, *_: (layer, 0, 0, 0, 0))],
        out_specs=pl.BlockSpec((nsub * nq, w), lambda b, i, *_: (b * nstep + i, 0)),
        scratch_shapes=[pltpu.VMEM((tabs["npat"], heads, nq, nkeys), F32)],
    )
    return pl.pallas_call(
        kern,
        grid_spec=grid_spec,
        out_shape=jax.ShapeDtypeStruct(q.shape, BF),
        compiler_params=_cparams("arbitrary", "arbitrary"),
        name="na_latent",
    )(tabs["kbs"], tabs["pids"], tabs["slots"], q, k, v, kc, vc, cbx)


def _na_ctx_kernel(q_ref, k_ref, v_ref, o_ref, *, heads):
    nq = q_ref.shape[0]
    lane = lax.broadcasted_iota(jnp.int32, (nq, LANES), 1)
    low = lane < HEAD_DIM
    dn = (((1,), (1,)), ((), ()))
    for hp in range(heads // 2):
        sl = slice(hp * LANES, (hp + 1) * LANES)
        q2 = q_ref[:, sl]
        k2 = k_ref[:, sl]
        v2 = v_ref[:, sl]
        outs = []
        for hh in range(2):
            qm = jnp.where(low if hh == 0 else jnp.logical_not(low), q2, jnp.zeros_like(q2))
            s = lax.dot_general(qm, k2, dn, preferred_element_type=F32)
            outs.append(_softmax_pv([(s, v2)], (nq, 1)))
        o_ref[:, sl] = jnp.where(low, outs[0], outs[1]).astype(BF)


def _na_ctx(qc, kc, vc, *, batch, n_ctx, heads):
    w = qc.shape[1]
    spec = pl.BlockSpec((n_ctx, w), lambda b: (b, 0))
    return pl.pallas_call(
        functools.partial(_na_ctx_kernel, heads=heads),
        grid=(batch,),
        in_specs=[spec, spec, spec],
        out_specs=spec,
        out_shape=jax.ShapeDtypeStruct(qc.shape, BF),
        compiler_params=_cparams("parallel"),
        name="na_ctx",
    )(qc, kc, vc)


def _outproj_kernel(z_ref, pf_ref, na_ref, x_ref, gate_ref, gpost_ref,
                    wglu_ref, cs_ref, wfo_ref, wout_ref, o_ref, *, widths, sub):
    w_ssm, w_fn, w_na = widths
    wglu, wfo, wout = (r[...].astype(BF) for r in (wglu_ref, wfo_ref, wout_ref))
    for r0 in range(0, x_ref.shape[0], sub):
        rs = slice(r0, r0 + sub)
        z = jnp.concatenate([z_ref[j, rs, :] for j in range(w_ssm // LANES)], axis=1)
        gl = jnp.dot(z.astype(BF), wglu, preferred_element_type=F32)
        y_ssm = (z * jax.nn.sigmoid(gl)).astype(BF)
        if len(pf_ref.shape) == 3:
            tiles = [pf_ref[c, rs, :] for c in range(pf_ref.shape[0])]
        else:
            l2 = pf_ref.shape[1]
            tiles = [jnp.concatenate([pf_ref[c, :, k1, :] for k1 in range(r0 // l2, (r0 + sub) // l2)], axis=0)
                     for c in range(pf_ref.shape[0])]
        pf = jnp.concatenate(tiles, axis=1).astype(BF)
        mixed = jnp.dot(pf, cs_ref[...], preferred_element_type=F32).astype(BF)
        y_fft = jnp.dot(mixed, wfo, preferred_element_type=F32).astype(BF)
        y = jnp.concatenate([y_ssm, y_fft, na_ref[rs, :]], axis=1)
        o = jnp.dot(y, wout, preferred_element_type=F32)
        ms = jnp.mean(o * o, axis=-1, keepdims=True)
        o_ref[rs, :] = x_ref[rs, :] + gate_ref[...] * (o * lax.rsqrt(ms + EPS) * gpost_ref[...])


def _outproj(z3, pf, yna, x2d, mod3, g_post, w_glu, cs, w_fo, w_out, *,
             layer, tm, tiles_per_batch, mod_row, widths):
    n, d = x2d.shape
    w_ssm, w_fn, w_na = widths
    if mod_row is None:
        row = lambda i: i // tiles_per_batch
    else:
        row = lambda i: mod_row
    kern = functools.partial(_outproj_kernel, widths=widths, sub=min(tm, 512))
    return pl.pallas_call(
        kern,
        grid=(n // tm,),
        in_specs=[pl.BlockSpec((w_ssm // LANES, tm, LANES), lambda i: (0, i, 0)),
                  (pl.BlockSpec((None, pf.shape[1], tm, LANES),
                                lambda i: (i // tiles_per_batch, 0, i % tiles_per_batch, 0))
                   if pf.ndim == 4 else
                   pl.BlockSpec((None, pf.shape[1], pf.shape[2], tm // pf.shape[2], LANES),
                                lambda i: (i // tiles_per_batch, 0, 0, i % tiles_per_batch, 0))),
                  pl.BlockSpec((tm, w_na), lambda i: (i, 0)),
                  pl.BlockSpec((tm, d), lambda i: (i, 0)),
                  pl.BlockSpec((None, 1, d), lambda i: (row(i), 0, 2)),
                  pl.BlockSpec((None, 1, d), lambda i: (layer, 0, 0)),
                  _layer_resident(w_glu, layer),
                  _resident(cs.shape, lambda i: (0, 0)),
                  _layer_resident(w_fo, layer),
                  _layer_resident(w_out, layer)],
        out_specs=pl.BlockSpec((tm, d), lambda i: (i, 0)),
        out_shape=jax.ShapeDtypeStruct((n, d), F32),
        compiler_params=_cparams("parallel"),
        name="outproj",
    )(z3, pf, yna, x2d, mod3, g_post.reshape(-1, 1, d), w_glu, cs, w_fo, w_out)


def _ffn_kernel(x_ref, sh_ref, sc_ref, gate_ref, gpre_ref, gpost_ref, wg_ref, wu_ref, wd_ref,
                o_ref, a_ref, *, sub, chunk):
    dff = wg_ref.shape[1]
    for r0 in range(0, x_ref.shape[0], sub):
        rs = slice(r0, r0 + sub)
        x = x_ref[rs, :]
        ms = jnp.mean(x * x, axis=-1, keepdims=True)
        m = ((x * lax.rsqrt(ms + EPS) * gpre_ref[...]) * (1.0 + sc_ref[...]) + sh_ref[...]).astype(BF)
        for c0 in range(0, dff, chunk):
            sl = slice(c0, min(c0 + chunk, dff))
            g = jnp.dot(m, wg_ref[:, sl], preferred_element_type=F32)
            u = jnp.dot(m, wu_ref[:, sl], preferred_element_type=F32)
            a_ref[rs, sl] = (g * jax.nn.sigmoid(g) * u).astype(BF)
        y = jnp.dot(a_ref[rs, :], wd_ref[...], preferred_element_type=F32)
        ms2 = jnp.mean(y * y, axis=-1, keepdims=True)
        o_ref[rs, :] = x + gate_ref[...] * (y * lax.rsqrt(ms2 + EPS) * gpost_ref[...])


FFN_WEIGHT_STEPS = 8


def _ffn_staged_kernel(*refs, sub, chunk, n_lat_tiles, with_ctx):
    if with_ctx:
        (x_ref, xc_ref, sh_ref, sc_ref, gate_ref, gpre_ref, gpost_ref, wg_ref, wu_ref, wd_ref,
         o_ref, oc_ref, a_ref, wg_s, wu_s, wd_s) = refs
    else:
        (x_ref, sh_ref, sc_ref, gate_ref, gpre_ref, gpost_ref, wg_ref, wu_ref, wd_ref,
         o_ref, a_ref, wg_s, wu_s, wd_s) = refs
    i = pl.program_id(0)
    common = (sh_ref, sc_ref, gate_ref, gpre_ref, gpost_ref, wg_s, wu_s, wd_s)

    @pl.when(i < FFN_WEIGHT_STEPS)
    def _cast_weights():
        for src, dst in ((wg_ref, wg_s), (wu_ref, wu_s), (wd_ref, wd_s)):
            rows = src.shape[0]
            dst[pl.ds(pl.multiple_of(i * rows, rows), rows), :] = src[...].astype(BF)

    @pl.when((i >= FFN_WEIGHT_STEPS) & (i < FFN_WEIGHT_STEPS + n_lat_tiles))
    def _latent():
        _ffn_kernel(x_ref, *common, o_ref, a_ref, sub=sub, chunk=chunk)

    if with_ctx:
        @pl.when(i >= FFN_WEIGHT_STEPS + n_lat_tiles)
        def _context():
            _ffn_kernel(xc_ref, *common, oc_ref, a_ref, sub=min(sub, xc_ref.shape[0]), chunk=chunk)


def _ffn(x2d, xc2d, mod3, g_pre, g_post, wg, wu, wd, *, layer, tm, tiles_per_batch, ctx_row):
    n, d = x2d.shape
    dff = wg.shape[2]
    nl = n // tm
    nw = FFN_WEIGHT_STEPS
    with_ctx = xc2d is not None
    gains = (g_pre.reshape(-1, 1, d), g_post.reshape(-1, 1, d))
    slab = lambda i: (layer, jnp.minimum(i, nw - 1), 0)
    weights = [pl.BlockSpec((None, d // nw, dff), slab), pl.BlockSpec((None, d // nw, dff), slab),
               pl.BlockSpec((None, dff // nw, d), slab)]
    gain_specs = [pl.BlockSpec((None, 1, d), lambda i: (layer, 0, 0))] * 2
    tile = lambda i: jnp.clip(i - nw, 0, nl - 1)
    row = lambda i: jnp.where(i < nw + nl, tile(i) // tiles_per_batch, ctx_row)
    lat = lambda i: (tile(i), 0)
    mods = [pl.BlockSpec((None, 1, d), functools.partial(lambda i, c: (row(i), 0, c), c=c)) for c in (3, 4, 5)]
    in_specs = [pl.BlockSpec((tm, d), lat)]
    out_specs = [pl.BlockSpec((tm, d), lat)]
    out_shape = [jax.ShapeDtypeStruct((n, d), F32)]
    args = [x2d]
    if with_ctx:
        nc = xc2d.shape[0]
        ctx = lambda i: (jnp.maximum(i - nw - nl, 0), 0)
        in_specs.append(pl.BlockSpec((nc, d), ctx, pipeline_mode=pl.Buffered(1)))
        out_specs.append(pl.BlockSpec((nc, d), ctx))
        out_shape.append(jax.ShapeDtypeStruct((nc, d), F32))
        args.append(xc2d)
    outs = pl.pallas_call(
        functools.partial(_ffn_staged_kernel, sub=min(tm, 512), chunk=3 * MXU_TILE, n_lat_tiles=nl,
                          with_ctx=with_ctx),
        grid=(nw + nl + int(with_ctx),),
        in_specs=in_specs + mods + gain_specs + weights,
        out_specs=out_specs,
        out_shape=out_shape,
        scratch_shapes=[pltpu.VMEM((tm, dff), BF), pltpu.VMEM((d, dff), BF), pltpu.VMEM((d, dff), BF),
                        pltpu.VMEM((dff, d), BF)],
        compiler_params=_cparams("arbitrary"),
        name="ffn_staged",
    )(*args, mod3, mod3, mod3, *gains, wg, wu, wd)
    return (outs[0], outs[1]) if with_ctx else (outs[0], None)


def _layer(x2d, xc2d, p, w, layer, *, batch, seq, n_ctx, last, rope_tabs, consts):
    d = x2d.shape[1]
    w_ssm = consts["w_ssm"]
    w_fn = w["w_fourier"].shape[1]
    w_na = (w["w_in"].shape[2] - w_ssm - w_fn) // 3
    widths = (w_ssm, w_fn, w_na)
    heads = w_na // HEAD_DIM
    tm = min(512, seq)
    tmc = min(512, batch * n_ctx)

    mod = w["mod"][layer]
    mod3 = mod.reshape(mod.shape[0], 1, 6 * d)

    cos, sin = rope_tabs
    tm2 = min(2 * tm, seq)
    u3, f, q, k, v = _inproj(x2d, mod3, p["g_pre_mix"], w["w_in"], cos, sin, layer=layer, tm=tm2,
                             tiles_per_batch=seq // tm2, mod_row=None, widths=widths, rope=True)
    ones = jnp.ones((tmc, LANES), F32)
    u3c, fc, qc, kc, vc = _inproj(xc2d, mod3, p["g_pre_mix"], w["w_in"], ones, ones, layer=layer, tm=tmc,
                                  tiles_per_batch=1, mod_row=batch, widths=widths, rope=False)

    z3, zc3 = _ssm(u3, u3c, *w["ssm"], layer=layer, batch=batch, seq=seq, n_ctx=n_ctx)

    l2 = consts["l2"]
    l1 = seq // l2
    g = _dft_a(consts["dft_a"], f.reshape(batch, l2, l1 * w_fn))
    pf = _dft_b(consts["dft_b"], g.reshape(batch, 2, l2, l1, w_fn), kb=min(8, l2))

    yna = _na_latent(q, k, v, kc, vc, consts["na"], layer, batch=batch, seq=seq, n_ctx=n_ctx)

    w_glu, w_fo, w_out = w["w_glu"], w["w_fourier"], w["w_out"]
    x2d = _outproj(z3, pf, yna, x2d, mod3, p["g_post_mix"], w_glu, consts["cs_lat"], w_fo, w_out,
                   layer=layer, tm=tm2, tiles_per_batch=seq // tm2, mod_row=None, widths=widths)
    if not last:
        pfc = _dft_ctx(consts["dft_ctx"], fc.reshape(batch, n_ctx, w_fn))
        ynac = _na_ctx(qc, kc, vc, batch=batch, n_ctx=n_ctx, heads=heads)
        tc = min(tmc, n_ctx)
        xc2d = _outproj(zc3, pfc, ynac, xc2d, mod3, p["g_post_mix"], w_glu, consts["cs_ctx"], w_fo, w_out,
                        layer=layer, tm=tc, tiles_per_batch=n_ctx // tc, mod_row=batch, widths=widths)
    else:
        xc2d = None
    wg, wu, wd = w["w_ffn_gate"], w["w_ffn_up"], w["w_ffn_down"]
    return _ffn(x2d, xc2d, mod3, p["g_pre_ffn"], p["g_post_ffn"], wg, wu, wd,
                layer=layer, tm=tm2, tiles_per_batch=seq // tm2, ctx_row=batch)


def kernel(x, c, ctx, c_ctx, w_mod, b_mod, g_pre_mix, g_post_mix, w_in, ssm_a_re, ssm_a_im, ssm_log_dt,
           ssm_b_re, ssm_b_im, ssm_c_re, ssm_c_im, ssm_d, w_glu, w_fourier, na_rpb, w_out, g_pre_ffn,
           g_post_ffn, w_ffn_gate, w_ffn_up, w_ffn_down):
    batch, seq, d = x.shape
    n_ctx = ctx.shape[1]
    depth = w_mod.shape[0]
    w_fn = w_fourier.shape[1]
    params = dict(g_pre_mix=g_pre_mix, g_post_mix=g_post_mix, g_pre_ffn=g_pre_ffn, g_post_ffn=g_post_ffn)
    seg_len = (seq + 2 * n_ctx) // SSM_T // SSM_SEG
    ssm_ops = jax.vmap(functools.partial(_ssm_weights, seg_len=seg_len))(
        ssm_a_re, ssm_a_im, ssm_log_dt, ssm_b_re, ssm_b_im, ssm_c_re, ssm_c_im, ssm_d)
    nrow = -(-(batch + 1) // 8) * 8
    cstack = jnp.concatenate([c, c_ctx[None, :], jnp.zeros((nrow - batch - 1, d), c.dtype)], axis=0)
    weights = dict(mod=_mod_rows(cstack, w_mod, b_mod), ssm=ssm_ops,
                   w_in=w_in, w_glu=w_glu, w_fourier=w_fourier, w_out=w_out,
                   w_ffn_gate=w_ffn_gate,
                   w_ffn_up=w_ffn_up, w_ffn_down=w_ffn_down)

    l2 = 64 if seq % (64 * 8) == 0 else 8
    l1 = seq // l2
    consts = dict(
        l2=l2,
        w_ssm=ssm_d.shape[1],
        na=_na_tables(na_rpb, seq, 4),
        dft_a=_dft_rows(l2).astype(BF),
        dft_b=_dft_b_mats(seq, l1, l2).astype(BF),
        dft_ctx=_dft_rows(n_ctx).astype(BF),
        cs_lat=_channel_dft(w_fn, seq).astype(BF),
        cs_ctx=_channel_dft(w_fn, n_ctx).astype(BF),
    )
    rope_tabs = _rope_tables(seq)

    x2d = x.reshape(batch * seq, d)
    xc2d = ctx.reshape(batch * n_ctx, d)
    for layer in range(depth):
        x2d, xc2d = _layer(x2d, xc2d, params, weights, layer, batch=batch, seq=seq, n_ctx=n_ctx,
                           last=(layer == depth - 1), rope_tabs=rope_tabs, consts=consts)
    return x2d.reshape(batch, seq, d)
```

```python
import functools
import math

import numpy as np
import jax
import jax.numpy as jnp
from jax import lax
from jax.experimental import pallas as pl
from jax.experimental.pallas import tpu as pltpu

BF = jnp.bfloat16
F32 = jnp.float32

EPS = 1e-6
GRID_W = 64
HEAD_DIM = 64
NA_ROWS = 8
NA_COLS = 16
ROPE_BASE = 10000.0
SSM_GROUP = 16
SSM_STATE = 64
FNET_GROUP = 64
LANES = 128
MXU_TILE = 256
SSM_T = 8
SSM_SEG = 8
SSM_STAGES = (4, 2)
NEG = -1e30
VMEM_LIMIT = 56 * 1024 * 1024


def _cparams(*sem):
    return pltpu.CompilerParams(dimension_semantics=sem, vmem_limit_bytes=VMEM_LIMIT)


def _resident(shape, index_map):
    return pl.BlockSpec(shape, index_map, pipeline_mode=pl.Buffered(1))


def _layer_resident(w_all, layer):
    return pl.BlockSpec((None,) + w_all.shape[1:], lambda *_: (layer, 0, 0), pipeline_mode=pl.Buffered(1))


def _mod_kernel(c_ref, w_ref, b_ref, o_ref):
    cs = c_ref[...]
    s = (cs * jax.nn.sigmoid(cs)).astype(BF)
    o_ref[...] = jnp.dot(s, w_ref[...].astype(BF), preferred_element_type=F32) + b_ref[...]


def _mod_rows(cstack, w_mod, b_mod):
    rows, d = cstack.shape
    depth, _, n = w_mod.shape
    bn = 2 * d if n % (2 * d) == 0 else d
    return pl.pallas_call(
        _mod_kernel,
        grid=(depth, n // bn),
        in_specs=[pl.BlockSpec((rows, d), lambda l, i: (0, 0)),
                  pl.BlockSpec((None, d, bn), lambda l, i: (l, 0, i)),
                  pl.BlockSpec((None, 1, bn), lambda l, i: (l, 0, i))],
        out_specs=pl.BlockSpec((None, rows, bn), lambda l, i: (l, 0, i)),
        out_shape=jax.ShapeDtypeStruct((depth, rows, n), F32),
        compiler_params=_cparams("parallel", "parallel"),
        name="mod_rows",
    )(cstack, w_mod, b_mod.reshape(depth, 1, n))


def _rope_tile(t, cos, sin, first):
    partner = jnp.where(first, pltpu.roll(t, LANES - 16, 1), pltpu.roll(t, 16, 1))
    return t * cos + partner * sin


def _inproj_kernel(x_ref, sh_ref, sc_ref, g_ref, w_ref, cos_ref, sin_ref,
                   u_ref, f_ref, q_ref, k_ref, v_ref, *scratch, widths, rope, sub, f_row_tokens):
    w_ssm, w_fn, w_na = widths
    scale = HEAD_DIM ** -0.5 * math.log2(math.e)
    w = w_ref[...].astype(BF)
    for r0 in range(0, x_ref.shape[0], sub):
        rs = slice(r0, r0 + sub)
        x = x_ref[rs, :]
        ms = jnp.mean(x * x, axis=-1, keepdims=True)
        xn = x * lax.rsqrt(ms + EPS) * g_ref[...]
        m = (xn * (1.0 + sc_ref[...]) + sh_ref[...]).astype(BF)

        h = jnp.dot(m, w, preferred_element_type=F32)

        def proj(lo, n, h=h):
            return h[:, lo:lo + n]

        u = proj(0, w_ssm)
        for j in range(w_ssm // LANES):
            u_ref[j, rs, :] = u[:, j * LANES:(j + 1) * LANES]
        f = proj(w_ssm, w_fn)
        if f_row_tokens is None:
            f_ref[rs, :] = f.astype(BF)
        else:
            (fs_ref,) = scratch
            l1 = f_row_tokens
            for c in range(w_fn // LANES):
                fs_ref[c] = f[:, c * LANES:(c + 1) * LANES]
            out_rows = slice(r0 // l1, (r0 + sub) // l1)
            for j in range(l1):
                for c in range(w_fn // LANES):
                    lo = j * w_fn + c * LANES
                    f_ref[out_rows, lo:lo + LANES] = fs_ref[c, pl.ds(j, sub // l1, stride=l1), :]
        q = proj(w_ssm + w_fn, w_na)
        k = proj(w_ssm + w_fn + w_na, w_na)
        v_ref[rs, :] = proj(w_ssm + w_fn + 2 * w_na, w_na).astype(BF)
        if rope:
            cos = cos_ref[rs, :]
            sin = sin_ref[rs, :]
            lane = lax.broadcasted_iota(jnp.int32, cos.shape, 1)
            first = (lane % 32) < 16
            for j in range(w_na // LANES):
                sl = slice(j * LANES, (j + 1) * LANES)
                q_ref[rs, sl] = (_rope_tile(q[:, sl], cos, sin, first) * scale).astype(BF)
                k_ref[rs, sl] = _rope_tile(k[:, sl], cos, sin, first).astype(BF)
        else:
            q_ref[rs, :] = (q * scale).astype(BF)
            k_ref[rs, :] = k.astype(BF)


def _inproj(x2d, mod3, g_pre, w_in, cos, sin, *, layer, tm, tiles_per_batch, mod_row, widths, rope,
            f_row_tokens=None):
    n, d = x2d.shape
    w_ssm, w_fn, w_na = widths
    nt = n // tm
    if mod_row is None:
        row = lambda i: i // tiles_per_batch
    else:
        row = lambda i: mod_row
    pos = lambda i: (i % tiles_per_batch, 0)
    sub = min(tm, 512)
    kern = functools.partial(_inproj_kernel, widths=widths, rope=rope, sub=sub, f_row_tokens=f_row_tokens)
    if f_row_tokens is None:
        f_spec = pl.BlockSpec((tm, w_fn), lambda i: (i, 0))
        f_shape = jax.ShapeDtypeStruct((n, w_fn), BF)
        scratch = []
    else:
        l1 = f_row_tokens
        assert sub % l1 == 0 and (sub // l1) % 4 == 0
        f_spec = pl.BlockSpec((tm // l1, l1 * w_fn), lambda i: (i, 0))
        f_shape = jax.ShapeDtypeStruct((n // l1, l1 * w_fn), F32)
        scratch = [pltpu.VMEM((w_fn // LANES, sub, LANES), F32)]
    return pl.pallas_call(
        kern,
        grid=(nt,),
        in_specs=[pl.BlockSpec((tm, d), lambda i: (i, 0)),
                  pl.BlockSpec((None, 1, d), lambda i: (row(i), 0, 0)),
                  pl.BlockSpec((None, 1, d), lambda i: (row(i), 0, 1)),
                  pl.BlockSpec((None, 1, d), lambda i: (layer, 0, 0)),
                  _layer_resident(w_in, layer),
                  pl.BlockSpec((tm, LANES), pos),
                  pl.BlockSpec((tm, LANES), pos)],
        out_specs=[pl.BlockSpec((w_ssm // LANES, tm, LANES), lambda i: (0, i, 0)),
                   f_spec,
                   pl.BlockSpec((tm, w_na), lambda i: (i, 0)),
                   pl.BlockSpec((tm, w_na), lambda i: (i, 0)),
                   pl.BlockSpec((tm, w_na), lambda i: (i, 0))],
        out_shape=[jax.ShapeDtypeStruct((w_ssm // LANES, n, LANES), F32),
                   f_shape,
                   jax.ShapeDtypeStruct((n, w_na), BF),
                   jax.ShapeDtypeStruct((n, w_na), BF),
                   jax.ShapeDtypeStruct((n, w_na), BF)],
        scratch_shapes=scratch,
        compiler_params=_cparams("parallel"),
        name="inproj_rope" if rope else "inproj_ctx",
    )(x2d, mod3, mod3, g_pre.reshape(-1, 1, d), w_in, cos, sin)


def _rope_tables(seq):
    t = np.arange(seq)
    row = (t // GRID_W).astype(np.float32)
    col = (t % GRID_W).astype(np.float32)
    quarter = HEAD_DIM // 4
    freqs = (np.float32(ROPE_BASE) ** (-np.arange(quarter, dtype=np.float32) / quarter)).astype(np.float32)
    d = np.arange(LANES) % HEAD_DIM
    use_col = (d // (HEAD_DIM // 2)) == 1
    fidx = d % quarter
    sign = np.where((d % (HEAD_DIM // 2)) // quarter == 0, -1.0, 1.0)
    pos = np.where(use_col[None, :], col[:, None], row[:, None])
    ang = (pos * freqs[fidx][None, :]).astype(np.float32).astype(np.float64)
    return jnp.asarray(np.cos(ang), F32), jnp.asarray(np.sin(ang) * sign[None, :], F32)


def _gelu_tanh(x):
    c = math.sqrt(2.0 / math.pi)
    return x * (0.5 * (1.0 + jnp.tanh(c * (x + 0.044715 * (x * x * x)))))


def _group_of(idx, width, groups):
    shift = width.bit_length() - 1
    assert width == 1 << shift and groups & (groups - 1) == 0
    return lax.bitwise_and(lax.shift_right_logical(idx, shift), groups - 1)


def _same_group(shape, row0, row_width, col0, col_width, groups):
    r = lax.broadcasted_iota(jnp.int32, shape, 0) + row0
    c = lax.broadcasted_iota(jnp.int32, shape, 1) + col0
    return _group_of(r, row_width, groups) == _group_of(c, col_width, groups)


def _swap_stage(tiles, d):
    lane = lax.broadcasted_iota(jnp.int32, tiles[0].shape, 1)
    upper = lax.bitwise_and(lax.shift_right_logical(lane, SSM_GROUP.bit_length() - 1), d) != 0
    out = list(tiles)
    for x in range(len(tiles)):
        if x & d:
            continue
        a, b = tiles[x], tiles[x + d]
        out[x] = jnp.where(upper, pltpu.roll(b, SSM_GROUP * d, 1), a)
        out[x + d] = jnp.where(upper, b, pltpu.roll(a, LANES - SSM_GROUP * d, 1))
    return out


def _ssm_kernel(ul_ref, uc_ref, wc_ref, lag_ref, dc_ref, rexp_ref, cexp_ref, lam_ref, lamseg_ref,
                z_ref, zc_ref, u_s, bm_s, kin_s, cm_s, s_ref, h_ref, e_ref, *, seg_len, n_lat, n_ctx, stages):
    T = SSM_T
    ns = 2 ** len(stages)
    gs = T // ns
    sw = gs * LANES
    rows = u_s.shape[1]
    tps = s_ref.shape[0] // ns
    tpq = tps // 4
    nst = tpq * LANES
    gpt = LANES // SSM_GROUP
    per_q = gpt * SSM_STATE

    def lanes(c):
        return slice(c * LANES, (c + 1) * LANES)

    def col0(s, q):
        return q * per_q + s * nst

    def swapped(tiles, order):
        for d in order:
            tiles = _swap_stage(tiles, d)
        return tiles

    @pl.when(pl.program_id(1) == 0)
    def _expand_operators():
        rexp = rexp_ref[...]
        cexp = cexp_ref[...]
        for s in range(ns):
            wsel = jnp.concatenate([wc_ref[:, col0(s, q):col0(s, q) + nst] for q in range(4)], axis=1)
            blk = jnp.dot(rexp, wsel, preferred_element_type=F32)
            keep = _same_group(blk.shape, 0, SSM_GROUP, 0, SSM_STATE, gs)
            bm_s[s] = jnp.where(keep, blk, 0.0).astype(BF)
            dsel = jnp.concatenate([dc_ref[col0(s, q):col0(s, q) + nst, :] for q in range(4)], axis=0)
            blk = jnp.dot(dsel, cexp, preferred_element_type=F32)
            keep = _same_group(blk.shape, 0, SSM_STATE, 0, SSM_GROUP, gs)
            cm_s[s] = jnp.where(keep, blk, 0.0).astype(BF)
        r16 = lax.broadcasted_iota(jnp.int32, (LANES, SSM_GROUP), 0)
        c16 = lax.broadcasted_iota(jnp.int32, (LANES, SSM_GROUP), 1)
        chan = jnp.where(lax.bitwise_and(r16, SSM_GROUP - 1) == c16, 1.0, 0.0).astype(BF)
        for t in range(T):
            lo = (T - 1 - t) * LANES
            blk = jnp.dot(chan, lag_ref[:, lo:lo + T * LANES], preferred_element_type=F32)
            keep = _same_group(blk.shape, 0, SSM_GROUP, 0, SSM_GROUP, gpt)
            blk = jnp.where(keep, blk, 0.0)
            tiles = swapped([blk[:, lanes(x)] for x in range(T)], stages)
            r0 = (t % gs) * LANES + (t // gs) * gs * SSM_GROUP
            for s in range(ns):
                part = jnp.concatenate(tiles[s * gs:(s + 1) * gs], axis=1)
                kin_s[s, r0:r0 + gs * SSM_GROUP, :] = part[s * gs * SSM_GROUP:(s + 1) * gs * SSM_GROUP, :].astype(BF)

    xs = []
    for t in range(T):
        ct = uc_ref[pl.ds(t, n_ctx, stride=T), :]
        xs.append(jnp.concatenate([ct, ul_ref[pl.ds(t, n_lat, stride=T), :], ct], axis=0))
    xs = swapped(xs, stages)
    for s in range(ns):
        for k in range(gs):
            u_s[s, :, lanes(k)] = xs[s * gs + k].astype(BF)
    for s in range(ns):
        r = jnp.dot(u_s[s], bm_s[s], preferred_element_type=F32)
        for c in range(tps):
            s_ref[s * tps + c] = r[:, lanes(c)]

    chains = []
    for s in range(ns):
        for d in range(2):
            for k in range(tpq):
                cr, ci = s * tps + 2 * d * tpq + k, s * tps + (2 * d + 1) * tpq + k
                lr = slice(col0(s, 2 * d) + k * LANES, col0(s, 2 * d) + (k + 1) * LANES)
                li = slice(col0(s, 2 * d + 1) + k * LANES, col0(s, 2 * d + 1) + (k + 1) * LANES)
                chains.append((cr, ci, d, lr, li))
    nt = ns * tps

    def scan_step(i, carry, record):
        new = list(carry)
        for cr, ci, d, lr_sl, li_sl in chains:
            rsl = pl.ds(i if d == 0 else seg_len - 1 - i, SSM_SEG, stride=seg_len)
            hr, hi = carry[cr], carry[ci]
            if record:
                h_ref[cr, rsl, :] = hr
                h_ref[ci, rsl, :] = hi
            lr, li = lam_ref[:, lr_sl], lam_ref[:, li_sl]
            new[cr] = lr * hr - li * hi + s_ref[cr, rsl, :]
            new[ci] = lr * hi + li * hr + s_ref[ci, rsl, :]
        return tuple(new)

    zero = jnp.zeros((SSM_SEG, LANES), F32)
    fin = lax.fori_loop(0, seg_len, lambda i, c: scan_step(i, c, False), (zero,) * nt, unroll=4)
    for c in range(nt):
        e_ref[c, 0:SSM_SEG, :] = fin[c]

    for cr, ci, d, lr_sl, li_sl in chains:
        lr, li = lamseg_ref[:, lr_sl], lamseg_ref[:, li_sl]
        er = jnp.zeros((1, LANES), F32)
        ei = jnp.zeros((1, LANES), F32)
        for n in range(SSM_SEG):
            g = n if d == 0 else SSM_SEG - 1 - n
            e_ref[cr, SSM_SEG + g:SSM_SEG + g + 1, :] = er
            e_ref[ci, SSM_SEG + g:SSM_SEG + g + 1, :] = ei
            fr, fi = e_ref[cr, g:g + 1, :], e_ref[ci, g:g + 1, :]
            er, ei = lr * er - li * ei + fr, lr * ei + li * er + fi

    ent = tuple(e_ref[c, SSM_SEG:2 * SSM_SEG, :] for c in range(nt))
    lax.fori_loop(0, seg_len, lambda i, c: scan_step(i, c, True), ent, unroll=2)

    ys, ycs = [], []
    for s in range(ns):
        hcat = jnp.concatenate([h_ref[s * tps + c].astype(BF) for c in range(tps)], axis=1)
        y = _gelu_tanh(jnp.dot(u_s[s], kin_s[s], preferred_element_type=F32)
                       + jnp.dot(hcat, cm_s[s], preferred_element_type=F32))
        ys += [y[:, lanes(k)] for k in range(gs)]
        hc = jnp.concatenate(
            [h_ref[s * tps + c, 0:n_ctx, :].astype(BF) for c in range(2 * tpq)]
            + [h_ref[s * tps + c, rows - n_ctx:rows, :].astype(BF) for c in range(2 * tpq, tps)], axis=1)
        yc = _gelu_tanh(jnp.dot(u_s[s, 0:n_ctx, :], kin_s[s], preferred_element_type=F32)
                        + jnp.dot(hc, cm_s[s], preferred_element_type=F32))
        ycs += [yc[:, lanes(k)] for k in range(gs)]
    ys = swapped(ys, stages[::-1])
    ycs = swapped(ycs, stages[::-1])
    for t in range(T):
        z_ref[pl.ds(t, n_lat, stride=T), :] = ys[t][n_ctx:n_ctx + n_lat, :]
        zc_ref[pl.ds(t, n_ctx, stride=T), :] = ycs[t]


def _ssm_weights(a_re, a_im, log_dt, b_re, b_im, c_re, c_im, d_skip, seg_len):
    T = SSM_T
    g, p = a_re.shape[1], a_re.shape[2]
    hc = b_re.shape[-1]
    gpt = LANES // hc
    J = g // gpt
    lam = lax.complex(a_re.astype(F32), a_im.astype(F32))
    dt = jnp.exp(log_dt.astype(F32))[..., None]
    ldt = lam * dt
    lam_bar = jnp.exp(ldt)
    bbar = ((lam_bar - 1) / lam)[..., None] * lax.complex(b_re.astype(F32), b_im.astype(F32))
    cmat = lax.complex(c_re.astype(F32), c_im.astype(F32))
    kk = jnp.arange(T + 1, dtype=F32)
    pw = jnp.exp(ldt[:, None] * kk[None, :, None, None])

    wf = pw[0, :T][::-1][:, :, None, :] * jnp.swapaxes(bbar[0], 1, 2)[None]
    wb = pw[1, :T][:, :, None, :] * jnp.swapaxes(bbar[1], 1, 2)[None]
    wq = jnp.stack([wf.real, wf.imag, wb.real, wb.imag]).reshape(4, T, J, gpt, hc, p)
    wc = wq.transpose(2, 1, 4, 0, 3, 5).reshape(J, T * hc, 4 * gpt * p)

    df = cmat[0][None] * pw[0, 1:][:, :, None, :]
    db = cmat[1][None] * pw[1, 1:][::-1][:, :, None, :]
    dq = jnp.stack([df.real, -df.imag, db.real, -db.imag]).reshape(4, T, J, gpt, hc, p)
    dc = dq.transpose(2, 0, 3, 5, 1, 4).reshape(J, 4 * gpt * p, T * hc)

    mf = jnp.einsum('gcp,kgp,gph->khgc', cmat[0], pw[0, :T], bbar[0]).real
    mb = jnp.einsum('gcp,kgp,gph->khgc', cmat[1], pw[1, :T], bbar[1]).real
    skip = jnp.eye(hc, dtype=F32)[:, None, :] * d_skip.astype(F32).reshape(g, hc)[None]
    m0 = mf[0] + mb[0] + skip
    lags = jnp.concatenate([mb[1:][::-1], m0[None], mf[1:]], axis=0)
    kc = lags.reshape(2 * T - 1, hc, J, LANES).transpose(2, 1, 0, 3).reshape(J, hc, (2 * T - 1) * LANES)

    def lam_rows(power):
        lp = jnp.exp(ldt * power)
        lt = lp.reshape(2, J, gpt * p)
        return jnp.concatenate([lt[0].real, lt[0].imag, lt[1].real, lt[1].imag], axis=-1)[:, None, :]

    lam_t = jnp.broadcast_to(lam_rows(float(T)), (J, SSM_SEG, 4 * gpt * p))
    return wc.astype(BF), kc.astype(BF), dc.astype(BF), lam_t, lam_rows(float(T * seg_len))


def _ssm_expanders(gs):
    r = np.arange(gs * LANES)
    k, slot, ch = r // LANES, (r % LANES) // SSM_GROUP, r % SSM_GROUP
    t = (slot // gs) * gs + k
    rexp = np.zeros((gs * LANES, SSM_T * SSM_GROUP), np.float32)
    rexp[r, t * SSM_GROUP + ch] = 1.0
    return jnp.asarray(rexp, BF), jnp.asarray(rexp.T, BF)


def _ssm(u3, u3c, wc, kc, dc, lam, lamseg, *, layer, batch, seq, n_ctx):
    T = SSM_T
    J = u3.shape[0]
    ns = wc.shape[-1]
    nsets = 2 ** len(SSM_STAGES)
    gs = T // nsets
    sw = gs * LANES
    n_lat, n_c = seq // T, n_ctx // T
    rows = n_lat + 2 * n_c
    seg_len = rows // SSM_SEG
    assert rows % SSM_SEG == 0 and n_c % 16 == 0
    rexp, cexp = _ssm_expanders(gs)
    kern = functools.partial(_ssm_kernel, seg_len=seg_len, n_lat=n_lat, n_ctx=n_c, stages=SSM_STAGES)
    const = lambda j, b: (0, 0)
    per_j = lambda j, b: (layer, j, 0, 0)
    per_jb = lambda j, b: (j, b, 0)
    return pl.pallas_call(
        kern,
        grid=(J, batch),
        in_specs=[pl.BlockSpec((None, seq, LANES), per_jb),
                  pl.BlockSpec((None, n_ctx, LANES), per_jb),
                  pl.BlockSpec((None, None) + wc.shape[2:], per_j),
                  pl.BlockSpec((None, None) + kc.shape[2:], per_j),
                  pl.BlockSpec((None, None) + dc.shape[2:], per_j),
                  pl.BlockSpec(rexp.shape, const),
                  pl.BlockSpec(cexp.shape, const),
                  pl.BlockSpec((None, None, SSM_SEG, ns), per_j),
                  pl.BlockSpec((None, None, 1, ns), per_j)],
        out_specs=[pl.BlockSpec((None, seq, LANES), per_jb),
                   pl.BlockSpec((None, n_ctx, LANES), per_jb)],
        out_shape=[jax.ShapeDtypeStruct(u3.shape, F32),
                   jax.ShapeDtypeStruct(u3c.shape, F32)],
        scratch_shapes=[pltpu.VMEM((nsets, rows, sw), BF),
                        pltpu.VMEM((nsets, sw, ns // nsets), BF),
                        pltpu.VMEM((nsets, sw, sw), BF),
                        pltpu.VMEM((nsets, ns // nsets, sw), BF),
                        pltpu.VMEM((ns // LANES, rows, LANES), F32),
                        pltpu.VMEM((ns // LANES, rows, LANES), F32),
                        pltpu.VMEM((ns // LANES, 2 * SSM_SEG, LANES), F32)],
        compiler_params=_cparams("arbitrary", "arbitrary"),
        name="ssm_chunk_scan",
    )(u3, u3c, wc, kc, dc, rexp, cexp, lam, lamseg)


def _dft_a_kernel(w_ref, x_ref, o_ref):
    o_ref[...] = jnp.dot(w_ref[...], x_ref[...].astype(BF), preferred_element_type=F32).astype(BF)


def _dft_rows(n):
    k = np.arange(n)
    m = (k[:, None] * k[None, :]) % n
    ang = m * (2.0 * math.pi / n)
    return jnp.asarray(np.concatenate([np.cos(ang), -np.sin(ang)], axis=0), BF)


def _dft_a(w, x3):
    B, n, cols = x3.shape
    tn = min(cols, 4096)
    return pl.pallas_call(
        _dft_a_kernel,
        grid=(B, cols // tn),
        in_specs=[pl.BlockSpec((2 * n, n), lambda b, i: (0, 0)),
                  pl.BlockSpec((None, n, tn), lambda b, i: (b, 0, i))],
        out_specs=pl.BlockSpec((None, 2 * n, tn), lambda b, i: (b, 0, i)),
        out_shape=jax.ShapeDtypeStruct((B, 2 * n, cols), BF),
        compiler_params=_cparams("parallel", "parallel"),
        name="dft_stage_a",
    )(w, x3)


def _store_lane_tiles(o_ref, p, n, wf, idx):
    per = wf // LANES
    for c in range(2 * per):
        part = p[(c // per) * n:(c // per + 1) * n, (c % per) * LANES:(c % per + 1) * LANES]
        o_ref[(c,) + idx] = part


def _dft_b_kernel(a_ref, g_ref, o_ref, *, kb, l1, wf):
    for i in range(kb):
        g = jnp.concatenate([g_ref[0, i], g_ref[1, i]], axis=0)
        p = jnp.dot(a_ref[i], g, preferred_element_type=F32)
        _store_lane_tiles(o_ref, p, l1, wf, (i,))


def _dft_b_mats(seq, l1, l2):
    k2 = np.arange(l2)[:, None, None]
    k1 = np.arange(l1)[None, :, None]
    j1 = np.arange(l1)[None, None, :]
    m = ((k1 * l2 + k2) * j1) % seq
    ang = m * (2.0 * math.pi / seq)
    ar, ai = np.cos(ang).astype(np.float32), -np.sin(ang).astype(np.float32)
    top = np.concatenate([ar, -ai], axis=2)
    bot = np.concatenate([ai, ar], axis=2)
    return jnp.asarray(np.concatenate([top, bot], axis=1), BF)


def _dft_b(amat, g5, *, kb):
    B, _, l2, l1, wf = g5.shape
    nlt = 2 * wf // LANES
    kern = functools.partial(_dft_b_kernel, kb=kb, l1=l1, wf=wf)
    return pl.pallas_call(
        kern,
        grid=(B, l2 // kb),
        in_specs=[pl.BlockSpec((kb, 2 * l1, 2 * l1), lambda b, i: (i, 0, 0)),
                  pl.BlockSpec((None, 2, kb, l1, wf), lambda b, i: (b, 0, i, 0, 0))],
        out_specs=pl.BlockSpec((None, nlt, kb, l1, LANES), lambda b, i: (b, 0, i, 0, 0)),
        out_shape=jax.ShapeDtypeStruct((B, nlt, l2, l1, LANES), F32),
        compiler_params=_cparams("parallel", "parallel"),
        name="dft_stage_b",
    )(amat, g5)


def _dft_ctx_kernel(w_ref, x_ref, o_ref, *, n, wf):
    p = jnp.dot(w_ref[...], x_ref[...], preferred_element_type=F32)
    _store_lane_tiles(o_ref, p, n, wf, ())


def _dft_ctx(w, x3):
    B, n, wf = x3.shape
    nlt = 2 * wf // LANES
    kern = functools.partial(_dft_ctx_kernel, n=n, wf=wf)
    return pl.pallas_call(
        kern,
        grid=(B,),
        in_specs=[pl.BlockSpec((2 * n, n), lambda b: (0, 0)),
                  pl.BlockSpec((None, n, wf), lambda b: (b, 0, 0))],
        out_specs=pl.BlockSpec((None, nlt, n, LANES), lambda b: (b, 0, 0, 0)),
        out_shape=jax.ShapeDtypeStruct((B, nlt, n, LANES), F32),
        compiler_params=_cparams("parallel"),
        name="dft_ctx",
    )(w, x3)


def _channel_dft(wf, seq):
    c = np.arange(wf)
    same = (c[:, None] // FNET_GROUP) == (c[None, :] // FNET_GROUP)
    m = ((c[:, None] % FNET_GROUP) * (c[None, :] % FNET_GROUP)) % FNET_GROUP
    ang = m * (2.0 * math.pi / FNET_GROUP)
    mask = same / math.sqrt(seq * FNET_GROUP)
    return jnp.asarray(np.concatenate([np.cos(ang) * mask, np.sin(ang) * mask], axis=0), BF)


def _softmax_pv(parts, l_shape):
    m = parts[0][0].max(axis=-1, keepdims=True)
    for s, _ in parts[1:]:
        m = jnp.maximum(m, s.max(axis=-1, keepdims=True))
    l = jnp.zeros(l_shape, F32)
    o = None
    for s, v in parts:
        p = jnp.exp2(s - m)
        l = l + p.sum(axis=-1, keepdims=True)
        pv = jnp.dot(p.astype(BF), v, preferred_element_type=F32)
        o = pv if o is None else o + pv
    return o / l


def _na_kernel(kb_ref, pid_ref, dr_ref, q_ref, k_ref, v_ref, kc_ref, vc_ref, cbx_ref, o_ref, bias_ref,
               *, nkeys, heads, nsub):
    step = pl.program_id(1)
    nq = q_ref.shape[0] // nsub
    rq, kr = nq // GRID_W, nkeys // GRID_W
    lane = lax.broadcasted_iota(jnp.int32, (nq, LANES), 1)
    low = lane < HEAD_DIM
    dn = (((1,), (1,)), ((), ()))

    @pl.when((pl.program_id(0) == 0) & (step == 0))
    def _build_bias():
        for pat in range(bias_ref.shape[0]):
            for qi in range(rq):
                for w in range(kr):
                    a = dr_ref[(pat * rq + qi) * kr + w]
                    for h in range(heads):
                        bias_ref[pat, h, qi * GRID_W:(qi + 1) * GRID_W, w * GRID_W:(w + 1) * GRID_W] = cbx_ref[h, a]

    tasks = [(sb, h) for sb in range(nsub) for h in range(heads)]
    starts = [pl.multiple_of(kb_ref[step * nsub + sb] * GRID_W, GRID_W) for sb in range(nsub)]
    pats = [pid_ref[step * nsub + sb] for sb in range(nsub)]

    def scores(sb, h):
        sl = slice((h // 2) * LANES, (h // 2 + 1) * LANES)
        q2 = q_ref[sb * nq:(sb + 1) * nq, sl]
        qm = jnp.where(low if h % 2 == 0 else jnp.logical_not(low), q2, jnp.zeros_like(q2))
        s_w = lax.dot_general(qm, k_ref[pl.ds(starts[sb], nkeys), sl], dn, preferred_element_type=F32)
        s_c = lax.dot_general(qm, kc_ref[:, sl], dn, preferred_element_type=F32)
        return s_w + bias_ref[pats[sb], h], s_c

    def weights(s_w, s_c):
        m = jnp.maximum(s_w.max(axis=-1, keepdims=True), s_c.max(axis=-1, keepdims=True))
        p_w = jnp.exp2(s_w - m)
        p_c = jnp.exp2(s_c - m)
        l = p_w.sum(axis=-1, keepdims=True) + p_c.sum(axis=-1, keepdims=True)
        return p_w.astype(BF), p_c.astype(BF), l

    outs = {}

    def finish(sb, h, p_w, p_c, l):
        sl = slice((h // 2) * LANES, (h // 2 + 1) * LANES)
        o = (jnp.dot(p_w, v_ref[pl.ds(starts[sb], nkeys), sl], preferred_element_type=F32)
             + jnp.dot(p_c, vc_ref[:, sl], preferred_element_type=F32))
        outs[h % 2] = o / l
        if h % 2 == 1:
            o_ref[sb * nq:(sb + 1) * nq, sl] = jnp.where(low, outs[0], outs[1]).astype(BF)

    nt = len(tasks)
    sc = {0: scores(*tasks[0])}
    if nt > 1:
        sc[1] = scores(*tasks[1])
    pw = {0: weights(*sc.pop(0))}
    for n in range(nt):
        if n + 2 < nt:
            sc[n + 2] = scores(*tasks[n + 2])
        if n + 1 < nt:
            pw[n + 1] = weights(*sc.pop(n + 1))
        finish(*tasks[n], *pw.pop(n))


def _na_plan(rows, rq):
    wr = min(NA_ROWS, rows)
    kr = rq + wr - 1
    nblk = rows // rq
    kbs, pids, pats = [], [], []
    for blk in range(nblk):
        r0 = blk * rq
        rs = [int(np.clip(r0 + i - wr // 2, 0, rows - wr)) for i in range(rq)]
        kb = min(rs[0], rows - kr)
        pat = (tuple(r - kb for r in rs), r0 - kb)
        if pat not in pats:
            pats.append(pat)
        kbs.append(kb)
        pids.append(pats.index(pat))
    return np.asarray(kbs, np.int32), np.asarray(pids, np.int32), pats, kr, wr


def _na_bias(rpb, pats, rq, kr, wr):
    cols = np.arange(GRID_W)
    cstart = np.clip(cols - NA_COLS // 2, 0, GRID_W - NA_COLS)
    kc = np.arange(GRID_W)
    col_ok = (kc[None, :] >= cstart[:, None]) & (kc[None, :] < cstart[:, None] + NA_COLS)
    dc = kc[None, :] - cols[:, None] + NA_COLS - 1
    depth, heads, nr, ncol = rpb.shape
    sel = (dc[:, :, None] == np.arange(ncol)[None, None, :]) & col_ok[:, :, None]
    cb = jnp.einsum('lhab,ckb->lhack', rpb.astype(F32), jnp.asarray(sel, F32), precision=lax.Precision.HIGHEST)
    cb = jnp.where(jnp.asarray(col_ok)[None, None, None], cb * math.log2(math.e), NEG)
    cbx = jnp.concatenate([cb, jnp.full((depth, heads, 1, GRID_W, GRID_W), NEG, F32)], axis=2)
    slots = []
    for rs_off, r_off in pats:
        i = np.arange(rq)[:, None]
        w = np.arange(kr)[None, :]
        rso = np.asarray(rs_off)[:, None]
        row_ok = (w >= rso) & (w < rso + wr)
        slots.append(np.where(row_ok, w - r_off - i + NA_ROWS - 1, nr))
    return cbx, np.stack(slots).reshape(-1).astype(np.int32)


def _na_tables(rpb_all, seq, rq):
    kbs, pids, pats, kr, wr = _na_plan(seq // GRID_W, rq)
    cbx, slots = _na_bias(rpb_all, pats, rq, kr, wr)
    return dict(kbs=jnp.asarray(kbs), pids=jnp.asarray(pids), slots=jnp.asarray(slots), cbx=cbx, kr=kr, rq=rq,
                npat=len(pats))


def _na_latent(q, k, v, kc, vc, tabs, layer, *, batch, seq, n_ctx):
    rows = seq // GRID_W
    cbx, rq, kr = tabs["cbx"], tabs["rq"], tabs["kr"]
    heads = cbx.shape[1]
    nblk = rows // rq
    nq = rq * GRID_W
    nkeys = kr * GRID_W
    w = q.shape[1]
    nsub = 2 if nblk % 2 == 0 else 1
    nstep = nblk // nsub
    kern = functools.partial(_na_kernel, nkeys=nkeys, heads=heads, nsub=nsub)
    grid_spec = pltpu.PrefetchScalarGridSpec(
        num_scalar_prefetch=3,
        grid=(batch, nstep),
        in_specs=[pl.BlockSpec((nsub * nq, w), lambda b, i, *_: (b * nstep + i, 0)),
                  pl.BlockSpec((seq, w), lambda b, i, *_: (b, 0)),
                  pl.BlockSpec((seq, w), lambda b, i, *_: (b, 0)),
                  pl.BlockSpec((n_ctx, w), lambda b, i, *_: (b, 0)),
                  pl.BlockSpec((n_ctx, w), lambda b, i, *_: (b, 0)),
                  pl.BlockSpec((None,) + cbx.shape[1:], lambda b, i, *_: (layer, 0, 0, 0, 0))],
        out_specs=pl.BlockSpec((nsub * nq, w), lambda b, i, *_: (b * nstep + i, 0)),
        scratch_shapes=[pltpu.VMEM((tabs["npat"], heads, nq, nkeys), F32)],
    )
    return pl.pallas_call(
        kern,
        grid_spec=grid_spec,
        out_shape=jax.ShapeDtypeStruct(q.shape, BF),
        compiler_params=_cparams("arbitrary", "arbitrary"),
        name="na_latent",
    )(tabs["kbs"], tabs["pids"], tabs["slots"], q, k, v, kc, vc, cbx)


def _na_ctx_kernel(q_ref, k_ref, v_ref, o_ref, *, heads):
    nq = q_ref.shape[0]
    lane = lax.broadcasted_iota(jnp.int32, (nq, LANES), 1)
    low = lane < HEAD_DIM
    dn = (((1,), (1,)), ((), ()))
    for hp in range(heads // 2):
        sl = slice(hp * LANES, (hp + 1) * LANES)
        q2 = q_ref[:, sl]
        k2 = k_ref[:, sl]
        v2 = v_ref[:, sl]
        outs = []
        for hh in range(2):
            qm = jnp.where(low if hh == 0 else jnp.logical_not(low), q2, jnp.zeros_like(q2))
            s = lax.dot_general(qm, k2, dn, preferred_element_type=F32)
            outs.append(_softmax_pv([(s, v2)], (nq, 1)))
        o_ref[:, sl] = jnp.where(low, outs[0], outs[1]).astype(BF)


def _na_ctx(qc, kc, vc, *, batch, n_ctx, heads):
    w = qc.shape[1]
    spec = pl.BlockSpec((n_ctx, w), lambda b: (b, 0))
    return pl.pallas_call(
        functools.partial(_na_ctx_kernel, heads=heads),
        grid=(batch,),
        in_specs=[spec, spec, spec],
        out_specs=spec,
        out_shape=jax.ShapeDtypeStruct(qc.shape, BF),
        compiler_params=_cparams("parallel"),
        name="na_ctx",
    )(qc, kc, vc)


def _outproj_kernel(z_ref, pf_ref, na_ref, x_ref, gate_ref, gpost_ref,
                    wglu_ref, cs_ref, wfo_ref, wout_ref, o_ref, *, widths, sub):
    w_ssm, w_fn, w_na = widths
    wglu, wfo, wout = (r[...].astype(BF) for r in (wglu_ref, wfo_ref, wout_ref))
    for r0 in range(0, x_ref.shape[0], sub):
        rs = slice(r0, r0 + sub)
        z = jnp.concatenate([z_ref[j, rs, :] for j in range(w_ssm // LANES)], axis=1)
        gl = jnp.dot(z.astype(BF), wglu, preferred_element_type=F32)
        y_ssm = (z * jax.nn.sigmoid(gl)).astype(BF)
        if len(pf_ref.shape) == 3:
            tiles = [pf_ref[c, rs, :] for c in range(pf_ref.shape[0])]
        else:
            l2 = pf_ref.shape[1]
            tiles = [jnp.concatenate([pf_ref[c, :, k1, :] for k1 in range(r0 // l2, (r0 + sub) // l2)], axis=0)
                     for c in range(pf_ref.shape[0])]
        pf = jnp.concatenate(tiles, axis=1).astype(BF)
        mixed = jnp.dot(pf, cs_ref[...], preferred_element_type=F32).astype(BF)
        y_fft = jnp.dot(mixed, wfo, preferred_element_type=F32).astype(BF)
        y = jnp.concatenate([y_ssm, y_fft, na_ref[rs, :]], axis=1)
        o = jnp.dot(y, wout, preferred_element_type=F32)
        ms = jnp.mean(o * o, axis=-1, keepdims=True)
        o_ref[rs, :] = x_ref[rs, :] + gate_ref[...] * (o * lax.rsqrt(ms + EPS) * gpost_ref[...])


def _outproj(z3, pf, yna, x2d, mod3, g_post, w_glu, cs, w_fo, w_out, *,
             layer, tm, tiles_per_batch, mod_row, widths):
    n, d = x2d.shape
    w_ssm, w_fn, w_na = widths
    if mod_row is None:
        row = lambda i: i // tiles_per_batch
    else:
        row = lambda i: mod_row
    kern = functools.partial(_outproj_kernel, widths=widths, sub=min(tm, 512))
    return pl.pallas_call(
        kern,
        grid=(n // tm,),
        in_specs=[pl.BlockSpec((w_ssm // LANES, tm, LANES), lambda i: (0, i, 0)),
                  (pl.BlockSpec((None, pf.shape[1], tm, LANES),
                                lambda i: (i // tiles_per_batch, 0, i % tiles_per_batch, 0))
                   if pf.ndim == 4 else
                   pl.BlockSpec((None, pf.shape[1], pf.shape[2], tm // pf.shape[2], LANES),
                                lambda i: (i // tiles_per_batch, 0, 0, i % tiles_per_batch, 0))),
                  pl.BlockSpec((tm, w_na), lambda i: (i, 0)),
                  pl.BlockSpec((tm, d), lambda i: (i, 0)),
                  pl.BlockSpec((None, 1, d), lambda i: (row(i), 0, 2)),
                  pl.BlockSpec((None, 1, d), lambda i: (layer, 0, 0)),
                  _layer_resident(w_glu, layer),
                  _resident(cs.shape, lambda i: (0, 0)),
                  _layer_resident(w_fo, layer),
                  _layer_resident(w_out, layer)],
        out_specs=pl.BlockSpec((tm, d), lambda i: (i, 0)),
        out_shape=jax.ShapeDtypeStruct((n, d), F32),
        compiler_params=_cparams("parallel"),
        name="outproj",
    )(z3, pf, yna, x2d, mod3, g_post.reshape(-1, 1, d), w_glu, cs, w_fo, w_out)


def _ffn_kernel(x_ref, sh_ref, sc_ref, gate_ref, gpre_ref, gpost_ref, wg_ref, wu_ref, wd_ref,
                o_ref, a_ref, *, sub, chunk):
    dff = wg_ref.shape[1]
    for r0 in range(0, x_ref.shape[0], sub):
        rs = slice(r0, r0 + sub)
        x = x_ref[rs, :]
        ms = jnp.mean(x * x, axis=-1, keepdims=True)
        m = ((x * lax.rsqrt(ms + EPS) * gpre_ref[...]) * (1.0 + sc_ref[...]) + sh_ref[...]).astype(BF)
        for c0 in range(0, dff, chunk):
            sl = slice(c0, min(c0 + chunk, dff))
            g = jnp.dot(m, wg_ref[:, sl], preferred_element_type=F32)
            u = jnp.dot(m, wu_ref[:, sl], preferred_element_type=F32)
            a_ref[rs, sl] = (g * jax.nn.sigmoid(g) * u).astype(BF)
        y = jnp.dot(a_ref[rs, :], wd_ref[...], preferred_element_type=F32)
        ms2 = jnp.mean(y * y, axis=-1, keepdims=True)
        o_ref[rs, :] = x + gate_ref[...] * (y * lax.rsqrt(ms2 + EPS) * gpost_ref[...])


FFN_WEIGHT_STEPS = 8


def _ffn_staged_kernel(*refs, sub, chunk, n_lat_tiles, with_ctx):
    if with_ctx:
        (x_ref, xc_ref, sh_ref, sc_ref, gate_ref, gpre_ref, gpost_ref, wg_ref, wu_ref, wd_ref,
         o_ref, oc_ref, a_ref, wg_s, wu_s, wd_s) = refs
    else:
        (x_ref, sh_ref, sc_ref, gate_ref, gpre_ref, gpost_ref, wg_ref, wu_ref, wd_ref,
         o_ref, a_ref, wg_s, wu_s, wd_s) = refs
    i = pl.program_id(0)
    common = (sh_ref, sc_ref, gate_ref, gpre_ref, gpost_ref, wg_s, wu_s, wd_s)

    @pl.when(i < FFN_WEIGHT_STEPS)
    def _cast_weights():
        for src, dst in ((wg_ref, wg_s), (wu_ref, wu_s), (wd_ref, wd_s)):
            rows = src.shape[0]
            dst[pl.ds(pl.multiple_of(i * rows, rows), rows), :] = src[...].astype(BF)

    @pl.when((i >= FFN_WEIGHT_STEPS) & (i < FFN_WEIGHT_STEPS + n_lat_tiles))
    def _latent():
        _ffn_kernel(x_ref, *common, o_ref, a_ref, sub=sub, chunk=chunk)

    if with_ctx:
        @pl.when(i >= FFN_WEIGHT_STEPS + n_lat_tiles)
        def _context():
            _ffn_kernel(xc_ref, *common, oc_ref, a_ref, sub=min(sub, xc_ref.shape[0]), chunk=chunk)


def _ffn(x2d, xc2d, mod3, g_pre, g_post, wg, wu, wd, *, layer, tm, tiles_per_batch, ctx_row):
    n, d = x2d.shape
    dff = wg.shape[2]
    nl = n // tm
    nw = FFN_WEIGHT_STEPS
    with_ctx = xc2d is not None
    gains = (g_pre.reshape(-1, 1, d), g_post.reshape(-1, 1, d))
    slab = lambda i: (layer, jnp.minimum(i, nw - 1), 0)
    weights = [pl.BlockSpec((None, d // nw, dff), slab), pl.BlockSpec((None, d // nw, dff), slab),
               pl.BlockSpec((None, dff // nw, d), slab)]
    gain_specs = [pl.BlockSpec((None, 1, d), lambda i: (layer, 0, 0))] * 2
    tile = lambda i: jnp.clip(i - nw, 0, nl - 1)
    row = lambda i: jnp.where(i < nw + nl, tile(i) // tiles_per_batch, ctx_row)
    lat = lambda i: (tile(i), 0)
    mods = [pl.BlockSpec((None, 1, d), functools.partial(lambda i, c: (row(i), 0, c), c=c)) for c in (3, 4, 5)]
    in_specs = [pl.BlockSpec((tm, d), lat)]
    out_specs = [pl.BlockSpec((tm, d), lat)]
    out_shape = [jax.ShapeDtypeStruct((n, d), F32)]
    args = [x2d]
    if with_ctx:
        nc = xc2d.shape[0]
        ctx = lambda i: (jnp.maximum(i - nw - nl, 0), 0)
        in_specs.append(pl.BlockSpec((nc, d), ctx, pipeline_mode=pl.Buffered(1)))
        out_specs.append(pl.BlockSpec((nc, d), ctx))
        out_shape.append(jax.ShapeDtypeStruct((nc, d), F32))
        args.append(xc2d)
    outs = pl.pallas_call(
        functools.partial(_ffn_staged_kernel, sub=min(tm, 512), chunk=3 * MXU_TILE, n_lat_tiles=nl,
                          with_ctx=with_ctx),
        grid=(nw + nl + int(with_ctx),),
        in_specs=in_specs + mods + gain_specs + weights,
        out_specs=out_specs,
        out_shape=out_shape,
        scratch_shapes=[pltpu.VMEM((tm, dff), BF), pltpu.VMEM((d, dff), BF), pltpu.VMEM((d, dff), BF),
                        pltpu.VMEM((dff, d), BF)],
        compiler_params=_cparams("arbitrary"),
        name="ffn_staged",
    )(*args, mod3, mod3, mod3, *gains, wg, wu, wd)
    return (outs[0], outs[1]) if with_ctx else (outs[0], None)


def _layer(x2d, xc2d, p, w, layer, *, batch, seq, n_ctx, last, rope_tabs, consts):
    d = x2d.shape[1]
    w_ssm = consts["w_ssm"]
    w_fn = w["w_fourier"].shape[1]
    w_na = (w["w_in"].shape[2] - w_ssm - w_fn) // 3
    widths = (w_ssm, w_fn, w_na)
    heads = w_na // HEAD_DIM
    tm = min(512, seq)
    tmc = min(512, batch * n_ctx)

    mod = w["mod"][layer]
    mod3 = mod.reshape(mod.shape[0], 1, 6 * d)

    cos, sin = rope_tabs
    tm2 = min(2 * tm, seq)
    l2 = consts["l2"]
    l1 = seq // l2
    u3, f, q, k, v = _inproj(x2d, mod3, p["g_pre_mix"], w["w_in"], cos, sin, layer=layer, tm=tm2,
                             tiles_per_batch=seq // tm2, mod_row=None, widths=widths, rope=True,
                             f_row_tokens=l1)
    ones = jnp.ones((tmc, LANES), F32)
    u3c, fc, qc, kc, vc = _inproj(xc2d, mod3, p["g_pre_mix"], w["w_in"], ones, ones, layer=layer, tm=tmc,
                                  tiles_per_batch=1, mod_row=batch, widths=widths, rope=False)

    z3, zc3 = _ssm(u3, u3c, *w["ssm"], layer=layer, batch=batch, seq=seq, n_ctx=n_ctx)

    l2 = consts["l2"]
    l1 = seq // l2
    g = _dft_a(consts["dft_a"], f.reshape(batch, l2, l1 * w_fn))
    pf = _dft_b(consts["dft_b"], g.reshape(batch, 2, l2, l1, w_fn), kb=min(8, l2))

    yna = _na_latent(q, k, v, kc, vc, consts["na"], layer, batch=batch, seq=seq, n_ctx=n_ctx)

    w_glu, w_fo, w_out = w["w_glu"], w["w_fourier"], w["w_out"]
    x2d = _outproj(z3, pf, yna, x2d, mod3, p["g_post_mix"], w_glu, consts["cs_lat"], w_fo, w_out,
                   layer=layer, tm=tm2, tiles_per_batch=seq // tm2, mod_row=None, widths=widths)
    if not last:
        pfc = _dft_ctx(consts["dft_ctx"], fc.reshape(batch, n_ctx, w_fn))
        ynac = _na_ctx(qc, kc, vc, batch=batch, n_ctx=n_ctx, heads=heads)
        tc = min(tmc, n_ctx)
        xc2d = _outproj(zc3, pfc, ynac, xc2d, mod3, p["g_post_mix"], w_glu, consts["cs_ctx"], w_fo, w_out,
                        layer=layer, tm=tc, tiles_per_batch=n_ctx // tc, mod_row=batch, widths=widths)
    else:
        xc2d = None
    wg, wu, wd = w["w_ffn_gate"], w["w_ffn_up"], w["w_ffn_down"]
    return _ffn(x2d, xc2d, mod3, p["g_pre_ffn"], p["g_post_ffn"], wg, wu, wd,
                layer=layer, tm=tm2, tiles_per_batch=seq // tm2, ctx_row=batch)


def kernel(x, c, ctx, c_ctx, w_mod, b_mod, g_pre_mix, g_post_mix, w_in, ssm_a_re, ssm_a_im, ssm_log_dt,
           ssm_b_re, ssm_b_im, ssm_c_re, ssm_c_im, ssm_d, w_glu, w_fourier, na_rpb, w_out, g_pre_ffn,
           g_post_ffn, w_ffn_gate, w_ffn_up, w_ffn_down):
    batch, seq, d = x.shape
    n_ctx = ctx.shape[1]
    depth = w_mod.shape[0]
    w_fn = w_fourier.shape[1]
    params = dict(g_pre_mix=g_pre_mix, g_post_mix=g_post_mix, g_pre_ffn=g_pre_ffn, g_post_ffn=g_post_ffn)
    seg_len = (seq + 2 * n_ctx) // SSM_T // SSM_SEG
    ssm_ops = jax.vmap(functools.partial(_ssm_weights, seg_len=seg_len))(
        ssm_a_re, ssm_a_im, ssm_log_dt, ssm_b_re, ssm_b_im, ssm_c_re, ssm_c_im, ssm_d)
    nrow = -(-(batch + 1) // 8) * 8
    cstack = jnp.concatenate([c, c_ctx[None, :], jnp.zeros((nrow - batch - 1, d), c.dtype)], axis=0)
    weights = dict(mod=_mod_rows(cstack, w_mod, b_mod), ssm=ssm_ops,
                   w_in=w_in, w_glu=w_glu, w_fourier=w_fourier, w_out=w_out,
                   w_ffn_gate=w_ffn_gate,
                   w_ffn_up=w_ffn_up, w_ffn_down=w_ffn_down)

    l2 = 64 if seq % (64 * 8) == 0 else 8
    l1 = seq // l2
    consts = dict(
        l2=l2,
        w_ssm=ssm_d.shape[1],
        na=_na_tables(na_rpb, seq, 4),
        dft_a=_dft_rows(l2).astype(BF),
        dft_b=_dft_b_mats(seq, l1, l2).astype(BF),
        dft_ctx=_dft_rows(n_ctx).astype(BF),
        cs_lat=_channel_dft(w_fn, seq).astype(BF),
        cs_ctx=_channel_dft(w_fn, n_ctx).astype(BF),
    )
    rope_tabs = _rope_tables(seq)

    x2d = x.reshape(batch * seq, d)
    xc2d = ctx.reshape(batch * n_ctx, d)
    for layer in range(depth):
        x2d, xc2d = _layer(x2d, xc2d, params, weights, layer, batch=batch, seq=seq, n_ctx=n_ctx,
                           last=(layer == depth - 1), rope_tabs=rope_tabs, consts=consts)
    return x2d.reshape(batch, seq, d)
```

```python
import functools
import math

import numpy as np
import jax
import jax.numpy as jnp
from jax import lax
from jax.experimental import pallas as pl
from jax.experimental.pallas import tpu as pltpu

BF = jnp.bfloat16
F32 = jnp.float32

EPS = 1e-6
GRID_W = 64
HEAD_DIM = 64
NA_ROWS = 8
NA_COLS = 16
ROPE_BASE = 10000.0
SSM_GROUP = 16
SSM_STATE = 64
FNET_GROUP = 64
LANES = 128
MXU_TILE = 256
SSM_T = 8
SSM_SEG = 8
SSM_STAGES = (4, 2)
NEG = -1e30
VMEM_LIMIT = 56 * 1024 * 1024


def _cparams(*sem):
    return pltpu.CompilerParams(dimension_semantics=sem, vmem_limit_bytes=VMEM_LIMIT)


def _resident(shape, index_map):
    return pl.BlockSpec(shape, index_map, pipeline_mode=pl.Buffered(1))


def _layer_resident(w_all, layer):
    return pl.BlockSpec((None,) + w_all.shape[1:], lambda *_: (layer, 0, 0), pipeline_mode=pl.Buffered(1))


def _mod_kernel(c_ref, w_ref, b_ref, o_ref):
    cs = c_ref[...]
    s = (cs * jax.nn.sigmoid(cs)).astype(BF)
    o_ref[...] = jnp.dot(s, w_ref[...].astype(BF), preferred_element_type=F32) + b_ref[...]


def _mod_rows(cstack, w_mod, b_mod):
    rows, d = cstack.shape
    depth, _, n = w_mod.shape
    bn = 2 * d if n % (2 * d) == 0 else d
    return pl.pallas_call(
        _mod_kernel,
        grid=(depth, n // bn),
        in_specs=[pl.BlockSpec((rows, d), lambda l, i: (0, 0)),
                  pl.BlockSpec((None, d, bn), lambda l, i: (l, 0, i)),
                  pl.BlockSpec((None, 1, bn), lambda l, i: (l, 0, i))],
        out_specs=pl.BlockSpec((None, rows, bn), lambda l, i: (l, 0, i)),
        out_shape=jax.ShapeDtypeStruct((depth, rows, n), F32),
        compiler_params=_cparams("parallel", "parallel"),
        name="mod_rows",
    )(cstack, w_mod, b_mod.reshape(depth, 1, n))


def _rope_tile(t, cos, sin, first):
    partner = jnp.where(first, pltpu.roll(t, LANES - 16, 1), pltpu.roll(t, 16, 1))
    return t * cos + partner * sin


def _inproj_kernel(x_ref, sh_ref, sc_ref, g_ref, w_ref, cos_ref, sin_ref,
                   u_ref, f_ref, q_ref, k_ref, v_ref, *, widths, rope, sub):
    w_ssm, w_fn, w_na = widths
    scale = HEAD_DIM ** -0.5 * math.log2(math.e)
    w = w_ref[...].astype(BF)
    for r0 in range(0, x_ref.shape[0], sub):
        rs = slice(r0, r0 + sub)
        x = x_ref[rs, :]
        ms = jnp.mean(x * x, axis=-1, keepdims=True)
        xn = x * lax.rsqrt(ms + EPS) * g_ref[...]
        m = (xn * (1.0 + sc_ref[...]) + sh_ref[...]).astype(BF)

        h = jnp.dot(m, w, preferred_element_type=F32)

        def proj(lo, n, h=h):
            return h[:, lo:lo + n]

        u = proj(0, w_ssm)
        for j in range(w_ssm // LANES):
            u_ref[j, rs, :] = u[:, j * LANES:(j + 1) * LANES]
        f_ref[rs, :] = proj(w_ssm, w_fn).astype(BF)
        q = proj(w_ssm + w_fn, w_na)
        k = proj(w_ssm + w_fn + w_na, w_na)
        v_ref[rs, :] = proj(w_ssm + w_fn + 2 * w_na, w_na).astype(BF)
        if rope:
            cos = cos_ref[rs, :]
            sin = sin_ref[rs, :]
            lane = lax.broadcasted_iota(jnp.int32, cos.shape, 1)
            first = (lane % 32) < 16
            for j in range(w_na // LANES):
                sl = slice(j * LANES, (j + 1) * LANES)
                q_ref[rs, sl] = (_rope_tile(q[:, sl], cos, sin, first) * scale).astype(BF)
                k_ref[rs, sl] = _rope_tile(k[:, sl], cos, sin, first).astype(BF)
        else:
            q_ref[rs, :] = (q * scale).astype(BF)
            k_ref[rs, :] = k.astype(BF)


def _inproj_pair_kernel(x_ref, xc_ref, sh_ref, sc_ref, g_ref, w_ref, cos_ref, sin_ref, *outs,
                        widths, sub, n_lat_tiles):
    common = (sh_ref, sc_ref, g_ref, w_ref, cos_ref, sin_ref)
    i = pl.program_id(0)

    @pl.when(i < n_lat_tiles)
    def _latent():
        _inproj_kernel(x_ref, *common, *outs[:5], widths=widths, rope=True, sub=sub)

    @pl.when(i >= n_lat_tiles)
    def _context():
        _inproj_kernel(xc_ref, *common, *outs[5:], widths=widths, rope=False, sub=min(sub, xc_ref.shape[0]))


def _inproj(x2d, xc2d, mod3, g_pre, w_in, cos, sin, *, layer, tm, tiles_per_batch, ctx_row, widths):
    n, d = x2d.shape
    nc = xc2d.shape[0]
    w_ssm, w_fn, w_na = widths
    nl = n // tm
    tile = lambda i: jnp.minimum(i, nl - 1)
    row = lambda i: jnp.where(i < nl, tile(i) // tiles_per_batch, ctx_row)
    pos = lambda i: (tile(i) % tiles_per_batch, 0)
    lat = lambda i: (tile(i), 0)
    one = lambda i: (0, 0)

    def outs(rows, idx, idx3):
        specs = [pl.BlockSpec((w_ssm // LANES, rows, LANES), idx3), pl.BlockSpec((rows, w_fn), idx)]
        specs += [pl.BlockSpec((rows, w_na), idx)] * 3
        return specs

    def shapes(rows):
        return ([jax.ShapeDtypeStruct((w_ssm // LANES, rows, LANES), F32), jax.ShapeDtypeStruct((rows, w_fn), BF)]
                + [jax.ShapeDtypeStruct((rows, w_na), BF)] * 3)

    kern = functools.partial(_inproj_pair_kernel, widths=widths, sub=min(tm, 512), n_lat_tiles=nl)
    res = pl.pallas_call(
        kern,
        grid=(nl + 1,),
        in_specs=[pl.BlockSpec((tm, d), lat),
                  pl.BlockSpec((nc, d), one),
                  pl.BlockSpec((None, 1, d), lambda i: (row(i), 0, 0)),
                  pl.BlockSpec((None, 1, d), lambda i: (row(i), 0, 1)),
                  pl.BlockSpec((None, 1, d), lambda i: (layer, 0, 0)),
                  _layer_resident(w_in, layer),
                  pl.BlockSpec((tm, LANES), pos),
                  pl.BlockSpec((tm, LANES), pos)],
        out_specs=outs(tm, lat, lambda i: (0, tile(i), 0)) + outs(nc, one, lambda i: (0, 0, 0)),
        out_shape=shapes(n) + shapes(nc),
        compiler_params=_cparams("arbitrary"),
        name="inproj",
    )(x2d, xc2d, mod3, mod3, g_pre.reshape(-1, 1, d), w_in, cos, sin)
    return res[:5], res[5:]


def _rope_tables(seq):
    t = np.arange(seq)
    row = (t // GRID_W).astype(np.float32)
    col = (t % GRID_W).astype(np.float32)
    quarter = HEAD_DIM // 4
    freqs = (np.float32(ROPE_BASE) ** (-np.arange(quarter, dtype=np.float32) / quarter)).astype(np.float32)
    d = np.arange(LANES) % HEAD_DIM
    use_col = (d // (HEAD_DIM // 2)) == 1
    fidx = d % quarter
    sign = np.where((d % (HEAD_DIM // 2)) // quarter == 0, -1.0, 1.0)
    pos = np.where(use_col[None, :], col[:, None], row[:, None])
    ang = (pos * freqs[fidx][None, :]).astype(np.float32).astype(np.float64)
    return jnp.asarray(np.cos(ang), F32), jnp.asarray(np.sin(ang) * sign[None, :], F32)


def _gelu_tanh(x):
    c = math.sqrt(2.0 / math.pi)
    return x * (0.5 * (1.0 + jnp.tanh(c * (x + 0.044715 * (x * x * x)))))


def _group_of(idx, width, groups):
    shift = width.bit_length() - 1
    assert width == 1 << shift and groups & (groups - 1) == 0
    return lax.bitwise_and(lax.shift_right_logical(idx, shift), groups - 1)


def _same_group(shape, row0, row_width, col0, col_width, groups):
    r = lax.broadcasted_iota(jnp.int32, shape, 0) + row0
    c = lax.broadcasted_iota(jnp.int32, shape, 1) + col0
    return _group_of(r, row_width, groups) == _group_of(c, col_width, groups)


def _swap_stage(tiles, d):
    lane = lax.broadcasted_iota(jnp.int32, tiles[0].shape, 1)
    upper = lax.bitwise_and(lax.shift_right_logical(lane, SSM_GROUP.bit_length() - 1), d) != 0
    out = list(tiles)
    for x in range(len(tiles)):
        if x & d:
            continue
        a, b = tiles[x], tiles[x + d]
        out[x] = jnp.where(upper, pltpu.roll(b, SSM_GROUP * d, 1), a)
        out[x + d] = jnp.where(upper, b, pltpu.roll(a, LANES - SSM_GROUP * d, 1))
    return out


def _ssm_kernel(ul_ref, uc_ref, wc_ref, lag_ref, dc_ref, rexp_ref, cexp_ref, lam_ref, lamseg_ref,
                z_ref, zc_ref, u_s, bm_s, kin_s, cm_s, s_ref, h_ref, e_ref, *, seg_len, n_lat, n_ctx, stages):
    T = SSM_T
    ns = 2 ** len(stages)
    gs = T // ns
    sw = gs * LANES
    rows = u_s.shape[1]
    tps = s_ref.shape[0] // ns
    tpq = tps // 4
    nst = tpq * LANES
    gpt = LANES // SSM_GROUP
    per_q = gpt * SSM_STATE

    def lanes(c):
        return slice(c * LANES, (c + 1) * LANES)

    def col0(s, q):
        return q * per_q + s * nst

    def swapped(tiles, order):
        for d in order:
            tiles = _swap_stage(tiles, d)
        return tiles

    @pl.when(pl.program_id(1) == 0)
    def _expand_operators():
        rexp = rexp_ref[...]
        cexp = cexp_ref[...]
        for s in range(ns):
            wsel = jnp.concatenate([wc_ref[:, col0(s, q):col0(s, q) + nst] for q in range(4)], axis=1)
            blk = jnp.dot(rexp, wsel, preferred_element_type=F32)
            keep = _same_group(blk.shape, 0, SSM_GROUP, 0, SSM_STATE, gs)
            bm_s[s] = jnp.where(keep, blk, 0.0).astype(BF)
            dsel = jnp.concatenate([dc_ref[col0(s, q):col0(s, q) + nst, :] for q in range(4)], axis=0)
            blk = jnp.dot(dsel, cexp, preferred_element_type=F32)
            keep = _same_group(blk.shape, 0, SSM_STATE, 0, SSM_GROUP, gs)
            cm_s[s] = jnp.where(keep, blk, 0.0).astype(BF)
        r16 = lax.broadcasted_iota(jnp.int32, (LANES, SSM_GROUP), 0)
        c16 = lax.broadcasted_iota(jnp.int32, (LANES, SSM_GROUP), 1)
        chan = jnp.where(lax.bitwise_and(r16, SSM_GROUP - 1) == c16, 1.0, 0.0).astype(BF)
        for t in range(T):
            lo = (T - 1 - t) * LANES
            blk = jnp.dot(chan, lag_ref[:, lo:lo + T * LANES], preferred_element_type=F32)
            keep = _same_group(blk.shape, 0, SSM_GROUP, 0, SSM_GROUP, gpt)
            blk = jnp.where(keep, blk, 0.0)
            tiles = swapped([blk[:, lanes(x)] for x in range(T)], stages)
            r0 = (t % gs) * LANES + (t // gs) * gs * SSM_GROUP
            for s in range(ns):
                part = jnp.concatenate(tiles[s * gs:(s + 1) * gs], axis=1)
                kin_s[s, r0:r0 + gs * SSM_GROUP, :] = part[s * gs * SSM_GROUP:(s + 1) * gs * SSM_GROUP, :].astype(BF)

    xs = []
    for t in range(T):
        ct = uc_ref[pl.ds(t, n_ctx, stride=T), :]
        xs.append(jnp.concatenate([ct, ul_ref[pl.ds(t, n_lat, stride=T), :], ct], axis=0))
    xs = swapped(xs, stages)
    for s in range(ns):
        for k in range(gs):
            u_s[s, :, lanes(k)] = xs[s * gs + k].astype(BF)
    for s in range(ns):
        r = jnp.dot(u_s[s], bm_s[s], preferred_element_type=F32)
        for c in range(tps):
            s_ref[s * tps + c] = r[:, lanes(c)]

    chains = []
    for s in range(ns):
        for d in range(2):
            for k in range(tpq):
                cr, ci = s * tps + 2 * d * tpq + k, s * tps + (2 * d + 1) * tpq + k
                lr = slice(col0(s, 2 * d) + k * LANES, col0(s, 2 * d) + (k + 1) * LANES)
                li = slice(col0(s, 2 * d + 1) + k * LANES, col0(s, 2 * d + 1) + (k + 1) * LANES)
                chains.append((cr, ci, d, lr, li))
    nt = ns * tps

    def scan_step(i, carry, record):
        new = list(carry)
        for cr, ci, d, lr_sl, li_sl in chains:
            rsl = pl.ds(i if d == 0 else seg_len - 1 - i, SSM_SEG, stride=seg_len)
            hr, hi = carry[cr], carry[ci]
            if record:
                h_ref[cr, rsl, :] = hr
                h_ref[ci, rsl, :] = hi
            lr, li = lam_ref[:, lr_sl], lam_ref[:, li_sl]
            new[cr] = lr * hr - li * hi + s_ref[cr, rsl, :]
            new[ci] = lr * hi + li * hr + s_ref[ci, rsl, :]
        return tuple(new)

    zero = jnp.zeros((SSM_SEG, LANES), F32)
    fin = lax.fori_loop(0, seg_len, lambda i, c: scan_step(i, c, False), (zero,) * nt, unroll=4)
    for c in range(nt):
        e_ref[c, 0:SSM_SEG, :] = fin[c]

    for cr, ci, d, lr_sl, li_sl in chains:
        lr, li = lamseg_ref[:, lr_sl], lamseg_ref[:, li_sl]
        er = jnp.zeros((1, LANES), F32)
        ei = jnp.zeros((1, LANES), F32)
        for n in range(SSM_SEG):
            g = n if d == 0 else SSM_SEG - 1 - n
            e_ref[cr, SSM_SEG + g:SSM_SEG + g + 1, :] = er
            e_ref[ci, SSM_SEG + g:SSM_SEG + g + 1, :] = ei
            fr, fi = e_ref[cr, g:g + 1, :], e_ref[ci, g:g + 1, :]
            er, ei = lr * er - li * ei + fr, lr * ei + li * er + fi

    ent = tuple(e_ref[c, SSM_SEG:2 * SSM_SEG, :] for c in range(nt))
    lax.fori_loop(0, seg_len, lambda i, c: scan_step(i, c, True), ent, unroll=2)

    ys, ycs = [], []
    for s in range(ns):
        hcat = jnp.concatenate([h_ref[s * tps + c].astype(BF) for c in range(tps)], axis=1)
        y = _gelu_tanh(jnp.dot(u_s[s], kin_s[s], preferred_element_type=F32)
                       + jnp.dot(hcat, cm_s[s], preferred_element_type=F32))
        ys += [y[:, lanes(k)] for k in range(gs)]
        hc = jnp.concatenate(
            [h_ref[s * tps + c, 0:n_ctx, :].astype(BF) for c in range(2 * tpq)]
            + [h_ref[s * tps + c, rows - n_ctx:rows, :].astype(BF) for c in range(2 * tpq, tps)], axis=1)
        yc = _gelu_tanh(jnp.dot(u_s[s, 0:n_ctx, :], kin_s[s], preferred_element_type=F32)
                        + jnp.dot(hc, cm_s[s], preferred_element_type=F32))
        ycs += [yc[:, lanes(k)] for k in range(gs)]
    ys = swapped(ys, stages[::-1])
    ycs = swapped(ycs, stages[::-1])
    for t in range(T):
        z_ref[pl.ds(t, n_lat, stride=T), :] = ys[t][n_ctx:n_ctx + n_lat, :]
        zc_ref[pl.ds(t, n_ctx, stride=T), :] = ycs[t]


def _ssm_weights(a_re, a_im, log_dt, b_re, b_im, c_re, c_im, d_skip, seg_len):
    T = SSM_T
    g, p = a_re.shape[1], a_re.shape[2]
    hc = b_re.shape[-1]
    gpt = LANES // hc
    J = g // gpt
    lam = lax.complex(a_re.astype(F32), a_im.astype(F32))
    dt = jnp.exp(log_dt.astype(F32))[..., None]
    ldt = lam * dt
    lam_bar = jnp.exp(ldt)
    bbar = ((lam_bar - 1) / lam)[..., None] * lax.complex(b_re.astype(F32), b_im.astype(F32))
    cmat = lax.complex(c_re.astype(F32), c_im.astype(F32))
    kk = jnp.arange(T + 1, dtype=F32)
    pw = jnp.exp(ldt[:, None] * kk[None, :, None, None])

    wf = pw[0, :T][::-1][:, :, None, :] * jnp.swapaxes(bbar[0], 1, 2)[None]
    wb = pw[1, :T][:, :, None, :] * jnp.swapaxes(bbar[1], 1, 2)[None]
    wq = jnp.stack([wf.real, wf.imag, wb.real, wb.imag]).reshape(4, T, J, gpt, hc, p)
    wc = wq.transpose(2, 1, 4, 0, 3, 5).reshape(J, T * hc, 4 * gpt * p)

    df = cmat[0][None] * pw[0, 1:][:, :, None, :]
    db = cmat[1][None] * pw[1, 1:][::-1][:, :, None, :]
    dq = jnp.stack([df.real, -df.imag, db.real, -db.imag]).reshape(4, T, J, gpt, hc, p)
    dc = dq.transpose(2, 0, 3, 5, 1, 4).reshape(J, 4 * gpt * p, T * hc)

    mf = jnp.einsum('gcp,kgp,gph->khgc', cmat[0], pw[0, :T], bbar[0]).real
    mb = jnp.einsum('gcp,kgp,gph->khgc', cmat[1], pw[1, :T], bbar[1]).real
    skip = jnp.eye(hc, dtype=F32)[:, None, :] * d_skip.astype(F32).reshape(g, hc)[None]
    m0 = mf[0] + mb[0] + skip
    lags = jnp.concatenate([mb[1:][::-1], m0[None], mf[1:]], axis=0)
    kc = lags.reshape(2 * T - 1, hc, J, LANES).transpose(2, 1, 0, 3).reshape(J, hc, (2 * T - 1) * LANES)

    def lam_rows(power):
        lp = jnp.exp(ldt * power)
        lt = lp.reshape(2, J, gpt * p)
        return jnp.concatenate([lt[0].real, lt[0].imag, lt[1].real, lt[1].imag], axis=-1)[:, None, :]

    lam_t = jnp.broadcast_to(lam_rows(float(T)), (J, SSM_SEG, 4 * gpt * p))
    return wc.astype(BF), kc.astype(BF), dc.astype(BF), lam_t, lam_rows(float(T * seg_len))


def _ssm_expanders(gs):
    r = np.arange(gs * LANES)
    k, slot, ch = r // LANES, (r % LANES) // SSM_GROUP, r % SSM_GROUP
    t = (slot // gs) * gs + k
    rexp = np.zeros((gs * LANES, SSM_T * SSM_GROUP), np.float32)
    rexp[r, t * SSM_GROUP + ch] = 1.0
    return jnp.asarray(rexp, BF), jnp.asarray(rexp.T, BF)


def _ssm(u3, u3c, wc, kc, dc, lam, lamseg, *, layer, batch, seq, n_ctx):
    T = SSM_T
    J = u3.shape[0]
    ns = wc.shape[-1]
    nsets = 2 ** len(SSM_STAGES)
    gs = T // nsets
    sw = gs * LANES
    n_lat, n_c = seq // T, n_ctx // T
    rows = n_lat + 2 * n_c
    seg_len = rows // SSM_SEG
    assert rows % SSM_SEG == 0 and n_c % 16 == 0
    rexp, cexp = _ssm_expanders(gs)
    kern = functools.partial(_ssm_kernel, seg_len=seg_len, n_lat=n_lat, n_ctx=n_c, stages=SSM_STAGES)
    const = lambda j, b: (0, 0)
    per_j = lambda j, b: (layer, j, 0, 0)
    per_jb = lambda j, b: (j, b, 0)
    return pl.pallas_call(
        kern,
        grid=(J, batch),
        in_specs=[pl.BlockSpec((None, seq, LANES), per_jb),
                  pl.BlockSpec((None, n_ctx, LANES), per_jb),
                  pl.BlockSpec((None, None) + wc.shape[2:], per_j),
                  pl.BlockSpec((None, None) + kc.shape[2:], per_j),
                  pl.BlockSpec((None, None) + dc.shape[2:], per_j),
                  pl.BlockSpec(rexp.shape, const),
                  pl.BlockSpec(cexp.shape, const),
                  pl.BlockSpec((None, None, SSM_SEG, ns), per_j),
                  pl.BlockSpec((None, None, 1, ns), per_j)],
        out_specs=[pl.BlockSpec((None, seq, LANES), per_jb),
                   pl.BlockSpec((None, n_ctx, LANES), per_jb)],
        out_shape=[jax.ShapeDtypeStruct(u3.shape, F32),
                   jax.ShapeDtypeStruct(u3c.shape, F32)],
        scratch_shapes=[pltpu.VMEM((nsets, rows, sw), BF),
                        pltpu.VMEM((nsets, sw, ns // nsets), BF),
                        pltpu.VMEM((nsets, sw, sw), BF),
                        pltpu.VMEM((nsets, ns // nsets, sw), BF),
                        pltpu.VMEM((ns // LANES, rows, LANES), F32),
                        pltpu.VMEM((ns // LANES, rows, LANES), F32),
                        pltpu.VMEM((ns // LANES, 2 * SSM_SEG, LANES), F32)],
        compiler_params=_cparams("arbitrary", "arbitrary"),
        name="ssm_chunk_scan",
    )(u3, u3c, wc, kc, dc, rexp, cexp, lam, lamseg)


def _dft_a_kernel(w_ref, x_ref, o_ref):
    o_ref[...] = jnp.dot(w_ref[...], x_ref[...], preferred_element_type=F32).astype(BF)


def _dft_rows(n):
    k = np.arange(n)
    m = (k[:, None] * k[None, :]) % n
    ang = m * (2.0 * math.pi / n)
    return jnp.asarray(np.concatenate([np.cos(ang), -np.sin(ang)], axis=0), BF)


def _dft_a(w, x3):
    B, n, cols = x3.shape
    tn = min(cols, 4096)
    return pl.pallas_call(
        _dft_a_kernel,
        grid=(B, cols // tn),
        in_specs=[pl.BlockSpec((2 * n, n), lambda b, i: (0, 0)),
                  pl.BlockSpec((None, n, tn), lambda b, i: (b, 0, i))],
        out_specs=pl.BlockSpec((None, 2 * n, tn), lambda b, i: (b, 0, i)),
        out_shape=jax.ShapeDtypeStruct((B, 2 * n, cols), BF),
        compiler_params=_cparams("parallel", "parallel"),
        name="dft_stage_a",
    )(w, x3)


def _store_lane_tiles(o_ref, p, n, wf, idx):
    per = wf // LANES
    for c in range(2 * per):
        part = p[(c // per) * n:(c // per + 1) * n, (c % per) * LANES:(c % per + 1) * LANES]
        o_ref[(c,) + idx] = part


def _dft_b_kernel(a_ref, g_ref, o_ref, *, kb, l1, wf):
    for i in range(kb):
        g = jnp.concatenate([g_ref[0, i], g_ref[1, i]], axis=0)
        p = jnp.dot(a_ref[i], g, preferred_element_type=F32)
        _store_lane_tiles(o_ref, p, l1, wf, (i,))


def _dft_b_mats(seq, l1, l2):
    k2 = np.arange(l2)[:, None, None]
    k1 = np.arange(l1)[None, :, None]
    j1 = np.arange(l1)[None, None, :]
    m = ((k1 * l2 + k2) * j1) % seq
    ang = m * (2.0 * math.pi / seq)
    ar, ai = np.cos(ang).astype(np.float32), -np.sin(ang).astype(np.float32)
    top = np.concatenate([ar, -ai], axis=2)
    bot = np.concatenate([ai, ar], axis=2)
    return jnp.asarray(np.concatenate([top, bot], axis=1), BF)


def _dft_b(amat, g5, *, kb):
    B, _, l2, l1, wf = g5.shape
    nlt = 2 * wf // LANES
    kern = functools.partial(_dft_b_kernel, kb=kb, l1=l1, wf=wf)
    return pl.pallas_call(
        kern,
        grid=(B, l2 // kb),
        in_specs=[pl.BlockSpec((kb, 2 * l1, 2 * l1), lambda b, i: (i, 0, 0)),
                  pl.BlockSpec((None, 2, kb, l1, wf), lambda b, i: (b, 0, i, 0, 0))],
        out_specs=pl.BlockSpec((None, nlt, kb, l1, LANES), lambda b, i: (b, 0, i, 0, 0)),
        out_shape=jax.ShapeDtypeStruct((B, nlt, l2, l1, LANES), F32),
        compiler_params=_cparams("parallel", "parallel"),
        name="dft_stage_b",
    )(amat, g5)


def _dft_ctx_kernel(w_ref, x_ref, o_ref, *, n, wf):
    p = jnp.dot(w_ref[...], x_ref[...], preferred_element_type=F32)
    _store_lane_tiles(o_ref, p, n, wf, ())


def _dft_ctx(w, x3):
    B, n, wf = x3.shape
    nlt = 2 * wf // LANES
    kern = functools.partial(_dft_ctx_kernel, n=n, wf=wf)
    return pl.pallas_call(
        kern,
        grid=(B,),
        in_specs=[pl.BlockSpec((2 * n, n), lambda b: (0, 0)),
                  pl.BlockSpec((None, n, wf), lambda b: (b, 0, 0))],
        out_specs=pl.BlockSpec((None, nlt, n, LANES), lambda b: (b, 0, 0, 0)),
        out_shape=jax.ShapeDtypeStruct((B, nlt, n, LANES), F32),
        compiler_params=_cparams("parallel"),
        name="dft_ctx",
    )(w, x3)


def _channel_dft(wf, seq):
    c = np.arange(wf)
    same = (c[:, None] // FNET_GROUP) == (c[None, :] // FNET_GROUP)
    m = ((c[:, None] % FNET_GROUP) * (c[None, :] % FNET_GROUP)) % FNET_GROUP
    ang = m * (2.0 * math.pi / FNET_GROUP)
    mask = same / math.sqrt(seq * FNET_GROUP)
    return jnp.asarray(np.concatenate([np.cos(ang) * mask, np.sin(ang) * mask], axis=0), BF)


def _softmax_pv(parts, l_shape):
    m = parts[0][0].max(axis=-1, keepdims=True)
    for s, _ in parts[1:]:
        m = jnp.maximum(m, s.max(axis=-1, keepdims=True))
    l = jnp.zeros(l_shape, F32)
    o = None
    for s, v in parts:
        p = jnp.exp2(s - m)
        l = l + p.sum(axis=-1, keepdims=True)
        pv = jnp.dot(p.astype(BF), v, preferred_element_type=F32)
        o = pv if o is None else o + pv
    return o / l


def _na_kernel(kb_ref, pid_ref, dr_ref, q_ref, k_ref, v_ref, kc_ref, vc_ref, cbx_ref, o_ref, bias_ref,
               *, nkeys, heads, nsub):
    step = pl.program_id(1)
    nq = q_ref.shape[0] // nsub
    rq, kr = nq // GRID_W, nkeys // GRID_W
    lane = lax.broadcasted_iota(jnp.int32, (nq, LANES), 1)
    low = lane < HEAD_DIM
    dn = (((1,), (1,)), ((), ()))

    @pl.when((pl.program_id(0) == 0) & (step == 0))
    def _build_bias():
        for pat in range(bias_ref.shape[0]):
            for qi in range(rq):
                for w in range(kr):
                    a = dr_ref[(pat * rq + qi) * kr + w]
                    for h in range(heads):
                        bias_ref[pat, h, qi * GRID_W:(qi + 1) * GRID_W, w * GRID_W:(w + 1) * GRID_W] = cbx_ref[h, a]

    tasks = [(sb, h) for sb in range(nsub) for h in range(heads)]
    starts = [pl.multiple_of(kb_ref[step * nsub + sb] * GRID_W, GRID_W) for sb in range(nsub)]
    pats = [pid_ref[step * nsub + sb] for sb in range(nsub)]

    def scores(sb, h):
        sl = slice((h // 2) * LANES, (h // 2 + 1) * LANES)
        q2 = q_ref[sb * nq:(sb + 1) * nq, sl]
        qm = jnp.where(low if h % 2 == 0 else jnp.logical_not(low), q2, jnp.zeros_like(q2))
        s_w = lax.dot_general(qm, k_ref[pl.ds(starts[sb], nkeys), sl], dn, preferred_element_type=F32)
        s_c = lax.dot_general(qm, kc_ref[:, sl], dn, preferred_element_type=F32)
        return s_w + bias_ref[pats[sb], h], s_c

    def weights(s_w, s_c):
        m = jnp.maximum(s_w.max(axis=-1, keepdims=True), s_c.max(axis=-1, keepdims=True))
        p_w = jnp.exp2(s_w - m)
        p_c = jnp.exp2(s_c - m)
        l = p_w.sum(axis=-1, keepdims=True) + p_c.sum(axis=-1, keepdims=True)
        return p_w.astype(BF), p_c.astype(BF), l

    outs = {}

    def finish(sb, h, p_w, p_c, l):
        sl = slice((h // 2) * LANES, (h // 2 + 1) * LANES)
        o = (jnp.dot(p_w, v_ref[pl.ds(starts[sb], nkeys), sl], preferred_element_type=F32)
             + jnp.dot(p_c, vc_ref[:, sl], preferred_element_type=F32))
        outs[h % 2] = o / l
        if h % 2 == 1:
            o_ref[sb * nq:(sb + 1) * nq, sl] = jnp.where(low, outs[0], outs[1]).astype(BF)

    nt = len(tasks)
    sc = {0: scores(*tasks[0])}
    if nt > 1:
        sc[1] = scores(*tasks[1])
    pw = {0: weights(*sc.pop(0))}
    for n in range(nt):
        if n + 2 < nt:
            sc[n + 2] = scores(*tasks[n + 2])
        if n + 1 < nt:
            pw[n + 1] = weights(*sc.pop(n + 1))
        finish(*tasks[n], *pw.pop(n))


def _na_plan(rows, rq):
    wr = min(NA_ROWS, rows)
    kr = rq + wr - 1
    nblk = rows // rq
    kbs, pids, pats = [], [], []
    for blk in range(nblk):
        r0 = blk * rq
        rs = [int(np.clip(r0 + i - wr // 2, 0, rows - wr)) for i in range(rq)]
        kb = min(rs[0], rows - kr)
        pat = (tuple(r - kb for r in rs), r0 - kb)
        if pat not in pats:
            pats.append(pat)
        kbs.append(kb)
        pids.append(pats.index(pat))
    return np.asarray(kbs, np.int32), np.asarray(pids, np.int32), pats, kr, wr


def _na_bias(rpb, pats, rq, kr, wr):
    cols = np.arange(GRID_W)
    cstart = np.clip(cols - NA_COLS // 2, 0, GRID_W - NA_COLS)
    kc = np.arange(GRID_W)
    col_ok = (kc[None, :] >= cstart[:, None]) & (kc[None, :] < cstart[:, None] + NA_COLS)
    dc = kc[None, :] - cols[:, None] + NA_COLS - 1
    depth, heads, nr, ncol = rpb.shape
    sel = (dc[:, :, None] == np.arange(ncol)[None, None, :]) & col_ok[:, :, None]
    cb = jnp.einsum('lhab,ckb->lhack', rpb.astype(F32), jnp.asarray(sel, F32), precision=lax.Precision.HIGHEST)
    cb = jnp.where(jnp.asarray(col_ok)[None, None, None], cb * math.log2(math.e), NEG)
    cbx = jnp.concatenate([cb, jnp.full((depth, heads, 1, GRID_W, GRID_W), NEG, F32)], axis=2)
    slots = []
    for rs_off, r_off in pats:
        i = np.arange(rq)[:, None]
        w = np.arange(kr)[None, :]
        rso = np.asarray(rs_off)[:, None]
        row_ok = (w >= rso) & (w < rso + wr)
        slots.append(np.where(row_ok, w - r_off - i + NA_ROWS - 1, nr))
    return cbx, np.stack(slots).reshape(-1).astype(np.int32)


def _na_tables(rpb_all, seq, rq):
    kbs, pids, pats, kr, wr = _na_plan(seq // GRID_W, rq)
    cbx, slots = _na_bias(rpb_all, pats, rq, kr, wr)
    return dict(kbs=jnp.asarray(kbs), pids=jnp.asarray(pids), slots=jnp.asarray(slots), cbx=cbx, kr=kr, rq=rq,
                npat=len(pats))


def _na_latent(q, k, v, kc, vc, tabs, layer, *, batch, seq, n_ctx):
    rows = seq // GRID_W
    cbx, rq, kr = tabs["cbx"], tabs["rq"], tabs["kr"]
    heads = cbx.shape[1]
    nblk = rows // rq
    nq = rq * GRID_W
    nkeys = kr * GRID_W
    w = q.shape[1]
    nsub = 2 if nblk % 2 == 0 else 1
    nstep = nblk // nsub
    kern = functools.partial(_na_kernel, nkeys=nkeys, heads=heads, nsub=nsub)
    grid_spec = pltpu.PrefetchScalarGridSpec(
        num_scalar_prefetch=3,
        grid=(batch, nstep),
        in_specs=[pl.BlockSpec((nsub * nq, w), lambda b, i, *_: (b * nstep + i, 0)),
                  pl.BlockSpec((seq, w), lambda b, i, *_: (b, 0)),
                  pl.BlockSpec((seq, w), lambda b, i, *_: (b, 0)),
                  pl.BlockSpec((n_ctx, w), lambda b, i, *_: (b, 0)),
                  pl.BlockSpec((n_ctx, w), lambda b, i, *_: (b, 0)),
                  pl.BlockSpec((None,) + cbx.shape[1:], lambda b, i, *_: (layer, 0, 0, 0, 0))],
        out_specs=pl.BlockSpec((nsub * nq, w), lambda b, i, *_: (b * nstep + i, 0)),
        scratch_shapes=[pltpu.VMEM((tabs["npat"], heads, nq, nkeys), F32)],
    )
    return pl.pallas_call(
        kern,
        grid_spec=grid_spec,
        out_shape=jax.ShapeDtypeStruct(q.shape, BF),
        compiler_params=_cparams("arbitrary", "arbitrary"),
        name="na_latent",
    )(tabs["kbs"], tabs["pids"], tabs["slots"], q, k, v, kc, vc, cbx)


def _na_ctx_kernel(q_ref, k_ref, v_ref, o_ref, *, heads):
    nq = q_ref.shape[0]
    lane = lax.broadcasted_iota(jnp.int32, (nq, LANES), 1)
    low = lane < HEAD_DIM
    dn = (((1,), (1,)), ((), ()))
    for hp in range(heads // 2):
        sl = slice(hp * LANES, (hp + 1) * LANES)
        q2 = q_ref[:, sl]
        k2 = k_ref[:, sl]
        v2 = v_ref[:, sl]
        outs = []
        for hh in range(2):
            qm = jnp.where(low if hh == 0 else jnp.logical_not(low), q2, jnp.zeros_like(q2))
            s = lax.dot_general(qm, k2, dn, preferred_element_type=F32)
            outs.append(_softmax_pv([(s, v2)], (nq, 1)))
        o_ref[:, sl] = jnp.where(low, outs[0], outs[1]).astype(BF)


def _na_ctx(qc, kc, vc, *, batch, n_ctx, heads):
    w = qc.shape[1]
    spec = pl.BlockSpec((n_ctx, w), lambda b: (b, 0))
    return pl.pallas_call(
        functools.partial(_na_ctx_kernel, heads=heads),
        grid=(batch,),
        in_specs=[spec, spec, spec],
        out_specs=spec,
        out_shape=jax.ShapeDtypeStruct(qc.shape, BF),
        compiler_params=_cparams("parallel"),
        name="na_ctx",
    )(qc, kc, vc)


def _outproj_kernel(z_ref, pf_ref, na_ref, x_ref, gate_ref, gpost_ref,
                    wglu_ref, cs_ref, wfo_ref, wout_ref, o_ref, *, widths, sub):
    w_ssm, w_fn, w_na = widths
    wglu, wfo, wout = (r[...].astype(BF) for r in (wglu_ref, wfo_ref, wout_ref))
    for r0 in range(0, x_ref.shape[0], sub):
        rs = slice(r0, r0 + sub)
        z = jnp.concatenate([z_ref[j, rs, :] for j in range(w_ssm // LANES)], axis=1)
        gl = jnp.dot(z.astype(BF), wglu, preferred_element_type=F32)
        y_ssm = (z * jax.nn.sigmoid(gl)).astype(BF)
        if len(pf_ref.shape) == 3:
            tiles = [pf_ref[c, rs, :] for c in range(pf_ref.shape[0])]
        else:
            l2 = pf_ref.shape[1]
            tiles = [jnp.concatenate([pf_ref[c, :, k1, :] for k1 in range(r0 // l2, (r0 + sub) // l2)], axis=0)
                     for c in range(pf_ref.shape[0])]
        pf = jnp.concatenate(tiles, axis=1).astype(BF)
        mixed = jnp.dot(pf, cs_ref[...], preferred_element_type=F32).astype(BF)
        y_fft = jnp.dot(mixed, wfo, preferred_element_type=F32).astype(BF)
        y = jnp.concatenate([y_ssm, y_fft, na_ref[rs, :]], axis=1)
        o = jnp.dot(y, wout, preferred_element_type=F32)
        ms = jnp.mean(o * o, axis=-1, keepdims=True)
        o_ref[rs, :] = x_ref[rs, :] + gate_ref[...] * (o * lax.rsqrt(ms + EPS) * gpost_ref[...])


def _outproj(z3, pf, yna, x2d, mod3, g_post, w_glu, cs, w_fo, w_out, *,
             layer, tm, tiles_per_batch, mod_row, widths):
    n, d = x2d.shape
    w_ssm, w_fn, w_na = widths
    if mod_row is None:
        row = lambda i: i // tiles_per_batch
    else:
        row = lambda i: mod_row
    kern = functools.partial(_outproj_kernel, widths=widths, sub=min(tm, 512))
    return pl.pallas_call(
        kern,
        grid=(n // tm,),
        in_specs=[pl.BlockSpec((w_ssm // LANES, tm, LANES), lambda i: (0, i, 0)),
                  (pl.BlockSpec((None, pf.shape[1], tm, LANES),
                                lambda i: (i // tiles_per_batch, 0, i % tiles_per_batch, 0))
                   if pf.ndim == 4 else
                   pl.BlockSpec((None, pf.shape[1], pf.shape[2], tm // pf.shape[2], LANES),
                                lambda i: (i // tiles_per_batch, 0, 0, i % tiles_per_batch, 0))),
                  pl.BlockSpec((tm, w_na), lambda i: (i, 0)),
                  pl.BlockSpec((tm, d), lambda i: (i, 0)),
                  pl.BlockSpec((None, 1, d), lambda i: (row(i), 0, 2)),
                  pl.BlockSpec((None, 1, d), lambda i: (layer, 0, 0)),
                  _layer_resident(w_glu, layer),
                  _resident(cs.shape, lambda i: (0, 0)),
                  _layer_resident(w_fo, layer),
                  _layer_resident(w_out, layer)],
        out_specs=pl.BlockSpec((tm, d), lambda i: (i, 0)),
        out_shape=jax.ShapeDtypeStruct((n, d), F32),
        compiler_params=_cparams("parallel"),
        name="outproj",
    )(z3, pf, yna, x2d, mod3, g_post.reshape(-1, 1, d), w_glu, cs, w_fo, w_out)


def _ffn_kernel(x_ref, sh_ref, sc_ref, gate_ref, gpre_ref, gpost_ref, wg_ref, wu_ref, wd_ref,
                o_ref, a_ref, *, sub, chunk):
    dff = wg_ref.shape[1]
    for r0 in range(0, x_ref.shape[0], sub):
        rs = slice(r0, r0 + sub)
        x = x_ref[rs, :]
        ms = jnp.mean(x * x, axis=-1, keepdims=True)
        m = ((x * lax.rsqrt(ms + EPS) * gpre_ref[...]) * (1.0 + sc_ref[...]) + sh_ref[...]).astype(BF)
        for c0 in range(0, dff, chunk):
            sl = slice(c0, min(c0 + chunk, dff))
            g = jnp.dot(m, wg_ref[:, sl], preferred_element_type=F32)
            u = jnp.dot(m, wu_ref[:, sl], preferred_element_type=F32)
            a_ref[rs, sl] = (g * jax.nn.sigmoid(g) * u).astype(BF)
        y = jnp.dot(a_ref[rs, :], wd_ref[...], preferred_element_type=F32)
        ms2 = jnp.mean(y * y, axis=-1, keepdims=True)
        o_ref[rs, :] = x + gate_ref[...] * (y * lax.rsqrt(ms2 + EPS) * gpost_ref[...])


FFN_WEIGHT_STEPS = 8


def _ffn_staged_kernel(*refs, sub, chunk, n_lat_tiles, with_ctx):
    if with_ctx:
        (x_ref, xc_ref, sh_ref, sc_ref, gate_ref, gpre_ref, gpost_ref, wg_ref, wu_ref, wd_ref,
         o_ref, oc_ref, a_ref, wg_s, wu_s, wd_s) = refs
    else:
        (x_ref, sh_ref, sc_ref, gate_ref, gpre_ref, gpost_ref, wg_ref, wu_ref, wd_ref,
         o_ref, a_ref, wg_s, wu_s, wd_s) = refs
    i = pl.program_id(0)
    common = (sh_ref, sc_ref, gate_ref, gpre_ref, gpost_ref, wg_s, wu_s, wd_s)

    @pl.when(i < FFN_WEIGHT_STEPS)
    def _cast_weights():
        for src, dst in ((wg_ref, wg_s), (wu_ref, wu_s), (wd_ref, wd_s)):
            rows = src.shape[0]
            dst[pl.ds(pl.multiple_of(i * rows, rows), rows), :] = src[...].astype(BF)

    @pl.when((i >= FFN_WEIGHT_STEPS) & (i < FFN_WEIGHT_STEPS + n_lat_tiles))
    def _latent():
        _ffn_kernel(x_ref, *common, o_ref, a_ref, sub=sub, chunk=chunk)

    if with_ctx:
        @pl.when(i >= FFN_WEIGHT_STEPS + n_lat_tiles)
        def _context():
            _ffn_kernel(xc_ref, *common, oc_ref, a_ref, sub=min(sub, xc_ref.shape[0]), chunk=chunk)


def _ffn(x2d, xc2d, mod3, g_pre, g_post, wg, wu, wd, *, layer, tm, tiles_per_batch, ctx_row):
    n, d = x2d.shape
    dff = wg.shape[2]
    nl = n // tm
    nw = FFN_WEIGHT_STEPS
    with_ctx = xc2d is not None
    gains = (g_pre.reshape(-1, 1, d), g_post.reshape(-1, 1, d))
    slab = lambda i: (layer, jnp.minimum(i, nw - 1), 0)
    weights = [pl.BlockSpec((None, d // nw, dff), slab), pl.BlockSpec((None, d // nw, dff), slab),
               pl.BlockSpec((None, dff // nw, d), slab)]
    gain_specs = [pl.BlockSpec((None, 1, d), lambda i: (layer, 0, 0))] * 2
    tile = lambda i: jnp.clip(i - nw, 0, nl - 1)
    row = lambda i: jnp.where(i < nw + nl, tile(i) // tiles_per_batch, ctx_row)
    lat = lambda i: (tile(i), 0)
    mods = [pl.BlockSpec((None, 1, d), functools.partial(lambda i, c: (row(i), 0, c), c=c)) for c in (3, 4, 5)]
    in_specs = [pl.BlockSpec((tm, d), lat)]
    out_specs = [pl.BlockSpec((tm, d), lat)]
    out_shape = [jax.ShapeDtypeStruct((n, d), F32)]
    args = [x2d]
    if with_ctx:
        nc = xc2d.shape[0]
        ctx = lambda i: (jnp.maximum(i - nw - nl, 0), 0)
        in_specs.append(pl.BlockSpec((nc, d), ctx, pipeline_mode=pl.Buffered(1)))
        out_specs.append(pl.BlockSpec((nc, d), ctx))
        out_shape.append(jax.ShapeDtypeStruct((nc, d), F32))
        args.append(xc2d)
    outs = pl.pallas_call(
        functools.partial(_ffn_staged_kernel, sub=min(tm, 512), chunk=3 * MXU_TILE, n_lat_tiles=nl,
                          with_ctx=with_ctx),
        grid=(nw + nl + int(with_ctx),),
        in_specs=in_specs + mods + gain_specs + weights,
        out_specs=out_specs,
        out_shape=out_shape,
        scratch_shapes=[pltpu.VMEM((tm, dff), BF), pltpu.VMEM((d, dff), BF), pltpu.VMEM((d, dff), BF),
                        pltpu.VMEM((dff, d), BF)],
        compiler_params=_cparams("arbitrary"),
        name="ffn_staged",
    )(*args, mod3, mod3, mod3, *gains, wg, wu, wd)
    return (outs[0], outs[1]) if with_ctx else (outs[0], None)


def _layer(x2d, xc2d, p, w, layer, *, batch, seq, n_ctx, last, rope_tabs, consts):
    d = x2d.shape[1]
    w_ssm = consts["w_ssm"]
    w_fn = w["w_fourier"].shape[1]
    w_na = (w["w_in"].shape[2] - w_ssm - w_fn) // 3
    widths = (w_ssm, w_fn, w_na)
    heads = w_na // HEAD_DIM
    tm = min(512, seq)
    tmc = min(512, batch * n_ctx)

    mod = w["mod"][layer]
    mod3 = mod.reshape(mod.shape[0], 1, 6 * d)

    cos, sin = rope_tabs
    tm2 = min(2 * tm, seq)
    (u3, f, q, k, v), (u3c, fc, qc, kc, vc) = _inproj(
        x2d, xc2d, mod3, p["g_pre_mix"], w["w_in"], cos, sin, layer=layer, tm=tm2,
        tiles_per_batch=seq // tm2, ctx_row=batch, widths=widths)

    z3, zc3 = _ssm(u3, u3c, *w["ssm"], layer=layer, batch=batch, seq=seq, n_ctx=n_ctx)

    l2 = consts["l2"]
    l1 = seq // l2
    g = _dft_a(consts["dft_a"], f.reshape(batch, l2, l1 * w_fn))
    pf = _dft_b(consts["dft_b"], g.reshape(batch, 2, l2, l1, w_fn), kb=min(8, l2))

    yna = _na_latent(q, k, v, kc, vc, consts["na"], layer, batch=batch, seq=seq, n_ctx=n_ctx)

    w_glu, w_fo, w_out = w["w_glu"], w["w_fourier"], w["w_out"]
    x2d = _outproj(z3, pf, yna, x2d, mod3, p["g_post_mix"], w_glu, consts["cs_lat"], w_fo, w_out,
                   layer=layer, tm=tm2, tiles_per_batch=seq // tm2, mod_row=None, widths=widths)
    if not last:
        pfc = _dft_ctx(consts["dft_ctx"], fc.reshape(batch, n_ctx, w_fn))
        ynac = _na_ctx(qc, kc, vc, batch=batch, n_ctx=n_ctx, heads=heads)
        tc = min(tmc, n_ctx)
        xc2d = _outproj(zc3, pfc, ynac, xc2d, mod3, p["g_post_mix"], w_glu, consts["cs_ctx"], w_fo, w_out,
                        layer=layer, tm=tc, tiles_per_batch=n_ctx // tc, mod_row=batch, widths=widths)
    else:
        xc2d = None
    wg, wu, wd = w["w_ffn_gate"], w["w_ffn_up"], w["w_ffn_down"]
    return _ffn(x2d, xc2d, mod3, p["g_pre_ffn"], p["g_post_ffn"], wg, wu, wd,
                layer=layer, tm=tm2, tiles_per_batch=seq // tm2, ctx_row=batch)


def kernel(x, c, ctx, c_ctx, w_mod, b_mod, g_pre_mix, g_post_mix, w_in, ssm_a_re, ssm_a_im, ssm_log_dt,
           ssm_b_re, ssm_b_im, ssm_c_re, ssm_c_im, ssm_d, w_glu, w_fourier, na_rpb, w_out, g_pre_ffn,
           g_post_ffn, w_ffn_gate, w_ffn_up, w_ffn_down):
    batch, seq, d = x.shape
    n_ctx = ctx.shape[1]
    depth = w_mod.shape[0]
    w_fn = w_fourier.shape[1]
    params = dict(g_pre_mix=g_pre_mix, g_post_mix=g_post_mix, g_pre_ffn=g_pre_ffn, g_post_ffn=g_post_ffn)
    seg_len = (seq + 2 * n_ctx) // SSM_T // SSM_SEG
    ssm_ops = jax.vmap(functools.partial(_ssm_weights, seg_len=seg_len))(
        ssm_a_re, ssm_a_im, ssm_log_dt, ssm_b_re, ssm_b_im, ssm_c_re, ssm_c_im, ssm_d)
    nrow = -(-(batch + 1) // 8) * 8
    cstack = jnp.concatenate([c, c_ctx[None, :], jnp.zeros((nrow - batch - 1, d), c.dtype)], axis=0)
    weights = dict(mod=_mod_rows(cstack, w_mod, b_mod), ssm=ssm_ops,
                   w_in=w_in, w_glu=w_glu, w_fourier=w_fourier, w_out=w_out,
                   w_ffn_gate=w_ffn_gate,
                   w_ffn_up=w_ffn_up, w_ffn_down=w_ffn_down)

    l2 = 64 if seq % (64 * 8) == 0 else 8
    l1 = seq // l2
    consts = dict(
        l2=l2,
        w_ssm=ssm_d.shape[1],
        na=_na_tables(na_rpb, seq, 4),
        dft_a=_dft_rows(l2).astype(BF),
        dft_b=_dft_b_mats(seq, l1, l2).astype(BF),
        dft_ctx=_dft_rows(n_ctx).astype(BF),
        cs_lat=_channel_dft(w_fn, seq).astype(BF),
        cs_ctx=_channel_dft(w_fn, n_ctx).astype(BF),
    )
    rope_tabs = _rope_tables(seq)

    x2d = x.reshape(batch * seq, d)
    xc2d = ctx.reshape(batch * n_ctx, d)
    for layer in range(depth):
        x2d, xc2d = _layer(x2d, xc2d, params, weights, layer, batch=batch, seq=seq, n_ctx=n_ctx,
                           last=(layer == depth - 1), rope_tabs=rope_tabs, consts=consts)
    return x2d.reshape(batch, seq, d)
```

```python
import functools
import math

import numpy as np
import jax
import jax.numpy as jnp
from jax import lax
from jax.experimental import pallas as pl
from jax.experimental.pallas import tpu as pltpu

BF = jnp.bfloat16
F32 = jnp.float32

EPS = 1e-6
GRID_W = 64
HEAD_DIM = 64
NA_ROWS = 8
NA_COLS = 16
ROPE_BASE = 10000.0
SSM_GROUP = 16
SSM_STATE = 64
FNET_GROUP = 64
LANES = 128
MXU_TILE = 256
SSM_T = 8
SSM_SEG = 8
SSM_STAGES = (4, 2)
NEG = -1e30
VMEM_LIMIT = 56 * 1024 * 1024


def _cparams(*sem):
    return pltpu.CompilerParams(dimension_semantics=sem, vmem_limit_bytes=VMEM_LIMIT)


def _resident(shape, index_map):
    return pl.BlockSpec(shape, index_map, pipeline_mode=pl.Buffered(1))


def _layer_resident(w_all, layer):
    return pl.BlockSpec((None,) + w_all.shape[1:], lambda *_: (layer, 0, 0), pipeline_mode=pl.Buffered(1))


def _mod_kernel(c_ref, w_ref, b_ref, o_ref):
    cs = c_ref[...]
    s = (cs * jax.nn.sigmoid(cs)).astype(BF)
    o_ref[...] = jnp.dot(s, w_ref[...].astype(BF), preferred_element_type=F32) + b_ref[...]


def _mod_rows(cstack, w_mod, b_mod):
    rows, d = cstack.shape
    depth, _, n = w_mod.shape
    bn = 2 * d if n % (2 * d) == 0 else d
    return pl.pallas_call(
        _mod_kernel,
        grid=(depth, n // bn),
        in_specs=[pl.BlockSpec((rows, d), lambda l, i: (0, 0)),
                  pl.BlockSpec((None, d, bn), lambda l, i: (l, 0, i)),
                  pl.BlockSpec((None, 1, bn), lambda l, i: (l, 0, i))],
        out_specs=pl.BlockSpec((None, rows, bn), lambda l, i: (l, 0, i)),
        out_shape=jax.ShapeDtypeStruct((depth, rows, n), F32),
        compiler_params=_cparams("parallel", "parallel"),
        name="mod_rows",
    )(cstack, w_mod, b_mod.reshape(depth, 1, n))


def _rope_tile(t, cos, sin, first):
    partner = jnp.where(first, pltpu.roll(t, LANES - 16, 1), pltpu.roll(t, 16, 1))
    return t * cos + partner * sin


def _inproj_kernel(x_ref, sh_ref, sc_ref, g_ref, w_ref, cos_ref, sin_ref,
                   u_ref, f_ref, q_ref, k_ref, v_ref, *, widths, rope, sub):
    w_ssm, w_fn, w_na = widths
    scale = HEAD_DIM ** -0.5 * math.log2(math.e)
    w = w_ref[...].astype(BF)
    for r0 in range(0, x_ref.shape[0], sub):
        rs = slice(r0, r0 + sub)
        x = x_ref[rs, :]
        ms = jnp.mean(x * x, axis=-1, keepdims=True)
        xn = x * lax.rsqrt(ms + EPS) * g_ref[...]
        m = (xn * (1.0 + sc_ref[...]) + sh_ref[...]).astype(BF)

        h = jnp.dot(m, w, preferred_element_type=F32)

        def proj(lo, n, h=h):
            return h[:, lo:lo + n]

        u = proj(0, w_ssm)
        for j in range(w_ssm // LANES):
            u_ref[j, rs, :] = u[:, j * LANES:(j + 1) * LANES]
        f_ref[rs, :] = proj(w_ssm, w_fn).astype(BF)
        q = proj(w_ssm + w_fn, w_na)
        k = proj(w_ssm + w_fn + w_na, w_na)
        v_ref[rs, :] = proj(w_ssm + w_fn + 2 * w_na, w_na).astype(BF)
        if rope:
            cos = cos_ref[rs, :]
            sin = sin_ref[rs, :]
            lane = lax.broadcasted_iota(jnp.int32, cos.shape, 1)
            first = (lane % 32) < 16
            for j in range(w_na // LANES):
                sl = slice(j * LANES, (j + 1) * LANES)
                q_ref[rs, sl] = (_rope_tile(q[:, sl], cos, sin, first) * scale).astype(BF)
                k_ref[rs, sl] = _rope_tile(k[:, sl], cos, sin, first).astype(BF)
        else:
            q_ref[rs, :] = (q * scale).astype(BF)
            k_ref[rs, :] = k.astype(BF)


def _inproj_pair_kernel(x_ref, xc_ref, sh_ref, sc_ref, g_ref, w_ref, cos_ref, sin_ref, *outs,
                        widths, sub, n_lat_tiles):
    common = (sh_ref, sc_ref, g_ref, w_ref, cos_ref, sin_ref)
    i = pl.program_id(0)

    @pl.when(i < n_lat_tiles)
    def _latent():
        _inproj_kernel(x_ref, *common, *outs[:5], widths=widths, rope=True, sub=sub)

    @pl.when(i >= n_lat_tiles)
    def _context():
        _inproj_kernel(xc_ref, *common, *outs[5:], widths=widths, rope=False, sub=min(sub, xc_ref.shape[0]))


def _inproj(x2d, xc2d, mod3, g_pre, w_in, cos, sin, *, layer, tm, tiles_per_batch, ctx_row, widths):
    n, d = x2d.shape
    nc = xc2d.shape[0]
    w_ssm, w_fn, w_na = widths
    nl = n // tm
    tile = lambda i: jnp.minimum(i, nl - 1)
    row = lambda i: jnp.where(i < nl, tile(i) // tiles_per_batch, ctx_row)
    pos = lambda i: (tile(i) % tiles_per_batch, 0)
    lat = lambda i: (tile(i), 0)
    one = lambda i: (0, 0)

    def outs(rows, idx, idx3):
        specs = [pl.BlockSpec((w_ssm // LANES, rows, LANES), idx3), pl.BlockSpec((rows, w_fn), idx)]
        specs += [pl.BlockSpec((rows, w_na), idx)] * 3
        return specs

    def shapes(rows):
        return ([jax.ShapeDtypeStruct((w_ssm // LANES, rows, LANES), F32), jax.ShapeDtypeStruct((rows, w_fn), BF)]
                + [jax.ShapeDtypeStruct((rows, w_na), BF)] * 3)

    kern = functools.partial(_inproj_pair_kernel, widths=widths, sub=min(tm, 512), n_lat_tiles=nl)
    res = pl.pallas_call(
        kern,
        grid=(nl + 1,),
        in_specs=[pl.BlockSpec((tm, d), lat),
                  pl.BlockSpec((nc, d), one),
                  pl.BlockSpec((None, 1, d), lambda i: (row(i), 0, 0)),
                  pl.BlockSpec((None, 1, d), lambda i: (row(i), 0, 1)),
                  pl.BlockSpec((None, 1, d), lambda i: (layer, 0, 0)),
                  _layer_resident(w_in, layer),
                  pl.BlockSpec((tm, LANES), pos),
                  pl.BlockSpec((tm, LANES), pos)],
        out_specs=outs(tm, lat, lambda i: (0, tile(i), 0)) + outs(nc, one, lambda i: (0, 0, 0)),
        out_shape=shapes(n) + shapes(nc),
        compiler_params=_cparams("arbitrary"),
        name="inproj",
    )(x2d, xc2d, mod3, mod3, g_pre.reshape(-1, 1, d), w_in, cos, sin)
    return res[:5], res[5:]


def _rope_tables(seq):
    t = np.arange(seq)
    row = (t // GRID_W).astype(np.float32)
    col = (t % GRID_W).astype(np.float32)
    quarter = HEAD_DIM // 4
    freqs = (np.float32(ROPE_BASE) ** (-np.arange(quarter, dtype=np.float32) / quarter)).astype(np.float32)
    d = np.arange(LANES) % HEAD_DIM
    use_col = (d // (HEAD_DIM // 2)) == 1
    fidx = d % quarter
    sign = np.where((d % (HEAD_DIM // 2)) // quarter == 0, -1.0, 1.0)
    pos = np.where(use_col[None, :], col[:, None], row[:, None])
    ang = (pos * freqs[fidx][None, :]).astype(np.float32).astype(np.float64)
    return jnp.asarray(np.cos(ang), F32), jnp.asarray(np.sin(ang) * sign[None, :], F32)


def _gelu_tanh(x):
    c = math.sqrt(2.0 / math.pi)
    return x * (0.5 * (1.0 + jnp.tanh(c * (x + 0.044715 * (x * x * x)))))


def _group_of(idx, width, groups):
    shift = width.bit_length() - 1
    assert width == 1 << shift and groups & (groups - 1) == 0
    return lax.bitwise_and(lax.shift_right_logical(idx, shift), groups - 1)


def _same_group(shape, row0, row_width, col0, col_width, groups):
    r = lax.broadcasted_iota(jnp.int32, shape, 0) + row0
    c = lax.broadcasted_iota(jnp.int32, shape, 1) + col0
    return _group_of(r, row_width, groups) == _group_of(c, col_width, groups)


def _swap_stage(tiles, d):
    lane = lax.broadcasted_iota(jnp.int32, tiles[0].shape, 1)
    upper = lax.bitwise_and(lax.shift_right_logical(lane, SSM_GROUP.bit_length() - 1), d) != 0
    out = list(tiles)
    for x in range(len(tiles)):
        if x & d:
            continue
        a, b = tiles[x], tiles[x + d]
        out[x] = jnp.where(upper, pltpu.roll(b, SSM_GROUP * d, 1), a)
        out[x + d] = jnp.where(upper, b, pltpu.roll(a, LANES - SSM_GROUP * d, 1))
    return out


def _ssm_kernel(ul_ref, uc_ref, wc_ref, lag_ref, dc_ref, rexp_ref, cexp_ref, lam_ref, lamseg_ref,
                z_ref, zc_ref, u_s, bm_s, kin_s, cm_s, s_ref, h_ref, e_ref, *, seg_len, n_lat, n_ctx, stages):
    T = SSM_T
    ns = 2 ** len(stages)
    gs = T // ns
    sw = gs * LANES
    rows = u_s.shape[1]
    tps = s_ref.shape[0] // ns
    tpq = tps // 4
    nst = tpq * LANES
    gpt = LANES // SSM_GROUP
    per_q = gpt * SSM_STATE

    def lanes(c):
        return slice(c * LANES, (c + 1) * LANES)

    def col0(s, q):
        return q * per_q + s * nst

    def swapped(tiles, order):
        for d in order:
            tiles = _swap_stage(tiles, d)
        return tiles

    @pl.when(pl.program_id(1) == 0)
    def _expand_operators():
        rexp = rexp_ref[...]
        cexp = cexp_ref[...]
        for s in range(ns):
            wsel = jnp.concatenate([wc_ref[:, col0(s, q):col0(s, q) + nst] for q in range(4)], axis=1)
            blk = jnp.dot(rexp, wsel, preferred_element_type=F32)
            keep = _same_group(blk.shape, 0, SSM_GROUP, 0, SSM_STATE, gs)
            bm_s[s] = jnp.where(keep, blk, 0.0).astype(BF)
            dsel = jnp.concatenate([dc_ref[col0(s, q):col0(s, q) + nst, :] for q in range(4)], axis=0)
            blk = jnp.dot(dsel, cexp, preferred_element_type=F32)
            keep = _same_group(blk.shape, 0, SSM_STATE, 0, SSM_GROUP, gs)
            cm_s[s] = jnp.where(keep, blk, 0.0).astype(BF)
        r16 = lax.broadcasted_iota(jnp.int32, (LANES, SSM_GROUP), 0)
        c16 = lax.broadcasted_iota(jnp.int32, (LANES, SSM_GROUP), 1)
        chan = jnp.where(lax.bitwise_and(r16, SSM_GROUP - 1) == c16, 1.0, 0.0).astype(BF)
        for t in range(T):
            lo = (T - 1 - t) * LANES
            blk = jnp.dot(chan, lag_ref[:, lo:lo + T * LANES], preferred_element_type=F32)
            keep = _same_group(blk.shape, 0, SSM_GROUP, 0, SSM_GROUP, gpt)
            blk = jnp.where(keep, blk, 0.0)
            tiles = swapped([blk[:, lanes(x)] for x in range(T)], stages)
            r0 = (t % gs) * LANES + (t // gs) * gs * SSM_GROUP
            for s in range(ns):
                part = jnp.concatenate(tiles[s * gs:(s + 1) * gs], axis=1)
                kin_s[s, r0:r0 + gs * SSM_GROUP, :] = part[s * gs * SSM_GROUP:(s + 1) * gs * SSM_GROUP, :].astype(BF)

    xs = []
    for t in range(T):
        ct = uc_ref[pl.ds(t, n_ctx, stride=T), :]
        xs.append(jnp.concatenate([ct, ul_ref[pl.ds(t, n_lat, stride=T), :], ct], axis=0))
    xs = swapped(xs, stages)
    for s in range(ns):
        for k in range(gs):
            u_s[s, :, lanes(k)] = xs[s * gs + k].astype(BF)
    for s in range(ns):
        r = jnp.dot(u_s[s], bm_s[s], preferred_element_type=F32)
        for c in range(tps):
            s_ref[s * tps + c] = r[:, lanes(c)]

    chains = []
    for s in range(ns):
        for d in range(2):
            for k in range(tpq):
                cr, ci = s * tps + 2 * d * tpq + k, s * tps + (2 * d + 1) * tpq + k
                lr = slice(col0(s, 2 * d) + k * LANES, col0(s, 2 * d) + (k + 1) * LANES)
                li = slice(col0(s, 2 * d + 1) + k * LANES, col0(s, 2 * d + 1) + (k + 1) * LANES)
                chains.append((cr, ci, d, lr, li))
    nt = ns * tps

    def scan_step(i, carry, record):
        new = list(carry)
        for cr, ci, d, lr_sl, li_sl in chains:
            rsl = pl.ds(i if d == 0 else seg_len - 1 - i, SSM_SEG, stride=seg_len)
            hr, hi = carry[cr], carry[ci]
            if record:
                h_ref[cr, rsl, :] = hr
                h_ref[ci, rsl, :] = hi
            lr, li = lam_ref[:, lr_sl], lam_ref[:, li_sl]
            new[cr] = lr * hr - li * hi + s_ref[cr, rsl, :]
            new[ci] = lr * hi + li * hr + s_ref[ci, rsl, :]
        return tuple(new)

    zero = jnp.zeros((SSM_SEG, LANES), F32)
    fin = lax.fori_loop(0, seg_len, lambda i, c: scan_step(i, c, False), (zero,) * nt, unroll=4)
    for c in range(nt):
        e_ref[c, 0:SSM_SEG, :] = fin[c]

    for cr, ci, d, lr_sl, li_sl in chains:
        lr, li = lamseg_ref[:, lr_sl], lamseg_ref[:, li_sl]
        er = jnp.zeros((1, LANES), F32)
        ei = jnp.zeros((1, LANES), F32)
        for n in range(SSM_SEG):
            g = n if d == 0 else SSM_SEG - 1 - n
            e_ref[cr, SSM_SEG + g:SSM_SEG + g + 1, :] = er
            e_ref[ci, SSM_SEG + g:SSM_SEG + g + 1, :] = ei
            fr, fi = e_ref[cr, g:g + 1, :], e_ref[ci, g:g + 1, :]
            er, ei = lr * er - li * ei + fr, lr * ei + li * er + fi

    ent = tuple(e_ref[c, SSM_SEG:2 * SSM_SEG, :] for c in range(nt))
    lax.fori_loop(0, seg_len, lambda i, c: scan_step(i, c, True), ent, unroll=2)

    ys, ycs = [], []
    for s in range(ns):
        hcat = jnp.concatenate([h_ref[s * tps + c].astype(BF) for c in range(tps)], axis=1)
        y = _gelu_tanh(jnp.dot(u_s[s], kin_s[s], preferred_element_type=F32)
                       + jnp.dot(hcat, cm_s[s], preferred_element_type=F32))
        ys += [y[:, lanes(k)] for k in range(gs)]
        hc = jnp.concatenate(
            [h_ref[s * tps + c, 0:n_ctx, :].astype(BF) for c in range(2 * tpq)]
            + [h_ref[s * tps + c, rows - n_ctx:rows, :].astype(BF) for c in range(2 * tpq, tps)], axis=1)
        yc = _gelu_tanh(jnp.dot(u_s[s, 0:n_ctx, :], kin_s[s], preferred_element_type=F32)
                        + jnp.dot(hc, cm_s[s], preferred_element_type=F32))
        ycs += [yc[:, lanes(k)] for k in range(gs)]
    ys = swapped(ys, stages[::-1])
    ycs = swapped(ycs, stages[::-1])
    for t in range(T):
        z_ref[pl.ds(t, n_lat, stride=T), :] = ys[t][n_ctx:n_ctx + n_lat, :]
        zc_ref[pl.ds(t, n_ctx, stride=T), :] = ycs[t]


def _ssm_weights(a_re, a_im, log_dt, b_re, b_im, c_re, c_im, d_skip, seg_len):
    T = SSM_T
    g, p = a_re.shape[1], a_re.shape[2]
    hc = b_re.shape[-1]
    gpt = LANES // hc
    J = g // gpt
    lam = lax.complex(a_re.astype(F32), a_im.astype(F32))
    dt = jnp.exp(log_dt.astype(F32))[..., None]
    ldt = lam * dt
    lam_bar = jnp.exp(ldt)
    bbar = ((lam_bar - 1) / lam)[..., None] * lax.complex(b_re.astype(F32), b_im.astype(F32))
    cmat = lax.complex(c_re.astype(F32), c_im.astype(F32))
    kk = jnp.arange(T + 1, dtype=F32)
    pw = jnp.exp(ldt[:, None] * kk[None, :, None, None])

    wf = pw[0, :T][::-1][:, :, None, :] * jnp.swapaxes(bbar[0], 1, 2)[None]
    wb = pw[1, :T][:, :, None, :] * jnp.swapaxes(bbar[1], 1, 2)[None]
    wq = jnp.stack([wf.real, wf.imag, wb.real, wb.imag]).reshape(4, T, J, gpt, hc, p)
    wc = wq.transpose(2, 1, 4, 0, 3, 5).reshape(J, T * hc, 4 * gpt * p)

    df = cmat[0][None] * pw[0, 1:][:, :, None, :]
    db = cmat[1][None] * pw[1, 1:][::-1][:, :, None, :]
    dq = jnp.stack([df.real, -df.imag, db.real, -db.imag]).reshape(4, T, J, gpt, hc, p)
    dc = dq.transpose(2, 0, 3, 5, 1, 4).reshape(J, 4 * gpt * p, T * hc)

    mf = jnp.einsum('gcp,kgp,gph->khgc', cmat[0], pw[0, :T], bbar[0]).real
    mb = jnp.einsum('gcp,kgp,gph->khgc', cmat[1], pw[1, :T], bbar[1]).real
    skip = jnp.eye(hc, dtype=F32)[:, None, :] * d_skip.astype(F32).reshape(g, hc)[None]
    m0 = mf[0] + mb[0] + skip
    lags = jnp.concatenate([mb[1:][::-1], m0[None], mf[1:]], axis=0)
    kc = lags.reshape(2 * T - 1, hc, J, LANES).transpose(2, 1, 0, 3).reshape(J, hc, (2 * T - 1) * LANES)

    def lam_rows(power):
        lp = jnp.exp(ldt * power)
        lt = lp.reshape(2, J, gpt * p)
        return jnp.concatenate([lt[0].real, lt[0].imag, lt[1].real, lt[1].imag], axis=-1)[:, None, :]

    lam_t = jnp.broadcast_to(lam_rows(float(T)), (J, SSM_SEG, 4 * gpt * p))
    return wc.astype(BF), kc.astype(BF), dc.astype(BF), lam_t, lam_rows(float(T * seg_len))


def _ssm_expanders(gs):
    r = np.arange(gs * LANES)
    k, slot, ch = r // LANES, (r % LANES) // SSM_GROUP, r % SSM_GROUP
    t = (slot // gs) * gs + k
    rexp = np.zeros((gs * LANES, SSM_T * SSM_GROUP), np.float32)
    rexp[r, t * SSM_GROUP + ch] = 1.0
    return jnp.asarray(rexp, BF), jnp.asarray(rexp.T, BF)


def _ssm(u3, u3c, wc, kc, dc, lam, lamseg, *, layer, batch, seq, n_ctx):
    T = SSM_T
    J = u3.shape[0]
    ns = wc.shape[-1]
    nsets = 2 ** len(SSM_STAGES)
    gs = T // nsets
    sw = gs * LANES
    n_lat, n_c = seq // T, n_ctx // T
    rows = n_lat + 2 * n_c
    seg_len = rows // SSM_SEG
    assert rows % SSM_SEG == 0 and n_c % 16 == 0
    rexp, cexp = _ssm_expanders(gs)
    kern = functools.partial(_ssm_kernel, seg_len=seg_len, n_lat=n_lat, n_ctx=n_c, stages=SSM_STAGES)
    const = lambda j, b: (0, 0)
    per_j = lambda j, b: (layer, j, 0, 0)
    per_jb = lambda j, b: (j, b, 0)
    return pl.pallas_call(
        kern,
        grid=(J, batch),
        in_specs=[pl.BlockSpec((None, seq, LANES), per_jb),
                  pl.BlockSpec((None, n_ctx, LANES), per_jb),
                  pl.BlockSpec((None, None) + wc.shape[2:], per_j),
                  pl.BlockSpec((None, None) + kc.shape[2:], per_j),
                  pl.BlockSpec((None, None) + dc.shape[2:], per_j),
                  pl.BlockSpec(rexp.shape, const),
                  pl.BlockSpec(cexp.shape, const),
                  pl.BlockSpec((None, None, SSM_SEG, ns), per_j),
                  pl.BlockSpec((None, None, 1, ns), per_j)],
        out_specs=[pl.BlockSpec((None, seq, LANES), per_jb),
                   pl.BlockSpec((None, n_ctx, LANES), per_jb)],
        out_shape=[jax.ShapeDtypeStruct(u3.shape, F32),
                   jax.ShapeDtypeStruct(u3c.shape, F32)],
        scratch_shapes=[pltpu.VMEM((nsets, rows, sw), BF),
                        pltpu.VMEM((nsets, sw, ns // nsets), BF),
                        pltpu.VMEM((nsets, sw, sw), BF),
                        pltpu.VMEM((nsets, ns // nsets, sw), BF),
                        pltpu.VMEM((ns // LANES, rows, LANES), F32),
                        pltpu.VMEM((ns // LANES, rows, LANES), F32),
                        pltpu.VMEM((ns // LANES, 2 * SSM_SEG, LANES), F32)],
        compiler_params=_cparams("arbitrary", "arbitrary"),
        name="ssm_chunk_scan",
    )(u3, u3c, wc, kc, dc, rexp, cexp, lam, lamseg)


def _dft_a_kernel(w_ref, x_ref, o_ref):
    o_ref[...] = jnp.dot(w_ref[...], x_ref[...], preferred_element_type=F32).astype(BF)


def _dft_rows(n):
    k = np.arange(n)
    m = (k[:, None] * k[None, :]) % n
    ang = m * (2.0 * math.pi / n)
    return jnp.asarray(np.concatenate([np.cos(ang), -np.sin(ang)], axis=0), BF)


def _dft_a(w, x3):
    B, n, cols = x3.shape
    tn = min(cols, 4096)
    return pl.pallas_call(
        _dft_a_kernel,
        grid=(B, cols // tn),
        in_specs=[pl.BlockSpec((2 * n, n), lambda b, i: (0, 0)),
                  pl.BlockSpec((None, n, tn), lambda b, i: (b, 0, i))],
        out_specs=pl.BlockSpec((None, 2 * n, tn), lambda b, i: (b, 0, i)),
        out_shape=jax.ShapeDtypeStruct((B, 2 * n, cols), BF),
        compiler_params=_cparams("parallel", "parallel"),
        name="dft_stage_a",
    )(w, x3)


def _store_lane_tiles(o_ref, p, n, wf, idx):
    per = wf // LANES
    for c in range(2 * per):
        part = p[(c // per) * n:(c // per + 1) * n, (c % per) * LANES:(c % per + 1) * LANES]
        o_ref[(c,) + idx] = part


def _dft_b_kernel(a_ref, g_ref, o_ref, *, kb, l1, wf):
    for i in range(kb):
        g = jnp.concatenate([g_ref[0, i], g_ref[1, i]], axis=0)
        p = jnp.dot(a_ref[i], g, preferred_element_type=F32)
        _store_lane_tiles(o_ref, p, l1, wf, (i,))


def _dft_b_mats(seq, l1, l2):
    k2 = np.arange(l2)[:, None, None]
    k1 = np.arange(l1)[None, :, None]
    j1 = np.arange(l1)[None, None, :]
    m = ((k1 * l2 + k2) * j1) % seq
    ang = m * (2.0 * math.pi / seq)
    ar, ai = np.cos(ang).astype(np.float32), -np.sin(ang).astype(np.float32)
    top = np.concatenate([ar, -ai], axis=2)
    bot = np.concatenate([ai, ar], axis=2)
    return jnp.asarray(np.concatenate([top, bot], axis=1), BF)


def _dft_b(amat, g5, *, kb):
    B, _, l2, l1, wf = g5.shape
    nlt = 2 * wf // LANES
    kern = functools.partial(_dft_b_kernel, kb=kb, l1=l1, wf=wf)
    return pl.pallas_call(
        kern,
        grid=(B, l2 // kb),
        in_specs=[pl.BlockSpec((kb, 2 * l1, 2 * l1), lambda b, i: (i, 0, 0)),
                  pl.BlockSpec((None, 2, kb, l1, wf), lambda b, i: (b, 0, i, 0, 0))],
        out_specs=pl.BlockSpec((None, nlt, kb, l1, LANES), lambda b, i: (b, 0, i, 0, 0)),
        out_shape=jax.ShapeDtypeStruct((B, nlt, l2, l1, LANES), F32),
        compiler_params=_cparams("parallel", "parallel"),
        name="dft_stage_b",
    )(amat, g5)


def _dft_ctx_kernel(w_ref, x_ref, o_ref, *, n, wf):
    p = jnp.dot(w_ref[...], x_ref[...], preferred_element_type=F32)
    _store_lane_tiles(o_ref, p, n, wf, ())


def _dft_ctx(w, x3):
    B, n, wf = x3.shape
    nlt = 2 * wf // LANES
    kern = functools.partial(_dft_ctx_kernel, n=n, wf=wf)
    return pl.pallas_call(
        kern,
        grid=(B,),
        in_specs=[pl.BlockSpec((2 * n, n), lambda b: (0, 0)),
                  pl.BlockSpec((None, n, wf), lambda b: (b, 0, 0))],
        out_specs=pl.BlockSpec((None, nlt, n, LANES), lambda b: (b, 0, 0, 0)),
        out_shape=jax.ShapeDtypeStruct((B, nlt, n, LANES), F32),
        compiler_params=_cparams("parallel"),
        name="dft_ctx",
    )(w, x3)


def _channel_dft(wf, seq):
    c = np.arange(wf)
    same = (c[:, None] // FNET_GROUP) == (c[None, :] // FNET_GROUP)
    m = ((c[:, None] % FNET_GROUP) * (c[None, :] % FNET_GROUP)) % FNET_GROUP
    ang = m * (2.0 * math.pi / FNET_GROUP)
    mask = same / math.sqrt(seq * FNET_GROUP)
    return jnp.asarray(np.concatenate([np.cos(ang) * mask, np.sin(ang) * mask], axis=0), BF)


def _softmax_pv(parts, l_shape):
    m = parts[0][0].max(axis=-1, keepdims=True)
    for s, _ in parts[1:]:
        m = jnp.maximum(m, s.max(axis=-1, keepdims=True))
    l = jnp.zeros(l_shape, F32)
    o = None
    for s, v in parts:
        p = jnp.exp2(s - m)
        l = l + p.sum(axis=-1, keepdims=True)
        pv = jnp.dot(p.astype(BF), v, preferred_element_type=F32)
        o = pv if o is None else o + pv
    return o / l


def _na_kernel(kb_ref, pid_ref, dr_ref, q_ref, k_ref, v_ref, kc_ref, vc_ref, cbx_ref, o_ref, bias_ref,
               *, nkeys, heads, nsub):
    step = pl.program_id(1)
    nq = q_ref.shape[0] // nsub
    rq, kr = nq // GRID_W, nkeys // GRID_W
    lane = lax.broadcasted_iota(jnp.int32, (nq, LANES), 1)
    low = lane < HEAD_DIM
    dn = (((1,), (1,)), ((), ()))

    @pl.when((pl.program_id(0) == 0) & (step == 0))
    def _build_bias():
        for pat in range(bias_ref.shape[0]):
            for qi in range(rq):
                for w in range(kr):
                    a = dr_ref[(pat * rq + qi) * kr + w]
                    for h in range(heads):
                        bias_ref[pat, h, qi * GRID_W:(qi + 1) * GRID_W, w * GRID_W:(w + 1) * GRID_W] = cbx_ref[h, a]

    tasks = [(sb, h) for sb in range(nsub) for h in range(heads)]
    starts = [pl.multiple_of(kb_ref[step * nsub + sb] * GRID_W, GRID_W) for sb in range(nsub)]
    pats = [pid_ref[step * nsub + sb] for sb in range(nsub)]

    n_ctx = kc_ref.shape[0]
    ones = jnp.ones((n_ctx + nkeys, LANES), BF)
    kv = {}

    def keys_values(sb, hp):
        if (sb, hp) not in kv:
            sl = slice(hp * LANES, (hp + 1) * LANES)
            win = pl.ds(starts[sb], nkeys)
            k_all = jnp.concatenate([kc_ref[:, sl], k_ref[win, sl]], axis=0)
            v_all = jnp.concatenate([jnp.concatenate([vc_ref[:, sl], v_ref[win, sl]], axis=0), ones], axis=1)
            kv[sb, hp] = (k_all, v_all)
        return kv[sb, hp]

    def scores(sb, h):
        sl = slice((h // 2) * LANES, (h // 2 + 1) * LANES)
        q2 = q_ref[sb * nq:(sb + 1) * nq, sl]
        qm = jnp.where(low if h % 2 == 0 else jnp.logical_not(low), q2, jnp.zeros_like(q2))
        s = lax.dot_general(qm, keys_values(sb, h // 2)[0], dn, preferred_element_type=F32)
        return s[:, n_ctx:] + bias_ref[pats[sb], h], s[:, :n_ctx]

    def weights(s_w, s_c):
        m = jnp.maximum(s_w.max(axis=-1, keepdims=True), s_c.max(axis=-1, keepdims=True))
        return (jnp.concatenate([jnp.exp2(s_c - m).astype(BF), jnp.exp2(s_w - m).astype(BF)], axis=1),)

    outs = {}

    def finish(sb, h, p):
        sl = slice((h // 2) * LANES, (h // 2 + 1) * LANES)
        ox = jnp.dot(p, keys_values(sb, h // 2)[1], preferred_element_type=F32)
        outs[h % 2] = ox[:, :LANES] / ox[:, LANES:]
        if h % 2 == 1:
            o_ref[sb * nq:(sb + 1) * nq, sl] = jnp.where(low, outs[0], outs[1]).astype(BF)

    nt = len(tasks)
    sc = {0: scores(*tasks[0])}
    if nt > 1:
        sc[1] = scores(*tasks[1])
    pw = {0: weights(*sc.pop(0))}
    for n in range(nt):
        if n + 2 < nt:
            sc[n + 2] = scores(*tasks[n + 2])
        if n + 1 < nt:
            pw[n + 1] = weights(*sc.pop(n + 1))
        finish(*tasks[n], *pw.pop(n))


def _na_plan(rows, rq):
    wr = min(NA_ROWS, rows)
    kr = rq + wr - 1
    nblk = rows // rq
    kbs, pids, pats = [], [], []
    for blk in range(nblk):
        r0 = blk * rq
        rs = [int(np.clip(r0 + i - wr // 2, 0, rows - wr)) for i in range(rq)]
        kb = min(rs[0], rows - kr)
        pat = (tuple(r - kb for r in rs), r0 - kb)
        if pat not in pats:
            pats.append(pat)
        kbs.append(kb)
        pids.append(pats.index(pat))
    return np.asarray(kbs, np.int32), np.asarray(pids, np.int32), pats, kr, wr


def _na_bias(rpb, pats, rq, kr, wr):
    cols = np.arange(GRID_W)
    cstart = np.clip(cols - NA_COLS // 2, 0, GRID_W - NA_COLS)
    kc = np.arange(GRID_W)
    col_ok = (kc[None, :] >= cstart[:, None]) & (kc[None, :] < cstart[:, None] + NA_COLS)
    dc = kc[None, :] - cols[:, None] + NA_COLS - 1
    depth, heads, nr, ncol = rpb.shape
    sel = (dc[:, :, None] == np.arange(ncol)[None, None, :]) & col_ok[:, :, None]
    cb = jnp.einsum('lhab,ckb->lhack', rpb.astype(F32), jnp.asarray(sel, F32), precision=lax.Precision.HIGHEST)
    cb = jnp.where(jnp.asarray(col_ok)[None, None, None], cb * math.log2(math.e), NEG)
    cbx = jnp.concatenate([cb, jnp.full((depth, heads, 1, GRID_W, GRID_W), NEG, F32)], axis=2)
    slots = []
    for rs_off, r_off in pats:
        i = np.arange(rq)[:, None]
        w = np.arange(kr)[None, :]
        rso = np.asarray(rs_off)[:, None]
        row_ok = (w >= rso) & (w < rso + wr)
        slots.append(np.where(row_ok, w - r_off - i + NA_ROWS - 1, nr))
    return cbx, np.stack(slots).reshape(-1).astype(np.int32)


def _na_tables(rpb_all, seq, rq):
    kbs, pids, pats, kr, wr = _na_plan(seq // GRID_W, rq)
    cbx, slots = _na_bias(rpb_all, pats, rq, kr, wr)
    return dict(kbs=jnp.asarray(kbs), pids=jnp.asarray(pids), slots=jnp.asarray(slots), cbx=cbx, kr=kr, rq=rq,
                npat=len(pats))


def _na_latent(q, k, v, kc, vc, tabs, layer, *, batch, seq, n_ctx):
    rows = seq // GRID_W
    cbx, rq, kr = tabs["cbx"], tabs["rq"], tabs["kr"]
    heads = cbx.shape[1]
    nblk = rows // rq
    nq = rq * GRID_W
    nkeys = kr * GRID_W
    w = q.shape[1]
    nsub = 2 if nblk % 2 == 0 else 1
    nstep = nblk // nsub
    kern = functools.partial(_na_kernel, nkeys=nkeys, heads=heads, nsub=nsub)
    grid_spec = pltpu.PrefetchScalarGridSpec(
        num_scalar_prefetch=3,
        grid=(batch, nstep),
        in_specs=[pl.BlockSpec((nsub * nq, w), lambda b, i, *_: (b * nstep + i, 0)),
                  pl.BlockSpec((seq, w), lambda b, i, *_: (b, 0)),
                  pl.BlockSpec((seq, w), lambda b, i, *_: (b, 0)),
                  pl.BlockSpec((n_ctx, w), lambda b, i, *_: (b, 0)),
                  pl.BlockSpec((n_ctx, w), lambda b, i, *_: (b, 0)),
                  pl.BlockSpec((None,) + cbx.shape[1:], lambda b, i, *_: (layer, 0, 0, 0, 0))],
        out_specs=pl.BlockSpec((nsub * nq, w), lambda b, i, *_: (b * nstep + i, 0)),
        scratch_shapes=[pltpu.VMEM((tabs["npat"], heads, nq, nkeys), F32)],
    )
    return pl.pallas_call(
        kern,
        grid_spec=grid_spec,
        out_shape=jax.ShapeDtypeStruct(q.shape, BF),
        compiler_params=_cparams("arbitrary", "arbitrary"),
        name="na_latent",
    )(tabs["kbs"], tabs["pids"], tabs["slots"], q, k, v, kc, vc, cbx)


def _na_ctx_kernel(q_ref, k_ref, v_ref, o_ref, *, heads):
    nq = q_ref.shape[0]
    lane = lax.broadcasted_iota(jnp.int32, (nq, LANES), 1)
    low = lane < HEAD_DIM
    dn = (((1,), (1,)), ((), ()))
    for hp in range(heads // 2):
        sl = slice(hp * LANES, (hp + 1) * LANES)
        q2 = q_ref[:, sl]
        k2 = k_ref[:, sl]
        v2 = v_ref[:, sl]
        outs = []
        for hh in range(2):
            qm = jnp.where(low if hh == 0 else jnp.logical_not(low), q2, jnp.zeros_like(q2))
            s = lax.dot_general(qm, k2, dn, preferred_element_type=F32)
            outs.append(_softmax_pv([(s, v2)], (nq, 1)))
        o_ref[:, sl] = jnp.where(low, outs[0], outs[1]).astype(BF)


def _na_ctx(qc, kc, vc, *, batch, n_ctx, heads):
    w = qc.shape[1]
    spec = pl.BlockSpec((n_ctx, w), lambda b: (b, 0))
    return pl.pallas_call(
        functools.partial(_na_ctx_kernel, heads=heads),
        grid=(batch,),
        in_specs=[spec, spec, spec],
        out_specs=spec,
        out_shape=jax.ShapeDtypeStruct(qc.shape, BF),
        compiler_params=_cparams("parallel"),
        name="na_ctx",
    )(qc, kc, vc)


def _outproj_kernel(z_ref, pf_ref, na_ref, x_ref, gate_ref, gpost_ref,
                    wglu_ref, cs_ref, wfo_ref, wout_ref, o_ref, *, widths, sub):
    w_ssm, w_fn, w_na = widths
    wglu, wfo, wout = (r[...].astype(BF) for r in (wglu_ref, wfo_ref, wout_ref))
    for r0 in range(0, x_ref.shape[0], sub):
        rs = slice(r0, r0 + sub)
        z = jnp.concatenate([z_ref[j, rs, :] for j in range(w_ssm // LANES)], axis=1)
        gl = jnp.dot(z.astype(BF), wglu, preferred_element_type=F32)
        y_ssm = (z * jax.nn.sigmoid(gl)).astype(BF)
        if len(pf_ref.shape) == 3:
            tiles = [pf_ref[c, rs, :] for c in range(pf_ref.shape[0])]
        else:
            l2 = pf_ref.shape[1]
            tiles = [jnp.concatenate([pf_ref[c, :, k1, :] for k1 in range(r0 // l2, (r0 + sub) // l2)], axis=0)
                     for c in range(pf_ref.shape[0])]
        pf = jnp.concatenate(tiles, axis=1).astype(BF)
        mixed = jnp.dot(pf, cs_ref[...], preferred_element_type=F32).astype(BF)
        y_fft = jnp.dot(mixed, wfo, preferred_element_type=F32).astype(BF)
        y = jnp.concatenate([y_ssm, y_fft, na_ref[rs, :]], axis=1)
        o = jnp.dot(y, wout, preferred_element_type=F32)
        ms = jnp.mean(o * o, axis=-1, keepdims=True)
        o_ref[rs, :] = x_ref[rs, :] + gate_ref[...] * (o * lax.rsqrt(ms + EPS) * gpost_ref[...])


def _outproj(z3, pf, yna, x2d, mod3, g_post, w_glu, cs, w_fo, w_out, *,
             layer, tm, tiles_per_batch, mod_row, widths):
    n, d = x2d.shape
    w_ssm, w_fn, w_na = widths
    if mod_row is None:
        row = lambda i: i // tiles_per_batch
    else:
        row = lambda i: mod_row
    kern = functools.partial(_outproj_kernel, widths=widths, sub=min(tm, 512))
    return pl.pallas_call(
        kern,
        grid=(n // tm,),
        in_specs=[pl.BlockSpec((w_ssm // LANES, tm, LANES), lambda i: (0, i, 0)),
                  (pl.BlockSpec((None, pf.shape[1], tm, LANES),
                                lambda i: (i // tiles_per_batch, 0, i % tiles_per_batch, 0))
                   if pf.ndim == 4 else
                   pl.BlockSpec((None, pf.shape[1], pf.shape[2], tm // pf.shape[2], LANES),
                                lambda i: (i // tiles_per_batch, 0, 0, i % tiles_per_batch, 0))),
                  pl.BlockSpec((tm, w_na), lambda i: (i, 0)),
                  pl.BlockSpec((tm, d), lambda i: (i, 0)),
                  pl.BlockSpec((None, 1, d), lambda i: (row(i), 0, 2)),
                  pl.BlockSpec((None, 1, d), lambda i: (layer, 0, 0)),
                  _layer_resident(w_glu, layer),
                  _resident(cs.shape, lambda i: (0, 0)),
                  _layer_resident(w_fo, layer),
                  _layer_resident(w_out, layer)],
        out_specs=pl.BlockSpec((tm, d), lambda i: (i, 0)),
        out_shape=jax.ShapeDtypeStruct((n, d), F32),
        compiler_params=_cparams("parallel"),
        name="outproj",
    )(z3, pf, yna, x2d, mod3, g_post.reshape(-1, 1, d), w_glu, cs, w_fo, w_out)


def _ffn_kernel(x_ref, sh_ref, sc_ref, gate_ref, gpre_ref, gpost_ref, wg_ref, wu_ref, wd_ref,
                o_ref, a_ref, *, sub, chunk):
    dff = wg_ref.shape[1]
    for r0 in range(0, x_ref.shape[0], sub):
        rs = slice(r0, r0 + sub)
        x = x_ref[rs, :]
        ms = jnp.mean(x * x, axis=-1, keepdims=True)
        m = ((x * lax.rsqrt(ms + EPS) * gpre_ref[...]) * (1.0 + sc_ref[...]) + sh_ref[...]).astype(BF)
        for c0 in range(0, dff, chunk):
            sl = slice(c0, min(c0 + chunk, dff))
            g = jnp.dot(m, wg_ref[:, sl], preferred_element_type=F32)
            u = jnp.dot(m, wu_ref[:, sl], preferred_element_type=F32)
            a_ref[rs, sl] = (g * jax.nn.sigmoid(g) * u).astype(BF)
        y = jnp.dot(a_ref[rs, :], wd_ref[...], preferred_element_type=F32)
        ms2 = jnp.mean(y * y, axis=-1, keepdims=True)
        o_ref[rs, :] = x + gate_ref[...] * (y * lax.rsqrt(ms2 + EPS) * gpost_ref[...])


FFN_WEIGHT_STEPS = 8


def _ffn_staged_kernel(*refs, sub, chunk, n_lat_tiles, with_ctx):
    if with_ctx:
        (x_ref, xc_ref, sh_ref, sc_ref, gate_ref, gpre_ref, gpost_ref, wg_ref, wu_ref, wd_ref,
         o_ref, oc_ref, a_ref, wg_s, wu_s, wd_s) = refs
    else:
        (x_ref, sh_ref, sc_ref, gate_ref, gpre_ref, gpost_ref, wg_ref, wu_ref, wd_ref,
         o_ref, a_ref, wg_s, wu_s, wd_s) = refs
    i = pl.program_id(0)
    common = (sh_ref, sc_ref, gate_ref, gpre_ref, gpost_ref, wg_s, wu_s, wd_s)

    @pl.when(i < FFN_WEIGHT_STEPS)
    def _cast_weights():
        for src, dst in ((wg_ref, wg_s), (wu_ref, wu_s), (wd_ref, wd_s)):
            rows = src.shape[0]
            dst[pl.ds(pl.multiple_of(i * rows, rows), rows), :] = src[...].astype(BF)

    @pl.when((i >= FFN_WEIGHT_STEPS) & (i < FFN_WEIGHT_STEPS + n_lat_tiles))
    def _latent():
        _ffn_kernel(x_ref, *common, o_ref, a_ref, sub=sub, chunk=chunk)

    if with_ctx:
        @pl.when(i >= FFN_WEIGHT_STEPS + n_lat_tiles)
        def _context():
            _ffn_kernel(xc_ref, *common, oc_ref, a_ref, sub=min(sub, xc_ref.shape[0]), chunk=chunk)


def _ffn(x2d, xc2d, mod3, g_pre, g_post, wg, wu, wd, *, layer, tm, tiles_per_batch, ctx_row):
    n, d = x2d.shape
    dff = wg.shape[2]
    nl = n // tm
    nw = FFN_WEIGHT_STEPS
    with_ctx = xc2d is not None
    gains = (g_pre.reshape(-1, 1, d), g_post.reshape(-1, 1, d))
    slab = lambda i: (layer, jnp.minimum(i, nw - 1), 0)
    weights = [pl.BlockSpec((None, d // nw, dff), slab), pl.BlockSpec((None, d // nw, dff), slab),
               pl.BlockSpec((None, dff // nw, d), slab)]
    gain_specs = [pl.BlockSpec((None, 1, d), lambda i: (layer, 0, 0))] * 2
    tile = lambda i: jnp.clip(i - nw, 0, nl - 1)
    row = lambda i: jnp.where(i < nw + nl, tile(i) // tiles_per_batch, ctx_row)
    lat = lambda i: (tile(i), 0)
    mods = [pl.BlockSpec((None, 1, d), functools.partial(lambda i, c: (row(i), 0, c), c=c)) for c in (3, 4, 5)]
    in_specs = [pl.BlockSpec((tm, d), lat)]
    out_specs = [pl.BlockSpec((tm, d), lat)]
    out_shape = [jax.ShapeDtypeStruct((n, d), F32)]
    args = [x2d]
    if with_ctx:
        nc = xc2d.shape[0]
        ctx = lambda i: (jnp.maximum(i - nw - nl, 0), 0)
        in_specs.append(pl.BlockSpec((nc, d), ctx, pipeline_mode=pl.Buffered(1)))
        out_specs.append(pl.BlockSpec((nc, d), ctx))
        out_shape.append(jax.ShapeDtypeStruct((nc, d), F32))
        args.append(xc2d)
    outs = pl.pallas_call(
        functools.partial(_ffn_staged_kernel, sub=min(tm, 512), chunk=3 * MXU_TILE, n_lat_tiles=nl,
                          with_ctx=with_ctx),
        grid=(nw + nl + int(with_ctx),),
        in_specs=in_specs + mods + gain_specs + weights,
        out_specs=out_specs,
        out_shape=out_shape,
        scratch_shapes=[pltpu.VMEM((tm, dff), BF), pltpu.VMEM((d, dff), BF), pltpu.VMEM((d, dff), BF),
                        pltpu.VMEM((dff, d), BF)],
        compiler_params=_cparams("arbitrary"),
        name="ffn_staged",
    )(*args, mod3, mod3, mod3, *gains, wg, wu, wd)
    return (outs[0], outs[1]) if with_ctx else (outs[0], None)


def _layer(x2d, xc2d, p, w, layer, *, batch, seq, n_ctx, last, rope_tabs, consts):
    d = x2d.shape[1]
    w_ssm = consts["w_ssm"]
    w_fn = w["w_fourier"].shape[1]
    w_na = (w["w_in"].shape[2] - w_ssm - w_fn) // 3
    widths = (w_ssm, w_fn, w_na)
    heads = w_na // HEAD_DIM
    tm = min(512, seq)
    tmc = min(512, batch * n_ctx)

    mod = w["mod"][layer]
    mod3 = mod.reshape(mod.shape[0], 1, 6 * d)

    cos, sin = rope_tabs
    tm2 = min(2 * tm, seq)
    (u3, f, q, k, v), (u3c, fc, qc, kc, vc) = _inproj(
        x2d, xc2d, mod3, p["g_pre_mix"], w["w_in"], cos, sin, layer=layer, tm=tm2,
        tiles_per_batch=seq // tm2, ctx_row=batch, widths=widths)

    z3, zc3 = _ssm(u3, u3c, *w["ssm"], layer=layer, batch=batch, seq=seq, n_ctx=n_ctx)

    l2 = consts["l2"]
    l1 = seq // l2
    g = _dft_a(consts["dft_a"], f.reshape(batch, l2, l1 * w_fn))
    pf = _dft_b(consts["dft_b"], g.reshape(batch, 2, l2, l1, w_fn), kb=min(8, l2))

    yna = _na_latent(q, k, v, kc, vc, consts["na"], layer, batch=batch, seq=seq, n_ctx=n_ctx)

    w_glu, w_fo, w_out = w["w_glu"], w["w_fourier"], w["w_out"]
    x2d = _outproj(z3, pf, yna, x2d, mod3, p["g_post_mix"], w_glu, consts["cs_lat"], w_fo, w_out,
                   layer=layer, tm=tm2, tiles_per_batch=seq // tm2, mod_row=None, widths=widths)
    if not last:
        pfc = _dft_ctx(consts["dft_ctx"], fc.reshape(batch, n_ctx, w_fn))
        ynac = _na_ctx(qc, kc, vc, batch=batch, n_ctx=n_ctx, heads=heads)
        tc = min(tmc, n_ctx)
        xc2d = _outproj(zc3, pfc, ynac, xc2d, mod3, p["g_post_mix"], w_glu, consts["cs_ctx"], w_fo, w_out,
                        layer=layer, tm=tc, tiles_per_batch=n_ctx // tc, mod_row=batch, widths=widths)
    else:
        xc2d = None
    wg, wu, wd = w["w_ffn_gate"], w["w_ffn_up"], w["w_ffn_down"]
    return _ffn(x2d, xc2d, mod3, p["g_pre_ffn"], p["g_post_ffn"], wg, wu, wd,
                layer=layer, tm=tm2, tiles_per_batch=seq // tm2, ctx_row=batch)


def kernel(x, c, ctx, c_ctx, w_mod, b_mod, g_pre_mix, g_post_mix, w_in, ssm_a_re, ssm_a_im, ssm_log_dt,
           ssm_b_re, ssm_b_im, ssm_c_re, ssm_c_im, ssm_d, w_glu, w_fourier, na_rpb, w_out, g_pre_ffn,
           g_post_ffn, w_ffn_gate, w_ffn_up, w_ffn_down):
    batch, seq, d = x.shape
    n_ctx = ctx.shape[1]
    depth = w_mod.shape[0]
    w_fn = w_fourier.shape[1]
    params = dict(g_pre_mix=g_pre_mix, g_post_mix=g_post_mix, g_pre_ffn=g_pre_ffn, g_post_ffn=g_post_ffn)
    seg_len = (seq + 2 * n_ctx) // SSM_T // SSM_SEG
    ssm_ops = jax.vmap(functools.partial(_ssm_weights, seg_len=seg_len))(
        ssm_a_re, ssm_a_im, ssm_log_dt, ssm_b_re, ssm_b_im, ssm_c_re, ssm_c_im, ssm_d)
    nrow = -(-(batch + 1) // 8) * 8
    cstack = jnp.concatenate([c, c_ctx[None, :], jnp.zeros((nrow - batch - 1, d), c.dtype)], axis=0)
    weights = dict(mod=_mod_rows(cstack, w_mod, b_mod), ssm=ssm_ops,
                   w_in=w_in, w_glu=w_glu, w_fourier=w_fourier, w_out=w_out,
                   w_ffn_gate=w_ffn_gate,
                   w_ffn_up=w_ffn_up, w_ffn_down=w_ffn_down)

    l2 = 64 if seq % (64 * 8) == 0 else 8
    l1 = seq // l2
    consts = dict(
        l2=l2,
        w_ssm=ssm_d.shape[1],
        na=_na_tables(na_rpb, seq, 4),
        dft_a=_dft_rows(l2).astype(BF),
        dft_b=_dft_b_mats(seq, l1, l2).astype(BF),
        dft_ctx=_dft_rows(n_ctx).astype(BF),
        cs_lat=_channel_dft(w_fn, seq).astype(BF),
        cs_ctx=_channel_dft(w_fn, n_ctx).astype(BF),
    )
    rope_tabs = _rope_tables(seq)

    x2d = x.reshape(batch * seq, d)
    xc2d = ctx.reshape(batch * n_ctx, d)
    for layer in range(depth):
        x2d, xc2d = _layer(x2d, xc2d, params, weights, layer, batch=batch, seq=seq, n_ctx=n_ctx,
                           last=(layer == depth - 1), rope_tabs=rope_tabs, consts=consts)
    return x2d.reshape(batch, seq, d)
```

```python
import functools
import math

import numpy as np
import jax
import jax.numpy as jnp
from jax import lax
from jax.experimental import pallas as pl
from jax.experimental.pallas import tpu as pltpu

BF = jnp.bfloat16
F32 = jnp.float32

EPS = 1e-6
GRID_W = 64
HEAD_DIM = 64
NA_ROWS = 8
NA_COLS = 16
ROPE_BASE = 10000.0
SSM_GROUP = 16
SSM_STATE = 64
FNET_GROUP = 64
LANES = 128
MXU_TILE = 256
SSM_T = 8
SSM_SEG = 8
SSM_STAGES = (4, 2)
NEG = -1e30
VMEM_LIMIT = 56 * 1024 * 1024


def _cparams(*sem):
    return pltpu.CompilerParams(dimension_semantics=sem, vmem_limit_bytes=VMEM_LIMIT)


def _resident(shape, index_map):
    return pl.BlockSpec(shape, index_map, pipeline_mode=pl.Buffered(1))


def _layer_resident(w_all, layer):
    return pl.BlockSpec((None,) + w_all.shape[1:], lambda *_: (layer, 0, 0), pipeline_mode=pl.Buffered(1))


def _mod_kernel(c_ref, w_ref, b_ref, o_ref):
    cs = c_ref[...]
    s = (cs * jax.nn.sigmoid(cs)).astype(BF)
    o_ref[...] = jnp.dot(s, w_ref[...].astype(BF), preferred_element_type=F32) + b_ref[...]


def _mod_rows(cstack, w_mod, b_mod):
    rows, d = cstack.shape
    depth, _, n = w_mod.shape
    bn = 2 * d if n % (2 * d) == 0 else d
    return pl.pallas_call(
        _mod_kernel,
        grid=(depth, n // bn),
        in_specs=[pl.BlockSpec((rows, d), lambda l, i: (0, 0)),
                  pl.BlockSpec((None, d, bn), lambda l, i: (l, 0, i)),
                  pl.BlockSpec((None, 1, bn), lambda l, i: (l, 0, i))],
        out_specs=pl.BlockSpec((None, rows, bn), lambda l, i: (l, 0, i)),
        out_shape=jax.ShapeDtypeStruct((depth, rows, n), F32),
        compiler_params=_cparams("parallel", "parallel"),
        name="mod_rows",
    )(cstack, w_mod, b_mod.reshape(depth, 1, n))


def _rope_tile(t, cos, sin, first):
    partner = jnp.where(first, pltpu.roll(t, LANES - 16, 1), pltpu.roll(t, 16, 1))
    return t * cos + partner * sin


def _inproj_kernel(x_ref, sh_ref, sc_ref, g_ref, w_ref, cos_ref, sin_ref,
                   u_ref, f_ref, q_ref, k_ref, v_ref, *, widths, rope, sub):
    w_ssm, w_fn, w_na = widths
    scale = HEAD_DIM ** -0.5 * math.log2(math.e)
    w = w_ref[...].astype(BF)
    for r0 in range(0, x_ref.shape[0], sub):
        rs = slice(r0, r0 + sub)
        x = x_ref[rs, :]
        ms = jnp.mean(x * x, axis=-1, keepdims=True)
        xn = x * lax.rsqrt(ms + EPS) * g_ref[...]
        m = (xn * (1.0 + sc_ref[...]) + sh_ref[...]).astype(BF)

        h = jnp.dot(m, w, preferred_element_type=F32)

        def proj(lo, n, h=h):
            return h[:, lo:lo + n]

        u = proj(0, w_ssm)
        for j in range(w_ssm // LANES):
            u_ref[j, rs, :] = u[:, j * LANES:(j + 1) * LANES]
        f_ref[rs, :] = proj(w_ssm, w_fn).astype(BF)
        q = proj(w_ssm + w_fn, w_na)
        k = proj(w_ssm + w_fn + w_na, w_na)
        v_ref[rs, :] = proj(w_ssm + w_fn + 2 * w_na, w_na).astype(BF)
        if rope:
            cos = cos_ref[rs, :]
            sin = sin_ref[rs, :]
            lane = lax.broadcasted_iota(jnp.int32, cos.shape, 1)
            first = (lane % 32) < 16
            for j in range(w_na // LANES):
                sl = slice(j * LANES, (j + 1) * LANES)
                q_ref[rs, sl] = (_rope_tile(q[:, sl], cos, sin, first) * scale).astype(BF)
                k_ref[rs, sl] = _rope_tile(k[:, sl], cos, sin, first).astype(BF)
        else:
            q_ref[rs, :] = (q * scale).astype(BF)
            k_ref[rs, :] = k.astype(BF)


def _inproj_pair_kernel(x_ref, xc_ref, sh_ref, sc_ref, g_ref, w_ref, cos_ref, sin_ref, *outs,
                        widths, sub, n_lat_tiles):
    common = (sh_ref, sc_ref, g_ref, w_ref, cos_ref, sin_ref)
    i = pl.program_id(0)

    @pl.when(i < n_lat_tiles)
    def _latent():
        _inproj_kernel(x_ref, *common, *outs[:5], widths=widths, rope=True, sub=sub)

    @pl.when(i >= n_lat_tiles)
    def _context():
        _inproj_kernel(xc_ref, *common, *outs[5:], widths=widths, rope=False, sub=min(sub, xc_ref.shape[0]))


def _inproj(x2d, xc2d, mod3, g_pre, w_in, cos, sin, *, layer, tm, tiles_per_batch, ctx_row, widths):
    n, d = x2d.shape
    nc = xc2d.shape[0]
    w_ssm, w_fn, w_na = widths
    nl = n // tm
    tile = lambda i: jnp.minimum(i, nl - 1)
    row = lambda i: jnp.where(i < nl, tile(i) // tiles_per_batch, ctx_row)
    pos = lambda i: (tile(i) % tiles_per_batch, 0)
    lat = lambda i: (tile(i), 0)
    one = lambda i: (0, 0)

    def outs(rows, idx, idx3):
        specs = [pl.BlockSpec((w_ssm // LANES, rows, LANES), idx3), pl.BlockSpec((rows, w_fn), idx)]
        specs += [pl.BlockSpec((rows, w_na), idx)] * 3
        return specs

    def shapes(rows):
        return ([jax.ShapeDtypeStruct((w_ssm // LANES, rows, LANES), F32), jax.ShapeDtypeStruct((rows, w_fn), BF)]
                + [jax.ShapeDtypeStruct((rows, w_na), BF)] * 3)

    kern = functools.partial(_inproj_pair_kernel, widths=widths, sub=min(tm, 512), n_lat_tiles=nl)
    res = pl.pallas_call(
        kern,
        grid=(nl + 1,),
        in_specs=[pl.BlockSpec((tm, d), lat),
                  pl.BlockSpec((nc, d), one),
                  pl.BlockSpec((None, 1, d), lambda i: (row(i), 0, 0)),
                  pl.BlockSpec((None, 1, d), lambda i: (row(i), 0, 1)),
                  pl.BlockSpec((None, 1, d), lambda i: (layer, 0, 0)),
                  _layer_resident(w_in, layer),
                  pl.BlockSpec((tm, LANES), pos),
                  pl.BlockSpec((tm, LANES), pos)],
        out_specs=outs(tm, lat, lambda i: (0, tile(i), 0)) + outs(nc, one, lambda i: (0, 0, 0)),
        out_shape=shapes(n) + shapes(nc),
        compiler_params=_cparams("arbitrary"),
        name="inproj",
    )(x2d, xc2d, mod3, mod3, g_pre.reshape(-1, 1, d), w_in, cos, sin)
    return res[:5], res[5:]


def _rope_tables(seq):
    t = np.arange(seq)
    row = (t // GRID_W).astype(np.float32)
    col = (t % GRID_W).astype(np.float32)
    quarter = HEAD_DIM // 4
    freqs = (np.float32(ROPE_BASE) ** (-np.arange(quarter, dtype=np.float32) / quarter)).astype(np.float32)
    d = np.arange(LANES) % HEAD_DIM
    use_col = (d // (HEAD_DIM // 2)) == 1
    fidx = d % quarter
    sign = np.where((d % (HEAD_DIM // 2)) // quarter == 0, -1.0, 1.0)
    pos = np.where(use_col[None, :], col[:, None], row[:, None])
    ang = (pos * freqs[fidx][None, :]).astype(np.float32).astype(np.float64)
    return jnp.asarray(np.cos(ang), F32), jnp.asarray(np.sin(ang) * sign[None, :], F32)


def _gelu_tanh(x):
    c = math.sqrt(2.0 / math.pi)
    return x * (0.5 * (1.0 + jnp.tanh(c * (x + 0.044715 * (x * x * x)))))


def _group_of(idx, width, groups):
    shift = width.bit_length() - 1
    assert width == 1 << shift and groups & (groups - 1) == 0
    return lax.bitwise_and(lax.shift_right_logical(idx, shift), groups - 1)


def _same_group(shape, row0, row_width, col0, col_width, groups):
    r = lax.broadcasted_iota(jnp.int32, shape, 0) + row0
    c = lax.broadcasted_iota(jnp.int32, shape, 1) + col0
    return _group_of(r, row_width, groups) == _group_of(c, col_width, groups)


def _swap_stage(tiles, d):
    lane = lax.broadcasted_iota(jnp.int32, tiles[0].shape, 1)
    upper = lax.bitwise_and(lax.shift_right_logical(lane, SSM_GROUP.bit_length() - 1), d) != 0
    out = list(tiles)
    for x in range(len(tiles)):
        if x & d:
            continue
        a, b = tiles[x], tiles[x + d]
        out[x] = jnp.where(upper, pltpu.roll(b, SSM_GROUP * d, 1), a)
        out[x + d] = jnp.where(upper, b, pltpu.roll(a, LANES - SSM_GROUP * d, 1))
    return out


def _ssm_kernel(ul_ref, uc_ref, wc_ref, lag_ref, dc_ref, rexp_ref, cexp_ref, lam_ref, lamseg_ref,
                z_ref, zc_ref, u_s, bm_s, kin_s, cm_s, s_ref, h_ref, e_ref, *, seg_len, n_lat, n_ctx, stages):
    T = SSM_T
    ns = 2 ** len(stages)
    gs = T // ns
    sw = gs * LANES
    rows = u_s.shape[1]
    tps = s_ref.shape[0] // ns
    tpq = tps // 4
    nst = tpq * LANES
    gpt = LANES // SSM_GROUP
    per_q = gpt * SSM_STATE

    def lanes(c):
        return slice(c * LANES, (c + 1) * LANES)

    def col0(s, q):
        return q * per_q + s * nst

    def swapped(tiles, order):
        for d in order:
            tiles = _swap_stage(tiles, d)
        return tiles

    @pl.when(pl.program_id(1) == 0)
    def _expand_operators():
        rexp = rexp_ref[...]
        cexp = cexp_ref[...]
        for s in range(ns):
            wsel = jnp.concatenate([wc_ref[:, col0(s, q):col0(s, q) + nst] for q in range(4)], axis=1)
            blk = jnp.dot(rexp, wsel, preferred_element_type=F32)
            keep = _same_group(blk.shape, 0, SSM_GROUP, 0, SSM_STATE, gs)
            bm_s[s] = jnp.where(keep, blk, 0.0).astype(BF)
            dsel = jnp.concatenate([dc_ref[col0(s, q):col0(s, q) + nst, :] for q in range(4)], axis=0)
            blk = jnp.dot(dsel, cexp, preferred_element_type=F32)
            keep = _same_group(blk.shape, 0, SSM_STATE, 0, SSM_GROUP, gs)
            cm_s[s] = jnp.where(keep, blk, 0.0).astype(BF)
        r16 = lax.broadcasted_iota(jnp.int32, (LANES, SSM_GROUP), 0)
        c16 = lax.broadcasted_iota(jnp.int32, (LANES, SSM_GROUP), 1)
        chan = jnp.where(lax.bitwise_and(r16, SSM_GROUP - 1) == c16, 1.0, 0.0).astype(BF)
        for t in range(T):
            lo = (T - 1 - t) * LANES
            blk = jnp.dot(chan, lag_ref[:, lo:lo + T * LANES], preferred_element_type=F32)
            keep = _same_group(blk.shape, 0, SSM_GROUP, 0, SSM_GROUP, gpt)
            blk = jnp.where(keep, blk, 0.0)
            tiles = swapped([blk[:, lanes(x)] for x in range(T)], stages)
            r0 = (t % gs) * LANES + (t // gs) * gs * SSM_GROUP
            for s in range(ns):
                part = jnp.concatenate(tiles[s * gs:(s + 1) * gs], axis=1)
                kin_s[s, r0:r0 + gs * SSM_GROUP, :] = part[s * gs * SSM_GROUP:(s + 1) * gs * SSM_GROUP, :].astype(BF)

    xs = []
    for t in range(T):
        ct = uc_ref[pl.ds(t, n_ctx, stride=T), :]
        xs.append(jnp.concatenate([ct, ul_ref[pl.ds(t, n_lat, stride=T), :], ct], axis=0))
    xs = swapped(xs, stages)
    for s in range(ns):
        for k in range(gs):
            u_s[s, :, lanes(k)] = xs[s * gs + k].astype(BF)
    for s in range(ns):
        r = jnp.dot(u_s[s], bm_s[s], preferred_element_type=F32)
        for c in range(tps):
            s_ref[s * tps + c] = r[:, lanes(c)]

    chains = []
    for s in range(ns):
        for d in range(2):
            for k in range(tpq):
                cr, ci = s * tps + 2 * d * tpq + k, s * tps + (2 * d + 1) * tpq + k
                lr = slice(col0(s, 2 * d) + k * LANES, col0(s, 2 * d) + (k + 1) * LANES)
                li = slice(col0(s, 2 * d + 1) + k * LANES, col0(s, 2 * d + 1) + (k + 1) * LANES)
                chains.append((cr, ci, d, lr, li))
    nt = ns * tps

    def scan_step(i, carry, record):
        new = list(carry)
        for cr, ci, d, lr_sl, li_sl in chains:
            rsl = pl.ds(i if d == 0 else seg_len - 1 - i, SSM_SEG, stride=seg_len)
            hr, hi = carry[cr], carry[ci]
            if record:
                h_ref[cr, rsl, :] = hr
                h_ref[ci, rsl, :] = hi
            lr, li = lam_ref[:, lr_sl], lam_ref[:, li_sl]
            new[cr] = lr * hr - li * hi + s_ref[cr, rsl, :]
            new[ci] = lr * hi + li * hr + s_ref[ci, rsl, :]
        return tuple(new)

    zero = jnp.zeros((SSM_SEG, LANES), F32)
    fin = lax.fori_loop(0, seg_len, lambda i, c: scan_step(i, c, False), (zero,) * nt, unroll=4)
    for c in range(nt):
        e_ref[c, 0:SSM_SEG, :] = fin[c]

    for cr, ci, d, lr_sl, li_sl in chains:
        lr, li = lamseg_ref[:, lr_sl], lamseg_ref[:, li_sl]
        er = jnp.zeros((1, LANES), F32)
        ei = jnp.zeros((1, LANES), F32)
        for n in range(SSM_SEG):
            g = n if d == 0 else SSM_SEG - 1 - n
            e_ref[cr, SSM_SEG + g:SSM_SEG + g + 1, :] = er
            e_ref[ci, SSM_SEG + g:SSM_SEG + g + 1, :] = ei
            fr, fi = e_ref[cr, g:g + 1, :], e_ref[ci, g:g + 1, :]
            er, ei = lr * er - li * ei + fr, lr * ei + li * er + fi

    ent = tuple(e_ref[c, SSM_SEG:2 * SSM_SEG, :] for c in range(nt))
    lax.fori_loop(0, seg_len, lambda i, c: scan_step(i, c, True), ent, unroll=2)

    ys, ycs = [], []
    for s in range(ns):
        hcat = jnp.concatenate([h_ref[s * tps + c].astype(BF) for c in range(tps)], axis=1)
        y = _gelu_tanh(jnp.dot(u_s[s], kin_s[s], preferred_element_type=F32)
                       + jnp.dot(hcat, cm_s[s], preferred_element_type=F32))
        ys += [y[:, lanes(k)] for k in range(gs)]
        hc = jnp.concatenate(
            [h_ref[s * tps + c, 0:n_ctx, :].astype(BF) for c in range(2 * tpq)]
            + [h_ref[s * tps + c, rows - n_ctx:rows, :].astype(BF) for c in range(2 * tpq, tps)], axis=1)
        yc = _gelu_tanh(jnp.dot(u_s[s, 0:n_ctx, :], kin_s[s], preferred_element_type=F32)
                        + jnp.dot(hc, cm_s[s], preferred_element_type=F32))
        ycs += [yc[:, lanes(k)] for k in range(gs)]
    ys = swapped(ys, stages[::-1])
    ycs = swapped(ycs, stages[::-1])
    for t in range(T):
        z_ref[pl.ds(t, n_lat, stride=T), :] = ys[t][n_ctx:n_ctx + n_lat, :]
        zc_ref[pl.ds(t, n_ctx, stride=T), :] = ycs[t]


def _ssm_weights(a_re, a_im, log_dt, b_re, b_im, c_re, c_im, d_skip, seg_len):
    T = SSM_T
    g, p = a_re.shape[1], a_re.shape[2]
    hc = b_re.shape[-1]
    gpt = LANES // hc
    J = g // gpt
    lam = lax.complex(a_re.astype(F32), a_im.astype(F32))
    dt = jnp.exp(log_dt.astype(F32))[..., None]
    ldt = lam * dt
    lam_bar = jnp.exp(ldt)
    bbar = ((lam_bar - 1) / lam)[..., None] * lax.complex(b_re.astype(F32), b_im.astype(F32))
    cmat = lax.complex(c_re.astype(F32), c_im.astype(F32))
    kk = jnp.arange(T + 1, dtype=F32)
    pw = jnp.exp(ldt[:, None] * kk[None, :, None, None])

    wf = pw[0, :T][::-1][:, :, None, :] * jnp.swapaxes(bbar[0], 1, 2)[None]
    wb = pw[1, :T][:, :, None, :] * jnp.swapaxes(bbar[1], 1, 2)[None]
    wq = jnp.stack([wf.real, wf.imag, wb.real, wb.imag]).reshape(4, T, J, gpt, hc, p)
    wc = wq.transpose(2, 1, 4, 0, 3, 5).reshape(J, T * hc, 4 * gpt * p)

    df = cmat[0][None] * pw[0, 1:][:, :, None, :]
    db = cmat[1][None] * pw[1, 1:][::-1][:, :, None, :]
    dq = jnp.stack([df.real, -df.imag, db.real, -db.imag]).reshape(4, T, J, gpt, hc, p)
    dc = dq.transpose(2, 0, 3, 5, 1, 4).reshape(J, 4 * gpt * p, T * hc)

    mf = jnp.einsum('gcp,kgp,gph->khgc', cmat[0], pw[0, :T], bbar[0]).real
    mb = jnp.einsum('gcp,kgp,gph->khgc', cmat[1], pw[1, :T], bbar[1]).real
    skip = jnp.eye(hc, dtype=F32)[:, None, :] * d_skip.astype(F32).reshape(g, hc)[None]
    m0 = mf[0] + mb[0] + skip
    lags = jnp.concatenate([mb[1:][::-1], m0[None], mf[1:]], axis=0)
    kc = lags.reshape(2 * T - 1, hc, J, LANES).transpose(2, 1, 0, 3).reshape(J, hc, (2 * T - 1) * LANES)

    def lam_rows(power):
        lp = jnp.exp(ldt * power)
        lt = lp.reshape(2, J, gpt * p)
        return jnp.concatenate([lt[0].real, lt[0].imag, lt[1].real, lt[1].imag], axis=-1)[:, None, :]

    lam_t = jnp.broadcast_to(lam_rows(float(T)), (J, SSM_SEG, 4 * gpt * p))
    return wc.astype(BF), kc.astype(BF), dc.astype(BF), lam_t, lam_rows(float(T * seg_len))


def _ssm_expanders(gs):
    r = np.arange(gs * LANES)
    k, slot, ch = r // LANES, (r % LANES) // SSM_GROUP, r % SSM_GROUP
    t = (slot // gs) * gs + k
    rexp = np.zeros((gs * LANES, SSM_T * SSM_GROUP), np.float32)
    rexp[r, t * SSM_GROUP + ch] = 1.0
    return jnp.asarray(rexp, BF), jnp.asarray(rexp.T, BF)


def _ssm(u3, u3c, wc, kc, dc, lam, lamseg, *, layer, batch, seq, n_ctx):
    T = SSM_T
    J = u3.shape[0]
    ns = wc.shape[-1]
    nsets = 2 ** len(SSM_STAGES)
    gs = T // nsets
    sw = gs * LANES
    n_lat, n_c = seq // T, n_ctx // T
    rows = n_lat + 2 * n_c
    seg_len = rows // SSM_SEG
    assert rows % SSM_SEG == 0 and n_c % 16 == 0
    rexp, cexp = _ssm_expanders(gs)
    kern = functools.partial(_ssm_kernel, seg_len=seg_len, n_lat=n_lat, n_ctx=n_c, stages=SSM_STAGES)
    const = lambda j, b: (0, 0)
    per_j = lambda j, b: (layer, j, 0, 0)
    per_jb = lambda j, b: (j, b, 0)
    return pl.pallas_call(
        kern,
        grid=(J, batch),
        in_specs=[pl.BlockSpec((None, seq, LANES), per_jb),
                  pl.BlockSpec((None, n_ctx, LANES), per_jb),
                  pl.BlockSpec((None, None) + wc.shape[2:], per_j),
                  pl.BlockSpec((None, None) + kc.shape[2:], per_j),
                  pl.BlockSpec((None, None) + dc.shape[2:], per_j),
                  pl.BlockSpec(rexp.shape, const),
                  pl.BlockSpec(cexp.shape, const),
                  pl.BlockSpec((None, None, SSM_SEG, ns), per_j),
                  pl.BlockSpec((None, None, 1, ns), per_j)],
        out_specs=[pl.BlockSpec((None, seq, LANES), per_jb),
                   pl.BlockSpec((None, n_ctx, LANES), per_jb)],
        out_shape=[jax.ShapeDtypeStruct(u3.shape, F32),
                   jax.ShapeDtypeStruct(u3c.shape, F32)],
        scratch_shapes=[pltpu.VMEM((nsets, rows, sw), BF),
                        pltpu.VMEM((nsets, sw, ns // nsets), BF),
                        pltpu.VMEM((nsets, sw, sw), BF),
                        pltpu.VMEM((nsets, ns // nsets, sw), BF),
                        pltpu.VMEM((ns // LANES, rows, LANES), F32),
                        pltpu.VMEM((ns // LANES, rows, LANES), F32),
                        pltpu.VMEM((ns // LANES, 2 * SSM_SEG, LANES), F32)],
        compiler_params=_cparams("arbitrary", "arbitrary"),
        name="ssm_chunk_scan",
    )(u3, u3c, wc, kc, dc, rexp, cexp, lam, lamseg)


def _dft_a_kernel(w_ref, x_ref, o_ref):
    o_ref[...] = jnp.dot(w_ref[...], x_ref[...], preferred_element_type=F32).astype(BF)


def _dft_rows(n):
    k = np.arange(n)
    m = (k[:, None] * k[None, :]) % n
    ang = m * (2.0 * math.pi / n)
    return jnp.asarray(np.concatenate([np.cos(ang), -np.sin(ang)], axis=0), BF)


def _dft_a(w, x3):
    B, n, cols = x3.shape
    tn = min(cols, 16384)
    return pl.pallas_call(
        _dft_a_kernel,
        grid=(B, cols // tn),
        in_specs=[pl.BlockSpec((2 * n, n), lambda b, i: (0, 0)),
                  pl.BlockSpec((None, n, tn), lambda b, i: (b, 0, i))],
        out_specs=pl.BlockSpec((None, 2 * n, tn), lambda b, i: (b, 0, i)),
        out_shape=jax.ShapeDtypeStruct((B, 2 * n, cols), BF),
        compiler_params=_cparams("parallel", "parallel"),
        name="dft_stage_a",
    )(w, x3)


def _store_lane_tiles(o_ref, p, n, wf, idx):
    per = wf // LANES
    for c in range(2 * per):
        part = p[(c // per) * n:(c // per + 1) * n, (c % per) * LANES:(c % per + 1) * LANES]
        o_ref[(c,) + idx] = part


def _dft_b_kernel(a_ref, g_ref, o_ref, *, kb, l1, wf):
    for i in range(kb):
        g = jnp.concatenate([g_ref[0, i], g_ref[1, i]], axis=0)
        p = jnp.dot(a_ref[i], g, preferred_element_type=F32)
        _store_lane_tiles(o_ref, p, l1, wf, (i,))


def _dft_b_mats(seq, l1, l2):
    k2 = np.arange(l2)[:, None, None]
    k1 = np.arange(l1)[None, :, None]
    j1 = np.arange(l1)[None, None, :]
    m = ((k1 * l2 + k2) * j1) % seq
    ang = m * (2.0 * math.pi / seq)
    ar, ai = np.cos(ang).astype(np.float32), -np.sin(ang).astype(np.float32)
    top = np.concatenate([ar, -ai], axis=2)
    bot = np.concatenate([ai, ar], axis=2)
    return jnp.asarray(np.concatenate([top, bot], axis=1), BF)


def _dft_b(amat, g5, *, kb):
    B, _, l2, l1, wf = g5.shape
    nlt = 2 * wf // LANES
    kern = functools.partial(_dft_b_kernel, kb=kb, l1=l1, wf=wf)
    return pl.pallas_call(
        kern,
        grid=(B, l2 // kb),
        in_specs=[pl.BlockSpec((kb, 2 * l1, 2 * l1), lambda b, i: (i, 0, 0)),
                  pl.BlockSpec((None, 2, kb, l1, wf), lambda b, i: (b, 0, i, 0, 0))],
        out_specs=pl.BlockSpec((None, nlt, kb, l1, LANES), lambda b, i: (b, 0, i, 0, 0)),
        out_shape=jax.ShapeDtypeStruct((B, nlt, l2, l1, LANES), F32),
        compiler_params=_cparams("parallel", "parallel"),
        name="dft_stage_b",
    )(amat, g5)


def _dft_ctx_kernel(w_ref, x_ref, o_ref, *, n, wf):
    p = jnp.dot(w_ref[...], x_ref[...], preferred_element_type=F32)
    _store_lane_tiles(o_ref, p, n, wf, ())


def _dft_ctx(w, x3):
    B, n, wf = x3.shape
    nlt = 2 * wf // LANES
    kern = functools.partial(_dft_ctx_kernel, n=n, wf=wf)
    return pl.pallas_call(
        kern,
        grid=(B,),
        in_specs=[pl.BlockSpec((2 * n, n), lambda b: (0, 0)),
                  pl.BlockSpec((None, n, wf), lambda b: (b, 0, 0))],
        out_specs=pl.BlockSpec((None, nlt, n, LANES), lambda b: (b, 0, 0, 0)),
        out_shape=jax.ShapeDtypeStruct((B, nlt, n, LANES), F32),
        compiler_params=_cparams("parallel"),
        name="dft_ctx",
    )(w, x3)


def _channel_dft(wf, seq):
    c = np.arange(wf)
    same = (c[:, None] // FNET_GROUP) == (c[None, :] // FNET_GROUP)
    m = ((c[:, None] % FNET_GROUP) * (c[None, :] % FNET_GROUP)) % FNET_GROUP
    ang = m * (2.0 * math.pi / FNET_GROUP)
    mask = same / math.sqrt(seq * FNET_GROUP)
    return jnp.asarray(np.concatenate([np.cos(ang) * mask, np.sin(ang) * mask], axis=0), BF)


def _softmax_pv(parts, l_shape):
    m = parts[0][0].max(axis=-1, keepdims=True)
    for s, _ in parts[1:]:
        m = jnp.maximum(m, s.max(axis=-1, keepdims=True))
    l = jnp.zeros(l_shape, F32)
    o = None
    for s, v in parts:
        p = jnp.exp2(s - m)
        l = l + p.sum(axis=-1, keepdims=True)
        pv = jnp.dot(p.astype(BF), v, preferred_element_type=F32)
        o = pv if o is None else o + pv
    return o / l


def _na_kernel(kb_ref, pid_ref, dr_ref, q_ref, k_ref, v_ref, kc_ref, vc_ref, cbx_ref, o_ref, bias_ref,
               *, nkeys, heads, nsub):
    step = pl.program_id(1)
    nq = q_ref.shape[0] // nsub
    rq, kr = nq // GRID_W, nkeys // GRID_W
    lane = lax.broadcasted_iota(jnp.int32, (nq, LANES), 1)
    low = lane < HEAD_DIM
    dn = (((1,), (1,)), ((), ()))

    @pl.when((pl.program_id(0) == 0) & (step == 0))
    def _build_bias():
        for pat in range(bias_ref.shape[0]):
            for qi in range(rq):
                for w in range(kr):
                    a = dr_ref[(pat * rq + qi) * kr + w]
                    for h in range(heads):
                        bias_ref[pat, h, qi * GRID_W:(qi + 1) * GRID_W, w * GRID_W:(w + 1) * GRID_W] = cbx_ref[h, a]

    tasks = [(sb, h) for sb in range(nsub) for h in range(heads)]
    starts = [pl.multiple_of(kb_ref[step * nsub + sb] * GRID_W, GRID_W) for sb in range(nsub)]
    pats = [pid_ref[step * nsub + sb] for sb in range(nsub)]

    n_ctx = kc_ref.shape[0]
    ones = jnp.ones((n_ctx + nkeys, LANES), BF)
    kv = {}

    def keys_values(sb, hp):
        if (sb, hp) not in kv:
            sl = slice(hp * LANES, (hp + 1) * LANES)
            win = pl.ds(starts[sb], nkeys)
            k_all = jnp.concatenate([kc_ref[:, sl], k_ref[win, sl]], axis=0)
            v_all = jnp.concatenate([jnp.concatenate([vc_ref[:, sl], v_ref[win, sl]], axis=0), ones], axis=1)
            kv[sb, hp] = (k_all, v_all)
        return kv[sb, hp]

    def scores(sb, h):
        sl = slice((h // 2) * LANES, (h // 2 + 1) * LANES)
        q2 = q_ref[sb * nq:(sb + 1) * nq, sl]
        qm = jnp.where(low if h % 2 == 0 else jnp.logical_not(low), q2, jnp.zeros_like(q2))
        s = lax.dot_general(qm, keys_values(sb, h // 2)[0], dn, preferred_element_type=F32)
        return s[:, n_ctx:] + bias_ref[pats[sb], h], s[:, :n_ctx]

    def weights(s_w, s_c):
        m = jnp.maximum(s_w.max(axis=-1, keepdims=True), s_c.max(axis=-1, keepdims=True))
        return (jnp.concatenate([jnp.exp2(s_c - m).astype(BF), jnp.exp2(s_w - m).astype(BF)], axis=1),)

    outs = {}

    def finish(sb, h, p):
        sl = slice((h // 2) * LANES, (h // 2 + 1) * LANES)
        ox = jnp.dot(p, keys_values(sb, h // 2)[1], preferred_element_type=F32)
        outs[h % 2] = ox[:, :LANES] / ox[:, LANES:]
        if h % 2 == 1:
            o_ref[sb * nq:(sb + 1) * nq, sl] = jnp.where(low, outs[0], outs[1]).astype(BF)

    nt = len(tasks)
    sc = {0: scores(*tasks[0])}
    if nt > 1:
        sc[1] = scores(*tasks[1])
    pw = {0: weights(*sc.pop(0))}
    for n in range(nt):
        if n + 2 < nt:
            sc[n + 2] = scores(*tasks[n + 2])
        if n + 1 < nt:
            pw[n + 1] = weights(*sc.pop(n + 1))
        finish(*tasks[n], *pw.pop(n))


def _na_plan(rows, rq):
    wr = min(NA_ROWS, rows)
    kr = rq + wr - 1
    nblk = rows // rq
    kbs, pids, pats = [], [], []
    for blk in range(nblk):
        r0 = blk * rq
        rs = [int(np.clip(r0 + i - wr // 2, 0, rows - wr)) for i in range(rq)]
        kb = min(rs[0], rows - kr)
        pat = (tuple(r - kb for r in rs), r0 - kb)
        if pat not in pats:
            pats.append(pat)
        kbs.append(kb)
        pids.append(pats.index(pat))
    return np.asarray(kbs, np.int32), np.asarray(pids, np.int32), pats, kr, wr


def _na_bias(rpb, pats, rq, kr, wr):
    cols = np.arange(GRID_W)
    cstart = np.clip(cols - NA_COLS // 2, 0, GRID_W - NA_COLS)
    kc = np.arange(GRID_W)
    col_ok = (kc[None, :] >= cstart[:, None]) & (kc[None, :] < cstart[:, None] + NA_COLS)
    dc = kc[None, :] - cols[:, None] + NA_COLS - 1
    depth, heads, nr, ncol = rpb.shape
    sel = (dc[:, :, None] == np.arange(ncol)[None, None, :]) & col_ok[:, :, None]
    cb = jnp.einsum('lhab,ckb->lhack', rpb.astype(F32), jnp.asarray(sel, F32), precision=lax.Precision.HIGHEST)
    cb = jnp.where(jnp.asarray(col_ok)[None, None, None], cb * math.log2(math.e), NEG)
    cbx = jnp.concatenate([cb, jnp.full((depth, heads, 1, GRID_W, GRID_W), NEG, F32)], axis=2)
    slots = []
    for rs_off, r_off in pats:
        i = np.arange(rq)[:, None]
        w = np.arange(kr)[None, :]
        rso = np.asarray(rs_off)[:, None]
        row_ok = (w >= rso) & (w < rso + wr)
        slots.append(np.where(row_ok, w - r_off - i + NA_ROWS - 1, nr))
    return cbx, np.stack(slots).reshape(-1).astype(np.int32)


def _na_tables(rpb_all, seq, rq):
    kbs, pids, pats, kr, wr = _na_plan(seq // GRID_W, rq)
    cbx, slots = _na_bias(rpb_all, pats, rq, kr, wr)
    return dict(kbs=jnp.asarray(kbs), pids=jnp.asarray(pids), slots=jnp.asarray(slots), cbx=cbx, kr=kr, rq=rq,
                npat=len(pats))


def _na_latent(q, k, v, kc, vc, tabs, layer, *, batch, seq, n_ctx):
    rows = seq // GRID_W
    cbx, rq, kr = tabs["cbx"], tabs["rq"], tabs["kr"]
    heads = cbx.shape[1]
    nblk = rows // rq
    nq = rq * GRID_W
    nkeys = kr * GRID_W
    w = q.shape[1]
    nsub = 2 if nblk % 2 == 0 else 1
    nstep = nblk // nsub
    kern = functools.partial(_na_kernel, nkeys=nkeys, heads=heads, nsub=nsub)
    grid_spec = pltpu.PrefetchScalarGridSpec(
        num_scalar_prefetch=3,
        grid=(batch, nstep),
        in_specs=[pl.BlockSpec((nsub * nq, w), lambda b, i, *_: (b * nstep + i, 0)),
                  pl.BlockSpec((seq, w), lambda b, i, *_: (b, 0)),
                  pl.BlockSpec((seq, w), lambda b, i, *_: (b, 0)),
                  pl.BlockSpec((n_ctx, w), lambda b, i, *_: (b, 0)),
                  pl.BlockSpec((n_ctx, w), lambda b, i, *_: (b, 0)),
                  pl.BlockSpec((None,) + cbx.shape[1:], lambda b, i, *_: (layer, 0, 0, 0, 0))],
        out_specs=pl.BlockSpec((nsub * nq, w), lambda b, i, *_: (b * nstep + i, 0)),
        scratch_shapes=[pltpu.VMEM((tabs["npat"], heads, nq, nkeys), F32)],
    )
    return pl.pallas_call(
        kern,
        grid_spec=grid_spec,
        out_shape=jax.ShapeDtypeStruct(q.shape, BF),
        compiler_params=_cparams("arbitrary", "arbitrary"),
        name="na_latent",
    )(tabs["kbs"], tabs["pids"], tabs["slots"], q, k, v, kc, vc, cbx)


def _na_ctx_kernel(q_ref, k_ref, v_ref, o_ref, *, heads):
    nq = q_ref.shape[0]
    lane = lax.broadcasted_iota(jnp.int32, (nq, LANES), 1)
    low = lane < HEAD_DIM
    dn = (((1,), (1,)), ((), ()))
    for hp in range(heads // 2):
        sl = slice(hp * LANES, (hp + 1) * LANES)
        q2 = q_ref[:, sl]
        k2 = k_ref[:, sl]
        v2 = v_ref[:, sl]
        outs = []
        for hh in range(2):
            qm = jnp.where(low if hh == 0 else jnp.logical_not(low), q2, jnp.zeros_like(q2))
            s = lax.dot_general(qm, k2, dn, preferred_element_type=F32)
            outs.append(_softmax_pv([(s, v2)], (nq, 1)))
        o_ref[:, sl] = jnp.where(low, outs[0], outs[1]).astype(BF)


def _na_ctx(qc, kc, vc, *, batch, n_ctx, heads):
    w = qc.shape[1]
    spec = pl.BlockSpec((n_ctx, w), lambda b: (b, 0))
    return pl.pallas_call(
        functools.partial(_na_ctx_kernel, heads=heads),
        grid=(batch,),
        in_specs=[spec, spec, spec],
        out_specs=spec,
        out_shape=jax.ShapeDtypeStruct(qc.shape, BF),
        compiler_params=_cparams("parallel"),
        name="na_ctx",
    )(qc, kc, vc)


def _outproj_kernel(z_ref, pf_ref, na_ref, x_ref, gate_ref, gpost_ref,
                    wglu_ref, cs_ref, wfo_ref, wout_ref, o_ref, *, widths, sub):
    w_ssm, w_fn, w_na = widths
    wglu, wfo, wout = (r[...].astype(BF) for r in (wglu_ref, wfo_ref, wout_ref))
    for r0 in range(0, x_ref.shape[0], sub):
        rs = slice(r0, r0 + sub)
        z = jnp.concatenate([z_ref[j, rs, :] for j in range(w_ssm // LANES)], axis=1)
        gl = jnp.dot(z.astype(BF), wglu, preferred_element_type=F32)
        y_ssm = (z * jax.nn.sigmoid(gl)).astype(BF)
        if len(pf_ref.shape) == 3:
            tiles = [pf_ref[c, rs, :] for c in range(pf_ref.shape[0])]
        else:
            l2 = pf_ref.shape[1]
            tiles = [jnp.concatenate([pf_ref[c, :, k1, :] for k1 in range(r0 // l2, (r0 + sub) // l2)], axis=0)
                     for c in range(pf_ref.shape[0])]
        pf = jnp.concatenate(tiles, axis=1).astype(BF)
        mixed = jnp.dot(pf, cs_ref[...], preferred_element_type=F32).astype(BF)
        y_fft = jnp.dot(mixed, wfo, preferred_element_type=F32).astype(BF)
        y = jnp.concatenate([y_ssm, y_fft, na_ref[rs, :]], axis=1)
        o = jnp.dot(y, wout, preferred_element_type=F32)
        ms = jnp.mean(o * o, axis=-1, keepdims=True)
        o_ref[rs, :] = x_ref[rs, :] + gate_ref[...] * (o * lax.rsqrt(ms + EPS) * gpost_ref[...])


def _outproj(z3, pf, yna, x2d, mod3, g_post, w_glu, cs, w_fo, w_out, *,
             layer, tm, tiles_per_batch, mod_row, widths):
    n, d = x2d.shape
    w_ssm, w_fn, w_na = widths
    if mod_row is None:
        row = lambda i: i // tiles_per_batch
    else:
        row = lambda i: mod_row
    kern = functools.partial(_outproj_kernel, widths=widths, sub=min(tm, 512))
    return pl.pallas_call(
        kern,
        grid=(n // tm,),
        in_specs=[pl.BlockSpec((w_ssm // LANES, tm, LANES), lambda i: (0, i, 0)),
                  (pl.BlockSpec((None, pf.shape[1], tm, LANES),
                                lambda i: (i // tiles_per_batch, 0, i % tiles_per_batch, 0))
                   if pf.ndim == 4 else
                   pl.BlockSpec((None, pf.shape[1], pf.shape[2], tm // pf.shape[2], LANES),
                                lambda i: (i // tiles_per_batch, 0, 0, i % tiles_per_batch, 0))),
                  pl.BlockSpec((tm, w_na), lambda i: (i, 0)),
                  pl.BlockSpec((tm, d), lambda i: (i, 0)),
                  pl.BlockSpec((None, 1, d), lambda i: (row(i), 0, 2)),
                  pl.BlockSpec((None, 1, d), lambda i: (layer, 0, 0)),
                  _layer_resident(w_glu, layer),
                  _resident(cs.shape, lambda i: (0, 0)),
                  _layer_resident(w_fo, layer),
                  _layer_resident(w_out, layer)],
        out_specs=pl.BlockSpec((tm, d), lambda i: (i, 0)),
        out_shape=jax.ShapeDtypeStruct((n, d), F32),
        compiler_params=_cparams("parallel"),
        name="outproj",
    )(z3, pf, yna, x2d, mod3, g_post.reshape(-1, 1, d), w_glu, cs, w_fo, w_out)


def _ffn_kernel(x_ref, sh_ref, sc_ref, gate_ref, gpre_ref, gpost_ref, wg_ref, wu_ref, wd_ref,
                o_ref, a_ref, *, sub, chunk):
    dff = wg_ref.shape[1]
    for r0 in range(0, x_ref.shape[0], sub):
        rs = slice(r0, r0 + sub)
        x = x_ref[rs, :]
        ms = jnp.mean(x * x, axis=-1, keepdims=True)
        m = ((x * lax.rsqrt(ms + EPS) * gpre_ref[...]) * (1.0 + sc_ref[...]) + sh_ref[...]).astype(BF)
        for c0 in range(0, dff, chunk):
            sl = slice(c0, min(c0 + chunk, dff))
            g = jnp.dot(m, wg_ref[:, sl], preferred_element_type=F32)
            u = jnp.dot(m, wu_ref[:, sl], preferred_element_type=F32)
            a_ref[rs, sl] = (g * jax.nn.sigmoid(g) * u).astype(BF)
        y = jnp.dot(a_ref[rs, :], wd_ref[...], preferred_element_type=F32)
        ms2 = jnp.mean(y * y, axis=-1, keepdims=True)
        o_ref[rs, :] = x + gate_ref[...] * (y * lax.rsqrt(ms2 + EPS) * gpost_ref[...])


FFN_WEIGHT_STEPS = 8


def _ffn_staged_kernel(*refs, sub, chunk, n_lat_tiles, with_ctx):
    if with_ctx:
        (x_ref, xc_ref, sh_ref, sc_ref, gate_ref, gpre_ref, gpost_ref, wg_ref, wu_ref, wd_ref,
         o_ref, oc_ref, a_ref, wg_s, wu_s, wd_s) = refs
    else:
        (x_ref, sh_ref, sc_ref, gate_ref, gpre_ref, gpost_ref, wg_ref, wu_ref, wd_ref,
         o_ref, a_ref, wg_s, wu_s, wd_s) = refs
    i = pl.program_id(0)
    common = (sh_ref, sc_ref, gate_ref, gpre_ref, gpost_ref, wg_s, wu_s, wd_s)

    @pl.when(i < FFN_WEIGHT_STEPS)
    def _cast_weights():
        for src, dst in ((wg_ref, wg_s), (wu_ref, wu_s), (wd_ref, wd_s)):
            rows = src.shape[0]
            dst[pl.ds(pl.multiple_of(i * rows, rows), rows), :] = src[...].astype(BF)

    @pl.when((i >= FFN_WEIGHT_STEPS) & (i < FFN_WEIGHT_STEPS + n_lat_tiles))
    def _latent():
        _ffn_kernel(x_ref, *common, o_ref, a_ref, sub=sub, chunk=chunk)

    if with_ctx:
        @pl.when(i >= FFN_WEIGHT_STEPS + n_lat_tiles)
        def _context():
            _ffn_kernel(xc_ref, *common, oc_ref, a_ref, sub=min(sub, xc_ref.shape[0]), chunk=chunk)


def _ffn(x2d, xc2d, mod3, g_pre, g_post, wg, wu, wd, *, layer, tm, tiles_per_batch, ctx_row):
    n, d = x2d.shape
    dff = wg.shape[2]
    nl = n // tm
    nw = FFN_WEIGHT_STEPS
    with_ctx = xc2d is not None
    gains = (g_pre.reshape(-1, 1, d), g_post.reshape(-1, 1, d))
    slab = lambda i: (layer, jnp.minimum(i, nw - 1), 0)
    weights = [pl.BlockSpec((None, d // nw, dff), slab), pl.BlockSpec((None, d // nw, dff), slab),
               pl.BlockSpec((None, dff // nw, d), slab)]
    gain_specs = [pl.BlockSpec((None, 1, d), lambda i: (layer, 0, 0))] * 2
    tile = lambda i: jnp.clip(i - nw, 0, nl - 1)
    row = lambda i: jnp.where(i < nw + nl, tile(i) // tiles_per_batch, ctx_row)
    lat = lambda i: (tile(i), 0)
    mods = [pl.BlockSpec((None, 1, d), functools.partial(lambda i, c: (row(i), 0, c), c=c)) for c in (3, 4, 5)]
    in_specs = [pl.BlockSpec((tm, d), lat)]
    out_specs = [pl.BlockSpec((tm, d), lat)]
    out_shape = [jax.ShapeDtypeStruct((n, d), F32)]
    args = [x2d]
    if with_ctx:
        nc = xc2d.shape[0]
        ctx = lambda i: (jnp.maximum(i - nw - nl, 0), 0)
        in_specs.append(pl.BlockSpec((nc, d), ctx, pipeline_mode=pl.Buffered(1)))
        out_specs.append(pl.BlockSpec((nc, d), ctx))
        out_shape.append(jax.ShapeDtypeStruct((nc, d), F32))
        args.append(xc2d)
    outs = pl.pallas_call(
        functools.partial(_ffn_staged_kernel, sub=min(tm, 512), chunk=3 * MXU_TILE, n_lat_tiles=nl,
                          with_ctx=with_ctx),
        grid=(nw + nl + int(with_ctx),),
        in_specs=in_specs + mods + gain_specs + weights,
        out_specs=out_specs,
        out_shape=out_shape,
        scratch_shapes=[pltpu.VMEM((tm, dff), BF), pltpu.VMEM((d, dff), BF), pltpu.VMEM((d, dff), BF),
                        pltpu.VMEM((dff, d), BF)],
        compiler_params=_cparams("arbitrary"),
        name="ffn_staged",
    )(*args, mod3, mod3, mod3, *gains, wg, wu, wd)
    return (outs[0], outs[1]) if with_ctx else (outs[0], None)


def _layer(x2d, xc2d, p, w, layer, *, batch, seq, n_ctx, last, rope_tabs, consts):
    d = x2d.shape[1]
    w_ssm = consts["w_ssm"]
    w_fn = w["w_fourier"].shape[1]
    w_na = (w["w_in"].shape[2] - w_ssm - w_fn) // 3
    widths = (w_ssm, w_fn, w_na)
    heads = w_na // HEAD_DIM
    tm = min(512, seq)
    tmc = min(512, batch * n_ctx)

    mod = w["mod"][layer]
    mod3 = mod.reshape(mod.shape[0], 1, 6 * d)

    cos, sin = rope_tabs
    tm2 = min(2 * tm, seq)
    (u3, f, q, k, v), (u3c, fc, qc, kc, vc) = _inproj(
        x2d, xc2d, mod3, p["g_pre_mix"], w["w_in"], cos, sin, layer=layer, tm=tm2,
        tiles_per_batch=seq // tm2, ctx_row=batch, widths=widths)

    z3, zc3 = _ssm(u3, u3c, *w["ssm"], layer=layer, batch=batch, seq=seq, n_ctx=n_ctx)

    l2 = consts["l2"]
    l1 = seq // l2
    g = _dft_a(consts["dft_a"], f.reshape(batch, l2, l1 * w_fn))
    pf = _dft_b(consts["dft_b"], g.reshape(batch, 2, l2, l1, w_fn), kb=min(16, l2))

    yna = _na_latent(q, k, v, kc, vc, consts["na"], layer, batch=batch, seq=seq, n_ctx=n_ctx)

    w_glu, w_fo, w_out = w["w_glu"], w["w_fourier"], w["w_out"]
    x2d = _outproj(z3, pf, yna, x2d, mod3, p["g_post_mix"], w_glu, consts["cs_lat"], w_fo, w_out,
                   layer=layer, tm=tm2, tiles_per_batch=seq // tm2, mod_row=None, widths=widths)
    if not last:
        pfc = _dft_ctx(consts["dft_ctx"], fc.reshape(batch, n_ctx, w_fn))
        ynac = _na_ctx(qc, kc, vc, batch=batch, n_ctx=n_ctx, heads=heads)
        tc = min(tmc, n_ctx)
        xc2d = _outproj(zc3, pfc, ynac, xc2d, mod3, p["g_post_mix"], w_glu, consts["cs_ctx"], w_fo, w_out,
                        layer=layer, tm=tc, tiles_per_batch=n_ctx // tc, mod_row=batch, widths=widths)
    else:
        xc2d = None
    wg, wu, wd = w["w_ffn_gate"], w["w_ffn_up"], w["w_ffn_down"]
    return _ffn(x2d, xc2d, mod3, p["g_pre_ffn"], p["g_post_ffn"], wg, wu, wd,
                layer=layer, tm=tm2, tiles_per_batch=seq // tm2, ctx_row=batch)


def kernel(x, c, ctx, c_ctx, w_mod, b_mod, g_pre_mix, g_post_mix, w_in, ssm_a_re, ssm_a_im, ssm_log_dt,
           ssm_b_re, ssm_b_im, ssm_c_re, ssm_c_im, ssm_d, w_glu, w_fourier, na_rpb, w_out, g_pre_ffn,
           g_post_ffn, w_ffn_gate, w_ffn_up, w_ffn_down):
    batch, seq, d = x.shape
    n_ctx = ctx.shape[1]
    depth = w_mod.shape[0]
    w_fn = w_fourier.shape[1]
    params = dict(g_pre_mix=g_pre_mix, g_post_mix=g_post_mix, g_pre_ffn=g_pre_ffn, g_post_ffn=g_post_ffn)
    seg_len = (seq + 2 * n_ctx) // SSM_T // SSM_SEG
    ssm_ops = jax.vmap(functools.partial(_ssm_weights, seg_len=seg_len))(
        ssm_a_re, ssm_a_im, ssm_log_dt, ssm_b_re, ssm_b_im, ssm_c_re, ssm_c_im, ssm_d)
    nrow = -(-(batch + 1) // 8) * 8
    cstack = jnp.concatenate([c, c_ctx[None, :], jnp.zeros((nrow - batch - 1, d), c.dtype)], axis=0)
    weights = dict(mod=_mod_rows(cstack, w_mod, b_mod), ssm=ssm_ops,
                   w_in=w_in, w_glu=w_glu, w_fourier=w_fourier, w_out=w_out,
                   w_ffn_gate=w_ffn_gate,
                   w_ffn_up=w_ffn_up, w_ffn_down=w_ffn_down)

    l2 = 64 if seq % (64 * 8) == 0 else 8
    l1 = seq // l2
    consts = dict(
        l2=l2,
        w_ssm=ssm_d.shape[1],
        na=_na_tables(na_rpb, seq, 4),
        dft_a=_dft_rows(l2).astype(BF),
        dft_b=_dft_b_mats(seq, l1, l2).astype(BF),
        dft_ctx=_dft_rows(n_ctx).astype(BF),
        cs_lat=_channel_dft(w_fn, seq).astype(BF),
        cs_ctx=_channel_dft(w_fn, n_ctx).astype(BF),
    )
    rope_tabs = _rope_tables(seq)

    x2d = x.reshape(batch * seq, d)
    xc2d = ctx.reshape(batch * n_ctx, d)
    for layer in range(depth):
        x2d, xc2d = _layer(x2d, xc2d, params, weights, layer, batch=batch, seq=seq, n_ctx=n_ctx,
                           last=(layer == depth - 1), rope_tabs=rope_tabs, consts=consts)
    return x2d.reshape(batch, seq, d)
```

```python
import functools
import math

import numpy as np
import jax
import jax.numpy as jnp
from jax import lax
from jax.experimental import pallas as pl
from jax.experimental.pallas import tpu as pltpu

BF = jnp.bfloat16
F32 = jnp.float32

EPS = 1e-6
GRID_W = 64
HEAD_DIM = 64
NA_ROWS = 8
NA_COLS = 16
ROPE_BASE = 10000.0
SSM_GROUP = 16
SSM_STATE = 64
FNET_GROUP = 64
LANES = 128
MXU_TILE = 256
SSM_T = 8
SSM_SEG = 8
SSM_STAGES = (4, 2)
NEG = -1e30
VMEM_LIMIT = 56 * 1024 * 1024
VMEM_LIMIT_NA = 60000 * 1024


def _cparams(*sem, vmem=VMEM_LIMIT):
    return pltpu.CompilerParams(dimension_semantics=sem, vmem_limit_bytes=vmem)


def _resident(shape, index_map):
    return pl.BlockSpec(shape, index_map, pipeline_mode=pl.Buffered(1))


def _layer_resident(w_all, layer):
    return pl.BlockSpec((None,) + w_all.shape[1:], lambda *_: (layer, 0, 0), pipeline_mode=pl.Buffered(1))


def _mod_kernel(c_ref, w_ref, b_ref, o_ref):
    cs = c_ref[...]
    s = (cs * jax.nn.sigmoid(cs)).astype(BF)
    o_ref[...] = jnp.dot(s, w_ref[...].astype(BF), preferred_element_type=F32) + b_ref[...]


def _mod_rows(cstack, w_mod, b_mod):
    rows, d = cstack.shape
    depth, _, n = w_mod.shape
    bn = 2 * d if n % (2 * d) == 0 else d
    return pl.pallas_call(
        _mod_kernel,
        grid=(depth, n // bn),
        in_specs=[pl.BlockSpec((rows, d), lambda l, i: (0, 0)),
                  pl.BlockSpec((None, d, bn), lambda l, i: (l, 0, i)),
                  pl.BlockSpec((None, 1, bn), lambda l, i: (l, 0, i))],
        out_specs=pl.BlockSpec((None, rows, bn), lambda l, i: (l, 0, i)),
        out_shape=jax.ShapeDtypeStruct((depth, rows, n), F32),
        compiler_params=_cparams("parallel", "parallel"),
        name="mod_rows",
    )(cstack, w_mod, b_mod.reshape(depth, 1, n))


def _rope_tile(t, cos, sin, first):
    partner = jnp.where(first, pltpu.roll(t, LANES - 16, 1), pltpu.roll(t, 16, 1))
    return t * cos + partner * sin


def _inproj_kernel(x_ref, sh_ref, sc_ref, g_ref, w_ref, cos_ref, sin_ref,
                   u_ref, f_ref, q_ref, k_ref, v_ref, *, widths, rope, sub):
    w_ssm, w_fn, w_na = widths
    scale = HEAD_DIM ** -0.5 * math.log2(math.e)
    w = w_ref[...].astype(BF)
    for r0 in range(0, x_ref.shape[0], sub):
        rs = slice(r0, r0 + sub)
        x = x_ref[rs, :]
        ms = jnp.mean(x * x, axis=-1, keepdims=True)
        xn = x * lax.rsqrt(ms + EPS) * g_ref[...]
        m = (xn * (1.0 + sc_ref[...]) + sh_ref[...]).astype(BF)

        h = jnp.dot(m, w, preferred_element_type=F32)

        def proj(lo, n, h=h):
            return h[:, lo:lo + n]

        u = proj(0, w_ssm)
        for j in range(w_ssm // LANES):
            u_ref[j, rs, :] = u[:, j * LANES:(j + 1) * LANES]
        f_ref[rs, :] = proj(w_ssm, w_fn).astype(BF)
        q = proj(w_ssm + w_fn, w_na)
        k = proj(w_ssm + w_fn + w_na, w_na)
        v_ref[rs, :] = proj(w_ssm + w_fn + 2 * w_na, w_na).astype(BF)
        if rope:
            cos = cos_ref[rs, :]
            sin = sin_ref[rs, :]
            lane = lax.broadcasted_iota(jnp.int32, cos.shape, 1)
            first = (lane % 32) < 16
            for j in range(w_na // LANES):
                sl = slice(j * LANES, (j + 1) * LANES)
                q_ref[rs, sl] = (_rope_tile(q[:, sl], cos, sin, first) * scale).astype(BF)
                k_ref[rs, sl] = _rope_tile(k[:, sl], cos, sin, first).astype(BF)
        else:
            q_ref[rs, :] = (q * scale).astype(BF)
            k_ref[rs, :] = k.astype(BF)


def _inproj_pair_kernel(x_ref, xc_ref, sh_ref, sc_ref, g_ref, w_ref, cos_ref, sin_ref, *outs,
                        widths, sub, n_lat_tiles):
    common = (sh_ref, sc_ref, g_ref, w_ref, cos_ref, sin_ref)
    i = pl.program_id(0)

    @pl.when(i < n_lat_tiles)
    def _latent():
        _inproj_kernel(x_ref, *common, *outs[:5], widths=widths, rope=True, sub=sub)

    @pl.when(i >= n_lat_tiles)
    def _context():
        _inproj_kernel(xc_ref, *common, *outs[5:], widths=widths, rope=False, sub=min(sub, xc_ref.shape[0]))


def _inproj(x2d, xc2d, mod3, g_pre, w_in, cos, sin, *, layer, tm, tiles_per_batch, ctx_row, widths):
    n, d = x2d.shape
    nc = xc2d.shape[0]
    w_ssm, w_fn, w_na = widths
    nl = n // tm
    tile = lambda i: jnp.minimum(i, nl - 1)
    row = lambda i: jnp.where(i < nl, tile(i) // tiles_per_batch, ctx_row)
    pos = lambda i: (tile(i) % tiles_per_batch, 0)
    lat = lambda i: (tile(i), 0)
    one = lambda i: (0, 0)

    def outs(rows, idx, idx3):
        specs = [pl.BlockSpec((w_ssm // LANES, rows, LANES), idx3), pl.BlockSpec((rows, w_fn), idx)]
        specs += [pl.BlockSpec((rows, w_na), idx)] * 3
        return specs

    def shapes(rows):
        return ([jax.ShapeDtypeStruct((w_ssm // LANES, rows, LANES), F32), jax.ShapeDtypeStruct((rows, w_fn), BF)]
                + [jax.ShapeDtypeStruct((rows, w_na), BF)] * 3)

    kern = functools.partial(_inproj_pair_kernel, widths=widths, sub=min(tm, 512), n_lat_tiles=nl)
    res = pl.pallas_call(
        kern,
        grid=(nl + 1,),
        in_specs=[pl.BlockSpec((tm, d), lat),
                  pl.BlockSpec((nc, d), one),
                  pl.BlockSpec((None, 1, d), lambda i: (row(i), 0, 0)),
                  pl.BlockSpec((None, 1, d), lambda i: (row(i), 0, 1)),
                  pl.BlockSpec((None, 1, d), lambda i: (layer, 0, 0)),
                  _layer_resident(w_in, layer),
                  pl.BlockSpec((tm, LANES), pos),
                  pl.BlockSpec((tm, LANES), pos)],
        out_specs=outs(tm, lat, lambda i: (0, tile(i), 0)) + outs(nc, one, lambda i: (0, 0, 0)),
        out_shape=shapes(n) + shapes(nc),
        compiler_params=_cparams("arbitrary"),
        name="inproj",
    )(x2d, xc2d, mod3, mod3, g_pre.reshape(-1, 1, d), w_in, cos, sin)
    return res[:5], res[5:]


def _rope_tables(seq):
    t = np.arange(seq)
    row = (t // GRID_W).astype(np.float32)
    col = (t % GRID_W).astype(np.float32)
    quarter = HEAD_DIM // 4
    freqs = (np.float32(ROPE_BASE) ** (-np.arange(quarter, dtype=np.float32) / quarter)).astype(np.float32)
    d = np.arange(LANES) % HEAD_DIM
    use_col = (d // (HEAD_DIM // 2)) == 1
    fidx = d % quarter
    sign = np.where((d % (HEAD_DIM // 2)) // quarter == 0, -1.0, 1.0)
    pos = np.where(use_col[None, :], col[:, None], row[:, None])
    ang = (pos * freqs[fidx][None, :]).astype(np.float32).astype(np.float64)
    return jnp.asarray(np.cos(ang), F32), jnp.asarray(np.sin(ang) * sign[None, :], F32)


def _gelu_tanh(x):
    c = math.sqrt(2.0 / math.pi)
    return x * (0.5 * (1.0 + jnp.tanh(c * (x + 0.044715 * (x * x * x)))))


def _group_of(idx, width, groups):
    shift = width.bit_length() - 1
    assert width == 1 << shift and groups & (groups - 1) == 0
    return lax.bitwise_and(lax.shift_right_logical(idx, shift), groups - 1)


def _same_group(shape, row0, row_width, col0, col_width, groups):
    r = lax.broadcasted_iota(jnp.int32, shape, 0) + row0
    c = lax.broadcasted_iota(jnp.int32, shape, 1) + col0
    return _group_of(r, row_width, groups) == _group_of(c, col_width, groups)


def _swap_stage(tiles, d):
    lane = lax.broadcasted_iota(jnp.int32, tiles[0].shape, 1)
    upper = lax.bitwise_and(lax.shift_right_logical(lane, SSM_GROUP.bit_length() - 1), d) != 0
    out = list(tiles)
    for x in range(len(tiles)):
        if x & d:
            continue
        a, b = tiles[x], tiles[x + d]
        out[x] = jnp.where(upper, pltpu.roll(b, SSM_GROUP * d, 1), a)
        out[x + d] = jnp.where(upper, b, pltpu.roll(a, LANES - SSM_GROUP * d, 1))
    return out


def _ssm_kernel(ul_ref, uc_ref, wc_ref, lag_ref, dc_ref, rexp_ref, cexp_ref, lam_ref, lamseg_ref,
                z_ref, zc_ref, u_s, bm_s, kin_s, cm_s, s_ref, h_ref, e_ref, *, seg_len, n_lat, n_ctx, stages):
    T = SSM_T
    ns = 2 ** len(stages)
    gs = T // ns
    sw = gs * LANES
    rows = u_s.shape[1]
    tps = s_ref.shape[0] // ns
    tpq = tps // 4
    nst = tpq * LANES
    gpt = LANES // SSM_GROUP
    per_q = gpt * SSM_STATE

    def lanes(c):
        return slice(c * LANES, (c + 1) * LANES)

    def col0(s, q):
        return q * per_q + s * nst

    def swapped(tiles, order):
        for d in order:
            tiles = _swap_stage(tiles, d)
        return tiles

    @pl.when(pl.program_id(1) == 0)
    def _expand_operators():
        rexp = rexp_ref[...]
        cexp = cexp_ref[...]
        for s in range(ns):
            wsel = jnp.concatenate([wc_ref[:, col0(s, q):col0(s, q) + nst] for q in range(4)], axis=1)
            blk = jnp.dot(rexp, wsel, preferred_element_type=F32)
            keep = _same_group(blk.shape, 0, SSM_GROUP, 0, SSM_STATE, gs)
            bm_s[s] = jnp.where(keep, blk, 0.0).astype(BF)
            dsel = jnp.concatenate([dc_ref[col0(s, q):col0(s, q) + nst, :] for q in range(4)], axis=0)
            blk = jnp.dot(dsel, cexp, preferred_element_type=F32)
            keep = _same_group(blk.shape, 0, SSM_STATE, 0, SSM_GROUP, gs)
            cm_s[s] = jnp.where(keep, blk, 0.0).astype(BF)
        r16 = lax.broadcasted_iota(jnp.int32, (LANES, SSM_GROUP), 0)
        c16 = lax.broadcasted_iota(jnp.int32, (LANES, SSM_GROUP), 1)
        chan = jnp.where(lax.bitwise_and(r16, SSM_GROUP - 1) == c16, 1.0, 0.0).astype(BF)
        for t in range(T):
            lo = (T - 1 - t) * LANES
            blk = jnp.dot(chan, lag_ref[:, lo:lo + T * LANES], preferred_element_type=F32)
            keep = _same_group(blk.shape, 0, SSM_GROUP, 0, SSM_GROUP, gpt)
            blk = jnp.where(keep, blk, 0.0)
            tiles = swapped([blk[:, lanes(x)] for x in range(T)], stages)
            r0 = (t % gs) * LANES + (t // gs) * gs * SSM_GROUP
            for s in range(ns):
                part = jnp.concatenate(tiles[s * gs:(s + 1) * gs], axis=1)
                kin_s[s, r0:r0 + gs * SSM_GROUP, :] = part[s * gs * SSM_GROUP:(s + 1) * gs * SSM_GROUP, :].astype(BF)

    xs = []
    for t in range(T):
        ct = uc_ref[pl.ds(t, n_ctx, stride=T), :]
        xs.append(jnp.concatenate([ct, ul_ref[pl.ds(t, n_lat, stride=T), :], ct], axis=0))
    xs = swapped(xs, stages)
    for s in range(ns):
        for k in range(gs):
            u_s[s, :, lanes(k)] = xs[s * gs + k].astype(BF)
    for s in range(ns):
        r = jnp.dot(u_s[s], bm_s[s], preferred_element_type=F32)
        for c in range(tps):
            s_ref[s * tps + c] = r[:, lanes(c)]

    chains = []
    for s in range(ns):
        for d in range(2):
            for k in range(tpq):
                cr, ci = s * tps + 2 * d * tpq + k, s * tps + (2 * d + 1) * tpq + k
                lr = slice(col0(s, 2 * d) + k * LANES, col0(s, 2 * d) + (k + 1) * LANES)
                li = slice(col0(s, 2 * d + 1) + k * LANES, col0(s, 2 * d + 1) + (k + 1) * LANES)
                chains.append((cr, ci, d, lr, li))
    nt = ns * tps

    def scan_step(i, carry, record):
        new = list(carry)
        for cr, ci, d, lr_sl, li_sl in chains:
            rsl = pl.ds(i if d == 0 else seg_len - 1 - i, SSM_SEG, stride=seg_len)
            hr, hi = carry[cr], carry[ci]
            if record:
                h_ref[cr, rsl, :] = hr
                h_ref[ci, rsl, :] = hi
            lr, li = lam_ref[:, lr_sl], lam_ref[:, li_sl]
            new[cr] = lr * hr - li * hi + s_ref[cr, rsl, :]
            new[ci] = lr * hi + li * hr + s_ref[ci, rsl, :]
        return tuple(new)

    zero = jnp.zeros((SSM_SEG, LANES), F32)
    fin = lax.fori_loop(0, seg_len, lambda i, c: scan_step(i, c, False), (zero,) * nt, unroll=4)
    for c in range(nt):
        e_ref[c, 0:SSM_SEG, :] = fin[c]

    for cr, ci, d, lr_sl, li_sl in chains:
        lr, li = lamseg_ref[:, lr_sl], lamseg_ref[:, li_sl]
        er = jnp.zeros((1, LANES), F32)
        ei = jnp.zeros((1, LANES), F32)
        for n in range(SSM_SEG):
            g = n if d == 0 else SSM_SEG - 1 - n
            e_ref[cr, SSM_SEG + g:SSM_SEG + g + 1, :] = er
            e_ref[ci, SSM_SEG + g:SSM_SEG + g + 1, :] = ei
            fr, fi = e_ref[cr, g:g + 1, :], e_ref[ci, g:g + 1, :]
            er, ei = lr * er - li * ei + fr, lr * ei + li * er + fi

    ent = tuple(e_ref[c, SSM_SEG:2 * SSM_SEG, :] for c in range(nt))
    lax.fori_loop(0, seg_len, lambda i, c: scan_step(i, c, True), ent, unroll=2)

    ys, ycs = [], []
    for s in range(ns):
        hcat = jnp.concatenate([h_ref[s * tps + c].astype(BF) for c in range(tps)], axis=1)
        y = _gelu_tanh(jnp.dot(u_s[s], kin_s[s], preferred_element_type=F32)
                       + jnp.dot(hcat, cm_s[s], preferred_element_type=F32))
        ys += [y[:, lanes(k)] for k in range(gs)]
        hc = jnp.concatenate(
            [h_ref[s * tps + c, 0:n_ctx, :].astype(BF) for c in range(2 * tpq)]
            + [h_ref[s * tps + c, rows - n_ctx:rows, :].astype(BF) for c in range(2 * tpq, tps)], axis=1)
        yc = _gelu_tanh(jnp.dot(u_s[s, 0:n_ctx, :], kin_s[s], preferred_element_type=F32)
                        + jnp.dot(hc, cm_s[s], preferred_element_type=F32))
        ycs += [yc[:, lanes(k)] for k in range(gs)]
    ys = swapped(ys, stages[::-1])
    ycs = swapped(ycs, stages[::-1])
    for t in range(T):
        z_ref[pl.ds(t, n_lat, stride=T), :] = ys[t][n_ctx:n_ctx + n_lat, :]
        zc_ref[pl.ds(t, n_ctx, stride=T), :] = ycs[t]


def _ssm_weights(a_re, a_im, log_dt, b_re, b_im, c_re, c_im, d_skip, seg_len):
    T = SSM_T
    g, p = a_re.shape[1], a_re.shape[2]
    hc = b_re.shape[-1]
    gpt = LANES // hc
    J = g // gpt
    lam = lax.complex(a_re.astype(F32), a_im.astype(F32))
    dt = jnp.exp(log_dt.astype(F32))[..., None]
    ldt = lam * dt
    lam_bar = jnp.exp(ldt)
    bbar = ((lam_bar - 1) / lam)[..., None] * lax.complex(b_re.astype(F32), b_im.astype(F32))
    cmat = lax.complex(c_re.astype(F32), c_im.astype(F32))
    kk = jnp.arange(T + 1, dtype=F32)
    pw = jnp.exp(ldt[:, None] * kk[None, :, None, None])

    wf = pw[0, :T][::-1][:, :, None, :] * jnp.swapaxes(bbar[0], 1, 2)[None]
    wb = pw[1, :T][:, :, None, :] * jnp.swapaxes(bbar[1], 1, 2)[None]
    wq = jnp.stack([wf.real, wf.imag, wb.real, wb.imag]).reshape(4, T, J, gpt, hc, p)
    wc = wq.transpose(2, 1, 4, 0, 3, 5).reshape(J, T * hc, 4 * gpt * p)

    df = cmat[0][None] * pw[0, 1:][:, :, None, :]
    db = cmat[1][None] * pw[1, 1:][::-1][:, :, None, :]
    dq = jnp.stack([df.real, -df.imag, db.real, -db.imag]).reshape(4, T, J, gpt, hc, p)
    dc = dq.transpose(2, 0, 3, 5, 1, 4).reshape(J, 4 * gpt * p, T * hc)

    mf = jnp.einsum('gcp,kgp,gph->khgc', cmat[0], pw[0, :T], bbar[0]).real
    mb = jnp.einsum('gcp,kgp,gph->khgc', cmat[1], pw[1, :T], bbar[1]).real
    skip = jnp.eye(hc, dtype=F32)[:, None, :] * d_skip.astype(F32).reshape(g, hc)[None]
    m0 = mf[0] + mb[0] + skip
    lags = jnp.concatenate([mb[1:][::-1], m0[None], mf[1:]], axis=0)
    kc = lags.reshape(2 * T - 1, hc, J, LANES).transpose(2, 1, 0, 3).reshape(J, hc, (2 * T - 1) * LANES)

    def lam_rows(power):
        lp = jnp.exp(ldt * power)
        lt = lp.reshape(2, J, gpt * p)
        return jnp.concatenate([lt[0].real, lt[0].imag, lt[1].real, lt[1].imag], axis=-1)[:, None, :]

    lam_t = jnp.broadcast_to(lam_rows(float(T)), (J, SSM_SEG, 4 * gpt * p))
    return wc.astype(BF), kc.astype(BF), dc.astype(BF), lam_t, lam_rows(float(T * seg_len))


def _ssm_expanders(gs):
    r = np.arange(gs * LANES)
    k, slot, ch = r // LANES, (r % LANES) // SSM_GROUP, r % SSM_GROUP
    t = (slot // gs) * gs + k
    rexp = np.zeros((gs * LANES, SSM_T * SSM_GROUP), np.float32)
    rexp[r, t * SSM_GROUP + ch] = 1.0
    return jnp.asarray(rexp, BF), jnp.asarray(rexp.T, BF)


def _ssm(u3, u3c, wc, kc, dc, lam, lamseg, *, layer, batch, seq, n_ctx):
    T = SSM_T
    J = u3.shape[0]
    ns = wc.shape[-1]
    nsets = 2 ** len(SSM_STAGES)
    gs = T // nsets
    sw = gs * LANES
    n_lat, n_c = seq // T, n_ctx // T
    rows = n_lat + 2 * n_c
    seg_len = rows // SSM_SEG
    assert rows % SSM_SEG == 0 and n_c % 16 == 0
    rexp, cexp = _ssm_expanders(gs)
    kern = functools.partial(_ssm_kernel, seg_len=seg_len, n_lat=n_lat, n_ctx=n_c, stages=SSM_STAGES)
    const = lambda j, b: (0, 0)
    per_j = lambda j, b: (layer, j, 0, 0)
    per_jb = lambda j, b: (j, b, 0)
    return pl.pallas_call(
        kern,
        grid=(J, batch),
        in_specs=[pl.BlockSpec((None, seq, LANES), per_jb),
                  pl.BlockSpec((None, n_ctx, LANES), per_jb),
                  pl.BlockSpec((None, None) + wc.shape[2:], per_j),
                  pl.BlockSpec((None, None) + kc.shape[2:], per_j),
                  pl.BlockSpec((None, None) + dc.shape[2:], per_j),
                  pl.BlockSpec(rexp.shape, const),
                  pl.BlockSpec(cexp.shape, const),
                  pl.BlockSpec((None, None, SSM_SEG, ns), per_j),
                  pl.BlockSpec((None, None, 1, ns), per_j)],
        out_specs=[pl.BlockSpec((None, seq, LANES), per_jb),
                   pl.BlockSpec((None, n_ctx, LANES), per_jb)],
        out_shape=[jax.ShapeDtypeStruct(u3.shape, F32),
                   jax.ShapeDtypeStruct(u3c.shape, F32)],
        scratch_shapes=[pltpu.VMEM((nsets, rows, sw), BF),
                        pltpu.VMEM((nsets, sw, ns // nsets), BF),
                        pltpu.VMEM((nsets, sw, sw), BF),
                        pltpu.VMEM((nsets, ns // nsets, sw), BF),
                        pltpu.VMEM((ns // LANES, rows, LANES), F32),
                        pltpu.VMEM((ns // LANES, rows, LANES), F32),
                        pltpu.VMEM((ns // LANES, 2 * SSM_SEG, LANES), F32)],
        compiler_params=_cparams("arbitrary", "arbitrary"),
        name="ssm_chunk_scan",
    )(u3, u3c, wc, kc, dc, rexp, cexp, lam, lamseg)


def _dft_a_kernel(w_ref, x_ref, o_ref):
    o_ref[...] = jnp.dot(w_ref[...], x_ref[...], preferred_element_type=F32).astype(BF)


def _dft_rows(n):
    k = np.arange(n)
    m = (k[:, None] * k[None, :]) % n
    ang = m * (2.0 * math.pi / n)
    return jnp.asarray(np.concatenate([np.cos(ang), -np.sin(ang)], axis=0), BF)


def _dft_a(w, x3):
    B, n, cols = x3.shape
    tn = min(cols, 16384)
    return pl.pallas_call(
        _dft_a_kernel,
        grid=(B, cols // tn),
        in_specs=[pl.BlockSpec((2 * n, n), lambda b, i: (0, 0)),
                  pl.BlockSpec((None, n, tn), lambda b, i: (b, 0, i))],
        out_specs=pl.BlockSpec((None, 2 * n, tn), lambda b, i: (b, 0, i)),
        out_shape=jax.ShapeDtypeStruct((B, 2 * n, cols), BF),
        compiler_params=_cparams("parallel", "parallel"),
        name="dft_stage_a",
    )(w, x3)


def _store_lane_tiles(o_ref, p, n, wf, idx):
    per = wf // LANES
    for c in range(2 * per):
        part = p[(c // per) * n:(c // per + 1) * n, (c % per) * LANES:(c % per + 1) * LANES]
        o_ref[(c,) + idx] = part


def _dft_b_kernel(a_ref, g_ref, o_ref, *, kb, l1, wf):
    for i in range(kb):
        g = jnp.concatenate([g_ref[0, i], g_ref[1, i]], axis=0)
        p = jnp.dot(a_ref[i], g, preferred_element_type=F32)
        _store_lane_tiles(o_ref, p, l1, wf, (i,))


def _dft_b_mats(seq, l1, l2):
    k2 = np.arange(l2)[:, None, None]
    k1 = np.arange(l1)[None, :, None]
    j1 = np.arange(l1)[None, None, :]
    m = ((k1 * l2 + k2) * j1) % seq
    ang = m * (2.0 * math.pi / seq)
    ar, ai = np.cos(ang).astype(np.float32), -np.sin(ang).astype(np.float32)
    top = np.concatenate([ar, -ai], axis=2)
    bot = np.concatenate([ai, ar], axis=2)
    return jnp.asarray(np.concatenate([top, bot], axis=1), BF)


def _dft_b(amat, g5, *, kb):
    B, _, l2, l1, wf = g5.shape
    nlt = 2 * wf // LANES
    kern = functools.partial(_dft_b_kernel, kb=kb, l1=l1, wf=wf)
    return pl.pallas_call(
        kern,
        grid=(B, l2 // kb),
        in_specs=[pl.BlockSpec((kb, 2 * l1, 2 * l1), lambda b, i: (i, 0, 0)),
                  pl.BlockSpec((None, 2, kb, l1, wf), lambda b, i: (b, 0, i, 0, 0))],
        out_specs=pl.BlockSpec((None, nlt, kb, l1, LANES), lambda b, i: (b, 0, i, 0, 0)),
        out_shape=jax.ShapeDtypeStruct((B, nlt, l2, l1, LANES), F32),
        compiler_params=_cparams("parallel", "parallel"),
        name="dft_stage_b",
    )(amat, g5)


def _dft_ctx_kernel(w_ref, x_ref, o_ref, *, n, wf):
    p = jnp.dot(w_ref[...], x_ref[...], preferred_element_type=F32)
    _store_lane_tiles(o_ref, p, n, wf, ())


def _dft_ctx(w, x3):
    B, n, wf = x3.shape
    nlt = 2 * wf // LANES
    kern = functools.partial(_dft_ctx_kernel, n=n, wf=wf)
    return pl.pallas_call(
        kern,
        grid=(B,),
        in_specs=[pl.BlockSpec((2 * n, n), lambda b: (0, 0)),
                  pl.BlockSpec((None, n, wf), lambda b: (b, 0, 0))],
        out_specs=pl.BlockSpec((None, nlt, n, LANES), lambda b: (b, 0, 0, 0)),
        out_shape=jax.ShapeDtypeStruct((B, nlt, n, LANES), F32),
        compiler_params=_cparams("parallel"),
        name="dft_ctx",
    )(w, x3)


def _channel_dft(wf, seq):
    c = np.arange(wf)
    same = (c[:, None] // FNET_GROUP) == (c[None, :] // FNET_GROUP)
    m = ((c[:, None] % FNET_GROUP) * (c[None, :] % FNET_GROUP)) % FNET_GROUP
    ang = m * (2.0 * math.pi / FNET_GROUP)
    mask = same / math.sqrt(seq * FNET_GROUP)
    return jnp.asarray(np.concatenate([np.cos(ang) * mask, np.sin(ang) * mask], axis=0), BF)


def _softmax_pv(parts, l_shape):
    m = parts[0][0].max(axis=-1, keepdims=True)
    for s, _ in parts[1:]:
        m = jnp.maximum(m, s.max(axis=-1, keepdims=True))
    l = jnp.zeros(l_shape, F32)
    o = None
    for s, v in parts:
        p = jnp.exp2(s - m)
        l = l + p.sum(axis=-1, keepdims=True)
        pv = jnp.dot(p.astype(BF), v, preferred_element_type=F32)
        o = pv if o is None else o + pv
    return o / l


def _na_kernel(kb_ref, pid_ref, dr_ref, q_ref, k_ref, v_ref, kc_ref, vc_ref, cbx_ref, o_ref, bias_ref,
               *, nkeys, heads, nsub):
    step = pl.program_id(1)
    nq = q_ref.shape[0] // nsub
    rq, kr = nq // GRID_W, nkeys // GRID_W
    lane = lax.broadcasted_iota(jnp.int32, (nq, LANES), 1)
    low = lane < HEAD_DIM
    dn = (((1,), (1,)), ((), ()))

    @pl.when((pl.program_id(0) == 0) & (step == 0))
    def _build_bias():
        for pat in range(bias_ref.shape[0]):
            for qi in range(rq):
                for w in range(kr):
                    a = dr_ref[(pat * rq + qi) * kr + w]
                    for h in range(heads):
                        bias_ref[pat, h, qi * GRID_W:(qi + 1) * GRID_W, w * GRID_W:(w + 1) * GRID_W] = cbx_ref[h, a]

    tasks = [(sb, h) for sb in range(nsub) for h in range(heads)]
    starts = [pl.multiple_of(kb_ref[step * nsub + sb] * GRID_W, GRID_W) for sb in range(nsub)]
    pats = [pid_ref[step * nsub + sb] for sb in range(nsub)]

    n_ctx = kc_ref.shape[0]
    ones = jnp.ones((n_ctx + nkeys, LANES), BF)
    kv = {}

    def keys_values(sb, hp):
        if (sb, hp) not in kv:
            sl = slice(hp * LANES, (hp + 1) * LANES)
            win = pl.ds(starts[sb], nkeys)
            k_all = jnp.concatenate([kc_ref[:, sl], k_ref[win, sl]], axis=0)
            v_all = jnp.concatenate([jnp.concatenate([vc_ref[:, sl], v_ref[win, sl]], axis=0), ones], axis=1)
            kv[sb, hp] = (k_all, v_all)
        return kv[sb, hp]

    def scores(sb, h):
        sl = slice((h // 2) * LANES, (h // 2 + 1) * LANES)
        q2 = q_ref[sb * nq:(sb + 1) * nq, sl]
        qm = jnp.where(low if h % 2 == 0 else jnp.logical_not(low), q2, jnp.zeros_like(q2))
        s = lax.dot_general(qm, keys_values(sb, h // 2)[0], dn, preferred_element_type=F32)
        return s[:, n_ctx:] + bias_ref[pats[sb], h], s[:, :n_ctx]

    def weights(s_w, s_c):
        m = jnp.maximum(s_w.max(axis=-1, keepdims=True), s_c.max(axis=-1, keepdims=True))
        return (jnp.concatenate([jnp.exp2(s_c - m).astype(BF), jnp.exp2(s_w - m).astype(BF)], axis=1),)

    outs = {}

    def finish(sb, h, p):
        sl = slice((h // 2) * LANES, (h // 2 + 1) * LANES)
        ox = jnp.dot(p, keys_values(sb, h // 2)[1], preferred_element_type=F32)
        outs[h % 2] = ox[:, :LANES] / ox[:, LANES:]
        if h % 2 == 1:
            o_ref[sb * nq:(sb + 1) * nq, sl] = jnp.where(low, outs[0], outs[1]).astype(BF)

    nt = len(tasks)
    sc = {0: scores(*tasks[0])}
    if nt > 1:
        sc[1] = scores(*tasks[1])
    pw = {0: weights(*sc.pop(0))}
    for n in range(nt):
        if n + 2 < nt:
            sc[n + 2] = scores(*tasks[n + 2])
        if n + 1 < nt:
            pw[n + 1] = weights(*sc.pop(n + 1))
        finish(*tasks[n], *pw.pop(n))


def _na_plan(rows, rq):
    wr = min(NA_ROWS, rows)
    kr = rq + wr - 1
    nblk = rows // rq
    kbs, pids, pats = [], [], []
    for blk in range(nblk):
        r0 = blk * rq
        rs = [int(np.clip(r0 + i - wr // 2, 0, rows - wr)) for i in range(rq)]
        kb = min(rs[0], rows - kr)
        pat = (tuple(r - kb for r in rs), r0 - kb)
        if pat not in pats:
            pats.append(pat)
        kbs.append(kb)
        pids.append(pats.index(pat))
    return np.asarray(kbs, np.int32), np.asarray(pids, np.int32), pats, kr, wr


def _na_bias(rpb, pats, rq, kr, wr):
    cols = np.arange(GRID_W)
    cstart = np.clip(cols - NA_COLS // 2, 0, GRID_W - NA_COLS)
    kc = np.arange(GRID_W)
    col_ok = (kc[None, :] >= cstart[:, None]) & (kc[None, :] < cstart[:, None] + NA_COLS)
    dc = kc[None, :] - cols[:, None] + NA_COLS - 1
    depth, heads, nr, ncol = rpb.shape
    sel = (dc[:, :, None] == np.arange(ncol)[None, None, :]) & col_ok[:, :, None]
    cb = jnp.einsum('lhab,ckb->lhack', rpb.astype(F32), jnp.asarray(sel, F32), precision=lax.Precision.HIGHEST)
    cb = jnp.where(jnp.asarray(col_ok)[None, None, None], cb * math.log2(math.e), NEG)
    cbx = jnp.concatenate([cb, jnp.full((depth, heads, 1, GRID_W, GRID_W), NEG, F32)], axis=2)
    slots = []
    for rs_off, r_off in pats:
        i = np.arange(rq)[:, None]
        w = np.arange(kr)[None, :]
        rso = np.asarray(rs_off)[:, None]
        row_ok = (w >= rso) & (w < rso + wr)
        slots.append(np.where(row_ok, w - r_off - i + NA_ROWS - 1, nr))
    return cbx, np.stack(slots).reshape(-1).astype(np.int32)


def _na_tables(rpb_all, seq, rq):
    kbs, pids, pats, kr, wr = _na_plan(seq // GRID_W, rq)
    cbx, slots = _na_bias(rpb_all, pats, rq, kr, wr)
    return dict(kbs=jnp.asarray(kbs), pids=jnp.asarray(pids), slots=jnp.asarray(slots), cbx=cbx, kr=kr, rq=rq,
                npat=len(pats))


def _na_latent(q, k, v, kc, vc, tabs, layer, *, batch, seq, n_ctx):
    rows = seq // GRID_W
    cbx, rq, kr = tabs["cbx"], tabs["rq"], tabs["kr"]
    heads = cbx.shape[1]
    nblk = rows // rq
    nq = rq * GRID_W
    nkeys = kr * GRID_W
    w = q.shape[1]
    nsub = 4 if nblk % 4 == 0 else 1
    nstep = nblk // nsub
    kern = functools.partial(_na_kernel, nkeys=nkeys, heads=heads, nsub=nsub)
    grid_spec = pltpu.PrefetchScalarGridSpec(
        num_scalar_prefetch=3,
        grid=(batch, nstep),
        in_specs=[pl.BlockSpec((nsub * nq, w), lambda b, i, *_: (b * nstep + i, 0)),
                  pl.BlockSpec((seq, w), lambda b, i, *_: (b, 0)),
                  pl.BlockSpec((seq, w), lambda b, i, *_: (b, 0)),
                  pl.BlockSpec((n_ctx, w), lambda b, i, *_: (b, 0)),
                  pl.BlockSpec((n_ctx, w), lambda b, i, *_: (b, 0)),
                  pl.BlockSpec((None,) + cbx.shape[1:], lambda b, i, *_: (layer, 0, 0, 0, 0))],
        out_specs=pl.BlockSpec((nsub * nq, w), lambda b, i, *_: (b * nstep + i, 0)),
        scratch_shapes=[pltpu.VMEM((tabs["npat"], heads, nq, nkeys), F32)],
    )
    return pl.pallas_call(
        kern,
        grid_spec=grid_spec,
        out_shape=jax.ShapeDtypeStruct(q.shape, BF),
        compiler_params=_cparams("arbitrary", "arbitrary", vmem=VMEM_LIMIT_NA),
        name="na_latent",
    )(tabs["kbs"], tabs["pids"], tabs["slots"], q, k, v, kc, vc, cbx)


def _na_ctx_kernel(q_ref, k_ref, v_ref, o_ref, *, heads):
    nq = q_ref.shape[0]
    lane = lax.broadcasted_iota(jnp.int32, (nq, LANES), 1)
    low = lane < HEAD_DIM
    dn = (((1,), (1,)), ((), ()))
    for hp in range(heads // 2):
        sl = slice(hp * LANES, (hp + 1) * LANES)
        q2 = q_ref[:, sl]
        k2 = k_ref[:, sl]
        v2 = v_ref[:, sl]
        outs = []
        for hh in range(2):
            qm = jnp.where(low if hh == 0 else jnp.logical_not(low), q2, jnp.zeros_like(q2))
            s = lax.dot_general(qm, k2, dn, preferred_element_type=F32)
            outs.append(_softmax_pv([(s, v2)], (nq, 1)))
        o_ref[:, sl] = jnp.where(low, outs[0], outs[1]).astype(BF)


def _na_ctx(qc, kc, vc, *, batch, n_ctx, heads):
    w = qc.shape[1]
    spec = pl.BlockSpec((n_ctx, w), lambda b: (b, 0))
    return pl.pallas_call(
        functools.partial(_na_ctx_kernel, heads=heads),
        grid=(batch,),
        in_specs=[spec, spec, spec],
        out_specs=spec,
        out_shape=jax.ShapeDtypeStruct(qc.shape, BF),
        compiler_params=_cparams("parallel"),
        name="na_ctx",
    )(qc, kc, vc)


def _outproj_kernel(z_ref, pf_ref, na_ref, x_ref, gate_ref, gpost_ref,
                    wglu_ref, cs_ref, wfo_ref, wout_ref, o_ref, *, widths, sub):
    w_ssm, w_fn, w_na = widths
    wglu, wfo, wout = (r[...].astype(BF) for r in (wglu_ref, wfo_ref, wout_ref))
    for r0 in range(0, x_ref.shape[0], sub):
        rs = slice(r0, r0 + sub)
        z = jnp.concatenate([z_ref[j, rs, :] for j in range(w_ssm // LANES)], axis=1)
        gl = jnp.dot(z.astype(BF), wglu, preferred_element_type=F32)
        y_ssm = (z * jax.nn.sigmoid(gl)).astype(BF)
        if len(pf_ref.shape) == 3:
            tiles = [pf_ref[c, rs, :] for c in range(pf_ref.shape[0])]
        else:
            l2 = pf_ref.shape[1]
            tiles = [jnp.concatenate([pf_ref[c, :, k1, :] for k1 in range(r0 // l2, (r0 + sub) // l2)], axis=0)
                     for c in range(pf_ref.shape[0])]
        pf = jnp.concatenate(tiles, axis=1).astype(BF)
        mixed = jnp.dot(pf, cs_ref[...], preferred_element_type=F32).astype(BF)
        y_fft = jnp.dot(mixed, wfo, preferred_element_type=F32).astype(BF)
        y = jnp.concatenate([y_ssm, y_fft, na_ref[rs, :]], axis=1)
        o = jnp.dot(y, wout, preferred_element_type=F32)
        ms = jnp.mean(o * o, axis=-1, keepdims=True)
        o_ref[rs, :] = x_ref[rs, :] + gate_ref[...] * (o * lax.rsqrt(ms + EPS) * gpost_ref[...])


def _outproj(z3, pf, yna, x2d, mod3, g_post, w_glu, cs, w_fo, w_out, *,
             layer, tm, tiles_per_batch, mod_row, widths):
    n, d = x2d.shape
    w_ssm, w_fn, w_na = widths
    if mod_row is None:
        row = lambda i: i // tiles_per_batch
    else:
        row = lambda i: mod_row
    kern = functools.partial(_outproj_kernel, widths=widths, sub=min(tm, 512))
    return pl.pallas_call(
        kern,
        grid=(n // tm,),
        in_specs=[pl.BlockSpec((w_ssm // LANES, tm, LANES), lambda i: (0, i, 0)),
                  (pl.BlockSpec((None, pf.shape[1], tm, LANES),
                                lambda i: (i // tiles_per_batch, 0, i % tiles_per_batch, 0))
                   if pf.ndim == 4 else
                   pl.BlockSpec((None, pf.shape[1], pf.shape[2], tm // pf.shape[2], LANES),
                                lambda i: (i // tiles_per_batch, 0, 0, i % tiles_per_batch, 0))),
                  pl.BlockSpec((tm, w_na), lambda i: (i, 0)),
                  pl.BlockSpec((tm, d), lambda i: (i, 0)),
                  pl.BlockSpec((None, 1, d), lambda i: (row(i), 0, 2)),
                  pl.BlockSpec((None, 1, d), lambda i: (layer, 0, 0)),
                  _layer_resident(w_glu, layer),
                  _resident(cs.shape, lambda i: (0, 0)),
                  _layer_resident(w_fo, layer),
                  _layer_resident(w_out, layer)],
        out_specs=pl.BlockSpec((tm, d), lambda i: (i, 0)),
        out_shape=jax.ShapeDtypeStruct((n, d), F32),
        compiler_params=_cparams("parallel"),
        name="outproj",
    )(z3, pf, yna, x2d, mod3, g_post.reshape(-1, 1, d), w_glu, cs, w_fo, w_out)


def _ffn_kernel(x_ref, sh_ref, sc_ref, gate_ref, gpre_ref, gpost_ref, wg_ref, wu_ref, wd_ref,
                o_ref, a_ref, *, sub, chunk):
    dff = wg_ref.shape[1]
    for r0 in range(0, x_ref.shape[0], sub):
        rs = slice(r0, r0 + sub)
        x = x_ref[rs, :]
        ms = jnp.mean(x * x, axis=-1, keepdims=True)
        m = ((x * lax.rsqrt(ms + EPS) * gpre_ref[...]) * (1.0 + sc_ref[...]) + sh_ref[...]).astype(BF)
        for c0 in range(0, dff, chunk):
            sl = slice(c0, min(c0 + chunk, dff))
            g = jnp.dot(m, wg_ref[:, sl], preferred_element_type=F32)
            u = jnp.dot(m, wu_ref[:, sl], preferred_element_type=F32)
            a_ref[rs, sl] = (g * jax.nn.sigmoid(g) * u).astype(BF)
        y = jnp.dot(a_ref[rs, :], wd_ref[...], preferred_element_type=F32)
        ms2 = jnp.mean(y * y, axis=-1, keepdims=True)
        o_ref[rs, :] = x + gate_ref[...] * (y * lax.rsqrt(ms2 + EPS) * gpost_ref[...])


FFN_WEIGHT_STEPS = 8


def _ffn_staged_kernel(*refs, sub, chunk, n_lat_tiles, with_ctx):
    if with_ctx:
        (x_ref, xc_ref, sh_ref, sc_ref, gate_ref, gpre_ref, gpost_ref, wg_ref, wu_ref, wd_ref,
         o_ref, oc_ref, a_ref, wg_s, wu_s, wd_s) = refs
    else:
        (x_ref, sh_ref, sc_ref, gate_ref, gpre_ref, gpost_ref, wg_ref, wu_ref, wd_ref,
         o_ref, a_ref, wg_s, wu_s, wd_s) = refs
    i = pl.program_id(0)
    common = (sh_ref, sc_ref, gate_ref, gpre_ref, gpost_ref, wg_s, wu_s, wd_s)

    @pl.when(i < FFN_WEIGHT_STEPS)
    def _cast_weights():
        for src, dst in ((wg_ref, wg_s), (wu_ref, wu_s), (wd_ref, wd_s)):
            rows = src.shape[0]
            dst[pl.ds(pl.multiple_of(i * rows, rows), rows), :] = src[...].astype(BF)

    @pl.when((i >= FFN_WEIGHT_STEPS) & (i < FFN_WEIGHT_STEPS + n_lat_tiles))
    def _latent():
        _ffn_kernel(x_ref, *common, o_ref, a_ref, sub=sub, chunk=chunk)

    if with_ctx:
        @pl.when(i >= FFN_WEIGHT_STEPS + n_lat_tiles)
        def _context():
            _ffn_kernel(xc_ref, *common, oc_ref, a_ref, sub=min(sub, xc_ref.shape[0]), chunk=chunk)


def _ffn(x2d, xc2d, mod3, g_pre, g_post, wg, wu, wd, *, layer, tm, tiles_per_batch, ctx_row):
    n, d = x2d.shape
    dff = wg.shape[2]
    nl = n // tm
    nw = FFN_WEIGHT_STEPS
    with_ctx = xc2d is not None
    gains = (g_pre.reshape(-1, 1, d), g_post.reshape(-1, 1, d))
    slab = lambda i: (layer, jnp.minimum(i, nw - 1), 0)
    weights = [pl.BlockSpec((None, d // nw, dff), slab), pl.BlockSpec((None, d // nw, dff), slab),
               pl.BlockSpec((None, dff // nw, d), slab)]
    gain_specs = [pl.BlockSpec((None, 1, d), lambda i: (layer, 0, 0))] * 2
    tile = lambda i: jnp.clip(i - nw, 0, nl - 1)
    row = lambda i: jnp.where(i < nw + nl, tile(i) // tiles_per_batch, ctx_row)
    lat = lambda i: (tile(i), 0)
    mods = [pl.BlockSpec((None, 1, d), functools.partial(lambda i, c: (row(i), 0, c), c=c)) for c in (3, 4, 5)]
    in_specs = [pl.BlockSpec((tm, d), lat)]
    out_specs = [pl.BlockSpec((tm, d), lat)]
    out_shape = [jax.ShapeDtypeStruct((n, d), F32)]
    args = [x2d]
    if with_ctx:
        nc = xc2d.shape[0]
        ctx = lambda i: (jnp.maximum(i - nw - nl, 0), 0)
        in_specs.append(pl.BlockSpec((nc, d), ctx, pipeline_mode=pl.Buffered(1)))
        out_specs.append(pl.BlockSpec((nc, d), ctx))
        out_shape.append(jax.ShapeDtypeStruct((nc, d), F32))
        args.append(xc2d)
    outs = pl.pallas_call(
        functools.partial(_ffn_staged_kernel, sub=min(tm, 512), chunk=3 * MXU_TILE, n_lat_tiles=nl,
                          with_ctx=with_ctx),
        grid=(nw + nl + int(with_ctx),),
        in_specs=in_specs + mods + gain_specs + weights,
        out_specs=out_specs,
        out_shape=out_shape,
        scratch_shapes=[pltpu.VMEM((tm, dff), BF), pltpu.VMEM((d, dff), BF), pltpu.VMEM((d, dff), BF),
                        pltpu.VMEM((dff, d), BF)],
        compiler_params=_cparams("arbitrary"),
        name="ffn_staged",
    )(*args, mod3, mod3, mod3, *gains, wg, wu, wd)
    return (outs[0], outs[1]) if with_ctx else (outs[0], None)


def _layer(x2d, xc2d, p, w, layer, *, batch, seq, n_ctx, last, rope_tabs, consts):
    d = x2d.shape[1]
    w_ssm = consts["w_ssm"]
    w_fn = w["w_fourier"].shape[1]
    w_na = (w["w_in"].shape[2] - w_ssm - w_fn) // 3
    widths = (w_ssm, w_fn, w_na)
    heads = w_na // HEAD_DIM
    tm = min(512, seq)
    tmc = min(512, batch * n_ctx)

    mod = w["mod"][layer]
    mod3 = mod.reshape(mod.shape[0], 1, 6 * d)

    cos, sin = rope_tabs
    tm2 = min(2 * tm, seq)
    (u3, f, q, k, v), (u3c, fc, qc, kc, vc) = _inproj(
        x2d, xc2d, mod3, p["g_pre_mix"], w["w_in"], cos, sin, layer=layer, tm=tm2,
        tiles_per_batch=seq // tm2, ctx_row=batch, widths=widths)

    z3, zc3 = _ssm(u3, u3c, *w["ssm"], layer=layer, batch=batch, seq=seq, n_ctx=n_ctx)

    l2 = consts["l2"]
    l1 = seq // l2
    g = _dft_a(consts["dft_a"], f.reshape(batch, l2, l1 * w_fn))
    pf = _dft_b(consts["dft_b"], g.reshape(batch, 2, l2, l1, w_fn), kb=min(16, l2))

    yna = _na_latent(q, k, v, kc, vc, consts["na"], layer, batch=batch, seq=seq, n_ctx=n_ctx)

    w_glu, w_fo, w_out = w["w_glu"], w["w_fourier"], w["w_out"]
    x2d = _outproj(z3, pf, yna, x2d, mod3, p["g_post_mix"], w_glu, consts["cs_lat"], w_fo, w_out,
                   layer=layer, tm=tm2, tiles_per_batch=seq // tm2, mod_row=None, widths=widths)
    if not last:
        pfc = _dft_ctx(consts["dft_ctx"], fc.reshape(batch, n_ctx, w_fn))
        ynac = _na_ctx(qc, kc, vc, batch=batch, n_ctx=n_ctx, heads=heads)
        tc = min(tmc, n_ctx)
        xc2d = _outproj(zc3, pfc, ynac, xc2d, mod3, p["g_post_mix"], w_glu, consts["cs_ctx"], w_fo, w_out,
                        layer=layer, tm=tc, tiles_per_batch=n_ctx // tc, mod_row=batch, widths=widths)
    else:
        xc2d = None
    wg, wu, wd = w["w_ffn_gate"], w["w_ffn_up"], w["w_ffn_down"]
    return _ffn(x2d, xc2d, mod3, p["g_pre_ffn"], p["g_post_ffn"], wg, wu, wd,
                layer=layer, tm=tm2, tiles_per_batch=seq // tm2, ctx_row=batch)


def kernel(x, c, ctx, c_ctx, w_mod, b_mod, g_pre_mix, g_post_mix, w_in, ssm_a_re, ssm_a_im, ssm_log_dt,
           ssm_b_re, ssm_b_im, ssm_c_re, ssm_c_im, ssm_d, w_glu, w_fourier, na_rpb, w_out, g_pre_ffn,
           g_post_ffn, w_ffn_gate, w_ffn_up, w_ffn_down):
    batch, seq, d = x.shape
    n_ctx = ctx.shape[1]
    depth = w_mod.shape[0]
    w_fn = w_fourier.shape[1]
    params = dict(g_pre_mix=g_pre_mix, g_post_mix=g_post_mix, g_pre_ffn=g_pre_ffn, g_post_ffn=g_post_ffn)
    seg_len = (seq + 2 * n_ctx) // SSM_T // SSM_SEG
    ssm_ops = jax.vmap(functools.partial(_ssm_weights, seg_len=seg_len))(
        ssm_a_re, ssm_a_im, ssm_log_dt, ssm_b_re, ssm_b_im, ssm_c_re, ssm_c_im, ssm_d)
    nrow = -(-(batch + 1) // 8) * 8
    cstack = jnp.concatenate([c, c_ctx[None, :], jnp.zeros((nrow - batch - 1, d), c.dtype)], axis=0)
    weights = dict(mod=_mod_rows(cstack, w_mod, b_mod), ssm=ssm_ops,
                   w_in=w_in, w_glu=w_glu, w_fourier=w_fourier, w_out=w_out,
                   w_ffn_gate=w_ffn_gate,
                   w_ffn_up=w_ffn_up, w_ffn_down=w_ffn_down)

    l2 = 64 if seq % (64 * 8) == 0 else 8
    l1 = seq // l2
    consts = dict(
        l2=l2,
        w_ssm=ssm_d.shape[1],
        na=_na_tables(na_rpb, seq, 4),
        dft_a=_dft_rows(l2).astype(BF),
        dft_b=_dft_b_mats(seq, l1, l2).astype(BF),
        dft_ctx=_dft_rows(n_ctx).astype(BF),
        cs_lat=_channel_dft(w_fn, seq).astype(BF),
        cs_ctx=_channel_dft(w_fn, n_ctx).astype(BF),
    )
    rope_tabs = _rope_tables(seq)

    x2d = x.reshape(batch * seq, d)
    xc2d = ctx.reshape(batch * n_ctx, d)
    for layer in range(depth):
        x2d, xc2d = _layer(x2d, xc2d, params, weights, layer, batch=batch, seq=seq, n_ctx=n_ctx,
                           last=(layer == depth - 1), rope_tabs=rope_tabs, consts=consts)
    return x2d.reshape(batch, seq, d)
```

```python
import functools
import math

import numpy as np
import jax
import jax.numpy as jnp
from jax import lax
from jax.experimental import pallas as pl
from jax.experimental.pallas import tpu as pltpu

BF = jnp.bfloat16
F32 = jnp.float32

EPS = 1e-6
GRID_W = 64
HEAD_DIM = 64
NA_ROWS = 8
NA_COLS = 16
ROPE_BASE = 10000.0
SSM_GROUP = 16
SSM_STATE = 64
FNET_GROUP = 64
LANES = 128
MXU_TILE = 256
SSM_T = 8
SSM_SEG = 8
SSM_STAGES = (4, 2)
NEG = -1e30
VMEM_LIMIT = 56 * 1024 * 1024
VMEM_LIMIT_NA = 60000 * 1024


def _cparams(*sem, vmem=VMEM_LIMIT):
    return pltpu.CompilerParams(dimension_semantics=sem, vmem_limit_bytes=vmem)


def _resident(shape, index_map):
    return pl.BlockSpec(shape, index_map, pipeline_mode=pl.Buffered(1))


def _layer_resident(w_all, layer):
    return pl.BlockSpec((None,) + w_all.shape[1:], lambda *_: (layer, 0, 0), pipeline_mode=pl.Buffered(1))


def _mod_kernel(c_ref, w_ref, b_ref, o_ref):
    cs = c_ref[...]
    s = (cs * jax.nn.sigmoid(cs)).astype(BF)
    o_ref[...] = jnp.dot(s, w_ref[...].astype(BF), preferred_element_type=F32) + b_ref[...]


def _mod_rows(cstack, w_mod, b_mod):
    rows, d = cstack.shape
    depth, _, n = w_mod.shape
    bn = 2 * d if n % (2 * d) == 0 else d
    return pl.pallas_call(
        _mod_kernel,
        grid=(depth, n // bn),
        in_specs=[pl.BlockSpec((rows, d), lambda l, i: (0, 0)),
                  pl.BlockSpec((None, d, bn), lambda l, i: (l, 0, i)),
                  pl.BlockSpec((None, 1, bn), lambda l, i: (l, 0, i))],
        out_specs=pl.BlockSpec((None, rows, bn), lambda l, i: (l, 0, i)),
        out_shape=jax.ShapeDtypeStruct((depth, rows, n), F32),
        compiler_params=_cparams("parallel", "parallel"),
        name="mod_rows",
    )(cstack, w_mod, b_mod.reshape(depth, 1, n))


def _rope_tile(t, cos, sin, first):
    partner = jnp.where(first, pltpu.roll(t, LANES - 16, 1), pltpu.roll(t, 16, 1))
    return t * cos + partner * sin


def _inproj_kernel(x_ref, sh_ref, sc_ref, g_ref, w_ref, cos_ref, sin_ref,
                   u_ref, f_ref, q_ref, k_ref, v_ref, *, widths, rope, sub):
    w_ssm, w_fn, w_na = widths
    scale = HEAD_DIM ** -0.5 * math.log2(math.e)
    w = w_ref[...].astype(BF)
    for r0 in range(0, x_ref.shape[0], sub):
        rs = slice(r0, r0 + sub)
        x = x_ref[rs, :]
        ms = jnp.mean(x * x, axis=-1, keepdims=True)
        xn = x * lax.rsqrt(ms + EPS) * g_ref[...]
        m = (xn * (1.0 + sc_ref[...]) + sh_ref[...]).astype(BF)

        h = jnp.dot(m, w, preferred_element_type=F32)

        def proj(lo, n, h=h):
            return h[:, lo:lo + n]

        u = proj(0, w_ssm)
        for j in range(w_ssm // LANES):
            u_ref[j, rs, :] = u[:, j * LANES:(j + 1) * LANES]
        f_ref[rs, :] = proj(w_ssm, w_fn).astype(BF)
        q = proj(w_ssm + w_fn, w_na)
        k = proj(w_ssm + w_fn + w_na, w_na)
        v_ref[rs, :] = proj(w_ssm + w_fn + 2 * w_na, w_na).astype(BF)
        if rope:
            cos = cos_ref[rs, :]
            sin = sin_ref[rs, :]
            lane = lax.broadcasted_iota(jnp.int32, cos.shape, 1)
            first = (lane % 32) < 16
            for j in range(w_na // LANES):
                sl = slice(j * LANES, (j + 1) * LANES)
                q_ref[rs, sl] = (_rope_tile(q[:, sl], cos, sin, first) * scale).astype(BF)
                k_ref[rs, sl] = _rope_tile(k[:, sl], cos, sin, first).astype(BF)
        else:
            q_ref[rs, :] = (q * scale).astype(BF)
            k_ref[rs, :] = k.astype(BF)


def _inproj_pair_kernel(x_ref, xc_ref, sh_ref, sc_ref, g_ref, w_ref, cos_ref, sin_ref, *outs,
                        widths, sub, n_lat_tiles):
    common = (sh_ref, sc_ref, g_ref, w_ref, cos_ref, sin_ref)
    i = pl.program_id(0)

    @pl.when(i < n_lat_tiles)
    def _latent():
        _inproj_kernel(x_ref, *common, *outs[:5], widths=widths, rope=True, sub=sub)

    @pl.when(i >= n_lat_tiles)
    def _context():
        _inproj_kernel(xc_ref, *common, *outs[5:], widths=widths, rope=False, sub=min(sub, xc_ref.shape[0]))


def _inproj(x2d, xc2d, mod3, g_pre, w_in, cos, sin, *, layer, tm, tiles_per_batch, ctx_row, widths):
    n, d = x2d.shape
    nc = xc2d.shape[0]
    w_ssm, w_fn, w_na = widths
    nl = n // tm
    tile = lambda i: jnp.minimum(i, nl - 1)
    row = lambda i: jnp.where(i < nl, tile(i) // tiles_per_batch, ctx_row)
    pos = lambda i: (tile(i) % tiles_per_batch, 0)
    lat = lambda i: (tile(i), 0)
    one = lambda i: (0, 0)

    def outs(rows, idx, idx3):
        specs = [pl.BlockSpec((w_ssm // LANES, rows, LANES), idx3), pl.BlockSpec((rows, w_fn), idx)]
        specs += [pl.BlockSpec((rows, w_na), idx)] * 3
        return specs

    def shapes(rows):
        return ([jax.ShapeDtypeStruct((w_ssm // LANES, rows, LANES), F32), jax.ShapeDtypeStruct((rows, w_fn), BF)]
                + [jax.ShapeDtypeStruct((rows, w_na), BF)] * 3)

    kern = functools.partial(_inproj_pair_kernel, widths=widths, sub=min(tm, 512), n_lat_tiles=nl)
    res = pl.pallas_call(
        kern,
        grid=(nl + 1,),
        in_specs=[pl.BlockSpec((tm, d), lat),
                  pl.BlockSpec((nc, d), one),
                  pl.BlockSpec((None, 1, d), lambda i: (row(i), 0, 0)),
                  pl.BlockSpec((None, 1, d), lambda i: (row(i), 0, 1)),
                  pl.BlockSpec((None, 1, d), lambda i: (layer, 0, 0)),
                  _layer_resident(w_in, layer),
                  pl.BlockSpec((tm, LANES), pos),
                  pl.BlockSpec((tm, LANES), pos)],
        out_specs=outs(tm, lat, lambda i: (0, tile(i), 0)) + outs(nc, one, lambda i: (0, 0, 0)),
        out_shape=shapes(n) + shapes(nc),
        compiler_params=_cparams("arbitrary"),
        name="inproj",
    )(x2d, xc2d, mod3, mod3, g_pre.reshape(-1, 1, d), w_in, cos, sin)
    return res[:5], res[5:]


def _rope_tables(seq):
    t = np.arange(seq)
    row = (t // GRID_W).astype(np.float32)
    col = (t % GRID_W).astype(np.float32)
    quarter = HEAD_DIM // 4
    freqs = (np.float32(ROPE_BASE) ** (-np.arange(quarter, dtype=np.float32) / quarter)).astype(np.float32)
    d = np.arange(LANES) % HEAD_DIM
    use_col = (d // (HEAD_DIM // 2)) == 1
    fidx = d % quarter
    sign = np.where((d % (HEAD_DIM // 2)) // quarter == 0, -1.0, 1.0)
    pos = np.where(use_col[None, :], col[:, None], row[:, None])
    ang = (pos * freqs[fidx][None, :]).astype(np.float32).astype(np.float64)
    return jnp.asarray(np.cos(ang), F32), jnp.asarray(np.sin(ang) * sign[None, :], F32)


def _gelu_tanh(x):
    c = math.sqrt(2.0 / math.pi)
    return x * (0.5 * (1.0 + jnp.tanh(c * (x + 0.044715 * (x * x * x)))))


def _group_of(idx, width, groups):
    shift = width.bit_length() - 1
    assert width == 1 << shift and groups & (groups - 1) == 0
    return lax.bitwise_and(lax.shift_right_logical(idx, shift), groups - 1)


def _same_group(shape, row0, row_width, col0, col_width, groups):
    r = lax.broadcasted_iota(jnp.int32, shape, 0) + row0
    c = lax.broadcasted_iota(jnp.int32, shape, 1) + col0
    return _group_of(r, row_width, groups) == _group_of(c, col_width, groups)


def _swap_stage(tiles, d):
    lane = lax.broadcasted_iota(jnp.int32, tiles[0].shape, 1)
    upper = lax.bitwise_and(lax.shift_right_logical(lane, SSM_GROUP.bit_length() - 1), d) != 0
    out = list(tiles)
    for x in range(len(tiles)):
        if x & d:
            continue
        a, b = tiles[x], tiles[x + d]
        out[x] = jnp.where(upper, pltpu.roll(b, SSM_GROUP * d, 1), a)
        out[x + d] = jnp.where(upper, b, pltpu.roll(a, LANES - SSM_GROUP * d, 1))
    return out


def _ssm_kernel(ul_ref, uc_ref, wc_ref, lag_ref, dc_ref, rexp_ref, cexp_ref, lam_ref, lamseg_ref,
                z_ref, zc_ref, u_s, bm_s, kin_s, cm_s, s_ref, h_ref, e_ref, *, seg_len, n_lat, n_ctx, stages):
    T = SSM_T
    ns = 2 ** len(stages)
    gs = T // ns
    sw = gs * LANES
    rows = u_s.shape[1]
    tps = s_ref.shape[0] // ns
    tpq = tps // 4
    nst = tpq * LANES
    gpt = LANES // SSM_GROUP
    per_q = gpt * SSM_STATE

    def lanes(c):
        return slice(c * LANES, (c + 1) * LANES)

    def col0(s, q):
        return q * per_q + s * nst

    def swapped(tiles, order):
        for d in order:
            tiles = _swap_stage(tiles, d)
        return tiles

    @pl.when(pl.program_id(1) == 0)
    def _expand_operators():
        rexp = rexp_ref[...]
        cexp = cexp_ref[...]
        for s in range(ns):
            wsel = jnp.concatenate([wc_ref[:, col0(s, q):col0(s, q) + nst] for q in range(4)], axis=1)
            blk = jnp.dot(rexp, wsel, preferred_element_type=F32)
            keep = _same_group(blk.shape, 0, SSM_GROUP, 0, SSM_STATE, gs)
            bm_s[s] = jnp.where(keep, blk, 0.0).astype(BF)
            dsel = jnp.concatenate([dc_ref[col0(s, q):col0(s, q) + nst, :] for q in range(4)], axis=0)
            blk = jnp.dot(dsel, cexp, preferred_element_type=F32)
            keep = _same_group(blk.shape, 0, SSM_STATE, 0, SSM_GROUP, gs)
            cm_s[s] = jnp.where(keep, blk, 0.0).astype(BF)
        r16 = lax.broadcasted_iota(jnp.int32, (LANES, SSM_GROUP), 0)
        c16 = lax.broadcasted_iota(jnp.int32, (LANES, SSM_GROUP), 1)
        chan = jnp.where(lax.bitwise_and(r16, SSM_GROUP - 1) == c16, 1.0, 0.0).astype(BF)
        for t in range(T):
            lo = (T - 1 - t) * LANES
            blk = jnp.dot(chan, lag_ref[:, lo:lo + T * LANES], preferred_element_type=F32)
            keep = _same_group(blk.shape, 0, SSM_GROUP, 0, SSM_GROUP, gpt)
            blk = jnp.where(keep, blk, 0.0)
            tiles = swapped([blk[:, lanes(x)] for x in range(T)], stages)
            r0 = (t % gs) * LANES + (t // gs) * gs * SSM_GROUP
            for s in range(ns):
                part = jnp.concatenate(tiles[s * gs:(s + 1) * gs], axis=1)
                kin_s[s, r0:r0 + gs * SSM_GROUP, :] = part[s * gs * SSM_GROUP:(s + 1) * gs * SSM_GROUP, :].astype(BF)

    xs = []
    for t in range(T):
        ct = uc_ref[pl.ds(t, n_ctx, stride=T), :]
        xs.append(jnp.concatenate([ct, ul_ref[pl.ds(t, n_lat, stride=T), :], ct], axis=0))
    xs = swapped(xs, stages)
    for s in range(ns):
        for k in range(gs):
            u_s[s, :, lanes(k)] = xs[s * gs + k].astype(BF)
    for s in range(ns):
        r = jnp.dot(u_s[s], bm_s[s], preferred_element_type=F32)
        for c in range(tps):
            s_ref[s * tps + c] = r[:, lanes(c)]

    chains = []
    for s in range(ns):
        for d in range(2):
            for k in range(tpq):
                cr, ci = s * tps + 2 * d * tpq + k, s * tps + (2 * d + 1) * tpq + k
                lr = slice(col0(s, 2 * d) + k * LANES, col0(s, 2 * d) + (k + 1) * LANES)
                li = slice(col0(s, 2 * d + 1) + k * LANES, col0(s, 2 * d + 1) + (k + 1) * LANES)
                chains.append((cr, ci, d, lr, li))
    nt = ns * tps

    def scan_step(i, carry, record):
        new = list(carry)
        for cr, ci, d, lr_sl, li_sl in chains:
            rsl = pl.ds(i if d == 0 else seg_len - 1 - i, SSM_SEG, stride=seg_len)
            hr, hi = carry[cr], carry[ci]
            if record:
                h_ref[cr, rsl, :] = hr
                h_ref[ci, rsl, :] = hi
            lr, li = lam_ref[:, lr_sl], lam_ref[:, li_sl]
            new[cr] = lr * hr - li * hi + s_ref[cr, rsl, :]
            new[ci] = lr * hi + li * hr + s_ref[ci, rsl, :]
        return tuple(new)

    zero = jnp.zeros((SSM_SEG, LANES), F32)
    fin = lax.fori_loop(0, seg_len, lambda i, c: scan_step(i, c, False), (zero,) * nt, unroll=4)
    for c in range(nt):
        e_ref[c, 0:SSM_SEG, :] = fin[c]

    for cr, ci, d, lr_sl, li_sl in chains:
        lr, li = lamseg_ref[:, lr_sl], lamseg_ref[:, li_sl]
        er = jnp.zeros((1, LANES), F32)
        ei = jnp.zeros((1, LANES), F32)
        for n in range(SSM_SEG):
            g = n if d == 0 else SSM_SEG - 1 - n
            e_ref[cr, SSM_SEG + g:SSM_SEG + g + 1, :] = er
            e_ref[ci, SSM_SEG + g:SSM_SEG + g + 1, :] = ei
            fr, fi = e_ref[cr, g:g + 1, :], e_ref[ci, g:g + 1, :]
            er, ei = lr * er - li * ei + fr, lr * ei + li * er + fi

    ent = tuple(e_ref[c, SSM_SEG:2 * SSM_SEG, :] for c in range(nt))
    lax.fori_loop(0, seg_len, lambda i, c: scan_step(i, c, True), ent, unroll=2)

    ys, ycs = [], []
    for s in range(ns):
        hcat = jnp.concatenate([h_ref[s * tps + c].astype(BF) for c in range(tps)], axis=1)
        y = _gelu_tanh(jnp.dot(u_s[s], kin_s[s], preferred_element_type=F32)
                       + jnp.dot(hcat, cm_s[s], preferred_element_type=F32))
        ys += [y[:, lanes(k)] for k in range(gs)]
        hc = jnp.concatenate(
            [h_ref[s * tps + c, 0:n_ctx, :].astype(BF) for c in range(2 * tpq)]
            + [h_ref[s * tps + c, rows - n_ctx:rows, :].astype(BF) for c in range(2 * tpq, tps)], axis=1)
        yc = _gelu_tanh(jnp.dot(u_s[s, 0:n_ctx, :], kin_s[s], preferred_element_type=F32)
                        + jnp.dot(hc, cm_s[s], preferred_element_type=F32))
        ycs += [yc[:, lanes(k)] for k in range(gs)]
    ys = swapped(ys, stages[::-1])
    ycs = swapped(ycs, stages[::-1])
    for t in range(T):
        z_ref[pl.ds(t, n_lat, stride=T), :] = ys[t][n_ctx:n_ctx + n_lat, :]
        zc_ref[pl.ds(t, n_ctx, stride=T), :] = ycs[t]


def _ssm_weights(a_re, a_im, log_dt, b_re, b_im, c_re, c_im, d_skip, seg_len):
    T = SSM_T
    g, p = a_re.shape[1], a_re.shape[2]
    hc = b_re.shape[-1]
    gpt = LANES // hc
    J = g // gpt
    lam = lax.complex(a_re.astype(F32), a_im.astype(F32))
    dt = jnp.exp(log_dt.astype(F32))[..., None]
    ldt = lam * dt
    lam_bar = jnp.exp(ldt)
    bbar = ((lam_bar - 1) / lam)[..., None] * lax.complex(b_re.astype(F32), b_im.astype(F32))
    cmat = lax.complex(c_re.astype(F32), c_im.astype(F32))
    kk = jnp.arange(T + 1, dtype=F32)
    pw = jnp.exp(ldt[:, None] * kk[None, :, None, None])

    wf = pw[0, :T][::-1][:, :, None, :] * jnp.swapaxes(bbar[0], 1, 2)[None]
    wb = pw[1, :T][:, :, None, :] * jnp.swapaxes(bbar[1], 1, 2)[None]
    wq = jnp.stack([wf.real, wf.imag, wb.real, wb.imag]).reshape(4, T, J, gpt, hc, p)
    wc = wq.transpose(2, 1, 4, 0, 3, 5).reshape(J, T * hc, 4 * gpt * p)

    df = cmat[0][None] * pw[0, 1:][:, :, None, :]
    db = cmat[1][None] * pw[1, 1:][::-1][:, :, None, :]
    dq = jnp.stack([df.real, -df.imag, db.real, -db.imag]).reshape(4, T, J, gpt, hc, p)
    dc = dq.transpose(2, 0, 3, 5, 1, 4).reshape(J, 4 * gpt * p, T * hc)

    mf = jnp.einsum('gcp,kgp,gph->khgc', cmat[0], pw[0, :T], bbar[0]).real
    mb = jnp.einsum('gcp,kgp,gph->khgc', cmat[1], pw[1, :T], bbar[1]).real
    skip = jnp.eye(hc, dtype=F32)[:, None, :] * d_skip.astype(F32).reshape(g, hc)[None]
    m0 = mf[0] + mb[0] + skip
    lags = jnp.concatenate([mb[1:][::-1], m0[None], mf[1:]], axis=0)
    kc = lags.reshape(2 * T - 1, hc, J, LANES).transpose(2, 1, 0, 3).reshape(J, hc, (2 * T - 1) * LANES)

    def lam_rows(power):
        lp = jnp.exp(ldt * power)
        lt = lp.reshape(2, J, gpt * p)
        return jnp.concatenate([lt[0].real, lt[0].imag, lt[1].real, lt[1].imag], axis=-1)[:, None, :]

    lam_t = jnp.broadcast_to(lam_rows(float(T)), (J, SSM_SEG, 4 * gpt * p))
    return wc.astype(BF), kc.astype(BF), dc.astype(BF), lam_t, lam_rows(float(T * seg_len))


def _ssm_expanders(gs):
    r = np.arange(gs * LANES)
    k, slot, ch = r // LANES, (r % LANES) // SSM_GROUP, r % SSM_GROUP
    t = (slot // gs) * gs + k
    rexp = np.zeros((gs * LANES, SSM_T * SSM_GROUP), np.float32)
    rexp[r, t * SSM_GROUP + ch] = 1.0
    return jnp.asarray(rexp, BF), jnp.asarray(rexp.T, BF)


def _ssm(u3, u3c, wc, kc, dc, lam, lamseg, *, layer, batch, seq, n_ctx):
    T = SSM_T
    J = u3.shape[0]
    ns = wc.shape[-1]
    nsets = 2 ** len(SSM_STAGES)
    gs = T // nsets
    sw = gs * LANES
    n_lat, n_c = seq // T, n_ctx // T
    rows = n_lat + 2 * n_c
    seg_len = rows // SSM_SEG
    assert rows % SSM_SEG == 0 and n_c % 16 == 0
    rexp, cexp = _ssm_expanders(gs)
    kern = functools.partial(_ssm_kernel, seg_len=seg_len, n_lat=n_lat, n_ctx=n_c, stages=SSM_STAGES)
    const = lambda j, b: (0, 0)
    per_j = lambda j, b: (layer, j, 0, 0)
    per_jb = lambda j, b: (j, b, 0)
    return pl.pallas_call(
        kern,
        grid=(J, batch),
        in_specs=[pl.BlockSpec((None, seq, LANES), per_jb),
                  pl.BlockSpec((None, n_ctx, LANES), per_jb),
                  pl.BlockSpec((None, None) + wc.shape[2:], per_j),
                  pl.BlockSpec((None, None) + kc.shape[2:], per_j),
                  pl.BlockSpec((None, None) + dc.shape[2:], per_j),
                  pl.BlockSpec(rexp.shape, const),
                  pl.BlockSpec(cexp.shape, const),
                  pl.BlockSpec((None, None, SSM_SEG, ns), per_j),
                  pl.BlockSpec((None, None, 1, ns), per_j)],
        out_specs=[pl.BlockSpec((None, seq, LANES), per_jb),
                   pl.BlockSpec((None, n_ctx, LANES), per_jb)],
        out_shape=[jax.ShapeDtypeStruct(u3.shape, F32),
                   jax.ShapeDtypeStruct(u3c.shape, F32)],
        scratch_shapes=[pltpu.VMEM((nsets, rows, sw), BF),
                        pltpu.VMEM((nsets, sw, ns // nsets), BF),
                        pltpu.VMEM((nsets, sw, sw), BF),
                        pltpu.VMEM((nsets, ns // nsets, sw), BF),
                        pltpu.VMEM((ns // LANES, rows, LANES), F32),
                        pltpu.VMEM((ns // LANES, rows, LANES), F32),
                        pltpu.VMEM((ns // LANES, 2 * SSM_SEG, LANES), F32)],
        compiler_params=_cparams("arbitrary", "arbitrary"),
        name="ssm_chunk_scan",
    )(u3, u3c, wc, kc, dc, rexp, cexp, lam, lamseg)


def _dft_a_kernel(w_ref, x_ref, o_ref):
    o_ref[...] = jnp.dot(w_ref[...], x_ref[...], preferred_element_type=F32).astype(BF)


def _dft_rows(n):
    k = np.arange(n)
    m = (k[:, None] * k[None, :]) % n
    ang = m * (2.0 * math.pi / n)
    return jnp.asarray(np.concatenate([np.cos(ang), -np.sin(ang)], axis=0), BF)


def _dft_a(w, x3):
    B, n, cols = x3.shape
    tn = min(cols, 16384)
    return pl.pallas_call(
        _dft_a_kernel,
        grid=(B, cols // tn),
        in_specs=[pl.BlockSpec((2 * n, n), lambda b, i: (0, 0)),
                  pl.BlockSpec((None, n, tn), lambda b, i: (b, 0, i))],
        out_specs=pl.BlockSpec((None, 2 * n, tn), lambda b, i: (b, 0, i)),
        out_shape=jax.ShapeDtypeStruct((B, 2 * n, cols), BF),
        compiler_params=_cparams("parallel", "parallel"),
        name="dft_stage_a",
    )(w, x3)


def _store_lane_tiles(o_ref, p, n, wf, idx):
    per = wf // LANES
    for c in range(2 * per):
        part = p[(c // per) * n:(c // per + 1) * n, (c % per) * LANES:(c % per + 1) * LANES]
        o_ref[(c,) + idx] = part


def _dft_b_kernel(a_ref, g_ref, o_ref, *, kb, l1, wf):
    for i in range(kb):
        g = jnp.concatenate([g_ref[0, i], g_ref[1, i]], axis=0)
        p = jnp.dot(a_ref[i], g, preferred_element_type=F32)
        _store_lane_tiles(o_ref, p, l1, wf, (i,))


def _dft_b_mats(seq, l1, l2):
    k2 = np.arange(l2)[:, None, None]
    k1 = np.arange(l1)[None, :, None]
    j1 = np.arange(l1)[None, None, :]
    m = ((k1 * l2 + k2) * j1) % seq
    ang = m * (2.0 * math.pi / seq)
    ar, ai = np.cos(ang).astype(np.float32), -np.sin(ang).astype(np.float32)
    top = np.concatenate([ar, -ai], axis=2)
    bot = np.concatenate([ai, ar], axis=2)
    return jnp.asarray(np.concatenate([top, bot], axis=1), BF)


def _dft_b(amat, g5, *, kb):
    B, _, l2, l1, wf = g5.shape
    nlt = 2 * wf // LANES
    kern = functools.partial(_dft_b_kernel, kb=kb, l1=l1, wf=wf)
    return pl.pallas_call(
        kern,
        grid=(B, l2 // kb),
        in_specs=[pl.BlockSpec((kb, 2 * l1, 2 * l1), lambda b, i: (i, 0, 0)),
                  pl.BlockSpec((None, 2, kb, l1, wf), lambda b, i: (b, 0, i, 0, 0))],
        out_specs=pl.BlockSpec((None, nlt, kb, l1, LANES), lambda b, i: (b, 0, i, 0, 0)),
        out_shape=jax.ShapeDtypeStruct((B, nlt, l2, l1, LANES), F32),
        compiler_params=_cparams("parallel", "parallel"),
        name="dft_stage_b",
    )(amat, g5)


def _dft_ctx_kernel(w_ref, x_ref, o_ref, *, n, wf):
    p = jnp.dot(w_ref[...], x_ref[...], preferred_element_type=F32)
    _store_lane_tiles(o_ref, p, n, wf, ())


def _dft_ctx(w, x3):
    B, n, wf = x3.shape
    nlt = 2 * wf // LANES
    kern = functools.partial(_dft_ctx_kernel, n=n, wf=wf)
    return pl.pallas_call(
        kern,
        grid=(B,),
        in_specs=[pl.BlockSpec((2 * n, n), lambda b: (0, 0)),
                  pl.BlockSpec((None, n, wf), lambda b: (b, 0, 0))],
        out_specs=pl.BlockSpec((None, nlt, n, LANES), lambda b: (b, 0, 0, 0)),
        out_shape=jax.ShapeDtypeStruct((B, nlt, n, LANES), F32),
        compiler_params=_cparams("parallel"),
        name="dft_ctx",
    )(w, x3)


def _channel_dft(wf, seq):
    c = np.arange(wf)
    same = (c[:, None] // FNET_GROUP) == (c[None, :] // FNET_GROUP)
    m = ((c[:, None] % FNET_GROUP) * (c[None, :] % FNET_GROUP)) % FNET_GROUP
    ang = m * (2.0 * math.pi / FNET_GROUP)
    mask = same / math.sqrt(seq * FNET_GROUP)
    return jnp.asarray(np.concatenate([np.cos(ang) * mask, np.sin(ang) * mask], axis=0), BF)


def _softmax_pv(parts, l_shape):
    m = parts[0][0].max(axis=-1, keepdims=True)
    for s, _ in parts[1:]:
        m = jnp.maximum(m, s.max(axis=-1, keepdims=True))
    l = jnp.zeros(l_shape, F32)
    o = None
    for s, v in parts:
        p = jnp.exp2(s - m)
        l = l + p.sum(axis=-1, keepdims=True)
        pv = jnp.dot(p.astype(BF), v, preferred_element_type=F32)
        o = pv if o is None else o + pv
    return o / l


def _na_kernel(kb_ref, pid_ref, dr_ref, q_ref, k_ref, v_ref, kc_ref, vc_ref, cbx_ref, o_ref, bias_ref,
               *, nkeys, heads, nsub):
    step = pl.program_id(1)
    nq = q_ref.shape[0] // nsub
    rq, kr = nq // GRID_W, nkeys // GRID_W
    lane = lax.broadcasted_iota(jnp.int32, (nq, LANES), 1)
    low = lane < HEAD_DIM
    dn = (((1,), (1,)), ((), ()))

    @pl.when((pl.program_id(0) == 0) & (step == 0))
    def _build_bias():
        for pat in range(bias_ref.shape[0]):
            for qi in range(rq):
                for w in range(kr):
                    a = dr_ref[(pat * rq + qi) * kr + w]
                    for h in range(heads):
                        bias_ref[pat, h, qi * GRID_W:(qi + 1) * GRID_W, w * GRID_W:(w + 1) * GRID_W] = cbx_ref[h, a]

    tasks = [(sb, h) for sb in range(nsub) for h in range(heads)]
    starts = [pl.multiple_of(kb_ref[step * nsub + sb] * GRID_W, GRID_W) for sb in range(nsub)]
    pats = [pid_ref[step * nsub + sb] for sb in range(nsub)]

    n_ctx = kc_ref.shape[0]
    ones = jnp.ones((n_ctx + nkeys, LANES), BF)
    kv = {}

    def keys_values(sb, hp):
        if (sb, hp) not in kv:
            sl = slice(hp * LANES, (hp + 1) * LANES)
            win = pl.ds(starts[sb], nkeys)
            k_all = jnp.concatenate([kc_ref[:, sl], k_ref[win, sl]], axis=0)
            v_all = jnp.concatenate([jnp.concatenate([vc_ref[:, sl], v_ref[win, sl]], axis=0), ones], axis=1)
            kv[sb, hp] = (k_all, v_all)
        return kv[sb, hp]

    def scores(sb, h):
        sl = slice((h // 2) * LANES, (h // 2 + 1) * LANES)
        q2 = q_ref[sb * nq:(sb + 1) * nq, sl]
        qm = jnp.where(low if h % 2 == 0 else jnp.logical_not(low), q2, jnp.zeros_like(q2))
        s = lax.dot_general(qm, keys_values(sb, h // 2)[0], dn, preferred_element_type=F32)
        return s[:, n_ctx:] + bias_ref[pats[sb], h], s[:, :n_ctx]

    def weights(s_w, s_c):
        m = jnp.maximum(s_w.max(axis=-1, keepdims=True), s_c.max(axis=-1, keepdims=True))
        return (jnp.concatenate([jnp.exp2(s_c - m).astype(BF), jnp.exp2(s_w - m).astype(BF)], axis=1),)

    outs = {}

    def finish(sb, h, p):
        sl = slice((h // 2) * LANES, (h // 2 + 1) * LANES)
        ox = jnp.dot(p, keys_values(sb, h // 2)[1], preferred_element_type=F32)
        outs[h % 2] = ox[:, :LANES] / ox[:, LANES:]
        if h % 2 == 1:
            o_ref[sb * nq:(sb + 1) * nq, sl] = jnp.where(low, outs[0], outs[1]).astype(BF)

    nt = len(tasks)
    sc = {0: scores(*tasks[0])}
    if nt > 1:
        sc[1] = scores(*tasks[1])
    pw = {0: weights(*sc.pop(0))}
    for n in range(nt):
        if n + 2 < nt:
            sc[n + 2] = scores(*tasks[n + 2])
        if n + 1 < nt:
            pw[n + 1] = weights(*sc.pop(n + 1))
        finish(*tasks[n], *pw.pop(n))


def _na_plan(rows, rq):
    wr = min(NA_ROWS, rows)
    kr = rq + wr - 1
    nblk = rows // rq
    kbs, pids, pats = [], [], []
    for blk in range(nblk):
        r0 = blk * rq
        rs = [int(np.clip(r0 + i - wr // 2, 0, rows - wr)) for i in range(rq)]
        kb = min(rs[0], rows - kr)
        pat = (tuple(r - kb for r in rs), r0 - kb)
        if pat not in pats:
            pats.append(pat)
        kbs.append(kb)
        pids.append(pats.index(pat))
    return np.asarray(kbs, np.int32), np.asarray(pids, np.int32), pats, kr, wr


def _na_bias(rpb, pats, rq, kr, wr):
    cols = np.arange(GRID_W)
    cstart = np.clip(cols - NA_COLS // 2, 0, GRID_W - NA_COLS)
    kc = np.arange(GRID_W)
    col_ok = (kc[None, :] >= cstart[:, None]) & (kc[None, :] < cstart[:, None] + NA_COLS)
    dc = kc[None, :] - cols[:, None] + NA_COLS - 1
    depth, heads, nr, ncol = rpb.shape
    sel = (dc[:, :, None] == np.arange(ncol)[None, None, :]) & col_ok[:, :, None]
    cb = jnp.einsum('lhab,ckb->lhack', rpb.astype(F32), jnp.asarray(sel, F32), precision=lax.Precision.HIGHEST)
    cb = jnp.where(jnp.asarray(col_ok)[None, None, None], cb * math.log2(math.e), NEG)
    cbx = jnp.concatenate([cb, jnp.full((depth, heads, 1, GRID_W, GRID_W), NEG, F32)], axis=2)
    slots = []
    for rs_off, r_off in pats:
        i = np.arange(rq)[:, None]
        w = np.arange(kr)[None, :]
        rso = np.asarray(rs_off)[:, None]
        row_ok = (w >= rso) & (w < rso + wr)
        slots.append(np.where(row_ok, w - r_off - i + NA_ROWS - 1, nr))
    return cbx, np.stack(slots).reshape(-1).astype(np.int32)


def _na_tables(rpb_all, seq, rq):
    kbs, pids, pats, kr, wr = _na_plan(seq // GRID_W, rq)
    cbx, slots = _na_bias(rpb_all, pats, rq, kr, wr)
    return dict(kbs=jnp.asarray(kbs), pids=jnp.asarray(pids), slots=jnp.asarray(slots), cbx=cbx, kr=kr, rq=rq,
                npat=len(pats))


def _na_latent(q, k, v, kc, vc, tabs, layer, *, batch, seq, n_ctx):
    rows = seq // GRID_W
    cbx, rq, kr = tabs["cbx"], tabs["rq"], tabs["kr"]
    heads = cbx.shape[1]
    nblk = rows // rq
    nq = rq * GRID_W
    nkeys = kr * GRID_W
    w = q.shape[1]
    nsub = 4 if nblk % 4 == 0 else 1
    nstep = nblk // nsub
    kern = functools.partial(_na_kernel, nkeys=nkeys, heads=heads, nsub=nsub)
    grid_spec = pltpu.PrefetchScalarGridSpec(
        num_scalar_prefetch=3,
        grid=(batch, nstep),
        in_specs=[pl.BlockSpec((nsub * nq, w), lambda b, i, *_: (b * nstep + i, 0)),
                  pl.BlockSpec((seq, w), lambda b, i, *_: (b, 0)),
                  pl.BlockSpec((seq, w), lambda b, i, *_: (b, 0)),
                  pl.BlockSpec((n_ctx, w), lambda b, i, *_: (b, 0)),
                  pl.BlockSpec((n_ctx, w), lambda b, i, *_: (b, 0)),
                  pl.BlockSpec((None,) + cbx.shape[1:], lambda b, i, *_: (layer, 0, 0, 0, 0))],
        out_specs=pl.BlockSpec((nsub * nq, w), lambda b, i, *_: (b * nstep + i, 0)),
        scratch_shapes=[pltpu.VMEM((tabs["npat"], heads, nq, nkeys), F32)],
    )
    return pl.pallas_call(
        kern,
        grid_spec=grid_spec,
        out_shape=jax.ShapeDtypeStruct(q.shape, BF),
        compiler_params=_cparams("arbitrary", "arbitrary", vmem=VMEM_LIMIT_NA),
        name="na_latent",
    )(tabs["kbs"], tabs["pids"], tabs["slots"], q, k, v, kc, vc, cbx)


def _na_ctx_kernel(q_ref, k_ref, v_ref, o_ref, *, heads):
    nq = q_ref.shape[0]
    lane = lax.broadcasted_iota(jnp.int32, (nq, LANES), 1)
    low = lane < HEAD_DIM
    dn = (((1,), (1,)), ((), ()))
    for hp in range(heads // 2):
        sl = slice(hp * LANES, (hp + 1) * LANES)
        q2 = q_ref[:, sl]
        k2 = k_ref[:, sl]
        v2 = v_ref[:, sl]
        outs = []
        for hh in range(2):
            qm = jnp.where(low if hh == 0 else jnp.logical_not(low), q2, jnp.zeros_like(q2))
            s = lax.dot_general(qm, k2, dn, preferred_element_type=F32)
            outs.append(_softmax_pv([(s, v2)], (nq, 1)))
        o_ref[:, sl] = jnp.where(low, outs[0], outs[1]).astype(BF)


def _na_ctx(qc, kc, vc, *, batch, n_ctx, heads):
    w = qc.shape[1]
    spec = pl.BlockSpec((n_ctx, w), lambda b: (b, 0))
    return pl.pallas_call(
        functools.partial(_na_ctx_kernel, heads=heads),
        grid=(batch,),
        in_specs=[spec, spec, spec],
        out_specs=spec,
        out_shape=jax.ShapeDtypeStruct(qc.shape, BF),
        compiler_params=_cparams("parallel"),
        name="na_ctx",
    )(qc, kc, vc)


def _outproj_kernel(z_ref, pf_ref, na_ref, x_ref, gate_ref, gpost_ref,
                    wglu_ref, cs_ref, wfo_ref, wout_ref, o_ref, *, widths, sub):
    w_ssm, w_fn, w_na = widths
    wglu, wfo, wout = (r[...].astype(BF) for r in (wglu_ref, wfo_ref, wout_ref))
    for r0 in range(0, x_ref.shape[0], sub):
        rs = slice(r0, r0 + sub)
        z = jnp.concatenate([z_ref[j, rs, :] for j in range(w_ssm // LANES)], axis=1)
        gl = jnp.dot(z.astype(BF), wglu, preferred_element_type=F32)
        y_ssm = (z * jax.nn.sigmoid(gl)).astype(BF)
        if len(pf_ref.shape) == 3:
            tiles = [pf_ref[c, rs, :] for c in range(pf_ref.shape[0])]
        else:
            l2 = pf_ref.shape[1]
            tiles = [jnp.concatenate([pf_ref[c, :, k1, :] for k1 in range(r0 // l2, (r0 + sub) // l2)], axis=0)
                     for c in range(pf_ref.shape[0])]
        pf = jnp.concatenate(tiles, axis=1).astype(BF)
        mixed = jnp.dot(pf, cs_ref[...], preferred_element_type=F32).astype(BF)
        y_fft = jnp.dot(mixed, wfo, preferred_element_type=F32).astype(BF)
        y = jnp.concatenate([y_ssm, y_fft, na_ref[rs, :]], axis=1)
        o = jnp.dot(y, wout, preferred_element_type=F32)
        ms = jnp.mean(o * o, axis=-1, keepdims=True)
        o_ref[rs, :] = x_ref[rs, :] + gate_ref[...] * (o * lax.rsqrt(ms + EPS) * gpost_ref[...])


def _outproj(z3, pf, yna, x2d, mod3, g_post, w_glu, cs, w_fo, w_out, *,
             layer, tm, tiles_per_batch, mod_row, widths):
    n, d = x2d.shape
    w_ssm, w_fn, w_na = widths
    if mod_row is None:
        row = lambda i: i // tiles_per_batch
    else:
        row = lambda i: mod_row
    kern = functools.partial(_outproj_kernel, widths=widths, sub=min(tm, 512))
    return pl.pallas_call(
        kern,
        grid=(n // tm,),
        in_specs=[pl.BlockSpec((w_ssm // LANES, tm, LANES), lambda i: (0, i, 0)),
                  (pl.BlockSpec((None, pf.shape[1], tm, LANES),
                                lambda i: (i // tiles_per_batch, 0, i % tiles_per_batch, 0))
                   if pf.ndim == 4 else
                   pl.BlockSpec((None, pf.shape[1], pf.shape[2], tm // pf.shape[2], LANES),
                                lambda i: (i // tiles_per_batch, 0, 0, i % tiles_per_batch, 0))),
                  pl.BlockSpec((tm, w_na), lambda i: (i, 0)),
                  pl.BlockSpec((tm, d), lambda i: (i, 0)),
                  pl.BlockSpec((None, 1, d), lambda i: (row(i), 0, 2)),
                  pl.BlockSpec((None, 1, d), lambda i: (layer, 0, 0)),
                  _layer_resident(w_glu, layer),
                  _resident(cs.shape, lambda i: (0, 0)),
                  _layer_resident(w_fo, layer),
                  _layer_resident(w_out, layer)],
        out_specs=pl.BlockSpec((tm, d), lambda i: (i, 0)),
        out_shape=jax.ShapeDtypeStruct((n, d), F32),
        compiler_params=_cparams("parallel"),
        name="outproj",
    )(z3, pf, yna, x2d, mod3, g_post.reshape(-1, 1, d), w_glu, cs, w_fo, w_out)


def _ffn_kernel(x_ref, sh_ref, sc_ref, gate_ref, gpre_ref, gpost_ref, wg_ref, wu_ref, wd_ref,
                o_ref, a_ref, *, sub, chunk):
    dff = wg_ref.shape[1]
    for r0 in range(0, x_ref.shape[0], sub):
        rs = slice(r0, r0 + sub)
        x = x_ref[rs, :]
        ms = jnp.mean(x * x, axis=-1, keepdims=True)
        m = ((x * lax.rsqrt(ms + EPS) * gpre_ref[...]) * (1.0 + sc_ref[...]) + sh_ref[...]).astype(BF)
        for c0 in range(0, dff, chunk):
            sl = slice(c0, min(c0 + chunk, dff))
            g = jnp.dot(m, wg_ref[:, sl], preferred_element_type=F32)
            u = jnp.dot(m, wu_ref[:, sl], preferred_element_type=F32)
            a_ref[rs, sl] = (g * jax.nn.sigmoid(g) * u).astype(BF)
        y = jnp.dot(a_ref[rs, :], wd_ref[...], preferred_element_type=F32)
        ms2 = jnp.mean(y * y, axis=-1, keepdims=True)
        o_ref[rs, :] = x + gate_ref[...] * (y * lax.rsqrt(ms2 + EPS) * gpost_ref[...])


FFN_WEIGHT_STEPS = 8


def _ffn_staged_kernel(*refs, sub, chunk, n_lat_tiles, with_ctx):
    if with_ctx:
        (x_ref, xc_ref, sh_ref, sc_ref, gate_ref, gpre_ref, gpost_ref, wg_ref, wu_ref, wd_ref,
         o_ref, oc_ref, a_ref, wg_s, wu_s, wd_s) = refs
    else:
        (x_ref, sh_ref, sc_ref, gate_ref, gpre_ref, gpost_ref, wg_ref, wu_ref, wd_ref,
         o_ref, a_ref, wg_s, wu_s, wd_s) = refs
    i = pl.program_id(0)
    common = (sh_ref, sc_ref, gate_ref, gpre_ref, gpost_ref, wg_s, wu_s, wd_s)

    @pl.when(i < FFN_WEIGHT_STEPS)
    def _cast_weights():
        for src, dst in ((wg_ref, wg_s), (wu_ref, wu_s), (wd_ref, wd_s)):
            rows = src.shape[0]
            dst[pl.ds(pl.multiple_of(i * rows, rows), rows), :] = src[...].astype(BF)

    @pl.when((i >= FFN_WEIGHT_STEPS) & (i < FFN_WEIGHT_STEPS + n_lat_tiles))
    def _latent():
        _ffn_kernel(x_ref, *common, o_ref, a_ref, sub=sub, chunk=chunk)

    if with_ctx:
        @pl.when(i >= FFN_WEIGHT_STEPS + n_lat_tiles)
        def _context():
            _ffn_kernel(xc_ref, *common, oc_ref, a_ref, sub=min(sub, xc_ref.shape[0]), chunk=chunk)


def _ffn(x2d, xc2d, mod3, g_pre, g_post, wg, wu, wd, *, layer, tm, tiles_per_batch, ctx_row):
    n, d = x2d.shape
    dff = wg.shape[2]
    nl = n // tm
    nw = FFN_WEIGHT_STEPS
    with_ctx = xc2d is not None
    gains = (g_pre.reshape(-1, 1, d), g_post.reshape(-1, 1, d))
    slab = lambda i: (layer, jnp.minimum(i, nw - 1), 0)
    weights = [pl.BlockSpec((None, d // nw, dff), slab), pl.BlockSpec((None, d // nw, dff), slab),
               pl.BlockSpec((None, dff // nw, d), slab)]
    gain_specs = [pl.BlockSpec((None, 1, d), lambda i: (layer, 0, 0))] * 2
    tile = lambda i: jnp.clip(i - nw, 0, nl - 1)
    row = lambda i: jnp.where(i < nw + nl, tile(i) // tiles_per_batch, ctx_row)
    lat = lambda i: (tile(i), 0)
    mods = [pl.BlockSpec((None, 1, d), functools.partial(lambda i, c: (row(i), 0, c), c=c)) for c in (3, 4, 5)]
    in_specs = [pl.BlockSpec((tm, d), lat)]
    out_specs = [pl.BlockSpec((tm, d), lat)]
    out_shape = [jax.ShapeDtypeStruct((n, d), F32)]
    args = [x2d]
    if with_ctx:
        nc = xc2d.shape[0]
        ctx = lambda i: (jnp.maximum(i - nw - nl, 0), 0)
        in_specs.append(pl.BlockSpec((nc, d), ctx, pipeline_mode=pl.Buffered(1)))
        out_specs.append(pl.BlockSpec((nc, d), ctx))
        out_shape.append(jax.ShapeDtypeStruct((nc, d), F32))
        args.append(xc2d)
    outs = pl.pallas_call(
        functools.partial(_ffn_staged_kernel, sub=min(tm, 512), chunk=3 * MXU_TILE, n_lat_tiles=nl,
                          with_ctx=with_ctx),
        grid=(nw + nl + int(with_ctx),),
        in_specs=in_specs + mods + gain_specs + weights,
        out_specs=out_specs,
        out_shape=out_shape,
        scratch_shapes=[pltpu.VMEM((tm, dff), BF), pltpu.VMEM((d, dff), BF), pltpu.VMEM((d, dff), BF),
                        pltpu.VMEM((dff, d), BF)],
        compiler_params=_cparams("arbitrary"),
        name="ffn_staged",
    )(*args, mod3, mod3, mod3, *gains, wg, wu, wd)
    return (outs[0], outs[1]) if with_ctx else (outs[0], None)


def _layer(x2d, xc2d, p, w, layer, *, batch, seq, n_ctx, last, rope_tabs, consts):
    d = x2d.shape[1]
    w_ssm = consts["w_ssm"]
    w_fn = w["w_fourier"].shape[1]
    w_na = (w["w_in"].shape[2] - w_ssm - w_fn) // 3
    widths = (w_ssm, w_fn, w_na)
    heads = w_na // HEAD_DIM
    tm = min(512, seq)
    tmc = min(512, batch * n_ctx)

    mod = w["mod"][layer]
    mod3 = mod.reshape(mod.shape[0], 1, 6 * d)

    cos, sin = rope_tabs
    tm2 = min(2 * tm, seq)
    tm4 = min(4 * tm, seq)
    (u3, f, q, k, v), (u3c, fc, qc, kc, vc) = _inproj(
        x2d, xc2d, mod3, p["g_pre_mix"], w["w_in"], cos, sin, layer=layer, tm=tm4,
        tiles_per_batch=seq // tm4, ctx_row=batch, widths=widths)

    z3, zc3 = _ssm(u3, u3c, *w["ssm"], layer=layer, batch=batch, seq=seq, n_ctx=n_ctx)

    l2 = consts["l2"]
    l1 = seq // l2
    g = _dft_a(consts["dft_a"], f.reshape(batch, l2, l1 * w_fn))
    pf = _dft_b(consts["dft_b"], g.reshape(batch, 2, l2, l1, w_fn), kb=min(16, l2))

    yna = _na_latent(q, k, v, kc, vc, consts["na"], layer, batch=batch, seq=seq, n_ctx=n_ctx)

    w_glu, w_fo, w_out = w["w_glu"], w["w_fourier"], w["w_out"]
    x2d = _outproj(z3, pf, yna, x2d, mod3, p["g_post_mix"], w_glu, consts["cs_lat"], w_fo, w_out,
                   layer=layer, tm=tm2, tiles_per_batch=seq // tm2, mod_row=None, widths=widths)
    if not last:
        pfc = _dft_ctx(consts["dft_ctx"], fc.reshape(batch, n_ctx, w_fn))
        ynac = _na_ctx(qc, kc, vc, batch=batch, n_ctx=n_ctx, heads=heads)
        tc = min(tmc, n_ctx)
        xc2d = _outproj(zc3, pfc, ynac, xc2d, mod3, p["g_post_mix"], w_glu, consts["cs_ctx"], w_fo, w_out,
                        layer=layer, tm=tc, tiles_per_batch=n_ctx // tc, mod_row=batch, widths=widths)
    else:
        xc2d = None
    wg, wu, wd = w["w_ffn_gate"], w["w_ffn_up"], w["w_ffn_down"]
    return _ffn(x2d, xc2d, mod3, p["g_pre_ffn"], p["g_post_ffn"], wg, wu, wd,
                layer=layer, tm=tm2, tiles_per_batch=seq // tm2, ctx_row=batch)


def kernel(x, c, ctx, c_ctx, w_mod, b_mod, g_pre_mix, g_post_mix, w_in, ssm_a_re, ssm_a_im, ssm_log_dt,
           ssm_b_re, ssm_b_im, ssm_c_re, ssm_c_im, ssm_d, w_glu, w_fourier, na_rpb, w_out, g_pre_ffn,
           g_post_ffn, w_ffn_gate, w_ffn_up, w_ffn_down):
    batch, seq, d = x.shape
    n_ctx = ctx.shape[1]
    depth = w_mod.shape[0]
    w_fn = w_fourier.shape[1]
    params = dict(g_pre_mix=g_pre_mix, g_post_mix=g_post_mix, g_pre_ffn=g_pre_ffn, g_post_ffn=g_post_ffn)
    seg_len = (seq + 2 * n_ctx) // SSM_T // SSM_SEG
    ssm_ops = jax.vmap(functools.partial(_ssm_weights, seg_len=seg_len))(
        ssm_a_re, ssm_a_im, ssm_log_dt, ssm_b_re, ssm_b_im, ssm_c_re, ssm_c_im, ssm_d)
    nrow = -(-(batch + 1) // 8) * 8
    cstack = jnp.concatenate([c, c_ctx[None, :], jnp.zeros((nrow - batch - 1, d), c.dtype)], axis=0)
    weights = dict(mod=_mod_rows(cstack, w_mod, b_mod), ssm=ssm_ops,
                   w_in=w_in, w_glu=w_glu, w_fourier=w_fourier, w_out=w_out,
                   w_ffn_gate=w_ffn_gate,
                   w_ffn_up=w_ffn_up, w_ffn_down=w_ffn_down)

    l2 = 64 if seq % (64 * 8) == 0 else 8
    l1 = seq // l2
    consts = dict(
        l2=l2,
        w_ssm=ssm_d.shape[1],
        na=_na_tables(na_rpb, seq, 4),
        dft_a=_dft_rows(l2).astype(BF),
        dft_b=_dft_b_mats(seq, l1, l2).astype(BF),
        dft_ctx=_dft_rows(n_ctx).astype(BF),
        cs_lat=_channel_dft(w_fn, seq).astype(BF),
        cs_ctx=_channel_dft(w_fn, n_ctx).astype(BF),
    )
    rope_tabs = _rope_tables(seq)

    x2d = x.reshape(batch * seq, d)
    xc2d = ctx.reshape(batch * n_ctx, d)
    for layer in range(depth):
        x2d, xc2d = _layer(x2d, xc2d, params, weights, layer, batch=batch, seq=seq, n_ctx=n_ctx,
                           last=(layer == depth - 1), rope_tabs=rope_tabs, consts=consts)
    return x2d.reshape(batch, seq, d)
```

```python
import functools
import math

import numpy as np
import jax
import jax.numpy as jnp
from jax import lax
from jax.experimental import pallas as pl
from jax.experimental.pallas import tpu as pltpu

BF = jnp.bfloat16
F32 = jnp.float32

EPS = 1e-6
GRID_W = 64
HEAD_DIM = 64
NA_ROWS = 8
NA_COLS = 16
ROPE_BASE = 10000.0
SSM_GROUP = 16
SSM_STATE = 64
FNET_GROUP = 64
LANES = 128
MXU_TILE = 256
SSM_T = 8
SSM_SEG = 8
SSM_STAGES = (4, 2)
NEG = -1e30
VMEM_LIMIT = 56 * 1024 * 1024
VMEM_LIMIT_NA = 60000 * 1024


def _cparams(*sem, vmem=VMEM_LIMIT):
    return pltpu.CompilerParams(dimension_semantics=sem, vmem_limit_bytes=vmem)


def _resident(shape, index_map):
    return pl.BlockSpec(shape, index_map, pipeline_mode=pl.Buffered(1))


def _layer_resident(w_all, layer):
    return pl.BlockSpec((None,) + w_all.shape[1:], lambda *_: (layer, 0, 0), pipeline_mode=pl.Buffered(1))


def _mod_kernel(c_ref, w_ref, b_ref, o_ref):
    cs = c_ref[...]
    s = (cs * jax.nn.sigmoid(cs)).astype(BF)
    o_ref[...] = jnp.dot(s, w_ref[...].astype(BF), preferred_element_type=F32) + b_ref[...]


def _mod_rows(cstack, w_mod, b_mod):
    rows, d = cstack.shape
    depth, _, n = w_mod.shape
    bn = 2 * d if n % (2 * d) == 0 else d
    return pl.pallas_call(
        _mod_kernel,
        grid=(depth, n // bn),
        in_specs=[pl.BlockSpec((rows, d), lambda l, i: (0, 0)),
                  pl.BlockSpec((None, d, bn), lambda l, i: (l, 0, i)),
                  pl.BlockSpec((None, 1, bn), lambda l, i: (l, 0, i))],
        out_specs=pl.BlockSpec((None, rows, bn), lambda l, i: (l, 0, i)),
        out_shape=jax.ShapeDtypeStruct((depth, rows, n), F32),
        compiler_params=_cparams("parallel", "parallel"),
        name="mod_rows",
    )(cstack, w_mod, b_mod.reshape(depth, 1, n))


def _rope_tile(t, cos, sin, first):
    partner = jnp.where(first, pltpu.roll(t, LANES - 16, 1), pltpu.roll(t, 16, 1))
    return t * cos + partner * sin


def _inproj_kernel(x_ref, sh_ref, sc_ref, g_ref, w_ref, cos_ref, sin_ref,
                   u_ref, f_ref, q_ref, k_ref, v_ref, *, widths, rope, sub):
    w_ssm, w_fn, w_na = widths
    scale = HEAD_DIM ** -0.5 * math.log2(math.e)
    w = w_ref[...].astype(BF)
    for r0 in range(0, x_ref.shape[0], sub):
        rs = slice(r0, r0 + sub)
        x = x_ref[rs, :]
        ms = jnp.mean(x * x, axis=-1, keepdims=True)
        xn = x * lax.rsqrt(ms + EPS) * g_ref[...]
        m = (xn * (1.0 + sc_ref[...]) + sh_ref[...]).astype(BF)

        h = jnp.dot(m, w, preferred_element_type=F32)

        def proj(lo, n, h=h):
            return h[:, lo:lo + n]

        u = proj(0, w_ssm)
        for j in range(w_ssm // LANES):
            u_ref[j, rs, :] = u[:, j * LANES:(j + 1) * LANES]
        f_ref[rs, :] = proj(w_ssm, w_fn).astype(BF)
        q = proj(w_ssm + w_fn, w_na)
        k = proj(w_ssm + w_fn + w_na, w_na)
        v_ref[rs, :] = proj(w_ssm + w_fn + 2 * w_na, w_na).astype(BF)
        if rope:
            cos = cos_ref[rs, :]
            sin = sin_ref[rs, :]
            lane = lax.broadcasted_iota(jnp.int32, cos.shape, 1)
            first = (lane % 32) < 16
            for j in range(w_na // LANES):
                sl = slice(j * LANES, (j + 1) * LANES)
                q_ref[rs, sl] = (_rope_tile(q[:, sl], cos, sin, first) * scale).astype(BF)
                k_ref[rs, sl] = _rope_tile(k[:, sl], cos, sin, first).astype(BF)
        else:
            q_ref[rs, :] = (q * scale).astype(BF)
            k_ref[rs, :] = k.astype(BF)


def _inproj_pair_kernel(x_ref, xc_ref, sh_ref, sc_ref, g_ref, w_ref, cos_ref, sin_ref, *outs,
                        widths, sub, n_lat_tiles):
    common = (sh_ref, sc_ref, g_ref, w_ref, cos_ref, sin_ref)
    i = pl.program_id(0)

    @pl.when(i < n_lat_tiles)
    def _latent():
        _inproj_kernel(x_ref, *common, *outs[:5], widths=widths, rope=True, sub=sub)

    @pl.when(i >= n_lat_tiles)
    def _context():
        _inproj_kernel(xc_ref, *common, *outs[5:], widths=widths, rope=False, sub=min(sub, xc_ref.shape[0]))


def _inproj(x2d, xc2d, mod3, g_pre, w_in, cos, sin, *, layer, tm, tiles_per_batch, ctx_row, widths):
    n, d = x2d.shape
    nc = xc2d.shape[0]
    w_ssm, w_fn, w_na = widths
    nl = n // tm
    tile = lambda i: jnp.minimum(i, nl - 1)
    row = lambda i: jnp.where(i < nl, tile(i) // tiles_per_batch, ctx_row)
    pos = lambda i: (tile(i) % tiles_per_batch, 0)
    lat = lambda i: (tile(i), 0)
    one = lambda i: (0, 0)

    def outs(rows, idx, idx3):
        specs = [pl.BlockSpec((w_ssm // LANES, rows, LANES), idx3), pl.BlockSpec((rows, w_fn), idx)]
        specs += [pl.BlockSpec((rows, w_na), idx)] * 3
        return specs

    def shapes(rows):
        return ([jax.ShapeDtypeStruct((w_ssm // LANES, rows, LANES), F32), jax.ShapeDtypeStruct((rows, w_fn), BF)]
                + [jax.ShapeDtypeStruct((rows, w_na), BF)] * 3)

    kern = functools.partial(_inproj_pair_kernel, widths=widths, sub=min(tm, 512), n_lat_tiles=nl)
    res = pl.pallas_call(
        kern,
        grid=(nl + 1,),
        in_specs=[pl.BlockSpec((tm, d), lat),
                  pl.BlockSpec((nc, d), one),
                  pl.BlockSpec((None, 1, d), lambda i: (row(i), 0, 0)),
                  pl.BlockSpec((None, 1, d), lambda i: (row(i), 0, 1)),
                  pl.BlockSpec((None, 1, d), lambda i: (layer, 0, 0)),
                  _layer_resident(w_in, layer),
                  pl.BlockSpec((tm, LANES), pos),
                  pl.BlockSpec((tm, LANES), pos)],
        out_specs=outs(tm, lat, lambda i: (0, tile(i), 0)) + outs(nc, one, lambda i: (0, 0, 0)),
        out_shape=shapes(n) + shapes(nc),
        compiler_params=_cparams("arbitrary"),
        name="inproj",
    )(x2d, xc2d, mod3, mod3, g_pre.reshape(-1, 1, d), w_in, cos, sin)
    return res[:5], res[5:]


def _rope_tables(seq):
    t = np.arange(seq)
    row = (t // GRID_W).astype(np.float32)
    col = (t % GRID_W).astype(np.float32)
    quarter = HEAD_DIM // 4
    freqs = (np.float32(ROPE_BASE) ** (-np.arange(quarter, dtype=np.float32) / quarter)).astype(np.float32)
    d = np.arange(LANES) % HEAD_DIM
    use_col = (d // (HEAD_DIM // 2)) == 1
    fidx = d % quarter
    sign = np.where((d % (HEAD_DIM // 2)) // quarter == 0, -1.0, 1.0)
    pos = np.where(use_col[None, :], col[:, None], row[:, None])
    ang = (pos * freqs[fidx][None, :]).astype(np.float32).astype(np.float64)
    return jnp.asarray(np.cos(ang), F32), jnp.asarray(np.sin(ang) * sign[None, :], F32)


def _gelu_tanh(x):
    c = math.sqrt(2.0 / math.pi)
    return x * (0.5 * (1.0 + jnp.tanh(c * (x + 0.044715 * (x * x * x)))))


def _group_of(idx, width, groups):
    shift = width.bit_length() - 1
    assert width == 1 << shift and groups & (groups - 1) == 0
    return lax.bitwise_and(lax.shift_right_logical(idx, shift), groups - 1)


def _same_group(shape, row0, row_width, col0, col_width, groups):
    r = lax.broadcasted_iota(jnp.int32, shape, 0) + row0
    c = lax.broadcasted_iota(jnp.int32, shape, 1) + col0
    return _group_of(r, row_width, groups) == _group_of(c, col_width, groups)


def _swap_stage(tiles, d):
    lane = lax.broadcasted_iota(jnp.int32, tiles[0].shape, 1)
    upper = lax.bitwise_and(lax.shift_right_logical(lane, SSM_GROUP.bit_length() - 1), d) != 0
    out = list(tiles)
    for x in range(len(tiles)):
        if x & d:
            continue
        a, b = tiles[x], tiles[x + d]
        out[x] = jnp.where(upper, pltpu.roll(b, SSM_GROUP * d, 1), a)
        out[x + d] = jnp.where(upper, b, pltpu.roll(a, LANES - SSM_GROUP * d, 1))
    return out


def _ssm_kernel(ul_ref, uc_ref, wc_ref, lag_ref, dc_ref, rexp_ref, cexp_ref, lam_ref, lamseg_ref,
                z_ref, zc_ref, u_s, bm_s, kin_s, cm_s, s_ref, h_ref, e_ref, *, seg_len, n_lat, n_ctx, stages):
    T = SSM_T
    ns = 2 ** len(stages)
    gs = T // ns
    sw = gs * LANES
    rows = u_s.shape[1]
    tps = s_ref.shape[0] // ns
    tpq = tps // 4
    nst = tpq * LANES
    gpt = LANES // SSM_GROUP
    per_q = gpt * SSM_STATE

    def lanes(c):
        return slice(c * LANES, (c + 1) * LANES)

    def col0(s, q):
        return q * per_q + s * nst

    def swapped(tiles, order):
        for d in order:
            tiles = _swap_stage(tiles, d)
        return tiles

    @pl.when(pl.program_id(1) == 0)
    def _expand_operators():
        rexp = rexp_ref[...]
        cexp = cexp_ref[...]
        for s in range(ns):
            wsel = jnp.concatenate([wc_ref[:, col0(s, q):col0(s, q) + nst] for q in range(4)], axis=1)
            blk = jnp.dot(rexp, wsel, preferred_element_type=F32)
            keep = _same_group(blk.shape, 0, SSM_GROUP, 0, SSM_STATE, gs)
            bm_s[s] = jnp.where(keep, blk, 0.0).astype(BF)
            dsel = jnp.concatenate([dc_ref[col0(s, q):col0(s, q) + nst, :] for q in range(4)], axis=0)
            blk = jnp.dot(dsel, cexp, preferred_element_type=F32)
            keep = _same_group(blk.shape, 0, SSM_STATE, 0, SSM_GROUP, gs)
            cm_s[s] = jnp.where(keep, blk, 0.0).astype(BF)
        r16 = lax.broadcasted_iota(jnp.int32, (LANES, SSM_GROUP), 0)
        c16 = lax.broadcasted_iota(jnp.int32, (LANES, SSM_GROUP), 1)
        chan = jnp.where(lax.bitwise_and(r16, SSM_GROUP - 1) == c16, 1.0, 0.0).astype(BF)
        for t in range(T):
            lo = (T - 1 - t) * LANES
            blk = jnp.dot(chan, lag_ref[:, lo:lo + T * LANES], preferred_element_type=F32)
            keep = _same_group(blk.shape, 0, SSM_GROUP, 0, SSM_GROUP, gpt)
            blk = jnp.where(keep, blk, 0.0)
            tiles = swapped([blk[:, lanes(x)] for x in range(T)], stages)
            r0 = (t % gs) * LANES + (t // gs) * gs * SSM_GROUP
            for s in range(ns):
                part = jnp.concatenate(tiles[s * gs:(s + 1) * gs], axis=1)
                kin_s[s, r0:r0 + gs * SSM_GROUP, :] = part[s * gs * SSM_GROUP:(s + 1) * gs * SSM_GROUP, :].astype(BF)

    xs = []
    for t in range(T):
        ct = uc_ref[pl.ds(t, n_ctx, stride=T), :]
        xs.append(jnp.concatenate([ct, ul_ref[pl.ds(t, n_lat, stride=T), :], ct], axis=0))
    xs = swapped(xs, stages)
    for s in range(ns):
        for k in range(gs):
            u_s[s, :, lanes(k)] = xs[s * gs + k].astype(BF)
    for s in range(ns):
        r = jnp.dot(u_s[s], bm_s[s], preferred_element_type=F32)
        for c in range(tps):
            s_ref[s * tps + c] = r[:, lanes(c)]

    chains = []
    for s in range(ns):
        for d in range(2):
            for k in range(tpq):
                cr, ci = s * tps + 2 * d * tpq + k, s * tps + (2 * d + 1) * tpq + k
                lr = slice(col0(s, 2 * d) + k * LANES, col0(s, 2 * d) + (k + 1) * LANES)
                li = slice(col0(s, 2 * d + 1) + k * LANES, col0(s, 2 * d + 1) + (k + 1) * LANES)
                chains.append((cr, ci, d, lr, li))
    nt = ns * tps

    def scan_step(i, carry, record):
        new = list(carry)
        for cr, ci, d, lr_sl, li_sl in chains:
            rsl = pl.ds(i if d == 0 else seg_len - 1 - i, SSM_SEG, stride=seg_len)
            hr, hi = carry[cr], carry[ci]
            if record:
                h_ref[cr, rsl, :] = hr
                h_ref[ci, rsl, :] = hi
            lr, li = lam_ref[:, lr_sl], lam_ref[:, li_sl]
            new[cr] = lr * hr - li * hi + s_ref[cr, rsl, :]
            new[ci] = lr * hi + li * hr + s_ref[ci, rsl, :]
        return tuple(new)

    zero = jnp.zeros((SSM_SEG, LANES), F32)
    fin = lax.fori_loop(0, seg_len, lambda i, c: scan_step(i, c, False), (zero,) * nt, unroll=4)
    for c in range(nt):
        e_ref[c, 0:SSM_SEG, :] = fin[c]

    for cr, ci, d, lr_sl, li_sl in chains:
        lr, li = lamseg_ref[:, lr_sl], lamseg_ref[:, li_sl]
        er = jnp.zeros((1, LANES), F32)
        ei = jnp.zeros((1, LANES), F32)
        for n in range(SSM_SEG):
            g = n if d == 0 else SSM_SEG - 1 - n
            e_ref[cr, SSM_SEG + g:SSM_SEG + g + 1, :] = er
            e_ref[ci, SSM_SEG + g:SSM_SEG + g + 1, :] = ei
            fr, fi = e_ref[cr, g:g + 1, :], e_ref[ci, g:g + 1, :]
            er, ei = lr * er - li * ei + fr, lr * ei + li * er + fi

    ent = tuple(e_ref[c, SSM_SEG:2 * SSM_SEG, :] for c in range(nt))
    lax.fori_loop(0, seg_len, lambda i, c: scan_step(i, c, True), ent, unroll=2)

    ys, ycs = [], []
    for s in range(ns):
        hcat = jnp.concatenate([h_ref[s * tps + c].astype(BF) for c in range(tps)], axis=1)
        y = _gelu_tanh(jnp.dot(u_s[s], kin_s[s], preferred_element_type=F32)
                       + jnp.dot(hcat, cm_s[s], preferred_element_type=F32))
        ys += [y[:, lanes(k)] for k in range(gs)]
        hc = jnp.concatenate(
            [h_ref[s * tps + c, 0:n_ctx, :].astype(BF) for c in range(2 * tpq)]
            + [h_ref[s * tps + c, rows - n_ctx:rows, :].astype(BF) for c in range(2 * tpq, tps)], axis=1)
        yc = _gelu_tanh(jnp.dot(u_s[s, 0:n_ctx, :], kin_s[s], preferred_element_type=F32)
                        + jnp.dot(hc, cm_s[s], preferred_element_type=F32))
        ycs += [yc[:, lanes(k)] for k in range(gs)]
    ys = swapped(ys, stages[::-1])
    ycs = swapped(ycs, stages[::-1])
    for t in range(T):
        z_ref[pl.ds(t, n_lat, stride=T), :] = ys[t][n_ctx:n_ctx + n_lat, :]
        zc_ref[pl.ds(t, n_ctx, stride=T), :] = ycs[t]


def _ssm_weights(a_re, a_im, log_dt, b_re, b_im, c_re, c_im, d_skip, seg_len):
    T = SSM_T
    g, p = a_re.shape[1], a_re.shape[2]
    hc = b_re.shape[-1]
    gpt = LANES // hc
    J = g // gpt
    lam = lax.complex(a_re.astype(F32), a_im.astype(F32))
    dt = jnp.exp(log_dt.astype(F32))[..., None]
    ldt = lam * dt
    lam_bar = jnp.exp(ldt)
    bbar = ((lam_bar - 1) / lam)[..., None] * lax.complex(b_re.astype(F32), b_im.astype(F32))
    cmat = lax.complex(c_re.astype(F32), c_im.astype(F32))
    kk = jnp.arange(T + 1, dtype=F32)
    pw = jnp.exp(ldt[:, None] * kk[None, :, None, None])

    wf = pw[0, :T][::-1][:, :, None, :] * jnp.swapaxes(bbar[0], 1, 2)[None]
    wb = pw[1, :T][:, :, None, :] * jnp.swapaxes(bbar[1], 1, 2)[None]
    wq = jnp.stack([wf.real, wf.imag, wb.real, wb.imag]).reshape(4, T, J, gpt, hc, p)
    wc = wq.transpose(2, 1, 4, 0, 3, 5).reshape(J, T * hc, 4 * gpt * p)

    df = cmat[0][None] * pw[0, 1:][:, :, None, :]
    db = cmat[1][None] * pw[1, 1:][::-1][:, :, None, :]
    dq = jnp.stack([df.real, -df.imag, db.real, -db.imag]).reshape(4, T, J, gpt, hc, p)
    dc = dq.transpose(2, 0, 3, 5, 1, 4).reshape(J, 4 * gpt * p, T * hc)

    mf = jnp.einsum('gcp,kgp,gph->khgc', cmat[0], pw[0, :T], bbar[0]).real
    mb = jnp.einsum('gcp,kgp,gph->khgc', cmat[1], pw[1, :T], bbar[1]).real
    skip = jnp.eye(hc, dtype=F32)[:, None, :] * d_skip.astype(F32).reshape(g, hc)[None]
    m0 = mf[0] + mb[0] + skip
    lags = jnp.concatenate([mb[1:][::-1], m0[None], mf[1:]], axis=0)
    kc = lags.reshape(2 * T - 1, hc, J, LANES).transpose(2, 1, 0, 3).reshape(J, hc, (2 * T - 1) * LANES)

    def lam_rows(power):
        lp = jnp.exp(ldt * power)
        lt = lp.reshape(2, J, gpt * p)
        return jnp.concatenate([lt[0].real, lt[0].imag, lt[1].real, lt[1].imag], axis=-1)[:, None, :]

    lam_t = jnp.broadcast_to(lam_rows(float(T)), (J, SSM_SEG, 4 * gpt * p))
    return wc.astype(BF), kc.astype(BF), dc.astype(BF), lam_t, lam_rows(float(T * seg_len))


def _ssm_expanders(gs):
    r = np.arange(gs * LANES)
    k, slot, ch = r // LANES, (r % LANES) // SSM_GROUP, r % SSM_GROUP
    t = (slot // gs) * gs + k
    rexp = np.zeros((gs * LANES, SSM_T * SSM_GROUP), np.float32)
    rexp[r, t * SSM_GROUP + ch] = 1.0
    return jnp.asarray(rexp, BF), jnp.asarray(rexp.T, BF)


def _ssm(u3, u3c, wc, kc, dc, lam, lamseg, *, layer, batch, seq, n_ctx):
    T = SSM_T
    J = u3.shape[0]
    ns = wc.shape[-1]
    nsets = 2 ** len(SSM_STAGES)
    gs = T // nsets
    sw = gs * LANES
    n_lat, n_c = seq // T, n_ctx // T
    rows = n_lat + 2 * n_c
    seg_len = rows // SSM_SEG
    assert rows % SSM_SEG == 0 and n_c % 16 == 0
    rexp, cexp = _ssm_expanders(gs)
    kern = functools.partial(_ssm_kernel, seg_len=seg_len, n_lat=n_lat, n_ctx=n_c, stages=SSM_STAGES)
    const = lambda j, b: (0, 0)
    per_j = lambda j, b: (layer, j, 0, 0)
    per_jb = lambda j, b: (j, b, 0)
    return pl.pallas_call(
        kern,
        grid=(J, batch),
        in_specs=[pl.BlockSpec((None, seq, LANES), per_jb),
                  pl.BlockSpec((None, n_ctx, LANES), per_jb),
                  pl.BlockSpec((None, None) + wc.shape[2:], per_j),
                  pl.BlockSpec((None, None) + kc.shape[2:], per_j),
                  pl.BlockSpec((None, None) + dc.shape[2:], per_j),
                  pl.BlockSpec(rexp.shape, const),
                  pl.BlockSpec(cexp.shape, const),
                  pl.BlockSpec((None, None, SSM_SEG, ns), per_j),
                  pl.BlockSpec((None, None, 1, ns), per_j)],
        out_specs=[pl.BlockSpec((None, seq, LANES), per_jb),
                   pl.BlockSpec((None, n_ctx, LANES), per_jb)],
        out_shape=[jax.ShapeDtypeStruct(u3.shape, F32),
                   jax.ShapeDtypeStruct(u3c.shape, F32)],
        scratch_shapes=[pltpu.VMEM((nsets, rows, sw), BF),
                        pltpu.VMEM((nsets, sw, ns // nsets), BF),
                        pltpu.VMEM((nsets, sw, sw), BF),
                        pltpu.VMEM((nsets, ns // nsets, sw), BF),
                        pltpu.VMEM((ns // LANES, rows, LANES), F32),
                        pltpu.VMEM((ns // LANES, rows, LANES), F32),
                        pltpu.VMEM((ns // LANES, 2 * SSM_SEG, LANES), F32)],
        compiler_params=_cparams("arbitrary", "arbitrary"),
        name="ssm_chunk_scan",
    )(u3, u3c, wc, kc, dc, rexp, cexp, lam, lamseg)


def _dft_a_kernel(w_ref, x_ref, o_ref):
    o_ref[...] = jnp.dot(w_ref[...], x_ref[...], preferred_element_type=F32).astype(BF)


def _dft_rows(n):
    k = np.arange(n)
    m = (k[:, None] * k[None, :]) % n
    ang = m * (2.0 * math.pi / n)
    return jnp.asarray(np.concatenate([np.cos(ang), -np.sin(ang)], axis=0), BF)


def _dft_a(w, x3):
    B, n, cols = x3.shape
    tn = min(cols, 16384)
    return pl.pallas_call(
        _dft_a_kernel,
        grid=(B, cols // tn),
        in_specs=[pl.BlockSpec((2 * n, n), lambda b, i: (0, 0)),
                  pl.BlockSpec((None, n, tn), lambda b, i: (b, 0, i))],
        out_specs=pl.BlockSpec((None, 2 * n, tn), lambda b, i: (b, 0, i)),
        out_shape=jax.ShapeDtypeStruct((B, 2 * n, cols), BF),
        compiler_params=_cparams("parallel", "parallel"),
        name="dft_stage_a",
    )(w, x3)


def _store_lane_tiles(o_ref, p, n, wf, idx):
    per = wf // LANES
    for c in range(2 * per):
        part = p[(c // per) * n:(c // per + 1) * n, (c % per) * LANES:(c % per + 1) * LANES]
        o_ref[(c,) + idx] = part


def _dft_b_kernel(a_ref, g_ref, o_ref, *, kb, l1, wf):
    for i in range(kb):
        g = jnp.concatenate([g_ref[0, i], g_ref[1, i]], axis=0)
        p = jnp.dot(a_ref[i], g, preferred_element_type=F32)
        _store_lane_tiles(o_ref, p, l1, wf, (i,))


def _dft_b_mats(seq, l1, l2):
    k2 = np.arange(l2)[:, None, None]
    k1 = np.arange(l1)[None, :, None]
    j1 = np.arange(l1)[None, None, :]
    m = ((k1 * l2 + k2) * j1) % seq
    ang = m * (2.0 * math.pi / seq)
    ar, ai = np.cos(ang).astype(np.float32), -np.sin(ang).astype(np.float32)
    top = np.concatenate([ar, -ai], axis=2)
    bot = np.concatenate([ai, ar], axis=2)
    return jnp.asarray(np.concatenate([top, bot], axis=1), BF)


def _dft_b(amat, g5, *, kb):
    B, _, l2, l1, wf = g5.shape
    nlt = 2 * wf // LANES
    kern = functools.partial(_dft_b_kernel, kb=kb, l1=l1, wf=wf)
    return pl.pallas_call(
        kern,
        grid=(B, l2 // kb),
        in_specs=[pl.BlockSpec((kb, 2 * l1, 2 * l1), lambda b, i: (i, 0, 0)),
                  pl.BlockSpec((None, 2, kb, l1, wf), lambda b, i: (b, 0, i, 0, 0))],
        out_specs=pl.BlockSpec((None, nlt, kb, l1, LANES), lambda b, i: (b, 0, i, 0, 0)),
        out_shape=jax.ShapeDtypeStruct((B, nlt, l2, l1, LANES), F32),
        compiler_params=_cparams("parallel", "parallel"),
        name="dft_stage_b",
    )(amat, g5)


def _dft_ctx_kernel(w_ref, x_ref, o_ref, *, n, wf):
    p = jnp.dot(w_ref[...], x_ref[...], preferred_element_type=F32)
    _store_lane_tiles(o_ref, p, n, wf, ())


def _dft_ctx(w, x3):
    B, n, wf = x3.shape
    nlt = 2 * wf // LANES
    kern = functools.partial(_dft_ctx_kernel, n=n, wf=wf)
    return pl.pallas_call(
        kern,
        grid=(B,),
        in_specs=[pl.BlockSpec((2 * n, n), lambda b: (0, 0)),
                  pl.BlockSpec((None, n, wf), lambda b: (b, 0, 0))],
        out_specs=pl.BlockSpec((None, nlt, n, LANES), lambda b: (b, 0, 0, 0)),
        out_shape=jax.ShapeDtypeStruct((B, nlt, n, LANES), F32),
        compiler_params=_cparams("parallel"),
        name="dft_ctx",
    )(w, x3)


def _channel_dft(wf, seq):
    c = np.arange(wf)
    same = (c[:, None] // FNET_GROUP) == (c[None, :] // FNET_GROUP)
    m = ((c[:, None] % FNET_GROUP) * (c[None, :] % FNET_GROUP)) % FNET_GROUP
    ang = m * (2.0 * math.pi / FNET_GROUP)
    mask = same / math.sqrt(seq * FNET_GROUP)
    return jnp.asarray(np.concatenate([np.cos(ang) * mask, np.sin(ang) * mask], axis=0), BF)


def _softmax_pv(parts, l_shape):
    m = parts[0][0].max(axis=-1, keepdims=True)
    for s, _ in parts[1:]:
        m = jnp.maximum(m, s.max(axis=-1, keepdims=True))
    l = jnp.zeros(l_shape, F32)
    o = None
    for s, v in parts:
        p = jnp.exp2(s - m)
        l = l + p.sum(axis=-1, keepdims=True)
        pv = jnp.dot(p.astype(BF), v, preferred_element_type=F32)
        o = pv if o is None else o + pv
    return o / l


def _na_kernel(kb_ref, pid_ref, dr_ref, q_ref, k_ref, v_ref, kc_ref, vc_ref, cbx_ref, o_ref, bias_ref,
               *, nkeys, heads, nsub):
    step = pl.program_id(1)
    nq = q_ref.shape[0] // nsub
    rq, kr = nq // GRID_W, nkeys // GRID_W
    lane = lax.broadcasted_iota(jnp.int32, (nq, LANES), 1)
    low = lane < HEAD_DIM
    dn = (((1,), (1,)), ((), ()))

    @pl.when((pl.program_id(0) == 0) & (step == 0))
    def _build_bias():
        for pat in range(bias_ref.shape[0]):
            for qi in range(rq):
                for w in range(kr):
                    a = dr_ref[(pat * rq + qi) * kr + w]
                    for h in range(heads):
                        bias_ref[pat, h, qi * GRID_W:(qi + 1) * GRID_W, w * GRID_W:(w + 1) * GRID_W] = cbx_ref[h, a]

    tasks = [(sb, h) for sb in range(nsub) for h in range(heads)]
    starts = [pl.multiple_of(kb_ref[step * nsub + sb] * GRID_W, GRID_W) for sb in range(nsub)]
    pats = [pid_ref[step * nsub + sb] for sb in range(nsub)]

    n_ctx = kc_ref.shape[0]
    ones = jnp.ones((n_ctx + nkeys, LANES), BF)
    kv = {}

    def keys_values(sb, hp):
        if (sb, hp) not in kv:
            sl = slice(hp * LANES, (hp + 1) * LANES)
            win = pl.ds(starts[sb], nkeys)
            k_all = jnp.concatenate([kc_ref[:, sl], k_ref[win, sl]], axis=0)
            v_all = jnp.concatenate([jnp.concatenate([vc_ref[:, sl], v_ref[win, sl]], axis=0), ones], axis=1)
            kv[sb, hp] = (k_all, v_all)
        return kv[sb, hp]

    def scores(sb, h):
        sl = slice((h // 2) * LANES, (h // 2 + 1) * LANES)
        q2 = q_ref[sb * nq:(sb + 1) * nq, sl]
        qm = jnp.where(low if h % 2 == 0 else jnp.logical_not(low), q2, jnp.zeros_like(q2))
        s = lax.dot_general(qm, keys_values(sb, h // 2)[0], dn, preferred_element_type=F32)
        return s[:, n_ctx:] + bias_ref[pats[sb], h], s[:, :n_ctx]

    def weights(s_w, s_c):
        m = jnp.maximum(s_w.max(axis=-1, keepdims=True), s_c.max(axis=-1, keepdims=True))
        return (jnp.concatenate([jnp.exp2(s_c - m).astype(BF), jnp.exp2(s_w - m).astype(BF)], axis=1),)

    outs = {}

    def finish(sb, h, p):
        sl = slice((h // 2) * LANES, (h // 2 + 1) * LANES)
        ox = jnp.dot(p, keys_values(sb, h // 2)[1], preferred_element_type=F32)
        outs[h % 2] = ox[:, :LANES] / ox[:, LANES:]
        if h % 2 == 1:
            o_ref[sb * nq:(sb + 1) * nq, sl] = jnp.where(low, outs[0], outs[1]).astype(BF)

    nt = len(tasks)
    sc = {0: scores(*tasks[0])}
    if nt > 1:
        sc[1] = scores(*tasks[1])
    pw = {0: weights(*sc.pop(0))}
    for n in range(nt):
        if n + 2 < nt:
            sc[n + 2] = scores(*tasks[n + 2])
        if n + 1 < nt:
            pw[n + 1] = weights(*sc.pop(n + 1))
        finish(*tasks[n], *pw.pop(n))


def _na_plan(rows, rq):
    wr = min(NA_ROWS, rows)
    kr = rq + wr - 1
    nblk = rows // rq
    kbs, pids, pats = [], [], []
    for blk in range(nblk):
        r0 = blk * rq
        rs = [int(np.clip(r0 + i - wr // 2, 0, rows - wr)) for i in range(rq)]
        kb = min(rs[0], rows - kr)
        pat = (tuple(r - kb for r in rs), r0 - kb)
        if pat not in pats:
            pats.append(pat)
        kbs.append(kb)
        pids.append(pats.index(pat))
    return np.asarray(kbs, np.int32), np.asarray(pids, np.int32), pats, kr, wr


def _na_bias(rpb, pats, rq, kr, wr):
    cols = np.arange(GRID_W)
    cstart = np.clip(cols - NA_COLS // 2, 0, GRID_W - NA_COLS)
    kc = np.arange(GRID_W)
    col_ok = (kc[None, :] >= cstart[:, None]) & (kc[None, :] < cstart[:, None] + NA_COLS)
    dc = kc[None, :] - cols[:, None] + NA_COLS - 1
    depth, heads, nr, ncol = rpb.shape
    sel = (dc[:, :, None] == np.arange(ncol)[None, None, :]) & col_ok[:, :, None]
    cb = jnp.einsum('lhab,ckb->lhack', rpb.astype(F32), jnp.asarray(sel, F32), precision=lax.Precision.HIGHEST)
    cb = jnp.where(jnp.asarray(col_ok)[None, None, None], cb * math.log2(math.e), NEG)
    cbx = jnp.concatenate([cb, jnp.full((depth, heads, 1, GRID_W, GRID_W), NEG, F32)], axis=2)
    slots = []
    for rs_off, r_off in pats:
        i = np.arange(rq)[:, None]
        w = np.arange(kr)[None, :]
        rso = np.asarray(rs_off)[:, None]
        row_ok = (w >= rso) & (w < rso + wr)
        slots.append(np.where(row_ok, w - r_off - i + NA_ROWS - 1, nr))
    return cbx, np.stack(slots).reshape(-1).astype(np.int32)


def _na_tables(rpb_all, seq, rq):
    kbs, pids, pats, kr, wr = _na_plan(seq // GRID_W, rq)
    cbx, slots = _na_bias(rpb_all, pats, rq, kr, wr)
    return dict(kbs=jnp.asarray(kbs), pids=jnp.asarray(pids), slots=jnp.asarray(slots), cbx=cbx, kr=kr, rq=rq,
                npat=len(pats))


def _na_latent(q, k, v, kc, vc, tabs, layer, *, batch, seq, n_ctx):
    rows = seq // GRID_W
    cbx, rq, kr = tabs["cbx"], tabs["rq"], tabs["kr"]
    heads = cbx.shape[1]
    nblk = rows // rq
    nq = rq * GRID_W
    nkeys = kr * GRID_W
    w = q.shape[1]
    nsub = 4 if nblk % 4 == 0 else 1
    nstep = nblk // nsub
    kern = functools.partial(_na_kernel, nkeys=nkeys, heads=heads, nsub=nsub)
    grid_spec = pltpu.PrefetchScalarGridSpec(
        num_scalar_prefetch=3,
        grid=(batch, nstep),
        in_specs=[pl.BlockSpec((nsub * nq, w), lambda b, i, *_: (b * nstep + i, 0)),
                  pl.BlockSpec((seq, w), lambda b, i, *_: (b, 0)),
                  pl.BlockSpec((seq, w), lambda b, i, *_: (b, 0)),
                  pl.BlockSpec((n_ctx, w), lambda b, i, *_: (b, 0)),
                  pl.BlockSpec((n_ctx, w), lambda b, i, *_: (b, 0)),
                  pl.BlockSpec((None,) + cbx.shape[1:], lambda b, i, *_: (layer, 0, 0, 0, 0))],
        out_specs=pl.BlockSpec((nsub * nq, w), lambda b, i, *_: (b * nstep + i, 0)),
        scratch_shapes=[pltpu.VMEM((tabs["npat"], heads, nq, nkeys), F32)],
    )
    return pl.pallas_call(
        kern,
        grid_spec=grid_spec,
        out_shape=jax.ShapeDtypeStruct(q.shape, BF),
        compiler_params=_cparams("arbitrary", "arbitrary", vmem=VMEM_LIMIT_NA),
        name="na_latent",
    )(tabs["kbs"], tabs["pids"], tabs["slots"], q, k, v, kc, vc, cbx)


def _na_ctx_kernel(q_ref, k_ref, v_ref, o_ref, *, heads):
    nq = q_ref.shape[0]
    lane = lax.broadcasted_iota(jnp.int32, (nq, LANES), 1)
    low = lane < HEAD_DIM
    dn = (((1,), (1,)), ((), ()))
    for hp in range(heads // 2):
        sl = slice(hp * LANES, (hp + 1) * LANES)
        q2 = q_ref[:, sl]
        k2 = k_ref[:, sl]
        v2 = v_ref[:, sl]
        outs = []
        for hh in range(2):
            qm = jnp.where(low if hh == 0 else jnp.logical_not(low), q2, jnp.zeros_like(q2))
            s = lax.dot_general(qm, k2, dn, preferred_element_type=F32)
            outs.append(_softmax_pv([(s, v2)], (nq, 1)))
        o_ref[:, sl] = jnp.where(low, outs[0], outs[1]).astype(BF)


def _na_ctx(qc, kc, vc, *, batch, n_ctx, heads):
    w = qc.shape[1]
    spec = pl.BlockSpec((n_ctx, w), lambda b: (b, 0))
    return pl.pallas_call(
        functools.partial(_na_ctx_kernel, heads=heads),
        grid=(batch,),
        in_specs=[spec, spec, spec],
        out_specs=spec,
        out_shape=jax.ShapeDtypeStruct(qc.shape, BF),
        compiler_params=_cparams("parallel"),
        name="na_ctx",
    )(qc, kc, vc)


def _outproj_kernel(z_ref, pf_ref, na_ref, x_ref, gate_ref, gpost_ref,
                    wglu_ref, cs_ref, wfo_ref, wout_ref, o_ref, *, widths, sub):
    w_ssm, w_fn, w_na = widths
    wglu, wfo, wout = (r[...].astype(BF) for r in (wglu_ref, wfo_ref, wout_ref))
    for r0 in range(0, x_ref.shape[0], sub):
        rs = slice(r0, r0 + sub)
        z = jnp.concatenate([z_ref[j, rs, :] for j in range(w_ssm // LANES)], axis=1)
        gl = jnp.dot(z.astype(BF), wglu, preferred_element_type=F32)
        y_ssm = (z * jax.nn.sigmoid(gl)).astype(BF)
        if len(pf_ref.shape) == 3:
            tiles = [pf_ref[c, rs, :] for c in range(pf_ref.shape[0])]
        else:
            l2 = pf_ref.shape[1]
            tiles = [jnp.swapaxes(pf_ref[c, :, r0 // l2:(r0 + sub) // l2, :], 0, 1).reshape(sub, LANES)
                     for c in range(pf_ref.shape[0])]
        pf = jnp.concatenate(tiles, axis=1).astype(BF)
        mixed = jnp.dot(pf, cs_ref[...], preferred_element_type=F32).astype(BF)
        y_fft = jnp.dot(mixed, wfo, preferred_element_type=F32).astype(BF)
        y = jnp.concatenate([y_ssm, y_fft, na_ref[rs, :]], axis=1)
        o = jnp.dot(y, wout, preferred_element_type=F32)
        ms = jnp.mean(o * o, axis=-1, keepdims=True)
        o_ref[rs, :] = x_ref[rs, :] + gate_ref[...] * (o * lax.rsqrt(ms + EPS) * gpost_ref[...])


def _outproj(z3, pf, yna, x2d, mod3, g_post, w_glu, cs, w_fo, w_out, *,
             layer, tm, tiles_per_batch, mod_row, widths):
    n, d = x2d.shape
    w_ssm, w_fn, w_na = widths
    if mod_row is None:
        row = lambda i: i // tiles_per_batch
    else:
        row = lambda i: mod_row
    kern = functools.partial(_outproj_kernel, widths=widths, sub=min(tm, 512))
    return pl.pallas_call(
        kern,
        grid=(n // tm,),
        in_specs=[pl.BlockSpec((w_ssm // LANES, tm, LANES), lambda i: (0, i, 0)),
                  (pl.BlockSpec((None, pf.shape[1], tm, LANES),
                                lambda i: (i // tiles_per_batch, 0, i % tiles_per_batch, 0))
                   if pf.ndim == 4 else
                   pl.BlockSpec((None, pf.shape[1], pf.shape[2], tm // pf.shape[2], LANES),
                                lambda i: (i // tiles_per_batch, 0, 0, i % tiles_per_batch, 0))),
                  pl.BlockSpec((tm, w_na), lambda i: (i, 0)),
                  pl.BlockSpec((tm, d), lambda i: (i, 0)),
                  pl.BlockSpec((None, 1, d), lambda i: (row(i), 0, 2)),
                  pl.BlockSpec((None, 1, d), lambda i: (layer, 0, 0)),
                  _layer_resident(w_glu, layer),
                  _resident(cs.shape, lambda i: (0, 0)),
                  _layer_resident(w_fo, layer),
                  _layer_resident(w_out, layer)],
        out_specs=pl.BlockSpec((tm, d), lambda i: (i, 0)),
        out_shape=jax.ShapeDtypeStruct((n, d), F32),
        compiler_params=_cparams("parallel"),
        name="outproj",
    )(z3, pf, yna, x2d, mod3, g_post.reshape(-1, 1, d), w_glu, cs, w_fo, w_out)


def _ffn_kernel(x_ref, sh_ref, sc_ref, gate_ref, gpre_ref, gpost_ref, wg_ref, wu_ref, wd_ref,
                o_ref, a_ref, *, sub, chunk):
    dff = wg_ref.shape[1]
    for r0 in range(0, x_ref.shape[0], sub):
        rs = slice(r0, r0 + sub)
        x = x_ref[rs, :]
        ms = jnp.mean(x * x, axis=-1, keepdims=True)
        m = ((x * lax.rsqrt(ms + EPS) * gpre_ref[...]) * (1.0 + sc_ref[...]) + sh_ref[...]).astype(BF)
        for c0 in range(0, dff, chunk):
            sl = slice(c0, min(c0 + chunk, dff))
            g = jnp.dot(m, wg_ref[:, sl], preferred_element_type=F32)
            u = jnp.dot(m, wu_ref[:, sl], preferred_element_type=F32)
            a_ref[rs, sl] = (g * jax.nn.sigmoid(g) * u).astype(BF)
        y = jnp.dot(a_ref[rs, :], wd_ref[...], preferred_element_type=F32)
        ms2 = jnp.mean(y * y, axis=-1, keepdims=True)
        o_ref[rs, :] = x + gate_ref[...] * (y * lax.rsqrt(ms2 + EPS) * gpost_ref[...])


FFN_WEIGHT_STEPS = 8


def _ffn_staged_kernel(*refs, sub, chunk, n_lat_tiles, with_ctx):
    if with_ctx:
        (x_ref, xc_ref, sh_ref, sc_ref, gate_ref, gpre_ref, gpost_ref, wg_ref, wu_ref, wd_ref,
         o_ref, oc_ref, a_ref, wg_s, wu_s, wd_s) = refs
    else:
        (x_ref, sh_ref, sc_ref, gate_ref, gpre_ref, gpost_ref, wg_ref, wu_ref, wd_ref,
         o_ref, a_ref, wg_s, wu_s, wd_s) = refs
    i = pl.program_id(0)
    common = (sh_ref, sc_ref, gate_ref, gpre_ref, gpost_ref, wg_s, wu_s, wd_s)

    @pl.when(i < FFN_WEIGHT_STEPS)
    def _cast_weights():
        for src, dst in ((wg_ref, wg_s), (wu_ref, wu_s), (wd_ref, wd_s)):
            rows = src.shape[0]
            dst[pl.ds(pl.multiple_of(i * rows, rows), rows), :] = src[...].astype(BF)

    @pl.when((i >= FFN_WEIGHT_STEPS) & (i < FFN_WEIGHT_STEPS + n_lat_tiles))
    def _latent():
        _ffn_kernel(x_ref, *common, o_ref, a_ref, sub=sub, chunk=chunk)

    if with_ctx:
        @pl.when(i >= FFN_WEIGHT_STEPS + n_lat_tiles)
        def _context():
            _ffn_kernel(xc_ref, *common, oc_ref, a_ref, sub=min(sub, xc_ref.shape[0]), chunk=chunk)


def _ffn(x2d, xc2d, mod3, g_pre, g_post, wg, wu, wd, *, layer, tm, tiles_per_batch, ctx_row):
    n, d = x2d.shape
    dff = wg.shape[2]
    nl = n // tm
    nw = FFN_WEIGHT_STEPS
    with_ctx = xc2d is not None
    gains = (g_pre.reshape(-1, 1, d), g_post.reshape(-1, 1, d))
    slab = lambda i: (layer, jnp.minimum(i, nw - 1), 0)
    weights = [pl.BlockSpec((None, d // nw, dff), slab), pl.BlockSpec((None, d // nw, dff), slab),
               pl.BlockSpec((None, dff // nw, d), slab)]
    gain_specs = [pl.BlockSpec((None, 1, d), lambda i: (layer, 0, 0))] * 2
    tile = lambda i: jnp.clip(i - nw, 0, nl - 1)
    row = lambda i: jnp.where(i < nw + nl, tile(i) // tiles_per_batch, ctx_row)
    lat = lambda i: (tile(i), 0)
    mods = [pl.BlockSpec((None, 1, d), functools.partial(lambda i, c: (row(i), 0, c), c=c)) for c in (3, 4, 5)]
    in_specs = [pl.BlockSpec((tm, d), lat)]
    out_specs = [pl.BlockSpec((tm, d), lat)]
    out_shape = [jax.ShapeDtypeStruct((n, d), F32)]
    args = [x2d]
    if with_ctx:
        nc = xc2d.shape[0]
        ctx = lambda i: (jnp.maximum(i - nw - nl, 0), 0)
        in_specs.append(pl.BlockSpec((nc, d), ctx, pipeline_mode=pl.Buffered(1)))
        out_specs.append(pl.BlockSpec((nc, d), ctx))
        out_shape.append(jax.ShapeDtypeStruct((nc, d), F32))
        args.append(xc2d)
    outs = pl.pallas_call(
        functools.partial(_ffn_staged_kernel, sub=min(tm, 512), chunk=3 * MXU_TILE, n_lat_tiles=nl,
                          with_ctx=with_ctx),
        grid=(nw + nl + int(with_ctx),),
        in_specs=in_specs + mods + gain_specs + weights,
        out_specs=out_specs,
        out_shape=out_shape,
        scratch_shapes=[pltpu.VMEM((tm, dff), BF), pltpu.VMEM((d, dff), BF), pltpu.VMEM((d, dff), BF),
                        pltpu.VMEM((dff, d), BF)],
        compiler_params=_cparams("arbitrary"),
        name="ffn_staged",
    )(*args, mod3, mod3, mod3, *gains, wg, wu, wd)
    return (outs[0], outs[1]) if with_ctx else (outs[0], None)


def _layer(x2d, xc2d, p, w, layer, *, batch, seq, n_ctx, last, rope_tabs, consts):
    d = x2d.shape[1]
    w_ssm = consts["w_ssm"]
    w_fn = w["w_fourier"].shape[1]
    w_na = (w["w_in"].shape[2] - w_ssm - w_fn) // 3
    widths = (w_ssm, w_fn, w_na)
    heads = w_na // HEAD_DIM
    tm = min(512, seq)
    tmc = min(512, batch * n_ctx)

    mod = w["mod"][layer]
    mod3 = mod.reshape(mod.shape[0], 1, 6 * d)

    cos, sin = rope_tabs
    tm2 = min(2 * tm, seq)
    (u3, f, q, k, v), (u3c, fc, qc, kc, vc) = _inproj(
        x2d, xc2d, mod3, p["g_pre_mix"], w["w_in"], cos, sin, layer=layer, tm=tm2,
        tiles_per_batch=seq // tm2, ctx_row=batch, widths=widths)

    z3, zc3 = _ssm(u3, u3c, *w["ssm"], layer=layer, batch=batch, seq=seq, n_ctx=n_ctx)

    l2 = consts["l2"]
    l1 = seq // l2
    g = _dft_a(consts["dft_a"], f.reshape(batch, l2, l1 * w_fn))
    pf = _dft_b(consts["dft_b"], g.reshape(batch, 2, l2, l1, w_fn), kb=min(16, l2))

    yna = _na_latent(q, k, v, kc, vc, consts["na"], layer, batch=batch, seq=seq, n_ctx=n_ctx)

    w_glu, w_fo, w_out = w["w_glu"], w["w_fourier"], w["w_out"]
    x2d = _outproj(z3, pf, yna, x2d, mod3, p["g_post_mix"], w_glu, consts["cs_lat"], w_fo, w_out,
                   layer=layer, tm=tm2, tiles_per_batch=seq // tm2, mod_row=None, widths=widths)
    if not last:
        pfc = _dft_ctx(consts["dft_ctx"], fc.reshape(batch, n_ctx, w_fn))
        ynac = _na_ctx(qc, kc, vc, batch=batch, n_ctx=n_ctx, heads=heads)
        tc = min(tmc, n_ctx)
        xc2d = _outproj(zc3, pfc, ynac, xc2d, mod3, p["g_post_mix"], w_glu, consts["cs_ctx"], w_fo, w_out,
                        layer=layer, tm=tc, tiles_per_batch=n_ctx // tc, mod_row=batch, widths=widths)
    else:
        xc2d = None
    wg, wu, wd = w["w_ffn_gate"], w["w_ffn_up"], w["w_ffn_down"]
    return _ffn(x2d, xc2d, mod3, p["g_pre_ffn"], p["g_post_ffn"], wg, wu, wd,
                layer=layer, tm=tm2, tiles_per_batch=seq // tm2, ctx_row=batch)


def kernel(x, c, ctx, c_ctx, w_mod, b_mod, g_pre_mix, g_post_mix, w_in, ssm_a_re, ssm_a_im, ssm_log_dt,
           ssm_b_re, ssm_b_im, ssm_c_re, ssm_c_im, ssm_d, w_glu, w_fourier, na_rpb, w_out, g_pre_ffn,
           g_post_ffn, w_ffn_gate, w_ffn_up, w_ffn_down):
    batch, seq, d = x.shape
    n_ctx = ctx.shape[1]
    depth = w_mod.shape[0]
    w_fn = w_fourier.shape[1]
    params = dict(g_pre_mix=g_pre_mix, g_post_mix=g_post_mix, g_pre_ffn=g_pre_ffn, g_post_ffn=g_post_ffn)
    seg_len = (seq + 2 * n_ctx) // SSM_T // SSM_SEG
    ssm_ops = jax.vmap(functools.partial(_ssm_weights, seg_len=seg_len))(
        ssm_a_re, ssm_a_im, ssm_log_dt, ssm_b_re, ssm_b_im, ssm_c_re, ssm_c_im, ssm_d)
    nrow = -(-(batch + 1) // 8) * 8
    cstack = jnp.concatenate([c, c_ctx[None, :], jnp.zeros((nrow - batch - 1, d), c.dtype)], axis=0)
    weights = dict(mod=_mod_rows(cstack, w_mod, b_mod), ssm=ssm_ops,
                   w_in=w_in, w_glu=w_glu, w_fourier=w_fourier, w_out=w_out,
                   w_ffn_gate=w_ffn_gate,
                   w_ffn_up=w_ffn_up, w_ffn_down=w_ffn_down)

    l2 = 64 if seq % (64 * 8) == 0 else 8
    l1 = seq // l2
    consts = dict(
        l2=l2,
        w_ssm=ssm_d.shape[1],
        na=_na_tables(na_rpb, seq, 4),
        dft_a=_dft_rows(l2).astype(BF),
        dft_b=_dft_b_mats(seq, l1, l2).astype(BF),
        dft_ctx=_dft_rows(n_ctx).astype(BF),
        cs_lat=_channel_dft(w_fn, seq).astype(BF),
        cs_ctx=_channel_dft(w_fn, n_ctx).astype(BF),
    )
    rope_tabs = _rope_tables(seq)

    x2d = x.reshape(batch * seq, d)
    xc2d = ctx.reshape(batch * n_ctx, d)
    for layer in range(depth):
        x2d, xc2d = _layer(x2d, xc2d, params, weights, layer, batch=batch, seq=seq, n_ctx=n_ctx,
                           last=(layer == depth - 1), rope_tabs=rope_tabs, consts=consts)
    return x2d.reshape(batch, seq, d)
```

```python
import functools
import math

import numpy as np
import jax
import jax.numpy as jnp
from jax import lax
from jax.experimental import pallas as pl
from jax.experimental.pallas import tpu as pltpu

BF = jnp.bfloat16
F32 = jnp.float32

EPS = 1e-6
GRID_W = 64
HEAD_DIM = 64
NA_ROWS = 8
NA_COLS = 16
ROPE_BASE = 10000.0
SSM_GROUP = 16
SSM_STATE = 64
FNET_GROUP = 64
LANES = 128
MXU_TILE = 256
SSM_T = 8
SSM_SEG = 8
SSM_STAGES = (4, 2)
NEG = -1e30
VMEM_LIMIT = 56 * 1024 * 1024
VMEM_LIMIT_NA = 60000 * 1024


def _cparams(*sem, vmem=VMEM_LIMIT):
    return pltpu.CompilerParams(dimension_semantics=sem, vmem_limit_bytes=vmem)


def _resident(shape, index_map):
    return pl.BlockSpec(shape, index_map, pipeline_mode=pl.Buffered(1))


def _layer_resident(w_all, layer):
    return pl.BlockSpec((None,) + w_all.shape[1:], lambda *_: (layer, 0, 0), pipeline_mode=pl.Buffered(1))


def _mod_kernel(c_ref, w_ref, b_ref, o_ref):
    cs = c_ref[...]
    s = (cs * jax.nn.sigmoid(cs)).astype(BF)
    o_ref[...] = jnp.dot(s, w_ref[...].astype(BF), preferred_element_type=F32) + b_ref[...]


def _mod_rows(cstack, w_mod, b_mod):
    rows, d = cstack.shape
    depth, _, n = w_mod.shape
    bn = 2 * d if n % (2 * d) == 0 else d
    return pl.pallas_call(
        _mod_kernel,
        grid=(depth, n // bn),
        in_specs=[pl.BlockSpec((rows, d), lambda l, i: (0, 0)),
                  pl.BlockSpec((None, d, bn), lambda l, i: (l, 0, i)),
                  pl.BlockSpec((None, 1, bn), lambda l, i: (l, 0, i))],
        out_specs=pl.BlockSpec((None, rows, bn), lambda l, i: (l, 0, i)),
        out_shape=jax.ShapeDtypeStruct((depth, rows, n), F32),
        compiler_params=_cparams("parallel", "parallel"),
        name="mod_rows",
    )(cstack, w_mod, b_mod.reshape(depth, 1, n))


def _rope_tile(t, cos, sin, first):
    partner = jnp.where(first, pltpu.roll(t, LANES - 16, 1), pltpu.roll(t, 16, 1))
    return t * cos + partner * sin


def _inproj_kernel(x_ref, sh_ref, sc_ref, g_ref, w_ref, cos_ref, sin_ref,
                   u_ref, f_ref, q_ref, k_ref, v_ref, *, widths, rope, sub):
    w_ssm, w_fn, w_na = widths
    scale = HEAD_DIM ** -0.5 * math.log2(math.e)
    w = w_ref[...].astype(BF)
    for r0 in range(0, x_ref.shape[0], sub):
        rs = slice(r0, r0 + sub)
        x = x_ref[rs, :]
        ms = jnp.mean(x * x, axis=-1, keepdims=True)
        xn = x * lax.rsqrt(ms + EPS) * g_ref[...]
        m = (xn * (1.0 + sc_ref[...]) + sh_ref[...]).astype(BF)

        h = jnp.dot(m, w, preferred_element_type=F32)

        def proj(lo, n, h=h):
            return h[:, lo:lo + n]

        u = proj(0, w_ssm)
        for j in range(w_ssm // LANES):
            u_ref[j, rs, :] = u[:, j * LANES:(j + 1) * LANES]
        f_ref[rs, :] = proj(w_ssm, w_fn).astype(BF)
        q = proj(w_ssm + w_fn, w_na)
        k = proj(w_ssm + w_fn + w_na, w_na)
        v_ref[rs, :] = proj(w_ssm + w_fn + 2 * w_na, w_na).astype(BF)
        if rope:
            cos = cos_ref[rs, :]
            sin = sin_ref[rs, :]
            lane = lax.broadcasted_iota(jnp.int32, cos.shape, 1)
            first = (lane % 32) < 16
            for j in range(w_na // LANES):
                sl = slice(j * LANES, (j + 1) * LANES)
                q_ref[rs, sl] = (_rope_tile(q[:, sl], cos, sin, first) * scale).astype(BF)
                k_ref[rs, sl] = _rope_tile(k[:, sl], cos, sin, first).astype(BF)
        else:
            q_ref[rs, :] = (q * scale).astype(BF)
            k_ref[rs, :] = k.astype(BF)


def _inproj_pair_kernel(x_ref, xc_ref, sh_ref, sc_ref, g_ref, w_ref, cos_ref, sin_ref, *outs,
                        widths, sub, n_lat_tiles):
    common = (sh_ref, sc_ref, g_ref, w_ref, cos_ref, sin_ref)
    i = pl.program_id(0)

    @pl.when(i < n_lat_tiles)
    def _latent():
        _inproj_kernel(x_ref, *common, *outs[:5], widths=widths, rope=True, sub=sub)

    @pl.when(i >= n_lat_tiles)
    def _context():
        _inproj_kernel(xc_ref, *common, *outs[5:], widths=widths, rope=False, sub=min(sub, xc_ref.shape[0]))


def _inproj(x2d, xc2d, mod3, g_pre, w_in, cos, sin, *, layer, tm, tiles_per_batch, ctx_row, widths):
    n, d = x2d.shape
    nc = xc2d.shape[0]
    w_ssm, w_fn, w_na = widths
    nl = n // tm
    tile = lambda i: jnp.minimum(i, nl - 1)
    row = lambda i: jnp.where(i < nl, tile(i) // tiles_per_batch, ctx_row)
    pos = lambda i: (tile(i) % tiles_per_batch, 0)
    lat = lambda i: (tile(i), 0)
    one = lambda i: (0, 0)

    def outs(rows, idx, idx3):
        specs = [pl.BlockSpec((w_ssm // LANES, rows, LANES), idx3), pl.BlockSpec((rows, w_fn), idx)]
        specs += [pl.BlockSpec((rows, w_na), idx)] * 3
        return specs

    def shapes(rows):
        return ([jax.ShapeDtypeStruct((w_ssm // LANES, rows, LANES), F32), jax.ShapeDtypeStruct((rows, w_fn), BF)]
                + [jax.ShapeDtypeStruct((rows, w_na), BF)] * 3)

    kern = functools.partial(_inproj_pair_kernel, widths=widths, sub=min(tm, 512), n_lat_tiles=nl)
    res = pl.pallas_call(
        kern,
        grid=(nl + 1,),
        in_specs=[pl.BlockSpec((tm, d), lat),
                  pl.BlockSpec((nc, d), one),
                  pl.BlockSpec((None, 1, d), lambda i: (row(i), 0, 0)),
                  pl.BlockSpec((None, 1, d), lambda i: (row(i), 0, 1)),
                  pl.BlockSpec((None, 1, d), lambda i: (layer, 0, 0)),
                  _layer_resident(w_in, layer),
                  pl.BlockSpec((tm, LANES), pos),
                  pl.BlockSpec((tm, LANES), pos)],
        out_specs=outs(tm, lat, lambda i: (0, tile(i), 0)) + outs(nc, one, lambda i: (0, 0, 0)),
        out_shape=shapes(n) + shapes(nc),
        compiler_params=_cparams("arbitrary"),
        name="inproj",
    )(x2d, xc2d, mod3, mod3, g_pre.reshape(-1, 1, d), w_in, cos, sin)
    return res[:5], res[5:]


def _rope_tables(seq):
    t = np.arange(seq)
    row = (t // GRID_W).astype(np.float32)
    col = (t % GRID_W).astype(np.float32)
    quarter = HEAD_DIM // 4
    freqs = (np.float32(ROPE_BASE) ** (-np.arange(quarter, dtype=np.float32) / quarter)).astype(np.float32)
    d = np.arange(LANES) % HEAD_DIM
    use_col = (d // (HEAD_DIM // 2)) == 1
    fidx = d % quarter
    sign = np.where((d % (HEAD_DIM // 2)) // quarter == 0, -1.0, 1.0)
    pos = np.where(use_col[None, :], col[:, None], row[:, None])
    ang = (pos * freqs[fidx][None, :]).astype(np.float32).astype(np.float64)
    return jnp.asarray(np.cos(ang), F32), jnp.asarray(np.sin(ang) * sign[None, :], F32)


def _gelu_tanh(x):
    c = math.sqrt(2.0 / math.pi)
    return x * (0.5 * (1.0 + jnp.tanh(c * (x + 0.044715 * (x * x * x)))))


def _group_of(idx, width, groups):
    shift = width.bit_length() - 1
    assert width == 1 << shift and groups & (groups - 1) == 0
    return lax.bitwise_and(lax.shift_right_logical(idx, shift), groups - 1)


def _same_group(shape, row0, row_width, col0, col_width, groups):
    r = lax.broadcasted_iota(jnp.int32, shape, 0) + row0
    c = lax.broadcasted_iota(jnp.int32, shape, 1) + col0
    return _group_of(r, row_width, groups) == _group_of(c, col_width, groups)


def _swap_stage(tiles, d):
    lane = lax.broadcasted_iota(jnp.int32, tiles[0].shape, 1)
    upper = lax.bitwise_and(lax.shift_right_logical(lane, SSM_GROUP.bit_length() - 1), d) != 0
    out = list(tiles)
    for x in range(len(tiles)):
        if x & d:
            continue
        a, b = tiles[x], tiles[x + d]
        out[x] = jnp.where(upper, pltpu.roll(b, SSM_GROUP * d, 1), a)
        out[x + d] = jnp.where(upper, b, pltpu.roll(a, LANES - SSM_GROUP * d, 1))
    return out


def _ssm_kernel(ul_ref, uc_ref, wc_ref, lag_ref, dc_ref, rexp_ref, cexp_ref, lam_ref, lamseg_ref,
                z_ref, zc_ref, u_s, bm_s, kin_s, cm_s, s_ref, h_ref, e_ref, *, seg_len, n_lat, n_ctx, stages):
    T = SSM_T
    ns = 2 ** len(stages)
    gs = T // ns
    sw = gs * LANES
    rows = u_s.shape[1]
    tps = s_ref.shape[0] // ns
    tpq = tps // 4
    nst = tpq * LANES
    gpt = LANES // SSM_GROUP
    per_q = gpt * SSM_STATE

    def lanes(c):
        return slice(c * LANES, (c + 1) * LANES)

    def col0(s, q):
        return q * per_q + s * nst

    def swapped(tiles, order):
        for d in order:
            tiles = _swap_stage(tiles, d)
        return tiles

    @pl.when(pl.program_id(1) == 0)
    def _expand_operators():
        rexp = rexp_ref[...]
        cexp = cexp_ref[...]
        for s in range(ns):
            wsel = jnp.concatenate([wc_ref[:, col0(s, q):col0(s, q) + nst] for q in range(4)], axis=1)
            blk = jnp.dot(rexp, wsel, preferred_element_type=F32)
            keep = _same_group(blk.shape, 0, SSM_GROUP, 0, SSM_STATE, gs)
            bm_s[s] = jnp.where(keep, blk, 0.0).astype(BF)
            dsel = jnp.concatenate([dc_ref[col0(s, q):col0(s, q) + nst, :] for q in range(4)], axis=0)
            blk = jnp.dot(dsel, cexp, preferred_element_type=F32)
            keep = _same_group(blk.shape, 0, SSM_STATE, 0, SSM_GROUP, gs)
            cm_s[s] = jnp.where(keep, blk, 0.0).astype(BF)
        r16 = lax.broadcasted_iota(jnp.int32, (LANES, SSM_GROUP), 0)
        c16 = lax.broadcasted_iota(jnp.int32, (LANES, SSM_GROUP), 1)
        chan = jnp.where(lax.bitwise_and(r16, SSM_GROUP - 1) == c16, 1.0, 0.0).astype(BF)
        for t in range(T):
            lo = (T - 1 - t) * LANES
            blk = jnp.dot(chan, lag_ref[:, lo:lo + T * LANES], preferred_element_type=F32)
            keep = _same_group(blk.shape, 0, SSM_GROUP, 0, SSM_GROUP, gpt)
            blk = jnp.where(keep, blk, 0.0)
            tiles = swapped([blk[:, lanes(x)] for x in range(T)], stages)
            r0 = (t % gs) * LANES + (t // gs) * gs * SSM_GROUP
            for s in range(ns):
                part = jnp.concatenate(tiles[s * gs:(s + 1) * gs], axis=1)
                kin_s[s, r0:r0 + gs * SSM_GROUP, :] = part[s * gs * SSM_GROUP:(s + 1) * gs * SSM_GROUP, :].astype(BF)

    xs = []
    for t in range(T):
        ct = uc_ref[pl.ds(t, n_ctx, stride=T), :]
        xs.append(jnp.concatenate([ct, ul_ref[pl.ds(t, n_lat, stride=T), :], ct], axis=0))
    xs = swapped(xs, stages)
    for s in range(ns):
        for k in range(gs):
            u_s[s, :, lanes(k)] = xs[s * gs + k].astype(BF)
    for s in range(ns):
        r = jnp.dot(u_s[s], bm_s[s], preferred_element_type=F32)
        for c in range(tps):
            s_ref[s * tps + c] = r[:, lanes(c)]

    chains = []
    for s in range(ns):
        for d in range(2):
            for k in range(tpq):
                cr, ci = s * tps + 2 * d * tpq + k, s * tps + (2 * d + 1) * tpq + k
                lr = slice(col0(s, 2 * d) + k * LANES, col0(s, 2 * d) + (k + 1) * LANES)
                li = slice(col0(s, 2 * d + 1) + k * LANES, col0(s, 2 * d + 1) + (k + 1) * LANES)
                chains.append((cr, ci, d, lr, li))
    nt = ns * tps

    def scan_step(i, carry, record):
        new = list(carry)
        for cr, ci, d, lr_sl, li_sl in chains:
            rsl = pl.ds(i if d == 0 else seg_len - 1 - i, SSM_SEG, stride=seg_len)
            hr, hi = carry[cr], carry[ci]
            if record:
                h_ref[cr, rsl, :] = hr
                h_ref[ci, rsl, :] = hi
            lr, li = lam_ref[:, lr_sl], lam_ref[:, li_sl]
            new[cr] = lr * hr - li * hi + s_ref[cr, rsl, :]
            new[ci] = lr * hi + li * hr + s_ref[ci, rsl, :]
        return tuple(new)

    zero = jnp.zeros((SSM_SEG, LANES), F32)
    fin = lax.fori_loop(0, seg_len, lambda i, c: scan_step(i, c, False), (zero,) * nt, unroll=4)
    for c in range(nt):
        e_ref[c, 0:SSM_SEG, :] = fin[c]

    for cr, ci, d, lr_sl, li_sl in chains:
        lr, li = lamseg_ref[:, lr_sl], lamseg_ref[:, li_sl]
        er = jnp.zeros((1, LANES), F32)
        ei = jnp.zeros((1, LANES), F32)
        for n in range(SSM_SEG):
            g = n if d == 0 else SSM_SEG - 1 - n
            e_ref[cr, SSM_SEG + g:SSM_SEG + g + 1, :] = er
            e_ref[ci, SSM_SEG + g:SSM_SEG + g + 1, :] = ei
            fr, fi = e_ref[cr, g:g + 1, :], e_ref[ci, g:g + 1, :]
            er, ei = lr * er - li * ei + fr, lr * ei + li * er + fi

    ent = tuple(e_ref[c, SSM_SEG:2 * SSM_SEG, :] for c in range(nt))
    lax.fori_loop(0, seg_len, lambda i, c: scan_step(i, c, True), ent, unroll=2)

    ys, ycs = [], []
    for s in range(ns):
        hcat = jnp.concatenate([h_ref[s * tps + c].astype(BF) for c in range(tps)], axis=1)
        y = _gelu_tanh(jnp.dot(u_s[s], kin_s[s], preferred_element_type=F32)
                       + jnp.dot(hcat, cm_s[s], preferred_element_type=F32))
        ys += [y[:, lanes(k)] for k in range(gs)]
        hc = jnp.concatenate(
            [h_ref[s * tps + c, 0:n_ctx, :].astype(BF) for c in range(2 * tpq)]
            + [h_ref[s * tps + c, rows - n_ctx:rows, :].astype(BF) for c in range(2 * tpq, tps)], axis=1)
        yc = _gelu_tanh(jnp.dot(u_s[s, 0:n_ctx, :], kin_s[s], preferred_element_type=F32)
                        + jnp.dot(hc, cm_s[s], preferred_element_type=F32))
        ycs += [yc[:, lanes(k)] for k in range(gs)]
    ys = swapped(ys, stages[::-1])
    ycs = swapped(ycs, stages[::-1])
    for t in range(T):
        z_ref[pl.ds(t, n_lat, stride=T), :] = ys[t][n_ctx:n_ctx + n_lat, :]
        zc_ref[pl.ds(t, n_ctx, stride=T), :] = ycs[t]


def _ssm_weights(a_re, a_im, log_dt, b_re, b_im, c_re, c_im, d_skip, seg_len):
    T = SSM_T
    g, p = a_re.shape[1], a_re.shape[2]
    hc = b_re.shape[-1]
    gpt = LANES // hc
    J = g // gpt
    lam = lax.complex(a_re.astype(F32), a_im.astype(F32))
    dt = jnp.exp(log_dt.astype(F32))[..., None]
    ldt = lam * dt
    lam_bar = jnp.exp(ldt)
    bbar = ((lam_bar - 1) / lam)[..., None] * lax.complex(b_re.astype(F32), b_im.astype(F32))
    cmat = lax.complex(c_re.astype(F32), c_im.astype(F32))
    kk = jnp.arange(T + 1, dtype=F32)
    pw = jnp.exp(ldt[:, None] * kk[None, :, None, None])

    wf = pw[0, :T][::-1][:, :, None, :] * jnp.swapaxes(bbar[0], 1, 2)[None]
    wb = pw[1, :T][:, :, None, :] * jnp.swapaxes(bbar[1], 1, 2)[None]
    wq = jnp.stack([wf.real, wf.imag, wb.real, wb.imag]).reshape(4, T, J, gpt, hc, p)
    wc = wq.transpose(2, 1, 4, 0, 3, 5).reshape(J, T * hc, 4 * gpt * p)

    df = cmat[0][None] * pw[0, 1:][:, :, None, :]
    db = cmat[1][None] * pw[1, 1:][::-1][:, :, None, :]
    dq = jnp.stack([df.real, -df.imag, db.real, -db.imag]).reshape(4, T, J, gpt, hc, p)
    dc = dq.transpose(2, 0, 3, 5, 1, 4).reshape(J, 4 * gpt * p, T * hc)

    mf = jnp.einsum('gcp,kgp,gph->khgc', cmat[0], pw[0, :T], bbar[0]).real
    mb = jnp.einsum('gcp,kgp,gph->khgc', cmat[1], pw[1, :T], bbar[1]).real
    skip = jnp.eye(hc, dtype=F32)[:, None, :] * d_skip.astype(F32).reshape(g, hc)[None]
    m0 = mf[0] + mb[0] + skip
    lags = jnp.concatenate([mb[1:][::-1], m0[None], mf[1:]], axis=0)
    kc = lags.reshape(2 * T - 1, hc, J, LANES).transpose(2, 1, 0, 3).reshape(J, hc, (2 * T - 1) * LANES)

    def lam_rows(power):
        lp = jnp.exp(ldt * power)
        lt = lp.reshape(2, J, gpt * p)
        return jnp.concatenate([lt[0].real, lt[0].imag, lt[1].real, lt[1].imag], axis=-1)[:, None, :]

    lam_t = jnp.broadcast_to(lam_rows(float(T)), (J, SSM_SEG, 4 * gpt * p))
    return wc.astype(BF), kc.astype(BF), dc.astype(BF), lam_t, lam_rows(float(T * seg_len))


def _ssm_expanders(gs):
    r = np.arange(gs * LANES)
    k, slot, ch = r // LANES, (r % LANES) // SSM_GROUP, r % SSM_GROUP
    t = (slot // gs) * gs + k
    rexp = np.zeros((gs * LANES, SSM_T * SSM_GROUP), np.float32)
    rexp[r, t * SSM_GROUP + ch] = 1.0
    return jnp.asarray(rexp, BF), jnp.asarray(rexp.T, BF)


def _ssm(u3, u3c, wc, kc, dc, lam, lamseg, *, layer, batch, seq, n_ctx):
    T = SSM_T
    J = u3.shape[0]
    ns = wc.shape[-1]
    nsets = 2 ** len(SSM_STAGES)
    gs = T // nsets
    sw = gs * LANES
    n_lat, n_c = seq // T, n_ctx // T
    rows = n_lat + 2 * n_c
    seg_len = rows // SSM_SEG
    assert rows % SSM_SEG == 0 and n_c % 16 == 0
    rexp, cexp = _ssm_expanders(gs)
    kern = functools.partial(_ssm_kernel, seg_len=seg_len, n_lat=n_lat, n_ctx=n_c, stages=SSM_STAGES)
    const = lambda j, b: (0, 0)
    per_j = lambda j, b: (layer, j, 0, 0)
    per_jb = lambda j, b: (j, b, 0)
    return pl.pallas_call(
        kern,
        grid=(J, batch),
        in_specs=[pl.BlockSpec((None, seq, LANES), per_jb),
                  pl.BlockSpec((None, n_ctx, LANES), per_jb),
                  pl.BlockSpec((None, None) + wc.shape[2:], per_j),
                  pl.BlockSpec((None, None) + kc.shape[2:], per_j),
                  pl.BlockSpec((None, None) + dc.shape[2:], per_j),
                  pl.BlockSpec(rexp.shape, const),
                  pl.BlockSpec(cexp.shape, const),
                  pl.BlockSpec((None, None, SSM_SEG, ns), per_j),
                  pl.BlockSpec((None, None, 1, ns), per_j)],
        out_specs=[pl.BlockSpec((None, seq, LANES), per_jb),
                   pl.BlockSpec((None, n_ctx, LANES), per_jb)],
        out_shape=[jax.ShapeDtypeStruct(u3.shape, F32),
                   jax.ShapeDtypeStruct(u3c.shape, F32)],
        scratch_shapes=[pltpu.VMEM((nsets, rows, sw), BF),
                        pltpu.VMEM((nsets, sw, ns // nsets), BF),
                        pltpu.VMEM((nsets, sw, sw), BF),
                        pltpu.VMEM((nsets, ns // nsets, sw), BF),
                        pltpu.VMEM((ns // LANES, rows, LANES), F32),
                        pltpu.VMEM((ns // LANES, rows, LANES), F32),
                        pltpu.VMEM((ns // LANES, 2 * SSM_SEG, LANES), F32)],
        compiler_params=_cparams("arbitrary", "arbitrary"),
        name="ssm_chunk_scan",
    )(u3, u3c, wc, kc, dc, rexp, cexp, lam, lamseg)


def _dft_a_kernel(w_ref, x_ref, o_ref):
    o_ref[...] = jnp.dot(w_ref[...], x_ref[...], preferred_element_type=F32).astype(BF)


def _dft_rows(n):
    k = np.arange(n)
    m = (k[:, None] * k[None, :]) % n
    ang = m * (2.0 * math.pi / n)
    return jnp.asarray(np.concatenate([np.cos(ang), -np.sin(ang)], axis=0), BF)


def _dft_a(w, x3):
    B, n, cols = x3.shape
    tn = min(cols, 16384)
    return pl.pallas_call(
        _dft_a_kernel,
        grid=(B, cols // tn),
        in_specs=[pl.BlockSpec((2 * n, n), lambda b, i: (0, 0)),
                  pl.BlockSpec((None, n, tn), lambda b, i: (b, 0, i))],
        out_specs=pl.BlockSpec((None, 2 * n, tn), lambda b, i: (b, 0, i)),
        out_shape=jax.ShapeDtypeStruct((B, 2 * n, cols), BF),
        compiler_params=_cparams("parallel", "parallel"),
        name="dft_stage_a",
    )(w, x3)


def _store_lane_tiles(o_ref, p, n, wf, idx):
    per = wf // LANES
    for c in range(2 * per):
        part = p[(c // per) * n:(c // per + 1) * n, (c % per) * LANES:(c % per + 1) * LANES]
        o_ref[(c,) + idx] = part


def _dft_b_kernel(a_ref, g_ref, o_ref, *, kb, l1, wf):
    for i in range(kb):
        g = jnp.concatenate([g_ref[0, i], g_ref[1, i]], axis=0)
        p = jnp.dot(a_ref[i], g, preferred_element_type=F32)
        _store_lane_tiles(o_ref, p, l1, wf, (i,))


def _dft_b_mats(seq, l1, l2):
    k2 = np.arange(l2)[:, None, None]
    k1 = np.arange(l1)[None, :, None]
    j1 = np.arange(l1)[None, None, :]
    m = ((k1 * l2 + k2) * j1) % seq
    ang = m * (2.0 * math.pi / seq)
    ar, ai = np.cos(ang).astype(np.float32), -np.sin(ang).astype(np.float32)
    top = np.concatenate([ar, -ai], axis=2)
    bot = np.concatenate([ai, ar], axis=2)
    return jnp.asarray(np.concatenate([top, bot], axis=1), BF)


def _dft_b(amat, g5, *, kb):
    B, _, l2, l1, wf = g5.shape
    nlt = 2 * wf // LANES
    kern = functools.partial(_dft_b_kernel, kb=kb, l1=l1, wf=wf)
    return pl.pallas_call(
        kern,
        grid=(B, l2 // kb),
        in_specs=[pl.BlockSpec((kb, 2 * l1, 2 * l1), lambda b, i: (i, 0, 0)),
                  pl.BlockSpec((None, 2, kb, l1, wf), lambda b, i: (b, 0, i, 0, 0))],
        out_specs=pl.BlockSpec((None, nlt, kb, l1, LANES), lambda b, i: (b, 0, i, 0, 0)),
        out_shape=jax.ShapeDtypeStruct((B, nlt, l2, l1, LANES), F32),
        compiler_params=_cparams("parallel", "parallel"),
        name="dft_stage_b",
    )(amat, g5)


def _dft_ctx_kernel(w_ref, x_ref, o_ref, *, n, wf):
    p = jnp.dot(w_ref[...], x_ref[...], preferred_element_type=F32)
    _store_lane_tiles(o_ref, p, n, wf, ())


def _dft_ctx(w, x3):
    B, n, wf = x3.shape
    nlt = 2 * wf // LANES
    kern = functools.partial(_dft_ctx_kernel, n=n, wf=wf)
    return pl.pallas_call(
        kern,
        grid=(B,),
        in_specs=[pl.BlockSpec((2 * n, n), lambda b: (0, 0)),
                  pl.BlockSpec((None, n, wf), lambda b: (b, 0, 0))],
        out_specs=pl.BlockSpec((None, nlt, n, LANES), lambda b: (b, 0, 0, 0)),
        out_shape=jax.ShapeDtypeStruct((B, nlt, n, LANES), F32),
        compiler_params=_cparams("parallel"),
        name="dft_ctx",
    )(w, x3)


def _channel_dft(wf, seq):
    c = np.arange(wf)
    same = (c[:, None] // FNET_GROUP) == (c[None, :] // FNET_GROUP)
    m = ((c[:, None] % FNET_GROUP) * (c[None, :] % FNET_GROUP)) % FNET_GROUP
    ang = m * (2.0 * math.pi / FNET_GROUP)
    mask = same / math.sqrt(seq * FNET_GROUP)
    return jnp.asarray(np.concatenate([np.cos(ang) * mask, np.sin(ang) * mask], axis=0), BF)


def _softmax_pv(parts, l_shape):
    m = parts[0][0].max(axis=-1, keepdims=True)
    for s, _ in parts[1:]:
        m = jnp.maximum(m, s.max(axis=-1, keepdims=True))
    l = jnp.zeros(l_shape, F32)
    o = None
    for s, v in parts:
        p = jnp.exp2(s - m)
        l = l + p.sum(axis=-1, keepdims=True)
        pv = jnp.dot(p.astype(BF), v, preferred_element_type=F32)
        o = pv if o is None else o + pv
    return o / l


def _na_kernel(kb_ref, pid_ref, dr_ref, q_ref, k_ref, v_ref, kc_ref, vc_ref, cbx_ref, o_ref, bias_ref,
               *, nkeys, heads, nsub):
    step = pl.program_id(1)
    nq = q_ref.shape[0] // nsub
    rq, kr = nq // GRID_W, nkeys // GRID_W
    lane = lax.broadcasted_iota(jnp.int32, (nq, LANES), 1)
    low = lane < HEAD_DIM
    dn = (((1,), (1,)), ((), ()))

    @pl.when((pl.program_id(0) == 0) & (step == 0))
    def _build_bias():
        for pat in range(bias_ref.shape[0]):
            for qi in range(rq):
                for w in range(kr):
                    a = dr_ref[(pat * rq + qi) * kr + w]
                    for h in range(heads):
                        bias_ref[pat, h, qi * GRID_W:(qi + 1) * GRID_W, w * GRID_W:(w + 1) * GRID_W] = cbx_ref[h, a]

    tasks = [(sb, h) for sb in range(nsub) for h in range(heads)]
    starts = [pl.multiple_of(kb_ref[step * nsub + sb] * GRID_W, GRID_W) for sb in range(nsub)]
    pats = [pid_ref[step * nsub + sb] for sb in range(nsub)]

    n_ctx = kc_ref.shape[0]
    ones = jnp.ones((n_ctx + nkeys, LANES), BF)
    kv = {}

    def keys_values(sb, hp):
        if (sb, hp) not in kv:
            sl = slice(hp * LANES, (hp + 1) * LANES)
            win = pl.ds(starts[sb], nkeys)
            k_all = jnp.concatenate([kc_ref[:, sl], k_ref[win, sl]], axis=0)
            v_all = jnp.concatenate([jnp.concatenate([vc_ref[:, sl], v_ref[win, sl]], axis=0), ones], axis=1)
            kv[sb, hp] = (k_all, v_all)
        return kv[sb, hp]

    def scores(sb, h):
        sl = slice((h // 2) * LANES, (h // 2 + 1) * LANES)
        q2 = q_ref[sb * nq:(sb + 1) * nq, sl]
        qm = jnp.where(low if h % 2 == 0 else jnp.logical_not(low), q2, jnp.zeros_like(q2))
        s = lax.dot_general(qm, keys_values(sb, h // 2)[0], dn, preferred_element_type=F32)
        return s[:, n_ctx:] + bias_ref[pats[sb], h], s[:, :n_ctx]

    def weights(s_w, s_c):
        m = jnp.maximum(s_w.max(axis=-1, keepdims=True), s_c.max(axis=-1, keepdims=True))
        return (jnp.concatenate([jnp.exp2(s_c - m).astype(BF), jnp.exp2(s_w - m).astype(BF)], axis=1),)

    outs = {}

    def finish(sb, h, p):
        sl = slice((h // 2) * LANES, (h // 2 + 1) * LANES)
        ox = jnp.dot(p, keys_values(sb, h // 2)[1], preferred_element_type=F32)
        outs[h % 2] = ox[:, :LANES] / ox[:, LANES:]
        if h % 2 == 1:
            o_ref[sb * nq:(sb + 1) * nq, sl] = jnp.where(low, outs[0], outs[1]).astype(BF)

    nt = len(tasks)
    sc = {0: scores(*tasks[0])}
    if nt > 1:
        sc[1] = scores(*tasks[1])
    pw = {0: weights(*sc.pop(0))}
    for n in range(nt):
        if n + 2 < nt:
            sc[n + 2] = scores(*tasks[n + 2])
        if n + 1 < nt:
            pw[n + 1] = weights(*sc.pop(n + 1))
        finish(*tasks[n], *pw.pop(n))


def _na_plan(rows, rq):
    wr = min(NA_ROWS, rows)
    kr = rq + wr - 1
    nblk = rows // rq
    kbs, pids, pats = [], [], []
    for blk in range(nblk):
        r0 = blk * rq
        rs = [int(np.clip(r0 + i - wr // 2, 0, rows - wr)) for i in range(rq)]
        kb = min(rs[0], rows - kr)
        pat = (tuple(r - kb for r in rs), r0 - kb)
        if pat not in pats:
            pats.append(pat)
        kbs.append(kb)
        pids.append(pats.index(pat))
    return np.asarray(kbs, np.int32), np.asarray(pids, np.int32), pats, kr, wr


def _na_bias(rpb, pats, rq, kr, wr):
    cols = np.arange(GRID_W)
    cstart = np.clip(cols - NA_COLS // 2, 0, GRID_W - NA_COLS)
    kc = np.arange(GRID_W)
    col_ok = (kc[None, :] >= cstart[:, None]) & (kc[None, :] < cstart[:, None] + NA_COLS)
    dc = kc[None, :] - cols[:, None] + NA_COLS - 1
    depth, heads, nr, ncol = rpb.shape
    sel = (dc[:, :, None] == np.arange(ncol)[None, None, :]) & col_ok[:, :, None]
    cb = jnp.einsum('lhab,ckb->lhack', rpb.astype(F32), jnp.asarray(sel, F32), precision=lax.Precision.HIGHEST)
    cb = jnp.where(jnp.asarray(col_ok)[None, None, None], cb * math.log2(math.e), NEG)
    cbx = jnp.concatenate([cb, jnp.full((depth, heads, 1, GRID_W, GRID_W), NEG, F32)], axis=2)
    slots = []
    for rs_off, r_off in pats:
        i = np.arange(rq)[:, None]
        w = np.arange(kr)[None, :]
        rso = np.asarray(rs_off)[:, None]
        row_ok = (w >= rso) & (w < rso + wr)
        slots.append(np.where(row_ok, w - r_off - i + NA_ROWS - 1, nr))
    return cbx, np.stack(slots).reshape(-1).astype(np.int32)


def _na_tables(rpb_all, seq, rq):
    kbs, pids, pats, kr, wr = _na_plan(seq // GRID_W, rq)
    cbx, slots = _na_bias(rpb_all, pats, rq, kr, wr)
    return dict(kbs=jnp.asarray(kbs), pids=jnp.asarray(pids), slots=jnp.asarray(slots), cbx=cbx, kr=kr, rq=rq,
                npat=len(pats))


def _na_latent(q, k, v, kc, vc, tabs, layer, *, batch, seq, n_ctx):
    rows = seq // GRID_W
    cbx, rq, kr = tabs["cbx"], tabs["rq"], tabs["kr"]
    heads = cbx.shape[1]
    nblk = rows // rq
    nq = rq * GRID_W
    nkeys = kr * GRID_W
    w = q.shape[1]
    nsub = 8 if nblk % 8 == 0 else (4 if nblk % 4 == 0 else 1)
    nstep = nblk // nsub
    kern = functools.partial(_na_kernel, nkeys=nkeys, heads=heads, nsub=nsub)
    grid_spec = pltpu.PrefetchScalarGridSpec(
        num_scalar_prefetch=3,
        grid=(batch, nstep),
        in_specs=[pl.BlockSpec((nsub * nq, w), lambda b, i, *_: (b * nstep + i, 0)),
                  pl.BlockSpec((seq, w), lambda b, i, *_: (b, 0)),
                  pl.BlockSpec((seq, w), lambda b, i, *_: (b, 0)),
                  pl.BlockSpec((n_ctx, w), lambda b, i, *_: (b, 0)),
                  pl.BlockSpec((n_ctx, w), lambda b, i, *_: (b, 0)),
                  pl.BlockSpec((None,) + cbx.shape[1:], lambda b, i, *_: (layer, 0, 0, 0, 0))],
        out_specs=pl.BlockSpec((nsub * nq, w), lambda b, i, *_: (b * nstep + i, 0)),
        scratch_shapes=[pltpu.VMEM((tabs["npat"], heads, nq, nkeys), F32)],
    )
    return pl.pallas_call(
        kern,
        grid_spec=grid_spec,
        out_shape=jax.ShapeDtypeStruct(q.shape, BF),
        compiler_params=_cparams("arbitrary", "arbitrary", vmem=VMEM_LIMIT_NA),
        name="na_latent",
    )(tabs["kbs"], tabs["pids"], tabs["slots"], q, k, v, kc, vc, cbx)


def _na_ctx_kernel(q_ref, k_ref, v_ref, o_ref, *, heads):
    nq = q_ref.shape[0]
    lane = lax.broadcasted_iota(jnp.int32, (nq, LANES), 1)
    low = lane < HEAD_DIM
    dn = (((1,), (1,)), ((), ()))
    for hp in range(heads // 2):
        sl = slice(hp * LANES, (hp + 1) * LANES)
        q2 = q_ref[:, sl]
        k2 = k_ref[:, sl]
        v2 = v_ref[:, sl]
        outs = []
        for hh in range(2):
            qm = jnp.where(low if hh == 0 else jnp.logical_not(low), q2, jnp.zeros_like(q2))
            s = lax.dot_general(qm, k2, dn, preferred_element_type=F32)
            outs.append(_softmax_pv([(s, v2)], (nq, 1)))
        o_ref[:, sl] = jnp.where(low, outs[0], outs[1]).astype(BF)


def _na_ctx(qc, kc, vc, *, batch, n_ctx, heads):
    w = qc.shape[1]
    spec = pl.BlockSpec((n_ctx, w), lambda b: (b, 0))
    return pl.pallas_call(
        functools.partial(_na_ctx_kernel, heads=heads),
        grid=(batch,),
        in_specs=[spec, spec, spec],
        out_specs=spec,
        out_shape=jax.ShapeDtypeStruct(qc.shape, BF),
        compiler_params=_cparams("parallel"),
        name="na_ctx",
    )(qc, kc, vc)


def _outproj_kernel(z_ref, pf_ref, na_ref, x_ref, gate_ref, gpost_ref,
                    wglu_ref, cs_ref, wfo_ref, wout_ref, o_ref, *, widths, sub):
    w_ssm, w_fn, w_na = widths
    wglu, wfo, wout = (r[...].astype(BF) for r in (wglu_ref, wfo_ref, wout_ref))
    for r0 in range(0, x_ref.shape[0], sub):
        rs = slice(r0, r0 + sub)
        z = jnp.concatenate([z_ref[j, rs, :] for j in range(w_ssm // LANES)], axis=1)
        gl = jnp.dot(z.astype(BF), wglu, preferred_element_type=F32)
        y_ssm = (z * jax.nn.sigmoid(gl)).astype(BF)
        if len(pf_ref.shape) == 3:
            tiles = [pf_ref[c, rs, :] for c in range(pf_ref.shape[0])]
        else:
            l2 = pf_ref.shape[1]
            tiles = [jnp.swapaxes(pf_ref[c, :, r0 // l2:(r0 + sub) // l2, :], 0, 1).reshape(sub, LANES)
                     for c in range(pf_ref.shape[0])]
        pf = jnp.concatenate(tiles, axis=1).astype(BF)
        mixed = jnp.dot(pf, cs_ref[...], preferred_element_type=F32).astype(BF)
        y_fft = jnp.dot(mixed, wfo, preferred_element_type=F32).astype(BF)
        y = jnp.concatenate([y_ssm, y_fft, na_ref[rs, :]], axis=1)
        o = jnp.dot(y, wout, preferred_element_type=F32)
        ms = jnp.mean(o * o, axis=-1, keepdims=True)
        o_ref[rs, :] = x_ref[rs, :] + gate_ref[...] * (o * lax.rsqrt(ms + EPS) * gpost_ref[...])


def _outproj(z3, pf, yna, x2d, mod3, g_post, w_glu, cs, w_fo, w_out, *,
             layer, tm, tiles_per_batch, mod_row, widths):
    n, d = x2d.shape
    w_ssm, w_fn, w_na = widths
    if mod_row is None:
        row = lambda i: i // tiles_per_batch
    else:
        row = lambda i: mod_row
    kern = functools.partial(_outproj_kernel, widths=widths, sub=min(tm, 512))
    return pl.pallas_call(
        kern,
        grid=(n // tm,),
        in_specs=[pl.BlockSpec((w_ssm // LANES, tm, LANES), lambda i: (0, i, 0)),
                  (pl.BlockSpec((None, pf.shape[1], tm, LANES),
                                lambda i: (i // tiles_per_batch, 0, i % tiles_per_batch, 0))
                   if pf.ndim == 4 else
                   pl.BlockSpec((None, pf.shape[1], pf.shape[2], tm // pf.shape[2], LANES),
                                lambda i: (i // tiles_per_batch, 0, 0, i % tiles_per_batch, 0))),
                  pl.BlockSpec((tm, w_na), lambda i: (i, 0)),
                  pl.BlockSpec((tm, d), lambda i: (i, 0)),
                  pl.BlockSpec((None, 1, d), lambda i: (row(i), 0, 2)),
                  pl.BlockSpec((None, 1, d), lambda i: (layer, 0, 0)),
                  _layer_resident(w_glu, layer),
                  _resident(cs.shape, lambda i: (0, 0)),
                  _layer_resident(w_fo, layer),
                  _layer_resident(w_out, layer)],
        out_specs=pl.BlockSpec((tm, d), lambda i: (i, 0)),
        out_shape=jax.ShapeDtypeStruct((n, d), F32),
        compiler_params=_cparams("parallel"),
        name="outproj",
    )(z3, pf, yna, x2d, mod3, g_post.reshape(-1, 1, d), w_glu, cs, w_fo, w_out)


def _ffn_kernel(x_ref, sh_ref, sc_ref, gate_ref, gpre_ref, gpost_ref, wg_ref, wu_ref, wd_ref,
                o_ref, a_ref, *, sub, chunk):
    dff = wg_ref.shape[1]
    for r0 in range(0, x_ref.shape[0], sub):
        rs = slice(r0, r0 + sub)
        x = x_ref[rs, :]
        ms = jnp.mean(x * x, axis=-1, keepdims=True)
        m = ((x * lax.rsqrt(ms + EPS) * gpre_ref[...]) * (1.0 + sc_ref[...]) + sh_ref[...]).astype(BF)
        for c0 in range(0, dff, chunk):
            sl = slice(c0, min(c0 + chunk, dff))
            g = jnp.dot(m, wg_ref[:, sl], preferred_element_type=F32)
            u = jnp.dot(m, wu_ref[:, sl], preferred_element_type=F32)
            a_ref[rs, sl] = (g * jax.nn.sigmoid(g) * u).astype(BF)
        y = jnp.dot(a_ref[rs, :], wd_ref[...], preferred_element_type=F32)
        ms2 = jnp.mean(y * y, axis=-1, keepdims=True)
        o_ref[rs, :] = x + gate_ref[...] * (y * lax.rsqrt(ms2 + EPS) * gpost_ref[...])


FFN_WEIGHT_STEPS = 8


def _ffn_staged_kernel(*refs, sub, chunk, n_lat_tiles, with_ctx):
    if with_ctx:
        (x_ref, xc_ref, sh_ref, sc_ref, gate_ref, gpre_ref, gpost_ref, wg_ref, wu_ref, wd_ref,
         o_ref, oc_ref, a_ref, wg_s, wu_s, wd_s) = refs
    else:
        (x_ref, sh_ref, sc_ref, gate_ref, gpre_ref, gpost_ref, wg_ref, wu_ref, wd_ref,
         o_ref, a_ref, wg_s, wu_s, wd_s) = refs
    i = pl.program_id(0)
    common = (sh_ref, sc_ref, gate_ref, gpre_ref, gpost_ref, wg_s, wu_s, wd_s)

    @pl.when(i < FFN_WEIGHT_STEPS)
    def _cast_weights():
        for src, dst in ((wg_ref, wg_s), (wu_ref, wu_s), (wd_ref, wd_s)):
            rows = src.shape[0]
            dst[pl.ds(pl.multiple_of(i * rows, rows), rows), :] = src[...].astype(BF)

    @pl.when((i >= FFN_WEIGHT_STEPS) & (i < FFN_WEIGHT_STEPS + n_lat_tiles))
    def _latent():
        _ffn_kernel(x_ref, *common, o_ref, a_ref, sub=sub, chunk=chunk)

    if with_ctx:
        @pl.when(i >= FFN_WEIGHT_STEPS + n_lat_tiles)
        def _context():
            _ffn_kernel(xc_ref, *common, oc_ref, a_ref, sub=min(sub, xc_ref.shape[0]), chunk=chunk)


def _ffn(x2d, xc2d, mod3, g_pre, g_post, wg, wu, wd, *, layer, tm, tiles_per_batch, ctx_row):
    n, d = x2d.shape
    dff = wg.shape[2]
    nl = n // tm
    nw = FFN_WEIGHT_STEPS
    with_ctx = xc2d is not None
    gains = (g_pre.reshape(-1, 1, d), g_post.reshape(-1, 1, d))
    slab = lambda i: (layer, jnp.minimum(i, nw - 1), 0)
    weights = [pl.BlockSpec((None, d // nw, dff), slab), pl.BlockSpec((None, d // nw, dff), slab),
               pl.BlockSpec((None, dff // nw, d), slab)]
    gain_specs = [pl.BlockSpec((None, 1, d), lambda i: (layer, 0, 0))] * 2
    tile = lambda i: jnp.clip(i - nw, 0, nl - 1)
    row = lambda i: jnp.where(i < nw + nl, tile(i) // tiles_per_batch, ctx_row)
    lat = lambda i: (tile(i), 0)
    mods = [pl.BlockSpec((None, 1, d), functools.partial(lambda i, c: (row(i), 0, c), c=c)) for c in (3, 4, 5)]
    in_specs = [pl.BlockSpec((tm, d), lat)]
    out_specs = [pl.BlockSpec((tm, d), lat)]
    out_shape = [jax.ShapeDtypeStruct((n, d), F32)]
    args = [x2d]
    if with_ctx:
        nc = xc2d.shape[0]
        ctx = lambda i: (jnp.maximum(i - nw - nl, 0), 0)
        in_specs.append(pl.BlockSpec((nc, d), ctx, pipeline_mode=pl.Buffered(1)))
        out_specs.append(pl.BlockSpec((nc, d), ctx))
        out_shape.append(jax.ShapeDtypeStruct((nc, d), F32))
        args.append(xc2d)
    outs = pl.pallas_call(
        functools.partial(_ffn_staged_kernel, sub=min(tm, 512), chunk=3 * MXU_TILE, n_lat_tiles=nl,
                          with_ctx=with_ctx),
        grid=(nw + nl + int(with_ctx),),
        in_specs=in_specs + mods + gain_specs + weights,
        out_specs=out_specs,
        out_shape=out_shape,
        scratch_shapes=[pltpu.VMEM((tm, dff), BF), pltpu.VMEM((d, dff), BF), pltpu.VMEM((d, dff), BF),
                        pltpu.VMEM((dff, d), BF)],
        compiler_params=_cparams("arbitrary"),
        name="ffn_staged",
    )(*args, mod3, mod3, mod3, *gains, wg, wu, wd)
    return (outs[0], outs[1]) if with_ctx else (outs[0], None)


def _layer(x2d, xc2d, p, w, layer, *, batch, seq, n_ctx, last, rope_tabs, consts):
    d = x2d.shape[1]
    w_ssm = consts["w_ssm"]
    w_fn = w["w_fourier"].shape[1]
    w_na = (w["w_in"].shape[2] - w_ssm - w_fn) // 3
    widths = (w_ssm, w_fn, w_na)
    heads = w_na // HEAD_DIM
    tm = min(512, seq)
    tmc = min(512, batch * n_ctx)

    mod = w["mod"][layer]
    mod3 = mod.reshape(mod.shape[0], 1, 6 * d)

    cos, sin = rope_tabs
    tm2 = min(2 * tm, seq)
    (u3, f, q, k, v), (u3c, fc, qc, kc, vc) = _inproj(
        x2d, xc2d, mod3, p["g_pre_mix"], w["w_in"], cos, sin, layer=layer, tm=tm2,
        tiles_per_batch=seq // tm2, ctx_row=batch, widths=widths)

    z3, zc3 = _ssm(u3, u3c, *w["ssm"], layer=layer, batch=batch, seq=seq, n_ctx=n_ctx)

    l2 = consts["l2"]
    l1 = seq // l2
    g = _dft_a(consts["dft_a"], f.reshape(batch, l2, l1 * w_fn))
    pf = _dft_b(consts["dft_b"], g.reshape(batch, 2, l2, l1, w_fn), kb=min(16, l2))

    yna = _na_latent(q, k, v, kc, vc, consts["na"], layer, batch=batch, seq=seq, n_ctx=n_ctx)

    w_glu, w_fo, w_out = w["w_glu"], w["w_fourier"], w["w_out"]
    x2d = _outproj(z3, pf, yna, x2d, mod3, p["g_post_mix"], w_glu, consts["cs_lat"], w_fo, w_out,
                   layer=layer, tm=tm2, tiles_per_batch=seq // tm2, mod_row=None, widths=widths)
    if not last:
        pfc = _dft_ctx(consts["dft_ctx"], fc.reshape(batch, n_ctx, w_fn))
        ynac = _na_ctx(qc, kc, vc, batch=batch, n_ctx=n_ctx, heads=heads)
        tc = min(tmc, n_ctx)
        xc2d = _outproj(zc3, pfc, ynac, xc2d, mod3, p["g_post_mix"], w_glu, consts["cs_ctx"], w_fo, w_out,
                        layer=layer, tm=tc, tiles_per_batch=n_ctx // tc, mod_row=batch, widths=widths)
    else:
        xc2d = None
    wg, wu, wd = w["w_ffn_gate"], w["w_ffn_up"], w["w_ffn_down"]
    return _ffn(x2d, xc2d, mod3, p["g_pre_ffn"], p["g_post_ffn"], wg, wu, wd,
                layer=layer, tm=tm2, tiles_per_batch=seq // tm2, ctx_row=batch)


def kernel(x, c, ctx, c_ctx, w_mod, b_mod, g_pre_mix, g_post_mix, w_in, ssm_a_re, ssm_a_im, ssm_log_dt,
           ssm_b_re, ssm_b_im, ssm_c_re, ssm_c_im, ssm_d, w_glu, w_fourier, na_rpb, w_out, g_pre_ffn,
           g_post_ffn, w_ffn_gate, w_ffn_up, w_ffn_down):
    batch, seq, d = x.shape
    n_ctx = ctx.shape[1]
    depth = w_mod.shape[0]
    w_fn = w_fourier.shape[1]
    params = dict(g_pre_mix=g_pre_mix, g_post_mix=g_post_mix, g_pre_ffn=g_pre_ffn, g_post_ffn=g_post_ffn)
    seg_len = (seq + 2 * n_ctx) // SSM_T // SSM_SEG
    ssm_ops = jax.vmap(functools.partial(_ssm_weights, seg_len=seg_len))(
        ssm_a_re, ssm_a_im, ssm_log_dt, ssm_b_re, ssm_b_im, ssm_c_re, ssm_c_im, ssm_d)
    nrow = -(-(batch + 1) // 8) * 8
    cstack = jnp.concatenate([c, c_ctx[None, :], jnp.zeros((nrow - batch - 1, d), c.dtype)], axis=0)
    weights = dict(mod=_mod_rows(cstack, w_mod, b_mod), ssm=ssm_ops,
                   w_in=w_in, w_glu=w_glu, w_fourier=w_fourier, w_out=w_out,
                   w_ffn_gate=w_ffn_gate,
                   w_ffn_up=w_ffn_up, w_ffn_down=w_ffn_down)

    l2 = 64 if seq % (64 * 8) == 0 else 8
    l1 = seq // l2
    consts = dict(
        l2=l2,
        w_ssm=ssm_d.shape[1],
        na=_na_tables(na_rpb, seq, 4),
        dft_a=_dft_rows(l2).astype(BF),
        dft_b=_dft_b_mats(seq, l1, l2).astype(BF),
        dft_ctx=_dft_rows(n_ctx).astype(BF),
        cs_lat=_channel_dft(w_fn, seq).astype(BF),
        cs_ctx=_channel_dft(w_fn, n_ctx).astype(BF),
    )
    rope_tabs = _rope_tables(seq)

    x2d = x.reshape(batch * seq, d)
    xc2d = ctx.reshape(batch * n_ctx, d)
    for layer in range(depth):
        x2d, xc2d = _layer(x2d, xc2d, params, weights, layer, batch=batch, seq=seq, n_ctx=n_ctx,
                           last=(layer == depth - 1), rope_tabs=rope_tabs, consts=consts)
    return x2d.reshape(batch, seq, d)
```

```python
import functools
import math

import numpy as np
import jax
import jax.numpy as jnp
from jax import lax
from jax.experimental import pallas as pl
from jax.experimental.pallas import tpu as pltpu

BF = jnp.bfloat16
F32 = jnp.float32

EPS = 1e-6
GRID_W = 64
HEAD_DIM = 64
NA_ROWS = 8
NA_COLS = 16
ROPE_BASE = 10000.0
SSM_GROUP = 16
SSM_STATE = 64
FNET_GROUP = 64
LANES = 128
MXU_TILE = 256
SSM_T = 8
SSM_SEG = 8
SSM_STAGES = (4, 2)
NEG = -1e30
VMEM_LIMIT = 56 * 1024 * 1024
VMEM_LIMIT_NA = 60000 * 1024


def _cparams(*sem, vmem=VMEM_LIMIT):
    return pltpu.CompilerParams(dimension_semantics=sem, vmem_limit_bytes=vmem)


def _resident(shape, index_map):
    return pl.BlockSpec(shape, index_map, pipeline_mode=pl.Buffered(1))


def _layer_resident(w_all, layer):
    return pl.BlockSpec((None,) + w_all.shape[1:], lambda *_: (layer, 0, 0), pipeline_mode=pl.Buffered(1))


def _mod_kernel(c_ref, w_ref, b_ref, o_ref):
    cs = c_ref[...]
    s = (cs * jax.nn.sigmoid(cs)).astype(BF)
    o_ref[...] = jnp.dot(s, w_ref[...].astype(BF), preferred_element_type=F32) + b_ref[...]


def _mod_rows(cstack, w_mod, b_mod):
    rows, d = cstack.shape
    depth, _, n = w_mod.shape
    bn = 2 * d if n % (2 * d) == 0 else d
    return pl.pallas_call(
        _mod_kernel,
        grid=(depth, n // bn),
        in_specs=[pl.BlockSpec((rows, d), lambda l, i: (0, 0)),
                  pl.BlockSpec((None, d, bn), lambda l, i: (l, 0, i)),
                  pl.BlockSpec((None, 1, bn), lambda l, i: (l, 0, i))],
        out_specs=pl.BlockSpec((None, rows, bn), lambda l, i: (l, 0, i)),
        out_shape=jax.ShapeDtypeStruct((depth, rows, n), F32),
        compiler_params=_cparams("parallel", "parallel"),
        name="mod_rows",
    )(cstack, w_mod, b_mod.reshape(depth, 1, n))


def _rope_tile(t, cos, sin, first):
    partner = jnp.where(first, pltpu.roll(t, LANES - 16, 1), pltpu.roll(t, 16, 1))
    return t * cos + partner * sin


def _inproj_kernel(x_ref, sh_ref, sc_ref, g_ref, w_ref, cos_ref, sin_ref,
                   u_ref, f_ref, q_ref, k_ref, v_ref, *, widths, rope, sub):
    w_ssm, w_fn, w_na = widths
    scale = HEAD_DIM ** -0.5 * math.log2(math.e)
    w = w_ref[...].astype(BF)
    for r0 in range(0, x_ref.shape[0], sub):
        rs = slice(r0, r0 + sub)
        x = x_ref[rs, :]
        ms = jnp.mean(x * x, axis=-1, keepdims=True)
        xn = x * lax.rsqrt(ms + EPS) * g_ref[...]
        m = (xn * (1.0 + sc_ref[...]) + sh_ref[...]).astype(BF)

        h = jnp.dot(m, w, preferred_element_type=F32)

        def proj(lo, n, h=h):
            return h[:, lo:lo + n]

        u = proj(0, w_ssm)
        for j in range(w_ssm // LANES):
            u_ref[j, rs, :] = u[:, j * LANES:(j + 1) * LANES]
        f_ref[rs, :] = proj(w_ssm, w_fn).astype(BF)
        q = proj(w_ssm + w_fn, w_na)
        k = proj(w_ssm + w_fn + w_na, w_na)
        v_ref[rs, :] = proj(w_ssm + w_fn + 2 * w_na, w_na).astype(BF)
        if rope:
            cos = cos_ref[rs, :]
            sin = sin_ref[rs, :]
            lane = lax.broadcasted_iota(jnp.int32, cos.shape, 1)
            first = (lane % 32) < 16
            for j in range(w_na // LANES):
                sl = slice(j * LANES, (j + 1) * LANES)
                q_ref[rs, sl] = (_rope_tile(q[:, sl], cos, sin, first) * scale).astype(BF)
                k_ref[rs, sl] = _rope_tile(k[:, sl], cos, sin, first).astype(BF)
        else:
            q_ref[rs, :] = (q * scale).astype(BF)
            k_ref[rs, :] = k.astype(BF)


def _inproj_pair_kernel(x_ref, xc_ref, sh_ref, sc_ref, g_ref, w_ref, cos_ref, sin_ref, *outs,
                        widths, sub, n_lat_tiles):
    common = (sh_ref, sc_ref, g_ref, w_ref, cos_ref, sin_ref)
    i = pl.program_id(0)

    @pl.when(i < n_lat_tiles)
    def _latent():
        _inproj_kernel(x_ref, *common, *outs[:5], widths=widths, rope=True, sub=sub)

    @pl.when(i >= n_lat_tiles)
    def _context():
        _inproj_kernel(xc_ref, *common, *outs[5:], widths=widths, rope=False, sub=min(sub, xc_ref.shape[0]))


def _inproj(x2d, xc2d, mod3, g_pre, w_in, cos, sin, *, layer, tm, tiles_per_batch, ctx_row, widths):
    n, d = x2d.shape
    nc = xc2d.shape[0]
    w_ssm, w_fn, w_na = widths
    nl = n // tm
    tile = lambda i: jnp.minimum(i, nl - 1)
    row = lambda i: jnp.where(i < nl, tile(i) // tiles_per_batch, ctx_row)
    pos = lambda i: (tile(i) % tiles_per_batch, 0)
    lat = lambda i: (tile(i), 0)
    one = lambda i: (0, 0)

    def outs(rows, idx, idx3):
        specs = [pl.BlockSpec((w_ssm // LANES, rows, LANES), idx3), pl.BlockSpec((rows, w_fn), idx)]
        specs += [pl.BlockSpec((rows, w_na), idx)] * 3
        return specs

    def shapes(rows):
        return ([jax.ShapeDtypeStruct((w_ssm // LANES, rows, LANES), F32), jax.ShapeDtypeStruct((rows, w_fn), BF)]
                + [jax.ShapeDtypeStruct((rows, w_na), BF)] * 3)

    kern = functools.partial(_inproj_pair_kernel, widths=widths, sub=min(tm, 512), n_lat_tiles=nl)
    res = pl.pallas_call(
        kern,
        grid=(nl + 1,),
        in_specs=[pl.BlockSpec((tm, d), lat),
                  pl.BlockSpec((nc, d), one),
                  pl.BlockSpec((None, 1, d), lambda i: (row(i), 0, 0)),
                  pl.BlockSpec((None, 1, d), lambda i: (row(i), 0, 1)),
                  pl.BlockSpec((None, 1, d), lambda i: (layer, 0, 0)),
                  _layer_resident(w_in, layer),
                  pl.BlockSpec((tm, LANES), pos),
                  pl.BlockSpec((tm, LANES), pos)],
        out_specs=outs(tm, lat, lambda i: (0, tile(i), 0)) + outs(nc, one, lambda i: (0, 0, 0)),
        out_shape=shapes(n) + shapes(nc),
        compiler_params=_cparams("arbitrary"),
        name="inproj",
    )(x2d, xc2d, mod3, mod3, g_pre.reshape(-1, 1, d), w_in, cos, sin)
    return res[:5], res[5:]


def _rope_tables(seq):
    t = np.arange(seq)
    row = (t // GRID_W).astype(np.float32)
    col = (t % GRID_W).astype(np.float32)
    quarter = HEAD_DIM // 4
    freqs = (np.float32(ROPE_BASE) ** (-np.arange(quarter, dtype=np.float32) / quarter)).astype(np.float32)
    d = np.arange(LANES) % HEAD_DIM
    use_col = (d // (HEAD_DIM // 2)) == 1
    fidx = d % quarter
    sign = np.where((d % (HEAD_DIM // 2)) // quarter == 0, -1.0, 1.0)
    pos = np.where(use_col[None, :], col[:, None], row[:, None])
    ang = (pos * freqs[fidx][None, :]).astype(np.float32).astype(np.float64)
    return jnp.asarray(np.cos(ang), F32), jnp.asarray(np.sin(ang) * sign[None, :], F32)


def _gelu_tanh(x):
    c = math.sqrt(2.0 / math.pi)
    return x * (0.5 * (1.0 + jnp.tanh(c * (x + 0.044715 * (x * x * x)))))


def _group_of(idx, width, groups):
    shift = width.bit_length() - 1
    assert width == 1 << shift and groups & (groups - 1) == 0
    return lax.bitwise_and(lax.shift_right_logical(idx, shift), groups - 1)


def _same_group(shape, row0, row_width, col0, col_width, groups):
    r = lax.broadcasted_iota(jnp.int32, shape, 0) + row0
    c = lax.broadcasted_iota(jnp.int32, shape, 1) + col0
    return _group_of(r, row_width, groups) == _group_of(c, col_width, groups)


def _swap_stage(tiles, d):
    lane = lax.broadcasted_iota(jnp.int32, tiles[0].shape, 1)
    upper = lax.bitwise_and(lax.shift_right_logical(lane, SSM_GROUP.bit_length() - 1), d) != 0
    out = list(tiles)
    for x in range(len(tiles)):
        if x & d:
            continue
        a, b = tiles[x], tiles[x + d]
        out[x] = jnp.where(upper, pltpu.roll(b, SSM_GROUP * d, 1), a)
        out[x + d] = jnp.where(upper, b, pltpu.roll(a, LANES - SSM_GROUP * d, 1))
    return out


def _ssm_kernel(ul_ref, uc_ref, wc_ref, lag_ref, dc_ref, rexp_ref, cexp_ref, lam_ref, lamseg_ref,
                z_ref, zc_ref, u_s, bm_s, kin_s, cm_s, s_ref, h_ref, e_ref, *, seg_len, n_lat, n_ctx, stages):
    T = SSM_T
    ns = 2 ** len(stages)
    gs = T // ns
    sw = gs * LANES
    rows = u_s.shape[1]
    tps = s_ref.shape[0] // ns
    tpq = tps // 4
    nst = tpq * LANES
    gpt = LANES // SSM_GROUP
    per_q = gpt * SSM_STATE

    def lanes(c):
        return slice(c * LANES, (c + 1) * LANES)

    def col0(s, q):
        return q * per_q + s * nst

    def swapped(tiles, order):
        for d in order:
            tiles = _swap_stage(tiles, d)
        return tiles

    @pl.when(pl.program_id(1) == 0)
    def _expand_operators():
        rexp = rexp_ref[...]
        cexp = cexp_ref[...]
        for s in range(ns):
            wsel = jnp.concatenate([wc_ref[:, col0(s, q):col0(s, q) + nst] for q in range(4)], axis=1)
            blk = jnp.dot(rexp, wsel, preferred_element_type=F32)
            keep = _same_group(blk.shape, 0, SSM_GROUP, 0, SSM_STATE, gs)
            bm_s[s] = jnp.where(keep, blk, 0.0).astype(BF)
            dsel = jnp.concatenate([dc_ref[col0(s, q):col0(s, q) + nst, :] for q in range(4)], axis=0)
            blk = jnp.dot(dsel, cexp, preferred_element_type=F32)
            keep = _same_group(blk.shape, 0, SSM_STATE, 0, SSM_GROUP, gs)
            cm_s[s] = jnp.where(keep, blk, 0.0).astype(BF)
        r16 = lax.broadcasted_iota(jnp.int32, (LANES, SSM_GROUP), 0)
        c16 = lax.broadcasted_iota(jnp.int32, (LANES, SSM_GROUP), 1)
        chan = jnp.where(lax.bitwise_and(r16, SSM_GROUP - 1) == c16, 1.0, 0.0).astype(BF)
        for t in range(T):
            lo = (T - 1 - t) * LANES
            blk = jnp.dot(chan, lag_ref[:, lo:lo + T * LANES], preferred_element_type=F32)
            keep = _same_group(blk.shape, 0, SSM_GROUP, 0, SSM_GROUP, gpt)
            blk = jnp.where(keep, blk, 0.0)
            tiles = swapped([blk[:, lanes(x)] for x in range(T)], stages)
            r0 = (t % gs) * LANES + (t // gs) * gs * SSM_GROUP
            for s in range(ns):
                part = jnp.concatenate(tiles[s * gs:(s + 1) * gs], axis=1)
                kin_s[s, r0:r0 + gs * SSM_GROUP, :] = part[s * gs * SSM_GROUP:(s + 1) * gs * SSM_GROUP, :].astype(BF)

    xs = []
    for t in range(T):
        ct = uc_ref[pl.ds(t, n_ctx, stride=T), :]
        xs.append(jnp.concatenate([ct, ul_ref[pl.ds(t, n_lat, stride=T), :], ct], axis=0))
    xs = swapped(xs, stages)
    for s in range(ns):
        for k in range(gs):
            u_s[s, :, lanes(k)] = xs[s * gs + k].astype(BF)
    for s in range(ns):
        r = jnp.dot(u_s[s], bm_s[s], preferred_element_type=F32)
        for c in range(tps):
            s_ref[s * tps + c] = r[:, lanes(c)]

    chains = []
    for s in range(ns):
        for d in range(2):
            for k in range(tpq):
                cr, ci = s * tps + 2 * d * tpq + k, s * tps + (2 * d + 1) * tpq + k
                lr = slice(col0(s, 2 * d) + k * LANES, col0(s, 2 * d) + (k + 1) * LANES)
                li = slice(col0(s, 2 * d + 1) + k * LANES, col0(s, 2 * d + 1) + (k + 1) * LANES)
                chains.append((cr, ci, d, lr, li))
    nt = ns * tps

    def scan_step(i, carry, record):
        new = list(carry)
        for cr, ci, d, lr_sl, li_sl in chains:
            rsl = pl.ds(i if d == 0 else seg_len - 1 - i, SSM_SEG, stride=seg_len)
            hr, hi = carry[cr], carry[ci]
            if record:
                h_ref[cr, rsl, :] = hr
                h_ref[ci, rsl, :] = hi
            lr, li = lam_ref[:, lr_sl], lam_ref[:, li_sl]
            new[cr] = lr * hr - li * hi + s_ref[cr, rsl, :]
            new[ci] = lr * hi + li * hr + s_ref[ci, rsl, :]
        return tuple(new)

    zero = jnp.zeros((SSM_SEG, LANES), F32)
    fin = lax.fori_loop(0, seg_len, lambda i, c: scan_step(i, c, False), (zero,) * nt, unroll=4)
    for c in range(nt):
        e_ref[c, 0:SSM_SEG, :] = fin[c]

    for cr, ci, d, lr_sl, li_sl in chains:
        lr, li = lamseg_ref[:, lr_sl], lamseg_ref[:, li_sl]
        er = jnp.zeros((1, LANES), F32)
        ei = jnp.zeros((1, LANES), F32)
        for n in range(SSM_SEG):
            g = n if d == 0 else SSM_SEG - 1 - n
            e_ref[cr, SSM_SEG + g:SSM_SEG + g + 1, :] = er
            e_ref[ci, SSM_SEG + g:SSM_SEG + g + 1, :] = ei
            fr, fi = e_ref[cr, g:g + 1, :], e_ref[ci, g:g + 1, :]
            er, ei = lr * er - li * ei + fr, lr * ei + li * er + fi

    ent = tuple(e_ref[c, SSM_SEG:2 * SSM_SEG, :] for c in range(nt))
    lax.fori_loop(0, seg_len, lambda i, c: scan_step(i, c, True), ent, unroll=2)

    ys, ycs = [], []
    for s in range(ns):
        hcat = jnp.concatenate([h_ref[s * tps + c].astype(BF) for c in range(tps)], axis=1)
        y = _gelu_tanh(jnp.dot(u_s[s], kin_s[s], preferred_element_type=F32)
                       + jnp.dot(hcat, cm_s[s], preferred_element_type=F32))
        ys += [y[:, lanes(k)] for k in range(gs)]
        hc = jnp.concatenate(
            [h_ref[s * tps + c, 0:n_ctx, :].astype(BF) for c in range(2 * tpq)]
            + [h_ref[s * tps + c, rows - n_ctx:rows, :].astype(BF) for c in range(2 * tpq, tps)], axis=1)
        yc = _gelu_tanh(jnp.dot(u_s[s, 0:n_ctx, :], kin_s[s], preferred_element_type=F32)
                        + jnp.dot(hc, cm_s[s], preferred_element_type=F32))
        ycs += [yc[:, lanes(k)] for k in range(gs)]
    ys = swapped(ys, stages[::-1])
    ycs = swapped(ycs, stages[::-1])
    for t in range(T):
        z_ref[pl.ds(t, n_lat, stride=T), :] = ys[t][n_ctx:n_ctx + n_lat, :]
        zc_ref[pl.ds(t, n_ctx, stride=T), :] = ycs[t]


def _ssm_weights(a_re, a_im, log_dt, b_re, b_im, c_re, c_im, d_skip, seg_len):
    T = SSM_T
    g, p = a_re.shape[1], a_re.shape[2]
    hc = b_re.shape[-1]
    gpt = LANES // hc
    J = g // gpt
    lam = lax.complex(a_re.astype(F32), a_im.astype(F32))
    dt = jnp.exp(log_dt.astype(F32))[..., None]
    ldt = lam * dt
    lam_bar = jnp.exp(ldt)
    bbar = ((lam_bar - 1) / lam)[..., None] * lax.complex(b_re.astype(F32), b_im.astype(F32))
    cmat = lax.complex(c_re.astype(F32), c_im.astype(F32))
    kk = jnp.arange(T + 1, dtype=F32)
    pw = jnp.exp(ldt[:, None] * kk[None, :, None, None])

    wf = pw[0, :T][::-1][:, :, None, :] * jnp.swapaxes(bbar[0], 1, 2)[None]
    wb = pw[1, :T][:, :, None, :] * jnp.swapaxes(bbar[1], 1, 2)[None]
    wq = jnp.stack([wf.real, wf.imag, wb.real, wb.imag]).reshape(4, T, J, gpt, hc, p)
    wc = wq.transpose(2, 1, 4, 0, 3, 5).reshape(J, T * hc, 4 * gpt * p)

    df = cmat[0][None] * pw[0, 1:][:, :, None, :]
    db = cmat[1][None] * pw[1, 1:][::-1][:, :, None, :]
    dq = jnp.stack([df.real, -df.imag, db.real, -db.imag]).reshape(4, T, J, gpt, hc, p)
    dc = dq.transpose(2, 0, 3, 5, 1, 4).reshape(J, 4 * gpt * p, T * hc)

    mf = jnp.einsum('gcp,kgp,gph->khgc', cmat[0], pw[0, :T], bbar[0]).real
    mb = jnp.einsum('gcp,kgp,gph->khgc', cmat[1], pw[1, :T], bbar[1]).real
    skip = jnp.eye(hc, dtype=F32)[:, None, :] * d_skip.astype(F32).reshape(g, hc)[None]
    m0 = mf[0] + mb[0] + skip
    lags = jnp.concatenate([mb[1:][::-1], m0[None], mf[1:]], axis=0)
    kc = lags.reshape(2 * T - 1, hc, J, LANES).transpose(2, 1, 0, 3).reshape(J, hc, (2 * T - 1) * LANES)

    def lam_rows(power):
        lp = jnp.exp(ldt * power)
        lt = lp.reshape(2, J, gpt * p)
        return jnp.concatenate([lt[0].real, lt[0].imag, lt[1].real, lt[1].imag], axis=-1)[:, None, :]

    lam_t = jnp.broadcast_to(lam_rows(float(T)), (J, SSM_SEG, 4 * gpt * p))
    return wc.astype(BF), kc.astype(BF), dc.astype(BF), lam_t, lam_rows(float(T * seg_len))


def _ssm_expanders(gs):
    r = np.arange(gs * LANES)
    k, slot, ch = r // LANES, (r % LANES) // SSM_GROUP, r % SSM_GROUP
    t = (slot // gs) * gs + k
    rexp = np.zeros((gs * LANES, SSM_T * SSM_GROUP), np.float32)
    rexp[r, t * SSM_GROUP + ch] = 1.0
    return jnp.asarray(rexp, BF), jnp.asarray(rexp.T, BF)


def _ssm(u3, u3c, wc, kc, dc, lam, lamseg, *, layer, batch, seq, n_ctx):
    T = SSM_T
    J = u3.shape[0]
    ns = wc.shape[-1]
    nsets = 2 ** len(SSM_STAGES)
    gs = T // nsets
    sw = gs * LANES
    n_lat, n_c = seq // T, n_ctx // T
    rows = n_lat + 2 * n_c
    seg_len = rows // SSM_SEG
    assert rows % SSM_SEG == 0 and n_c % 16 == 0
    rexp, cexp = _ssm_expanders(gs)
    kern = functools.partial(_ssm_kernel, seg_len=seg_len, n_lat=n_lat, n_ctx=n_c, stages=SSM_STAGES)
    const = lambda j, b: (0, 0)
    per_j = lambda j, b: (layer, j, 0, 0)
    per_jb = lambda j, b: (j, b, 0)
    return pl.pallas_call(
        kern,
        grid=(J, batch),
        in_specs=[pl.BlockSpec((None, seq, LANES), per_jb),
                  pl.BlockSpec((None, n_ctx, LANES), per_jb),
                  pl.BlockSpec((None, None) + wc.shape[2:], per_j),
                  pl.BlockSpec((None, None) + kc.shape[2:], per_j),
                  pl.BlockSpec((None, None) + dc.shape[2:], per_j),
                  pl.BlockSpec(rexp.shape, const),
                  pl.BlockSpec(cexp.shape, const),
                  pl.BlockSpec((None, None, SSM_SEG, ns), per_j),
                  pl.BlockSpec((None, None, 1, ns), per_j)],
        out_specs=[pl.BlockSpec((None, seq, LANES), per_jb),
                   pl.BlockSpec((None, n_ctx, LANES), per_jb)],
        out_shape=[jax.ShapeDtypeStruct(u3.shape, F32),
                   jax.ShapeDtypeStruct(u3c.shape, F32)],
        scratch_shapes=[pltpu.VMEM((nsets, rows, sw), BF),
                        pltpu.VMEM((nsets, sw, ns // nsets), BF),
                        pltpu.VMEM((nsets, sw, sw), BF),
                        pltpu.VMEM((nsets, ns // nsets, sw), BF),
                        pltpu.VMEM((ns // LANES, rows, LANES), F32),
                        pltpu.VMEM((ns // LANES, rows, LANES), F32),
                        pltpu.VMEM((ns // LANES, 2 * SSM_SEG, LANES), F32)],
        compiler_params=_cparams("arbitrary", "arbitrary"),
        name="ssm_chunk_scan",
    )(u3, u3c, wc, kc, dc, rexp, cexp, lam, lamseg)


def _dft_a_kernel(w_ref, x_ref, o_ref):
    o_ref[...] = jnp.dot(w_ref[...], x_ref[...], preferred_element_type=F32).astype(BF)


def _dft_rows(n):
    k = np.arange(n)
    m = (k[:, None] * k[None, :]) % n
    ang = m * (2.0 * math.pi / n)
    return jnp.asarray(np.concatenate([np.cos(ang), -np.sin(ang)], axis=0), BF)


def _dft_a(w, x3):
    B, n, cols = x3.shape
    tn = min(cols, 16384)
    return pl.pallas_call(
        _dft_a_kernel,
        grid=(B, cols // tn),
        in_specs=[pl.BlockSpec((2 * n, n), lambda b, i: (0, 0)),
                  pl.BlockSpec((None, n, tn), lambda b, i: (b, 0, i))],
        out_specs=pl.BlockSpec((None, 2 * n, tn), lambda b, i: (b, 0, i)),
        out_shape=jax.ShapeDtypeStruct((B, 2 * n, cols), BF),
        compiler_params=_cparams("parallel", "parallel"),
        name="dft_stage_a",
    )(w, x3)


def _dft_b_kernel(a_ref, g_ref, o_ref, *, kb, l1, wf):
    ps = []
    for i in range(kb):
        g = jnp.concatenate([g_ref[0, i], g_ref[1, i]], axis=0)
        p = jnp.dot(a_ref[i], g, preferred_element_type=F32)
        ps.append(jnp.concatenate([p[:l1], p[l1:]], axis=1))
    o_ref[...] = jnp.swapaxes(jnp.stack(ps), 0, 1).astype(BF)


def _dft_b_mats(seq, l1, l2):
    k2 = np.arange(l2)[:, None, None]
    k1 = np.arange(l1)[None, :, None]
    j1 = np.arange(l1)[None, None, :]
    m = ((k1 * l2 + k2) * j1) % seq
    ang = m * (2.0 * math.pi / seq)
    ar, ai = np.cos(ang).astype(np.float32), -np.sin(ang).astype(np.float32)
    top = np.concatenate([ar, -ai], axis=2)
    bot = np.concatenate([ai, ar], axis=2)
    return jnp.asarray(np.concatenate([top, bot], axis=1), BF)


def _dft_b(amat, g5, *, kb):
    B, _, l2, l1, wf = g5.shape
    kern = functools.partial(_dft_b_kernel, kb=kb, l1=l1, wf=wf)
    return pl.pallas_call(
        kern,
        grid=(B, l2 // kb),
        in_specs=[pl.BlockSpec((kb, 2 * l1, 2 * l1), lambda b, i: (i, 0, 0)),
                  pl.BlockSpec((None, 2, kb, l1, wf), lambda b, i: (b, 0, i, 0, 0))],
        out_specs=pl.BlockSpec((None, l1, kb, 2 * wf), lambda b, i: (b, 0, i, 0)),
        out_shape=jax.ShapeDtypeStruct((B, l1, l2, 2 * wf), BF),
        compiler_params=_cparams("parallel", "parallel"),
        name="dft_stage_b",
    )(amat, g5).reshape(B * l1 * l2, 2 * wf)


def _dft_ctx_kernel(w_ref, x_ref, o_ref, *, n, wf):
    p = jnp.dot(w_ref[...], x_ref[...], preferred_element_type=F32)
    o_ref[...] = jnp.concatenate([p[:n], p[n:]], axis=1).astype(BF)


def _dft_ctx(w, x3):
    B, n, wf = x3.shape
    kern = functools.partial(_dft_ctx_kernel, n=n, wf=wf)
    return pl.pallas_call(
        kern,
        grid=(B,),
        in_specs=[pl.BlockSpec((2 * n, n), lambda b: (0, 0)),
                  pl.BlockSpec((None, n, wf), lambda b: (b, 0, 0))],
        out_specs=pl.BlockSpec((None, n, 2 * wf), lambda b: (b, 0, 0)),
        out_shape=jax.ShapeDtypeStruct((B, n, 2 * wf), BF),
        compiler_params=_cparams("parallel"),
        name="dft_ctx",
    )(w, x3).reshape(B * n, 2 * wf)


def _channel_dft(wf, seq):
    c = np.arange(wf)
    same = (c[:, None] // FNET_GROUP) == (c[None, :] // FNET_GROUP)
    m = ((c[:, None] % FNET_GROUP) * (c[None, :] % FNET_GROUP)) % FNET_GROUP
    ang = m * (2.0 * math.pi / FNET_GROUP)
    mask = same / math.sqrt(seq * FNET_GROUP)
    return jnp.asarray(np.concatenate([np.cos(ang) * mask, np.sin(ang) * mask], axis=0), BF)


def _softmax_pv(parts, l_shape):
    m = parts[0][0].max(axis=-1, keepdims=True)
    for s, _ in parts[1:]:
        m = jnp.maximum(m, s.max(axis=-1, keepdims=True))
    l = jnp.zeros(l_shape, F32)
    o = None
    for s, v in parts:
        p = jnp.exp2(s - m)
        l = l + p.sum(axis=-1, keepdims=True)
        pv = jnp.dot(p.astype(BF), v, preferred_element_type=F32)
        o = pv if o is None else o + pv
    return o / l


def _na_kernel(kb_ref, pid_ref, dr_ref, q_ref, k_ref, v_ref, kc_ref, vc_ref, cbx_ref, o_ref, bias_ref,
               *, nkeys, heads, nsub):
    step = pl.program_id(1)
    nq = q_ref.shape[0] // nsub
    rq, kr = nq // GRID_W, nkeys // GRID_W
    lane = lax.broadcasted_iota(jnp.int32, (nq, LANES), 1)
    low = lane < HEAD_DIM
    dn = (((1,), (1,)), ((), ()))

    @pl.when((pl.program_id(0) == 0) & (step == 0))
    def _build_bias():
        for pat in range(bias_ref.shape[0]):
            for qi in range(rq):
                for w in range(kr):
                    a = dr_ref[(pat * rq + qi) * kr + w]
                    for h in range(heads):
                        bias_ref[pat, h, qi * GRID_W:(qi + 1) * GRID_W, w * GRID_W:(w + 1) * GRID_W] = cbx_ref[h, a]

    tasks = [(sb, h) for sb in range(nsub) for h in range(heads)]
    starts = [pl.multiple_of(kb_ref[step * nsub + sb] * GRID_W, GRID_W) for sb in range(nsub)]
    pats = [pid_ref[step * nsub + sb] for sb in range(nsub)]

    n_ctx = kc_ref.shape[0]
    ones = jnp.ones((n_ctx + nkeys, LANES), BF)
    kv = {}

    def keys_values(sb, hp):
        if (sb, hp) not in kv:
            sl = slice(hp * LANES, (hp + 1) * LANES)
            win = pl.ds(starts[sb], nkeys)
            k_all = jnp.concatenate([kc_ref[:, sl], k_ref[win, sl]], axis=0)
            v_all = jnp.concatenate([jnp.concatenate([vc_ref[:, sl], v_ref[win, sl]], axis=0), ones], axis=1)
            kv[sb, hp] = (k_all, v_all)
        return kv[sb, hp]

    def scores(sb, h):
        sl = slice((h // 2) * LANES, (h // 2 + 1) * LANES)
        q2 = q_ref[sb * nq:(sb + 1) * nq, sl]
        qm = jnp.where(low if h % 2 == 0 else jnp.logical_not(low), q2, jnp.zeros_like(q2))
        s = lax.dot_general(qm, keys_values(sb, h // 2)[0], dn, preferred_element_type=F32)
        return s[:, n_ctx:] + bias_ref[pats[sb], h], s[:, :n_ctx]

    def weights(s_w, s_c):
        m = jnp.maximum(s_w.max(axis=-1, keepdims=True), s_c.max(axis=-1, keepdims=True))
        return (jnp.concatenate([jnp.exp2(s_c - m).astype(BF), jnp.exp2(s_w - m).astype(BF)], axis=1),)

    outs = {}

    def finish(sb, h, p):
        sl = slice((h // 2) * LANES, (h // 2 + 1) * LANES)
        ox = jnp.dot(p, keys_values(sb, h // 2)[1], preferred_element_type=F32)
        outs[h % 2] = ox[:, :LANES] / ox[:, LANES:]
        if h % 2 == 1:
            o_ref[sb * nq:(sb + 1) * nq, sl] = jnp.where(low, outs[0], outs[1]).astype(BF)

    nt = len(tasks)
    sc = {0: scores(*tasks[0])}
    if nt > 1:
        sc[1] = scores(*tasks[1])
    pw = {0: weights(*sc.pop(0))}
    for n in range(nt):
        if n + 2 < nt:
            sc[n + 2] = scores(*tasks[n + 2])
        if n + 1 < nt:
            pw[n + 1] = weights(*sc.pop(n + 1))
        finish(*tasks[n], *pw.pop(n))


def _na_plan(rows, rq):
    wr = min(NA_ROWS, rows)
    kr = rq + wr - 1
    nblk = rows // rq
    kbs, pids, pats = [], [], []
    for blk in range(nblk):
        r0 = blk * rq
        rs = [int(np.clip(r0 + i - wr // 2, 0, rows - wr)) for i in range(rq)]
        kb = min(rs[0], rows - kr)
        pat = (tuple(r - kb for r in rs), r0 - kb)
        if pat not in pats:
            pats.append(pat)
        kbs.append(kb)
        pids.append(pats.index(pat))
    return np.asarray(kbs, np.int32), np.asarray(pids, np.int32), pats, kr, wr


def _na_bias(rpb, pats, rq, kr, wr):
    cols = np.arange(GRID_W)
    cstart = np.clip(cols - NA_COLS // 2, 0, GRID_W - NA_COLS)
    kc = np.arange(GRID_W)
    col_ok = (kc[None, :] >= cstart[:, None]) & (kc[None, :] < cstart[:, None] + NA_COLS)
    dc = kc[None, :] - cols[:, None] + NA_COLS - 1
    depth, heads, nr, ncol = rpb.shape
    sel = (dc[:, :, None] == np.arange(ncol)[None, None, :]) & col_ok[:, :, None]
    cb = jnp.einsum('lhab,ckb->lhack', rpb.astype(F32), jnp.asarray(sel, F32), precision=lax.Precision.HIGHEST)
    cb = jnp.where(jnp.asarray(col_ok)[None, None, None], cb * math.log2(math.e), NEG)
    cbx = jnp.concatenate([cb, jnp.full((depth, heads, 1, GRID_W, GRID_W), NEG, F32)], axis=2)
    slots = []
    for rs_off, r_off in pats:
        i = np.arange(rq)[:, None]
        w = np.arange(kr)[None, :]
        rso = np.asarray(rs_off)[:, None]
        row_ok = (w >= rso) & (w < rso + wr)
        slots.append(np.where(row_ok, w - r_off - i + NA_ROWS - 1, nr))
    return cbx, np.stack(slots).reshape(-1).astype(np.int32)


def _na_tables(rpb_all, seq, rq):
    kbs, pids, pats, kr, wr = _na_plan(seq // GRID_W, rq)
    cbx, slots = _na_bias(rpb_all, pats, rq, kr, wr)
    return dict(kbs=jnp.asarray(kbs), pids=jnp.asarray(pids), slots=jnp.asarray(slots), cbx=cbx, kr=kr, rq=rq,
                npat=len(pats))


def _na_latent(q, k, v, kc, vc, tabs, layer, *, batch, seq, n_ctx):
    rows = seq // GRID_W
    cbx, rq, kr = tabs["cbx"], tabs["rq"], tabs["kr"]
    heads = cbx.shape[1]
    nblk = rows // rq
    nq = rq * GRID_W
    nkeys = kr * GRID_W
    w = q.shape[1]
    nsub = 8 if nblk % 8 == 0 else (4 if nblk % 4 == 0 else 1)
    nstep = nblk // nsub
    kern = functools.partial(_na_kernel, nkeys=nkeys, heads=heads, nsub=nsub)
    grid_spec = pltpu.PrefetchScalarGridSpec(
        num_scalar_prefetch=3,
        grid=(batch, nstep),
        in_specs=[pl.BlockSpec((nsub * nq, w), lambda b, i, *_: (b * nstep + i, 0)),
                  pl.BlockSpec((seq, w), lambda b, i, *_: (b, 0)),
                  pl.BlockSpec((seq, w), lambda b, i, *_: (b, 0)),
                  pl.BlockSpec((n_ctx, w), lambda b, i, *_: (b, 0)),
                  pl.BlockSpec((n_ctx, w), lambda b, i, *_: (b, 0)),
                  pl.BlockSpec((None,) + cbx.shape[1:], lambda b, i, *_: (layer, 0, 0, 0, 0))],
        out_specs=pl.BlockSpec((nsub * nq, w), lambda b, i, *_: (b * nstep + i, 0)),
        scratch_shapes=[pltpu.VMEM((tabs["npat"], heads, nq, nkeys), F32)],
    )
    return pl.pallas_call(
        kern,
        grid_spec=grid_spec,
        out_shape=jax.ShapeDtypeStruct(q.shape, BF),
        compiler_params=_cparams("arbitrary", "arbitrary", vmem=VMEM_LIMIT_NA),
        name="na_latent",
    )(tabs["kbs"], tabs["pids"], tabs["slots"], q, k, v, kc, vc, cbx)


def _na_ctx_kernel(q_ref, k_ref, v_ref, o_ref, *, heads):
    nq = q_ref.shape[0]
    lane = lax.broadcasted_iota(jnp.int32, (nq, LANES), 1)
    low = lane < HEAD_DIM
    dn = (((1,), (1,)), ((), ()))
    for hp in range(heads // 2):
        sl = slice(hp * LANES, (hp + 1) * LANES)
        q2 = q_ref[:, sl]
        k2 = k_ref[:, sl]
        v2 = v_ref[:, sl]
        outs = []
        for hh in range(2):
            qm = jnp.where(low if hh == 0 else jnp.logical_not(low), q2, jnp.zeros_like(q2))
            s = lax.dot_general(qm, k2, dn, preferred_element_type=F32)
            outs.append(_softmax_pv([(s, v2)], (nq, 1)))
        o_ref[:, sl] = jnp.where(low, outs[0], outs[1]).astype(BF)


def _na_ctx(qc, kc, vc, *, batch, n_ctx, heads):
    w = qc.shape[1]
    spec = pl.BlockSpec((n_ctx, w), lambda b: (b, 0))
    return pl.pallas_call(
        functools.partial(_na_ctx_kernel, heads=heads),
        grid=(batch,),
        in_specs=[spec, spec, spec],
        out_specs=spec,
        out_shape=jax.ShapeDtypeStruct(qc.shape, BF),
        compiler_params=_cparams("parallel"),
        name="na_ctx",
    )(qc, kc, vc)


def _outproj_kernel(z_ref, pf_ref, na_ref, x_ref, gate_ref, gpost_ref,
                    wglu_ref, cs_ref, wfo_ref, wout_ref, o_ref, *, widths, sub):
    w_ssm, w_fn, w_na = widths
    wglu, wfo, wout = (r[...].astype(BF) for r in (wglu_ref, wfo_ref, wout_ref))
    for r0 in range(0, x_ref.shape[0], sub):
        rs = slice(r0, r0 + sub)
        z = jnp.concatenate([z_ref[j, rs, :] for j in range(w_ssm // LANES)], axis=1)
        gl = jnp.dot(z.astype(BF), wglu, preferred_element_type=F32)
        y_ssm = (z * jax.nn.sigmoid(gl)).astype(BF)
        mixed = jnp.dot(pf_ref[rs, :], cs_ref[...], preferred_element_type=F32).astype(BF)
        y_fft = jnp.dot(mixed, wfo, preferred_element_type=F32).astype(BF)
        y = jnp.concatenate([y_ssm, y_fft, na_ref[rs, :]], axis=1)
        o = jnp.dot(y, wout, preferred_element_type=F32)
        ms = jnp.mean(o * o, axis=-1, keepdims=True)
        o_ref[rs, :] = x_ref[rs, :] + gate_ref[...] * (o * lax.rsqrt(ms + EPS) * gpost_ref[...])


def _outproj(z3, pf, yna, x2d, mod3, g_post, w_glu, cs, w_fo, w_out, *,
             layer, tm, tiles_per_batch, mod_row, widths):
    n, d = x2d.shape
    w_ssm, w_fn, w_na = widths
    if mod_row is None:
        row = lambda i: i // tiles_per_batch
    else:
        row = lambda i: mod_row
    kern = functools.partial(_outproj_kernel, widths=widths, sub=min(tm, 512))
    return pl.pallas_call(
        kern,
        grid=(n // tm,),
        in_specs=[pl.BlockSpec((w_ssm // LANES, tm, LANES), lambda i: (0, i, 0)),
                  pl.BlockSpec((tm, 2 * w_fn), lambda i: (i, 0)),
                  pl.BlockSpec((tm, w_na), lambda i: (i, 0)),
                  pl.BlockSpec((tm, d), lambda i: (i, 0)),
                  pl.BlockSpec((None, 1, d), lambda i: (row(i), 0, 2)),
                  pl.BlockSpec((None, 1, d), lambda i: (layer, 0, 0)),
                  _layer_resident(w_glu, layer),
                  _resident(cs.shape, lambda i: (0, 0)),
                  _layer_resident(w_fo, layer),
                  _layer_resident(w_out, layer)],
        out_specs=pl.BlockSpec((tm, d), lambda i: (i, 0)),
        out_shape=jax.ShapeDtypeStruct((n, d), F32),
        compiler_params=_cparams("parallel"),
        name="outproj",
    )(z3, pf, yna, x2d, mod3, g_post.reshape(-1, 1, d), w_glu, cs, w_fo, w_out)


def _ffn_kernel(x_ref, sh_ref, sc_ref, gate_ref, gpre_ref, gpost_ref, wg_ref, wu_ref, wd_ref,
                o_ref, a_ref, *, sub, chunk):
    dff = wg_ref.shape[1]
    for r0 in range(0, x_ref.shape[0], sub):
        rs = slice(r0, r0 + sub)
        x = x_ref[rs, :]
        ms = jnp.mean(x * x, axis=-1, keepdims=True)
        m = ((x * lax.rsqrt(ms + EPS) * gpre_ref[...]) * (1.0 + sc_ref[...]) + sh_ref[...]).astype(BF)
        for c0 in range(0, dff, chunk):
            sl = slice(c0, min(c0 + chunk, dff))
            g = jnp.dot(m, wg_ref[:, sl], preferred_element_type=F32)
            u = jnp.dot(m, wu_ref[:, sl], preferred_element_type=F32)
            a_ref[rs, sl] = (g * jax.nn.sigmoid(g) * u).astype(BF)
        y = jnp.dot(a_ref[rs, :], wd_ref[...], preferred_element_type=F32)
        ms2 = jnp.mean(y * y, axis=-1, keepdims=True)
        o_ref[rs, :] = x + gate_ref[...] * (y * lax.rsqrt(ms2 + EPS) * gpost_ref[...])


FFN_WEIGHT_STEPS = 8


def _ffn_staged_kernel(*refs, sub, chunk, n_lat_tiles, with_ctx):
    if with_ctx:
        (x_ref, xc_ref, sh_ref, sc_ref, gate_ref, gpre_ref, gpost_ref, wg_ref, wu_ref, wd_ref,
         o_ref, oc_ref, a_ref, wg_s, wu_s, wd_s) = refs
    else:
        (x_ref, sh_ref, sc_ref, gate_ref, gpre_ref, gpost_ref, wg_ref, wu_ref, wd_ref,
         o_ref, a_ref, wg_s, wu_s, wd_s) = refs
    i = pl.program_id(0)
    common = (sh_ref, sc_ref, gate_ref, gpre_ref, gpost_ref, wg_s, wu_s, wd_s)

    @pl.when(i < FFN_WEIGHT_STEPS)
    def _cast_weights():
        for src, dst in ((wg_ref, wg_s), (wu_ref, wu_s), (wd_ref, wd_s)):
            rows = src.shape[0]
            dst[pl.ds(pl.multiple_of(i * rows, rows), rows), :] = src[...].astype(BF)

    @pl.when((i >= FFN_WEIGHT_STEPS) & (i < FFN_WEIGHT_STEPS + n_lat_tiles))
    def _latent():
        _ffn_kernel(x_ref, *common, o_ref, a_ref, sub=sub, chunk=chunk)

    if with_ctx:
        @pl.when(i >= FFN_WEIGHT_STEPS + n_lat_tiles)
        def _context():
            _ffn_kernel(xc_ref, *common, oc_ref, a_ref, sub=min(sub, xc_ref.shape[0]), chunk=chunk)


def _ffn(x2d, xc2d, mod3, g_pre, g_post, wg, wu, wd, *, layer, tm, tiles_per_batch, ctx_row):
    n, d = x2d.shape
    dff = wg.shape[2]
    nl = n // tm
    nw = FFN_WEIGHT_STEPS
    with_ctx = xc2d is not None
    gains = (g_pre.reshape(-1, 1, d), g_post.reshape(-1, 1, d))
    slab = lambda i: (layer, jnp.minimum(i, nw - 1), 0)
    weights = [pl.BlockSpec((None, d // nw, dff), slab), pl.BlockSpec((None, d // nw, dff), slab),
               pl.BlockSpec((None, dff // nw, d), slab)]
    gain_specs = [pl.BlockSpec((None, 1, d), lambda i: (layer, 0, 0))] * 2
    tile = lambda i: jnp.clip(i - nw, 0, nl - 1)
    row = lambda i: jnp.where(i < nw + nl, tile(i) // tiles_per_batch, ctx_row)
    lat = lambda i: (tile(i), 0)
    mods = [pl.BlockSpec((None, 1, d), functools.partial(lambda i, c: (row(i), 0, c), c=c)) for c in (3, 4, 5)]
    in_specs = [pl.BlockSpec((tm, d), lat)]
    out_specs = [pl.BlockSpec((tm, d), lat)]
    out_shape = [jax.ShapeDtypeStruct((n, d), F32)]
    args = [x2d]
    if with_ctx:
        nc = xc2d.shape[0]
        ctx = lambda i: (jnp.maximum(i - nw - nl, 0), 0)
        in_specs.append(pl.BlockSpec((nc, d), ctx, pipeline_mode=pl.Buffered(1)))
        out_specs.append(pl.BlockSpec((nc, d), ctx))
        out_shape.append(jax.ShapeDtypeStruct((nc, d), F32))
        args.append(xc2d)
    outs = pl.pallas_call(
        functools.partial(_ffn_staged_kernel, sub=min(tm, 512), chunk=3 * MXU_TILE, n_lat_tiles=nl,
                          with_ctx=with_ctx),
        grid=(nw + nl + int(with_ctx),),
        in_specs=in_specs + mods + gain_specs + weights,
        out_specs=out_specs,
        out_shape=out_shape,
        scratch_shapes=[pltpu.VMEM((tm, dff), BF), pltpu.VMEM((d, dff), BF), pltpu.VMEM((d, dff), BF),
                        pltpu.VMEM((dff, d), BF)],
        compiler_params=_cparams("arbitrary"),
        name="ffn_staged",
    )(*args, mod3, mod3, mod3, *gains, wg, wu, wd)
    return (outs[0], outs[1]) if with_ctx else (outs[0], None)


def _layer(x2d, xc2d, p, w, layer, *, batch, seq, n_ctx, last, rope_tabs, consts):
    d = x2d.shape[1]
    w_ssm = consts["w_ssm"]
    w_fn = w["w_fourier"].shape[1]
    w_na = (w["w_in"].shape[2] - w_ssm - w_fn) // 3
    widths = (w_ssm, w_fn, w_na)
    heads = w_na // HEAD_DIM
    tm = min(512, seq)
    tmc = min(512, batch * n_ctx)

    mod = w["mod"][layer]
    mod3 = mod.reshape(mod.shape[0], 1, 6 * d)

    cos, sin = rope_tabs
    tm2 = min(2 * tm, seq)
    (u3, f, q, k, v), (u3c, fc, qc, kc, vc) = _inproj(
        x2d, xc2d, mod3, p["g_pre_mix"], w["w_in"], cos, sin, layer=layer, tm=tm2,
        tiles_per_batch=seq // tm2, ctx_row=batch, widths=widths)

    z3, zc3 = _ssm(u3, u3c, *w["ssm"], layer=layer, batch=batch, seq=seq, n_ctx=n_ctx)

    l2 = consts["l2"]
    l1 = seq // l2
    g = _dft_a(consts["dft_a"], f.reshape(batch, l2, l1 * w_fn))
    pf = _dft_b(consts["dft_b"], g.reshape(batch, 2, l2, l1, w_fn), kb=min(16, l2))

    yna = _na_latent(q, k, v, kc, vc, consts["na"], layer, batch=batch, seq=seq, n_ctx=n_ctx)

    w_glu, w_fo, w_out = w["w_glu"], w["w_fourier"], w["w_out"]
    x2d = _outproj(z3, pf, yna, x2d, mod3, p["g_post_mix"], w_glu, consts["cs_lat"], w_fo, w_out,
                   layer=layer, tm=tm2, tiles_per_batch=seq // tm2, mod_row=None, widths=widths)
    if not last:
        pfc = _dft_ctx(consts["dft_ctx"], fc.reshape(batch, n_ctx, w_fn))
        ynac = _na_ctx(qc, kc, vc, batch=batch, n_ctx=n_ctx, heads=heads)
        tc = min(tmc, n_ctx)
        xc2d = _outproj(zc3, pfc, ynac, xc2d, mod3, p["g_post_mix"], w_glu, consts["cs_ctx"], w_fo, w_out,
                        layer=layer, tm=tc, tiles_per_batch=n_ctx // tc, mod_row=batch, widths=widths)
    else:
        xc2d = None
    wg, wu, wd = w["w_ffn_gate"], w["w_ffn_up"], w["w_ffn_down"]
    return _ffn(x2d, xc2d, mod3, p["g_pre_ffn"], p["g_post_ffn"], wg, wu, wd,
                layer=layer, tm=tm2, tiles_per_batch=seq // tm2, ctx_row=batch)


def kernel(x, c, ctx, c_ctx, w_mod, b_mod, g_pre_mix, g_post_mix, w_in, ssm_a_re, ssm_a_im, ssm_log_dt,
           ssm_b_re, ssm_b_im, ssm_c_re, ssm_c_im, ssm_d, w_glu, w_fourier, na_rpb, w_out, g_pre_ffn,
           g_post_ffn, w_ffn_gate, w_ffn_up, w_ffn_down):
    batch, seq, d = x.shape
    n_ctx = ctx.shape[1]
    depth = w_mod.shape[0]
    w_fn = w_fourier.shape[1]
    params = dict(g_pre_mix=g_pre_mix, g_post_mix=g_post_mix, g_pre_ffn=g_pre_ffn, g_post_ffn=g_post_ffn)
    seg_len = (seq + 2 * n_ctx) // SSM_T // SSM_SEG
    ssm_ops = jax.vmap(functools.partial(_ssm_weights, seg_len=seg_len))(
        ssm_a_re, ssm_a_im, ssm_log_dt, ssm_b_re, ssm_b_im, ssm_c_re, ssm_c_im, ssm_d)
    nrow = -(-(batch + 1) // 8) * 8
    cstack = jnp.concatenate([c, c_ctx[None, :], jnp.zeros((nrow - batch - 1, d), c.dtype)], axis=0)
    weights = dict(mod=_mod_rows(cstack, w_mod, b_mod), ssm=ssm_ops,
                   w_in=w_in, w_glu=w_glu, w_fourier=w_fourier, w_out=w_out,
                   w_ffn_gate=w_ffn_gate,
                   w_ffn_up=w_ffn_up, w_ffn_down=w_ffn_down)

    l2 = 64 if seq % (64 * 8) == 0 else 8
    l1 = seq // l2
    consts = dict(
        l2=l2,
        w_ssm=ssm_d.shape[1],
        na=_na_tables(na_rpb, seq, 4),
        dft_a=_dft_rows(l2).astype(BF),
        dft_b=_dft_b_mats(seq, l1, l2).astype(BF),
        dft_ctx=_dft_rows(n_ctx).astype(BF),
        cs_lat=_channel_dft(w_fn, seq).astype(BF),
        cs_ctx=_channel_dft(w_fn, n_ctx).astype(BF),
    )
    rope_tabs = _rope_tables(seq)

    x2d = x.reshape(batch * seq, d)
    xc2d = ctx.reshape(batch * n_ctx, d)
    for layer in range(depth):
        x2d, xc2d = _layer(x2d, xc2d, params, weights, layer, batch=batch, seq=seq, n_ctx=n_ctx,
                           last=(layer == depth - 1), rope_tabs=rope_tabs, consts=consts)
    return x2d.reshape(batch, seq, d)
```

```python
import functools
import math

import numpy as np
import jax
import jax.numpy as jnp
from jax import lax
from jax.experimental import pallas as pl
from jax.experimental.pallas import tpu as pltpu

BF = jnp.bfloat16
F32 = jnp.float32

EPS = 1e-6
GRID_W = 64
HEAD_DIM = 64
NA_ROWS = 8
NA_COLS = 16
ROPE_BASE = 10000.0
SSM_GROUP = 16
SSM_STATE = 64
FNET_GROUP = 64
LANES = 128
MXU_TILE = 256
SSM_T = 8
SSM_SEG = 8
SSM_STAGES = (4, 2)
NEG = -1e30
VMEM_LIMIT = 56 * 1024 * 1024
VMEM_LIMIT_NA = 60000 * 1024


def _cparams(*sem, vmem=VMEM_LIMIT):
    return pltpu.CompilerParams(dimension_semantics=sem, vmem_limit_bytes=vmem)


def _resident(shape, index_map):
    return pl.BlockSpec(shape, index_map, pipeline_mode=pl.Buffered(1))


def _layer_resident(w_all, layer):
    return pl.BlockSpec((None,) + w_all.shape[1:], lambda *_: (layer, 0, 0), pipeline_mode=pl.Buffered(1))


def _mod_kernel(c_ref, w_ref, b_ref, o_ref):
    cs = c_ref[...]
    s = (cs * jax.nn.sigmoid(cs)).astype(BF)
    o_ref[...] = jnp.dot(s, w_ref[...].astype(BF), preferred_element_type=F32) + b_ref[...]


def _mod_rows(cstack, w_mod, b_mod):
    rows, d = cstack.shape
    depth, _, n = w_mod.shape
    bn = 2 * d if n % (2 * d) == 0 else d
    return pl.pallas_call(
        _mod_kernel,
        grid=(depth, n // bn),
        in_specs=[pl.BlockSpec((rows, d), lambda l, i: (0, 0)),
                  pl.BlockSpec((None, d, bn), lambda l, i: (l, 0, i)),
                  pl.BlockSpec((None, 1, bn), lambda l, i: (l, 0, i))],
        out_specs=pl.BlockSpec((None, rows, bn), lambda l, i: (l, 0, i)),
        out_shape=jax.ShapeDtypeStruct((depth, rows, n), F32),
        compiler_params=_cparams("parallel", "parallel"),
        name="mod_rows",
    )(cstack, w_mod, b_mod.reshape(depth, 1, n))


def _rope_tile(t, cos, sin, first):
    partner = jnp.where(first, pltpu.roll(t, LANES - 16, 1), pltpu.roll(t, 16, 1))
    return t * cos + partner * sin


def _inproj_kernel(x_ref, sh_ref, sc_ref, g_ref, w_ref, cos_ref, sin_ref,
                   u_ref, f_ref, q_ref, k_ref, v_ref, *, widths, rope, sub):
    w_ssm, w_fn, w_na = widths
    scale = HEAD_DIM ** -0.5 * math.log2(math.e)
    w = w_ref[...].astype(BF)
    for r0 in range(0, x_ref.shape[0], sub):
        rs = slice(r0, r0 + sub)
        x = x_ref[rs, :]
        ms = jnp.mean(x * x, axis=-1, keepdims=True)
        xn = x * lax.rsqrt(ms + EPS) * g_ref[...]
        m = (xn * (1.0 + sc_ref[...]) + sh_ref[...]).astype(BF)

        h = jnp.dot(m, w, preferred_element_type=F32)

        def proj(lo, n, h=h):
            return h[:, lo:lo + n]

        u = proj(0, w_ssm)
        for j in range(w_ssm // LANES):
            u_ref[j, rs, :] = u[:, j * LANES:(j + 1) * LANES]
        f_ref[rs, :] = proj(w_ssm, w_fn).astype(BF)
        q = proj(w_ssm + w_fn, w_na)
        k = proj(w_ssm + w_fn + w_na, w_na)
        v_ref[rs, :] = proj(w_ssm + w_fn + 2 * w_na, w_na).astype(BF)
        if rope:
            cos = cos_ref[rs, :]
            sin = sin_ref[rs, :]
            lane = lax.broadcasted_iota(jnp.int32, cos.shape, 1)
            first = (lane % 32) < 16
            for j in range(w_na // LANES):
                sl = slice(j * LANES, (j + 1) * LANES)
                q_ref[rs, sl] = (_rope_tile(q[:, sl], cos, sin, first) * scale).astype(BF)
                k_ref[rs, sl] = _rope_tile(k[:, sl], cos, sin, first).astype(BF)
        else:
            q_ref[rs, :] = (q * scale).astype(BF)
            k_ref[rs, :] = k.astype(BF)


def _inproj_pair_kernel(x_ref, xc_ref, sh_ref, sc_ref, g_ref, w_ref, cos_ref, sin_ref, *outs,
                        widths, sub, n_lat_tiles):
    common = (sh_ref, sc_ref, g_ref, w_ref, cos_ref, sin_ref)
    i = pl.program_id(0)

    @pl.when(i < n_lat_tiles)
    def _latent():
        _inproj_kernel(x_ref, *common, *outs[:5], widths=widths, rope=True, sub=sub)

    @pl.when(i >= n_lat_tiles)
    def _context():
        _inproj_kernel(xc_ref, *common, *outs[5:], widths=widths, rope=False, sub=min(sub, xc_ref.shape[0]))


def _inproj(x2d, xc2d, mod3, g_pre, w_in, cos, sin, *, layer, tm, tiles_per_batch, ctx_row, widths):
    n, d = x2d.shape
    nc = xc2d.shape[0]
    w_ssm, w_fn, w_na = widths
    nl = n // tm
    tile = lambda i: jnp.minimum(i, nl - 1)
    row = lambda i: jnp.where(i < nl, tile(i) // tiles_per_batch, ctx_row)
    pos = lambda i: (tile(i) % tiles_per_batch, 0)
    lat = lambda i: (tile(i), 0)
    one = lambda i: (0, 0)

    def outs(rows, idx, idx3):
        specs = [pl.BlockSpec((w_ssm // LANES, rows, LANES), idx3), pl.BlockSpec((rows, w_fn), idx)]
        specs += [pl.BlockSpec((rows, w_na), idx)] * 3
        return specs

    def shapes(rows):
        return ([jax.ShapeDtypeStruct((w_ssm // LANES, rows, LANES), F32), jax.ShapeDtypeStruct((rows, w_fn), BF)]
                + [jax.ShapeDtypeStruct((rows, w_na), BF)] * 3)

    kern = functools.partial(_inproj_pair_kernel, widths=widths, sub=min(tm, 512), n_lat_tiles=nl)
    res = pl.pallas_call(
        kern,
        grid=(nl + 1,),
        in_specs=[pl.BlockSpec((tm, d), lat),
                  pl.BlockSpec((nc, d), one),
                  pl.BlockSpec((None, 1, d), lambda i: (row(i), 0, 0)),
                  pl.BlockSpec((None, 1, d), lambda i: (row(i), 0, 1)),
                  pl.BlockSpec((None, 1, d), lambda i: (layer, 0, 0)),
                  _layer_resident(w_in, layer),
                  pl.BlockSpec((tm, LANES), pos),
                  pl.BlockSpec((tm, LANES), pos)],
        out_specs=outs(tm, lat, lambda i: (0, tile(i), 0)) + outs(nc, one, lambda i: (0, 0, 0)),
        out_shape=shapes(n) + shapes(nc),
        compiler_params=_cparams("arbitrary"),
        name="inproj",
    )(x2d, xc2d, mod3, mod3, g_pre.reshape(-1, 1, d), w_in, cos, sin)
    return res[:5], res[5:]


def _rope_tables(seq):
    t = np.arange(seq)
    row = (t // GRID_W).astype(np.float32)
    col = (t % GRID_W).astype(np.float32)
    quarter = HEAD_DIM // 4
    freqs = (np.float32(ROPE_BASE) ** (-np.arange(quarter, dtype=np.float32) / quarter)).astype(np.float32)
    d = np.arange(LANES) % HEAD_DIM
    use_col = (d // (HEAD_DIM // 2)) == 1
    fidx = d % quarter
    sign = np.where((d % (HEAD_DIM // 2)) // quarter == 0, -1.0, 1.0)
    pos = np.where(use_col[None, :], col[:, None], row[:, None])
    ang = (pos * freqs[fidx][None, :]).astype(np.float32).astype(np.float64)
    return jnp.asarray(np.cos(ang), F32), jnp.asarray(np.sin(ang) * sign[None, :], F32)


def _gelu_tanh(x):
    c = math.sqrt(2.0 / math.pi)
    return x * (0.5 * (1.0 + jnp.tanh(c * (x + 0.044715 * (x * x * x)))))


def _group_of(idx, width, groups):
    shift = width.bit_length() - 1
    assert width == 1 << shift and groups & (groups - 1) == 0
    return lax.bitwise_and(lax.shift_right_logical(idx, shift), groups - 1)


def _same_group(shape, row0, row_width, col0, col_width, groups):
    r = lax.broadcasted_iota(jnp.int32, shape, 0) + row0
    c = lax.broadcasted_iota(jnp.int32, shape, 1) + col0
    return _group_of(r, row_width, groups) == _group_of(c, col_width, groups)


def _swap_stage(tiles, d):
    lane = lax.broadcasted_iota(jnp.int32, tiles[0].shape, 1)
    upper = lax.bitwise_and(lax.shift_right_logical(lane, SSM_GROUP.bit_length() - 1), d) != 0
    out = list(tiles)
    for x in range(len(tiles)):
        if x & d:
            continue
        a, b = tiles[x], tiles[x + d]
        out[x] = jnp.where(upper, pltpu.roll(b, SSM_GROUP * d, 1), a)
        out[x + d] = jnp.where(upper, b, pltpu.roll(a, LANES - SSM_GROUP * d, 1))
    return out


def _ssm_kernel(ul_ref, uc_ref, wc_ref, lag_ref, dc_ref, rexp_ref, cexp_ref, lam_ref, lamseg_ref,
                z_ref, zc_ref, u_s, bm_s, kin_s, cm_s, s_ref, h_ref, e_ref, *, seg_len, n_lat, n_ctx, stages):
    T = SSM_T
    ns = 2 ** len(stages)
    gs = T // ns
    sw = gs * LANES
    rows = u_s.shape[1]
    tps = s_ref.shape[0] // ns
    tpq = tps // 4
    nst = tpq * LANES
    gpt = LANES // SSM_GROUP
    per_q = gpt * SSM_STATE

    def lanes(c):
        return slice(c * LANES, (c + 1) * LANES)

    def col0(s, q):
        return q * per_q + s * nst

    def swapped(tiles, order):
        for d in order:
            tiles = _swap_stage(tiles, d)
        return tiles

    @pl.when(pl.program_id(1) == 0)
    def _expand_operators():
        rexp = rexp_ref[...]
        cexp = cexp_ref[...]
        for s in range(ns):
            wsel = jnp.concatenate([wc_ref[:, col0(s, q):col0(s, q) + nst] for q in range(4)], axis=1)
            blk = jnp.dot(rexp, wsel, preferred_element_type=F32)
            keep = _same_group(blk.shape, 0, SSM_GROUP, 0, SSM_STATE, gs)
            bm_s[s] = jnp.where(keep, blk, 0.0).astype(BF)
            dsel = jnp.concatenate([dc_ref[col0(s, q):col0(s, q) + nst, :] for q in range(4)], axis=0)
            blk = jnp.dot(dsel, cexp, preferred_element_type=F32)
            keep = _same_group(blk.shape, 0, SSM_STATE, 0, SSM_GROUP, gs)
            cm_s[s] = jnp.where(keep, blk, 0.0).astype(BF)
        r16 = lax.broadcasted_iota(jnp.int32, (LANES, SSM_GROUP), 0)
        c16 = lax.broadcasted_iota(jnp.int32, (LANES, SSM_GROUP), 1)
        chan = jnp.where(lax.bitwise_and(r16, SSM_GROUP - 1) == c16, 1.0, 0.0).astype(BF)
        for t in range(T):
            lo = (T - 1 - t) * LANES
            blk = jnp.dot(chan, lag_ref[:, lo:lo + T * LANES], preferred_element_type=F32)
            keep = _same_group(blk.shape, 0, SSM_GROUP, 0, SSM_GROUP, gpt)
            blk = jnp.where(keep, blk, 0.0)
            tiles = swapped([blk[:, lanes(x)] for x in range(T)], stages)
            r0 = (t % gs) * LANES + (t // gs) * gs * SSM_GROUP
            for s in range(ns):
                part = jnp.concatenate(tiles[s * gs:(s + 1) * gs], axis=1)
                kin_s[s, r0:r0 + gs * SSM_GROUP, :] = part[s * gs * SSM_GROUP:(s + 1) * gs * SSM_GROUP, :].astype(BF)

    xs = []
    for t in range(T):
        ct = uc_ref[pl.ds(t, n_ctx, stride=T), :]
        xs.append(jnp.concatenate([ct, ul_ref[pl.ds(t, n_lat, stride=T), :], ct], axis=0))
    xs = swapped(xs, stages)
    for s in range(ns):
        for k in range(gs):
            u_s[s, :, lanes(k)] = xs[s * gs + k].astype(BF)
    for s in range(ns):
        r = jnp.dot(u_s[s], bm_s[s], preferred_element_type=F32)
        for c in range(tps):
            s_ref[s * tps + c] = r[:, lanes(c)]

    chains = []
    for s in range(ns):
        for d in range(2):
            for k in range(tpq):
                cr, ci = s * tps + 2 * d * tpq + k, s * tps + (2 * d + 1) * tpq + k
                lr = slice(col0(s, 2 * d) + k * LANES, col0(s, 2 * d) + (k + 1) * LANES)
                li = slice(col0(s, 2 * d + 1) + k * LANES, col0(s, 2 * d + 1) + (k + 1) * LANES)
                chains.append((cr, ci, d, lr, li))
    nt = ns * tps

    def scan_step(i, carry, record):
        new = list(carry)
        for cr, ci, d, lr_sl, li_sl in chains:
            rsl = pl.ds(i if d == 0 else seg_len - 1 - i, SSM_SEG, stride=seg_len)
            hr, hi = carry[cr], carry[ci]
            if record:
                h_ref[cr, rsl, :] = hr
                h_ref[ci, rsl, :] = hi
            lr, li = lam_ref[:, lr_sl], lam_ref[:, li_sl]
            new[cr] = lr * hr - li * hi + s_ref[cr, rsl, :]
            new[ci] = lr * hi + li * hr + s_ref[ci, rsl, :]
        return tuple(new)

    zero = jnp.zeros((SSM_SEG, LANES), F32)
    fin = lax.fori_loop(0, seg_len, lambda i, c: scan_step(i, c, False), (zero,) * nt, unroll=4)
    for c in range(nt):
        e_ref[c, 0:SSM_SEG, :] = fin[c]

    for cr, ci, d, lr_sl, li_sl in chains:
        lr, li = lamseg_ref[:, lr_sl], lamseg_ref[:, li_sl]
        er = jnp.zeros((1, LANES), F32)
        ei = jnp.zeros((1, LANES), F32)
        for n in range(SSM_SEG):
            g = n if d == 0 else SSM_SEG - 1 - n
            e_ref[cr, SSM_SEG + g:SSM_SEG + g + 1, :] = er
            e_ref[ci, SSM_SEG + g:SSM_SEG + g + 1, :] = ei
            fr, fi = e_ref[cr, g:g + 1, :], e_ref[ci, g:g + 1, :]
            er, ei = lr * er - li * ei + fr, lr * ei + li * er + fi

    ent = tuple(e_ref[c, SSM_SEG:2 * SSM_SEG, :] for c in range(nt))
    lax.fori_loop(0, seg_len, lambda i, c: scan_step(i, c, True), ent, unroll=2)

    ys, ycs = [], []
    for s in range(ns):
        hcat = jnp.concatenate([h_ref[s * tps + c].astype(BF) for c in range(tps)], axis=1)
        y = _gelu_tanh(jnp.dot(u_s[s], kin_s[s], preferred_element_type=F32)
                       + jnp.dot(hcat, cm_s[s], preferred_element_type=F32))
        ys += [y[:, lanes(k)] for k in range(gs)]
        hc = jnp.concatenate(
            [h_ref[s * tps + c, 0:n_ctx, :].astype(BF) for c in range(2 * tpq)]
            + [h_ref[s * tps + c, rows - n_ctx:rows, :].astype(BF) for c in range(2 * tpq, tps)], axis=1)
        yc = _gelu_tanh(jnp.dot(u_s[s, 0:n_ctx, :], kin_s[s], preferred_element_type=F32)
                        + jnp.dot(hc, cm_s[s], preferred_element_type=F32))
        ycs += [yc[:, lanes(k)] for k in range(gs)]
    ys = swapped(ys, stages[::-1])
    ycs = swapped(ycs, stages[::-1])
    for t in range(T):
        z_ref[pl.ds(t, n_lat, stride=T), :] = ys[t][n_ctx:n_ctx + n_lat, :]
        zc_ref[pl.ds(t, n_ctx, stride=T), :] = ycs[t]


def _ssm_weights(a_re, a_im, log_dt, b_re, b_im, c_re, c_im, d_skip, seg_len):
    T = SSM_T
    g, p = a_re.shape[1], a_re.shape[2]
    hc = b_re.shape[-1]
    gpt = LANES // hc
    J = g // gpt
    lam = lax.complex(a_re.astype(F32), a_im.astype(F32))
    dt = jnp.exp(log_dt.astype(F32))[..., None]
    ldt = lam * dt
    lam_bar = jnp.exp(ldt)
    bbar = ((lam_bar - 1) / lam)[..., None] * lax.complex(b_re.astype(F32), b_im.astype(F32))
    cmat = lax.complex(c_re.astype(F32), c_im.astype(F32))
    kk = jnp.arange(T + 1, dtype=F32)
    pw = jnp.exp(ldt[:, None] * kk[None, :, None, None])

    wf = pw[0, :T][::-1][:, :, None, :] * jnp.swapaxes(bbar[0], 1, 2)[None]
    wb = pw[1, :T][:, :, None, :] * jnp.swapaxes(bbar[1], 1, 2)[None]
    wq = jnp.stack([wf.real, wf.imag, wb.real, wb.imag]).reshape(4, T, J, gpt, hc, p)
    wc = wq.transpose(2, 1, 4, 0, 3, 5).reshape(J, T * hc, 4 * gpt * p)

    df = cmat[0][None] * pw[0, 1:][:, :, None, :]
    db = cmat[1][None] * pw[1, 1:][::-1][:, :, None, :]
    dq = jnp.stack([df.real, -df.imag, db.real, -db.imag]).reshape(4, T, J, gpt, hc, p)
    dc = dq.transpose(2, 0, 3, 5, 1, 4).reshape(J, 4 * gpt * p, T * hc)

    mf = jnp.einsum('gcp,kgp,gph->khgc', cmat[0], pw[0, :T], bbar[0]).real
    mb = jnp.einsum('gcp,kgp,gph->khgc', cmat[1], pw[1, :T], bbar[1]).real
    skip = jnp.eye(hc, dtype=F32)[:, None, :] * d_skip.astype(F32).reshape(g, hc)[None]
    m0 = mf[0] + mb[0] + skip
    lags = jnp.concatenate([mb[1:][::-1], m0[None], mf[1:]], axis=0)
    kc = lags.reshape(2 * T - 1, hc, J, LANES).transpose(2, 1, 0, 3).reshape(J, hc, (2 * T - 1) * LANES)

    def lam_rows(power):
        lp = jnp.exp(ldt * power)
        lt = lp.reshape(2, J, gpt * p)
        return jnp.concatenate([lt[0].real, lt[0].imag, lt[1].real, lt[1].imag], axis=-1)[:, None, :]

    lam_t = jnp.broadcast_to(lam_rows(float(T)), (J, SSM_SEG, 4 * gpt * p))
    return wc.astype(BF), kc.astype(BF), dc.astype(BF), lam_t, lam_rows(float(T * seg_len))


def _ssm_expanders(gs):
    r = np.arange(gs * LANES)
    k, slot, ch = r // LANES, (r % LANES) // SSM_GROUP, r % SSM_GROUP
    t = (slot // gs) * gs + k
    rexp = np.zeros((gs * LANES, SSM_T * SSM_GROUP), np.float32)
    rexp[r, t * SSM_GROUP + ch] = 1.0
    return jnp.asarray(rexp, BF), jnp.asarray(rexp.T, BF)


def _ssm(u3, u3c, wc, kc, dc, lam, lamseg, *, layer, batch, seq, n_ctx):
    T = SSM_T
    J = u3.shape[0]
    ns = wc.shape[-1]
    nsets = 2 ** len(SSM_STAGES)
    gs = T // nsets
    sw = gs * LANES
    n_lat, n_c = seq // T, n_ctx // T
    rows = n_lat + 2 * n_c
    seg_len = rows // SSM_SEG
    assert rows % SSM_SEG == 0 and n_c % 16 == 0
    rexp, cexp = _ssm_expanders(gs)
    kern = functools.partial(_ssm_kernel, seg_len=seg_len, n_lat=n_lat, n_ctx=n_c, stages=SSM_STAGES)
    const = lambda j, b: (0, 0)
    per_j = lambda j, b: (layer, j, 0, 0)
    per_jb = lambda j, b: (j, b, 0)
    return pl.pallas_call(
        kern,
        grid=(J, batch),
        in_specs=[pl.BlockSpec((None, seq, LANES), per_jb),
                  pl.BlockSpec((None, n_ctx, LANES), per_jb),
                  pl.BlockSpec((None, None) + wc.shape[2:], per_j),
                  pl.BlockSpec((None, None) + kc.shape[2:], per_j),
                  pl.BlockSpec((None, None) + dc.shape[2:], per_j),
                  pl.BlockSpec(rexp.shape, const),
                  pl.BlockSpec(cexp.shape, const),
                  pl.BlockSpec((None, None, SSM_SEG, ns), per_j),
                  pl.BlockSpec((None, None, 1, ns), per_j)],
        out_specs=[pl.BlockSpec((None, seq, LANES), per_jb),
                   pl.BlockSpec((None, n_ctx, LANES), per_jb)],
        out_shape=[jax.ShapeDtypeStruct(u3.shape, F32),
                   jax.ShapeDtypeStruct(u3c.shape, F32)],
        scratch_shapes=[pltpu.VMEM((nsets, rows, sw), BF),
                        pltpu.VMEM((nsets, sw, ns // nsets), BF),
                        pltpu.VMEM((nsets, sw, sw), BF),
                        pltpu.VMEM((nsets, ns // nsets, sw), BF),
                        pltpu.VMEM((ns // LANES, rows, LANES), F32),
                        pltpu.VMEM((ns // LANES, rows, LANES), F32),
                        pltpu.VMEM((ns // LANES, 2 * SSM_SEG, LANES), F32)],
        compiler_params=_cparams("arbitrary", "arbitrary"),
        name="ssm_chunk_scan",
    )(u3, u3c, wc, kc, dc, rexp, cexp, lam, lamseg)


def _dft_a_kernel(w_ref, x_ref, o_ref):
    o_ref[...] = jnp.dot(w_ref[...], x_ref[...], preferred_element_type=F32).astype(BF)


def _dft_rows(n):
    k = np.arange(n)
    m = (k[:, None] * k[None, :]) % n
    ang = m * (2.0 * math.pi / n)
    return jnp.asarray(np.concatenate([np.cos(ang), -np.sin(ang)], axis=0), BF)


def _dft_a(w, x3):
    B, n, cols = x3.shape
    tn = min(cols, 16384)
    return pl.pallas_call(
        _dft_a_kernel,
        grid=(B, cols // tn),
        in_specs=[pl.BlockSpec((2 * n, n), lambda b, i: (0, 0)),
                  pl.BlockSpec((None, n, tn), lambda b, i: (b, 0, i))],
        out_specs=pl.BlockSpec((None, 2 * n, tn), lambda b, i: (b, 0, i)),
        out_shape=jax.ShapeDtypeStruct((B, 2 * n, cols), BF),
        compiler_params=_cparams("parallel", "parallel"),
        name="dft_stage_a",
    )(w, x3)


def _dft_b_kernel(a_ref, g_ref, o_ref, *, kb, l1, wf):
    ps = []
    for i in range(kb):
        g = jnp.concatenate([g_ref[0, i], g_ref[1, i]], axis=0)
        p = jnp.dot(a_ref[i], g, preferred_element_type=F32)
        ps.append(jnp.concatenate([p[:l1], p[l1:]], axis=1))
    o_ref[...] = jnp.swapaxes(jnp.stack(ps), 0, 1).astype(BF)


def _dft_b_mats(seq, l1, l2):
    k2 = np.arange(l2)[:, None, None]
    k1 = np.arange(l1)[None, :, None]
    j1 = np.arange(l1)[None, None, :]
    m = ((k1 * l2 + k2) * j1) % seq
    ang = m * (2.0 * math.pi / seq)
    ar, ai = np.cos(ang).astype(np.float32), -np.sin(ang).astype(np.float32)
    top = np.concatenate([ar, -ai], axis=2)
    bot = np.concatenate([ai, ar], axis=2)
    return jnp.asarray(np.concatenate([top, bot], axis=1), BF)


def _dft_b(amat, g5, *, kb):
    B, _, l2, l1, wf = g5.shape
    kern = functools.partial(_dft_b_kernel, kb=kb, l1=l1, wf=wf)
    return pl.pallas_call(
        kern,
        grid=(B, l2 // kb),
        in_specs=[pl.BlockSpec((kb, 2 * l1, 2 * l1), lambda b, i: (i, 0, 0)),
                  pl.BlockSpec((None, 2, kb, l1, wf), lambda b, i: (b, 0, i, 0, 0))],
        out_specs=pl.BlockSpec((None, l1, kb, 2 * wf), lambda b, i: (b, 0, i, 0)),
        out_shape=jax.ShapeDtypeStruct((B, l1, l2, 2 * wf), BF),
        compiler_params=_cparams("parallel", "parallel"),
        name="dft_stage_b",
    )(amat, g5).reshape(B * l1 * l2, 2 * wf)


def _dft_ctx_kernel(w_ref, x_ref, o_ref, *, n, wf):
    p = jnp.dot(w_ref[...], x_ref[...], preferred_element_type=F32)
    o_ref[...] = jnp.concatenate([p[:n], p[n:]], axis=1).astype(BF)


def _dft_ctx(w, x3):
    B, n, wf = x3.shape
    kern = functools.partial(_dft_ctx_kernel, n=n, wf=wf)
    return pl.pallas_call(
        kern,
        grid=(B,),
        in_specs=[pl.BlockSpec((2 * n, n), lambda b: (0, 0)),
                  pl.BlockSpec((None, n, wf), lambda b: (b, 0, 0))],
        out_specs=pl.BlockSpec((None, n, 2 * wf), lambda b: (b, 0, 0)),
        out_shape=jax.ShapeDtypeStruct((B, n, 2 * wf), BF),
        compiler_params=_cparams("parallel"),
        name="dft_ctx",
    )(w, x3).reshape(B * n, 2 * wf)


def _channel_dft(wf, seq):
    c = np.arange(wf)
    same = (c[:, None] // FNET_GROUP) == (c[None, :] // FNET_GROUP)
    m = ((c[:, None] % FNET_GROUP) * (c[None, :] % FNET_GROUP)) % FNET_GROUP
    ang = m * (2.0 * math.pi / FNET_GROUP)
    mask = same / math.sqrt(seq * FNET_GROUP)
    return jnp.asarray(np.concatenate([np.cos(ang) * mask, np.sin(ang) * mask], axis=0), BF)


def _softmax_pv(parts, l_shape):
    m = parts[0][0].max(axis=-1, keepdims=True)
    for s, _ in parts[1:]:
        m = jnp.maximum(m, s.max(axis=-1, keepdims=True))
    l = jnp.zeros(l_shape, F32)
    o = None
    for s, v in parts:
        p = jnp.exp2(s - m)
        l = l + p.sum(axis=-1, keepdims=True)
        pv = jnp.dot(p.astype(BF), v, preferred_element_type=F32)
        o = pv if o is None else o + pv
    return o / l


def _na_kernel(kb_ref, pid_ref, dr_ref, q_ref, k_ref, v_ref, kc_ref, vc_ref, cbx_ref, o_ref, bias_ref,
               *, nkeys, heads, nsub):
    step = pl.program_id(1)
    nq = q_ref.shape[0] // nsub
    rq, kr = nq // GRID_W, nkeys // GRID_W
    lane = lax.broadcasted_iota(jnp.int32, (nq, LANES), 1)
    low = lane < HEAD_DIM
    dn = (((1,), (1,)), ((), ()))

    @pl.when((pl.program_id(0) == 0) & (step == 0))
    def _build_bias():
        for pat in range(bias_ref.shape[0]):
            for qi in range(rq):
                for w in range(kr):
                    a = dr_ref[(pat * rq + qi) * kr + w]
                    for h in range(heads):
                        bias_ref[pat, h, qi * GRID_W:(qi + 1) * GRID_W, w * GRID_W:(w + 1) * GRID_W] = cbx_ref[h, a]

    tasks = [(sb, h) for sb in range(nsub) for h in range(heads)]
    starts = [pl.multiple_of(kb_ref[step * nsub + sb] * GRID_W, GRID_W) for sb in range(nsub)]
    pats = [pid_ref[step * nsub + sb] for sb in range(nsub)]

    n_ctx = kc_ref.shape[0]
    ones = jnp.ones((n_ctx + nkeys, LANES), BF)
    kv = {}

    def keys_values(sb, hp):
        if (sb, hp) not in kv:
            sl = slice(hp * LANES, (hp + 1) * LANES)
            win = pl.ds(starts[sb], nkeys)
            k_all = jnp.concatenate([kc_ref[:, sl], k_ref[win, sl]], axis=0)
            v_all = jnp.concatenate([jnp.concatenate([vc_ref[:, sl], v_ref[win, sl]], axis=0), ones], axis=1)
            kv[sb, hp] = (k_all, v_all)
        return kv[sb, hp]

    def scores(sb, h):
        sl = slice((h // 2) * LANES, (h // 2 + 1) * LANES)
        q2 = q_ref[sb * nq:(sb + 1) * nq, sl]
        qm = jnp.where(low if h % 2 == 0 else jnp.logical_not(low), q2, jnp.zeros_like(q2))
        s = lax.dot_general(qm, keys_values(sb, h // 2)[0], dn, preferred_element_type=F32)
        return s[:, n_ctx:] + bias_ref[pats[sb], h], s[:, :n_ctx]

    def weights(s_w, s_c):
        m = jnp.maximum(s_w.max(axis=-1, keepdims=True), s_c.max(axis=-1, keepdims=True))
        return (jnp.concatenate([jnp.exp2((s_c - m).astype(BF)), jnp.exp2((s_w - m).astype(BF))], axis=1),)

    outs = {}

    def finish(sb, h, p):
        sl = slice((h // 2) * LANES, (h // 2 + 1) * LANES)
        ox = jnp.dot(p, keys_values(sb, h // 2)[1], preferred_element_type=F32)
        outs[h % 2] = ox[:, :LANES] / ox[:, LANES:]
        if h % 2 == 1:
            o_ref[sb * nq:(sb + 1) * nq, sl] = jnp.where(low, outs[0], outs[1]).astype(BF)

    nt = len(tasks)
    sc = {0: scores(*tasks[0])}
    if nt > 1:
        sc[1] = scores(*tasks[1])
    pw = {0: weights(*sc.pop(0))}
    for n in range(nt):
        if n + 2 < nt:
            sc[n + 2] = scores(*tasks[n + 2])
        if n + 1 < nt:
            pw[n + 1] = weights(*sc.pop(n + 1))
        finish(*tasks[n], *pw.pop(n))


def _na_plan(rows, rq):
    wr = min(NA_ROWS, rows)
    kr = rq + wr - 1
    nblk = rows // rq
    kbs, pids, pats = [], [], []
    for blk in range(nblk):
        r0 = blk * rq
        rs = [int(np.clip(r0 + i - wr // 2, 0, rows - wr)) for i in range(rq)]
        kb = min(rs[0], rows - kr)
        pat = (tuple(r - kb for r in rs), r0 - kb)
        if pat not in pats:
            pats.append(pat)
        kbs.append(kb)
        pids.append(pats.index(pat))
    return np.asarray(kbs, np.int32), np.asarray(pids, np.int32), pats, kr, wr


def _na_bias(rpb, pats, rq, kr, wr):
    cols = np.arange(GRID_W)
    cstart = np.clip(cols - NA_COLS // 2, 0, GRID_W - NA_COLS)
    kc = np.arange(GRID_W)
    col_ok = (kc[None, :] >= cstart[:, None]) & (kc[None, :] < cstart[:, None] + NA_COLS)
    dc = kc[None, :] - cols[:, None] + NA_COLS - 1
    depth, heads, nr, ncol = rpb.shape
    sel = (dc[:, :, None] == np.arange(ncol)[None, None, :]) & col_ok[:, :, None]
    cb = jnp.einsum('lhab,ckb->lhack', rpb.astype(F32), jnp.asarray(sel, F32), precision=lax.Precision.HIGHEST)
    cb = jnp.where(jnp.asarray(col_ok)[None, None, None], cb * math.log2(math.e), NEG)
    cbx = jnp.concatenate([cb, jnp.full((depth, heads, 1, GRID_W, GRID_W), NEG, F32)], axis=2)
    slots = []
    for rs_off, r_off in pats:
        i = np.arange(rq)[:, None]
        w = np.arange(kr)[None, :]
        rso = np.asarray(rs_off)[:, None]
        row_ok = (w >= rso) & (w < rso + wr)
        slots.append(np.where(row_ok, w - r_off - i + NA_ROWS - 1, nr))
    return cbx, np.stack(slots).reshape(-1).astype(np.int32)


def _na_tables(rpb_all, seq, rq):
    kbs, pids, pats, kr, wr = _na_plan(seq // GRID_W, rq)
    cbx, slots = _na_bias(rpb_all, pats, rq, kr, wr)
    return dict(kbs=jnp.asarray(kbs), pids=jnp.asarray(pids), slots=jnp.asarray(slots), cbx=cbx, kr=kr, rq=rq,
                npat=len(pats))


def _na_latent(q, k, v, kc, vc, tabs, layer, *, batch, seq, n_ctx):
    rows = seq // GRID_W
    cbx, rq, kr = tabs["cbx"], tabs["rq"], tabs["kr"]
    heads = cbx.shape[1]
    nblk = rows // rq
    nq = rq * GRID_W
    nkeys = kr * GRID_W
    w = q.shape[1]
    nsub = 8 if nblk % 8 == 0 else (4 if nblk % 4 == 0 else 1)
    nstep = nblk // nsub
    kern = functools.partial(_na_kernel, nkeys=nkeys, heads=heads, nsub=nsub)
    grid_spec = pltpu.PrefetchScalarGridSpec(
        num_scalar_prefetch=3,
        grid=(batch, nstep),
        in_specs=[pl.BlockSpec((nsub * nq, w), lambda b, i, *_: (b * nstep + i, 0)),
                  pl.BlockSpec((seq, w), lambda b, i, *_: (b, 0)),
                  pl.BlockSpec((seq, w), lambda b, i, *_: (b, 0)),
                  pl.BlockSpec((n_ctx, w), lambda b, i, *_: (b, 0)),
                  pl.BlockSpec((n_ctx, w), lambda b, i, *_: (b, 0)),
                  pl.BlockSpec((None,) + cbx.shape[1:], lambda b, i, *_: (layer, 0, 0, 0, 0))],
        out_specs=pl.BlockSpec((nsub * nq, w), lambda b, i, *_: (b * nstep + i, 0)),
        scratch_shapes=[pltpu.VMEM((tabs["npat"], heads, nq, nkeys), F32)],
    )
    return pl.pallas_call(
        kern,
        grid_spec=grid_spec,
        out_shape=jax.ShapeDtypeStruct(q.shape, BF),
        compiler_params=_cparams("arbitrary", "arbitrary", vmem=VMEM_LIMIT_NA),
        name="na_latent",
    )(tabs["kbs"], tabs["pids"], tabs["slots"], q, k, v, kc, vc, cbx)


def _na_ctx_kernel(q_ref, k_ref, v_ref, o_ref, *, heads):
    nq = q_ref.shape[0]
    lane = lax.broadcasted_iota(jnp.int32, (nq, LANES), 1)
    low = lane < HEAD_DIM
    dn = (((1,), (1,)), ((), ()))
    for hp in range(heads // 2):
        sl = slice(hp * LANES, (hp + 1) * LANES)
        q2 = q_ref[:, sl]
        k2 = k_ref[:, sl]
        v2 = v_ref[:, sl]
        outs = []
        for hh in range(2):
            qm = jnp.where(low if hh == 0 else jnp.logical_not(low), q2, jnp.zeros_like(q2))
            s = lax.dot_general(qm, k2, dn, preferred_element_type=F32)
            outs.append(_softmax_pv([(s, v2)], (nq, 1)))
        o_ref[:, sl] = jnp.where(low, outs[0], outs[1]).astype(BF)


def _na_ctx(qc, kc, vc, *, batch, n_ctx, heads):
    w = qc.shape[1]
    spec = pl.BlockSpec((n_ctx, w), lambda b: (b, 0))
    return pl.pallas_call(
        functools.partial(_na_ctx_kernel, heads=heads),
        grid=(batch,),
        in_specs=[spec, spec, spec],
        out_specs=spec,
        out_shape=jax.ShapeDtypeStruct(qc.shape, BF),
        compiler_params=_cparams("parallel"),
        name="na_ctx",
    )(qc, kc, vc)


def _outproj_kernel(z_ref, pf_ref, na_ref, x_ref, gate_ref, gpost_ref,
                    wglu_ref, cs_ref, wfo_ref, wout_ref, o_ref, *, widths, sub):
    w_ssm, w_fn, w_na = widths
    wglu, wfo, wout = (r[...].astype(BF) for r in (wglu_ref, wfo_ref, wout_ref))
    for r0 in range(0, x_ref.shape[0], sub):
        rs = slice(r0, r0 + sub)
        z = jnp.concatenate([z_ref[j, rs, :] for j in range(w_ssm // LANES)], axis=1)
        gl = jnp.dot(z.astype(BF), wglu, preferred_element_type=F32)
        y_ssm = (z * jax.nn.sigmoid(gl)).astype(BF)
        mixed = jnp.dot(pf_ref[rs, :], cs_ref[...], preferred_element_type=F32).astype(BF)
        y_fft = jnp.dot(mixed, wfo, preferred_element_type=F32).astype(BF)
        y = jnp.concatenate([y_ssm, y_fft, na_ref[rs, :]], axis=1)
        o = jnp.dot(y, wout, preferred_element_type=F32)
        ms = jnp.mean(o * o, axis=-1, keepdims=True)
        o_ref[rs, :] = x_ref[rs, :] + gate_ref[...] * (o * lax.rsqrt(ms + EPS) * gpost_ref[...])


def _outproj(z3, pf, yna, x2d, mod3, g_post, w_glu, cs, w_fo, w_out, *,
             layer, tm, tiles_per_batch, mod_row, widths):
    n, d = x2d.shape
    w_ssm, w_fn, w_na = widths
    if mod_row is None:
        row = lambda i: i // tiles_per_batch
    else:
        row = lambda i: mod_row
    kern = functools.partial(_outproj_kernel, widths=widths, sub=min(tm, 512))
    return pl.pallas_call(
        kern,
        grid=(n // tm,),
        in_specs=[pl.BlockSpec((w_ssm // LANES, tm, LANES), lambda i: (0, i, 0)),
                  pl.BlockSpec((tm, 2 * w_fn), lambda i: (i, 0)),
                  pl.BlockSpec((tm, w_na), lambda i: (i, 0)),
                  pl.BlockSpec((tm, d), lambda i: (i, 0)),
                  pl.BlockSpec((None, 1, d), lambda i: (row(i), 0, 2)),
                  pl.BlockSpec((None, 1, d), lambda i: (layer, 0, 0)),
                  _layer_resident(w_glu, layer),
                  _resident(cs.shape, lambda i: (0, 0)),
                  _layer_resident(w_fo, layer),
                  _layer_resident(w_out, layer)],
        out_specs=pl.BlockSpec((tm, d), lambda i: (i, 0)),
        out_shape=jax.ShapeDtypeStruct((n, d), F32),
        compiler_params=_cparams("parallel"),
        name="outproj",
    )(z3, pf, yna, x2d, mod3, g_post.reshape(-1, 1, d), w_glu, cs, w_fo, w_out)


def _ffn_kernel(x_ref, sh_ref, sc_ref, gate_ref, gpre_ref, gpost_ref, wg_ref, wu_ref, wd_ref,
                o_ref, a_ref, *, sub, chunk):
    dff = wg_ref.shape[1]
    for r0 in range(0, x_ref.shape[0], sub):
        rs = slice(r0, r0 + sub)
        x = x_ref[rs, :]
        ms = jnp.mean(x * x, axis=-1, keepdims=True)
        m = ((x * lax.rsqrt(ms + EPS) * gpre_ref[...]) * (1.0 + sc_ref[...]) + sh_ref[...]).astype(BF)
        for c0 in range(0, dff, chunk):
            sl = slice(c0, min(c0 + chunk, dff))
            g = jnp.dot(m, wg_ref[:, sl], preferred_element_type=F32)
            u = jnp.dot(m, wu_ref[:, sl], preferred_element_type=F32)
            a_ref[rs, sl] = (g * jax.nn.sigmoid(g) * u).astype(BF)
        y = jnp.dot(a_ref[rs, :], wd_ref[...], preferred_element_type=F32)
        ms2 = jnp.mean(y * y, axis=-1, keepdims=True)
        o_ref[rs, :] = x + gate_ref[...] * (y * lax.rsqrt(ms2 + EPS) * gpost_ref[...])


FFN_WEIGHT_STEPS = 8


def _ffn_staged_kernel(*refs, sub, chunk, n_lat_tiles, with_ctx):
    if with_ctx:
        (x_ref, xc_ref, sh_ref, sc_ref, gate_ref, gpre_ref, gpost_ref, wg_ref, wu_ref, wd_ref,
         o_ref, oc_ref, a_ref, wg_s, wu_s, wd_s) = refs
    else:
        (x_ref, sh_ref, sc_ref, gate_ref, gpre_ref, gpost_ref, wg_ref, wu_ref, wd_ref,
         o_ref, a_ref, wg_s, wu_s, wd_s) = refs
    i = pl.program_id(0)
    common = (sh_ref, sc_ref, gate_ref, gpre_ref, gpost_ref, wg_s, wu_s, wd_s)

    @pl.when(i < FFN_WEIGHT_STEPS)
    def _cast_weights():
        for src, dst in ((wg_ref, wg_s), (wu_ref, wu_s), (wd_ref, wd_s)):
            rows = src.shape[0]
            dst[pl.ds(pl.multiple_of(i * rows, rows), rows), :] = src[...].astype(BF)

    @pl.when((i >= FFN_WEIGHT_STEPS) & (i < FFN_WEIGHT_STEPS + n_lat_tiles))
    def _latent():
        _ffn_kernel(x_ref, *common, o_ref, a_ref, sub=sub, chunk=chunk)

    if with_ctx:
        @pl.when(i >= FFN_WEIGHT_STEPS + n_lat_tiles)
        def _context():
            _ffn_kernel(xc_ref, *common, oc_ref, a_ref, sub=min(sub, xc_ref.shape[0]), chunk=chunk)


def _ffn(x2d, xc2d, mod3, g_pre, g_post, wg, wu, wd, *, layer, tm, tiles_per_batch, ctx_row):
    n, d = x2d.shape
    dff = wg.shape[2]
    nl = n // tm
    nw = FFN_WEIGHT_STEPS
    with_ctx = xc2d is not None
    gains = (g_pre.reshape(-1, 1, d), g_post.reshape(-1, 1, d))
    slab = lambda i: (layer, jnp.minimum(i, nw - 1), 0)
    weights = [pl.BlockSpec((None, d // nw, dff), slab), pl.BlockSpec((None, d // nw, dff), slab),
               pl.BlockSpec((None, dff // nw, d), slab)]
    gain_specs = [pl.BlockSpec((None, 1, d), lambda i: (layer, 0, 0))] * 2
    tile = lambda i: jnp.clip(i - nw, 0, nl - 1)
    row = lambda i: jnp.where(i < nw + nl, tile(i) // tiles_per_batch, ctx_row)
    lat = lambda i: (tile(i), 0)
    mods = [pl.BlockSpec((None, 1, d), functools.partial(lambda i, c: (row(i), 0, c), c=c)) for c in (3, 4, 5)]
    in_specs = [pl.BlockSpec((tm, d), lat)]
    out_specs = [pl.BlockSpec((tm, d), lat)]
    out_shape = [jax.ShapeDtypeStruct((n, d), F32)]
    args = [x2d]
    if with_ctx:
        nc = xc2d.shape[0]
        ctx = lambda i: (jnp.maximum(i - nw - nl, 0), 0)
        in_specs.append(pl.BlockSpec((nc, d), ctx, pipeline_mode=pl.Buffered(1)))
        out_specs.append(pl.BlockSpec((nc, d), ctx))
        out_shape.append(jax.ShapeDtypeStruct((nc, d), F32))
        args.append(xc2d)
    outs = pl.pallas_call(
        functools.partial(_ffn_staged_kernel, sub=min(tm, 512), chunk=3 * MXU_TILE, n_lat_tiles=nl,
                          with_ctx=with_ctx),
        grid=(nw + nl + int(with_ctx),),
        in_specs=in_specs + mods + gain_specs + weights,
        out_specs=out_specs,
        out_shape=out_shape,
        scratch_shapes=[pltpu.VMEM((tm, dff), BF), pltpu.VMEM((d, dff), BF), pltpu.VMEM((d, dff), BF),
                        pltpu.VMEM((dff, d), BF)],
        compiler_params=_cparams("arbitrary"),
        name="ffn_staged",
    )(*args, mod3, mod3, mod3, *gains, wg, wu, wd)
    return (outs[0], outs[1]) if with_ctx else (outs[0], None)


def _layer(x2d, xc2d, p, w, layer, *, batch, seq, n_ctx, last, rope_tabs, consts):
    d = x2d.shape[1]
    w_ssm = consts["w_ssm"]
    w_fn = w["w_fourier"].shape[1]
    w_na = (w["w_in"].shape[2] - w_ssm - w_fn) // 3
    widths = (w_ssm, w_fn, w_na)
    heads = w_na // HEAD_DIM
    tm = min(512, seq)
    tmc = min(512, batch * n_ctx)

    mod = w["mod"][layer]
    mod3 = mod.reshape(mod.shape[0], 1, 6 * d)

    cos, sin = rope_tabs
    tm2 = min(2 * tm, seq)
    (u3, f, q, k, v), (u3c, fc, qc, kc, vc) = _inproj(
        x2d, xc2d, mod3, p["g_pre_mix"], w["w_in"], cos, sin, layer=layer, tm=tm2,
        tiles_per_batch=seq // tm2, ctx_row=batch, widths=widths)

    z3, zc3 = _ssm(u3, u3c, *w["ssm"], layer=layer, batch=batch, seq=seq, n_ctx=n_ctx)

    l2 = consts["l2"]
    l1 = seq // l2
    g = _dft_a(consts["dft_a"], f.reshape(batch, l2, l1 * w_fn))
    pf = _dft_b(consts["dft_b"], g.reshape(batch, 2, l2, l1, w_fn), kb=min(16, l2))

    yna = _na_latent(q, k, v, kc, vc, consts["na"], layer, batch=batch, seq=seq, n_ctx=n_ctx)

    w_glu, w_fo, w_out = w["w_glu"], w["w_fourier"], w["w_out"]
    x2d = _outproj(z3, pf, yna, x2d, mod3, p["g_post_mix"], w_glu, consts["cs_lat"], w_fo, w_out,
                   layer=layer, tm=tm2, tiles_per_batch=seq // tm2, mod_row=None, widths=widths)
    if not last:
        pfc = _dft_ctx(consts["dft_ctx"], fc.reshape(batch, n_ctx, w_fn))
        ynac = _na_ctx(qc, kc, vc, batch=batch, n_ctx=n_ctx, heads=heads)
        tc = min(tmc, n_ctx)
        xc2d = _outproj(zc3, pfc, ynac, xc2d, mod3, p["g_post_mix"], w_glu, consts["cs_ctx"], w_fo, w_out,
                        layer=layer, tm=tc, tiles_per_batch=n_ctx // tc, mod_row=batch, widths=widths)
    else:
        xc2d = None
    wg, wu, wd = w["w_ffn_gate"], w["w_ffn_up"], w["w_ffn_down"]
    return _ffn(x2d, xc2d, mod3, p["g_pre_ffn"], p["g_post_ffn"], wg, wu, wd,
                layer=layer, tm=tm2, tiles_per_batch=seq // tm2, ctx_row=batch)


def kernel(x, c, ctx, c_ctx, w_mod, b_mod, g_pre_mix, g_post_mix, w_in, ssm_a_re, ssm_a_im, ssm_log_dt,
           ssm_b_re, ssm_b_im, ssm_c_re, ssm_c_im, ssm_d, w_glu, w_fourier, na_rpb, w_out, g_pre_ffn,
           g_post_ffn, w_ffn_gate, w_ffn_up, w_ffn_down):
    batch, seq, d = x.shape
    n_ctx = ctx.shape[1]
    depth = w_mod.shape[0]
    w_fn = w_fourier.shape[1]
    params = dict(g_pre_mix=g_pre_mix, g_post_mix=g_post_mix, g_pre_ffn=g_pre_ffn, g_post_ffn=g_post_ffn)
    seg_len = (seq + 2 * n_ctx) // SSM_T // SSM_SEG
    ssm_ops = jax.vmap(functools.partial(_ssm_weights, seg_len=seg_len))(
        ssm_a_re, ssm_a_im, ssm_log_dt, ssm_b_re, ssm_b_im, ssm_c_re, ssm_c_im, ssm_d)
    nrow = -(-(batch + 1) // 8) * 8
    cstack = jnp.concatenate([c, c_ctx[None, :], jnp.zeros((nrow - batch - 1, d), c.dtype)], axis=0)
    weights = dict(mod=_mod_rows(cstack, w_mod, b_mod), ssm=ssm_ops,
                   w_in=w_in, w_glu=w_glu, w_fourier=w_fourier, w_out=w_out,
                   w_ffn_gate=w_ffn_gate,
                   w_ffn_up=w_ffn_up, w_ffn_down=w_ffn_down)

    l2 = 64 if seq % (64 * 8) == 0 else 8
    l1 = seq // l2
    consts = dict(
        l2=l2,
        w_ssm=ssm_d.shape[1],
        na=_na_tables(na_rpb, seq, 4),
        dft_a=_dft_rows(l2).astype(BF),
        dft_b=_dft_b_mats(seq, l1, l2).astype(BF),
        dft_ctx=_dft_rows(n_ctx).astype(BF),
        cs_lat=_channel_dft(w_fn, seq).astype(BF),
        cs_ctx=_channel_dft(w_fn, n_ctx).astype(BF),
    )
    rope_tabs = _rope_tables(seq)

    x2d = x.reshape(batch * seq, d)
    xc2d = ctx.reshape(batch * n_ctx, d)
    for layer in range(depth):
        x2d, xc2d = _layer(x2d, xc2d, params, weights, layer, batch=batch, seq=seq, n_ctx=n_ctx,
                           last=(layer == depth - 1), rope_tabs=rope_tabs, consts=consts)
    return x2d.reshape(batch, seq, d)
```
